```python
import math
import jax, jax.numpy as jnp
from jax import lax
import numpy as np

D_MODEL = 1024
BATCH = 16
SEQ = 4096
DEPTH = 1

MEM_LEN = 256
EPS = 1e-6
GM_WIDTH = D_MODEL
GM_CHUNK = 128
GM_GROUP_CH = 128
GM_GROUPS = GM_WIDTH // GM_GROUP_CH
S5_WIDTH = D_MODEL // 2
S5_GROUP_CH = 16
S5_GROUPS = S5_WIDTH // S5_GROUP_CH
S5_STATE = 64
CA_HEADS = 4
CA_HEAD_DIM = D_MODEL // CA_HEADS
FFN_HIDDEN = ((8 * D_MODEL + 3 * 256 - 1) // (3 * 256)) * 256
IN_COLS = 2 * GM_WIDTH + S5_WIDTH + 2 * D_MODEL
SPLIT_POINTS = (2 * GM_WIDTH, 2 * GM_WIDTH + S5_WIDTH, 2 * GM_WIDTH + S5_WIDTH + D_MODEL)

kernel_name = 'hybrid_gmlp_s5_memxattn_block'


def rms_norm(x, g):
    xf = x.astype(jnp.float32)
    y = xf * lax.rsqrt(jnp.mean(xf * xf, axis=-1, keepdims=True) + EPS)
    return (y * g.astype(jnp.float32)).astype(x.dtype)


def layer_norm(x, g, b):
    xf = x.astype(jnp.float32)
    mu = jnp.mean(xf, axis=-1, keepdims=True)
    xc = xf - mu
    y = xc * lax.rsqrt(jnp.mean(xc * xc, axis=-1, keepdims=True) + EPS)
    return (y * g.astype(jnp.float32) + b.astype(jnp.float32)).astype(x.dtype)


def gmlp_branch(z, ln_g, ln_b, w_s, b_s):
    bsz, seq, _ = z.shape
    z = jax.nn.gelu(z)
    u, v = jnp.split(z, 2, axis=-1)
    v = layer_norm(v, ln_g, ln_b)
    v = v.reshape(bsz, seq // GM_CHUNK, GM_CHUNK, GM_GROUPS, GM_GROUP_CH)
    mask = jnp.tril(jnp.ones((GM_CHUNK, GM_CHUNK), dtype=bool))
    w = jnp.where(mask[None], w_s, jnp.zeros((), w_s.dtype))
    sv = jnp.einsum('gts,bnsgc->bntgc', w, v) + b_s.T[:, :, None]
    return u * sv.reshape(bsz, seq, GM_WIDTH)


def _ssm_combine(e1, e2):
    a1r, a1i, b1r, b1i = e1
    a2r, a2i, b2r, b2i = e2
    ar = a2r * a1r - a2i * a1i
    ai = a2r * a1i + a2i * a1r
    br = a2r * b1r - a2i * b1i + b2r
    bi = a2r * b1i + a2i * b1r + b2i
    return (ar, ai, br, bi)


def s5_branch(u, lam_re, lam_im, log_step, b_re, b_im, c_re, c_im, d, w_glu):
    f32 = jnp.float32
    dt = u.dtype
    bsz, seq, _ = u.shape
    uf = u.astype(f32).reshape(bsz, seq, S5_GROUPS, S5_GROUP_CH)
    lr = lam_re.astype(f32)
    li = lam_im.astype(f32)
    step = jnp.exp(log_step.astype(f32))[:, None]
    mag = jnp.exp(lr * step)
    ab_re = mag * jnp.cos(li * step)
    ab_im = mag * jnp.sin(li * step)
    den = lr * lr + li * li
    nr = ab_re - 1.0
    co_re = (nr * lr + ab_im * li) / den
    co_im = (ab_im * lr - nr * li) / den
    br_ = b_re.astype(f32)
    bi_ = b_im.astype(f32)
    bb_re = co_re[..., None] * br_ - co_im[..., None] * bi_
    bb_im = co_re[..., None] * bi_ + co_im[..., None] * br_
    bu_re = jnp.einsum('bsgh,gph->bsgp', uf, bb_re)
    bu_im = jnp.einsum('bsgh,gph->bsgp', uf, bb_im)
    a_re = jnp.broadcast_to(ab_re, (seq, S5_GROUPS, S5_STATE))
    a_im = jnp.broadcast_to(ab_im, (seq, S5_GROUPS, S5_STATE))

    def scan_one(br, bi):
        _, _, sr, si = lax.associative_scan(_ssm_combine, (a_re, a_im, br, bi), axis=0)
        return sr, si

    s_re, s_im = jax.vmap(scan_one)(bu_re, bu_im)
    y = (jnp.einsum('bsgp,ghp->bsgh', s_re, c_re.astype(f32))
         - jnp.einsum('bsgp,ghp->bsgh', s_im, c_im.astype(f32))
         + d.astype(f32) * uf)
    y = jax.nn.gelu(y.reshape(bsz, seq, S5_WIDTH))
    y = y * jax.nn.sigmoid(y @ w_glu.astype(f32))
    return y.astype(dt)


def cross_attention(h, mem_n, w_q, w_kv, w_o):
    bsz, seq, _ = h.shape
    q = (h @ w_q).reshape(bsz, seq, CA_HEADS, CA_HEAD_DIM)
    k, v = jnp.split(mem_n @ w_kv, 2, axis=-1)
    k = k.reshape(bsz, -1, CA_HEADS, CA_HEAD_DIM)
    v = v.reshape(bsz, -1, CA_HEADS, CA_HEAD_DIM)
    s = jnp.einsum('bshd,bmhd->bhsm', q, k).astype(jnp.float32) * (CA_HEAD_DIM ** -0.5)
    p = jax.nn.softmax(s, axis=-1).astype(h.dtype)
    o = jnp.einsum('bhsm,bmhd->bshd', p, v).reshape(bsz, seq, D_MODEL)
    return o @ w_o


def swiglu(h, w_gu, w_down):
    g, u = jnp.split(h @ w_gu, 2, axis=-1)
    return (jax.nn.silu(g) * u) @ w_down


def _fwd_setup_inputs(seed: int = 0) -> dict:
    key = jax.random.key(seed)
    ks = jax.random.split(key, 32)
    L = DEPTH
    f32 = jnp.float32

    def nrm(k, shape, scale):
        return jax.random.normal(k, shape, f32) * scale

    def gain(k, shape):
        return 1.0 + 0.01 * jax.random.normal(k, shape, f32)

    lam_re = -0.5 * jnp.exp(0.05 * jax.random.normal(ks[8], (L, S5_GROUPS, S5_STATE), f32))
    lam_im = (math.pi * jnp.arange(S5_STATE, dtype=f32))[None, None, :] + 0.01 * jax.random.normal(ks[9], (L, S5_GROUPS, S5_STATE), f32)
    log_step = jax.random.uniform(ks[10], (L, S5_GROUPS), f32, math.log(1e-3), math.log(1e-1))
    return {
        'x': jax.random.normal(ks[0], (BATCH, SEQ, D_MODEL), f32),
        'mem': jax.random.normal(ks[1], (BATCH, MEM_LEN, D_MODEL), f32),
        'g_mix_pre': gain(ks[2], (L, D_MODEL)),
        'w_in': nrm(ks[3], (L, D_MODEL, IN_COLS), D_MODEL ** -0.5),
        'gm_ln_g': gain(ks[4], (L, GM_WIDTH)),
        'gm_ln_b': nrm(ks[5], (L, GM_WIDTH), 0.01),
        'gm_w_s': nrm(ks[6], (L, GM_GROUPS, GM_CHUNK, GM_CHUNK), GM_CHUNK ** -0.5),
        'gm_b_s': gain(ks[7], (L, GM_GROUPS, GM_CHUNK)),
        's5_lam_re': lam_re,
        's5_lam_im': lam_im,
        's5_log_step': log_step,
        's5_b_re': nrm(ks[11], (L, S5_GROUPS, S5_STATE, S5_GROUP_CH), (2 * S5_GROUP_CH) ** -0.5),
        's5_b_im': nrm(ks[12], (L, S5_GROUPS, S5_STATE, S5_GROUP_CH), (2 * S5_GROUP_CH) ** -0.5),
        's5_c_re': nrm(ks[13], (L, S5_GROUPS, S5_GROUP_CH, S5_STATE), (2 * S5_STATE) ** -0.5),
        's5_c_im': nrm(ks[14], (L, S5_GROUPS, S5_GROUP_CH, S5_STATE), (2 * S5_STATE) ** -0.5),
        's5_d': nrm(ks[15], (L, S5_GROUPS, S5_GROUP_CH), 1.0),
        's5_w_glu': nrm(ks[16], (L, S5_WIDTH, S5_WIDTH), S5_WIDTH ** -0.5),
        'w_br_gm': nrm(ks[17], (L, GM_WIDTH, D_MODEL), GM_WIDTH ** -0.5),
        'w_br_s5': nrm(ks[18], (L, S5_WIDTH, D_MODEL), S5_WIDTH ** -0.5),
        'w_mix_out': nrm(ks[19], (L, D_MODEL, D_MODEL), D_MODEL ** -0.5),
        'g_mix_post': gain(ks[20], (L, D_MODEL)),
        'g_ca_pre': gain(ks[21], (L, D_MODEL)),
        'g_mem': gain(ks[22], (L, D_MODEL)),
        'ca_w_q': nrm(ks[23], (L, D_MODEL, D_MODEL), D_MODEL ** -0.5),
        'ca_w_kv': nrm(ks[24], (L, D_MODEL, 2 * D_MODEL), D_MODEL ** -0.5),
        'ca_w_o': nrm(ks[25], (L, D_MODEL, D_MODEL), D_MODEL ** -0.5),
        'g_ca_post': gain(ks[26], (L, D_MODEL)),
        'g_ffn_pre': gain(ks[27], (L, D_MODEL)),
        'ffn_w_gu': nrm(ks[28], (L, D_MODEL, 2 * FFN_HIDDEN), D_MODEL ** -0.5),
        'ffn_w_down': nrm(ks[29], (L, FFN_HIDDEN, D_MODEL), FFN_HIDDEN ** -0.5),
        'g_ffn_post': gain(ks[30], (L, D_MODEL)),
    }


def _fwd_reference(x, mem, g_mix_pre, w_in, gm_ln_g, gm_ln_b, gm_w_s, gm_b_s,
              s5_lam_re, s5_lam_im, s5_log_step, s5_b_re, s5_b_im, s5_c_re, s5_c_im,
              s5_d, s5_w_glu, w_br_gm, w_br_s5, w_mix_out, g_mix_post,
              g_ca_pre, g_mem, ca_w_q, ca_w_kv, ca_w_o, g_ca_post,
              g_ffn_pre, ffn_w_gu, ffn_w_down, g_ffn_post):
    for l in range(DEPTH):
        h = rms_norm(x, g_mix_pre[l])
        z = h @ w_in[l]
        z_gm, z_s5, z_ga, z_gb = jnp.split(z, SPLIT_POINTS, axis=-1)
        y_gm = gmlp_branch(z_gm, gm_ln_g[l], gm_ln_b[l], gm_w_s[l], gm_b_s[l])
        y_s5 = s5_branch(z_s5, s5_lam_re[l], s5_lam_im[l], s5_log_step[l],
                         s5_b_re[l], s5_b_im[l], s5_c_re[l], s5_c_im[l], s5_d[l], s5_w_glu[l])
        merged = (jax.nn.sigmoid(z_ga) * (y_gm @ w_br_gm[l])
                  + jax.nn.sigmoid(z_gb) * (y_s5 @ w_br_s5[l]))
        x = x + rms_norm(merged @ w_mix_out[l], g_mix_post[l])
        hc = rms_norm(x, g_ca_pre[l])
        mem_n = rms_norm(mem, g_mem[l])
        x = x + rms_norm(cross_attention(hc, mem_n, ca_w_q[l], ca_w_kv[l], ca_w_o[l]), g_ca_post[l])
        hf = rms_norm(x, g_ffn_pre[l])
        x = x + rms_norm(swiglu(hf, ffn_w_gu[l], ffn_w_down[l]), g_ffn_post[l])
    return x


import jax as _jax
import jax.numpy as _jnp

TWIN_FORMAT = 'train_step'
FWD_PARAMS = ['x', 'mem', 'g_mix_pre', 'w_in', 'gm_ln_g', 'gm_ln_b', 'gm_w_s', 'gm_b_s', 's5_lam_re', 's5_lam_im', 's5_log_step', 's5_b_re', 's5_b_im', 's5_c_re', 's5_c_im', 's5_d', 's5_w_glu', 'w_br_gm', 'w_br_s5', 'w_mix_out', 'g_mix_post', 'g_ca_pre', 'g_mem', 'ca_w_q', 'ca_w_kv', 'ca_w_o', 'g_ca_post', 'g_ffn_pre', 'ffn_w_gu', 'ffn_w_down', 'g_ffn_post']
TWIN_WEIGHTS = ['g_mix_pre', 'w_in', 'gm_ln_g', 'gm_ln_b', 'gm_w_s', 'gm_b_s', 's5_lam_re', 's5_lam_im', 's5_log_step', 's5_b_re', 's5_b_im', 's5_c_re', 's5_c_im', 's5_d', 's5_w_glu', 'w_br_gm', 'w_br_s5', 'w_mix_out', 'g_mix_post', 'g_ca_pre', 'g_mem', 'ca_w_q', 'ca_w_kv', 'ca_w_o', 'g_ca_post', 'g_ffn_pre', 'ffn_w_gu', 'ffn_w_down', 'g_ffn_post']
TWIN_DIFF_INPUT = 'x'
TWIN_INPUTS = ['x', 'mem', 'g_mix_pre', 'w_in', 'gm_ln_g', 'gm_ln_b', 'gm_w_s', 'gm_b_s', 's5_lam_re', 's5_lam_im', 's5_log_step', 's5_b_re', 's5_b_im', 's5_c_re', 's5_c_im', 's5_d', 's5_w_glu', 'w_br_gm', 'w_br_s5', 'w_mix_out', 'g_mix_post', 'g_ca_pre', 'g_mem', 'ca_w_q', 'ca_w_kv', 'ca_w_o', 'g_ca_post', 'g_ffn_pre', 'ffn_w_gu', 'ffn_w_down', 'g_ffn_post', 'loss_target', 'm_g_mix_pre', 'm_w_in', 'm_gm_ln_g', 'm_gm_ln_b', 'm_gm_w_s', 'm_gm_b_s', 'm_s5_lam_re', 'm_s5_lam_im', 'm_s5_log_step', 'm_s5_b_re', 'm_s5_b_im', 'm_s5_c_re', 'm_s5_c_im', 'm_s5_d', 'm_s5_w_glu', 'm_w_br_gm', 'm_w_br_s5', 'm_w_mix_out', 'm_g_mix_post', 'm_g_ca_pre', 'm_g_mem', 'm_ca_w_q', 'm_ca_w_kv', 'm_ca_w_o', 'm_g_ca_post', 'm_g_ffn_pre', 'm_ffn_w_gu', 'm_ffn_w_down', 'm_g_ffn_post', 'v_g_mix_pre', 'v_w_in', 'v_gm_ln_g', 'v_gm_ln_b', 'v_gm_w_s', 'v_gm_b_s', 'v_s5_lam_re', 'v_s5_lam_im', 'v_s5_log_step', 'v_s5_b_re', 'v_s5_b_im', 'v_s5_c_re', 'v_s5_c_im', 'v_s5_d', 'v_s5_w_glu', 'v_w_br_gm', 'v_w_br_s5', 'v_w_mix_out', 'v_g_mix_post', 'v_g_ca_pre', 'v_g_mem', 'v_ca_w_q', 'v_ca_w_kv', 'v_ca_w_o', 'v_g_ca_post', 'v_g_ffn_pre', 'v_ffn_w_gu', 'v_ffn_w_down', 'v_g_ffn_post']
TWIN_OUTPUTS = ['loss', 'grad_x', 'grad_g_mix_pre', 'grad_w_in', 'grad_gm_ln_g', 'grad_gm_ln_b', 'grad_gm_w_s', 'grad_gm_b_s', 'grad_s5_lam_re', 'grad_s5_lam_im', 'grad_s5_log_step', 'grad_s5_b_re', 'grad_s5_b_im', 'grad_s5_c_re', 'grad_s5_c_im', 'grad_s5_d', 'grad_s5_w_glu', 'grad_w_br_gm', 'grad_w_br_s5', 'grad_w_mix_out', 'grad_g_mix_post', 'grad_g_ca_pre', 'grad_g_mem', 'grad_ca_w_q', 'grad_ca_w_kv', 'grad_ca_w_o', 'grad_g_ca_post', 'grad_g_ffn_pre', 'grad_ffn_w_gu', 'grad_ffn_w_down', 'grad_g_ffn_post', 'delta_g_mix_pre', 'delta_w_in', 'delta_gm_ln_g', 'delta_gm_ln_b', 'delta_gm_w_s', 'delta_gm_b_s', 'delta_s5_lam_re', 'delta_s5_lam_im', 'delta_s5_log_step', 'delta_s5_b_re', 'delta_s5_b_im', 'delta_s5_c_re', 'delta_s5_c_im', 'delta_s5_d', 'delta_s5_w_glu', 'delta_w_br_gm', 'delta_w_br_s5', 'delta_w_mix_out', 'delta_g_mix_post', 'delta_g_ca_pre', 'delta_g_mem', 'delta_ca_w_q', 'delta_ca_w_kv', 'delta_ca_w_o', 'delta_g_ca_post', 'delta_g_ffn_pre', 'delta_ffn_w_gu', 'delta_ffn_w_down', 'delta_g_ffn_post', 'new_m_g_mix_pre', 'new_m_w_in', 'new_m_gm_ln_g', 'new_m_gm_ln_b', 'new_m_gm_w_s', 'new_m_gm_b_s', 'new_m_s5_lam_re', 'new_m_s5_lam_im', 'new_m_s5_log_step', 'new_m_s5_b_re', 'new_m_s5_b_im', 'new_m_s5_c_re', 'new_m_s5_c_im', 'new_m_s5_d', 'new_m_s5_w_glu', 'new_m_w_br_gm', 'new_m_w_br_s5', 'new_m_w_mix_out', 'new_m_g_mix_post', 'new_m_g_ca_pre', 'new_m_g_mem', 'new_m_ca_w_q', 'new_m_ca_w_kv', 'new_m_ca_w_o', 'new_m_g_ca_post', 'new_m_g_ffn_pre', 'new_m_ffn_w_gu', 'new_m_ffn_w_down', 'new_m_g_ffn_post', 'new_v_g_mix_pre', 'new_v_w_in', 'new_v_gm_ln_g', 'new_v_gm_ln_b', 'new_v_gm_w_s', 'new_v_gm_b_s', 'new_v_s5_lam_re', 'new_v_s5_lam_im', 'new_v_s5_log_step', 'new_v_s5_b_re', 'new_v_s5_b_im', 'new_v_s5_c_re', 'new_v_s5_c_im', 'new_v_s5_d', 'new_v_s5_w_glu', 'new_v_w_br_gm', 'new_v_w_br_s5', 'new_v_w_mix_out', 'new_v_g_mix_post', 'new_v_g_ca_pre', 'new_v_g_mem', 'new_v_ca_w_q', 'new_v_ca_w_kv', 'new_v_ca_w_o', 'new_v_g_ca_post', 'new_v_g_ffn_pre', 'new_v_ffn_w_gu', 'new_v_ffn_w_down', 'new_v_g_ffn_post']
TWIN_LEAF_KINDS = {'loss': 'loss', 'grad_x': 'grad_x', 'grad_g_mix_pre': 'grad_w', 'grad_w_in': 'grad_w', 'grad_gm_ln_g': 'grad_w', 'grad_gm_ln_b': 'grad_w', 'grad_gm_w_s': 'grad_w', 'grad_gm_b_s': 'grad_w', 'grad_s5_lam_re': 'grad_w', 'grad_s5_lam_im': 'grad_w', 'grad_s5_log_step': 'grad_w', 'grad_s5_b_re': 'grad_w', 'grad_s5_b_im': 'grad_w', 'grad_s5_c_re': 'grad_w', 'grad_s5_c_im': 'grad_w', 'grad_s5_d': 'grad_w', 'grad_s5_w_glu': 'grad_w', 'grad_w_br_gm': 'grad_w', 'grad_w_br_s5': 'grad_w', 'grad_w_mix_out': 'grad_w', 'grad_g_mix_post': 'grad_w', 'grad_g_ca_pre': 'grad_w', 'grad_g_mem': 'grad_w', 'grad_ca_w_q': 'grad_w', 'grad_ca_w_kv': 'grad_w', 'grad_ca_w_o': 'grad_w', 'grad_g_ca_post': 'grad_w', 'grad_g_ffn_pre': 'grad_w', 'grad_ffn_w_gu': 'grad_w', 'grad_ffn_w_down': 'grad_w', 'grad_g_ffn_post': 'grad_w', 'delta_g_mix_pre': 'delta_w', 'delta_w_in': 'delta_w', 'delta_gm_ln_g': 'delta_w', 'delta_gm_ln_b': 'delta_w', 'delta_gm_w_s': 'delta_w', 'delta_gm_b_s': 'delta_w', 'delta_s5_lam_re': 'delta_w', 'delta_s5_lam_im': 'delta_w', 'delta_s5_log_step': 'delta_w', 'delta_s5_b_re': 'delta_w', 'delta_s5_b_im': 'delta_w', 'delta_s5_c_re': 'delta_w', 'delta_s5_c_im': 'delta_w', 'delta_s5_d': 'delta_w', 'delta_s5_w_glu': 'delta_w', 'delta_w_br_gm': 'delta_w', 'delta_w_br_s5': 'delta_w', 'delta_w_mix_out': 'delta_w', 'delta_g_mix_post': 'delta_w', 'delta_g_ca_pre': 'delta_w', 'delta_g_mem': 'delta_w', 'delta_ca_w_q': 'delta_w', 'delta_ca_w_kv': 'delta_w', 'delta_ca_w_o': 'delta_w', 'delta_g_ca_post': 'delta_w', 'delta_g_ffn_pre': 'delta_w', 'delta_ffn_w_gu': 'delta_w', 'delta_ffn_w_down': 'delta_w', 'delta_g_ffn_post': 'delta_w', 'new_m_g_mix_pre': 'new_m', 'new_m_w_in': 'new_m', 'new_m_gm_ln_g': 'new_m', 'new_m_gm_ln_b': 'new_m', 'new_m_gm_w_s': 'new_m', 'new_m_gm_b_s': 'new_m', 'new_m_s5_lam_re': 'new_m', 'new_m_s5_lam_im': 'new_m', 'new_m_s5_log_step': 'new_m', 'new_m_s5_b_re': 'new_m', 'new_m_s5_b_im': 'new_m', 'new_m_s5_c_re': 'new_m', 'new_m_s5_c_im': 'new_m', 'new_m_s5_d': 'new_m', 'new_m_s5_w_glu': 'new_m', 'new_m_w_br_gm': 'new_m', 'new_m_w_br_s5': 'new_m', 'new_m_w_mix_out': 'new_m', 'new_m_g_mix_post': 'new_m', 'new_m_g_ca_pre': 'new_m', 'new_m_g_mem': 'new_m', 'new_m_ca_w_q': 'new_m', 'new_m_ca_w_kv': 'new_m', 'new_m_ca_w_o': 'new_m', 'new_m_g_ca_post': 'new_m', 'new_m_g_ffn_pre': 'new_m', 'new_m_ffn_w_gu': 'new_m', 'new_m_ffn_w_down': 'new_m', 'new_m_g_ffn_post': 'new_m', 'new_v_g_mix_pre': 'new_v', 'new_v_w_in': 'new_v', 'new_v_gm_ln_g': 'new_v', 'new_v_gm_ln_b': 'new_v', 'new_v_gm_w_s': 'new_v', 'new_v_gm_b_s': 'new_v', 'new_v_s5_lam_re': 'new_v', 'new_v_s5_lam_im': 'new_v', 'new_v_s5_log_step': 'new_v', 'new_v_s5_b_re': 'new_v', 'new_v_s5_b_im': 'new_v', 'new_v_s5_c_re': 'new_v', 'new_v_s5_c_im': 'new_v', 'new_v_s5_d': 'new_v', 'new_v_s5_w_glu': 'new_v', 'new_v_w_br_gm': 'new_v', 'new_v_w_br_s5': 'new_v', 'new_v_w_mix_out': 'new_v', 'new_v_g_mix_post': 'new_v', 'new_v_g_ca_pre': 'new_v', 'new_v_g_mem': 'new_v', 'new_v_ca_w_q': 'new_v', 'new_v_ca_w_kv': 'new_v', 'new_v_ca_w_o': 'new_v', 'new_v_g_ca_post': 'new_v', 'new_v_g_ffn_pre': 'new_v', 'new_v_ffn_w_gu': 'new_v', 'new_v_ffn_w_down': 'new_v', 'new_v_g_ffn_post': 'new_v'}


def _forward(args):
    return _fwd_reference(*[args[k] for k in FWD_PARAMS])


def _output_shape():
    out = _jax.eval_shape(lambda: _forward(_fwd_setup_inputs(0)))
    return out.shape, out.dtype

N_MICROBATCH = 1
ADAM_LR = 0.001
ADAM_B1 = 0.9
ADAM_B2 = 0.999
ADAM_EPS = 1e-08
ADAM_WD = 0.01
ADAM_STEP = 10
PER_EXAMPLE_BATCH_AXIS = {'x': 0, 'mem': 0, 'loss_target': 0}
SHARED_INPUTS = []
_WEIGHT_DTYPES = {'g_mix_pre': _jnp.float32, 'w_in': _jnp.float32, 'gm_ln_g': _jnp.float32, 'gm_ln_b': _jnp.float32, 'gm_w_s': _jnp.float32, 'gm_b_s': _jnp.float32, 's5_lam_re': _jnp.float32, 's5_lam_im': _jnp.float32, 's5_log_step': _jnp.float32, 's5_b_re': _jnp.float32, 's5_b_im': _jnp.float32, 's5_c_re': _jnp.float32, 's5_c_im': _jnp.float32, 's5_d': _jnp.float32, 's5_w_glu': _jnp.float32, 'w_br_gm': _jnp.float32, 'w_br_s5': _jnp.float32, 'w_mix_out': _jnp.float32, 'g_mix_post': _jnp.float32, 'g_ca_pre': _jnp.float32, 'g_mem': _jnp.float32, 'ca_w_q': _jnp.float32, 'ca_w_kv': _jnp.float32, 'ca_w_o': _jnp.float32, 'g_ca_post': _jnp.float32, 'g_ffn_pre': _jnp.float32, 'ffn_w_gu': _jnp.float32, 'ffn_w_down': _jnp.float32, 'g_ffn_post': _jnp.float32}
MOMENT_SCALE = {'g_mix_pre': 1.278445e+00, 'w_in': 5.738945e-01, 'gm_ln_g': 4.471767e-01, 'gm_ln_b': 4.451642e-01, 'gm_w_s': 4.081685e-01, 'gm_b_s': 6.742392e-01, 's5_lam_re': 2.908196e-02, 's5_lam_im': 3.410256e-02, 's5_log_step': 3.960345e+01, 's5_b_re': 1.893229e-02, 's5_b_im': 1.844117e-02, 's5_c_re': 3.725619e-02, 's5_c_im': 3.745932e-02, 's5_d': 5.936214e+00, 's5_w_glu': 7.582630e-01, 'w_br_gm': 5.676642e+00, 'w_br_s5': 3.649005e+00, 'w_mix_out': 6.958144e+00, 'g_mix_post': 6.532310e+01, 'g_ca_pre': 2.314061e+00, 'g_mem': 8.388954e+00, 'ca_w_q': 2.220410e+00, 'ca_w_kv': 5.749499e+00, 'ca_w_o': 7.951128e+00, 'g_ca_post': 6.575373e+01, 'g_ffn_pre': 4.339856e+00, 'ffn_w_gu': 1.873235e+00, 'ffn_w_down': 3.719318e+00, 'g_ffn_post': 6.361082e+01}


def _to_microbatches(a, axis):
    t = _jnp.moveaxis(a, axis, 0)
    t = t.reshape((N_MICROBATCH, t.shape[0] // N_MICROBATCH) + t.shape[1:])
    return _jnp.moveaxis(t, 1, axis + 1)


def setup_inputs(seed: int = 0) -> dict:
    inp = _fwd_setup_inputs(seed)
    key = _jax.random.fold_in(_jax.random.key(seed), 7919)
    shape, _ = _output_shape()
    out = dict(inp)
    out["loss_target"] = _jax.random.normal(_jax.random.fold_in(key, 0), shape, _jnp.float32)
    for i, name in enumerate(TWIN_WEIGHTS):
        w = inp[name].astype(_jnp.float32)
        if MOMENT_SCALE is None:
            s = _jnp.sqrt(_jnp.mean(_jnp.square(w)) + 1e-30)
        else:
            s = MOMENT_SCALE[name]
        km, kv = _jax.random.split(_jax.random.fold_in(key, i + 1))
        out[name] = w
        out["m_" + name] = s * _jax.random.normal(km, w.shape, _jnp.float32)
        out["v_" + name] = (s * s) * _jax.random.uniform(kv, w.shape, _jnp.float32, 0.5, 1.5)
    if N_MICROBATCH > 1:
        for name, axis in PER_EXAMPLE_BATCH_AXIS.items():
            out[name] = _to_microbatches(out[name], axis)
    return {'x': out['x'], 'mem': out['mem'], 'g_mix_pre': out['g_mix_pre'], 'w_in': out['w_in'], 'gm_ln_g': out['gm_ln_g'], 'gm_ln_b': out['gm_ln_b'], 'gm_w_s': out['gm_w_s'], 'gm_b_s': out['gm_b_s'], 's5_lam_re': out['s5_lam_re'], 's5_lam_im': out['s5_lam_im'], 's5_log_step': out['s5_log_step'], 's5_b_re': out['s5_b_re'], 's5_b_im': out['s5_b_im'], 's5_c_re': out['s5_c_re'], 's5_c_im': out['s5_c_im'], 's5_d': out['s5_d'], 's5_w_glu': out['s5_w_glu'], 'w_br_gm': out['w_br_gm'], 'w_br_s5': out['w_br_s5'], 'w_mix_out': out['w_mix_out'], 'g_mix_post': out['g_mix_post'], 'g_ca_pre': out['g_ca_pre'], 'g_mem': out['g_mem'], 'ca_w_q': out['ca_w_q'], 'ca_w_kv': out['ca_w_kv'], 'ca_w_o': out['ca_w_o'], 'g_ca_post': out['g_ca_post'], 'g_ffn_pre': out['g_ffn_pre'], 'ffn_w_gu': out['ffn_w_gu'], 'ffn_w_down': out['ffn_w_down'], 'g_ffn_post': out['g_ffn_post'], 'loss_target': out['loss_target'], 'm_g_mix_pre': out['m_g_mix_pre'], 'm_w_in': out['m_w_in'], 'm_gm_ln_g': out['m_gm_ln_g'], 'm_gm_ln_b': out['m_gm_ln_b'], 'm_gm_w_s': out['m_gm_w_s'], 'm_gm_b_s': out['m_gm_b_s'], 'm_s5_lam_re': out['m_s5_lam_re'], 'm_s5_lam_im': out['m_s5_lam_im'], 'm_s5_log_step': out['m_s5_log_step'], 'm_s5_b_re': out['m_s5_b_re'], 'm_s5_b_im': out['m_s5_b_im'], 'm_s5_c_re': out['m_s5_c_re'], 'm_s5_c_im': out['m_s5_c_im'], 'm_s5_d': out['m_s5_d'], 'm_s5_w_glu': out['m_s5_w_glu'], 'm_w_br_gm': out['m_w_br_gm'], 'm_w_br_s5': out['m_w_br_s5'], 'm_w_mix_out': out['m_w_mix_out'], 'm_g_mix_post': out['m_g_mix_post'], 'm_g_ca_pre': out['m_g_ca_pre'], 'm_g_mem': out['m_g_mem'], 'm_ca_w_q': out['m_ca_w_q'], 'm_ca_w_kv': out['m_ca_w_kv'], 'm_ca_w_o': out['m_ca_w_o'], 'm_g_ca_post': out['m_g_ca_post'], 'm_g_ffn_pre': out['m_g_ffn_pre'], 'm_ffn_w_gu': out['m_ffn_w_gu'], 'm_ffn_w_down': out['m_ffn_w_down'], 'm_g_ffn_post': out['m_g_ffn_post'], 'v_g_mix_pre': out['v_g_mix_pre'], 'v_w_in': out['v_w_in'], 'v_gm_ln_g': out['v_gm_ln_g'], 'v_gm_ln_b': out['v_gm_ln_b'], 'v_gm_w_s': out['v_gm_w_s'], 'v_gm_b_s': out['v_gm_b_s'], 'v_s5_lam_re': out['v_s5_lam_re'], 'v_s5_lam_im': out['v_s5_lam_im'], 'v_s5_log_step': out['v_s5_log_step'], 'v_s5_b_re': out['v_s5_b_re'], 'v_s5_b_im': out['v_s5_b_im'], 'v_s5_c_re': out['v_s5_c_re'], 'v_s5_c_im': out['v_s5_c_im'], 'v_s5_d': out['v_s5_d'], 'v_s5_w_glu': out['v_s5_w_glu'], 'v_w_br_gm': out['v_w_br_gm'], 'v_w_br_s5': out['v_w_br_s5'], 'v_w_mix_out': out['v_w_mix_out'], 'v_g_mix_post': out['v_g_mix_post'], 'v_g_ca_pre': out['v_g_ca_pre'], 'v_g_mem': out['v_g_mem'], 'v_ca_w_q': out['v_ca_w_q'], 'v_ca_w_kv': out['v_ca_w_kv'], 'v_ca_w_o': out['v_ca_w_o'], 'v_g_ca_post': out['v_g_ca_post'], 'v_g_ffn_pre': out['v_g_ffn_pre'], 'v_ffn_w_gu': out['v_ffn_w_gu'], 'v_ffn_w_down': out['v_ffn_w_down'], 'v_g_ffn_post': out['v_g_ffn_post']}


def _loss(weights, diff, rest, loss_target):
    with _jax.named_scope("forward"):
        args = {**rest, TWIN_DIFF_INPUT: diff, **{k: w.astype(_WEIGHT_DTYPES[k]) for k, w in weights.items()}}
        y = _forward(args)
    with _jax.named_scope("loss_head"):
        err = _jnp.square(y.astype(_jnp.float32) - loss_target)
        return 0.5 * _jnp.sum(_jnp.mean(err, axis=-1)) if err.ndim else 0.5 * err


def _adamw(w, g, m, v):
    m = ADAM_B1 * m + (1.0 - ADAM_B1) * g
    v = ADAM_B2 * v + (1.0 - ADAM_B2) * _jnp.square(g)
    m_hat = m / (1.0 - ADAM_B1 ** ADAM_STEP)
    v_hat = v / (1.0 - ADAM_B2 ** ADAM_STEP)
    delta = -ADAM_LR * (m_hat / (_jnp.sqrt(v_hat) + ADAM_EPS) + ADAM_WD * w)
    return delta, m, v


def reference(x, mem, g_mix_pre, w_in, gm_ln_g, gm_ln_b, gm_w_s, gm_b_s, s5_lam_re, s5_lam_im, s5_log_step, s5_b_re, s5_b_im, s5_c_re, s5_c_im, s5_d, s5_w_glu, w_br_gm, w_br_s5, w_mix_out, g_mix_post, g_ca_pre, g_mem, ca_w_q, ca_w_kv, ca_w_o, g_ca_post, g_ffn_pre, ffn_w_gu, ffn_w_down, g_ffn_post, loss_target, m_g_mix_pre, m_w_in, m_gm_ln_g, m_gm_ln_b, m_gm_w_s, m_gm_b_s, m_s5_lam_re, m_s5_lam_im, m_s5_log_step, m_s5_b_re, m_s5_b_im, m_s5_c_re, m_s5_c_im, m_s5_d, m_s5_w_glu, m_w_br_gm, m_w_br_s5, m_w_mix_out, m_g_mix_post, m_g_ca_pre, m_g_mem, m_ca_w_q, m_ca_w_kv, m_ca_w_o, m_g_ca_post, m_g_ffn_pre, m_ffn_w_gu, m_ffn_w_down, m_g_ffn_post, v_g_mix_pre, v_w_in, v_gm_ln_g, v_gm_ln_b, v_gm_w_s, v_gm_b_s, v_s5_lam_re, v_s5_lam_im, v_s5_log_step, v_s5_b_re, v_s5_b_im, v_s5_c_re, v_s5_c_im, v_s5_d, v_s5_w_glu, v_w_br_gm, v_w_br_s5, v_w_mix_out, v_g_mix_post, v_g_ca_pre, v_g_mem, v_ca_w_q, v_ca_w_kv, v_ca_w_o, v_g_ca_post, v_g_ffn_pre, v_ffn_w_gu, v_ffn_w_down, v_g_ffn_post):
    given = dict(x=x, mem=mem, g_mix_pre=g_mix_pre, w_in=w_in, gm_ln_g=gm_ln_g, gm_ln_b=gm_ln_b, gm_w_s=gm_w_s, gm_b_s=gm_b_s, s5_lam_re=s5_lam_re, s5_lam_im=s5_lam_im, s5_log_step=s5_log_step, s5_b_re=s5_b_re, s5_b_im=s5_b_im, s5_c_re=s5_c_re, s5_c_im=s5_c_im, s5_d=s5_d, s5_w_glu=s5_w_glu, w_br_gm=w_br_gm, w_br_s5=w_br_s5, w_mix_out=w_mix_out, g_mix_post=g_mix_post, g_ca_pre=g_ca_pre, g_mem=g_mem, ca_w_q=ca_w_q, ca_w_kv=ca_w_kv, ca_w_o=ca_w_o, g_ca_post=g_ca_post, g_ffn_pre=g_ffn_pre, ffn_w_gu=ffn_w_gu, ffn_w_down=ffn_w_down, g_ffn_post=g_ffn_post, loss_target=loss_target, m_g_mix_pre=m_g_mix_pre, m_w_in=m_w_in, m_gm_ln_g=m_gm_ln_g, m_gm_ln_b=m_gm_ln_b, m_gm_w_s=m_gm_w_s, m_gm_b_s=m_gm_b_s, m_s5_lam_re=m_s5_lam_re, m_s5_lam_im=m_s5_lam_im, m_s5_log_step=m_s5_log_step, m_s5_b_re=m_s5_b_re, m_s5_b_im=m_s5_b_im, m_s5_c_re=m_s5_c_re, m_s5_c_im=m_s5_c_im, m_s5_d=m_s5_d, m_s5_w_glu=m_s5_w_glu, m_w_br_gm=m_w_br_gm, m_w_br_s5=m_w_br_s5, m_w_mix_out=m_w_mix_out, m_g_mix_post=m_g_mix_post, m_g_ca_pre=m_g_ca_pre, m_g_mem=m_g_mem, m_ca_w_q=m_ca_w_q, m_ca_w_kv=m_ca_w_kv, m_ca_w_o=m_ca_w_o, m_g_ca_post=m_g_ca_post, m_g_ffn_pre=m_g_ffn_pre, m_ffn_w_gu=m_ffn_w_gu, m_ffn_w_down=m_ffn_w_down, m_g_ffn_post=m_g_ffn_post, v_g_mix_pre=v_g_mix_pre, v_w_in=v_w_in, v_gm_ln_g=v_gm_ln_g, v_gm_ln_b=v_gm_ln_b, v_gm_w_s=v_gm_w_s, v_gm_b_s=v_gm_b_s, v_s5_lam_re=v_s5_lam_re, v_s5_lam_im=v_s5_lam_im, v_s5_log_step=v_s5_log_step, v_s5_b_re=v_s5_b_re, v_s5_b_im=v_s5_b_im, v_s5_c_re=v_s5_c_re, v_s5_c_im=v_s5_c_im, v_s5_d=v_s5_d, v_s5_w_glu=v_s5_w_glu, v_w_br_gm=v_w_br_gm, v_w_br_s5=v_w_br_s5, v_w_mix_out=v_w_mix_out, v_g_mix_post=v_g_mix_post, v_g_ca_pre=v_g_ca_pre, v_g_mem=v_g_mem, v_ca_w_q=v_ca_w_q, v_ca_w_kv=v_ca_w_kv, v_ca_w_o=v_ca_w_o, v_g_ca_post=v_g_ca_post, v_g_ffn_pre=v_g_ffn_pre, v_ffn_w_gu=v_ffn_w_gu, v_ffn_w_down=v_ffn_w_down, v_g_ffn_post=v_g_ffn_post)
    weights = {n: given[n] for n in TWIN_WEIGHTS}
    shared = {n: given[n] for n in SHARED_INPUTS}
    per_example = {n: given[n] for n in ['x', 'mem']}
    grad_fn = _jax.value_and_grad(_loss, argnums=(0, 1))

    def one_microbatch(ex, loss_target):
        ex = dict(ex)
        diff = ex.pop(TWIN_DIFF_INPUT)
        return grad_fn(weights, diff, {**shared, **ex}, loss_target)

    if N_MICROBATCH == 1:
        loss, (grad_w, grad_x) = one_microbatch(per_example, given["loss_target"])
    else:
        def body(carry, xs):
            loss_sum, grad_sum = carry
            l_k, (gw_k, gx_k) = one_microbatch(xs[0], xs[1])
            with _jax.named_scope("update"):
                return (loss_sum + l_k, _jax.tree.map(_jnp.add, grad_sum, gw_k)), gx_k

        init = (_jnp.zeros((), _jnp.float32), _jax.tree.map(_jnp.zeros_like, weights))
        (loss, grad_w), grad_x = _jax.lax.scan(body, init, (per_example, given["loss_target"]))
    with _jax.named_scope("update"):
        delta_w, new_m, new_v = {}, {}, {}
        for n in TWIN_WEIGHTS:
            delta_w[n], new_m[n], new_v[n] = _adamw(weights[n], grad_w[n], given["m_" + n], given["v_" + n])
    return (loss, grad_x, *[grad_w[n] for n in TWIN_WEIGHTS], *[delta_w[n] for n in TWIN_WEIGHTS],
            *[new_m[n] for n in TWIN_WEIGHTS], *[new_v[n] for n in TWIN_WEIGHTS])
```

```python
import functools
import math

import jax
import jax.numpy as jnp
from jax import lax
from jax.experimental import pallas as pl
from jax.experimental.pallas import tpu as pltpu

F32 = jnp.float32
BF16 = jnp.bfloat16
EPS = 1e-6
N_DEV = 8
V7X_VMEM_LIMIT = 56 * 1024 * 1024
ROW_TILE = 256
GM_CHUNK = 128
GM_GROUPS = 8
S5_GROUPS = 32
S5_STATE = 64
S5_CH = 16
SCAN_CHUNKS = 32
SCAN_LANES = 128
CA_HEADS = 4
ADAM_LR, ADAM_B1, ADAM_B2, ADAM_EPS, ADAM_WD, ADAM_STEP = 0.001, 0.9, 0.999, 1e-08, 0.01, 10

WEIGHTS = ['g_mix_pre', 'w_in', 'gm_ln_g', 'gm_ln_b', 'gm_w_s', 'gm_b_s', 's5_lam_re', 's5_lam_im', 's5_log_step',
           's5_b_re', 's5_b_im', 's5_c_re', 's5_c_im', 's5_d', 's5_w_glu', 'w_br_gm', 'w_br_s5', 'w_mix_out',
           'g_mix_post', 'g_ca_pre', 'g_mem', 'ca_w_q', 'ca_w_kv', 'ca_w_o', 'g_ca_post', 'g_ffn_pre', 'ffn_w_gu',
           'ffn_w_down', 'g_ffn_post']
SHARDED = {'w_in': 'col', 's5_w_glu': 'row', 'w_br_gm': 'row', 'w_br_s5': 'col', 'w_mix_out': 'row',
           'ca_w_q': 'row', 'ca_w_kv': 'col', 'ca_w_o': 'row', 'ffn_w_gu': 'col', 'ffn_w_down': 'row'}
SMALL = [n for n in WEIGHTS if n not in SHARDED]


def _rms(x, g):
    r = lax.rsqrt(jnp.mean(x * x, axis=-1, keepdims=True) + EPS)
    return (x * r) * g


def _rms_bwd(x, g, dy):
    r = lax.rsqrt(jnp.mean(x * x, axis=-1, keepdims=True) + EPS)
    n = x * r
    dn = dy * g
    dx = r * (dn - n * jnp.mean(dn * n, axis=-1, keepdims=True))
    return dx, jnp.sum(dy * n, axis=0, keepdims=True)


_GELU_C = math.sqrt(2.0 / math.pi)


def _gelu(x):
    return 0.5 * x * (1.0 + jnp.tanh(_GELU_C * (x + 0.044715 * (x * x * x))))


def _gelu_grad(x):
    t = jnp.tanh(_GELU_C * (x + 0.044715 * (x * x * x)))
    return 0.5 * (1.0 + t) + 0.5 * x * (1.0 - t * t) * (_GELU_C * (1.0 + 3.0 * 0.044715 * (x * x)))


def _sigmoid(x):
    return 1.0 / (1.0 + jnp.exp(-x))


def _dot(a, b):
    return jnp.dot(a, b, preferred_element_type=F32)


def _dot_nt(a, b):
    return lax.dot_general(a, b, (((1,), (1,)), ((), ())), preferred_element_type=F32)


def _dot_tn(a, b):
    return lax.dot_general(a, b, (((0,), (0,)), ((), ())), preferred_element_type=F32)


def _adamw(w, g, m, v):
    m = ADAM_B1 * m + (1.0 - ADAM_B1) * g
    v = ADAM_B2 * v + (1.0 - ADAM_B2) * (g * g)
    m_hat = m / (1.0 - ADAM_B1 ** ADAM_STEP)
    v_hat = v / (1.0 - ADAM_B2 ** ADAM_STEP)
    delta = -ADAM_LR * (m_hat / (jnp.sqrt(v_hat) + ADAM_EPS) + ADAM_WD * w)
    return delta, m, v


def _params(n_grid):
    return pltpu.CompilerParams(dimension_semantics=("arbitrary",) * n_grid, vmem_limit_bytes=V7X_VMEM_LIMIT)


def _rowcall(name, body, n_rows, tm, ins, outs, accs=(), scratch=()):
    arrays, in_specs = [], []
    for a, kind in ins:
        arrays.append(a)
        if kind == 'row':
            in_specs.append(pl.BlockSpec((tm,) + a.shape[1:], lambda i, nd=a.ndim: (i,) + (0,) * (nd - 1)))
        elif kind == 'full':
            in_specs.append(pl.BlockSpec(a.shape, lambda i, nd=a.ndim: (0,) * nd))
        else:
            in_specs.append(kind)
    out_shape, out_specs = [], []
    for cols, dt in outs:
        out_shape.append(jax.ShapeDtypeStruct((n_rows, cols), dt))
        out_specs.append(pl.BlockSpec((tm, cols), lambda i: (i, 0)))
    for shp, dt in accs:
        if isinstance(dt, tuple):
            dt, spec = dt
        else:
            spec = pl.BlockSpec(shp, lambda i, nd=len(shp): (0,) * nd)
        out_shape.append(jax.ShapeDtypeStruct(shp, dt))
        out_specs.append(spec)
    return pl.pallas_call(functools.partial(body), grid=(n_rows // tm,), in_specs=in_specs, out_specs=out_specs,
                          out_shape=out_shape, scratch_shapes=list(scratch), name=name, compiler_params=_params(1))(*arrays)


def _dwcall(name, a, dc, out_dtype, tn=512, tm=512):
    n, ka = a.shape
    nn = dc.shape[1]
    tm = min(tm, n)
    tn = min(tn, nn)
    n_i = n // tm

    def body(a_ref, dc_ref, o_ref, acc_ref):
        i = pl.program_id(1)

        @pl.when(i == 0)
        def _():
            acc_ref[...] = jnp.zeros_like(acc_ref)

        acc_ref[...] += _dot_tn(a_ref[...].astype(BF16), dc_ref[...].astype(BF16))

        @pl.when(i == n_i - 1)
        def _():
            o_ref[...] = acc_ref[...].astype(o_ref.dtype)

    return pl.pallas_call(
        body, grid=(nn // tn, n_i),
        in_specs=[pl.BlockSpec((tm, ka), lambda j, i: (i, 0)), pl.BlockSpec((tm, tn), lambda j, i: (i, j))],
        out_specs=pl.BlockSpec((ka, tn), lambda j, i: (0, j)),
        out_shape=jax.ShapeDtypeStruct((ka, nn), out_dtype),
        scratch_shapes=[pltpu.VMEM((ka, tn), F32)], name=name, compiler_params=_params(2))(a, dc)


def _my_place():
    x, y, c = lax.axis_index("x"), lax.axis_index("y"), lax.axis_index("c")
    return x, y, c


def _peer(x, y, c, k):
    px = 1 - x if k & 4 else x
    py = 1 - y if k & 2 else y
    pc = 1 - c if k & 1 else c
    return (px, py, pc), 4 * px + 2 * py + pc


def _allgather(shards):
    n = len(shards)
    hbm = pl.BlockSpec(memory_space=pl.ANY)

    def body(*refs):
        ins, outs, stage = refs[:n], refs[n:2 * n], refs[2 * n:3 * n]
        send_sems, recv_sems, own_sems = refs[3 * n:]
        x, y, c = _my_place()
        me = 4 * x + 2 * y + c
        started = []
        for w in range(n):
            stage[w][...] = ins[w][...].astype(BF16)
        for w in range(n):
            own = pltpu.make_async_copy(stage[w], outs[w].at[me], own_sems.at[w])
            own.start()
            started.append(own)
        sends = []
        for w in range(n):
            for k in range(1, N_DEV):
                peer, _ = _peer(x, y, c, k)
                cp = pltpu.make_async_remote_copy(
                    src_ref=stage[w], dst_ref=outs[w].at[me], send_sem=send_sems.at[w * 7 + k - 1],
                    recv_sem=recv_sems.at[w * 7 + k - 1], device_id=peer, device_id_type=pl.DeviceIdType.MESH)
                cp.start()
                sends.append(cp)
        for w in range(n):
            for k in range(1, N_DEV):
                peer, peer_block = _peer(x, y, c, k)
                pltpu.make_async_remote_copy(
                    src_ref=stage[w], dst_ref=outs[w].at[peer_block], send_sem=send_sems.at[w * 7 + k - 1],
                    recv_sem=recv_sems.at[w * 7 + k - 1], device_id=peer,
                    device_id_type=pl.DeviceIdType.MESH).wait_recv()
        for cp in sends:
            cp.wait_send()
        for own in started:
            own.wait()

    return pl.pallas_call(
        body, out_shape=[jax.ShapeDtypeStruct((N_DEV,) + s.shape, BF16) for s in shards],
        in_specs=[pl.BlockSpec(memory_space=pltpu.VMEM)] * n, out_specs=[hbm] * n,
        scratch_shapes=[pltpu.VMEM(s.shape, BF16) for s in shards]
        + [pltpu.SemaphoreType.DMA((7 * n,)), pltpu.SemaphoreType.DMA((7 * n,)), pltpu.SemaphoreType.DMA((n,))],
        name="allgather_weights", compiler_params=pltpu.CompilerParams(vmem_limit_bytes=V7X_VMEM_LIMIT))(*shards)


def _exchange_and_sum(parts_ref, recv_ref, send_sems, recv_sems, own_sem):
    x, y, c = _my_place()
    me = 4 * x + 2 * y + c
    own = pltpu.make_async_copy(parts_ref.at[me], recv_ref.at[me], own_sem)
    own.start()
    sends = []
    for k in range(1, N_DEV):
        peer, peer_block = _peer(x, y, c, k)
        cp = pltpu.make_async_remote_copy(
            src_ref=parts_ref.at[peer_block], dst_ref=recv_ref.at[me], send_sem=send_sems.at[k - 1],
            recv_sem=recv_sems.at[k - 1], device_id=peer, device_id_type=pl.DeviceIdType.MESH)
        cp.start()
        sends.append(cp)
    for k in range(1, N_DEV):
        peer, peer_block = _peer(x, y, c, k)
        pltpu.make_async_remote_copy(
            src_ref=parts_ref.at[peer_block], dst_ref=recv_ref.at[peer_block], send_sem=send_sems.at[k - 1],
            recv_sem=recv_sems.at[k - 1], device_id=peer, device_id_type=pl.DeviceIdType.MESH).wait_recv()
    for cp in sends:
        cp.wait_send()
    own.wait()
    total = recv_ref[0].astype(F32)
    for k in range(1, N_DEV):
        total = total + recv_ref[k].astype(F32)
    return total


def _reduce_scatter_adamw(name, parts, w, m, v):
    shp = w.shape

    def body(parts_ref, w_ref, m_ref, v_ref, g_ref, d_ref, nm_ref, nv_ref, recv_ref, send_sems, recv_sems, own_sem):
        g = _exchange_and_sum(parts_ref, recv_ref, send_sems, recv_sems, own_sem)
        d, nm, nv = _adamw(w_ref[...], g, m_ref[...], v_ref[...])
        g_ref[...] = g
        d_ref[...] = d
        nm_ref[...] = nm
        nv_ref[...] = nv

    vm = pl.BlockSpec(memory_space=pltpu.VMEM)
    return pl.pallas_call(
        body, out_shape=[jax.ShapeDtypeStruct(shp, F32)] * 4,
        in_specs=[pl.BlockSpec(memory_space=pl.ANY), vm, vm, vm], out_specs=[vm] * 4,
        scratch_shapes=[pltpu.VMEM(parts.shape, parts.dtype), pltpu.SemaphoreType.DMA((7,)),
                        pltpu.SemaphoreType.DMA((7,)), pltpu.SemaphoreType.DMA],
        name=name, compiler_params=pltpu.CompilerParams(vmem_limit_bytes=V7X_VMEM_LIMIT))(parts, w, m, v)


def _allreduce_adamw(gpack, wpack, mpack, vpack):
    shp = gpack.shape

    def body(g_in, w_ref, m_ref, v_ref, g_ref, d_ref, nm_ref, nv_ref, recv_ref, send_sems, recv_sems, own_sem):
        x, y, c = _my_place()
        me = 4 * x + 2 * y + c
        own = pltpu.make_async_copy(g_in, recv_ref.at[me], own_sem)
        own.start()
        sends = []
        for k in range(1, N_DEV):
            peer, _ = _peer(x, y, c, k)
            cp = pltpu.make_async_remote_copy(
                src_ref=g_in, dst_ref=recv_ref.at[me], send_sem=send_sems.at[k - 1], recv_sem=recv_sems.at[k - 1],
                device_id=peer, device_id_type=pl.DeviceIdType.MESH)
            cp.start()
            sends.append(cp)
        for k in range(1, N_DEV):
            peer, peer_block = _peer(x, y, c, k)
            pltpu.make_async_remote_copy(
                src_ref=g_in, dst_ref=recv_ref.at[peer_block], send_sem=send_sems.at[k - 1],
                recv_sem=recv_sems.at[k - 1], device_id=peer, device_id_type=pl.DeviceIdType.MESH).wait_recv()
        for cp in sends:
            cp.wait_send()
        own.wait()
        g = recv_ref[0]
        for k in range(1, N_DEV):
            g = g + recv_ref[k]
        d, nm, nv = _adamw(w_ref[...], g, m_ref[...], v_ref[...])
        g_ref[...] = g
        d_ref[...] = d
        nm_ref[...] = nm
        nv_ref[...] = nv

    vm = pl.BlockSpec(memory_space=pltpu.VMEM)
    return pl.pallas_call(
        body, out_shape=[jax.ShapeDtypeStruct(shp, F32)] * 4, in_specs=[vm] * 4, out_specs=[vm] * 4,
        scratch_shapes=[pltpu.VMEM((N_DEV,) + shp, F32), pltpu.SemaphoreType.DMA((7,)),
                        pltpu.SemaphoreType.DMA((7,)), pltpu.SemaphoreType.DMA],
        name="allreduce_small_adamw",
        compiler_params=pltpu.CompilerParams(vmem_limit_bytes=V7X_VMEM_LIMIT))(gpack, wpack, mpack, vpack)


def _s5_tables(lam_re, lam_im, log_step, b_re, b_im, c_re, c_im):
    g = lam_re.shape[0]
    step = jnp.exp(log_step)[:, None]
    mag = jnp.exp(lam_re * step)
    ab_re = mag * jnp.cos(lam_im * step)
    ab_im = mag * jnp.sin(lam_im * step)
    den = lam_re * lam_re + lam_im * lam_im
    nr = ab_re - 1.0
    co_re = (nr * lam_re + ab_im * lam_im) / den
    co_im = (ab_im * lam_re - nr * lam_im) / den
    bb_re = co_re[..., None] * b_re - co_im[..., None] * b_im
    bb_im = co_re[..., None] * b_im + co_im[..., None] * b_re
    eye = jnp.eye(g, dtype=F32)

    def blockdiag(t):
        return (t[:, :, None, :] * eye[:, None, :, None]).reshape(g * t.shape[1], g * t.shape[2])

    return (ab_re.reshape(1, -1), ab_im.reshape(1, -1), blockdiag(bb_re.transpose(0, 2, 1)),
            blockdiag(bb_im.transpose(0, 2, 1)), blockdiag(c_re.transpose(0, 2, 1)), blockdiag(c_im.transpose(0, 2, 1)))


def _scan(name, xr, xi, a_re, a_im, reverse, s_re=None, s_im=None):
    nb, tc, nc, lt = xr.shape
    ln = SCAN_LANES
    n_sq = int(math.log2(tc))
    assert 2 ** n_sq == tc
    with_da = s_re is not None

    def body(*refs):
        if with_da:
            xr_ref, xi_ref, ar_ref, ai_ref, sr_ref, si_ref, or_ref, oi_ref, dar_ref, dai_ref, cr_ref, ci_ref = refs
        else:
            xr_ref, xi_ref, ar_ref, ai_ref, or_ref, oi_ref, cr_ref, ci_ref = refs
        ar = jnp.broadcast_to(ar_ref[...], (nc, ln))
        ai = jnp.broadcast_to(ai_ref[...], (nc, ln))
        zero = jnp.zeros((nc, ln), F32)

        def at(t):
            return tc - 1 - t if reverse else t

        def local(t, carry):
            sr, si = carry
            j = at(t)
            nr = ar * sr - ai * si + xr_ref[0, j]
            ni = ar * si + ai * sr + xi_ref[0, j]
            or_ref[0, j] = nr
            oi_ref[0, j] = ni
            return nr, ni

        lr, li = lax.fori_loop(0, tc, local, (zero, zero))
        pr, pi = ar_ref[...], ai_ref[...]
        for _ in range(n_sq):
            pr, pi = pr * pr - pi * pi, 2.0 * (pr * pi)
        cr_ref[...] = lr
        ci_ref[...] = li
        tr = jnp.zeros((1, ln), F32)
        ti = jnp.zeros((1, ln), F32)
        for c in (range(nc - 1, -1, -1) if reverse else range(nc)):
            l_r = cr_ref[c:c + 1, :]
            l_i = ci_ref[c:c + 1, :]
            cr_ref[c:c + 1, :] = tr
            ci_ref[c:c + 1, :] = ti
            tr, ti = pr * tr - pi * ti + l_r, pr * ti + pi * tr + l_i

        def fix(j, qr, qi):
            nqr = ar * qr - ai * qi
            nqi = ar * qi + ai * qr
            gr = or_ref[0, j] + nqr
            gi = oi_ref[0, j] + nqi
            or_ref[0, j] = gr
            oi_ref[0, j] = gi
            return nqr, nqi, gr, gi

        if not with_da:
            def fixup(t, carry):
                qr, qi, _, _ = fix(at(t), *carry)
                return qr, qi

            lax.fori_loop(0, tc, fixup, (cr_ref[...], ci_ref[...]))
        else:
            def fixup(t, carry):
                qr, qi, dr, di = carry
                j = at(t)
                qr, qi, gr, gi = fix(j, qr, qi)
                pr_, pi_ = sr_ref[0, j - 1], si_ref[0, j - 1]
                return qr, qi, dr + (pr_ * gr + pi_ * gi), di + (pr_ * gi - pi_ * gr)

            qr, qi, dr, di = lax.fori_loop(0, tc - 1, fixup, (cr_ref[...], ci_ref[...], zero, zero))
            _, _, gr, gi = fix(0, qr, qi)
            row = lax.broadcasted_iota(jnp.int32, (nc, ln), 0)
            pr_ = jnp.where(row == 0, 0.0, pltpu.roll(sr_ref[0, tc - 1], 1, 0))
            pi_ = jnp.where(row == 0, 0.0, pltpu.roll(si_ref[0, tc - 1], 1, 0))
            dr = dr + (pr_ * gr + pi_ * gi)
            di = di + (pr_ * gi - pi_ * gr)
            dar_ref[0] = jnp.sum(dr, axis=0, keepdims=True)
            dai_ref[0] = jnp.sum(di, axis=0, keepdims=True)

    blk = pl.BlockSpec((1, tc, nc, ln), lambda b, l: (b, 0, 0, l))
    a_spec = pl.BlockSpec((1, ln), lambda b, l: (0, l))
    ins = [xr, xi, a_re, a_im] + ([s_re, s_im] if with_da else [])
    in_specs = [blk, blk, a_spec, a_spec] + ([blk, blk] if with_da else [])
    out_shape = [jax.ShapeDtypeStruct(xr.shape, F32)] * 2
    out_specs = [blk, blk]
    if with_da:
        out_shape += [jax.ShapeDtypeStruct((nb, 1, lt), F32)] * 2
        out_specs += [pl.BlockSpec((1, 1, ln), lambda b, l: (b, 0, l))] * 2
    return pl.pallas_call(body, grid=(nb, lt // ln), in_specs=in_specs, out_specs=out_specs, out_shape=out_shape,
                          scratch_shapes=[pltpu.VMEM((nc, ln), F32)] * 2, name=name, compiler_params=_params(2))(*ins)


def kernel(x, mem, g_mix_pre, w_in, gm_ln_g, gm_ln_b, gm_w_s, gm_b_s, s5_lam_re, s5_lam_im, s5_log_step, s5_b_re, s5_b_im, s5_c_re, s5_c_im, s5_d, s5_w_glu, w_br_gm, w_br_s5, w_mix_out, g_mix_post, g_ca_pre, g_mem, ca_w_q, ca_w_kv, ca_w_o, g_ca_post, g_ffn_pre, ffn_w_gu, ffn_w_down, g_ffn_post, loss_target, m_g_mix_pre, m_w_in, m_gm_ln_g, m_gm_ln_b, m_gm_w_s, m_gm_b_s, m_s5_lam_re, m_s5_lam_im, m_s5_log_step, m_s5_b_re, m_s5_b_im, m_s5_c_re, m_s5_c_im, m_s5_d, m_s5_w_glu, m_w_br_gm, m_w_br_s5, m_w_mix_out, m_g_mix_post, m_g_ca_pre, m_g_mem, m_ca_w_q, m_ca_w_kv, m_ca_w_o, m_g_ca_post, m_g_ffn_pre, m_ffn_w_gu, m_ffn_w_down, m_g_ffn_post, v_g_mix_pre, v_w_in, v_gm_ln_g, v_gm_ln_b, v_gm_w_s, v_gm_b_s, v_s5_lam_re, v_s5_lam_im, v_s5_log_step, v_s5_b_re, v_s5_b_im, v_s5_c_re, v_s5_c_im, v_s5_d, v_s5_w_glu, v_w_br_gm, v_w_br_s5, v_w_mix_out, v_g_mix_post, v_g_ca_pre, v_g_mem, v_ca_w_q, v_ca_w_kv, v_ca_w_o, v_g_ca_post, v_g_ffn_pre, v_ffn_w_gu, v_ffn_w_down, v_g_ffn_post):
    args = locals()
    W = {n: args[n][0] for n in WEIGHTS}
    M = {n: args['m_' + n][0] for n in WEIGHTS}
    V = {n: args['v_' + n][0] for n in WEIGHTS}

    nb, seq, d = x.shape
    n = nb * seq
    tm = min(ROW_TILE, n)
    nmem = mem.shape[1]
    d2, dh = 2 * d, d // 2
    hd = d // CA_HEADS
    x2d = x.reshape(n, d)
    tgt = loss_target.reshape(n, d)
    mem2d = mem.reshape(nb * nmem, d)

    def row(v):
        return v.reshape(1, -1)

    names = list(SHARDED)
    gathered = _allgather([W[k] for k in names])
    full = {}
    for k, gth in zip(names, gathered):
        r, c = W[k].shape
        full[k] = gth.reshape(N_DEV * r, c) if SHARDED[k] == 'row' else gth.transpose(1, 0, 2).reshape(r, N_DEV * c)
    w_in_f, w_glu_f, w_brgm_f, w_brs5_f, w_mix_f = (full[k] for k in ['w_in', 's5_w_glu', 'w_br_gm', 'w_br_s5', 'w_mix_out'])
    w_q_f, w_kv_f, w_o_f, w_gu_f, w_down_f = (full[k] for k in ['ca_w_q', 'ca_w_kv', 'ca_w_o', 'ffn_w_gu', 'ffn_w_down'])
    ffh = w_down_f.shape[0]

    s5_in = (W['s5_lam_re'], W['s5_lam_im'], W['s5_log_step'], W['s5_b_re'], W['s5_b_im'], W['s5_c_re'], W['s5_c_im'])
    (a_re, a_im, bb_re, bb_im, cc_re, cc_im), s5_vjp = jax.vjp(_s5_tables, *s5_in)
    bb_re_b, bb_im_b, cc_re_b, cc_im_b = (t.astype(BF16) for t in (bb_re, bb_im, cc_re, cc_im))
    s5_d_row = row(W['s5_d'])
    tril = jnp.tril(jnp.ones((GM_CHUNK, GM_CHUNK), bool))
    w_s = jnp.where(tril[None], W['gm_w_s'], 0.0).astype(BF16)
    w_s_t = w_s.transpose(0, 2, 1)
    gm_bias = jnp.repeat(W['gm_b_s'].T, d // GM_GROUPS, axis=1)

    def in_proj_body(x_ref, g_ref, w_ref, zgm_ref, u5_ref, zga_ref, zgb_ref, h_ref):
        hb = _rms(x_ref[...], g_ref[...]).astype(BF16)
        h_ref[...] = hb
        for lo in range(0, w_ref.shape[1], 512):
            acc = _dot(hb, w_ref[:, lo:lo + 512])
            if lo < d2:
                zgm_ref[:, lo:lo + 512] = acc
            elif lo < d2 + dh:
                u5_ref[...] = acc
            elif lo < d2 + dh + d:
                zga_ref[:, lo - d2 - dh:lo - d2 - dh + 512] = acc
            else:
                zgb_ref[:, lo - d2 - dh - d:lo - d2 - dh - d + 512] = acc

    z_gm, u5, z_ga, z_gb, h0 = _rowcall(
        "in_proj", in_proj_body, n, tm, [(x2d, 'row'), (row(W['g_mix_pre']), 'full'), (w_in_f, 'full')],
        [(d2, F32), (dh, F32), (d, F32), (d, F32), (d, BF16)])

    def gm_recompute(z_ref, lng_ref, lnb_ref):
        zg = _gelu(z_ref[...])
        u, v = zg[:, :d], zg[:, d:]
        vc = v - jnp.mean(v, axis=-1, keepdims=True)
        rstd = lax.rsqrt(jnp.mean(vc * vc, axis=-1, keepdims=True) + EPS)
        vhat = vc * rstd
        vn = vhat * lng_ref[...] + lnb_ref[...]
        return u, vhat, rstd, vn.astype(BF16)

    gw = d // GM_GROUPS

    def gm_fwd_body(z_ref, lng_ref, lnb_ref, ws_ref, bias_ref, y_ref):
        u, _, _, vnb = gm_recompute(z_ref, lng_ref, lnb_ref)
        for g in range(GM_GROUPS):
            sl = slice(g * gw, (g + 1) * gw)
            sv = _dot(ws_ref[g], vnb[:, sl]) + bias_ref[:, sl]
            y_ref[:, sl] = (u[:, sl] * sv).astype(BF16)

    (y_gm,) = _rowcall("gmlp_fwd", gm_fwd_body, n, GM_CHUNK,
                       [(z_gm, 'row'), (row(W['gm_ln_g']), 'full'), (row(W['gm_ln_b']), 'full'), (w_s, 'full'),
                        (gm_bias, 'full')], [(d, BF16)])

    tc = seq // SCAN_CHUNKS
    lt = S5_GROUPS * S5_STATE

    def to_scan_order(t):
        return t.reshape(nb, SCAN_CHUNKS, tc, t.shape[-1]).transpose(0, 2, 1, 3).reshape(n, t.shape[-1])

    def from_scan_order(t):
        return t.reshape(nb, tc, SCAN_CHUNKS, t.shape[-1]).transpose(0, 2, 1, 3).reshape(n, t.shape[-1])

    u5p = to_scan_order(u5)
    kq, nq = 4 * S5_CH * 2, 4 * S5_STATE * 2
    n_blk = dh // kq

    def s5_in_body(u_ref, br_ref, bi_ref, or_ref, oi_ref):
        ub = u_ref[...].astype(BF16)
        for q in range(n_blk):
            uq = ub[:, q * kq:(q + 1) * kq]
            or_ref[:, q * nq:(q + 1) * nq] = _dot(uq, br_ref[q * kq:(q + 1) * kq, q * nq:(q + 1) * nq])
            oi_ref[:, q * nq:(q + 1) * nq] = _dot(uq, bi_ref[q * kq:(q + 1) * kq, q * nq:(q + 1) * nq])

    bu_re, bu_im = _rowcall("s5_in", s5_in_body, n, tm, [(u5p, 'row'), (bb_re_b, 'full'), (bb_im_b, 'full')],
                            [(lt, F32), (lt, F32)])
    shape4 = (nb, tc, SCAN_CHUNKS, lt)
    s_re4, s_im4 = _scan("s5_scan_fwd", bu_re.reshape(shape4), bu_im.reshape(shape4), a_re, a_im, False)
    s_re, s_im = s_re4.reshape(n, lt), s_im4.reshape(n, lt)

    def s5_out_body(sr_ref, si_ref, u_ref, cr_ref, ci_ref, d_ref, wg_ref, ypre_ref, gate_ref, y_ref):
        srb, sib = sr_ref[...].astype(BF16), si_ref[...].astype(BF16)
        for q in range(n_blk):
            rs, cs = slice(q * nq, (q + 1) * nq), slice(q * kq, (q + 1) * kq)
            yq = _dot(srb[:, rs], cr_ref[rs, cs]) - _dot(sib[:, rs], ci_ref[rs, cs])
            ypre_ref[:, cs] = yq + d_ref[:, cs] * u_ref[:, cs]
        yg = _gelu(ypre_ref[...])
        gate = _dot(yg.astype(BF16), wg_ref[...])
        gate_ref[...] = gate
        y_ref[...] = (yg * _sigmoid(gate)).astype(BF16)

    y_pre, gate, y_s5p = _rowcall(
        "s5_out", s5_out_body, n, tm,
        [(s_re, 'row'), (s_im, 'row'), (u5p, 'row'), (cc_re_b, 'full'), (cc_im_b, 'full'), (s5_d_row, 'full'),
         (w_glu_f, 'full')], [(dh, F32), (dh, F32), (dh, BF16)])
    y_s5 = from_scan_order(y_s5p)

    def merge_body(ygm_ref, ys5_ref, zga_ref, zgb_ref, wa_ref, wb_ref, pa_ref, pb_ref, mg_ref):
        pa = _dot(ygm_ref[...], wa_ref[...])
        pb = _dot(ys5_ref[...], wb_ref[...])
        pa_ref[...] = pa
        pb_ref[...] = pb
        mg_ref[...] = (_sigmoid(zga_ref[...]) * pa + _sigmoid(zgb_ref[...]) * pb).astype(BF16)

    p_a, p_b, merged = _rowcall(
        "merge", merge_body, n, tm,
        [(y_gm, 'row'), (y_s5, 'row'), (z_ga, 'row'), (z_gb, 'row'), (w_brgm_f, 'full'), (w_brs5_f, 'full')],
        [(d, F32), (d, F32), (d, BF16)])

    def close_sublayer(name, a_in, w_out, x_res, g_post, g_next, w_next=None):
        def body(*refs):
            a_ref, w_ref, x_ref, gp_ref, gn_ref = refs[:5]
            rest = refs[5:]
            if w_next is not None:
                wn_ref, rest = rest[0], rest[1:]
            o_ref, xo_ref, h_ref = rest[:3]
            o = _dot(a_ref[...], w_ref[...])
            o_ref[...] = o
            xo = x_ref[...] + _rms(o, gp_ref[...])
            xo_ref[...] = xo
            hb = _rms(xo, gn_ref[...]).astype(BF16)
            h_ref[...] = hb
            if w_next is not None:
                rest[3][...] = _dot(hb, wn_ref[...]).astype(BF16)

        ins = [(a_in, 'row'), (w_out, 'full'), (x_res, 'row'), (row(g_post), 'full'), (row(g_next), 'full')]
        outs = [(d, F32), (d, F32), (d, BF16)]
        if w_next is not None:
            ins.append((w_next, 'full'))
            outs.append((w_next.shape[1], BF16))
        return _rowcall(name, body, n, tm, ins, outs)

    o1, x1, hc, q = close_sublayer("mix_out", merged, w_mix_f, x2d, W['g_mix_post'], W['g_ca_pre'], w_q_f)

    tmm = min(ROW_TILE, nb * nmem)

    def memkv_body(m_ref, g_ref, w_ref, mn_ref, k_ref, v_ref):
        mnb = _rms(m_ref[...], g_ref[...]).astype(BF16)
        mn_ref[...] = mnb
        k_ref[...] = _dot(mnb, w_ref[:, :d]).astype(BF16)
        v_ref[...] = _dot(mnb, w_ref[:, d:]).astype(BF16)

    mem_n, k_mem, v_mem = _rowcall("mem_kv", memkv_body, nb * nmem, tmm,
                                   [(mem2d, 'row'), (row(W['g_mem']), 'full'), (w_kv_f, 'full')],
                                   [(d, BF16), (d, BF16), (d, BF16)])

    tiles_per_ex = seq // tm
    kv_spec = pl.BlockSpec((nmem, d), lambda i: (i // tiles_per_ex, 0))
    scale = hd ** -0.5

    def softmax_rows(qh, kh):
        s = _dot_nt(qh, kh) * scale
        e = jnp.exp(s - jnp.max(s, axis=-1, keepdims=True))
        return e / jnp.sum(e, axis=-1, keepdims=True)

    def attn_body(q_ref, k_ref, v_ref, o_ref):
        for h in range(CA_HEADS):
            sl = slice(h * hd, (h + 1) * hd)
            p = softmax_rows(q_ref[:, sl], k_ref[:, sl])
            o_ref[:, sl] = _dot(p.astype(BF16), v_ref[:, sl]).astype(BF16)

    (att,) = _rowcall("attn_fwd", attn_body, n, tm, [(q, 'row'), (k_mem, kv_spec), (v_mem, kv_spec)], [(d, BF16)])

    o2, x2, hf = close_sublayer("attn_out", att, w_o_f, x1, W['g_ca_post'], W['g_ffn_pre'])

    ck = 256

    def ffn_up_body(h_ref, w_ref, gu_ref, a_ref):
        hb = h_ref[...]
        for lo in range(0, ffh, ck):
            gt = _dot(hb, w_ref[:, lo:lo + ck])
            ut = _dot(hb, w_ref[:, ffh + lo:ffh + lo + ck])
            gu_ref[:, lo:lo + ck] = gt
            gu_ref[:, ffh + lo:ffh + lo + ck] = ut
            a_ref[:, lo:lo + ck] = ((gt * _sigmoid(gt)) * ut).astype(BF16)

    gu, act = _rowcall("ffn_up", ffn_up_body, n, tm, [(hf, 'row'), (w_gu_f, 'full')], [(2 * ffh, F32), (ffh, BF16)])

    def ffn_down_body(a_ref, w_ref, x_ref, t_ref, g_ref, dx_ref, do_ref, loss_ref, dg_ref):
        i = pl.program_id(0)

        @pl.when(i == 0)
        def _():
            loss_ref[...] = jnp.zeros_like(loss_ref)
            dg_ref[...] = jnp.zeros_like(dg_ref)

        o = _dot(a_ref[...], w_ref[...])
        diff = x_ref[...] + _rms(o, g_ref[...]) - t_ref[...]
        loss_ref[...] += jnp.full(loss_ref.shape, 0.5 / d, F32) * jnp.sum(diff * diff)
        dx = diff * (1.0 / d)
        dx_ref[...] = dx
        do, dg = _rms_bwd(o, g_ref[...], dx)
        do_ref[...] = do.astype(BF16)
        dg_ref[...] += dg

    dx3, do3, loss_part, dg_ffn_post = _rowcall(
        "ffn_down_loss", ffn_down_body, n, tm,
        [(act, 'row'), (w_down_f, 'full'), (x2, 'row'), (tgt, 'row'), (row(W['g_ffn_post']), 'full')],
        [(d, F32), (d, BF16)], accs=[((1, 128), F32), ((1, d), F32)])

    G = {'g_ffn_post': dg_ffn_post}
    GW = {}
    GW['ffn_w_down'] = _dwcall("dw_ffn_down", act, do3, BF16)

    def ffn_act_bwd_body(do_ref, w_ref, gu_ref, dgu_ref):
        dob = do_ref[...]
        for lo in range(0, ffh, ck):
            da = _dot_nt(dob, w_ref[lo:lo + ck, :])
            gt = gu_ref[:, lo:lo + ck]
            ut = gu_ref[:, ffh + lo:ffh + lo + ck]
            sg = _sigmoid(gt)
            dgu_ref[:, lo:lo + ck] = (da * ut * (sg * (1.0 + gt * (1.0 - sg)))).astype(BF16)
            dgu_ref[:, ffh + lo:ffh + lo + ck] = (da * (gt * sg)).astype(BF16)

    (dgu,) = _rowcall("ffn_act_bwd", ffn_act_bwd_body, n, tm, [(do3, 'row'), (w_down_f, 'full'), (gu, 'row')],
                      [(2 * ffh, BF16)])
    GW['ffn_w_gu'] = _dwcall("dw_ffn_gu", hf, dgu, BF16)

    def open_sublayer(name, pieces, w_full, x_in, g_pre, dx_up, o_prev=None, g_post_prev=None):
        n_p = len(pieces)
        second = o_prev is not None

        def body(*refs):
            dc_refs, (w_ref, x_ref, g_ref, dxu_ref), rest = refs[:n_p], refs[n_p:n_p + 4], refs[n_p + 4:]
            if second:
                (op_ref, gp_ref), rest = rest[:2], rest[2:]
            i = pl.program_id(0)
            dhid = None
            for dc_ref, (_, lo, hi) in zip(dc_refs, pieces):
                part = _dot_nt(dc_ref[...], w_ref[:, lo:hi])
                dhid = part if dhid is None else dhid + part
            dxn, dg = _rms_bwd(x_ref[...], g_ref[...], dhid)
            dx = dxu_ref[...] + dxn
            if second:
                dx_ref, do_ref, dg_ref, dg2_ref = rest
            else:
                dx_ref, dg_ref = rest

            @pl.when(i == 0)
            def _():
                dg_ref[...] = jnp.zeros_like(dg_ref)
                if second:
                    dg2_ref[...] = jnp.zeros_like(dg2_ref)

            dx_ref[...] = dx
            dg_ref[...] += dg
            if second:
                do, dg2 = _rms_bwd(op_ref[...], gp_ref[...], dx)
                do_ref[...] = do.astype(BF16)
                dg2_ref[...] += dg2

        ins = [(p[0], 'row') for p in pieces] + [(w_full, 'full'), (x_in, 'row'), (row(g_pre), 'full'), (dx_up, 'row')]
        outs = [(d, F32)]
        accs = [((1, d), F32)]
        if second:
            ins += [(o_prev, 'row'), (row(g_post_prev), 'full')]
            outs.append((d, BF16))
            accs.append(((1, d), F32))
        res = _rowcall(name, body, n, tm, ins, outs, accs=accs)
        if second:
            dx, do, dg, dg2 = res
            return dx, dg, do, dg2
        return tuple(res)

    dx2, G['g_ffn_pre'], do2, G['g_ca_post'] = open_sublayer(
        "ffn_in_bwd", [(dgu, 0, 2 * ffh)], w_gu_f, x2, W['g_ffn_pre'], dx3, o2, W['g_ca_post'])
    GW['ca_w_o'] = _dwcall("dw_ca_o", att, do2, BF16)

    def attn_bwd_body(q_ref, k_ref, v_ref, do_ref, wo_ref, dq_ref, dk_ref, dv_ref):
        i = pl.program_id(0)

        @pl.when(i % tiles_per_ex == 0)
        def _():
            dk_ref[...] = jnp.zeros_like(dk_ref)
            dv_ref[...] = jnp.zeros_like(dv_ref)

        d_att = _dot_nt(do_ref[...], wo_ref[...]).astype(BF16)
        for h in range(CA_HEADS):
            sl = slice(h * hd, (h + 1) * hd)
            qh, kh, vh, dah = q_ref[:, sl], k_ref[:, sl], v_ref[:, sl], d_att[:, sl]
            p = softmax_rows(qh, kh)
            dp = _dot_nt(dah, vh)
            ds = (p * (dp - jnp.sum(p * dp, axis=-1, keepdims=True)) * scale).astype(BF16)
            dq_ref[:, sl] = _dot(ds, kh).astype(BF16)
            dk_ref[:, sl] += _dot_tn(ds, qh)
            dv_ref[:, sl] += _dot_tn(p.astype(BF16), dah)

    kv_acc = ((nb * nmem, d), (F32, pl.BlockSpec((nmem, d), lambda i: (i // tiles_per_ex, 0))))
    dq, dk_mem, dv_mem = _rowcall(
        "attn_bwd", attn_bwd_body, n, tm,
        [(q, 'row'), (k_mem, kv_spec), (v_mem, kv_spec), (do2, 'row'), (w_o_f, 'full')], [(d, BF16)],
        accs=[kv_acc, kv_acc])
    GW['ca_w_q'] = _dwcall("dw_ca_q", hc, dq, BF16)

    def memkv_bwd_body(dk_ref, dv_ref, m_ref, g_ref, w_ref, dkv_ref, dg_ref):
        i = pl.program_id(0)

        @pl.when(i == 0)
        def _():
            dg_ref[...] = jnp.zeros_like(dg_ref)

        dkb, dvb = dk_ref[...].astype(BF16), dv_ref[...].astype(BF16)
        dkv_ref[:, :d] = dkb
        dkv_ref[:, d:] = dvb
        dmn = _dot_nt(dkb, w_ref[:, :d]) + _dot_nt(dvb, w_ref[:, d:])
        _, dg = _rms_bwd(m_ref[...], g_ref[...], dmn)
        dg_ref[...] += dg

    dkv, G['g_mem'] = _rowcall(
        "mem_kv_bwd", memkv_bwd_body, nb * nmem, tmm,
        [(dk_mem, 'row'), (dv_mem, 'row'), (mem2d, 'row'), (row(W['g_mem']), 'full'), (w_kv_f, 'full')],
        [(d2, BF16)], accs=[((1, d), F32)])
    GW['ca_w_kv'] = _dwcall("dw_ca_kv", mem_n, dkv, BF16)

    dx1, G['g_ca_pre'], do1, G['g_mix_post'] = open_sublayer(
        "attn_in_bwd", [(dq, 0, d)], w_q_f, x1, W['g_ca_pre'], dx2, o1, W['g_mix_post'])
    GW['w_mix_out'] = _dwcall("dw_mix_out", merged, do1, BF16)

    def merge_bwd_body(do_ref, wm_ref, zga_ref, zgb_ref, pa_ref, pb_ref, wb_ref, dpa_ref, dpb_ref, dza_ref, dzb_ref,
                       dys_ref):
        dm = _dot_nt(do_ref[...], wm_ref[...])
        sa, sb = _sigmoid(zga_ref[...]), _sigmoid(zgb_ref[...])
        dpb = (dm * sb).astype(BF16)
        dpa_ref[...] = (dm * sa).astype(BF16)
        dpb_ref[...] = dpb
        dza_ref[...] = (dm * pa_ref[...] * (sa * (1.0 - sa))).astype(BF16)
        dzb_ref[...] = (dm * pb_ref[...] * (sb * (1.0 - sb))).astype(BF16)
        dys_ref[...] = _dot_nt(dpb, wb_ref[...]).astype(BF16)

    dp_a, dp_b, dz_ga, dz_gb, dy_s5 = _rowcall(
        "merge_bwd", merge_bwd_body, n, tm,
        [(do1, 'row'), (w_mix_f, 'full'), (z_ga, 'row'), (z_gb, 'row'), (p_a, 'row'), (p_b, 'row'), (w_brs5_f, 'full')],
        [(d, BF16), (d, BF16), (d, BF16), (d, BF16), (dh, BF16)])
    GW['w_br_gm'] = _dwcall("dw_br_gm", y_gm, dp_a, BF16)
    GW['w_br_s5'] = _dwcall("dw_br_s5", y_s5, dp_b, BF16)

    def gm_bwd_body(z_ref, dpa_ref, wa_ref, lng_ref, lnb_ref, ws_ref, wst_ref, bias_ref, dz_ref, dws_ref, dbias_ref,
                    dlng_ref, dlnb_ref, du_s, dvn_s):
        i = pl.program_id(0)

        @pl.when(i == 0)
        def _():
            dws_ref[...] = jnp.zeros_like(dws_ref)
            dbias_ref[...] = jnp.zeros_like(dbias_ref)
            dlng_ref[...] = jnp.zeros_like(dlng_ref)
            dlnb_ref[...] = jnp.zeros_like(dlnb_ref)

        u, vhat, rstd, vnb = gm_recompute(z_ref, lng_ref, lnb_ref)
        dy = _dot_nt(dpa_ref[...], wa_ref[...])
        for g in range(GM_GROUPS):
            sl = slice(g * gw, (g + 1) * gw)
            sv = _dot(ws_ref[g], vnb[:, sl]) + bias_ref[:, sl]
            du_s[:, sl] = dy[:, sl] * sv
            dsv = dy[:, sl] * u[:, sl]
            dsvb = dsv.astype(BF16)
            dvn_s[:, sl] = _dot(wst_ref[g], dsvb)
            dws_ref[g] += _dot_nt(dsvb, vnb[:, sl])
            dbias_ref[:, sl] += dsv
        dvn = dvn_s[...]
        dlng_ref[...] += jnp.sum(dvn * vhat, axis=0, keepdims=True)
        dlnb_ref[...] += jnp.sum(dvn, axis=0, keepdims=True)
        dvh = dvn * lng_ref[...]
        dv = rstd * (dvh - jnp.mean(dvh, axis=-1, keepdims=True) - vhat * jnp.mean(dvh * vhat, axis=-1, keepdims=True))
        dz_ref[:, :d] = (du_s[...] * _gelu_grad(z_ref[:, :d])).astype(BF16)
        dz_ref[:, d:] = (dv * _gelu_grad(z_ref[:, d:])).astype(BF16)

    dz_gm, dws_full, dbias_full, G['gm_ln_g'], G['gm_ln_b'] = _rowcall(
        "gmlp_bwd", gm_bwd_body, n, GM_CHUNK,
        [(z_gm, 'row'), (dp_a, 'row'), (w_brgm_f, 'full'), (row(W['gm_ln_g']), 'full'), (row(W['gm_ln_b']), 'full'),
         (w_s, 'full'), (w_s_t, 'full'), (gm_bias, 'full')],
        [(d2, BF16)], accs=[((GM_GROUPS, GM_CHUNK, GM_CHUNK), F32), ((GM_CHUNK, d), F32), ((1, d), F32), ((1, d), F32)],
        scratch=[pltpu.VMEM((GM_CHUNK, d), F32), pltpu.VMEM((GM_CHUNK, d), F32)])
    G['gm_w_s'] = jnp.where(tril[None], dws_full, 0.0)
    G['gm_b_s'] = dbias_full.reshape(GM_CHUNK, GM_GROUPS, gw).sum(-1).T

    dy_s5p = to_scan_order(dy_s5)

    def s5_out_bwd_body(dy_ref, ypre_ref, gate_ref, u_ref, wg_ref, cr_ref, ci_ref, dyp_ref, dgate_ref, yg_ref, gr_ref,
                        gi_ref, dd_ref):
        i = pl.program_id(0)

        @pl.when(i == 0)
        def _():
            dd_ref[...] = jnp.zeros_like(dd_ref)

        dy = dy_ref[...].astype(F32)
        ypre = ypre_ref[...]
        yg = _gelu(ypre)
        sg = _sigmoid(gate_ref[...])
        dgb = (dy * yg * (sg * (1.0 - sg))).astype(BF16)
        dgate_ref[...] = dgb
        yg_ref[...] = yg.astype(BF16)
        dyp = (dy * sg + _dot_nt(dgb, wg_ref[...])) * _gelu_grad(ypre)
        dyp_ref[...] = dyp
        dd_ref[...] += jnp.sum(dyp * u_ref[...], axis=0, keepdims=True)
        dypb = dyp.astype(BF16)
        for q in range(n_blk):
            rs, cs = slice(q * nq, (q + 1) * nq), slice(q * kq, (q + 1) * kq)
            gr_ref[:, rs] = _dot_nt(dypb[:, cs], cr_ref[rs, cs])
            gi_ref[:, rs] = -_dot_nt(dypb[:, cs], ci_ref[rs, cs])

    dy_pre, dgate, yg_b, gin_re, gin_im, dd = _rowcall(
        "s5_out_bwd", s5_out_bwd_body, n, tm,
        [(dy_s5p, 'row'), (y_pre, 'row'), (gate, 'row'), (u5p, 'row'), (w_glu_f, 'full'), (cc_re_b, 'full'),
         (cc_im_b, 'full')],
        [(dh, F32), (dh, BF16), (dh, BF16), (lt, F32), (lt, F32)], accs=[((1, dh), F32)])
    GW['s5_w_glu'] = _dwcall("dw_s5_glu", yg_b, dgate, BF16)
    d_cc_re = _dwcall("dw_s5_c_re", s_re, dy_pre, F32)
    d_cc_im = -_dwcall("dw_s5_c_im", s_im, dy_pre, F32)

    gs_re4, gs_im4, da_re, da_im = _scan("s5_scan_bwd", gin_re.reshape(shape4), gin_im.reshape(shape4), a_re, -a_im,
                                         True, s_re4, s_im4)
    gs_re, gs_im = gs_re4.reshape(n, lt), gs_im4.reshape(n, lt)
    d_bb_re = _dwcall("dw_s5_b_re", u5p, gs_re, F32)
    d_bb_im = _dwcall("dw_s5_b_im", u5p, gs_im, F32)

    def s5_in_bwd_body(gr_ref, gi_ref, dyp_ref, br_ref, bi_ref, d_ref, du_ref):
        grb, gib = gr_ref[...].astype(BF16), gi_ref[...].astype(BF16)
        for q in range(n_blk):
            rs, cs = slice(q * kq, (q + 1) * kq), slice(q * nq, (q + 1) * nq)
            du = _dot_nt(grb[:, cs], br_ref[rs, cs]) + _dot_nt(gib[:, cs], bi_ref[rs, cs])
            du_ref[:, rs] = (du + d_ref[:, rs] * dyp_ref[:, rs]).astype(BF16)

    (du5p,) = _rowcall("s5_in_bwd", s5_in_bwd_body, n, tm,
                       [(gs_re, 'row'), (gs_im, 'row'), (dy_pre, 'row'), (bb_re_b, 'full'), (bb_im_b, 'full'),
                        (s5_d_row, 'full')], [(dh, BF16)])
    du5 = from_scan_order(du5p)
    s5_grads = s5_vjp((jnp.sum(da_re, axis=0), jnp.sum(da_im, axis=0), d_bb_re, d_bb_im, d_cc_re, d_cc_im))
    for k, gval in zip(['s5_lam_re', 's5_lam_im', 's5_log_step', 's5_b_re', 's5_b_im', 's5_c_re', 's5_c_im'], s5_grads):
        G[k] = gval
    G['s5_d'] = dd

    pieces = [(dz_gm, 0, d2), (du5, d2, d2 + dh), (dz_ga, d2 + dh, d2 + dh + d), (dz_gb, d2 + dh + d, d2 + dh + 2 * d)]
    grad_x2d, G['g_mix_pre'] = open_sublayer("in_proj_bwd", pieces, w_in_f, x2d, W['g_mix_pre'], dx1)
    GW['w_in'] = jnp.concatenate(
        [_dwcall("dw_in_%d" % j, h0, p[0], BF16) for j, p in enumerate(pieces)], axis=1)

    out = {}
    for k in names:
        r, c = W[k].shape
        gfull = GW[k]
        parts = gfull.reshape(N_DEV, r, c) if SHARDED[k] == 'row' else gfull.reshape(r, N_DEV, c).transpose(1, 0, 2)
        out[k] = _reduce_scatter_adamw("rs_adamw_" + k, parts, W[k], M[k], V[k])

    def pack(vals, extra=None):
        flat = [vals[k].reshape(-1).astype(F32) for k in SMALL]
        if extra is not None:
            flat.append(extra)
        flat = jnp.concatenate(flat)
        pad = (-flat.shape[0]) % 1024
        return jnp.pad(flat, (0, pad)).reshape(-1, 128)

    one = jnp.zeros((1,), F32)
    gp, dp, nmp, nvp = _allreduce_adamw(pack(G, loss_part[0, :1]), pack(W, one), pack(M, one), pack(V, one + 1.0))
    off = 0
    small_out = {}
    for k in SMALL:
        sz = W[k].size
        small_out[k] = tuple(t.reshape(-1)[off:off + sz].reshape(W[k].shape) for t in (gp, dp, nmp, nvp))
        off += sz
    loss = gp.reshape(-1)[off]
    out.update(small_out)

    res = [loss, grad_x2d.reshape(x.shape)]
    for j in range(4):
        res += [out[k][j][None] for k in WEIGHTS]
    return tuple(res)
```

```python
import functools
import math

import jax
import jax.numpy as jnp
from jax import lax
from jax.experimental import pallas as pl
from jax.experimental.pallas import tpu as pltpu

F32 = jnp.float32
BF16 = jnp.bfloat16
EPS = 1e-6
N_DEV = 8
V7X_VMEM_LIMIT = 56 * 1024 * 1024
ROW_TILE = 256
GM_CHUNK = 128
GM_GROUPS = 8
S5_GROUPS = 32
S5_STATE = 64
S5_CH = 16
SCAN_CHUNKS = 32
SCAN_LANES = 128
CA_HEADS = 4
ADAM_LR, ADAM_B1, ADAM_B2, ADAM_EPS, ADAM_WD, ADAM_STEP = 0.001, 0.9, 0.999, 1e-08, 0.01, 10

WEIGHTS = ['g_mix_pre', 'w_in', 'gm_ln_g', 'gm_ln_b', 'gm_w_s', 'gm_b_s', 's5_lam_re', 's5_lam_im', 's5_log_step',
           's5_b_re', 's5_b_im', 's5_c_re', 's5_c_im', 's5_d', 's5_w_glu', 'w_br_gm', 'w_br_s5', 'w_mix_out',
           'g_mix_post', 'g_ca_pre', 'g_mem', 'ca_w_q', 'ca_w_kv', 'ca_w_o', 'g_ca_post', 'g_ffn_pre', 'ffn_w_gu',
           'ffn_w_down', 'g_ffn_post']
SHARDED = {'w_in': 'col', 's5_w_glu': 'row', 'w_br_gm': 'row', 'w_br_s5': 'col', 'w_mix_out': 'row',
           'ca_w_q': 'row', 'ca_w_kv': 'col', 'ca_w_o': 'row', 'ffn_w_gu': 'col', 'ffn_w_down': 'row'}
SMALL = [n for n in WEIGHTS if n not in SHARDED]


def _rms(x, g):
    r = lax.rsqrt(jnp.mean(x * x, axis=-1, keepdims=True) + EPS)
    return (x * r) * g


def _rms_bwd(x, g, dy):
    r = lax.rsqrt(jnp.mean(x * x, axis=-1, keepdims=True) + EPS)
    n = x * r
    dn = dy * g
    dx = r * (dn - n * jnp.mean(dn * n, axis=-1, keepdims=True))
    return dx, jnp.sum(dy * n, axis=0, keepdims=True)


_GELU_C = math.sqrt(2.0 / math.pi)


def _gelu(x):
    return 0.5 * x * (1.0 + jnp.tanh(_GELU_C * (x + 0.044715 * (x * x * x))))


def _gelu_grad(x):
    t = jnp.tanh(_GELU_C * (x + 0.044715 * (x * x * x)))
    return 0.5 * (1.0 + t) + 0.5 * x * (1.0 - t * t) * (_GELU_C * (1.0 + 3.0 * 0.044715 * (x * x)))


def _sigmoid(x):
    return 1.0 / (1.0 + jnp.exp(-x))


def _dot(a, b):
    return jnp.dot(a, b, preferred_element_type=F32)


def _dot_nt(a, b):
    return lax.dot_general(a, b, (((1,), (1,)), ((), ())), preferred_element_type=F32)


def _dot_tn(a, b):
    return lax.dot_general(a, b, (((0,), (0,)), ((), ())), preferred_element_type=F32)


def _adamw(w, g, m, v):
    m = ADAM_B1 * m + (1.0 - ADAM_B1) * g
    v = ADAM_B2 * v + (1.0 - ADAM_B2) * (g * g)
    m_hat = m / (1.0 - ADAM_B1 ** ADAM_STEP)
    v_hat = v / (1.0 - ADAM_B2 ** ADAM_STEP)
    delta = -ADAM_LR * (m_hat / (jnp.sqrt(v_hat) + ADAM_EPS) + ADAM_WD * w)
    return delta, m, v


def _params(n_grid):
    return pltpu.CompilerParams(dimension_semantics=("arbitrary",) * n_grid, vmem_limit_bytes=V7X_VMEM_LIMIT)


def _my_place():
    x, y, c = lax.axis_index("x"), lax.axis_index("y"), lax.axis_index("c")
    return x, y, c


def _peer(x, y, c, k):
    px = 1 - x if k & 4 else x
    py = 1 - y if k & 2 else y
    pc = 1 - c if k & 1 else c
    return (px, py, pc), 4 * px + 2 * py + pc


def _exchange(kind, src_refs, dst_refs, sems, first, phase):
    x, y, c = _my_place()
    me = 4 * x + 2 * y + c
    send_sems, recv_sems, own_sems = sems
    for j, (src, dst) in enumerate(zip(src_refs, dst_refs), start=first):
        own = pltpu.make_async_copy(src if kind == 'gather' else src.at[me], dst.at[me], own_sems.at[j])
        if phase == 'start':
            own.start()
        for k in range(1, N_DEV):
            peer, peer_block = _peer(x, y, c, k)
            out = pltpu.make_async_remote_copy(
                src_ref=src if kind == 'gather' else src.at[peer_block], dst_ref=dst.at[me],
                send_sem=send_sems.at[7 * j + k - 1], recv_sem=recv_sems.at[7 * j + k - 1], device_id=peer,
                device_id_type=pl.DeviceIdType.MESH)
            if phase == 'start':
                out.start()
            else:
                pltpu.make_async_remote_copy(
                    src_ref=src if kind == 'gather' else src.at[peer_block], dst_ref=dst.at[peer_block],
                    send_sem=send_sems.at[7 * j + k - 1], recv_sem=recv_sems.at[7 * j + k - 1], device_id=peer,
                    device_id_type=pl.DeviceIdType.MESH).wait_recv()
                out.wait_send()
        if phase == 'wait':
            own.wait()


def _rowcall(name, body, n_rows, tm, ins, outs, accs=(), scratch=(), gather=(), scatter=()):
    n_steps = n_rows // tm
    moved = [('gather', a) for a in gather] + [('scatter', a) for a in scatter]
    if moved:
        n_in, n_out, n_mv, n_scr = len(ins), len(outs) + len(accs), len(moved), len(scratch)
        inner = body

        def body(*refs):
            mv_src = refs[n_in:n_in + n_mv]
            mv_dst = refs[n_in + n_mv + n_out:n_in + 2 * n_mv + n_out]
            sems = refs[n_in + 2 * n_mv + n_out + n_scr:]
            i = pl.program_id(0)

            def exchange(phase):
                for kind, lo, hi in (('gather', 0, len(gather)), ('scatter', len(gather), n_mv)):
                    if hi > lo:
                        _exchange(kind, mv_src[lo:hi], mv_dst[lo:hi], sems, lo, phase)

            pl.when(i == 0)(functools.partial(exchange, 'start'))
            inner(*refs[:n_in], *refs[n_in + n_mv:n_in + n_mv + n_out],
                  *refs[n_in + 2 * n_mv + n_out:n_in + 2 * n_mv + n_out + n_scr])
            pl.when(i == n_steps - 1)(functools.partial(exchange, 'wait'))

    arrays, in_specs = [], []
    for a, kind in ins:
        arrays.append(a)
        if kind == 'row':
            in_specs.append(pl.BlockSpec((tm,) + a.shape[1:], lambda i, nd=a.ndim: (i,) + (0,) * (nd - 1)))
        elif kind == 'full':
            in_specs.append(pl.BlockSpec(a.shape, lambda i, nd=a.ndim: (0,) * nd))
        else:
            in_specs.append(kind)
    out_shape, out_specs = [], []
    for cols, dt in outs:
        out_shape.append(jax.ShapeDtypeStruct((n_rows, cols), dt))
        out_specs.append(pl.BlockSpec((tm, cols), lambda i: (i, 0)))
    for shp, dt in accs:
        if isinstance(dt, tuple):
            dt, spec = dt
        else:
            spec = pl.BlockSpec(shp, lambda i, nd=len(shp): (0,) * nd)
        out_shape.append(jax.ShapeDtypeStruct(shp, dt))
        out_specs.append(spec)
    scratch = list(scratch)
    for kind, a in moved:
        arrays.append(a)
        in_specs.append(pl.BlockSpec(memory_space=pl.ANY))
        out_shape.append(jax.ShapeDtypeStruct((N_DEV,) + a.shape if kind == 'gather' else a.shape, a.dtype))
        out_specs.append(pl.BlockSpec(memory_space=pl.ANY))
    if moved:
        scratch += [pltpu.SemaphoreType.DMA((7 * len(moved),)), pltpu.SemaphoreType.DMA((7 * len(moved),)),
                    pltpu.SemaphoreType.DMA((len(moved),))]
    return pl.pallas_call(functools.partial(body), grid=(n_steps,), in_specs=in_specs, out_specs=out_specs,
                          out_shape=out_shape, scratch_shapes=scratch, name=name, compiler_params=_params(1))(*arrays)


def _dw_tiles(n, ka, nn, a_itemsize):
    tn = max(t for t in range(128, min(nn, 1536) + 1, 128) if nn % t == 0)
    tm = min(n, 2048)
    while tm > 256 and tm * ka * a_itemsize > 6 * 1024 * 1024:
        tm //= 2
    return tm, tn


def _dwcall(name, a, dc, out_dtype):
    n, ka = a.shape
    nn = dc.shape[1]
    tm, tn = _dw_tiles(n, ka, nn, a.dtype.itemsize)
    n_i = n // tm

    def body(a_ref, dc_ref, o_ref, acc_ref):
        i = pl.program_id(1)

        @pl.when(i == 0)
        def _():
            acc_ref[...] = jnp.zeros_like(acc_ref)

        acc_ref[...] += _dot_tn(a_ref[...].astype(BF16), dc_ref[...].astype(BF16))

        @pl.when(i == n_i - 1)
        def _():
            o_ref[...] = acc_ref[...].astype(o_ref.dtype)

    return pl.pallas_call(
        body, grid=(nn // tn, n_i),
        in_specs=[pl.BlockSpec((tm, ka), lambda j, i: (i, 0)), pl.BlockSpec((tm, tn), lambda j, i: (i, j))],
        out_specs=pl.BlockSpec((ka, tn), lambda j, i: (0, j)),
        out_shape=jax.ShapeDtypeStruct((ka, nn), out_dtype),
        scratch_shapes=[pltpu.VMEM((ka, tn), F32)], name=name, compiler_params=_params(2))(a, dc)


def _first_gather(first, later):
    n = len(later)
    vm = pl.BlockSpec(memory_space=pltpu.VMEM)

    def body(*refs):
        first_ref, later_refs = refs[0], refs[1:1 + n]
        out_ref, cast_refs = refs[1 + n], refs[2 + n:2 + 2 * n]
        stage = refs[2 + 2 * n]
        sems = refs[3 + 2 * n:]
        stage[...] = first_ref[...].astype(BF16)
        _exchange('gather', [stage], [out_ref], sems, 0, 'start')
        for src, dst in zip(later_refs, cast_refs):
            dst[...] = src[...].astype(BF16)
        _exchange('gather', [stage], [out_ref], sems, 0, 'wait')

    return pl.pallas_call(
        body, out_shape=[jax.ShapeDtypeStruct((N_DEV,) + first.shape, BF16)]
        + [jax.ShapeDtypeStruct(s.shape, BF16) for s in later],
        in_specs=[vm] * (1 + n), out_specs=[pl.BlockSpec(memory_space=pl.ANY)] + [vm] * n,
        scratch_shapes=[pltpu.VMEM(first.shape, BF16), pltpu.SemaphoreType.DMA((7,)), pltpu.SemaphoreType.DMA((7,)),
                        pltpu.SemaphoreType.DMA((1,))],
        name="gather_first", compiler_params=pltpu.CompilerParams(vmem_limit_bytes=V7X_VMEM_LIMIT))(first, *later)


def _sum_adamw(name, recv, w, m, v):
    shp = w.shape

    def body(recv_ref, w_ref, m_ref, v_ref, g_ref, d_ref, nm_ref, nv_ref):
        g = recv_ref[0].astype(F32)
        for k in range(1, N_DEV):
            g = g + recv_ref[k].astype(F32)
        d, nm, nv = _adamw(w_ref[...], g, m_ref[...], v_ref[...])
        g_ref[...] = g
        d_ref[...] = d
        nm_ref[...] = nm
        nv_ref[...] = nv

    vm = pl.BlockSpec(memory_space=pltpu.VMEM)
    return pl.pallas_call(
        body, out_shape=[jax.ShapeDtypeStruct(shp, F32)] * 4, in_specs=[vm] * 4, out_specs=[vm] * 4,
        name=name, compiler_params=pltpu.CompilerParams(vmem_limit_bytes=V7X_VMEM_LIMIT))(recv, w, m, v)


def _allreduce_adamw(gpack, wpack, mpack, vpack):
    shp = gpack.shape

    def body(g_in, w_ref, m_ref, v_ref, g_ref, d_ref, nm_ref, nv_ref, recv_ref, send_sems, recv_sems, own_sem):
        x, y, c = _my_place()
        me = 4 * x + 2 * y + c
        own = pltpu.make_async_copy(g_in, recv_ref.at[me], own_sem)
        own.start()
        sends = []
        for k in range(1, N_DEV):
            peer, _ = _peer(x, y, c, k)
            cp = pltpu.make_async_remote_copy(
                src_ref=g_in, dst_ref=recv_ref.at[me], send_sem=send_sems.at[k - 1], recv_sem=recv_sems.at[k - 1],
                device_id=peer, device_id_type=pl.DeviceIdType.MESH)
            cp.start()
            sends.append(cp)
        for k in range(1, N_DEV):
            peer, peer_block = _peer(x, y, c, k)
            pltpu.make_async_remote_copy(
                src_ref=g_in, dst_ref=recv_ref.at[peer_block], send_sem=send_sems.at[k - 1],
                recv_sem=recv_sems.at[k - 1], device_id=peer, device_id_type=pl.DeviceIdType.MESH).wait_recv()
        for cp in sends:
            cp.wait_send()
        own.wait()
        g = recv_ref[0]
        for k in range(1, N_DEV):
            g = g + recv_ref[k]
        d, nm, nv = _adamw(w_ref[...], g, m_ref[...], v_ref[...])
        g_ref[...] = g
        d_ref[...] = d
        nm_ref[...] = nm
        nv_ref[...] = nv

    vm = pl.BlockSpec(memory_space=pltpu.VMEM)
    return pl.pallas_call(
        body, out_shape=[jax.ShapeDtypeStruct(shp, F32)] * 4, in_specs=[vm] * 4, out_specs=[vm] * 4,
        scratch_shapes=[pltpu.VMEM((N_DEV,) + shp, F32), pltpu.SemaphoreType.DMA((7,)),
                        pltpu.SemaphoreType.DMA((7,)), pltpu.SemaphoreType.DMA],
        name="allreduce_small_adamw",
        compiler_params=pltpu.CompilerParams(vmem_limit_bytes=V7X_VMEM_LIMIT))(gpack, wpack, mpack, vpack)


def _s5_tables(lam_re, lam_im, log_step, b_re, b_im, c_re, c_im):
    g = lam_re.shape[0]
    step = jnp.exp(log_step)[:, None]
    mag = jnp.exp(lam_re * step)
    ab_re = mag * jnp.cos(lam_im * step)
    ab_im = mag * jnp.sin(lam_im * step)
    den = lam_re * lam_re + lam_im * lam_im
    nr = ab_re - 1.0
    co_re = (nr * lam_re + ab_im * lam_im) / den
    co_im = (ab_im * lam_re - nr * lam_im) / den
    bb_re = co_re[..., None] * b_re - co_im[..., None] * b_im
    bb_im = co_re[..., None] * b_im + co_im[..., None] * b_re
    eye = jnp.eye(g, dtype=F32)

    def blockdiag(t):
        return (t[:, :, None, :] * eye[:, None, :, None]).reshape(g * t.shape[1], g * t.shape[2])

    return (ab_re.reshape(1, -1), ab_im.reshape(1, -1), blockdiag(bb_re.transpose(0, 2, 1)),
            blockdiag(bb_im.transpose(0, 2, 1)), blockdiag(c_re.transpose(0, 2, 1)), blockdiag(c_im.transpose(0, 2, 1)))


def _scan(name, xr, xi, a_re, a_im, reverse, s_re=None, s_im=None):
    nb, tc, nc, lt = xr.shape
    ln = SCAN_LANES
    n_sq = int(math.log2(tc))
    assert 2 ** n_sq == tc
    with_da = s_re is not None

    def body(*refs):
        if with_da:
            xr_ref, xi_ref, ar_ref, ai_ref, sr_ref, si_ref, or_ref, oi_ref, dar_ref, dai_ref, cr_ref, ci_ref = refs
        else:
            xr_ref, xi_ref, ar_ref, ai_ref, or_ref, oi_ref, cr_ref, ci_ref = refs
        ar = jnp.broadcast_to(ar_ref[...], (nc, ln))
        ai = jnp.broadcast_to(ai_ref[...], (nc, ln))
        zero = jnp.zeros((nc, ln), F32)

        def at(t):
            return tc - 1 - t if reverse else t

        def local(t, carry):
            sr, si = carry
            j = at(t)
            nr = ar * sr - ai * si + xr_ref[0, j]
            ni = ar * si + ai * sr + xi_ref[0, j]
            or_ref[0, j] = nr
            oi_ref[0, j] = ni
            return nr, ni

        lr, li = lax.fori_loop(0, tc, local, (zero, zero))
        pr, pi = ar_ref[...], ai_ref[...]
        for _ in range(n_sq):
            pr, pi = pr * pr - pi * pi, 2.0 * (pr * pi)
        cr_ref[...] = lr
        ci_ref[...] = li
        tr = jnp.zeros((1, ln), F32)
        ti = jnp.zeros((1, ln), F32)
        for c in (range(nc - 1, -1, -1) if reverse else range(nc)):
            l_r = cr_ref[c:c + 1, :]
            l_i = ci_ref[c:c + 1, :]
            cr_ref[c:c + 1, :] = tr
            ci_ref[c:c + 1, :] = ti
            tr, ti = pr * tr - pi * ti + l_r, pr * ti + pi * tr + l_i

        def fix(j, qr, qi):
            nqr = ar * qr - ai * qi
            nqi = ar * qi + ai * qr
            gr = or_ref[0, j] + nqr
            gi = oi_ref[0, j] + nqi
            or_ref[0, j] = gr
            oi_ref[0, j] = gi
            return nqr, nqi, gr, gi

        if not with_da:
            def fixup(t, carry):
                qr, qi, _, _ = fix(at(t), *carry)
                return qr, qi

            lax.fori_loop(0, tc, fixup, (cr_ref[...], ci_ref[...]))
        else:
            def fixup(t, carry):
                qr, qi, dr, di = carry
                j = at(t)
                qr, qi, gr, gi = fix(j, qr, qi)
                pr_, pi_ = sr_ref[0, j - 1], si_ref[0, j - 1]
                return qr, qi, dr + (pr_ * gr + pi_ * gi), di + (pr_ * gi - pi_ * gr)

            qr, qi, dr, di = lax.fori_loop(0, tc - 1, fixup, (cr_ref[...], ci_ref[...], zero, zero))
            _, _, gr, gi = fix(0, qr, qi)
            row = lax.broadcasted_iota(jnp.int32, (nc, ln), 0)
            pr_ = jnp.where(row == 0, 0.0, pltpu.roll(sr_ref[0, tc - 1], 1, 0))
            pi_ = jnp.where(row == 0, 0.0, pltpu.roll(si_ref[0, tc - 1], 1, 0))
            dr = dr + (pr_ * gr + pi_ * gi)
            di = di + (pr_ * gi - pi_ * gr)
            dar_ref[0] = jnp.sum(dr, axis=0, keepdims=True)
            dai_ref[0] = jnp.sum(di, axis=0, keepdims=True)

    blk = pl.BlockSpec((1, tc, nc, ln), lambda b, l: (b, 0, 0, l))
    a_spec = pl.BlockSpec((1, ln), lambda b, l: (0, l))
    ins = [xr, xi, a_re, a_im] + ([s_re, s_im] if with_da else [])
    in_specs = [blk, blk, a_spec, a_spec] + ([blk, blk] if with_da else [])
    out_shape = [jax.ShapeDtypeStruct(xr.shape, F32)] * 2
    out_specs = [blk, blk]
    if with_da:
        out_shape += [jax.ShapeDtypeStruct((nb, 1, lt), F32)] * 2
        out_specs += [pl.BlockSpec((1, 1, ln), lambda b, l: (b, 0, l))] * 2
    return pl.pallas_call(body, grid=(nb, lt // ln), in_specs=in_specs, out_specs=out_specs, out_shape=out_shape,
                          scratch_shapes=[pltpu.VMEM((nc, ln), F32)] * 2, name=name, compiler_params=_params(2))(*ins)


def kernel(x, mem, g_mix_pre, w_in, gm_ln_g, gm_ln_b, gm_w_s, gm_b_s, s5_lam_re, s5_lam_im, s5_log_step, s5_b_re, s5_b_im, s5_c_re, s5_c_im, s5_d, s5_w_glu, w_br_gm, w_br_s5, w_mix_out, g_mix_post, g_ca_pre, g_mem, ca_w_q, ca_w_kv, ca_w_o, g_ca_post, g_ffn_pre, ffn_w_gu, ffn_w_down, g_ffn_post, loss_target, m_g_mix_pre, m_w_in, m_gm_ln_g, m_gm_ln_b, m_gm_w_s, m_gm_b_s, m_s5_lam_re, m_s5_lam_im, m_s5_log_step, m_s5_b_re, m_s5_b_im, m_s5_c_re, m_s5_c_im, m_s5_d, m_s5_w_glu, m_w_br_gm, m_w_br_s5, m_w_mix_out, m_g_mix_post, m_g_ca_pre, m_g_mem, m_ca_w_q, m_ca_w_kv, m_ca_w_o, m_g_ca_post, m_g_ffn_pre, m_ffn_w_gu, m_ffn_w_down, m_g_ffn_post, v_g_mix_pre, v_w_in, v_gm_ln_g, v_gm_ln_b, v_gm_w_s, v_gm_b_s, v_s5_lam_re, v_s5_lam_im, v_s5_log_step, v_s5_b_re, v_s5_b_im, v_s5_c_re, v_s5_c_im, v_s5_d, v_s5_w_glu, v_w_br_gm, v_w_br_s5, v_w_mix_out, v_g_mix_post, v_g_ca_pre, v_g_mem, v_ca_w_q, v_ca_w_kv, v_ca_w_o, v_g_ca_post, v_g_ffn_pre, v_ffn_w_gu, v_ffn_w_down, v_g_ffn_post):
    args = locals()
    W = {n: args[n][0] for n in WEIGHTS}
    M = {n: args['m_' + n][0] for n in WEIGHTS}
    V = {n: args['v_' + n][0] for n in WEIGHTS}

    nb, seq, d = x.shape
    n = nb * seq
    tm = min(ROW_TILE, n)
    nmem = mem.shape[1]
    d2, dh = 2 * d, d // 2
    hd = d // CA_HEADS
    x2d = x.reshape(n, d)
    tgt = loss_target.reshape(n, d)
    mem2d = mem.reshape(nb * nmem, d)

    def row(v):
        return v.reshape(1, -1)

    names = list(SHARDED)
    later = [k for k in names if k != 'w_in']
    first_g, *casts = _first_gather(W['w_in'], [W[k] for k in later])
    shard_b = dict(zip(later, casts))
    half = shard_b['ffn_w_gu'].shape[0] // 2
    gu_halves = [shard_b['ffn_w_gu'][:half], shard_b['ffn_w_gu'][half:]]

    def assemble(k, gth):
        r, c = gth.shape[1:]
        return gth.reshape(N_DEV * r, c) if SHARDED[k] == 'row' else gth.transpose(1, 0, 2).reshape(r, N_DEV * c)

    w_in_f = assemble('w_in', first_g)
    ffh = N_DEV * W['ffn_w_down'].shape[0]

    s5_in = (W['s5_lam_re'], W['s5_lam_im'], W['s5_log_step'], W['s5_b_re'], W['s5_b_im'], W['s5_c_re'], W['s5_c_im'])
    (a_re, a_im, bb_re, bb_im, cc_re, cc_im), s5_vjp = jax.vjp(_s5_tables, *s5_in)
    bb_re_b, bb_im_b, cc_re_b, cc_im_b = (t.astype(BF16) for t in (bb_re, bb_im, cc_re, cc_im))
    s5_d_row = row(W['s5_d'])
    tril = jnp.tril(jnp.ones((GM_CHUNK, GM_CHUNK), bool))
    w_s = jnp.where(tril[None], W['gm_w_s'], 0.0).astype(BF16)
    w_s_t = w_s.transpose(0, 2, 1)
    gm_bias = jnp.repeat(W['gm_b_s'].T, d // GM_GROUPS, axis=1)

    def in_proj_body(x_ref, g_ref, w_ref, zgm_ref, u5_ref, zga_ref, zgb_ref, h_ref):
        hb = _rms(x_ref[...], g_ref[...]).astype(BF16)
        h_ref[...] = hb
        for lo in range(0, w_ref.shape[1], 512):
            acc = _dot(hb, w_ref[:, lo:lo + 512])
            if lo < d2:
                zgm_ref[:, lo:lo + 512] = acc
            elif lo < d2 + dh:
                u5_ref[...] = acc
            elif lo < d2 + dh + d:
                zga_ref[:, lo - d2 - dh:lo - d2 - dh + 512] = acc
            else:
                zgb_ref[:, lo - d2 - dh - d:lo - d2 - dh - d + 512] = acc

    ride = ['s5_w_glu', 'w_br_gm', 'w_br_s5', 'w_mix_out', 'ca_w_q']
    z_gm, u5, z_ga, z_gb, h0, *got = _rowcall(
        "in_proj", in_proj_body, n, tm, [(x2d, 'row'), (row(W['g_mix_pre']), 'full'), (w_in_f, 'full')],
        [(d2, F32), (dh, F32), (d, F32), (d, F32), (d, BF16)], gather=[shard_b[k] for k in ride])
    w_glu_f, w_brgm_f, w_brs5_f, w_mix_f, w_q_f = (assemble(k, g) for k, g in zip(ride, got))

    def gm_recompute(z_ref, lng_ref, lnb_ref):
        zg = _gelu(z_ref[...])
        u, v = zg[:, :d], zg[:, d:]
        vc = v - jnp.mean(v, axis=-1, keepdims=True)
        rstd = lax.rsqrt(jnp.mean(vc * vc, axis=-1, keepdims=True) + EPS)
        vhat = vc * rstd
        vn = vhat * lng_ref[...] + lnb_ref[...]
        return u, vhat, rstd, vn.astype(BF16)

    gw = d // GM_GROUPS

    def gm_fwd_body(z_ref, lng_ref, lnb_ref, ws_ref, bias_ref, y_ref):
        u, _, _, vnb = gm_recompute(z_ref, lng_ref, lnb_ref)
        for g in range(GM_GROUPS):
            sl = slice(g * gw, (g + 1) * gw)
            sv = _dot(ws_ref[g], vnb[:, sl]) + bias_ref[:, sl]
            y_ref[:, sl] = (u[:, sl] * sv).astype(BF16)

    y_gm, g_kv, g_o = _rowcall("gmlp_fwd", gm_fwd_body, n, GM_CHUNK,
                               [(z_gm, 'row'), (row(W['gm_ln_g']), 'full'), (row(W['gm_ln_b']), 'full'), (w_s, 'full'),
                                (gm_bias, 'full')], [(d, BF16)], gather=[shard_b['ca_w_kv'], shard_b['ca_w_o']])
    w_kv_f, w_o_f = assemble('ca_w_kv', g_kv), assemble('ca_w_o', g_o)

    tc = seq // SCAN_CHUNKS
    lt = S5_GROUPS * S5_STATE

    def to_scan_order(t):
        return t.reshape(nb, SCAN_CHUNKS, tc, t.shape[-1]).transpose(0, 2, 1, 3).reshape(n, t.shape[-1])

    def from_scan_order(t):
        return t.reshape(nb, tc, SCAN_CHUNKS, t.shape[-1]).transpose(0, 2, 1, 3).reshape(n, t.shape[-1])

    u5p = to_scan_order(u5)
    kq, nq = 4 * S5_CH * 2, 4 * S5_STATE * 2
    n_blk = dh // kq

    def s5_in_body(u_ref, br_ref, bi_ref, or_ref, oi_ref):
        ub = u_ref[...].astype(BF16)
        for q in range(n_blk):
            uq = ub[:, q * kq:(q + 1) * kq]
            or_ref[:, q * nq:(q + 1) * nq] = _dot(uq, br_ref[q * kq:(q + 1) * kq, q * nq:(q + 1) * nq])
            oi_ref[:, q * nq:(q + 1) * nq] = _dot(uq, bi_ref[q * kq:(q + 1) * kq, q * nq:(q + 1) * nq])

    bu_re, bu_im, g_gu0 = _rowcall("s5_in", s5_in_body, n, tm, [(u5p, 'row'), (bb_re_b, 'full'), (bb_im_b, 'full')],
                                   [(lt, F32), (lt, F32)], gather=[gu_halves[0]])
    shape4 = (nb, tc, SCAN_CHUNKS, lt)
    s_re4, s_im4 = _scan("s5_scan_fwd", bu_re.reshape(shape4), bu_im.reshape(shape4), a_re, a_im, False)
    s_re, s_im = s_re4.reshape(n, lt), s_im4.reshape(n, lt)

    def s5_out_body(sr_ref, si_ref, u_ref, cr_ref, ci_ref, d_ref, wg_ref, ypre_ref, gate_ref, y_ref):
        srb, sib = sr_ref[...].astype(BF16), si_ref[...].astype(BF16)
        for q in range(n_blk):
            rs, cs = slice(q * nq, (q + 1) * nq), slice(q * kq, (q + 1) * kq)
            yq = _dot(srb[:, rs], cr_ref[rs, cs]) - _dot(sib[:, rs], ci_ref[rs, cs])
            ypre_ref[:, cs] = yq + d_ref[:, cs] * u_ref[:, cs]
        yg = _gelu(ypre_ref[...])
        gate = _dot(yg.astype(BF16), wg_ref[...])
        gate_ref[...] = gate
        y_ref[...] = (yg * _sigmoid(gate)).astype(BF16)

    y_pre, gate, y_s5p, g_gu1 = _rowcall(
        "s5_out", s5_out_body, n, tm,
        [(s_re, 'row'), (s_im, 'row'), (u5p, 'row'), (cc_re_b, 'full'), (cc_im_b, 'full'), (s5_d_row, 'full'),
         (w_glu_f, 'full')], [(dh, F32), (dh, F32), (dh, BF16)], gather=[gu_halves[1]])
    y_s5 = from_scan_order(y_s5p)
    w_gu_f = jnp.concatenate([assemble('ffn_w_gu', g_gu0), assemble('ffn_w_gu', g_gu1)], axis=0)

    def merge_body(ygm_ref, ys5_ref, zga_ref, zgb_ref, wa_ref, wb_ref, pa_ref, pb_ref, mg_ref):
        pa = _dot(ygm_ref[...], wa_ref[...])
        pb = _dot(ys5_ref[...], wb_ref[...])
        pa_ref[...] = pa
        pb_ref[...] = pb
        mg_ref[...] = (_sigmoid(zga_ref[...]) * pa + _sigmoid(zgb_ref[...]) * pb).astype(BF16)

    p_a, p_b, merged, g_down = _rowcall(
        "merge", merge_body, n, tm,
        [(y_gm, 'row'), (y_s5, 'row'), (z_ga, 'row'), (z_gb, 'row'), (w_brgm_f, 'full'), (w_brs5_f, 'full')],
        [(d, F32), (d, F32), (d, BF16)], gather=[shard_b['ffn_w_down']])
    w_down_f = assemble('ffn_w_down', g_down)

    def close_sublayer(name, a_in, w_out, x_res, g_post, g_next, w_next=None):
        def body(*refs):
            a_ref, w_ref, x_ref, gp_ref, gn_ref = refs[:5]
            rest = refs[5:]
            if w_next is not None:
                wn_ref, rest = rest[0], rest[1:]
            o_ref, xo_ref, h_ref = rest[:3]
            o = _dot(a_ref[...], w_ref[...])
            o_ref[...] = o
            xo = x_ref[...] + _rms(o, gp_ref[...])
            xo_ref[...] = xo
            hb = _rms(xo, gn_ref[...]).astype(BF16)
            h_ref[...] = hb
            if w_next is not None:
                rest[3][...] = _dot(hb, wn_ref[...]).astype(BF16)

        ins = [(a_in, 'row'), (w_out, 'full'), (x_res, 'row'), (row(g_post), 'full'), (row(g_next), 'full')]
        outs = [(d, F32), (d, F32), (d, BF16)]
        if w_next is not None:
            ins.append((w_next, 'full'))
            outs.append((w_next.shape[1], BF16))
        return _rowcall(name, body, n, tm, ins, outs)

    o1, x1, hc, q = close_sublayer("mix_out", merged, w_mix_f, x2d, W['g_mix_post'], W['g_ca_pre'], w_q_f)

    tmm = min(ROW_TILE, nb * nmem)

    def memkv_body(m_ref, g_ref, w_ref, mn_ref, k_ref, v_ref):
        mnb = _rms(m_ref[...], g_ref[...]).astype(BF16)
        mn_ref[...] = mnb
        k_ref[...] = _dot(mnb, w_ref[:, :d]).astype(BF16)
        v_ref[...] = _dot(mnb, w_ref[:, d:]).astype(BF16)

    mem_n, k_mem, v_mem = _rowcall("mem_kv", memkv_body, nb * nmem, tmm,
                                   [(mem2d, 'row'), (row(W['g_mem']), 'full'), (w_kv_f, 'full')],
                                   [(d, BF16), (d, BF16), (d, BF16)])

    tiles_per_ex = seq // tm
    kv_spec = pl.BlockSpec((nmem, d), lambda i: (i // tiles_per_ex, 0))
    scale = hd ** -0.5

    def softmax_rows(qh, kh):
        s = _dot_nt(qh, kh) * scale
        e = jnp.exp(s - jnp.max(s, axis=-1, keepdims=True))
        return e / jnp.sum(e, axis=-1, keepdims=True)

    def attn_body(q_ref, k_ref, v_ref, o_ref):
        for h in range(CA_HEADS):
            sl = slice(h * hd, (h + 1) * hd)
            p = softmax_rows(q_ref[:, sl], k_ref[:, sl])
            o_ref[:, sl] = _dot(p.astype(BF16), v_ref[:, sl]).astype(BF16)

    (att,) = _rowcall("attn_fwd", attn_body, n, tm, [(q, 'row'), (k_mem, kv_spec), (v_mem, kv_spec)], [(d, BF16)])

    o2, x2, hf = close_sublayer("attn_out", att, w_o_f, x1, W['g_ca_post'], W['g_ffn_pre'])

    ck = 256

    def ffn_up_body(h_ref, w_ref, gu_ref, a_ref):
        hb = h_ref[...]
        for lo in range(0, ffh, ck):
            gt = _dot(hb, w_ref[:, lo:lo + ck])
            ut = _dot(hb, w_ref[:, ffh + lo:ffh + lo + ck])
            gu_ref[:, lo:lo + ck] = gt
            gu_ref[:, ffh + lo:ffh + lo + ck] = ut
            a_ref[:, lo:lo + ck] = ((gt * _sigmoid(gt)) * ut).astype(BF16)

    gu, act = _rowcall("ffn_up", ffn_up_body, n, tm, [(hf, 'row'), (w_gu_f, 'full')], [(2 * ffh, F32), (ffh, BF16)])

    def ffn_down_body(a_ref, w_ref, x_ref, t_ref, g_ref, dx_ref, do_ref, loss_ref, dg_ref):
        i = pl.program_id(0)

        @pl.when(i == 0)
        def _():
            loss_ref[...] = jnp.zeros_like(loss_ref)
            dg_ref[...] = jnp.zeros_like(dg_ref)

        o = _dot(a_ref[...], w_ref[...])
        diff = x_ref[...] + _rms(o, g_ref[...]) - t_ref[...]
        loss_ref[...] += jnp.full(loss_ref.shape, 0.5 / d, F32) * jnp.sum(diff * diff)
        dx = diff * (1.0 / d)
        dx_ref[...] = dx
        do, dg = _rms_bwd(o, g_ref[...], dx)
        do_ref[...] = do.astype(BF16)
        dg_ref[...] += dg

    dx3, do3, loss_part, dg_ffn_post = _rowcall(
        "ffn_down_loss", ffn_down_body, n, tm,
        [(act, 'row'), (w_down_f, 'full'), (x2, 'row'), (tgt, 'row'), (row(W['g_ffn_post']), 'full')],
        [(d, F32), (d, BF16)], accs=[((1, 128), F32), ((1, d), F32)])

    G = {'g_ffn_post': dg_ffn_post}
    RECV = {}

    def parts_of(k, gfull):
        r, c = W[k].shape
        return gfull.reshape(N_DEV, r, c) if SHARDED[k] == 'row' else gfull.reshape(r, N_DEV, c).transpose(1, 0, 2)

    p_down = parts_of('ffn_w_down', _dwcall("dw_ffn_down", act, do3, BF16))

    def ffn_act_bwd_body(do_ref, w_ref, gu_ref, dgu_ref):
        dob = do_ref[...]
        for lo in range(0, ffh, ck):
            da = _dot_nt(dob, w_ref[lo:lo + ck, :])
            gt = gu_ref[:, lo:lo + ck]
            ut = gu_ref[:, ffh + lo:ffh + lo + ck]
            sg = _sigmoid(gt)
            dgu_ref[:, lo:lo + ck] = (da * ut * (sg * (1.0 + gt * (1.0 - sg)))).astype(BF16)
            dgu_ref[:, ffh + lo:ffh + lo + ck] = (da * (gt * sg)).astype(BF16)

    dgu, RECV['ffn_w_down'] = _rowcall("ffn_act_bwd", ffn_act_bwd_body, n, tm,
                                       [(do3, 'row'), (w_down_f, 'full'), (gu, 'row')], [(2 * ffh, BF16)],
                                       scatter=[p_down])
    p_gu = parts_of('ffn_w_gu', _dwcall("dw_ffn_gu", hf, dgu, BF16))

    def open_sublayer(name, pieces, w_full, x_in, g_pre, dx_up, o_prev=None, g_post_prev=None, scatter=()):
        n_p = len(pieces)
        second = o_prev is not None

        def body(*refs):
            dc_refs, (w_ref, x_ref, g_ref, dxu_ref), rest = refs[:n_p], refs[n_p:n_p + 4], refs[n_p + 4:]
            if second:
                (op_ref, gp_ref), rest = rest[:2], rest[2:]
            i = pl.program_id(0)
            dhid = None
            for dc_ref, (_, lo, hi) in zip(dc_refs, pieces):
                part = _dot_nt(dc_ref[...], w_ref[:, lo:hi])
                dhid = part if dhid is None else dhid + part
            dxn, dg = _rms_bwd(x_ref[...], g_ref[...], dhid)
            dx = dxu_ref[...] + dxn
            if second:
                dx_ref, do_ref, dg_ref, dg2_ref = rest
            else:
                dx_ref, dg_ref = rest

            @pl.when(i == 0)
            def _():
                dg_ref[...] = jnp.zeros_like(dg_ref)
                if second:
                    dg2_ref[...] = jnp.zeros_like(dg2_ref)

            dx_ref[...] = dx
            dg_ref[...] += dg
            if second:
                do, dg2 = _rms_bwd(op_ref[...], gp_ref[...], dx)
                do_ref[...] = do.astype(BF16)
                dg2_ref[...] += dg2

        ins = [(p[0], 'row') for p in pieces] + [(w_full, 'full'), (x_in, 'row'), (row(g_pre), 'full'), (dx_up, 'row')]
        outs = [(d, F32)]
        accs = [((1, d), F32)]
        if second:
            ins += [(o_prev, 'row'), (row(g_post_prev), 'full')]
            outs.append((d, BF16))
            accs.append(((1, d), F32))
        res = _rowcall(name, body, n, tm, ins, outs, accs=accs, scatter=scatter)
        if second:
            dx, do, dg, dg2 = res[:4]
            return dx, dg, do, dg2, res[4:]
        return res[0], res[1], res[2:]

    dx2, G['g_ffn_pre'], do2, G['g_ca_post'], (RECV['ffn_w_gu'],) = open_sublayer(
        "ffn_in_bwd", [(dgu, 0, 2 * ffh)], w_gu_f, x2, W['g_ffn_pre'], dx3, o2, W['g_ca_post'], scatter=[p_gu])
    p_o = parts_of('ca_w_o', _dwcall("dw_ca_o", att, do2, BF16))

    def attn_bwd_body(q_ref, k_ref, v_ref, do_ref, wo_ref, dq_ref, dk_ref, dv_ref):
        i = pl.program_id(0)

        @pl.when(i % tiles_per_ex == 0)
        def _():
            dk_ref[...] = jnp.zeros_like(dk_ref)
            dv_ref[...] = jnp.zeros_like(dv_ref)

        d_att = _dot_nt(do_ref[...], wo_ref[...]).astype(BF16)
        for h in range(CA_HEADS):
            sl = slice(h * hd, (h + 1) * hd)
            qh, kh, vh, dah = q_ref[:, sl], k_ref[:, sl], v_ref[:, sl], d_att[:, sl]
            p = softmax_rows(qh, kh)
            dp = _dot_nt(dah, vh)
            ds = (p * (dp - jnp.sum(p * dp, axis=-1, keepdims=True)) * scale).astype(BF16)
            dq_ref[:, sl] = _dot(ds, kh).astype(BF16)
            dk_ref[:, sl] += _dot_tn(ds, qh)
            dv_ref[:, sl] += _dot_tn(p.astype(BF16), dah)

    kv_acc = ((nb * nmem, d), (F32, pl.BlockSpec((nmem, d), lambda i: (i // tiles_per_ex, 0))))
    dq, dk_mem, dv_mem, RECV['ca_w_o'] = _rowcall(
        "attn_bwd", attn_bwd_body, n, tm,
        [(q, 'row'), (k_mem, kv_spec), (v_mem, kv_spec), (do2, 'row'), (w_o_f, 'full')], [(d, BF16)],
        accs=[kv_acc, kv_acc], scatter=[p_o])
    p_q = parts_of('ca_w_q', _dwcall("dw_ca_q", hc, dq, BF16))

    def memkv_bwd_body(dk_ref, dv_ref, m_ref, g_ref, w_ref, dkv_ref, dg_ref):
        i = pl.program_id(0)

        @pl.when(i == 0)
        def _():
            dg_ref[...] = jnp.zeros_like(dg_ref)

        dkb, dvb = dk_ref[...].astype(BF16), dv_ref[...].astype(BF16)
        dkv_ref[:, :d] = dkb
        dkv_ref[:, d:] = dvb
        dmn = _dot_nt(dkb, w_ref[:, :d]) + _dot_nt(dvb, w_ref[:, d:])
        _, dg = _rms_bwd(m_ref[...], g_ref[...], dmn)
        dg_ref[...] += dg

    dkv, G['g_mem'] = _rowcall(
        "mem_kv_bwd", memkv_bwd_body, nb * nmem, tmm,
        [(dk_mem, 'row'), (dv_mem, 'row'), (mem2d, 'row'), (row(W['g_mem']), 'full'), (w_kv_f, 'full')],
        [(d2, BF16)], accs=[((1, d), F32)])
    p_kv = parts_of('ca_w_kv', _dwcall("dw_ca_kv", mem_n, dkv, BF16))

    dx1, G['g_ca_pre'], do1, G['g_mix_post'], (RECV['ca_w_q'], RECV['ca_w_kv']) = open_sublayer(
        "attn_in_bwd", [(dq, 0, d)], w_q_f, x1, W['g_ca_pre'], dx2, o1, W['g_mix_post'], scatter=[p_q, p_kv])
    p_mix = parts_of('w_mix_out', _dwcall("dw_mix_out", merged, do1, BF16))

    def merge_bwd_body(do_ref, wm_ref, zga_ref, zgb_ref, pa_ref, pb_ref, wb_ref, dpa_ref, dpb_ref, dza_ref, dzb_ref,
                       dys_ref):
        dm = _dot_nt(do_ref[...], wm_ref[...])
        sa, sb = _sigmoid(zga_ref[...]), _sigmoid(zgb_ref[...])
        dpb = (dm * sb).astype(BF16)
        dpa_ref[...] = (dm * sa).astype(BF16)
        dpb_ref[...] = dpb
        dza_ref[...] = (dm * pa_ref[...] * (sa * (1.0 - sa))).astype(BF16)
        dzb_ref[...] = (dm * pb_ref[...] * (sb * (1.0 - sb))).astype(BF16)
        dys_ref[...] = _dot_nt(dpb, wb_ref[...]).astype(BF16)

    dp_a, dp_b, dz_ga, dz_gb, dy_s5, RECV['w_mix_out'] = _rowcall(
        "merge_bwd", merge_bwd_body, n, tm,
        [(do1, 'row'), (w_mix_f, 'full'), (z_ga, 'row'), (z_gb, 'row'), (p_a, 'row'), (p_b, 'row'), (w_brs5_f, 'full')],
        [(d, BF16), (d, BF16), (d, BF16), (d, BF16), (dh, BF16)], scatter=[p_mix])
    p_brgm = parts_of('w_br_gm', _dwcall("dw_br_gm", y_gm, dp_a, BF16))
    p_brs5 = parts_of('w_br_s5', _dwcall("dw_br_s5", y_s5, dp_b, BF16))

    def gm_bwd_body(z_ref, dpa_ref, wa_ref, lng_ref, lnb_ref, ws_ref, wst_ref, bias_ref, dz_ref, dws_ref, dbias_ref,
                    dlng_ref, dlnb_ref, du_s, dvn_s):
        i = pl.program_id(0)

        @pl.when(i == 0)
        def _():
            dws_ref[...] = jnp.zeros_like(dws_ref)
            dbias_ref[...] = jnp.zeros_like(dbias_ref)
            dlng_ref[...] = jnp.zeros_like(dlng_ref)
            dlnb_ref[...] = jnp.zeros_like(dlnb_ref)

        u, vhat, rstd, vnb = gm_recompute(z_ref, lng_ref, lnb_ref)
        dy = _dot_nt(dpa_ref[...], wa_ref[...])
        for g in range(GM_GROUPS):
            sl = slice(g * gw, (g + 1) * gw)
            sv = _dot(ws_ref[g], vnb[:, sl]) + bias_ref[:, sl]
            du_s[:, sl] = dy[:, sl] * sv
            dsv = dy[:, sl] * u[:, sl]
            dsvb = dsv.astype(BF16)
            dvn_s[:, sl] = _dot(wst_ref[g], dsvb)
            dws_ref[g] += _dot_nt(dsvb, vnb[:, sl])
            dbias_ref[:, sl] += dsv
        dvn = dvn_s[...]
        dlng_ref[...] += jnp.sum(dvn * vhat, axis=0, keepdims=True)
        dlnb_ref[...] += jnp.sum(dvn, axis=0, keepdims=True)
        dvh = dvn * lng_ref[...]
        dv = rstd * (dvh - jnp.mean(dvh, axis=-1, keepdims=True) - vhat * jnp.mean(dvh * vhat, axis=-1, keepdims=True))
        dz_ref[:, :d] = (du_s[...] * _gelu_grad(z_ref[:, :d])).astype(BF16)
        dz_ref[:, d:] = (dv * _gelu_grad(z_ref[:, d:])).astype(BF16)

    dz_gm, dws_full, dbias_full, G['gm_ln_g'], G['gm_ln_b'], RECV['w_br_gm'], RECV['w_br_s5'] = _rowcall(
        "gmlp_bwd", gm_bwd_body, n, GM_CHUNK,
        [(z_gm, 'row'), (dp_a, 'row'), (w_brgm_f, 'full'), (row(W['gm_ln_g']), 'full'), (row(W['gm_ln_b']), 'full'),
         (w_s, 'full'), (w_s_t, 'full'), (gm_bias, 'full')],
        [(d2, BF16)], accs=[((GM_GROUPS, GM_CHUNK, GM_CHUNK), F32), ((GM_CHUNK, d), F32), ((1, d), F32), ((1, d), F32)],
        scratch=[pltpu.VMEM((GM_CHUNK, d), F32), pltpu.VMEM((GM_CHUNK, d), F32)], scatter=[p_brgm, p_brs5])
    G['gm_w_s'] = jnp.where(tril[None], dws_full, 0.0)
    G['gm_b_s'] = dbias_full.reshape(GM_CHUNK, GM_GROUPS, gw).sum(-1).T

    dy_s5p = to_scan_order(dy_s5)

    def s5_out_bwd_body(dy_ref, ypre_ref, gate_ref, u_ref, wg_ref, cr_ref, ci_ref, dyp_ref, dgate_ref, yg_ref, gr_ref,
                        gi_ref, dd_ref):
        i = pl.program_id(0)

        @pl.when(i == 0)
        def _():
            dd_ref[...] = jnp.zeros_like(dd_ref)

        dy = dy_ref[...].astype(F32)
        ypre = ypre_ref[...]
        yg = _gelu(ypre)
        sg = _sigmoid(gate_ref[...])
        dgb = (dy * yg * (sg * (1.0 - sg))).astype(BF16)
        dgate_ref[...] = dgb
        yg_ref[...] = yg.astype(BF16)
        dyp = (dy * sg + _dot_nt(dgb, wg_ref[...])) * _gelu_grad(ypre)
        dyp_ref[...] = dyp
        dd_ref[...] += jnp.sum(dyp * u_ref[...], axis=0, keepdims=True)
        dypb = dyp.astype(BF16)
        for q in range(n_blk):
            rs, cs = slice(q * nq, (q + 1) * nq), slice(q * kq, (q + 1) * kq)
            gr_ref[:, rs] = _dot_nt(dypb[:, cs], cr_ref[rs, cs])
            gi_ref[:, rs] = -_dot_nt(dypb[:, cs], ci_ref[rs, cs])

    dy_pre, dgate, yg_b, gin_re, gin_im, dd = _rowcall(
        "s5_out_bwd", s5_out_bwd_body, n, tm,
        [(dy_s5p, 'row'), (y_pre, 'row'), (gate, 'row'), (u5p, 'row'), (w_glu_f, 'full'), (cc_re_b, 'full'),
         (cc_im_b, 'full')],
        [(dh, F32), (dh, BF16), (dh, BF16), (lt, F32), (lt, F32)], accs=[((1, dh), F32)])
    p_glu = parts_of('s5_w_glu', _dwcall("dw_s5_glu", yg_b, dgate, BF16))
    d_cc_re = _dwcall("dw_s5_c_re", s_re, dy_pre, F32)
    d_cc_im = -_dwcall("dw_s5_c_im", s_im, dy_pre, F32)

    gs_re4, gs_im4, da_re, da_im = _scan("s5_scan_bwd", gin_re.reshape(shape4), gin_im.reshape(shape4), a_re, -a_im,
                                         True, s_re4, s_im4)
    gs_re, gs_im = gs_re4.reshape(n, lt), gs_im4.reshape(n, lt)
    d_bb_re = _dwcall("dw_s5_b_re", u5p, gs_re, F32)
    d_bb_im = _dwcall("dw_s5_b_im", u5p, gs_im, F32)

    def s5_in_bwd_body(gr_ref, gi_ref, dyp_ref, br_ref, bi_ref, d_ref, du_ref):
        grb, gib = gr_ref[...].astype(BF16), gi_ref[...].astype(BF16)
        for q in range(n_blk):
            rs, cs = slice(q * kq, (q + 1) * kq), slice(q * nq, (q + 1) * nq)
            du = _dot_nt(grb[:, cs], br_ref[rs, cs]) + _dot_nt(gib[:, cs], bi_ref[rs, cs])
            du_ref[:, rs] = (du + d_ref[:, rs] * dyp_ref[:, rs]).astype(BF16)

    du5p, RECV['s5_w_glu'] = _rowcall("s5_in_bwd", s5_in_bwd_body, n, tm,
                                      [(gs_re, 'row'), (gs_im, 'row'), (dy_pre, 'row'), (bb_re_b, 'full'),
                                       (bb_im_b, 'full'), (s5_d_row, 'full')], [(dh, BF16)], scatter=[p_glu])
    du5 = from_scan_order(du5p)
    s5_grads = s5_vjp((jnp.sum(da_re, axis=0), jnp.sum(da_im, axis=0), d_bb_re, d_bb_im, d_cc_re, d_cc_im))
    for k, gval in zip(['s5_lam_re', 's5_lam_im', 's5_log_step', 's5_b_re', 's5_b_im', 's5_c_re', 's5_c_im'], s5_grads):
        G[k] = gval
    G['s5_d'] = dd

    pieces = [(dz_gm, 0, d2), (du5, d2, d2 + dh), (dz_ga, d2 + dh, d2 + dh + d), (dz_gb, d2 + dh + d, d2 + dh + 2 * d)]
    p_in = parts_of('w_in', jnp.concatenate(
        [_dwcall("dw_in_%d" % j, h0, p[0], BF16) for j, p in enumerate(pieces)], axis=1))
    grad_x2d, G['g_mix_pre'], (RECV['w_in'],) = open_sublayer("in_proj_bwd", pieces, w_in_f, x2d, W['g_mix_pre'], dx1,
                                                              scatter=[p_in])

    out = {k: _sum_adamw("sum_adamw_" + k, RECV[k], W[k], M[k], V[k]) for k in names}

    def pack(vals, extra=None):
        flat = [vals[k].reshape(-1).astype(F32) for k in SMALL]
        if extra is not None:
            flat.append(extra)
        flat = jnp.concatenate(flat)
        pad = (-flat.shape[0]) % 1024
        return jnp.pad(flat, (0, pad)).reshape(-1, 128)

    one = jnp.zeros((1,), F32)
    gp, dp, nmp, nvp = _allreduce_adamw(pack(G, loss_part[0, :1]), pack(W, one), pack(M, one), pack(V, one + 1.0))
    off = 0
    small_out = {}
    for k in SMALL:
        sz = W[k].size
        small_out[k] = tuple(t.reshape(-1)[off:off + sz].reshape(W[k].shape) for t in (gp, dp, nmp, nvp))
        off += sz
    loss = gp.reshape(-1)[off]
    out.update(small_out)

    res = [loss, grad_x2d.reshape(x.shape)]
    for j in range(4):
        res += [out[k][j][None] for k in WEIGHTS]
    return tuple(res)
```

```python
import functools
import math

import jax
import jax.numpy as jnp
from jax import lax
from jax.experimental import pallas as pl
from jax.experimental.pallas import tpu as pltpu

F32 = jnp.float32
BF16 = jnp.bfloat16
EPS = 1e-6
N_DEV = 8
V7X_VMEM_LIMIT = 56 * 1024 * 1024
ROW_TILE = 256
GM_CHUNK = 128
GM_GROUPS = 8
S5_GROUPS = 32
S5_STATE = 64
S5_CH = 16
SCAN_CHUNKS = 32
SCAN_LANES = 128
CA_HEADS = 4
ADAM_LR, ADAM_B1, ADAM_B2, ADAM_EPS, ADAM_WD, ADAM_STEP = 0.001, 0.9, 0.999, 1e-08, 0.01, 10

WEIGHTS = ['g_mix_pre', 'w_in', 'gm_ln_g', 'gm_ln_b', 'gm_w_s', 'gm_b_s', 's5_lam_re', 's5_lam_im', 's5_log_step',
           's5_b_re', 's5_b_im', 's5_c_re', 's5_c_im', 's5_d', 's5_w_glu', 'w_br_gm', 'w_br_s5', 'w_mix_out',
           'g_mix_post', 'g_ca_pre', 'g_mem', 'ca_w_q', 'ca_w_kv', 'ca_w_o', 'g_ca_post', 'g_ffn_pre', 'ffn_w_gu',
           'ffn_w_down', 'g_ffn_post']
SHARDED = {'w_in': 'col', 's5_w_glu': 'row', 'w_br_gm': 'row', 'w_br_s5': 'col', 'w_mix_out': 'row',
           'ca_w_q': 'row', 'ca_w_kv': 'col', 'ca_w_o': 'row', 'ffn_w_gu': 'col', 'ffn_w_down': 'row'}
SMALL = [n for n in WEIGHTS if n not in SHARDED]


def _rms(x, g):
    r = lax.rsqrt(jnp.mean(x * x, axis=-1, keepdims=True) + EPS)
    return (x * r) * g


def _rms_bwd(x, g, dy):
    r = lax.rsqrt(jnp.mean(x * x, axis=-1, keepdims=True) + EPS)
    n = x * r
    dn = dy * g
    dx = r * (dn - n * jnp.mean(dn * n, axis=-1, keepdims=True))
    return dx, jnp.sum(dy * n, axis=0, keepdims=True)


_GELU_C = math.sqrt(2.0 / math.pi)


def _gelu(x):
    return 0.5 * x * (1.0 + jnp.tanh(_GELU_C * (x + 0.044715 * (x * x * x))))


def _gelu_grad(x):
    t = jnp.tanh(_GELU_C * (x + 0.044715 * (x * x * x)))
    return 0.5 * (1.0 + t) + 0.5 * x * (1.0 - t * t) * (_GELU_C * (1.0 + 3.0 * 0.044715 * (x * x)))


def _sigmoid(x):
    return 1.0 / (1.0 + jnp.exp(-x))


def _dot(a, b):
    return jnp.dot(a, b, preferred_element_type=F32)


def _dot_nt(a, b):
    return lax.dot_general(a, b, (((1,), (1,)), ((), ())), preferred_element_type=F32)


def _dot_tn(a, b):
    return lax.dot_general(a, b, (((0,), (0,)), ((), ())), preferred_element_type=F32)


def _adamw(w, g, m, v):
    m = ADAM_B1 * m + (1.0 - ADAM_B1) * g
    v = ADAM_B2 * v + (1.0 - ADAM_B2) * (g * g)
    m_hat = m / (1.0 - ADAM_B1 ** ADAM_STEP)
    v_hat = v / (1.0 - ADAM_B2 ** ADAM_STEP)
    delta = -ADAM_LR * (m_hat / (jnp.sqrt(v_hat) + ADAM_EPS) + ADAM_WD * w)
    return delta, m, v


def _params(n_grid):
    return pltpu.CompilerParams(dimension_semantics=("arbitrary",) * n_grid, vmem_limit_bytes=V7X_VMEM_LIMIT)


def _my_place():
    x, y, c = lax.axis_index("x"), lax.axis_index("y"), lax.axis_index("c")
    return x, y, c


def _peer(x, y, c, k):
    px = 1 - x if k & 4 else x
    py = 1 - y if k & 2 else y
    pc = 1 - c if k & 1 else c
    return (px, py, pc), 4 * px + 2 * py + pc


def _exchange(kind, src_refs, dst_refs, sems, first, phase):
    x, y, c = _my_place()
    me = 4 * x + 2 * y + c
    send_sems, recv_sems, own_sems = sems
    for j, (src, dst) in enumerate(zip(src_refs, dst_refs), start=first):
        own = pltpu.make_async_copy(src if kind == 'gather' else src.at[me], dst.at[me], own_sems.at[j])
        if phase == 'start':
            own.start()
        for k in range(1, N_DEV):
            peer, peer_block = _peer(x, y, c, k)
            out = pltpu.make_async_remote_copy(
                src_ref=src if kind == 'gather' else src.at[peer_block], dst_ref=dst.at[me],
                send_sem=send_sems.at[7 * j + k - 1], recv_sem=recv_sems.at[7 * j + k - 1], device_id=peer,
                device_id_type=pl.DeviceIdType.MESH)
            if phase == 'start':
                out.start()
            else:
                pltpu.make_async_remote_copy(
                    src_ref=src if kind == 'gather' else src.at[peer_block], dst_ref=dst.at[peer_block],
                    send_sem=send_sems.at[7 * j + k - 1], recv_sem=recv_sems.at[7 * j + k - 1], device_id=peer,
                    device_id_type=pl.DeviceIdType.MESH).wait_recv()
                out.wait_send()
        if phase == 'wait':
            own.wait()


def _rowcall(name, body, n_rows, tm, ins, outs, accs=(), scratch=(), gather=(), scatter=()):
    n_steps = n_rows // tm
    moved = [('gather', a) for a in gather] + [('scatter', a) for a in scatter]
    if moved:
        n_in, n_out, n_mv, n_scr = len(ins), len(outs) + len(accs), len(moved), len(scratch)
        inner = body

        def body(*refs):
            mv_src = refs[n_in:n_in + n_mv]
            mv_dst = refs[n_in + n_mv + n_out:n_in + 2 * n_mv + n_out]
            sems = refs[n_in + 2 * n_mv + n_out + n_scr:]
            i = pl.program_id(0)

            def exchange(phase):
                for kind, lo, hi in (('gather', 0, len(gather)), ('scatter', len(gather), n_mv)):
                    if hi > lo:
                        _exchange(kind, mv_src[lo:hi], mv_dst[lo:hi], sems, lo, phase)

            pl.when(i == 0)(functools.partial(exchange, 'start'))
            inner(*refs[:n_in], *refs[n_in + n_mv:n_in + n_mv + n_out],
                  *refs[n_in + 2 * n_mv + n_out:n_in + 2 * n_mv + n_out + n_scr])
            pl.when(i == n_steps - 1)(functools.partial(exchange, 'wait'))

    arrays, in_specs = [], []
    for a, kind in ins:
        arrays.append(a)
        if kind == 'row':
            in_specs.append(pl.BlockSpec((tm,) + a.shape[1:], lambda i, nd=a.ndim: (i,) + (0,) * (nd - 1)))
        elif kind == 'full':
            in_specs.append(pl.BlockSpec(a.shape, lambda i, nd=a.ndim: (0,) * nd))
        else:
            in_specs.append(kind)
    out_shape, out_specs = [], []
    for cols, dt in outs:
        out_shape.append(jax.ShapeDtypeStruct((n_rows, cols), dt))
        out_specs.append(pl.BlockSpec((tm, cols), lambda i: (i, 0)))
    for shp, dt in accs:
        if isinstance(dt, tuple):
            dt, spec = dt
        else:
            spec = pl.BlockSpec(shp, lambda i, nd=len(shp): (0,) * nd)
        out_shape.append(jax.ShapeDtypeStruct(shp, dt))
        out_specs.append(spec)
    scratch = list(scratch)
    for kind, a in moved:
        arrays.append(a)
        in_specs.append(pl.BlockSpec(memory_space=pl.ANY))
        out_shape.append(jax.ShapeDtypeStruct((N_DEV,) + a.shape if kind == 'gather' else a.shape, a.dtype))
        out_specs.append(pl.BlockSpec(memory_space=pl.ANY))
    if moved:
        scratch += [pltpu.SemaphoreType.DMA((7 * len(moved),)), pltpu.SemaphoreType.DMA((7 * len(moved),)),
                    pltpu.SemaphoreType.DMA((len(moved),))]
    return pl.pallas_call(functools.partial(body), grid=(n_steps,), in_specs=in_specs, out_specs=out_specs,
                          out_shape=out_shape, scratch_shapes=scratch, name=name, compiler_params=_params(1))(*arrays)


def _dw_tiles(n, ka, nn, a_itemsize):
    tn = max(t for t in range(128, min(nn, 1536) + 1, 128) if nn % t == 0)
    tm = min(n, 2048)
    while tm > 256 and tm * ka * a_itemsize > 6 * 1024 * 1024:
        tm //= 2
    return tm, tn


def _dwcall(name, a, dc, out_dtype):
    n, ka = a.shape
    nn = dc.shape[1]
    tm, tn = _dw_tiles(n, ka, nn, a.dtype.itemsize)
    n_i = n // tm

    def body(a_ref, dc_ref, o_ref, acc_ref):
        i = pl.program_id(1)

        @pl.when(i == 0)
        def _():
            acc_ref[...] = jnp.zeros_like(acc_ref)

        acc_ref[...] += _dot_tn(a_ref[...].astype(BF16), dc_ref[...].astype(BF16))

        @pl.when(i == n_i - 1)
        def _():
            o_ref[...] = acc_ref[...].astype(o_ref.dtype)

    return pl.pallas_call(
        body, grid=(nn // tn, n_i),
        in_specs=[pl.BlockSpec((tm, ka), lambda j, i: (i, 0)), pl.BlockSpec((tm, tn), lambda j, i: (i, j))],
        out_specs=pl.BlockSpec((ka, tn), lambda j, i: (0, j)),
        out_shape=jax.ShapeDtypeStruct((ka, nn), out_dtype),
        scratch_shapes=[pltpu.VMEM((ka, tn), F32)], name=name, compiler_params=_params(2))(a, dc)


def _first_gather(first, later):
    n = len(later)
    vm = pl.BlockSpec(memory_space=pltpu.VMEM)

    def body(*refs):
        first_ref, later_refs = refs[0], refs[1:1 + n]
        out_ref, cast_refs = refs[1 + n], refs[2 + n:2 + 2 * n]
        stage = refs[2 + 2 * n]
        sems = refs[3 + 2 * n:]
        stage[...] = first_ref[...].astype(BF16)
        _exchange('gather', [stage], [out_ref], sems, 0, 'start')
        for src, dst in zip(later_refs, cast_refs):
            dst[...] = src[...].astype(BF16)
        _exchange('gather', [stage], [out_ref], sems, 0, 'wait')

    return pl.pallas_call(
        body, out_shape=[jax.ShapeDtypeStruct((N_DEV,) + first.shape, BF16)]
        + [jax.ShapeDtypeStruct(s.shape, BF16) for s in later],
        in_specs=[vm] * (1 + n), out_specs=[pl.BlockSpec(memory_space=pl.ANY)] + [vm] * n,
        scratch_shapes=[pltpu.VMEM(first.shape, BF16), pltpu.SemaphoreType.DMA((7,)), pltpu.SemaphoreType.DMA((7,)),
                        pltpu.SemaphoreType.DMA((1,))],
        name="gather_first", compiler_params=pltpu.CompilerParams(vmem_limit_bytes=V7X_VMEM_LIMIT))(first, *later)


def _sum_adamw(name, recv, w, m, v):
    shp = w.shape

    def body(recv_ref, w_ref, m_ref, v_ref, g_ref, d_ref, nm_ref, nv_ref):
        g = recv_ref[0].astype(F32)
        for k in range(1, N_DEV):
            g = g + recv_ref[k].astype(F32)
        d, nm, nv = _adamw(w_ref[...], g, m_ref[...], v_ref[...])
        g_ref[...] = g
        d_ref[...] = d
        nm_ref[...] = nm
        nv_ref[...] = nv

    vm = pl.BlockSpec(memory_space=pltpu.VMEM)
    return pl.pallas_call(
        body, out_shape=[jax.ShapeDtypeStruct(shp, F32)] * 4, in_specs=[vm] * 4, out_specs=[vm] * 4,
        name=name, compiler_params=pltpu.CompilerParams(vmem_limit_bytes=V7X_VMEM_LIMIT))(recv, w, m, v)


def _allreduce_adamw(grads, ws, ms, vs, loss_part):
    n = len(grads)

    def body(*refs):
        g_in, loss_in = refs[:n], refs[n]
        w_refs, m_refs, v_refs = refs[n + 1:2 * n + 1], refs[2 * n + 1:3 * n + 1], refs[3 * n + 1:4 * n + 1]
        outs = refs[4 * n + 1:8 * n + 1]
        loss_out = refs[8 * n + 1]
        recv = refs[8 * n + 2:9 * n + 3]
        sems = refs[9 * n + 3:]
        src = [g.at[0] for g in g_in] + [loss_in.at[0]]
        _exchange('gather', src, recv, sems, 0, 'start')
        _exchange('gather', src, recv, sems, 0, 'wait')

        def total(r, rank):
            if rank == 1:
                t = r[0:1, :]
                for k in range(1, N_DEV):
                    t = t + r[k:k + 1, :]
                return t
            t = r[0]
            for k in range(1, N_DEV):
                t = t + r[k]
            return t[None]

        for j in range(n):
            g = total(recv[j], len(grads[j].shape) - 1)
            d, nm, nv = _adamw(w_refs[j][...], g, m_refs[j][...], v_refs[j][...])
            for ref, val in zip(outs[4 * j:4 * j + 4], (g, d, nm, nv)):
                ref[...] = val
        loss_out[...] = total(recv[n], 1)

    vm = pl.BlockSpec(memory_space=pltpu.VMEM)
    items = list(grads) + [loss_part]
    out_shape = []
    for g in grads:
        out_shape += [jax.ShapeDtypeStruct(g.shape, F32)] * 4
    out_shape.append(jax.ShapeDtypeStruct(loss_part.shape, F32))
    res = pl.pallas_call(
        body, out_shape=out_shape, in_specs=[vm] * (4 * n + 1), out_specs=[vm] * (4 * n + 1),
        scratch_shapes=[pltpu.VMEM((N_DEV,) + t.shape[1:], F32) for t in items]
        + [pltpu.SemaphoreType.DMA((7 * (n + 1),)), pltpu.SemaphoreType.DMA((7 * (n + 1),)),
           pltpu.SemaphoreType.DMA((n + 1,))],
        name="allreduce_small_adamw",
        compiler_params=pltpu.CompilerParams(vmem_limit_bytes=V7X_VMEM_LIMIT))(*grads, loss_part, *ws, *ms, *vs)
    return [res[4 * j:4 * j + 4] for j in range(n)], res[4 * n]


def _s5_tables(lam_re, lam_im, log_step, b_re, b_im, c_re, c_im):
    g = lam_re.shape[0]
    step = jnp.exp(log_step)[:, None]
    mag = jnp.exp(lam_re * step)
    ab_re = mag * jnp.cos(lam_im * step)
    ab_im = mag * jnp.sin(lam_im * step)
    den = lam_re * lam_re + lam_im * lam_im
    nr = ab_re - 1.0
    co_re = (nr * lam_re + ab_im * lam_im) / den
    co_im = (ab_im * lam_re - nr * lam_im) / den
    bb_re = co_re[..., None] * b_re - co_im[..., None] * b_im
    bb_im = co_re[..., None] * b_im + co_im[..., None] * b_re
    per = 8
    eye = jnp.eye(per, dtype=F32)

    def blockdiag(t):
        t = t.reshape(g // per, per, t.shape[1], t.shape[2])
        return (t[:, :, :, None, :] * eye[None, :, None, :, None]).reshape(g // per, per * t.shape[2], per * t.shape[3])

    return (ab_re.reshape(1, -1), ab_im.reshape(1, -1), blockdiag(bb_re.transpose(0, 2, 1)),
            blockdiag(bb_im.transpose(0, 2, 1)), blockdiag(c_re.transpose(0, 2, 1)), blockdiag(c_im.transpose(0, 2, 1)))


def _scan(name, xr, xi, a_re, a_im, reverse, s_re=None, s_im=None):
    nb, tc, nc, lt = xr.shape
    ln = SCAN_LANES
    n_sq = int(math.log2(tc))
    assert 2 ** n_sq == tc
    with_da = s_re is not None

    def body(*refs):
        if with_da:
            xr_ref, xi_ref, ar_ref, ai_ref, sr_ref, si_ref, or_ref, oi_ref, dar_ref, dai_ref, cr_ref, ci_ref = refs
        else:
            xr_ref, xi_ref, ar_ref, ai_ref, or_ref, oi_ref, cr_ref, ci_ref = refs
        ar = jnp.broadcast_to(ar_ref[...], (nc, ln))
        ai = jnp.broadcast_to(ai_ref[...], (nc, ln))
        zero = jnp.zeros((nc, ln), F32)

        def at(t):
            return tc - 1 - t if reverse else t

        def local(t, carry):
            sr, si = carry
            j = at(t)
            nr = ar * sr - ai * si + xr_ref[0, j]
            ni = ar * si + ai * sr + xi_ref[0, j]
            or_ref[0, j] = nr
            oi_ref[0, j] = ni
            return nr, ni

        lr, li = lax.fori_loop(0, tc, local, (zero, zero))
        pr, pi = ar_ref[...], ai_ref[...]
        for _ in range(n_sq):
            pr, pi = pr * pr - pi * pi, 2.0 * (pr * pi)
        cr_ref[...] = lr
        ci_ref[...] = li
        tr = jnp.zeros((1, ln), F32)
        ti = jnp.zeros((1, ln), F32)
        for c in (range(nc - 1, -1, -1) if reverse else range(nc)):
            l_r = cr_ref[c:c + 1, :]
            l_i = ci_ref[c:c + 1, :]
            cr_ref[c:c + 1, :] = tr
            ci_ref[c:c + 1, :] = ti
            tr, ti = pr * tr - pi * ti + l_r, pr * ti + pi * tr + l_i

        def fix(j, qr, qi):
            nqr = ar * qr - ai * qi
            nqi = ar * qi + ai * qr
            gr = or_ref[0, j] + nqr
            gi = oi_ref[0, j] + nqi
            or_ref[0, j] = gr
            oi_ref[0, j] = gi
            return nqr, nqi, gr, gi

        if not with_da:
            def fixup(t, carry):
                qr, qi, _, _ = fix(at(t), *carry)
                return qr, qi

            lax.fori_loop(0, tc, fixup, (cr_ref[...], ci_ref[...]))
        else:
            def fixup(t, carry):
                qr, qi, dr, di = carry
                j = at(t)
                qr, qi, gr, gi = fix(j, qr, qi)
                pr_, pi_ = sr_ref[0, j - 1], si_ref[0, j - 1]
                return qr, qi, dr + (pr_ * gr + pi_ * gi), di + (pr_ * gi - pi_ * gr)

            qr, qi, dr, di = lax.fori_loop(0, tc - 1, fixup, (cr_ref[...], ci_ref[...], zero, zero))
            _, _, gr, gi = fix(0, qr, qi)
            row = lax.broadcasted_iota(jnp.int32, (nc, ln), 0)
            pr_ = jnp.where(row == 0, 0.0, pltpu.roll(sr_ref[0, tc - 1], 1, 0))
            pi_ = jnp.where(row == 0, 0.0, pltpu.roll(si_ref[0, tc - 1], 1, 0))
            dr = dr + (pr_ * gr + pi_ * gi)
            di = di + (pr_ * gi - pi_ * gr)
            dar_ref[0] = jnp.sum(dr, axis=0, keepdims=True)
            dai_ref[0] = jnp.sum(di, axis=0, keepdims=True)

    blk = pl.BlockSpec((1, tc, nc, ln), lambda b, l: (b, 0, 0, l))
    a_spec = pl.BlockSpec((1, ln), lambda b, l: (0, l))
    ins = [xr, xi, a_re, a_im] + ([s_re, s_im] if with_da else [])
    in_specs = [blk, blk, a_spec, a_spec] + ([blk, blk] if with_da else [])
    out_shape = [jax.ShapeDtypeStruct(xr.shape, F32)] * 2
    out_specs = [blk, blk]
    if with_da:
        out_shape += [jax.ShapeDtypeStruct((nb, 1, lt), F32)] * 2
        out_specs += [pl.BlockSpec((1, 1, ln), lambda b, l: (b, 0, l))] * 2
    return pl.pallas_call(body, grid=(nb, lt // ln), in_specs=in_specs, out_specs=out_specs, out_shape=out_shape,
                          scratch_shapes=[pltpu.VMEM((nc, ln), F32)] * 2, name=name, compiler_params=_params(2))(*ins)


def kernel(x, mem, g_mix_pre, w_in, gm_ln_g, gm_ln_b, gm_w_s, gm_b_s, s5_lam_re, s5_lam_im, s5_log_step, s5_b_re, s5_b_im, s5_c_re, s5_c_im, s5_d, s5_w_glu, w_br_gm, w_br_s5, w_mix_out, g_mix_post, g_ca_pre, g_mem, ca_w_q, ca_w_kv, ca_w_o, g_ca_post, g_ffn_pre, ffn_w_gu, ffn_w_down, g_ffn_post, loss_target, m_g_mix_pre, m_w_in, m_gm_ln_g, m_gm_ln_b, m_gm_w_s, m_gm_b_s, m_s5_lam_re, m_s5_lam_im, m_s5_log_step, m_s5_b_re, m_s5_b_im, m_s5_c_re, m_s5_c_im, m_s5_d, m_s5_w_glu, m_w_br_gm, m_w_br_s5, m_w_mix_out, m_g_mix_post, m_g_ca_pre, m_g_mem, m_ca_w_q, m_ca_w_kv, m_ca_w_o, m_g_ca_post, m_g_ffn_pre, m_ffn_w_gu, m_ffn_w_down, m_g_ffn_post, v_g_mix_pre, v_w_in, v_gm_ln_g, v_gm_ln_b, v_gm_w_s, v_gm_b_s, v_s5_lam_re, v_s5_lam_im, v_s5_log_step, v_s5_b_re, v_s5_b_im, v_s5_c_re, v_s5_c_im, v_s5_d, v_s5_w_glu, v_w_br_gm, v_w_br_s5, v_w_mix_out, v_g_mix_post, v_g_ca_pre, v_g_mem, v_ca_w_q, v_ca_w_kv, v_ca_w_o, v_g_ca_post, v_g_ffn_pre, v_ffn_w_gu, v_ffn_w_down, v_g_ffn_post):
    args = locals()
    W = {n: args[n][0] for n in WEIGHTS}
    M = {n: args['m_' + n][0] for n in WEIGHTS}
    V = {n: args['v_' + n][0] for n in WEIGHTS}

    nb, seq, d = x.shape
    n = nb * seq
    tm = min(ROW_TILE, n)
    nmem = mem.shape[1]
    d2, dh = 2 * d, d // 2
    hd = d // CA_HEADS
    x2d = x.reshape(n, d)
    tgt = loss_target.reshape(n, d)
    mem2d = mem.reshape(nb * nmem, d)

    def row(v):
        return v.reshape(1, -1)

    names = list(SHARDED)
    later = [k for k in names if k != 'w_in']
    first_g, *casts = _first_gather(W['w_in'], [W[k] for k in later])
    shard_b = dict(zip(later, casts))
    half = shard_b['ffn_w_gu'].shape[0] // 2
    gu_halves = [shard_b['ffn_w_gu'][:half], shard_b['ffn_w_gu'][half:]]

    def assemble(k, gth):
        r, c = gth.shape[1:]
        return gth.reshape(N_DEV * r, c) if SHARDED[k] == 'row' else gth.transpose(1, 0, 2).reshape(r, N_DEV * c)

    w_in_f = assemble('w_in', first_g)
    ffh = N_DEV * W['ffn_w_down'].shape[0]

    s5_in = (W['s5_lam_re'], W['s5_lam_im'], W['s5_log_step'], W['s5_b_re'], W['s5_b_im'], W['s5_c_re'], W['s5_c_im'])
    (a_re, a_im, bb_re, bb_im, cc_re, cc_im), s5_vjp = jax.vjp(_s5_tables, *s5_in)
    bb_re_b, bb_im_b, cc_re_b, cc_im_b = (t.astype(BF16) for t in (bb_re, bb_im, cc_re, cc_im))
    s5_d_row = row(W['s5_d'])
    tril = jnp.tril(jnp.ones((GM_CHUNK, GM_CHUNK), bool))
    w_s = jnp.where(tril[None], W['gm_w_s'], 0.0).astype(BF16)
    w_s_t = w_s.transpose(0, 2, 1)
    gm_bias = jnp.repeat(W['gm_b_s'].T, d // GM_GROUPS, axis=1)

    def in_proj_body(x_ref, g_ref, w_ref, zgm_ref, u5_ref, zga_ref, zgb_ref, h_ref):
        hb = _rms(x_ref[...], g_ref[...]).astype(BF16)
        h_ref[...] = hb
        for lo in range(0, w_ref.shape[1], 512):
            acc = _dot(hb, w_ref[:, lo:lo + 512])
            if lo < d2:
                zgm_ref[:, lo:lo + 512] = acc.astype(BF16)
            elif lo < d2 + dh:
                u5_ref[...] = acc
            elif lo < d2 + dh + d:
                zga_ref[:, lo - d2 - dh:lo - d2 - dh + 512] = acc.astype(BF16)
            else:
                zgb_ref[:, lo - d2 - dh - d:lo - d2 - dh - d + 512] = acc.astype(BF16)

    ride = ['s5_w_glu', 'w_br_gm', 'w_br_s5', 'w_mix_out', 'ca_w_q']
    z_gm, u5, z_ga, z_gb, h0, *got = _rowcall(
        "in_proj", in_proj_body, n, tm, [(x2d, 'row'), (row(W['g_mix_pre']), 'full'), (w_in_f, 'full')],
        [(d2, BF16), (dh, F32), (d, BF16), (d, BF16), (d, BF16)], gather=[shard_b[k] for k in ride])
    w_glu_f, w_brgm_f, w_brs5_f, w_mix_f, w_q_f = (assemble(k, g) for k, g in zip(ride, got))

    def gm_recompute(z_ref, lng_ref, lnb_ref):
        zg = _gelu(z_ref[...].astype(F32))
        u, v = zg[:, :d], zg[:, d:]
        vc = v - jnp.mean(v, axis=-1, keepdims=True)
        rstd = lax.rsqrt(jnp.mean(vc * vc, axis=-1, keepdims=True) + EPS)
        vhat = vc * rstd
        vn = vhat * lng_ref[...] + lnb_ref[...]
        return u, vhat, rstd, vn.astype(BF16)

    gw = d // GM_GROUPS

    def gm_fwd_body(z_ref, lng_ref, lnb_ref, ws_ref, bias_ref, y_ref):
        u, _, _, vnb = gm_recompute(z_ref, lng_ref, lnb_ref)
        for g in range(GM_GROUPS):
            sl = slice(g * gw, (g + 1) * gw)
            sv = _dot(ws_ref[g], vnb[:, sl]) + bias_ref[:, sl]
            y_ref[:, sl] = (u[:, sl] * sv).astype(BF16)

    y_gm, g_kv, g_o = _rowcall("gmlp_fwd", gm_fwd_body, n, GM_CHUNK,
                               [(z_gm, 'row'), (row(W['gm_ln_g']), 'full'), (row(W['gm_ln_b']), 'full'), (w_s, 'full'),
                                (gm_bias, 'full')], [(d, BF16)], gather=[shard_b['ca_w_kv'], shard_b['ca_w_o']])
    w_kv_f, w_o_f = assemble('ca_w_kv', g_kv), assemble('ca_w_o', g_o)

    tc = seq // SCAN_CHUNKS
    lt = S5_GROUPS * S5_STATE

    def to_scan_order(t):
        return t.reshape(nb, SCAN_CHUNKS, tc, t.shape[-1]).transpose(0, 2, 1, 3).reshape(n, t.shape[-1])

    def from_scan_order(t):
        return t.reshape(nb, tc, SCAN_CHUNKS, t.shape[-1]).transpose(0, 2, 1, 3).reshape(n, t.shape[-1])

    u5p = to_scan_order(u5)
    kq, nq = 4 * S5_CH * 2, 4 * S5_STATE * 2
    n_blk = dh // kq

    def s5_in_body(u_ref, br_ref, bi_ref, or_ref, oi_ref):
        ub = u_ref[...].astype(BF16)
        for q in range(n_blk):
            uq = ub[:, q * kq:(q + 1) * kq]
            or_ref[:, q * nq:(q + 1) * nq] = _dot(uq, br_ref[q])
            oi_ref[:, q * nq:(q + 1) * nq] = _dot(uq, bi_ref[q])

    bu_re, bu_im, g_gu0 = _rowcall("s5_in", s5_in_body, n, tm, [(u5p, 'row'), (bb_re_b, 'full'), (bb_im_b, 'full')],
                                   [(lt, F32), (lt, F32)], gather=[gu_halves[0]])
    shape4 = (nb, tc, SCAN_CHUNKS, lt)
    s_re4, s_im4 = _scan("s5_scan_fwd", bu_re.reshape(shape4), bu_im.reshape(shape4), a_re, a_im, False)
    s_re, s_im = s_re4.reshape(n, lt), s_im4.reshape(n, lt)

    def s5_out_body(sr_ref, si_ref, u_ref, cr_ref, ci_ref, d_ref, wg_ref, ypre_ref, gate_ref, y_ref):
        srb, sib = sr_ref[...].astype(BF16), si_ref[...].astype(BF16)
        for q in range(n_blk):
            rs, cs = slice(q * nq, (q + 1) * nq), slice(q * kq, (q + 1) * kq)
            yq = _dot(srb[:, rs], cr_ref[q]) - _dot(sib[:, rs], ci_ref[q])
            ypre_ref[:, cs] = yq + d_ref[:, cs] * u_ref[:, cs]
        yg = _gelu(ypre_ref[...])
        gate = _dot(yg.astype(BF16), wg_ref[...])
        gate_ref[...] = gate
        y_ref[...] = (yg * _sigmoid(gate)).astype(BF16)

    y_pre, gate, y_s5p, g_gu1 = _rowcall(
        "s5_out", s5_out_body, n, tm,
        [(s_re, 'row'), (s_im, 'row'), (u5p, 'row'), (cc_re_b, 'full'), (cc_im_b, 'full'), (s5_d_row, 'full'),
         (w_glu_f, 'full')], [(dh, F32), (dh, F32), (dh, BF16)], gather=[gu_halves[1]])
    y_s5 = from_scan_order(y_s5p)
    w_gu_f = jnp.concatenate([assemble('ffn_w_gu', g_gu0), assemble('ffn_w_gu', g_gu1)], axis=0)

    def merge_body(ygm_ref, ys5_ref, zga_ref, zgb_ref, wa_ref, wb_ref, pa_ref, pb_ref, mg_ref):
        pa = _dot(ygm_ref[...], wa_ref[...])
        pb = _dot(ys5_ref[...], wb_ref[...])
        pa_ref[...] = pa.astype(BF16)
        pb_ref[...] = pb.astype(BF16)
        mg_ref[...] = (_sigmoid(zga_ref[...].astype(F32)) * pa + _sigmoid(zgb_ref[...].astype(F32)) * pb).astype(BF16)

    p_a, p_b, merged, g_down = _rowcall(
        "merge", merge_body, n, tm,
        [(y_gm, 'row'), (y_s5, 'row'), (z_ga, 'row'), (z_gb, 'row'), (w_brgm_f, 'full'), (w_brs5_f, 'full')],
        [(d, BF16), (d, BF16), (d, BF16)], gather=[shard_b['ffn_w_down']])
    w_down_f = assemble('ffn_w_down', g_down)

    def close_sublayer(name, a_in, w_out, x_res, g_post, g_next, w_next=None):
        def body(*refs):
            a_ref, w_ref, x_ref, gp_ref, gn_ref = refs[:5]
            rest = refs[5:]
            if w_next is not None:
                wn_ref, rest = rest[0], rest[1:]
            o_ref, xo_ref, h_ref = rest[:3]
            o = _dot(a_ref[...], w_ref[...])
            o_ref[...] = o.astype(BF16)
            xo = x_ref[...] + _rms(o, gp_ref[...])
            xo_ref[...] = xo
            hb = _rms(xo, gn_ref[...]).astype(BF16)
            h_ref[...] = hb
            if w_next is not None:
                rest[3][...] = _dot(hb, wn_ref[...]).astype(BF16)

        ins = [(a_in, 'row'), (w_out, 'full'), (x_res, 'row'), (row(g_post), 'full'), (row(g_next), 'full')]
        outs = [(d, BF16), (d, F32), (d, BF16)]
        if w_next is not None:
            ins.append((w_next, 'full'))
            outs.append((w_next.shape[1], BF16))
        return _rowcall(name, body, n, tm, ins, outs)

    o1, x1, hc, q = close_sublayer("mix_out", merged, w_mix_f, x2d, W['g_mix_post'], W['g_ca_pre'], w_q_f)

    tmm = min(ROW_TILE, nb * nmem)

    def memkv_body(m_ref, g_ref, w_ref, mn_ref, k_ref, v_ref):
        mnb = _rms(m_ref[...], g_ref[...]).astype(BF16)
        mn_ref[...] = mnb
        k_ref[...] = _dot(mnb, w_ref[:, :d]).astype(BF16)
        v_ref[...] = _dot(mnb, w_ref[:, d:]).astype(BF16)

    mem_n, k_mem, v_mem = _rowcall("mem_kv", memkv_body, nb * nmem, tmm,
                                   [(mem2d, 'row'), (row(W['g_mem']), 'full'), (w_kv_f, 'full')],
                                   [(d, BF16), (d, BF16), (d, BF16)])

    tiles_per_ex = seq // tm
    kv_spec = pl.BlockSpec((nmem, d), lambda i: (i // tiles_per_ex, 0))
    scale = hd ** -0.5

    def softmax_rows(qh, kh):
        s = _dot_nt(qh, kh) * scale
        e = jnp.exp(s - jnp.max(s, axis=-1, keepdims=True))
        return e / jnp.sum(e, axis=-1, keepdims=True)

    def attn_body(q_ref, k_ref, v_ref, o_ref):
        for h in range(CA_HEADS):
            sl = slice(h * hd, (h + 1) * hd)
            p = softmax_rows(q_ref[:, sl], k_ref[:, sl])
            o_ref[:, sl] = _dot(p.astype(BF16), v_ref[:, sl]).astype(BF16)

    (att,) = _rowcall("attn_fwd", attn_body, n, tm, [(q, 'row'), (k_mem, kv_spec), (v_mem, kv_spec)], [(d, BF16)])

    o2, x2, hf = close_sublayer("attn_out", att, w_o_f, x1, W['g_ca_post'], W['g_ffn_pre'])

    ck = 256

    def ffn_up_body(h_ref, w_ref, gu_ref, a_ref):
        hb = h_ref[...]
        for lo in range(0, ffh, ck):
            gt = _dot(hb, w_ref[:, lo:lo + ck])
            ut = _dot(hb, w_ref[:, ffh + lo:ffh + lo + ck])
            gu_ref[:, lo:lo + ck] = gt.astype(BF16)
            gu_ref[:, ffh + lo:ffh + lo + ck] = ut.astype(BF16)
            a_ref[:, lo:lo + ck] = ((gt * _sigmoid(gt)) * ut).astype(BF16)

    gu, act = _rowcall("ffn_up", ffn_up_body, n, tm, [(hf, 'row'), (w_gu_f, 'full')], [(2 * ffh, BF16), (ffh, BF16)])

    def ffn_down_body(a_ref, w_ref, x_ref, t_ref, g_ref, dx_ref, do_ref, loss_ref, dg_ref):
        i = pl.program_id(0)

        @pl.when(i == 0)
        def _():
            loss_ref[...] = jnp.zeros_like(loss_ref)
            dg_ref[...] = jnp.zeros_like(dg_ref)

        o = _dot(a_ref[...], w_ref[...])
        diff = x_ref[...] + _rms(o, g_ref[...]) - t_ref[...]
        loss_ref[...] += jnp.full(loss_ref.shape, 0.5 / d, F32) * jnp.sum(diff * diff)
        dx = diff * (1.0 / d)
        dx_ref[...] = dx
        do, dg = _rms_bwd(o, g_ref[...], dx)
        do_ref[...] = do.astype(BF16)
        dg_ref[...] += dg

    dx3, do3, loss_part, dg_ffn_post = _rowcall(
        "ffn_down_loss", ffn_down_body, n, tm,
        [(act, 'row'), (w_down_f, 'full'), (x2, 'row'), (tgt, 'row'), (row(W['g_ffn_post']), 'full')],
        [(d, F32), (d, BF16)], accs=[((1, 128), F32), ((1, d), F32)])

    G = {'g_ffn_post': dg_ffn_post}
    RECV = {}

    def parts_of(k, gfull):
        r, c = W[k].shape
        return gfull.reshape(N_DEV, r, c) if SHARDED[k] == 'row' else gfull.reshape(r, N_DEV, c).transpose(1, 0, 2)

    p_down = parts_of('ffn_w_down', _dwcall("dw_ffn_down", act, do3, BF16))

    def ffn_act_bwd_body(do_ref, w_ref, gu_ref, dgu_ref):
        dob = do_ref[...]
        for lo in range(0, ffh, ck):
            da = _dot_nt(dob, w_ref[lo:lo + ck, :])
            gt = gu_ref[:, lo:lo + ck].astype(F32)
            ut = gu_ref[:, ffh + lo:ffh + lo + ck].astype(F32)
            sg = _sigmoid(gt)
            dgu_ref[:, lo:lo + ck] = (da * ut * (sg * (1.0 + gt * (1.0 - sg)))).astype(BF16)
            dgu_ref[:, ffh + lo:ffh + lo + ck] = (da * (gt * sg)).astype(BF16)

    dgu, RECV['ffn_w_down'] = _rowcall("ffn_act_bwd", ffn_act_bwd_body, n, tm,
                                       [(do3, 'row'), (w_down_f, 'full'), (gu, 'row')], [(2 * ffh, BF16)],
                                       scatter=[p_down])
    p_gu = parts_of('ffn_w_gu', _dwcall("dw_ffn_gu", hf, dgu, BF16))

    def open_sublayer(name, pieces, w_full, x_in, g_pre, dx_up, o_prev=None, g_post_prev=None, scatter=()):
        n_p = len(pieces)
        second = o_prev is not None

        def body(*refs):
            dc_refs, (w_ref, x_ref, g_ref, dxu_ref), rest = refs[:n_p], refs[n_p:n_p + 4], refs[n_p + 4:]
            if second:
                (op_ref, gp_ref), rest = rest[:2], rest[2:]
            i = pl.program_id(0)
            dhid = None
            for dc_ref, (_, lo, hi) in zip(dc_refs, pieces):
                part = _dot_nt(dc_ref[...], w_ref[:, lo:hi])
                dhid = part if dhid is None else dhid + part
            dxn, dg = _rms_bwd(x_ref[...], g_ref[...], dhid)
            dx = dxu_ref[...] + dxn
            if second:
                dx_ref, do_ref, dg_ref, dg2_ref = rest
            else:
                dx_ref, dg_ref = rest

            @pl.when(i == 0)
            def _():
                dg_ref[...] = jnp.zeros_like(dg_ref)
                if second:
                    dg2_ref[...] = jnp.zeros_like(dg2_ref)

            dx_ref[...] = dx
            dg_ref[...] += dg
            if second:
                do, dg2 = _rms_bwd(op_ref[...].astype(F32), gp_ref[...], dx)
                do_ref[...] = do.astype(BF16)
                dg2_ref[...] += dg2

        ins = [(p[0], 'row') for p in pieces] + [(w_full, 'full'), (x_in, 'row'), (row(g_pre), 'full'), (dx_up, 'row')]
        outs = [(d, F32)]
        accs = [((1, d), F32)]
        if second:
            ins += [(o_prev, 'row'), (row(g_post_prev), 'full')]
            outs.append((d, BF16))
            accs.append(((1, d), F32))
        res = _rowcall(name, body, n, tm, ins, outs, accs=accs, scatter=scatter)
        if second:
            dx, do, dg, dg2 = res[:4]
            return dx, dg, do, dg2, res[4:]
        return res[0], res[1], res[2:]

    dx2, G['g_ffn_pre'], do2, G['g_ca_post'], (RECV['ffn_w_gu'],) = open_sublayer(
        "ffn_in_bwd", [(dgu, 0, 2 * ffh)], w_gu_f, x2, W['g_ffn_pre'], dx3, o2, W['g_ca_post'], scatter=[p_gu])
    p_o = parts_of('ca_w_o', _dwcall("dw_ca_o", att, do2, BF16))

    def attn_bwd_body(q_ref, k_ref, v_ref, do_ref, wo_ref, dq_ref, dk_ref, dv_ref):
        i = pl.program_id(0)

        @pl.when(i % tiles_per_ex == 0)
        def _():
            dk_ref[...] = jnp.zeros_like(dk_ref)
            dv_ref[...] = jnp.zeros_like(dv_ref)

        d_att = _dot_nt(do_ref[...], wo_ref[...]).astype(BF16)
        for h in range(CA_HEADS):
            sl = slice(h * hd, (h + 1) * hd)
            qh, kh, vh, dah = q_ref[:, sl], k_ref[:, sl], v_ref[:, sl], d_att[:, sl]
            p = softmax_rows(qh, kh)
            dp = _dot_nt(dah, vh)
            ds = (p * (dp - jnp.sum(p * dp, axis=-1, keepdims=True)) * scale).astype(BF16)
            dq_ref[:, sl] = _dot(ds, kh).astype(BF16)
            dk_ref[:, sl] += _dot_tn(ds, qh)
            dv_ref[:, sl] += _dot_tn(p.astype(BF16), dah)

    kv_acc = ((nb * nmem, d), (F32, pl.BlockSpec((nmem, d), lambda i: (i // tiles_per_ex, 0))))
    dq, dk_mem, dv_mem, RECV['ca_w_o'] = _rowcall(
        "attn_bwd", attn_bwd_body, n, tm,
        [(q, 'row'), (k_mem, kv_spec), (v_mem, kv_spec), (do2, 'row'), (w_o_f, 'full')], [(d, BF16)],
        accs=[kv_acc, kv_acc], scatter=[p_o])
    p_q = parts_of('ca_w_q', _dwcall("dw_ca_q", hc, dq, BF16))

    def memkv_bwd_body(dk_ref, dv_ref, m_ref, g_ref, w_ref, dkv_ref, dg_ref):
        i = pl.program_id(0)

        @pl.when(i == 0)
        def _():
            dg_ref[...] = jnp.zeros_like(dg_ref)

        dkb, dvb = dk_ref[...].astype(BF16), dv_ref[...].astype(BF16)
        dkv_ref[:, :d] = dkb
        dkv_ref[:, d:] = dvb
        dmn = _dot_nt(dkb, w_ref[:, :d]) + _dot_nt(dvb, w_ref[:, d:])
        _, dg = _rms_bwd(m_ref[...], g_ref[...], dmn)
        dg_ref[...] += dg

    dkv, G['g_mem'] = _rowcall(
        "mem_kv_bwd", memkv_bwd_body, nb * nmem, tmm,
        [(dk_mem, 'row'), (dv_mem, 'row'), (mem2d, 'row'), (row(W['g_mem']), 'full'), (w_kv_f, 'full')],
        [(d2, BF16)], accs=[((1, d), F32)])
    p_kv = parts_of('ca_w_kv', _dwcall("dw_ca_kv", mem_n, dkv, BF16))

    dx1, G['g_ca_pre'], do1, G['g_mix_post'], (RECV['ca_w_q'], RECV['ca_w_kv']) = open_sublayer(
        "attn_in_bwd", [(dq, 0, d)], w_q_f, x1, W['g_ca_pre'], dx2, o1, W['g_mix_post'], scatter=[p_q, p_kv])
    p_mix = parts_of('w_mix_out', _dwcall("dw_mix_out", merged, do1, BF16))

    def merge_bwd_body(do_ref, wm_ref, zga_ref, zgb_ref, pa_ref, pb_ref, wb_ref, dpa_ref, dpb_ref, dza_ref, dzb_ref,
                       dys_ref):
        dm = _dot_nt(do_ref[...], wm_ref[...])
        sa, sb = _sigmoid(zga_ref[...].astype(F32)), _sigmoid(zgb_ref[...].astype(F32))
        dpb = (dm * sb).astype(BF16)
        dpa_ref[...] = (dm * sa).astype(BF16)
        dpb_ref[...] = dpb
        dza_ref[...] = (dm * pa_ref[...].astype(F32) * (sa * (1.0 - sa))).astype(BF16)
        dzb_ref[...] = (dm * pb_ref[...].astype(F32) * (sb * (1.0 - sb))).astype(BF16)
        dys_ref[...] = _dot_nt(dpb, wb_ref[...]).astype(BF16)

    dp_a, dp_b, dz_ga, dz_gb, dy_s5, RECV['w_mix_out'] = _rowcall(
        "merge_bwd", merge_bwd_body, n, tm,
        [(do1, 'row'), (w_mix_f, 'full'), (z_ga, 'row'), (z_gb, 'row'), (p_a, 'row'), (p_b, 'row'), (w_brs5_f, 'full')],
        [(d, BF16), (d, BF16), (d, BF16), (d, BF16), (dh, BF16)], scatter=[p_mix])
    p_brgm = parts_of('w_br_gm', _dwcall("dw_br_gm", y_gm, dp_a, BF16))
    p_brs5 = parts_of('w_br_s5', _dwcall("dw_br_s5", y_s5, dp_b, BF16))

    def gm_bwd_body(z_ref, dpa_ref, wa_ref, lng_ref, lnb_ref, ws_ref, wst_ref, bias_ref, dz_ref, dws_ref, dbias_ref,
                    dlng_ref, dlnb_ref, du_s, dvn_s):
        i = pl.program_id(0)

        @pl.when(i == 0)
        def _():
            dws_ref[...] = jnp.zeros_like(dws_ref)
            dbias_ref[...] = jnp.zeros_like(dbias_ref)
            dlng_ref[...] = jnp.zeros_like(dlng_ref)
            dlnb_ref[...] = jnp.zeros_like(dlnb_ref)

        u, vhat, rstd, vnb = gm_recompute(z_ref, lng_ref, lnb_ref)
        dy = _dot_nt(dpa_ref[...], wa_ref[...])
        for g in range(GM_GROUPS):
            sl = slice(g * gw, (g + 1) * gw)
            sv = _dot(ws_ref[g], vnb[:, sl]) + bias_ref[:, sl]
            du_s[:, sl] = dy[:, sl] * sv
            dsv = dy[:, sl] * u[:, sl]
            dsvb = dsv.astype(BF16)
            dvn_s[:, sl] = _dot(wst_ref[g], dsvb)
            dws_ref[g] += _dot_nt(dsvb, vnb[:, sl])
            dbias_ref[:, sl] += dsv
        dvn = dvn_s[...]
        dlng_ref[...] += jnp.sum(dvn * vhat, axis=0, keepdims=True)
        dlnb_ref[...] += jnp.sum(dvn, axis=0, keepdims=True)
        dvh = dvn * lng_ref[...]
        dv = rstd * (dvh - jnp.mean(dvh, axis=-1, keepdims=True) - vhat * jnp.mean(dvh * vhat, axis=-1, keepdims=True))
        dz_ref[:, :d] = (du_s[...] * _gelu_grad(z_ref[:, :d].astype(F32))).astype(BF16)
        dz_ref[:, d:] = (dv * _gelu_grad(z_ref[:, d:].astype(F32))).astype(BF16)

    dz_gm, dws_full, dbias_full, G['gm_ln_g'], G['gm_ln_b'], RECV['w_br_gm'], RECV['w_br_s5'] = _rowcall(
        "gmlp_bwd", gm_bwd_body, n, GM_CHUNK,
        [(z_gm, 'row'), (dp_a, 'row'), (w_brgm_f, 'full'), (row(W['gm_ln_g']), 'full'), (row(W['gm_ln_b']), 'full'),
         (w_s, 'full'), (w_s_t, 'full'), (gm_bias, 'full')],
        [(d2, BF16)], accs=[((GM_GROUPS, GM_CHUNK, GM_CHUNK), F32), ((GM_CHUNK, d), F32), ((1, d), F32), ((1, d), F32)],
        scratch=[pltpu.VMEM((GM_CHUNK, d), F32), pltpu.VMEM((GM_CHUNK, d), F32)], scatter=[p_brgm, p_brs5])
    G['gm_w_s'] = jnp.where(tril[None], dws_full, 0.0)
    G['gm_b_s'] = dbias_full.reshape(GM_CHUNK, GM_GROUPS, gw).sum(-1).T

    dy_s5p = to_scan_order(dy_s5)

    def s5_out_bwd_body(dy_ref, ypre_ref, gate_ref, u_ref, sr_ref, si_ref, wg_ref, cr_ref, ci_ref, dyp_ref, dgate_ref,
                        yg_ref, gr_ref, gi_ref, dd_ref, dcr_ref, dci_ref):
        i = pl.program_id(0)

        @pl.when(i == 0)
        def _():
            dd_ref[...] = jnp.zeros_like(dd_ref)
            dcr_ref[...] = jnp.zeros_like(dcr_ref)
            dci_ref[...] = jnp.zeros_like(dci_ref)

        dy = dy_ref[...].astype(F32)
        ypre = ypre_ref[...]
        yg = _gelu(ypre)
        sg = _sigmoid(gate_ref[...])
        dgb = (dy * yg * (sg * (1.0 - sg))).astype(BF16)
        dgate_ref[...] = dgb
        yg_ref[...] = yg.astype(BF16)
        dyp = (dy * sg + _dot_nt(dgb, wg_ref[...])) * _gelu_grad(ypre)
        dyp_ref[...] = dyp
        dd_ref[...] += jnp.sum(dyp * u_ref[...], axis=0, keepdims=True)
        dypb = dyp.astype(BF16)
        for q in range(n_blk):
            rs, cs = slice(q * nq, (q + 1) * nq), slice(q * kq, (q + 1) * kq)
            gr_ref[:, rs] = _dot_nt(dypb[:, cs], cr_ref[q])
            gi_ref[:, rs] = -_dot_nt(dypb[:, cs], ci_ref[q])
            dcr_ref[q] += _dot_tn(sr_ref[:, rs].astype(BF16), dypb[:, cs])
            dci_ref[q] -= _dot_tn(si_ref[:, rs].astype(BF16), dypb[:, cs])

    dy_pre, dgate, yg_b, gin_re, gin_im, dd, d_cc_re, d_cc_im = _rowcall(
        "s5_out_bwd", s5_out_bwd_body, n, tm,
        [(dy_s5p, 'row'), (y_pre, 'row'), (gate, 'row'), (u5p, 'row'), (s_re, 'row'), (s_im, 'row'), (w_glu_f, 'full'),
         (cc_re_b, 'full'), (cc_im_b, 'full')],
        [(dh, F32), (dh, BF16), (dh, BF16), (lt, F32), (lt, F32)],
        accs=[((1, dh), F32), ((n_blk, nq, kq), F32), ((n_blk, nq, kq), F32)])
    p_glu = parts_of('s5_w_glu', _dwcall("dw_s5_glu", yg_b, dgate, BF16))

    gs_re4, gs_im4, da_re, da_im = _scan("s5_scan_bwd", gin_re.reshape(shape4), gin_im.reshape(shape4), a_re, -a_im,
                                         True, s_re4, s_im4)
    gs_re, gs_im = gs_re4.reshape(n, lt), gs_im4.reshape(n, lt)
    def s5_in_bwd_body(gr_ref, gi_ref, dyp_ref, u_ref, br_ref, bi_ref, d_ref, du_ref, dbr_ref, dbi_ref):
        i = pl.program_id(0)

        @pl.when(i == 0)
        def _():
            dbr_ref[...] = jnp.zeros_like(dbr_ref)
            dbi_ref[...] = jnp.zeros_like(dbi_ref)

        grb, gib, ub = gr_ref[...].astype(BF16), gi_ref[...].astype(BF16), u_ref[...].astype(BF16)
        for q in range(n_blk):
            rs, cs = slice(q * kq, (q + 1) * kq), slice(q * nq, (q + 1) * nq)
            du = _dot_nt(grb[:, cs], br_ref[q]) + _dot_nt(gib[:, cs], bi_ref[q])
            du_ref[:, rs] = (du + d_ref[:, rs] * dyp_ref[:, rs]).astype(BF16)
            dbr_ref[q] += _dot_tn(ub[:, rs], grb[:, cs])
            dbi_ref[q] += _dot_tn(ub[:, rs], gib[:, cs])

    du5p, d_bb_re, d_bb_im, RECV['s5_w_glu'] = _rowcall(
        "s5_in_bwd", s5_in_bwd_body, n, tm,
        [(gs_re, 'row'), (gs_im, 'row'), (dy_pre, 'row'), (u5p, 'row'), (bb_re_b, 'full'), (bb_im_b, 'full'),
         (s5_d_row, 'full')], [(dh, BF16)], accs=[((n_blk, kq, nq), F32), ((n_blk, kq, nq), F32)], scatter=[p_glu])
    du5 = from_scan_order(du5p)
    s5_grads = s5_vjp((jnp.sum(da_re, axis=0), jnp.sum(da_im, axis=0), d_bb_re, d_bb_im, d_cc_re, d_cc_im))
    for k, gval in zip(['s5_lam_re', 's5_lam_im', 's5_log_step', 's5_b_re', 's5_b_im', 's5_c_re', 's5_c_im'], s5_grads):
        G[k] = gval
    G['s5_d'] = dd

    pieces = [(dz_gm, 0, d2), (du5, d2, d2 + dh), (dz_ga, d2 + dh, d2 + dh + d), (dz_gb, d2 + dh + d, d2 + dh + 2 * d)]
    p_in = parts_of('w_in', jnp.concatenate(
        [_dwcall("dw_in_%d" % j, h0, p[0], BF16) for j, p in enumerate(pieces)], axis=1))
    grad_x2d, G['g_mix_pre'], (RECV['w_in'],) = open_sublayer("in_proj_bwd", pieces, w_in_f, x2d, W['g_mix_pre'], dx1,
                                                              scatter=[p_in])

    out = {k: [t[None] for t in _sum_adamw("sum_adamw_" + k, RECV[k], W[k], M[k], V[k])] for k in names}
    small, loss_row = _allreduce_adamw([G[k].reshape(args[k].shape) for k in SMALL], [args[k] for k in SMALL],
                                       [args['m_' + k] for k in SMALL], [args['v_' + k] for k in SMALL], loss_part)
    out.update(zip(SMALL, small))

    res = [loss_row[0, 0], grad_x2d.reshape(x.shape)]
    for j in range(4):
        res += [out[k][j] for k in WEIGHTS]
    return tuple(res)
```

```python
import functools
import math

import jax
import jax.numpy as jnp
from jax import lax
from jax.experimental import pallas as pl
from jax.experimental.pallas import tpu as pltpu

F32 = jnp.float32
BF16 = jnp.bfloat16
EPS = 1e-6
N_DEV = 8
V7X_VMEM_LIMIT = 56 * 1024 * 1024
ROW_TILE = 256
GM_CHUNK = 128
GM_GROUPS = 8
S5_GROUPS = 32
S5_STATE = 64
S5_CH = 16
SCAN_CHUNKS = 32
SCAN_LANES = 128
CA_HEADS = 4
ADAM_LR, ADAM_B1, ADAM_B2, ADAM_EPS, ADAM_WD, ADAM_STEP = 0.001, 0.9, 0.999, 1e-08, 0.01, 10

WEIGHTS = ['g_mix_pre', 'w_in', 'gm_ln_g', 'gm_ln_b', 'gm_w_s', 'gm_b_s', 's5_lam_re', 's5_lam_im', 's5_log_step',
           's5_b_re', 's5_b_im', 's5_c_re', 's5_c_im', 's5_d', 's5_w_glu', 'w_br_gm', 'w_br_s5', 'w_mix_out',
           'g_mix_post', 'g_ca_pre', 'g_mem', 'ca_w_q', 'ca_w_kv', 'ca_w_o', 'g_ca_post', 'g_ffn_pre', 'ffn_w_gu',
           'ffn_w_down', 'g_ffn_post']
SHARDED = {'w_in': 'col', 's5_w_glu': 'row', 'w_br_gm': 'row', 'w_br_s5': 'col', 'w_mix_out': 'row',
           'ca_w_q': 'row', 'ca_w_kv': 'col', 'ca_w_o': 'row', 'ffn_w_gu': 'col', 'ffn_w_down': 'row'}
SMALL = [n for n in WEIGHTS if n not in SHARDED]


def _rms(x, g):
    r = lax.rsqrt(jnp.mean(x * x, axis=-1, keepdims=True) + EPS)
    return (x * r) * g


def _rms_bwd(x, g, dy):
    r = lax.rsqrt(jnp.mean(x * x, axis=-1, keepdims=True) + EPS)
    n = x * r
    dn = dy * g
    dx = r * (dn - n * jnp.mean(dn * n, axis=-1, keepdims=True))
    return dx, jnp.sum(dy * n, axis=0, keepdims=True)


_GELU_C = math.sqrt(2.0 / math.pi)


def _gelu(x):
    return 0.5 * x * (1.0 + jnp.tanh(_GELU_C * (x + 0.044715 * (x * x * x))))


def _gelu_grad(x):
    t = jnp.tanh(_GELU_C * (x + 0.044715 * (x * x * x)))
    return 0.5 * (1.0 + t) + 0.5 * x * (1.0 - t * t) * (_GELU_C * (1.0 + 3.0 * 0.044715 * (x * x)))


def _sigmoid(x):
    return 1.0 / (1.0 + jnp.exp(-x))


def _dot(a, b):
    return jnp.dot(a, b, preferred_element_type=F32)


def _dot_nt(a, b):
    return lax.dot_general(a, b, (((1,), (1,)), ((), ())), preferred_element_type=F32)


def _dot_tn(a, b):
    return lax.dot_general(a, b, (((0,), (0,)), ((), ())), preferred_element_type=F32)


def _adamw(w, g, m, v):
    m = ADAM_B1 * m + (1.0 - ADAM_B1) * g
    v = ADAM_B2 * v + (1.0 - ADAM_B2) * (g * g)
    m_hat = m / (1.0 - ADAM_B1 ** ADAM_STEP)
    v_hat = v / (1.0 - ADAM_B2 ** ADAM_STEP)
    delta = -ADAM_LR * (m_hat / (jnp.sqrt(v_hat) + ADAM_EPS) + ADAM_WD * w)
    return delta, m, v


def _params(n_grid):
    return pltpu.CompilerParams(dimension_semantics=("arbitrary",) * n_grid, vmem_limit_bytes=V7X_VMEM_LIMIT)


def _my_place():
    x, y, c = lax.axis_index("x"), lax.axis_index("y"), lax.axis_index("c")
    return x, y, c


def _peer(x, y, c, k):
    px = 1 - x if k & 4 else x
    py = 1 - y if k & 2 else y
    pc = 1 - c if k & 1 else c
    return (px, py, pc), 4 * px + 2 * py + pc


def _exchange(kind, src_refs, dst_refs, sems, first, phase):
    x, y, c = _my_place()
    me = 4 * x + 2 * y + c
    send_sems, recv_sems, own_sems = sems
    for j, (src, dst) in enumerate(zip(src_refs, dst_refs), start=first):
        own = pltpu.make_async_copy(src if kind == 'gather' else src.at[me], dst.at[me], own_sems.at[j])
        if phase == 'start':
            own.start()
        for k in range(1, N_DEV):
            peer, peer_block = _peer(x, y, c, k)
            out = pltpu.make_async_remote_copy(
                src_ref=src if kind == 'gather' else src.at[peer_block], dst_ref=dst.at[me],
                send_sem=send_sems.at[7 * j + k - 1], recv_sem=recv_sems.at[7 * j + k - 1], device_id=peer,
                device_id_type=pl.DeviceIdType.MESH)
            if phase == 'start':
                out.start()
            else:
                pltpu.make_async_remote_copy(
                    src_ref=src if kind == 'gather' else src.at[peer_block], dst_ref=dst.at[peer_block],
                    send_sem=send_sems.at[7 * j + k - 1], recv_sem=recv_sems.at[7 * j + k - 1], device_id=peer,
                    device_id_type=pl.DeviceIdType.MESH).wait_recv()
                out.wait_send()
        if phase == 'wait':
            own.wait()


def _rowcall(name, body, n_rows, tm, ins, outs, accs=(), scratch=(), gather=(), scatter=()):
    n_steps = n_rows // tm
    moved = [('gather', a) for a in gather] + [('scatter', a) for a in scatter]
    if moved:
        n_in, n_out, n_mv, n_scr = len(ins), len(outs) + len(accs), len(moved), len(scratch)
        inner = body

        def body(*refs):
            mv_src = refs[n_in:n_in + n_mv]
            mv_dst = refs[n_in + n_mv + n_out:n_in + 2 * n_mv + n_out]
            sems = refs[n_in + 2 * n_mv + n_out + n_scr:]
            i = pl.program_id(0)

            def exchange(phase):
                for kind, lo, hi in (('gather', 0, len(gather)), ('scatter', len(gather), n_mv)):
                    if hi > lo:
                        _exchange(kind, mv_src[lo:hi], mv_dst[lo:hi], sems, lo, phase)

            pl.when(i == 0)(functools.partial(exchange, 'start'))
            inner(*refs[:n_in], *refs[n_in + n_mv:n_in + n_mv + n_out],
                  *refs[n_in + 2 * n_mv + n_out:n_in + 2 * n_mv + n_out + n_scr])
            pl.when(i == n_steps - 1)(functools.partial(exchange, 'wait'))

    arrays, in_specs = [], []
    for a, kind in ins:
        arrays.append(a)
        if kind == 'row':
            in_specs.append(pl.BlockSpec((tm,) + a.shape[1:], lambda i, nd=a.ndim: (i,) + (0,) * (nd - 1)))
        elif kind == 'full':
            in_specs.append(pl.BlockSpec(a.shape, lambda i, nd=a.ndim: (0,) * nd))
        else:
            in_specs.append(kind)
    out_shape, out_specs = [], []
    for cols, dt in outs:
        out_shape.append(jax.ShapeDtypeStruct((n_rows, cols), dt))
        out_specs.append(pl.BlockSpec((tm, cols), lambda i: (i, 0)))
    for shp, dt in accs:
        if isinstance(dt, tuple):
            dt, spec = dt
        else:
            spec = pl.BlockSpec(shp, lambda i, nd=len(shp): (0,) * nd)
        out_shape.append(jax.ShapeDtypeStruct(shp, dt))
        out_specs.append(spec)
    scratch = list(scratch)
    for kind, a in moved:
        arrays.append(a)
        in_specs.append(pl.BlockSpec(memory_space=pl.ANY))
        out_shape.append(jax.ShapeDtypeStruct((N_DEV,) + a.shape if kind == 'gather' else a.shape, a.dtype))
        out_specs.append(pl.BlockSpec(memory_space=pl.ANY))
    if moved:
        scratch += [pltpu.SemaphoreType.DMA((7 * len(moved),)), pltpu.SemaphoreType.DMA((7 * len(moved),)),
                    pltpu.SemaphoreType.DMA((len(moved),))]
    return pl.pallas_call(functools.partial(body), grid=(n_steps,), in_specs=in_specs, out_specs=out_specs,
                          out_shape=out_shape, scratch_shapes=scratch, name=name, compiler_params=_params(1))(*arrays)


def _dw_tiles(n, ka, nn, a_itemsize):
    tn = max(t for t in range(128, min(nn, 1536) + 1, 128) if nn % t == 0)
    tm = min(n, 2048)
    while tm > 256 and tm * ka * a_itemsize > 6 * 1024 * 1024:
        tm //= 2
    return tm, tn


def _dwcall(name, a, dc, out_dtype):
    n, ka = a.shape
    nn = dc.shape[1]
    tm, tn = _dw_tiles(n, ka, nn, a.dtype.itemsize)
    n_i = n // tm

    def body(a_ref, dc_ref, o_ref, acc_ref):
        i = pl.program_id(1)

        @pl.when(i == 0)
        def _():
            acc_ref[...] = jnp.zeros_like(acc_ref)

        acc_ref[...] += _dot_tn(a_ref[...].astype(BF16), dc_ref[...].astype(BF16))

        @pl.when(i == n_i - 1)
        def _():
            o_ref[...] = acc_ref[...].astype(o_ref.dtype)

    return pl.pallas_call(
        body, grid=(nn // tn, n_i),
        in_specs=[pl.BlockSpec((tm, ka), lambda j, i: (i, 0)), pl.BlockSpec((tm, tn), lambda j, i: (i, j))],
        out_specs=pl.BlockSpec((ka, tn), lambda j, i: (0, j)),
        out_shape=jax.ShapeDtypeStruct((ka, nn), out_dtype),
        scratch_shapes=[pltpu.VMEM((ka, tn), F32)], name=name, compiler_params=_params(2))(a, dc)


def _first_gather(first, later):
    n = len(later)
    vm = pl.BlockSpec(memory_space=pltpu.VMEM)

    def body(*refs):
        first_ref, later_refs = refs[0], refs[1:1 + n]
        out_ref, cast_refs = refs[1 + n], refs[2 + n:2 + 2 * n]
        stage = refs[2 + 2 * n]
        sems = refs[3 + 2 * n:]
        send_sems, recv_sems, own_sems = sems
        stage[...] = first_ref[...].astype(BF16)
        x, y, c = _my_place()
        me, sibling = (x, y, c), (x, y, 1 - c)
        chips = [(1 - x, y), (x, 1 - y), (1 - x, 1 - y)]

        def rows(px, py, pc):
            return out_ref.at[4 * px + 2 * py + pc]

        def copy(k, block, to, src=None):
            return pltpu.make_async_remote_copy(
                src_ref=rows(*block) if src is None else src, dst_ref=rows(*block), send_sem=send_sems.at[k],
                recv_sem=recv_sems.at[k], device_id=to, device_id_type=pl.DeviceIdType.MESH)

        mine = pltpu.make_async_copy(stage, rows(*me), own_sems.at[0])
        mine.start()
        first_out = [copy(0, me, sibling, src=stage)]
        first_out += [copy(1 + j, me, (*chip, c), src=stage) for j, chip in enumerate(chips)]
        for cp in first_out:
            cp.start()
        for src, dst in zip(later_refs, cast_refs):
            dst[...] = src[...].astype(BF16)
        passed = [copy(4 + j, (*chip, c), sibling) for j, chip in enumerate(chips)]
        for j, chip in enumerate(chips):
            copy(1 + j, (*chip, c), me).wait_recv()
            passed[j].start()
        copy(0, sibling, me).wait_recv()
        for j, chip in enumerate(chips):
            copy(4 + j, (*chip, 1 - c), me).wait_recv()
        for cp in first_out + passed:
            cp.wait_send()
        mine.wait()

    return pl.pallas_call(
        body, out_shape=[jax.ShapeDtypeStruct((N_DEV,) + first.shape, BF16)]
        + [jax.ShapeDtypeStruct(s.shape, BF16) for s in later],
        in_specs=[vm] * (1 + n), out_specs=[pl.BlockSpec(memory_space=pl.ANY)] + [vm] * n,
        scratch_shapes=[pltpu.VMEM(first.shape, BF16), pltpu.SemaphoreType.DMA((7,)), pltpu.SemaphoreType.DMA((7,)),
                        pltpu.SemaphoreType.DMA((1,))],
        name="gather_first", compiler_params=pltpu.CompilerParams(vmem_limit_bytes=V7X_VMEM_LIMIT))(first, *later)


def _sum_adamw(name, recv, w, m, v):
    shp = w.shape

    def body(recv_ref, w_ref, m_ref, v_ref, g_ref, d_ref, nm_ref, nv_ref):
        g = recv_ref[0].astype(F32)
        for k in range(1, N_DEV):
            g = g + recv_ref[k].astype(F32)
        d, nm, nv = _adamw(w_ref[...], g, m_ref[...], v_ref[...])
        g_ref[...] = g
        d_ref[...] = d
        nm_ref[...] = nm
        nv_ref[...] = nv

    vm = pl.BlockSpec(memory_space=pltpu.VMEM)
    return pl.pallas_call(
        body, out_shape=[jax.ShapeDtypeStruct(shp, F32)] * 4, in_specs=[vm] * 4, out_specs=[vm] * 4,
        name=name, compiler_params=pltpu.CompilerParams(vmem_limit_bytes=V7X_VMEM_LIMIT))(recv, w, m, v)


def _gather_last(a):
    def body(a_ref, out_ref, send_sems, recv_sems, own_sems):
        sems = (send_sems, recv_sems, own_sems)
        _exchange('gather', [a_ref], [out_ref], sems, 0, 'start')
        _exchange('gather', [a_ref], [out_ref], sems, 0, 'wait')

    vm = pl.BlockSpec(memory_space=pltpu.VMEM)
    return pl.pallas_call(
        body, out_shape=jax.ShapeDtypeStruct((N_DEV,) + a.shape, a.dtype), in_specs=[vm], out_specs=vm,
        scratch_shapes=[pltpu.SemaphoreType.DMA((7,)), pltpu.SemaphoreType.DMA((7,)), pltpu.SemaphoreType.DMA((1,))],
        name="gather_last", compiler_params=pltpu.CompilerParams(vmem_limit_bytes=V7X_VMEM_LIMIT))(a)


def _small_adamw(entries, gathered, loss_at):
    n, ng = len(entries), len(gathered)

    def body(*refs):
        g_refs, wmv = refs[:ng], refs[ng:ng + 3 * n]
        outs, loss_out = refs[ng + 3 * n:ng + 7 * n], refs[ng + 7 * n]

        def total(ref, r):
            if r is None:
                t = ref[0]
                for k in range(1, N_DEV):
                    t = t + ref[k]
                return t[None]
            t = ref[0, r:r + 1, :]
            for k in range(1, N_DEV):
                t = t + ref[k, r:r + 1, :]
            return t

        for j, (_, _, _, gi, r) in enumerate(entries):
            g = total(g_refs[gi], r)
            d, nm, nv = _adamw(wmv[3 * j][...], g, wmv[3 * j + 1][...], wmv[3 * j + 2][...])
            for ref, val in zip(outs[4 * j:4 * j + 4], (g, d, nm, nv)):
                ref[...] = val
        loss_out[...] = total(g_refs[loss_at[0]], loss_at[1])[:, :128]

    vm = pl.BlockSpec(memory_space=pltpu.VMEM)
    out_shape, arrays = [], list(gathered)
    for w, m, v, _, _ in entries:
        out_shape += [jax.ShapeDtypeStruct(w.shape, F32)] * 4
        arrays += [w, m, v]
    out_shape.append(jax.ShapeDtypeStruct((1, 128), F32))
    res = pl.pallas_call(
        body, out_shape=out_shape, in_specs=[vm] * len(arrays), out_specs=[vm] * len(out_shape),
        name="small_adamw", compiler_params=pltpu.CompilerParams(vmem_limit_bytes=V7X_VMEM_LIMIT))(*arrays)
    return [res[4 * j:4 * j + 4] for j in range(n)], res[4 * n]


def _blockdiag8(t):
    g, per = t.shape[0], 8
    eye = jnp.eye(per, dtype=F32)
    t = t.reshape(g // per, per, t.shape[1], t.shape[2])
    return (t[:, :, :, None, :] * eye[None, :, None, :, None]).reshape(g // per, per * t.shape[2], per * t.shape[3])


def _s5_out_tables(c_re, c_im):
    return _blockdiag8(c_re.transpose(0, 2, 1)), _blockdiag8(c_im.transpose(0, 2, 1))


def _s5_in_tables(lam_re, lam_im, log_step, b_re, b_im):
    step = jnp.exp(log_step)[:, None]
    mag = jnp.exp(lam_re * step)
    ab_re = mag * jnp.cos(lam_im * step)
    ab_im = mag * jnp.sin(lam_im * step)
    den = lam_re * lam_re + lam_im * lam_im
    nr = ab_re - 1.0
    co_re = (nr * lam_re + ab_im * lam_im) / den
    co_im = (ab_im * lam_re - nr * lam_im) / den
    bb_re = co_re[..., None] * b_re - co_im[..., None] * b_im
    bb_im = co_re[..., None] * b_im + co_im[..., None] * b_re
    return (ab_re.reshape(1, -1), ab_im.reshape(1, -1), _blockdiag8(bb_re.transpose(0, 2, 1)),
            _blockdiag8(bb_im.transpose(0, 2, 1)))


def _scan(name, xr, xi, a_re, a_im, reverse, s_re=None, s_im=None):
    nb, tc, nc, lt = xr.shape
    ln = SCAN_LANES
    n_sq = int(math.log2(tc))
    assert 2 ** n_sq == tc
    with_da = s_re is not None

    def body(*refs):
        if with_da:
            xr_ref, xi_ref, ar_ref, ai_ref, sr_ref, si_ref, or_ref, oi_ref, dar_ref, dai_ref, cr_ref, ci_ref = refs
        else:
            xr_ref, xi_ref, ar_ref, ai_ref, or_ref, oi_ref, cr_ref, ci_ref = refs
        ar = jnp.broadcast_to(ar_ref[...], (nc, ln))
        ai = jnp.broadcast_to(ai_ref[...], (nc, ln))
        zero = jnp.zeros((nc, ln), F32)

        def at(t):
            return tc - 1 - t if reverse else t

        def local(t, carry):
            sr, si = carry
            j = at(t)
            nr = ar * sr - ai * si + xr_ref[0, j]
            ni = ar * si + ai * sr + xi_ref[0, j]
            or_ref[0, j] = nr
            oi_ref[0, j] = ni
            return nr, ni

        lr, li = lax.fori_loop(0, tc, local, (zero, zero))
        pr, pi = ar_ref[...], ai_ref[...]
        for _ in range(n_sq):
            pr, pi = pr * pr - pi * pi, 2.0 * (pr * pi)
        cr_ref[...] = lr
        ci_ref[...] = li
        tr = jnp.zeros((1, ln), F32)
        ti = jnp.zeros((1, ln), F32)
        for c in (range(nc - 1, -1, -1) if reverse else range(nc)):
            l_r = cr_ref[c:c + 1, :]
            l_i = ci_ref[c:c + 1, :]
            cr_ref[c:c + 1, :] = tr
            ci_ref[c:c + 1, :] = ti
            tr, ti = pr * tr - pi * ti + l_r, pr * ti + pi * tr + l_i

        def fix(j, qr, qi):
            nqr = ar * qr - ai * qi
            nqi = ar * qi + ai * qr
            gr = or_ref[0, j] + nqr
            gi = oi_ref[0, j] + nqi
            or_ref[0, j] = gr
            oi_ref[0, j] = gi
            return nqr, nqi, gr, gi

        if not with_da:
            def fixup(t, carry):
                qr, qi, _, _ = fix(at(t), *carry)
                return qr, qi

            lax.fori_loop(0, tc, fixup, (cr_ref[...], ci_ref[...]))
        else:
            def fixup(t, carry):
                qr, qi, dr, di = carry
                j = at(t)
                qr, qi, gr, gi = fix(j, qr, qi)
                pr_, pi_ = sr_ref[0, j - 1], si_ref[0, j - 1]
                return qr, qi, dr + (pr_ * gr + pi_ * gi), di + (pr_ * gi - pi_ * gr)

            qr, qi, dr, di = lax.fori_loop(0, tc - 1, fixup, (cr_ref[...], ci_ref[...], zero, zero))
            _, _, gr, gi = fix(0, qr, qi)
            row = lax.broadcasted_iota(jnp.int32, (nc, ln), 0)
            pr_ = jnp.where(row == 0, 0.0, pltpu.roll(sr_ref[0, tc - 1], 1, 0))
            pi_ = jnp.where(row == 0, 0.0, pltpu.roll(si_ref[0, tc - 1], 1, 0))
            dr = dr + (pr_ * gr + pi_ * gi)
            di = di + (pr_ * gi - pi_ * gr)
            dar_ref[0] = jnp.sum(dr, axis=0, keepdims=True)
            dai_ref[0] = jnp.sum(di, axis=0, keepdims=True)

    blk = pl.BlockSpec((1, tc, nc, ln), lambda b, l: (b, 0, 0, l))
    a_spec = pl.BlockSpec((1, ln), lambda b, l: (0, l))
    ins = [xr, xi, a_re, a_im] + ([s_re, s_im] if with_da else [])
    in_specs = [blk, blk, a_spec, a_spec] + ([blk, blk] if with_da else [])
    out_shape = [jax.ShapeDtypeStruct(xr.shape, F32)] * 2
    out_specs = [blk, blk]
    if with_da:
        out_shape += [jax.ShapeDtypeStruct((nb, 1, lt), F32)] * 2
        out_specs += [pl.BlockSpec((1, 1, ln), lambda b, l: (b, 0, l))] * 2
    return pl.pallas_call(body, grid=(nb, lt // ln), in_specs=in_specs, out_specs=out_specs, out_shape=out_shape,
                          scratch_shapes=[pltpu.VMEM((nc, ln), F32)] * 2, name=name, compiler_params=_params(2))(*ins)


def kernel(x, mem, g_mix_pre, w_in, gm_ln_g, gm_ln_b, gm_w_s, gm_b_s, s5_lam_re, s5_lam_im, s5_log_step, s5_b_re, s5_b_im, s5_c_re, s5_c_im, s5_d, s5_w_glu, w_br_gm, w_br_s5, w_mix_out, g_mix_post, g_ca_pre, g_mem, ca_w_q, ca_w_kv, ca_w_o, g_ca_post, g_ffn_pre, ffn_w_gu, ffn_w_down, g_ffn_post, loss_target, m_g_mix_pre, m_w_in, m_gm_ln_g, m_gm_ln_b, m_gm_w_s, m_gm_b_s, m_s5_lam_re, m_s5_lam_im, m_s5_log_step, m_s5_b_re, m_s5_b_im, m_s5_c_re, m_s5_c_im, m_s5_d, m_s5_w_glu, m_w_br_gm, m_w_br_s5, m_w_mix_out, m_g_mix_post, m_g_ca_pre, m_g_mem, m_ca_w_q, m_ca_w_kv, m_ca_w_o, m_g_ca_post, m_g_ffn_pre, m_ffn_w_gu, m_ffn_w_down, m_g_ffn_post, v_g_mix_pre, v_w_in, v_gm_ln_g, v_gm_ln_b, v_gm_w_s, v_gm_b_s, v_s5_lam_re, v_s5_lam_im, v_s5_log_step, v_s5_b_re, v_s5_b_im, v_s5_c_re, v_s5_c_im, v_s5_d, v_s5_w_glu, v_w_br_gm, v_w_br_s5, v_w_mix_out, v_g_mix_post, v_g_ca_pre, v_g_mem, v_ca_w_q, v_ca_w_kv, v_ca_w_o, v_g_ca_post, v_g_ffn_pre, v_ffn_w_gu, v_ffn_w_down, v_g_ffn_post):
    args = locals()
    W = {n: args[n][0] for n in WEIGHTS}
    M = {n: args['m_' + n][0] for n in WEIGHTS}
    V = {n: args['v_' + n][0] for n in WEIGHTS}

    nb, seq, d = x.shape
    n = nb * seq
    tm = min(ROW_TILE, n)
    nmem = mem.shape[1]
    d2, dh = 2 * d, d // 2
    hd = d // CA_HEADS
    x2d = x.reshape(n, d)
    tgt = loss_target.reshape(n, d)
    mem2d = mem.reshape(nb * nmem, d)

    def row(v):
        return v.reshape(1, -1)

    names = list(SHARDED)
    later = [k for k in names if k != 'w_in']
    first_g, *casts = _first_gather(W['w_in'], [W[k] for k in later])
    shard_b = dict(zip(later, casts))
    half = shard_b['ffn_w_gu'].shape[0] // 2
    gu_halves = [shard_b['ffn_w_gu'][:half], shard_b['ffn_w_gu'][half:]]

    def assemble(k, gth):
        r, c = gth.shape[1:]
        return gth.reshape(N_DEV * r, c) if SHARDED[k] == 'row' else gth.transpose(1, 0, 2).reshape(r, N_DEV * c)

    w_in_f = assemble('w_in', first_g)
    ffh = N_DEV * W['ffn_w_down'].shape[0]

    (a_re, a_im, bb_re, bb_im), s5_in_vjp = jax.vjp(
        _s5_in_tables, W['s5_lam_re'], W['s5_lam_im'], W['s5_log_step'], W['s5_b_re'], W['s5_b_im'])
    (cc_re, cc_im), s5_out_vjp = jax.vjp(_s5_out_tables, W['s5_c_re'], W['s5_c_im'])
    bb_re_b, bb_im_b, cc_re_b, cc_im_b = (t.astype(BF16) for t in (bb_re, bb_im, cc_re, cc_im))
    s5_d_row = row(W['s5_d'])
    tril = jnp.tril(jnp.ones((GM_CHUNK, GM_CHUNK), bool))
    w_s = jnp.where(tril[None], W['gm_w_s'], 0.0).astype(BF16)
    w_s_t = w_s.transpose(0, 2, 1)
    gm_bias = jnp.repeat(W['gm_b_s'].T, d // GM_GROUPS, axis=1)

    def in_proj_body(x_ref, g_ref, w_ref, zgm_ref, u5_ref, zga_ref, zgb_ref, h_ref):
        hb = _rms(x_ref[...], g_ref[...]).astype(BF16)
        h_ref[...] = hb
        for lo in range(0, w_ref.shape[1], 512):
            acc = _dot(hb, w_ref[:, lo:lo + 512])
            if lo < d2:
                zgm_ref[:, lo:lo + 512] = acc.astype(BF16)
            elif lo < d2 + dh:
                u5_ref[...] = acc
            elif lo < d2 + dh + d:
                zga_ref[:, lo - d2 - dh:lo - d2 - dh + 512] = acc.astype(BF16)
            else:
                zgb_ref[:, lo - d2 - dh - d:lo - d2 - dh - d + 512] = acc.astype(BF16)

    ride = ['s5_w_glu', 'w_br_gm', 'w_br_s5', 'w_mix_out', 'ca_w_q']
    z_gm, u5, z_ga, z_gb, h0, *got = _rowcall(
        "in_proj", in_proj_body, n, tm, [(x2d, 'row'), (row(W['g_mix_pre']), 'full'), (w_in_f, 'full')],
        [(d2, BF16), (dh, F32), (d, BF16), (d, BF16), (d, BF16)], gather=[shard_b[k] for k in ride])
    w_glu_f, w_brgm_f, w_brs5_f, w_mix_f, w_q_f = (assemble(k, g) for k, g in zip(ride, got))

    def gm_recompute(z_ref, lng_ref, lnb_ref):
        zg = _gelu(z_ref[...].astype(F32))
        u, v = zg[:, :d], zg[:, d:]
        vc = v - jnp.mean(v, axis=-1, keepdims=True)
        rstd = lax.rsqrt(jnp.mean(vc * vc, axis=-1, keepdims=True) + EPS)
        vhat = vc * rstd
        vn = vhat * lng_ref[...] + lnb_ref[...]
        return u, vhat, rstd, vn.astype(BF16)

    gw = d // GM_GROUPS

    def gm_fwd_body(z_ref, lng_ref, lnb_ref, ws_ref, bias_ref, y_ref):
        u, _, _, vnb = gm_recompute(z_ref, lng_ref, lnb_ref)
        for g in range(GM_GROUPS):
            sl = slice(g * gw, (g + 1) * gw)
            sv = _dot(ws_ref[g], vnb[:, sl]) + bias_ref[:, sl]
            y_ref[:, sl] = (u[:, sl] * sv).astype(BF16)

    y_gm, g_kv, g_o = _rowcall("gmlp_fwd", gm_fwd_body, n, GM_CHUNK,
                               [(z_gm, 'row'), (row(W['gm_ln_g']), 'full'), (row(W['gm_ln_b']), 'full'), (w_s, 'full'),
                                (gm_bias, 'full')], [(d, BF16)], gather=[shard_b['ca_w_kv'], shard_b['ca_w_o']])
    w_kv_f, w_o_f = assemble('ca_w_kv', g_kv), assemble('ca_w_o', g_o)

    tc = seq // SCAN_CHUNKS
    lt = S5_GROUPS * S5_STATE

    def to_scan_order(t):
        return t.reshape(nb, SCAN_CHUNKS, tc, t.shape[-1]).transpose(0, 2, 1, 3).reshape(n, t.shape[-1])

    def from_scan_order(t):
        return t.reshape(nb, tc, SCAN_CHUNKS, t.shape[-1]).transpose(0, 2, 1, 3).reshape(n, t.shape[-1])

    u5p = to_scan_order(u5)
    kq, nq = 4 * S5_CH * 2, 4 * S5_STATE * 2
    n_blk = dh // kq

    def s5_in_body(u_ref, br_ref, bi_ref, or_ref, oi_ref):
        ub = u_ref[...].astype(BF16)
        for q in range(n_blk):
            uq = ub[:, q * kq:(q + 1) * kq]
            or_ref[:, q * nq:(q + 1) * nq] = _dot(uq, br_ref[q])
            oi_ref[:, q * nq:(q + 1) * nq] = _dot(uq, bi_ref[q])

    bu_re, bu_im, g_gu0 = _rowcall("s5_in", s5_in_body, n, tm, [(u5p, 'row'), (bb_re_b, 'full'), (bb_im_b, 'full')],
                                   [(lt, F32), (lt, F32)], gather=[gu_halves[0]])
    shape4 = (nb, tc, SCAN_CHUNKS, lt)
    s_re4, s_im4 = _scan("s5_scan_fwd", bu_re.reshape(shape4), bu_im.reshape(shape4), a_re, a_im, False)
    s_re, s_im = s_re4.reshape(n, lt), s_im4.reshape(n, lt)

    def s5_out_body(sr_ref, si_ref, u_ref, cr_ref, ci_ref, d_ref, wg_ref, ypre_ref, gate_ref, y_ref):
        srb, sib = sr_ref[...].astype(BF16), si_ref[...].astype(BF16)
        for q in range(n_blk):
            rs, cs = slice(q * nq, (q + 1) * nq), slice(q * kq, (q + 1) * kq)
            yq = _dot(srb[:, rs], cr_ref[q]) - _dot(sib[:, rs], ci_ref[q])
            ypre_ref[:, cs] = yq + d_ref[:, cs] * u_ref[:, cs]
        yg = _gelu(ypre_ref[...])
        gate = _dot(yg.astype(BF16), wg_ref[...])
        gate_ref[...] = gate
        y_ref[...] = (yg * _sigmoid(gate)).astype(BF16)

    y_pre, gate, y_s5p, g_gu1 = _rowcall(
        "s5_out", s5_out_body, n, tm,
        [(s_re, 'row'), (s_im, 'row'), (u5p, 'row'), (cc_re_b, 'full'), (cc_im_b, 'full'), (s5_d_row, 'full'),
         (w_glu_f, 'full')], [(dh, F32), (dh, F32), (dh, BF16)], gather=[gu_halves[1]])
    y_s5 = from_scan_order(y_s5p)
    w_gu_f = jnp.concatenate([assemble('ffn_w_gu', g_gu0), assemble('ffn_w_gu', g_gu1)], axis=0)

    def merge_body(ygm_ref, ys5_ref, zga_ref, zgb_ref, wa_ref, wb_ref, pa_ref, pb_ref, mg_ref):
        pa = _dot(ygm_ref[...], wa_ref[...])
        pb = _dot(ys5_ref[...], wb_ref[...])
        pa_ref[...] = pa.astype(BF16)
        pb_ref[...] = pb.astype(BF16)
        mg_ref[...] = (_sigmoid(zga_ref[...].astype(F32)) * pa + _sigmoid(zgb_ref[...].astype(F32)) * pb).astype(BF16)

    p_a, p_b, merged, g_down = _rowcall(
        "merge", merge_body, n, tm,
        [(y_gm, 'row'), (y_s5, 'row'), (z_ga, 'row'), (z_gb, 'row'), (w_brgm_f, 'full'), (w_brs5_f, 'full')],
        [(d, BF16), (d, BF16), (d, BF16)], gather=[shard_b['ffn_w_down']])
    w_down_f = assemble('ffn_w_down', g_down)

    def close_sublayer(name, a_in, w_out, x_res, g_post, g_next, w_next=None):
        def body(*refs):
            a_ref, w_ref, x_ref, gp_ref, gn_ref = refs[:5]
            rest = refs[5:]
            if w_next is not None:
                wn_ref, rest = rest[0], rest[1:]
            o_ref, xo_ref, h_ref = rest[:3]
            o = _dot(a_ref[...], w_ref[...])
            o_ref[...] = o.astype(BF16)
            xo = x_ref[...] + _rms(o, gp_ref[...])
            xo_ref[...] = xo
            hb = _rms(xo, gn_ref[...]).astype(BF16)
            h_ref[...] = hb
            if w_next is not None:
                rest[3][...] = _dot(hb, wn_ref[...]).astype(BF16)

        ins = [(a_in, 'row'), (w_out, 'full'), (x_res, 'row'), (row(g_post), 'full'), (row(g_next), 'full')]
        outs = [(d, BF16), (d, F32), (d, BF16)]
        if w_next is not None:
            ins.append((w_next, 'full'))
            outs.append((w_next.shape[1], BF16))
        return _rowcall(name, body, n, tm, ins, outs)

    o1, x1, hc, q = close_sublayer("mix_out", merged, w_mix_f, x2d, W['g_mix_post'], W['g_ca_pre'], w_q_f)

    tmm = min(ROW_TILE, nb * nmem)

    def memkv_body(m_ref, g_ref, w_ref, mn_ref, k_ref, v_ref):
        mnb = _rms(m_ref[...], g_ref[...]).astype(BF16)
        mn_ref[...] = mnb
        k_ref[...] = _dot(mnb, w_ref[:, :d]).astype(BF16)
        v_ref[...] = _dot(mnb, w_ref[:, d:]).astype(BF16)

    mem_n, k_mem, v_mem = _rowcall("mem_kv", memkv_body, nb * nmem, tmm,
                                   [(mem2d, 'row'), (row(W['g_mem']), 'full'), (w_kv_f, 'full')],
                                   [(d, BF16), (d, BF16), (d, BF16)])

    tiles_per_ex = seq // tm
    kv_spec = pl.BlockSpec((nmem, d), lambda i: (i // tiles_per_ex, 0))
    scale = hd ** -0.5

    def softmax_rows(qh, kh):
        s = _dot_nt(qh, kh) * scale
        e = jnp.exp(s - jnp.max(s, axis=-1, keepdims=True))
        return e / jnp.sum(e, axis=-1, keepdims=True)

    def attn_body(q_ref, k_ref, v_ref, o_ref):
        for h in range(CA_HEADS):
            sl = slice(h * hd, (h + 1) * hd)
            p = softmax_rows(q_ref[:, sl], k_ref[:, sl])
            o_ref[:, sl] = _dot(p.astype(BF16), v_ref[:, sl]).astype(BF16)

    (att,) = _rowcall("attn_fwd", attn_body, n, tm, [(q, 'row'), (k_mem, kv_spec), (v_mem, kv_spec)], [(d, BF16)])

    o2, x2, hf = close_sublayer("attn_out", att, w_o_f, x1, W['g_ca_post'], W['g_ffn_pre'])

    ck = 256

    def ffn_up_body(h_ref, w_ref, gu_ref, a_ref):
        hb = h_ref[...]
        for lo in range(0, ffh, ck):
            gt = _dot(hb, w_ref[:, lo:lo + ck])
            ut = _dot(hb, w_ref[:, ffh + lo:ffh + lo + ck])
            gu_ref[:, lo:lo + ck] = gt.astype(BF16)
            gu_ref[:, ffh + lo:ffh + lo + ck] = ut.astype(BF16)
            a_ref[:, lo:lo + ck] = ((gt * _sigmoid(gt)) * ut).astype(BF16)

    gu, act = _rowcall("ffn_up", ffn_up_body, n, tm, [(hf, 'row'), (w_gu_f, 'full')], [(2 * ffh, BF16), (ffh, BF16)])

    def ffn_down_body(a_ref, w_ref, x_ref, t_ref, g_ref, dx_ref, do_ref, loss_ref, dg_ref):
        i = pl.program_id(0)

        @pl.when(i == 0)
        def _():
            loss_ref[...] = jnp.zeros_like(loss_ref)
            dg_ref[...] = jnp.zeros_like(dg_ref)

        o = _dot(a_ref[...], w_ref[...])
        diff = x_ref[...] + _rms(o, g_ref[...]) - t_ref[...]
        loss_ref[...] += jnp.full(loss_ref.shape, 0.5 / d, F32) * jnp.sum(diff * diff)
        dx = diff * (1.0 / d)
        dx_ref[...] = dx
        do, dg = _rms_bwd(o, g_ref[...], dx)
        do_ref[...] = do.astype(BF16)
        dg_ref[...] += dg

    dx3, do3, loss_part, dg_ffn_post = _rowcall(
        "ffn_down_loss", ffn_down_body, n, tm,
        [(act, 'row'), (w_down_f, 'full'), (x2, 'row'), (tgt, 'row'), (row(W['g_ffn_post']), 'full')],
        [(d, F32), (d, BF16)], accs=[((1, 128), F32), ((1, d), F32)])

    G = {'g_ffn_post': dg_ffn_post}
    RECV = {}

    def parts_of(k, gfull):
        r, c = W[k].shape
        return gfull.reshape(N_DEV, r, c) if SHARDED[k] == 'row' else gfull.reshape(r, N_DEV, c).transpose(1, 0, 2)

    p_down = parts_of('ffn_w_down', _dwcall("dw_ffn_down", act, do3, BF16))

    def ffn_act_bwd_body(do_ref, w_ref, gu_ref, dgu_ref):
        dob = do_ref[...]
        for lo in range(0, ffh, ck):
            da = _dot_nt(dob, w_ref[lo:lo + ck, :])
            gt = gu_ref[:, lo:lo + ck].astype(F32)
            ut = gu_ref[:, ffh + lo:ffh + lo + ck].astype(F32)
            sg = _sigmoid(gt)
            dgu_ref[:, lo:lo + ck] = (da * ut * (sg * (1.0 + gt * (1.0 - sg)))).astype(BF16)
            dgu_ref[:, ffh + lo:ffh + lo + ck] = (da * (gt * sg)).astype(BF16)

    dgu, RECV['ffn_w_down'] = _rowcall("ffn_act_bwd", ffn_act_bwd_body, n, tm,
                                       [(do3, 'row'), (w_down_f, 'full'), (gu, 'row')], [(2 * ffh, BF16)],
                                       scatter=[p_down])
    p_gu = parts_of('ffn_w_gu', _dwcall("dw_ffn_gu", hf, dgu, BF16))

    def open_sublayer(name, pieces, w_full, x_in, g_pre, dx_up, o_prev=None, g_post_prev=None, scatter=(), gather=()):
        n_p = len(pieces)
        second = o_prev is not None

        def body(*refs):
            dc_refs, (w_ref, x_ref, g_ref, dxu_ref), rest = refs[:n_p], refs[n_p:n_p + 4], refs[n_p + 4:]
            if second:
                (op_ref, gp_ref), rest = rest[:2], rest[2:]
            i = pl.program_id(0)
            dhid = None
            for dc_ref, (_, lo, hi) in zip(dc_refs, pieces):
                part = _dot_nt(dc_ref[...], w_ref[:, lo:hi])
                dhid = part if dhid is None else dhid + part
            dxn, dg = _rms_bwd(x_ref[...], g_ref[...], dhid)
            dx = dxu_ref[...] + dxn
            if second:
                dx_ref, do_ref, dg_ref, dg2_ref = rest
            else:
                dx_ref, dg_ref = rest

            @pl.when(i == 0)
            def _():
                dg_ref[...] = jnp.zeros_like(dg_ref)
                if second:
                    dg2_ref[...] = jnp.zeros_like(dg2_ref)

            dx_ref[...] = dx
            dg_ref[...] += dg
            if second:
                do, dg2 = _rms_bwd(op_ref[...].astype(F32), gp_ref[...], dx)
                do_ref[...] = do.astype(BF16)
                dg2_ref[...] += dg2

        ins = [(p[0], 'row') for p in pieces] + [(w_full, 'full'), (x_in, 'row'), (row(g_pre), 'full'), (dx_up, 'row')]
        outs = [(d, F32)]
        accs = [((1, d), F32)]
        if second:
            ins += [(o_prev, 'row'), (row(g_post_prev), 'full')]
            outs.append((d, BF16))
            accs.append(((1, d), F32))
        res = _rowcall(name, body, n, tm, ins, outs, accs=accs, scatter=scatter, gather=gather)
        if second:
            dx, do, dg, dg2 = res[:4]
            return dx, dg, do, dg2, res[4:]
        return res[0], res[1], res[2:]

    dx2, G['g_ffn_pre'], do2, G['g_ca_post'], (RECV['ffn_w_gu'],) = open_sublayer(
        "ffn_in_bwd", [(dgu, 0, 2 * ffh)], w_gu_f, x2, W['g_ffn_pre'], dx3, o2, W['g_ca_post'], scatter=[p_gu])
    p_o = parts_of('ca_w_o', _dwcall("dw_ca_o", att, do2, BF16))

    def attn_bwd_body(q_ref, k_ref, v_ref, do_ref, wo_ref, dq_ref, dk_ref, dv_ref):
        i = pl.program_id(0)

        @pl.when(i % tiles_per_ex == 0)
        def _():
            dk_ref[...] = jnp.zeros_like(dk_ref)
            dv_ref[...] = jnp.zeros_like(dv_ref)

        d_att = _dot_nt(do_ref[...], wo_ref[...]).astype(BF16)
        for h in range(CA_HEADS):
            sl = slice(h * hd, (h + 1) * hd)
            qh, kh, vh, dah = q_ref[:, sl], k_ref[:, sl], v_ref[:, sl], d_att[:, sl]
            p = softmax_rows(qh, kh)
            dp = _dot_nt(dah, vh)
            ds = (p * (dp - jnp.sum(p * dp, axis=-1, keepdims=True)) * scale).astype(BF16)
            dq_ref[:, sl] = _dot(ds, kh).astype(BF16)
            dk_ref[:, sl] += _dot_tn(ds, qh)
            dv_ref[:, sl] += _dot_tn(p.astype(BF16), dah)

    kv_acc = ((nb * nmem, d), (F32, pl.BlockSpec((nmem, d), lambda i: (i // tiles_per_ex, 0))))
    dq, dk_mem, dv_mem, RECV['ca_w_o'] = _rowcall(
        "attn_bwd", attn_bwd_body, n, tm,
        [(q, 'row'), (k_mem, kv_spec), (v_mem, kv_spec), (do2, 'row'), (w_o_f, 'full')], [(d, BF16)],
        accs=[kv_acc, kv_acc], scatter=[p_o])
    p_q = parts_of('ca_w_q', _dwcall("dw_ca_q", hc, dq, BF16))

    def memkv_bwd_body(dk_ref, dv_ref, m_ref, g_ref, w_ref, dkv_ref, dg_ref):
        i = pl.program_id(0)

        @pl.when(i == 0)
        def _():
            dg_ref[...] = jnp.zeros_like(dg_ref)

        dkb, dvb = dk_ref[...].astype(BF16), dv_ref[...].astype(BF16)
        dkv_ref[:, :d] = dkb
        dkv_ref[:, d:] = dvb
        dmn = _dot_nt(dkb, w_ref[:, :d]) + _dot_nt(dvb, w_ref[:, d:])
        _, dg = _rms_bwd(m_ref[...], g_ref[...], dmn)
        dg_ref[...] += dg

    dkv, G['g_mem'] = _rowcall(
        "mem_kv_bwd", memkv_bwd_body, nb * nmem, tmm,
        [(dk_mem, 'row'), (dv_mem, 'row'), (mem2d, 'row'), (row(W['g_mem']), 'full'), (w_kv_f, 'full')],
        [(d2, BF16)], accs=[((1, d), F32)])
    p_kv = parts_of('ca_w_kv', _dwcall("dw_ca_kv", mem_n, dkv, BF16))

    dx1, G['g_ca_pre'], do1, G['g_mix_post'], (RECV['ca_w_q'], RECV['ca_w_kv']) = open_sublayer(
        "attn_in_bwd", [(dq, 0, d)], w_q_f, x1, W['g_ca_pre'], dx2, o1, W['g_mix_post'], scatter=[p_q, p_kv])
    p_mix = parts_of('w_mix_out', _dwcall("dw_mix_out", merged, do1, BF16))

    def merge_bwd_body(do_ref, wm_ref, zga_ref, zgb_ref, pa_ref, pb_ref, wb_ref, dpa_ref, dpb_ref, dza_ref, dzb_ref,
                       dys_ref):
        dm = _dot_nt(do_ref[...], wm_ref[...])
        sa, sb = _sigmoid(zga_ref[...].astype(F32)), _sigmoid(zgb_ref[...].astype(F32))
        dpb = (dm * sb).astype(BF16)
        dpa_ref[...] = (dm * sa).astype(BF16)
        dpb_ref[...] = dpb
        dza_ref[...] = (dm * pa_ref[...].astype(F32) * (sa * (1.0 - sa))).astype(BF16)
        dzb_ref[...] = (dm * pb_ref[...].astype(F32) * (sb * (1.0 - sb))).astype(BF16)
        dys_ref[...] = _dot_nt(dpb, wb_ref[...]).astype(BF16)

    NORM_ROWS = ['g_ffn_post', 'g_ffn_pre', 'g_ca_post', 'g_mem', 'g_ca_pre', 'g_mix_post']
    norm_rows = jnp.concatenate([G[k] for k in NORM_ROWS] + [jnp.tile(loss_part, (1, d // 128))], axis=0)
    dp_a, dp_b, dz_ga, dz_gb, dy_s5, all_norm_rows, RECV['w_mix_out'] = _rowcall(
        "merge_bwd", merge_bwd_body, n, tm,
        [(do1, 'row'), (w_mix_f, 'full'), (z_ga, 'row'), (z_gb, 'row'), (p_a, 'row'), (p_b, 'row'), (w_brs5_f, 'full')],
        [(d, BF16), (d, BF16), (d, BF16), (d, BF16), (dh, BF16)], scatter=[p_mix], gather=[norm_rows])
    p_brgm = parts_of('w_br_gm', _dwcall("dw_br_gm", y_gm, dp_a, BF16))
    p_brs5 = parts_of('w_br_s5', _dwcall("dw_br_s5", y_s5, dp_b, BF16))

    def gm_bwd_body(z_ref, dpa_ref, wa_ref, lng_ref, lnb_ref, ws_ref, wst_ref, bias_ref, dz_ref, dws_ref, dbias_ref,
                    dlng_ref, dlnb_ref, du_s, dvn_s):
        i = pl.program_id(0)

        @pl.when(i == 0)
        def _():
            dws_ref[...] = jnp.zeros_like(dws_ref)
            dbias_ref[...] = jnp.zeros_like(dbias_ref)
            dlng_ref[...] = jnp.zeros_like(dlng_ref)
            dlnb_ref[...] = jnp.zeros_like(dlnb_ref)

        u, vhat, rstd, vnb = gm_recompute(z_ref, lng_ref, lnb_ref)
        dy = _dot_nt(dpa_ref[...], wa_ref[...])
        for g in range(GM_GROUPS):
            sl = slice(g * gw, (g + 1) * gw)
            sv = _dot(ws_ref[g], vnb[:, sl]) + bias_ref[:, sl]
            du_s[:, sl] = dy[:, sl] * sv
            dsv = dy[:, sl] * u[:, sl]
            dsvb = dsv.astype(BF16)
            dvn_s[:, sl] = _dot(wst_ref[g], dsvb)
            dws_ref[g] += _dot_nt(dsvb, vnb[:, sl])
            dbias_ref[:, sl] += dsv
        dvn = dvn_s[...]
        dlng_ref[...] += jnp.sum(dvn * vhat, axis=0, keepdims=True)
        dlnb_ref[...] += jnp.sum(dvn, axis=0, keepdims=True)
        dvh = dvn * lng_ref[...]
        dv = rstd * (dvh - jnp.mean(dvh, axis=-1, keepdims=True) - vhat * jnp.mean(dvh * vhat, axis=-1, keepdims=True))
        dz_ref[:, :d] = (du_s[...] * _gelu_grad(z_ref[:, :d].astype(F32))).astype(BF16)
        dz_ref[:, d:] = (dv * _gelu_grad(z_ref[:, d:].astype(F32))).astype(BF16)

    dz_gm, dws_full, dbias_full, G['gm_ln_g'], G['gm_ln_b'], RECV['w_br_gm'], RECV['w_br_s5'] = _rowcall(
        "gmlp_bwd", gm_bwd_body, n, GM_CHUNK,
        [(z_gm, 'row'), (dp_a, 'row'), (w_brgm_f, 'full'), (row(W['gm_ln_g']), 'full'), (row(W['gm_ln_b']), 'full'),
         (w_s, 'full'), (w_s_t, 'full'), (gm_bias, 'full')],
        [(d2, BF16)], accs=[((GM_GROUPS, GM_CHUNK, GM_CHUNK), F32), ((GM_CHUNK, d), F32), ((1, d), F32), ((1, d), F32)],
        scratch=[pltpu.VMEM((GM_CHUNK, d), F32), pltpu.VMEM((GM_CHUNK, d), F32)], scatter=[p_brgm, p_brs5])
    G['gm_w_s'] = jnp.where(tril[None], dws_full, 0.0)
    G['gm_b_s'] = dbias_full.reshape(GM_CHUNK, GM_GROUPS, gw).sum(-1).T

    dy_s5p = to_scan_order(dy_s5)

    def s5_out_bwd_body(dy_ref, ypre_ref, gate_ref, u_ref, sr_ref, si_ref, wg_ref, cr_ref, ci_ref, dyp_ref, dgate_ref,
                        yg_ref, gr_ref, gi_ref, dd_ref, dcr_ref, dci_ref):
        i = pl.program_id(0)

        @pl.when(i == 0)
        def _():
            dd_ref[...] = jnp.zeros_like(dd_ref)
            dcr_ref[...] = jnp.zeros_like(dcr_ref)
            dci_ref[...] = jnp.zeros_like(dci_ref)

        dy = dy_ref[...].astype(F32)
        ypre = ypre_ref[...]
        yg = _gelu(ypre)
        sg = _sigmoid(gate_ref[...])
        dgb = (dy * yg * (sg * (1.0 - sg))).astype(BF16)
        dgate_ref[...] = dgb
        yg_ref[...] = yg.astype(BF16)
        dyp = (dy * sg + _dot_nt(dgb, wg_ref[...])) * _gelu_grad(ypre)
        dyp_ref[...] = dyp
        dd_ref[...] += jnp.sum(dyp * u_ref[...], axis=0, keepdims=True)
        dypb = dyp.astype(BF16)
        for q in range(n_blk):
            rs, cs = slice(q * nq, (q + 1) * nq), slice(q * kq, (q + 1) * kq)
            gr_ref[:, rs] = _dot_nt(dypb[:, cs], cr_ref[q])
            gi_ref[:, rs] = -_dot_nt(dypb[:, cs], ci_ref[q])
            dcr_ref[q] += _dot_tn(sr_ref[:, rs].astype(BF16), dypb[:, cs])
            dci_ref[q] -= _dot_tn(si_ref[:, rs].astype(BF16), dypb[:, cs])

    gm_ln_rows = jnp.concatenate([G['gm_ln_g'], G['gm_ln_b']], axis=0)
    dy_pre, dgate, yg_b, gin_re, gin_im, dd, d_cc_re, d_cc_im, all_gm_w_s, all_gm_b_s, all_gm_ln = _rowcall(
        "s5_out_bwd", s5_out_bwd_body, n, tm,
        [(dy_s5p, 'row'), (y_pre, 'row'), (gate, 'row'), (u5p, 'row'), (s_re, 'row'), (s_im, 'row'), (w_glu_f, 'full'),
         (cc_re_b, 'full'), (cc_im_b, 'full')],
        [(dh, F32), (dh, BF16), (dh, BF16), (lt, F32), (lt, F32)],
        accs=[((1, dh), F32), ((n_blk, nq, kq), F32), ((n_blk, nq, kq), F32)],
        gather=[G['gm_w_s'], G['gm_b_s'], gm_ln_rows])
    p_glu = parts_of('s5_w_glu', _dwcall("dw_s5_glu", yg_b, dgate, BF16))
    lane_shape = (S5_GROUPS, 8, 128)
    g_c_re, g_c_im = (t.reshape(lane_shape) for t in s5_out_vjp((d_cc_re, d_cc_im)))
    g_d = dd.reshape(S5_GROUPS, S5_CH)

    gs_re4, gs_im4, da_re, da_im = _scan("s5_scan_bwd", gin_re.reshape(shape4), gin_im.reshape(shape4), a_re, -a_im,
                                         True, s_re4, s_im4)
    gs_re, gs_im = gs_re4.reshape(n, lt), gs_im4.reshape(n, lt)
    def s5_in_bwd_body(gr_ref, gi_ref, dyp_ref, u_ref, br_ref, bi_ref, d_ref, du_ref, dbr_ref, dbi_ref):
        i = pl.program_id(0)

        @pl.when(i == 0)
        def _():
            dbr_ref[...] = jnp.zeros_like(dbr_ref)
            dbi_ref[...] = jnp.zeros_like(dbi_ref)

        grb, gib, ub = gr_ref[...].astype(BF16), gi_ref[...].astype(BF16), u_ref[...].astype(BF16)
        for q in range(n_blk):
            rs, cs = slice(q * kq, (q + 1) * kq), slice(q * nq, (q + 1) * nq)
            du = _dot_nt(grb[:, cs], br_ref[q]) + _dot_nt(gib[:, cs], bi_ref[q])
            du_ref[:, rs] = (du + d_ref[:, rs] * dyp_ref[:, rs]).astype(BF16)
            dbr_ref[q] += _dot_tn(ub[:, rs], grb[:, cs])
            dbi_ref[q] += _dot_tn(ub[:, rs], gib[:, cs])

    du5p, d_bb_re, d_bb_im, all_c_re, all_c_im, all_d, RECV['s5_w_glu'] = _rowcall(
        "s5_in_bwd", s5_in_bwd_body, n, tm,
        [(gs_re, 'row'), (gs_im, 'row'), (dy_pre, 'row'), (u5p, 'row'), (bb_re_b, 'full'), (bb_im_b, 'full'),
         (s5_d_row, 'full')], [(dh, BF16)], accs=[((n_blk, kq, nq), F32), ((n_blk, kq, nq), F32)], scatter=[p_glu],
        gather=[g_c_re, g_c_im, g_d])
    du5 = from_scan_order(du5p)
    g_lam_re, g_lam_im, g_log_step, g_b_re, g_b_im = s5_in_vjp(
        (jnp.sum(da_re, axis=0), jnp.sum(da_im, axis=0), d_bb_re, d_bb_im))
    g_log_step, g_b_re, g_b_im = g_log_step.reshape(1, -1), g_b_re.reshape(lane_shape), g_b_im.reshape(lane_shape)

    pieces = [(dz_gm, 0, d2), (du5, d2, d2 + dh), (dz_ga, d2 + dh, d2 + dh + d), (dz_gb, d2 + dh + d, d2 + dh + 2 * d)]
    p_in = parts_of('w_in', jnp.concatenate(
        [_dwcall("dw_in_%d" % j, h0, p[0], BF16) for j, p in enumerate(pieces)], axis=1))
    grad_x2d, g_mix_pre_part, (all_lam_re, all_lam_im, all_log_step, all_b_re, all_b_im, RECV['w_in']) = open_sublayer(
        "in_proj_bwd", pieces, w_in_f, x2d, W['g_mix_pre'], dx1, scatter=[p_in],
        gather=[g_lam_re, g_lam_im, g_log_step, g_b_re, g_b_im])

    out = {k: [t[None] for t in _sum_adamw("sum_adamw_" + k, RECV[k], W[k], M[k], V[k])] for k in names}
    gathered = [all_norm_rows, all_gm_w_s, all_gm_b_s, all_gm_ln, all_c_re, all_c_im, all_d, all_lam_re, all_lam_im,
                all_log_step, all_b_re, all_b_im, _gather_last(g_mix_pre_part)]
    where = {'g_ffn_post': (0, 0), 'g_ffn_pre': (0, 1), 'g_ca_post': (0, 2), 'g_mem': (0, 3), 'g_ca_pre': (0, 4),
             'g_mix_post': (0, 5), 'gm_w_s': (1, None), 'gm_b_s': (2, None), 'gm_ln_g': (3, 0), 'gm_ln_b': (3, 1),
             's5_c_re': (4, None), 's5_c_im': (5, None), 's5_d': (6, None), 's5_lam_re': (7, None),
             's5_lam_im': (8, None), 's5_log_step': (9, 0), 's5_b_re': (10, None), 's5_b_im': (11, None),
             'g_mix_pre': (12, 0)}
    folded = ('s5_b_re', 's5_b_im', 's5_c_re', 's5_c_im')

    def as_updated(k, t):
        return t.reshape((1,) + lane_shape) if k in folded else t

    small, loss_row = _small_adamw(
        [(as_updated(k, args[k]), as_updated(k, args['m_' + k]), as_updated(k, args['v_' + k])) + where[k]
         for k in SMALL], gathered, (0, len(NORM_ROWS)))
    out.update({k: [t.reshape(args[k].shape) for t in quad] for k, quad in zip(SMALL, small)})

    res = [loss_row[0, 0], grad_x2d.reshape(x.shape)]
    for j in range(4):
        res += [out[k][j] for k in WEIGHTS]
    return tuple(res)
```

```python
import functools
import math

import jax
import jax.numpy as jnp
from jax import lax
from jax.experimental import pallas as pl
from jax.experimental.pallas import tpu as pltpu

F32 = jnp.float32
BF16 = jnp.bfloat16
EPS = 1e-6
N_DEV = 8
V7X_VMEM_LIMIT = 56 * 1024 * 1024
ROW_TILE = 256
GM_CHUNK = 128
GM_GROUPS = 8
S5_GROUPS = 32
S5_STATE = 64
S5_CH = 16
SCAN_CHUNKS = 32
SCAN_LANES = 128
CA_HEADS = 4
ADAM_LR, ADAM_B1, ADAM_B2, ADAM_EPS, ADAM_WD, ADAM_STEP = 0.001, 0.9, 0.999, 1e-08, 0.01, 10

WEIGHTS = ['g_mix_pre', 'w_in', 'gm_ln_g', 'gm_ln_b', 'gm_w_s', 'gm_b_s', 's5_lam_re', 's5_lam_im', 's5_log_step',
           's5_b_re', 's5_b_im', 's5_c_re', 's5_c_im', 's5_d', 's5_w_glu', 'w_br_gm', 'w_br_s5', 'w_mix_out',
           'g_mix_post', 'g_ca_pre', 'g_mem', 'ca_w_q', 'ca_w_kv', 'ca_w_o', 'g_ca_post', 'g_ffn_pre', 'ffn_w_gu',
           'ffn_w_down', 'g_ffn_post']
SHARDED = {'w_in': 'col', 's5_w_glu': 'row', 'w_br_gm': 'row', 'w_br_s5': 'col', 'w_mix_out': 'row',
           'ca_w_q': 'row', 'ca_w_kv': 'col', 'ca_w_o': 'row', 'ffn_w_gu': 'col', 'ffn_w_down': 'row'}
SMALL = [n for n in WEIGHTS if n not in SHARDED]


def _rms(x, g):
    r = lax.rsqrt(jnp.mean(x * x, axis=-1, keepdims=True) + EPS)
    return (x * r) * g


def _rms_bwd(x, g, dy):
    r = lax.rsqrt(jnp.mean(x * x, axis=-1, keepdims=True) + EPS)
    n = x * r
    dn = dy * g
    dx = r * (dn - n * jnp.mean(dn * n, axis=-1, keepdims=True))
    return dx, jnp.sum(dy * n, axis=0, keepdims=True)


_GELU_C = math.sqrt(2.0 / math.pi)


def _gelu(x):
    return 0.5 * x * (1.0 + jnp.tanh(_GELU_C * (x + 0.044715 * (x * x * x))))


def _gelu_and_grad(x):
    x2 = x * x
    t = jnp.tanh(_GELU_C * (x + 0.044715 * (x2 * x)))
    h = 0.5 * (1.0 + t)
    return x * h, h + 0.5 * x * (1.0 - t * t) * (_GELU_C * (1.0 + 3.0 * 0.044715 * x2))


def _sigmoid(x):
    return 0.5 * (1.0 + jnp.tanh(0.5 * x))


def _dot(a, b):
    return jnp.dot(a, b, preferred_element_type=F32)


def _dot_nt(a, b):
    return lax.dot_general(a, b, (((1,), (1,)), ((), ())), preferred_element_type=F32)


def _dot_tn(a, b):
    return lax.dot_general(a, b, (((0,), (0,)), ((), ())), preferred_element_type=F32)


def _adamw(w, g, m, v):
    m = ADAM_B1 * m + (1.0 - ADAM_B1) * g
    v = ADAM_B2 * v + (1.0 - ADAM_B2) * (g * g)
    m_hat = m / (1.0 - ADAM_B1 ** ADAM_STEP)
    v_hat = v / (1.0 - ADAM_B2 ** ADAM_STEP)
    delta = -ADAM_LR * (m_hat / (jnp.sqrt(v_hat) + ADAM_EPS) + ADAM_WD * w)
    return delta, m, v


def _params(n_grid):
    return pltpu.CompilerParams(dimension_semantics=("arbitrary",) * n_grid, vmem_limit_bytes=V7X_VMEM_LIMIT)


def _my_place():
    x, y, c = lax.axis_index("x"), lax.axis_index("y"), lax.axis_index("c")
    return x, y, c


def _peer(x, y, c, k):
    px = 1 - x if k & 4 else x
    py = 1 - y if k & 2 else y
    pc = 1 - c if k & 1 else c
    return (px, py, pc), 4 * px + 2 * py + pc


def _exchange(kind, src_refs, dst_refs, sems, first, phase):
    x, y, c = _my_place()
    me = 4 * x + 2 * y + c
    send_sems, recv_sems, own_sems = sems
    for j, (src, dst) in enumerate(zip(src_refs, dst_refs), start=first):
        own = pltpu.make_async_copy(src if kind == 'gather' else src.at[me], dst.at[me], own_sems.at[j])
        if phase == 'start':
            own.start()
        for k in range(1, N_DEV):
            peer, peer_block = _peer(x, y, c, k)
            out = pltpu.make_async_remote_copy(
                src_ref=src if kind == 'gather' else src.at[peer_block], dst_ref=dst.at[me],
                send_sem=send_sems.at[7 * j + k - 1], recv_sem=recv_sems.at[7 * j + k - 1], device_id=peer,
                device_id_type=pl.DeviceIdType.MESH)
            if phase == 'start':
                out.start()
            else:
                pltpu.make_async_remote_copy(
                    src_ref=src if kind == 'gather' else src.at[peer_block], dst_ref=dst.at[peer_block],
                    send_sem=send_sems.at[7 * j + k - 1], recv_sem=recv_sems.at[7 * j + k - 1], device_id=peer,
                    device_id_type=pl.DeviceIdType.MESH).wait_recv()
                out.wait_send()
        if phase == 'wait':
            own.wait()


class _Ride:
    def __init__(self, gather=(), scatter=()):
        self.n_gather = len(gather)
        self.arrays = list(gather) + list(scatter)
        n = len(self.arrays)
        hbm = pl.BlockSpec(memory_space=pl.ANY)
        self.in_specs = [hbm] * n
        self.out_specs = [hbm] * n
        self.out_shape = [jax.ShapeDtypeStruct((N_DEV,) + a.shape, a.dtype) for a in gather]
        self.out_shape += [jax.ShapeDtypeStruct(a.shape, a.dtype) for a in scatter]
        self.scratch = [pltpu.SemaphoreType.DMA((7 * n,)), pltpu.SemaphoreType.DMA((7 * n,)),
                        pltpu.SemaphoreType.DMA((n,))] if n else []

    def wrap(self, inner, n_in, n_out, n_scr, is_first, is_last):
        n_mv = len(self.arrays)
        if not n_mv:
            return inner

        def body(*refs):
            mv_src = refs[n_in:n_in + n_mv]
            mv_dst = refs[n_in + n_mv + n_out:n_in + 2 * n_mv + n_out]
            sems = refs[n_in + 2 * n_mv + n_out + n_scr:]

            def exchange(phase):
                for kind, lo, hi in (('gather', 0, self.n_gather), ('scatter', self.n_gather, n_mv)):
                    if hi > lo:
                        _exchange(kind, mv_src[lo:hi], mv_dst[lo:hi], sems, lo, phase)

            pl.when(is_first())(functools.partial(exchange, 'start'))
            inner(*refs[:n_in], *refs[n_in + n_mv:n_in + n_mv + n_out],
                  *refs[n_in + 2 * n_mv + n_out:n_in + 2 * n_mv + n_out + n_scr])
            pl.when(is_last())(functools.partial(exchange, 'wait'))

        return body


def _rowcall(name, body, n_rows, tm, ins, outs, accs=(), scratch=(), gather=(), scatter=()):
    n_steps = n_rows // tm
    ride = _Ride(gather, scatter)
    body = ride.wrap(body, len(ins), len(outs) + len(accs), len(scratch), lambda: pl.program_id(0) == 0,
                     lambda: pl.program_id(0) == n_steps - 1)
    arrays, in_specs = [], []
    for a, kind in ins:
        arrays.append(a)
        if kind == 'row':
            in_specs.append(pl.BlockSpec((tm,) + a.shape[1:], lambda i, nd=a.ndim: (i,) + (0,) * (nd - 1)))
        elif kind == 'full':
            in_specs.append(pl.BlockSpec(a.shape, lambda i, nd=a.ndim: (0,) * nd))
        else:
            in_specs.append(kind)
    out_shape, out_specs = [], []
    for cols, dt in outs:
        out_shape.append(jax.ShapeDtypeStruct((n_rows, cols), dt))
        out_specs.append(pl.BlockSpec((tm, cols), lambda i: (i, 0)))
    for shp, dt in accs:
        if isinstance(dt, tuple):
            dt, spec = dt
        else:
            spec = pl.BlockSpec(shp, lambda i, nd=len(shp): (0,) * nd)
        out_shape.append(jax.ShapeDtypeStruct(shp, dt))
        out_specs.append(spec)
    return pl.pallas_call(functools.partial(body), grid=(n_steps,), in_specs=in_specs + ride.in_specs,
                          out_specs=out_specs + ride.out_specs, out_shape=out_shape + ride.out_shape,
                          scratch_shapes=list(scratch) + ride.scratch, name=name,
                          compiler_params=_params(1))(*arrays, *ride.arrays)


def _dw_tiles(n, ka, nn, a_itemsize):
    tn = max(t for t in range(128, min(nn, 1536) + 1, 128) if nn % t == 0)
    tm = min(n, 2048)
    while tm > 256 and tm * ka * a_itemsize > 6 * 1024 * 1024:
        tm //= 2
    return tm, tn


def _dwcall(name, a, dc, out_dtype, gather=(), scatter=()):
    n, ka = a.shape
    nn = dc.shape[1]
    tm, tn = _dw_tiles(n, ka, nn, a.dtype.itemsize)
    n_i, n_j = n // tm, nn // tn
    ride = _Ride(gather, scatter)

    def body(a_ref, dc_ref, o_ref, acc_ref):
        i = pl.program_id(1)

        @pl.when(i == 0)
        def _():
            acc_ref[...] = jnp.zeros_like(acc_ref)

        acc_ref[...] += _dot_tn(a_ref[...].astype(BF16), dc_ref[...].astype(BF16))

        @pl.when(i == n_i - 1)
        def _():
            o_ref[...] = acc_ref[...].astype(o_ref.dtype)

    body = ride.wrap(body, 2, 1, 1, lambda: (pl.program_id(0) == 0) & (pl.program_id(1) == 0),
                     lambda: (pl.program_id(0) == n_j - 1) & (pl.program_id(1) == n_i - 1))
    res = pl.pallas_call(
        body, grid=(n_j, n_i),
        in_specs=[pl.BlockSpec((tm, ka), lambda j, i: (i, 0)), pl.BlockSpec((tm, tn), lambda j, i: (i, j))] + ride.in_specs,
        out_specs=[pl.BlockSpec((ka, tn), lambda j, i: (0, j))] + ride.out_specs,
        out_shape=[jax.ShapeDtypeStruct((ka, nn), out_dtype)] + ride.out_shape,
        scratch_shapes=[pltpu.VMEM((ka, tn), F32)] + ride.scratch, name=name, compiler_params=_params(2))(
            a, dc, *ride.arrays)
    return res if ride.arrays else res[0]


def _first_gather(first, later):
    n = len(later)
    vm = pl.BlockSpec(memory_space=pltpu.VMEM)

    def body(*refs):
        first_ref, later_refs = refs[0], refs[1:1 + n]
        out_ref, cast_refs = refs[1 + n], refs[2 + n:2 + 2 * n]
        stage = refs[2 + 2 * n]
        sems = refs[3 + 2 * n:]
        send_sems, recv_sems, own_sems = sems
        stage[...] = first_ref[...].astype(BF16)
        x, y, c = _my_place()
        me, sibling = (x, y, c), (x, y, 1 - c)
        chips = [(1 - x, y), (x, 1 - y), (1 - x, 1 - y)]

        def rows(px, py, pc):
            return out_ref.at[4 * px + 2 * py + pc]

        def copy(k, block, to, src=None):
            return pltpu.make_async_remote_copy(
                src_ref=rows(*block) if src is None else src, dst_ref=rows(*block), send_sem=send_sems.at[k],
                recv_sem=recv_sems.at[k], device_id=to, device_id_type=pl.DeviceIdType.MESH)

        mine = pltpu.make_async_copy(stage, rows(*me), own_sems.at[0])
        mine.start()
        first_out = [copy(0, me, sibling, src=stage)]
        first_out += [copy(1 + j, me, (*chip, c), src=stage) for j, chip in enumerate(chips)]
        for cp in first_out:
            cp.start()
        for src, dst in zip(later_refs, cast_refs):
            dst[...] = src[...].astype(BF16)
        passed = [copy(4 + j, (*chip, c), sibling) for j, chip in enumerate(chips)]
        for j, chip in enumerate(chips):
            copy(1 + j, (*chip, c), me).wait_recv()
            passed[j].start()
        copy(0, sibling, me).wait_recv()
        for j, chip in enumerate(chips):
            copy(4 + j, (*chip, 1 - c), me).wait_recv()
        for cp in first_out + passed:
            cp.wait_send()
        mine.wait()

    return pl.pallas_call(
        body, out_shape=[jax.ShapeDtypeStruct((N_DEV,) + first.shape, BF16)]
        + [jax.ShapeDtypeStruct(s.shape, BF16) for s in later],
        in_specs=[vm] * (1 + n), out_specs=[pl.BlockSpec(memory_space=pl.ANY)] + [vm] * n,
        scratch_shapes=[pltpu.VMEM(first.shape, BF16), pltpu.SemaphoreType.DMA((7,)), pltpu.SemaphoreType.DMA((7,)),
                        pltpu.SemaphoreType.DMA((1,))],
        name="gather_first", compiler_params=pltpu.CompilerParams(vmem_limit_bytes=V7X_VMEM_LIMIT))(first, *later)


def _sum_adamw(name, recv, w, m, v):
    shp = w.shape

    def body(recv_ref, w_ref, m_ref, v_ref, g_ref, d_ref, nm_ref, nv_ref):
        g = recv_ref[0].astype(F32)
        for k in range(1, N_DEV):
            g = g + recv_ref[k].astype(F32)
        d, nm, nv = _adamw(w_ref[...], g, m_ref[...], v_ref[...])
        g_ref[...] = g
        d_ref[...] = d
        nm_ref[...] = nm
        nv_ref[...] = nv

    vm = pl.BlockSpec(memory_space=pltpu.VMEM)
    return pl.pallas_call(
        body, out_shape=[jax.ShapeDtypeStruct(shp, F32)] * 4, in_specs=[vm] * 4, out_specs=[vm] * 4,
        name=name, compiler_params=pltpu.CompilerParams(vmem_limit_bytes=V7X_VMEM_LIMIT))(recv, w, m, v)


def _gather_last(a):
    def body(a_ref, out_ref, send_sems, recv_sems, own_sems):
        sems = (send_sems, recv_sems, own_sems)
        _exchange('gather', [a_ref], [out_ref], sems, 0, 'start')
        _exchange('gather', [a_ref], [out_ref], sems, 0, 'wait')

    vm = pl.BlockSpec(memory_space=pltpu.VMEM)
    return pl.pallas_call(
        body, out_shape=jax.ShapeDtypeStruct((N_DEV,) + a.shape, a.dtype), in_specs=[vm], out_specs=vm,
        scratch_shapes=[pltpu.SemaphoreType.DMA((7,)), pltpu.SemaphoreType.DMA((7,)), pltpu.SemaphoreType.DMA((1,))],
        name="gather_last", compiler_params=pltpu.CompilerParams(vmem_limit_bytes=V7X_VMEM_LIMIT))(a)


def _small_adamw(entries, gathered, loss_at):
    n, ng = len(entries), len(gathered)

    def body(*refs):
        g_refs, wmv = refs[:ng], refs[ng:ng + 3 * n]
        outs, loss_out = refs[ng + 3 * n:ng + 7 * n], refs[ng + 7 * n]

        def total(ref, r):
            if r is None:
                t = ref[0]
                for k in range(1, N_DEV):
                    t = t + ref[k]
                return t[None]
            t = ref[0, r:r + 1, :]
            for k in range(1, N_DEV):
                t = t + ref[k, r:r + 1, :]
            return t

        for j, (_, _, _, gi, r) in enumerate(entries):
            g = total(g_refs[gi], r)
            d, nm, nv = _adamw(wmv[3 * j][...], g, wmv[3 * j + 1][...], wmv[3 * j + 2][...])
            for ref, val in zip(outs[4 * j:4 * j + 4], (g, d, nm, nv)):
                ref[...] = val
        loss_out[...] = total(g_refs[loss_at[0]], loss_at[1])[:, :128]

    vm = pl.BlockSpec(memory_space=pltpu.VMEM)
    out_shape, arrays = [], list(gathered)
    for w, m, v, _, _ in entries:
        out_shape += [jax.ShapeDtypeStruct(w.shape, F32)] * 4
        arrays += [w, m, v]
    out_shape.append(jax.ShapeDtypeStruct((1, 128), F32))
    res = pl.pallas_call(
        body, out_shape=out_shape, in_specs=[vm] * len(arrays), out_specs=[vm] * len(out_shape),
        name="small_adamw", compiler_params=pltpu.CompilerParams(vmem_limit_bytes=V7X_VMEM_LIMIT))(*arrays)
    return [res[4 * j:4 * j + 4] for j in range(n)], res[4 * n]


def _blockdiag8(t):
    g, per = t.shape[0], 8
    eye = jnp.eye(per, dtype=F32)
    t = t.reshape(g // per, per, t.shape[1], t.shape[2])
    return (t[:, :, :, None, :] * eye[None, :, None, :, None]).reshape(g // per, per * t.shape[2], per * t.shape[3])


def _s5_out_tables(c_re, c_im):
    return _blockdiag8(c_re.transpose(0, 2, 1)), _blockdiag8(c_im.transpose(0, 2, 1))


def _s5_in_tables(lam_re, lam_im, log_step, b_re, b_im):
    step = jnp.exp(log_step)[:, None]
    mag = jnp.exp(lam_re * step)
    ab_re = mag * jnp.cos(lam_im * step)
    ab_im = mag * jnp.sin(lam_im * step)
    den = lam_re * lam_re + lam_im * lam_im
    nr = ab_re - 1.0
    co_re = (nr * lam_re + ab_im * lam_im) / den
    co_im = (ab_im * lam_re - nr * lam_im) / den
    bb_re = co_re[..., None] * b_re - co_im[..., None] * b_im
    bb_im = co_re[..., None] * b_im + co_im[..., None] * b_re
    return (ab_re.reshape(1, -1), ab_im.reshape(1, -1), _blockdiag8(bb_re.transpose(0, 2, 1)),
            _blockdiag8(bb_im.transpose(0, 2, 1)))


def _scan_passes(xr_s, xi_s, ar_ref, ai_ref, cr_ref, ci_ref, reverse):
    tc, nc, ln = xr_s.shape
    n_sq = int(math.log2(tc))
    assert 2 ** n_sq == tc
    ar = jnp.broadcast_to(ar_ref[...], (nc, ln))
    ai = jnp.broadcast_to(ai_ref[...], (nc, ln))
    zero = jnp.zeros((nc, ln), F32)

    def at(t):
        return tc - 1 - t if reverse else t

    def local(t, carry):
        sr, si = carry
        j = at(t)
        nr = ar * sr - ai * si + xr_s[j]
        ni = ar * si + ai * sr + xi_s[j]
        xr_s[j] = nr
        xi_s[j] = ni
        return nr, ni

    lr, li = lax.fori_loop(0, tc, local, (zero, zero))
    pr, pi = ar_ref[...], ai_ref[...]
    for _ in range(n_sq):
        pr, pi = pr * pr - pi * pi, 2.0 * (pr * pi)
    cr_ref[...] = lr
    ci_ref[...] = li
    tr = jnp.zeros((1, ln), F32)
    ti = jnp.zeros((1, ln), F32)
    for c in (range(nc - 1, -1, -1) if reverse else range(nc)):
        l_r = cr_ref[c:c + 1, :]
        l_i = ci_ref[c:c + 1, :]
        cr_ref[c:c + 1, :] = tr
        ci_ref[c:c + 1, :] = ti
        tr, ti = pr * tr - pi * ti + l_r, pr * ti + pi * tr + l_i

    def second_pass(on_fixed):
        def fixup(t, carry):
            qr, qi, acc = carry
            j = at(t)
            qr, qi = ar * qr - ai * qi, ar * qi + ai * qr
            gr = xr_s[j] + qr
            gi = xi_s[j] + qi
            xr_s[j] = gr
            xi_s[j] = gi
            return qr, qi, on_fixed(j, gr, gi, acc)

        return fixup

    return second_pass, (cr_ref[...], ci_ref[...]), zero


def _s5_scan_fwd(u5p, a_re, a_im, bb_re, bb_im, cc_re, cc_im, nb):
    n, dh = u5p.shape
    n_blk, kq, nq = bb_re.shape
    rows = n // nb
    nc, ln = SCAN_CHUNKS, SCAN_LANES
    tc = rows // nc
    sub = nq // ln
    lt = n_blk * nq

    def body(u_ref, ar_ref, ai_ref, br_ref, bi_ref, cr_ref, ci_ref, sr_ref, si_ref, y_ref, xr_s, xi_s, car_r, car_i):
        ub = u_ref[...].astype(BF16)
        xr_s[...] = _dot(ub, br_ref[0]).reshape(tc, nc, ln)
        xi_s[...] = _dot(ub, bi_ref[0]).reshape(tc, nc, ln)

        def keep(j, gr, gi, acc):
            sr_ref[0, j] = gr.astype(BF16)
            si_ref[0, j] = gi.astype(BF16)
            return acc

        second_pass, start, zero = _scan_passes(xr_s, xi_s, ar_ref, ai_ref, car_r, car_i, False)
        lax.fori_loop(0, tc, second_pass(keep), (*start, zero))
        y = (_dot(xr_s[...].reshape(rows, ln).astype(BF16), cr_ref[0])
             - _dot(xi_s[...].reshape(rows, ln).astype(BF16), ci_ref[0]))

        @pl.when(pl.program_id(1) % sub == 0)
        def _():
            y_ref[...] = y

        @pl.when(pl.program_id(1) % sub != 0)
        def _():
            y_ref[...] += y

    row_blk = pl.BlockSpec((rows, kq), lambda b, l: (b, l // sub))
    s_blk = pl.BlockSpec((1, tc, nc, ln), lambda b, l: (b, 0, 0, l))
    a_spec = pl.BlockSpec((1, ln), lambda b, l: (0, l))
    in_map = pl.BlockSpec((1, kq, ln), lambda b, l: (l // sub, 0, l % sub))
    out_map = pl.BlockSpec((1, ln, kq), lambda b, l: (l // sub, l % sub, 0))
    return pl.pallas_call(
        body, grid=(nb, lt // ln), in_specs=[row_blk, a_spec, a_spec, in_map, in_map, out_map, out_map],
        out_specs=[s_blk, s_blk, row_blk],
        out_shape=[jax.ShapeDtypeStruct((nb, tc, nc, lt), BF16)] * 2 + [jax.ShapeDtypeStruct((n, dh), F32)],
        scratch_shapes=[pltpu.VMEM((tc, nc, ln), F32)] * 2 + [pltpu.VMEM((nc, ln), F32)] * 2,
        name="s5_scan_fwd", compiler_params=_params(2))(u5p, a_re, a_im, bb_re, bb_im, cc_re, cc_im)


def _s5_scan_bwd(dy_pre, u5p, s_re, s_im, a_re, a_im, bb_re, bb_im, cc_re, cc_im, d_row):
    n, dh = u5p.shape
    n_blk, kq, nq = bb_re.shape
    nb, tc, nc, lt = s_re.shape
    rows = n // nb
    ln = SCAN_LANES
    sub = nq // ln

    def body(dy_ref, u_ref, sr_ref, si_ref, ar_ref, ai_ref, br_ref, bi_ref, cr_ref, ci_ref, d_ref,
             du_ref, dbr_ref, dbi_ref, dcr_ref, dci_ref, dar_ref, dai_ref, xr_s, xi_s, car_r, car_i):
        dyb = dy_ref[...].astype(BF16)
        xr_s[...] = _dot_nt(dyb, cr_ref[0]).reshape(tc, nc, ln)
        xi_s[...] = -_dot_nt(dyb, ci_ref[0]).reshape(tc, nc, ln)

        def with_state_before(j, gr, gi, acc):
            dr, di = acc
            pr_, pi_ = sr_ref[0, j - 1].astype(F32), si_ref[0, j - 1].astype(F32)
            return dr + (pr_ * gr + pi_ * gi), di + (pr_ * gi - pi_ * gr)

        second_pass, start, zero = _scan_passes(xr_s, xi_s, ar_ref, ai_ref, car_r, car_i, True)
        carry = lax.fori_loop(0, tc - 1, second_pass(with_state_before), (*start, (zero, zero)))
        row = lax.broadcasted_iota(jnp.int32, (nc, ln), 0)
        pr_ = jnp.where(row == 0, 0.0, pltpu.roll(sr_ref[0, tc - 1].astype(F32), 1, 0))
        pi_ = jnp.where(row == 0, 0.0, pltpu.roll(si_ref[0, tc - 1].astype(F32), 1, 0))

        def at_first_time(j, gr, gi, acc):
            return acc[0] + (pr_ * gr + pi_ * gi), acc[1] + (pr_ * gi - pi_ * gr)

        _, _, (dr, di) = second_pass(at_first_time)(tc - 1, carry)
        dar_ref[0] = jnp.sum(dr, axis=0, keepdims=True)
        dai_ref[0] = jnp.sum(di, axis=0, keepdims=True)

        grb = xr_s[...].reshape(rows, ln).astype(BF16)
        gib = xi_s[...].reshape(rows, ln).astype(BF16)
        ub = u_ref[...].astype(BF16)
        du = _dot_nt(grb, br_ref[0]) + _dot_nt(gib, bi_ref[0])

        @pl.when(pl.program_id(1) % sub == 0)
        def _():
            du_ref[...] = du + d_ref[...] * dy_ref[...]

        @pl.when(pl.program_id(1) % sub != 0)
        def _():
            du_ref[...] += du

        dbr_ref[0, 0] = _dot_tn(ub, grb)
        dbi_ref[0, 0] = _dot_tn(ub, gib)
        dcr_ref[0, 0] = _dot_tn(sr_ref[0].reshape(rows, ln), dyb)
        dci_ref[0, 0] = -_dot_tn(si_ref[0].reshape(rows, ln), dyb)

    row_blk = pl.BlockSpec((rows, kq), lambda b, l: (b, l // sub))
    s_blk = pl.BlockSpec((1, tc, nc, ln), lambda b, l: (b, 0, 0, l))
    a_spec = pl.BlockSpec((1, ln), lambda b, l: (0, l))
    in_map = pl.BlockSpec((1, kq, ln), lambda b, l: (l // sub, 0, l % sub))
    out_map = pl.BlockSpec((1, ln, kq), lambda b, l: (l // sub, l % sub, 0))
    d_spec = pl.BlockSpec((1, kq), lambda b, l: (0, l // sub))
    da_spec = pl.BlockSpec((1, 1, ln), lambda b, l: (b, 0, l))
    return pl.pallas_call(
        body, grid=(nb, lt // ln),
        in_specs=[row_blk, row_blk, s_blk, s_blk, a_spec, a_spec, in_map, in_map, out_map, out_map, d_spec],
        out_specs=[row_blk,
                   pl.BlockSpec((1, 1, kq, ln), lambda b, l: (b, l // sub, 0, l % sub)),
                   pl.BlockSpec((1, 1, kq, ln), lambda b, l: (b, l // sub, 0, l % sub)),
                   pl.BlockSpec((1, 1, ln, kq), lambda b, l: (b, l // sub, l % sub, 0)),
                   pl.BlockSpec((1, 1, ln, kq), lambda b, l: (b, l // sub, l % sub, 0)), da_spec, da_spec],
        out_shape=[jax.ShapeDtypeStruct((n, dh), F32)] + [jax.ShapeDtypeStruct((nb, n_blk, kq, nq), F32)] * 2
        + [jax.ShapeDtypeStruct((nb, n_blk, nq, kq), F32)] * 2 + [jax.ShapeDtypeStruct((nb, 1, lt), F32)] * 2,
        scratch_shapes=[pltpu.VMEM((tc, nc, ln), F32)] * 2 + [pltpu.VMEM((nc, ln), F32)] * 2,
        name="s5_scan_bwd", compiler_params=_params(2))(dy_pre, u5p, s_re, s_im, a_re, a_im, bb_re, bb_im, cc_re, cc_im,
                                                        d_row)


def kernel(x, mem, g_mix_pre, w_in, gm_ln_g, gm_ln_b, gm_w_s, gm_b_s, s5_lam_re, s5_lam_im, s5_log_step, s5_b_re, s5_b_im, s5_c_re, s5_c_im, s5_d, s5_w_glu, w_br_gm, w_br_s5, w_mix_out, g_mix_post, g_ca_pre, g_mem, ca_w_q, ca_w_kv, ca_w_o, g_ca_post, g_ffn_pre, ffn_w_gu, ffn_w_down, g_ffn_post, loss_target, m_g_mix_pre, m_w_in, m_gm_ln_g, m_gm_ln_b, m_gm_w_s, m_gm_b_s, m_s5_lam_re, m_s5_lam_im, m_s5_log_step, m_s5_b_re, m_s5_b_im, m_s5_c_re, m_s5_c_im, m_s5_d, m_s5_w_glu, m_w_br_gm, m_w_br_s5, m_w_mix_out, m_g_mix_post, m_g_ca_pre, m_g_mem, m_ca_w_q, m_ca_w_kv, m_ca_w_o, m_g_ca_post, m_g_ffn_pre, m_ffn_w_gu, m_ffn_w_down, m_g_ffn_post, v_g_mix_pre, v_w_in, v_gm_ln_g, v_gm_ln_b, v_gm_w_s, v_gm_b_s, v_s5_lam_re, v_s5_lam_im, v_s5_log_step, v_s5_b_re, v_s5_b_im, v_s5_c_re, v_s5_c_im, v_s5_d, v_s5_w_glu, v_w_br_gm, v_w_br_s5, v_w_mix_out, v_g_mix_post, v_g_ca_pre, v_g_mem, v_ca_w_q, v_ca_w_kv, v_ca_w_o, v_g_ca_post, v_g_ffn_pre, v_ffn_w_gu, v_ffn_w_down, v_g_ffn_post):
    args = locals()
    W = {n: args[n][0] for n in WEIGHTS}
    M = {n: args['m_' + n][0] for n in WEIGHTS}
    V = {n: args['v_' + n][0] for n in WEIGHTS}

    nb, seq, d = x.shape
    n = nb * seq
    tm = min(ROW_TILE, n)
    nmem = mem.shape[1]
    d2, dh = 2 * d, d // 2
    hd = d // CA_HEADS
    x2d = x.reshape(n, d)
    tgt = loss_target.reshape(n, d)
    mem2d = mem.reshape(nb * nmem, d)

    def row(v):
        return v.reshape(1, -1)

    names = list(SHARDED)
    later = [k for k in names if k != 'w_in']
    first_g, *casts = _first_gather(W['w_in'], [W[k] for k in later])
    shard_b = dict(zip(later, casts))
    half = shard_b['ffn_w_gu'].shape[0] // 2
    gu_halves = [shard_b['ffn_w_gu'][:half], shard_b['ffn_w_gu'][half:]]

    def assemble(k, gth):
        r, c = gth.shape[1:]
        return gth.reshape(N_DEV * r, c) if SHARDED[k] == 'row' else gth.transpose(1, 0, 2).reshape(r, N_DEV * c)

    w_in_f = assemble('w_in', first_g)
    ffh = N_DEV * W['ffn_w_down'].shape[0]

    (a_re, a_im, bb_re, bb_im), s5_in_vjp = jax.vjp(
        _s5_in_tables, W['s5_lam_re'], W['s5_lam_im'], W['s5_log_step'], W['s5_b_re'], W['s5_b_im'])
    (cc_re, cc_im), s5_out_vjp = jax.vjp(_s5_out_tables, W['s5_c_re'], W['s5_c_im'])
    bb_re_b, bb_im_b, cc_re_b, cc_im_b = (t.astype(BF16) for t in (bb_re, bb_im, cc_re, cc_im))
    s5_d_row = row(W['s5_d'])
    tril = jnp.tril(jnp.ones((GM_CHUNK, GM_CHUNK), bool))
    w_s = jnp.where(tril[None], W['gm_w_s'], 0.0).astype(BF16)
    w_s_t = w_s.transpose(0, 2, 1)
    gm_bias = jnp.repeat(W['gm_b_s'].T, d // GM_GROUPS, axis=1)

    def in_proj_body(x_ref, g_ref, w_ref, zgm_ref, u5_ref, zga_ref, zgb_ref, h_ref):
        hb = _rms(x_ref[...], g_ref[...]).astype(BF16)
        h_ref[...] = hb
        for lo in range(0, w_ref.shape[1], 512):
            acc = _dot(hb, w_ref[:, lo:lo + 512])
            if lo < d2:
                zgm_ref[:, lo:lo + 512] = acc.astype(BF16)
            elif lo < d2 + dh:
                u5_ref[...] = acc
            elif lo < d2 + dh + d:
                zga_ref[:, lo - d2 - dh:lo - d2 - dh + 512] = acc.astype(BF16)
            else:
                zgb_ref[:, lo - d2 - dh - d:lo - d2 - dh - d + 512] = acc.astype(BF16)

    ride = ['s5_w_glu', 'w_br_gm', 'w_br_s5', 'w_mix_out', 'ca_w_q']
    z_gm, u5, z_ga, z_gb, h0, *got = _rowcall(
        "in_proj", in_proj_body, n, tm, [(x2d, 'row'), (row(W['g_mix_pre']), 'full'), (w_in_f, 'full')],
        [(d2, BF16), (dh, F32), (d, BF16), (d, BF16), (d, BF16)], gather=[shard_b[k] for k in ride])
    w_glu_f, w_brgm_f, w_brs5_f, w_mix_f, w_q_f = (assemble(k, g) for k, g in zip(ride, got))

    def gm_recompute(z_ref, lng_ref, lnb_ref, grad_ref=None):
        if grad_ref is None:
            zg = _gelu(z_ref[...].astype(F32))
        else:
            zg, grad_ref[...] = _gelu_and_grad(z_ref[...].astype(F32))
        u, v = zg[:, :d], zg[:, d:]
        vc = v - jnp.mean(v, axis=-1, keepdims=True)
        rstd = lax.rsqrt(jnp.mean(vc * vc, axis=-1, keepdims=True) + EPS)
        vhat = vc * rstd
        vn = vhat * lng_ref[...] + lnb_ref[...]
        return u, vhat, rstd, vn.astype(BF16)

    gw = d // GM_GROUPS

    def gm_fwd_body(z_ref, lng_ref, lnb_ref, ws_ref, bias_ref, y_ref):
        u, _, _, vnb = gm_recompute(z_ref, lng_ref, lnb_ref)
        for g in range(GM_GROUPS):
            sl = slice(g * gw, (g + 1) * gw)
            sv = _dot(ws_ref[g], vnb[:, sl]) + bias_ref[:, sl]
            y_ref[:, sl] = (u[:, sl] * sv).astype(BF16)

    y_gm, g_kv, g_o = _rowcall("gmlp_fwd", gm_fwd_body, n, GM_CHUNK,
                               [(z_gm, 'row'), (row(W['gm_ln_g']), 'full'), (row(W['gm_ln_b']), 'full'), (w_s, 'full'),
                                (gm_bias, 'full')], [(d, BF16)], gather=[shard_b['ca_w_kv'], shard_b['ca_w_o']])
    w_kv_f, w_o_f = assemble('ca_w_kv', g_kv), assemble('ca_w_o', g_o)

    tc = seq // SCAN_CHUNKS
    lt = S5_GROUPS * S5_STATE

    def to_scan_order(t):
        return t.reshape(nb, SCAN_CHUNKS, tc, t.shape[-1]).transpose(0, 2, 1, 3).reshape(n, t.shape[-1])

    def from_scan_order(t):
        return t.reshape(nb, tc, SCAN_CHUNKS, t.shape[-1]).transpose(0, 2, 1, 3).reshape(n, t.shape[-1])

    u5p = to_scan_order(u5)
    s_re4, s_im4, y_lin = _s5_scan_fwd(u5p, a_re, a_im, bb_re_b, bb_im_b, cc_re_b, cc_im_b, nb)

    def s5_out_body(yl_ref, u_ref, d_ref, wg_ref, ypre_ref, gate_ref, y_ref):
        ypre = yl_ref[...] + d_ref[...] * u_ref[...]
        ypre_ref[...] = ypre
        yg = _gelu(ypre)
        gate = _dot(yg.astype(BF16), wg_ref[...])
        gate_ref[...] = gate
        y_ref[...] = (yg * _sigmoid(gate)).astype(BF16)

    tm5 = min(1024, n)
    y_pre, gate, y_s5p = _rowcall(
        "s5_out", s5_out_body, n, tm5, [(y_lin, 'row'), (u5p, 'row'), (s5_d_row, 'full'), (w_glu_f, 'full')],
        [(dh, F32), (dh, F32), (dh, BF16)])
    y_s5 = from_scan_order(y_s5p)

    def merge_body(ygm_ref, ys5_ref, zga_ref, zgb_ref, wa_ref, wb_ref, pa_ref, pb_ref, mg_ref):
        pa = _dot(ygm_ref[...], wa_ref[...])
        pb = _dot(ys5_ref[...], wb_ref[...])
        pa_ref[...] = pa.astype(BF16)
        pb_ref[...] = pb.astype(BF16)
        mg_ref[...] = (_sigmoid(zga_ref[...].astype(F32)) * pa + _sigmoid(zgb_ref[...].astype(F32)) * pb).astype(BF16)

    p_a, p_b, merged, g_down = _rowcall(
        "merge", merge_body, n, tm,
        [(y_gm, 'row'), (y_s5, 'row'), (z_ga, 'row'), (z_gb, 'row'), (w_brgm_f, 'full'), (w_brs5_f, 'full')],
        [(d, BF16), (d, BF16), (d, BF16)], gather=[shard_b['ffn_w_down']])
    w_down_f = assemble('ffn_w_down', g_down)

    def close_sublayer(name, a_in, w_out, x_res, g_post, g_next, w_next=None, gather=()):
        def body(*refs):
            a_ref, w_ref, x_ref, gp_ref, gn_ref = refs[:5]
            rest = refs[5:]
            if w_next is not None:
                wn_ref, rest = rest[0], rest[1:]
            o_ref, xo_ref, h_ref = rest[:3]
            o = _dot(a_ref[...], w_ref[...])
            o_ref[...] = o.astype(BF16)
            xo = x_ref[...] + _rms(o, gp_ref[...])
            xo_ref[...] = xo
            hb = _rms(xo, gn_ref[...]).astype(BF16)
            h_ref[...] = hb
            if w_next is not None:
                rest[3][...] = _dot(hb, wn_ref[...]).astype(BF16)

        ins = [(a_in, 'row'), (w_out, 'full'), (x_res, 'row'), (row(g_post), 'full'), (row(g_next), 'full')]
        outs = [(d, BF16), (d, F32), (d, BF16)]
        if w_next is not None:
            ins.append((w_next, 'full'))
            outs.append((w_next.shape[1], BF16))
        return _rowcall(name, body, n, tm, ins, outs, gather=gather)

    o1, x1, hc, q, g_gu0 = close_sublayer("mix_out", merged, w_mix_f, x2d, W['g_mix_post'], W['g_ca_pre'], w_q_f,
                                          gather=[gu_halves[0]])

    tmm = min(ROW_TILE, nb * nmem)

    def memkv_body(m_ref, g_ref, w_ref, mn_ref, k_ref, v_ref):
        mnb = _rms(m_ref[...], g_ref[...]).astype(BF16)
        mn_ref[...] = mnb
        k_ref[...] = _dot(mnb, w_ref[:, :d]).astype(BF16)
        v_ref[...] = _dot(mnb, w_ref[:, d:]).astype(BF16)

    mem_n, k_mem, v_mem = _rowcall("mem_kv", memkv_body, nb * nmem, tmm,
                                   [(mem2d, 'row'), (row(W['g_mem']), 'full'), (w_kv_f, 'full')],
                                   [(d, BF16), (d, BF16), (d, BF16)])

    tiles_per_ex = seq // tm
    kv_spec = pl.BlockSpec((nmem, d), lambda i: (i // tiles_per_ex, 0))
    scale = hd ** -0.5

    def softmax_rows(qh, kh):
        s = _dot_nt(qh, kh) * scale
        e = jnp.exp(s - jnp.max(s, axis=-1, keepdims=True))
        return e / jnp.sum(e, axis=-1, keepdims=True)

    def attn_body(q_ref, k_ref, v_ref, o_ref):
        for h in range(CA_HEADS):
            sl = slice(h * hd, (h + 1) * hd)
            p = softmax_rows(q_ref[:, sl], k_ref[:, sl])
            o_ref[:, sl] = _dot(p.astype(BF16), v_ref[:, sl]).astype(BF16)

    (att,) = _rowcall("attn_fwd", attn_body, n, tm, [(q, 'row'), (k_mem, kv_spec), (v_mem, kv_spec)], [(d, BF16)])

    o2, x2, hf, g_gu1 = close_sublayer("attn_out", att, w_o_f, x1, W['g_ca_post'], W['g_ffn_pre'],
                                       gather=[gu_halves[1]])
    w_gu_f = jnp.concatenate([assemble('ffn_w_gu', g_gu0), assemble('ffn_w_gu', g_gu1)], axis=0)

    ck = 256

    def ffn_up_body(h_ref, w_ref, gu_ref, a_ref):
        hb = h_ref[...]
        for lo in range(0, ffh, ck):
            gt = _dot(hb, w_ref[:, lo:lo + ck])
            ut = _dot(hb, w_ref[:, ffh + lo:ffh + lo + ck])
            gu_ref[:, lo:lo + ck] = gt.astype(BF16)
            gu_ref[:, ffh + lo:ffh + lo + ck] = ut.astype(BF16)
            a_ref[:, lo:lo + ck] = ((gt * _sigmoid(gt)) * ut).astype(BF16)

    gu, act = _rowcall("ffn_up", ffn_up_body, n, tm, [(hf, 'row'), (w_gu_f, 'full')], [(2 * ffh, BF16), (ffh, BF16)])

    def ffn_down_body(a_ref, w_ref, x_ref, t_ref, g_ref, dx_ref, do_ref, loss_ref, dg_ref):
        i = pl.program_id(0)

        @pl.when(i == 0)
        def _():
            loss_ref[...] = jnp.zeros_like(loss_ref)
            dg_ref[...] = jnp.zeros_like(dg_ref)

        o = _dot(a_ref[...], w_ref[...])
        diff = x_ref[...] + _rms(o, g_ref[...]) - t_ref[...]
        loss_ref[...] += jnp.full(loss_ref.shape, 0.5 / d, F32) * jnp.sum(diff * diff)
        dx = diff * (1.0 / d)
        dx_ref[...] = dx
        do, dg = _rms_bwd(o, g_ref[...], dx)
        do_ref[...] = do.astype(BF16)
        dg_ref[...] += dg

    dx3, do3, loss_part, dg_ffn_post = _rowcall(
        "ffn_down_loss", ffn_down_body, n, tm,
        [(act, 'row'), (w_down_f, 'full'), (x2, 'row'), (tgt, 'row'), (row(W['g_ffn_post']), 'full')],
        [(d, F32), (d, BF16)], accs=[((1, 128), F32), ((1, d), F32)])

    G = {'g_ffn_post': dg_ffn_post}
    RECV = {}

    def parts_of(k, gfull):
        r, c = W[k].shape
        return gfull.reshape(N_DEV, r, c) if SHARDED[k] == 'row' else gfull.reshape(r, N_DEV, c).transpose(1, 0, 2)

    p_down = parts_of('ffn_w_down', _dwcall("dw_ffn_down", act, do3, BF16))

    def ffn_act_bwd_body(do_ref, w_ref, gu_ref, dgu_ref):
        dob = do_ref[...]
        for lo in range(0, ffh, ck):
            da = _dot_nt(dob, w_ref[lo:lo + ck, :])
            gt = gu_ref[:, lo:lo + ck].astype(F32)
            ut = gu_ref[:, ffh + lo:ffh + lo + ck].astype(F32)
            sg = _sigmoid(gt)
            dgu_ref[:, lo:lo + ck] = (da * ut * (sg * (1.0 + gt * (1.0 - sg)))).astype(BF16)
            dgu_ref[:, ffh + lo:ffh + lo + ck] = (da * (gt * sg)).astype(BF16)

    dgu, RECV['ffn_w_down'] = _rowcall("ffn_act_bwd", ffn_act_bwd_body, n, tm,
                                       [(do3, 'row'), (w_down_f, 'full'), (gu, 'row')], [(2 * ffh, BF16)],
                                       scatter=[p_down])
    p_gu = parts_of('ffn_w_gu', _dwcall("dw_ffn_gu", hf, dgu, BF16))

    def open_sublayer(name, pieces, w_full, x_in, g_pre, dx_up, o_prev=None, g_post_prev=None, scatter=(), gather=()):
        n_p = len(pieces)
        second = o_prev is not None

        def body(*refs):
            dc_refs, (w_ref, x_ref, g_ref, dxu_ref), rest = refs[:n_p], refs[n_p:n_p + 4], refs[n_p + 4:]
            if second:
                (op_ref, gp_ref), rest = rest[:2], rest[2:]
            i = pl.program_id(0)
            dhid = None
            for dc_ref, (_, lo, hi) in zip(dc_refs, pieces):
                part = _dot_nt(dc_ref[...], w_ref[:, lo:hi])
                dhid = part if dhid is None else dhid + part
            dxn, dg = _rms_bwd(x_ref[...], g_ref[...], dhid)
            dx = dxu_ref[...] + dxn
            if second:
                dx_ref, do_ref, dg_ref, dg2_ref = rest
            else:
                dx_ref, dg_ref = rest

            @pl.when(i == 0)
            def _():
                dg_ref[...] = jnp.zeros_like(dg_ref)
                if second:
                    dg2_ref[...] = jnp.zeros_like(dg2_ref)

            dx_ref[...] = dx
            dg_ref[...] += dg
            if second:
                do, dg2 = _rms_bwd(op_ref[...].astype(F32), gp_ref[...], dx)
                do_ref[...] = do.astype(BF16)
                dg2_ref[...] += dg2

        ins = [(p[0], 'row') for p in pieces] + [(w_full, 'full'), (x_in, 'row'), (row(g_pre), 'full'), (dx_up, 'row')]
        outs = [(d, F32)]
        accs = [((1, d), F32)]
        if second:
            ins += [(o_prev, 'row'), (row(g_post_prev), 'full')]
            outs.append((d, BF16))
            accs.append(((1, d), F32))
        res = _rowcall(name, body, n, tm, ins, outs, accs=accs, scatter=scatter, gather=gather)
        if second:
            dx, do, dg, dg2 = res[:4]
            return dx, dg, do, dg2, res[4:]
        return res[0], res[1], res[2:]

    dx2, G['g_ffn_pre'], do2, G['g_ca_post'], (RECV['ffn_w_gu'],) = open_sublayer(
        "ffn_in_bwd", [(dgu, 0, 2 * ffh)], w_gu_f, x2, W['g_ffn_pre'], dx3, o2, W['g_ca_post'], scatter=[p_gu])
    p_o = parts_of('ca_w_o', _dwcall("dw_ca_o", att, do2, BF16))

    def attn_bwd_body(q_ref, k_ref, v_ref, do_ref, wo_ref, dq_ref, dk_ref, dv_ref):
        i = pl.program_id(0)

        @pl.when(i % tiles_per_ex == 0)
        def _():
            dk_ref[...] = jnp.zeros_like(dk_ref)
            dv_ref[...] = jnp.zeros_like(dv_ref)

        d_att = _dot_nt(do_ref[...], wo_ref[...]).astype(BF16)
        for h in range(CA_HEADS):
            sl = slice(h * hd, (h + 1) * hd)
            qh, kh, vh, dah = q_ref[:, sl], k_ref[:, sl], v_ref[:, sl], d_att[:, sl]
            p = softmax_rows(qh, kh)
            dp = _dot_nt(dah, vh)
            ds = (p * (dp - jnp.sum(p * dp, axis=-1, keepdims=True)) * scale).astype(BF16)
            dq_ref[:, sl] = _dot(ds, kh).astype(BF16)
            dk_ref[:, sl] += _dot_tn(ds, qh)
            dv_ref[:, sl] += _dot_tn(p.astype(BF16), dah)

    kv_acc = ((nb * nmem, d), (F32, pl.BlockSpec((nmem, d), lambda i: (i // tiles_per_ex, 0))))
    dq, dk_mem, dv_mem, RECV['ca_w_o'] = _rowcall(
        "attn_bwd", attn_bwd_body, n, tm,
        [(q, 'row'), (k_mem, kv_spec), (v_mem, kv_spec), (do2, 'row'), (w_o_f, 'full')], [(d, BF16)],
        accs=[kv_acc, kv_acc], scatter=[p_o])
    p_q = parts_of('ca_w_q', _dwcall("dw_ca_q", hc, dq, BF16))

    def memkv_bwd_body(dk_ref, dv_ref, m_ref, g_ref, w_ref, dkv_ref, dg_ref):
        i = pl.program_id(0)

        @pl.when(i == 0)
        def _():
            dg_ref[...] = jnp.zeros_like(dg_ref)

        dkb, dvb = dk_ref[...].astype(BF16), dv_ref[...].astype(BF16)
        dkv_ref[:, :d] = dkb
        dkv_ref[:, d:] = dvb
        dmn = _dot_nt(dkb, w_ref[:, :d]) + _dot_nt(dvb, w_ref[:, d:])
        _, dg = _rms_bwd(m_ref[...], g_ref[...], dmn)
        dg_ref[...] += dg

    dkv, G['g_mem'] = _rowcall(
        "mem_kv_bwd", memkv_bwd_body, nb * nmem, tmm,
        [(dk_mem, 'row'), (dv_mem, 'row'), (mem2d, 'row'), (row(W['g_mem']), 'full'), (w_kv_f, 'full')],
        [(d2, BF16)], accs=[((1, d), F32)])
    p_kv = parts_of('ca_w_kv', _dwcall("dw_ca_kv", mem_n, dkv, BF16))

    dx1, G['g_ca_pre'], do1, G['g_mix_post'], (RECV['ca_w_q'], RECV['ca_w_kv']) = open_sublayer(
        "attn_in_bwd", [(dq, 0, d)], w_q_f, x1, W['g_ca_pre'], dx2, o1, W['g_mix_post'], scatter=[p_q, p_kv])
    p_mix = parts_of('w_mix_out', _dwcall("dw_mix_out", merged, do1, BF16))

    def merge_bwd_body(do_ref, wm_ref, zga_ref, zgb_ref, pa_ref, pb_ref, wb_ref, dpa_ref, dpb_ref, dza_ref, dzb_ref,
                       dys_ref):
        dm = _dot_nt(do_ref[...], wm_ref[...])
        sa, sb = _sigmoid(zga_ref[...].astype(F32)), _sigmoid(zgb_ref[...].astype(F32))
        dpb = (dm * sb).astype(BF16)
        dpa_ref[...] = (dm * sa).astype(BF16)
        dpb_ref[...] = dpb
        dza_ref[...] = (dm * pa_ref[...].astype(F32) * (sa * (1.0 - sa))).astype(BF16)
        dzb_ref[...] = (dm * pb_ref[...].astype(F32) * (sb * (1.0 - sb))).astype(BF16)
        dys_ref[...] = _dot_nt(dpb, wb_ref[...]).astype(BF16)

    NORM_ROWS = ['g_ffn_post', 'g_ffn_pre', 'g_ca_post', 'g_mem', 'g_ca_pre', 'g_mix_post']
    norm_rows = jnp.concatenate([G[k] for k in NORM_ROWS] + [jnp.tile(loss_part, (1, d // 128))], axis=0)
    dp_a, dp_b, dz_ga, dz_gb, dy_s5, all_norm_rows, RECV['w_mix_out'] = _rowcall(
        "merge_bwd", merge_bwd_body, n, tm,
        [(do1, 'row'), (w_mix_f, 'full'), (z_ga, 'row'), (z_gb, 'row'), (p_a, 'row'), (p_b, 'row'), (w_brs5_f, 'full')],
        [(d, BF16), (d, BF16), (d, BF16), (d, BF16), (dh, BF16)], scatter=[p_mix], gather=[norm_rows])
    p_brgm = parts_of('w_br_gm', _dwcall("dw_br_gm", y_gm, dp_a, BF16))
    p_brs5 = parts_of('w_br_s5', _dwcall("dw_br_s5", y_s5, dp_b, BF16))

    def gm_bwd_body(z_ref, dpa_ref, wa_ref, lng_ref, lnb_ref, ws_ref, wst_ref, bias_ref, dz_ref, dws_ref, dbias_ref,
                    dlng_ref, dlnb_ref, du_s, dvn_s, dgelu_s):
        i = pl.program_id(0)

        @pl.when(i == 0)
        def _():
            dws_ref[...] = jnp.zeros_like(dws_ref)
            dbias_ref[...] = jnp.zeros_like(dbias_ref)
            dlng_ref[...] = jnp.zeros_like(dlng_ref)
            dlnb_ref[...] = jnp.zeros_like(dlnb_ref)

        u, vhat, rstd, vnb = gm_recompute(z_ref, lng_ref, lnb_ref, dgelu_s)
        dy = _dot_nt(dpa_ref[...], wa_ref[...])
        for g in range(GM_GROUPS):
            sl = slice(g * gw, (g + 1) * gw)
            sv = _dot(ws_ref[g], vnb[:, sl]) + bias_ref[:, sl]
            du_s[:, sl] = dy[:, sl] * sv
            dsv = dy[:, sl] * u[:, sl]
            dsvb = dsv.astype(BF16)
            dvn_s[:, sl] = _dot(wst_ref[g], dsvb)
            dws_ref[g] += _dot_nt(dsvb, vnb[:, sl])
            dbias_ref[:, sl] += dsv
        dvn = dvn_s[...]
        dlng_ref[...] += jnp.sum(dvn * vhat, axis=0, keepdims=True)
        dlnb_ref[...] += jnp.sum(dvn, axis=0, keepdims=True)
        dvh = dvn * lng_ref[...]
        dv = rstd * (dvh - jnp.mean(dvh, axis=-1, keepdims=True) - vhat * jnp.mean(dvh * vhat, axis=-1, keepdims=True))
        dz_ref[:, :d] = (du_s[...] * dgelu_s[:, :d]).astype(BF16)
        dz_ref[:, d:] = (dv * dgelu_s[:, d:]).astype(BF16)

    dz_gm, dws_full, dbias_full, G['gm_ln_g'], G['gm_ln_b'], RECV['w_br_gm'], RECV['w_br_s5'] = _rowcall(
        "gmlp_bwd", gm_bwd_body, n, GM_CHUNK,
        [(z_gm, 'row'), (dp_a, 'row'), (w_brgm_f, 'full'), (row(W['gm_ln_g']), 'full'), (row(W['gm_ln_b']), 'full'),
         (w_s, 'full'), (w_s_t, 'full'), (gm_bias, 'full')],
        [(d2, BF16)], accs=[((GM_GROUPS, GM_CHUNK, GM_CHUNK), F32), ((GM_CHUNK, d), F32), ((1, d), F32), ((1, d), F32)],
        scratch=[pltpu.VMEM((GM_CHUNK, d), F32), pltpu.VMEM((GM_CHUNK, d), F32), pltpu.VMEM((GM_CHUNK, d2), F32)],
        scatter=[p_brgm, p_brs5])
    G['gm_w_s'] = jnp.where(tril[None], dws_full, 0.0)
    G['gm_b_s'] = dbias_full.reshape(GM_CHUNK, GM_GROUPS, gw).sum(-1).T

    dy_s5p = to_scan_order(dy_s5)

    def s5_out_bwd_body(dy_ref, ypre_ref, gate_ref, u_ref, wg_ref, dyp_ref, dgate_ref, yg_ref, dd_ref):
        i = pl.program_id(0)

        @pl.when(i == 0)
        def _():
            dd_ref[...] = jnp.zeros_like(dd_ref)

        dy = dy_ref[...].astype(F32)
        yg, dgelu = _gelu_and_grad(ypre_ref[...])
        sg = _sigmoid(gate_ref[...])
        dgb = (dy * yg * (sg * (1.0 - sg))).astype(BF16)
        dgate_ref[...] = dgb
        yg_ref[...] = yg.astype(BF16)
        dyp = (dy * sg + _dot_nt(dgb, wg_ref[...])) * dgelu
        dyp_ref[...] = dyp
        dd_ref[...] += jnp.sum(dyp * u_ref[...], axis=0, keepdims=True)

    dy_pre, dgate, yg_b, dd = _rowcall(
        "s5_out_bwd", s5_out_bwd_body, n, tm5,
        [(dy_s5p, 'row'), (y_pre, 'row'), (gate, 'row'), (u5p, 'row'), (w_glu_f, 'full')],
        [(dh, F32), (dh, BF16), (dh, BF16)], accs=[((1, dh), F32)])
    du5p, d_bb_re, d_bb_im, d_cc_re, d_cc_im, da_re, da_im = _s5_scan_bwd(
        dy_pre, u5p, s_re4, s_im4, a_re, -a_im, bb_re_b, bb_im_b, cc_re_b, cc_im_b, s5_d_row)
    du5 = from_scan_order(du5p.astype(BF16))
    lane_shape = (S5_GROUPS, 8, 128)
    g_c_re, g_c_im = (t.reshape(lane_shape) for t in s5_out_vjp((jnp.sum(d_cc_re, axis=0), jnp.sum(d_cc_im, axis=0))))
    g_d = dd.reshape(S5_GROUPS, S5_CH)
    g_lam_re, g_lam_im, g_log_step, g_b_re, g_b_im = s5_in_vjp(
        (jnp.sum(da_re, axis=0), jnp.sum(da_im, axis=0), jnp.sum(d_bb_re, axis=0), jnp.sum(d_bb_im, axis=0)))
    g_log_step, g_b_re, g_b_im = g_log_step.reshape(1, -1), g_b_re.reshape(lane_shape), g_b_im.reshape(lane_shape)
    p_glu = parts_of('s5_w_glu', _dwcall("dw_s5_glu", yg_b, dgate, BF16))

    pieces = [(dz_gm, 0, d2), (du5, d2, d2 + dh), (dz_ga, d2 + dh, d2 + dh + d), (dz_gb, d2 + dh + d, d2 + dh + 2 * d)]
    gm_ln_rows = jnp.concatenate([G['gm_ln_g'], G['gm_ln_b']], axis=0)
    dw_gm, all_lam_re, all_lam_im, all_log_step, all_b_re, all_b_im = _dwcall(
        "dw_in_0", h0, dz_gm, BF16, gather=[g_lam_re, g_lam_im, g_log_step, g_b_re, g_b_im])
    dw_s5 = _dwcall("dw_in_1", h0, du5, BF16)
    dw_ga, all_gm_w_s, all_gm_b_s, all_gm_ln = _dwcall(
        "dw_in_2", h0, dz_ga, BF16, gather=[G['gm_w_s'], G['gm_b_s'], gm_ln_rows])
    dw_gb, all_c_re, all_c_im, all_d, RECV['s5_w_glu'] = _dwcall(
        "dw_in_3", h0, dz_gb, BF16, gather=[g_c_re, g_c_im, g_d], scatter=[p_glu])
    p_in = parts_of('w_in', jnp.concatenate([dw_gm, dw_s5, dw_ga, dw_gb], axis=1))
    grad_x2d, g_mix_pre_part, (RECV['w_in'],) = open_sublayer(
        "in_proj_bwd", pieces, w_in_f, x2d, W['g_mix_pre'], dx1, scatter=[p_in])

    out = {k: [t[None] for t in _sum_adamw("sum_adamw_" + k, RECV[k], W[k], M[k], V[k])] for k in names}
    gathered = [all_norm_rows, all_gm_w_s, all_gm_b_s, all_gm_ln, all_c_re, all_c_im, all_d, all_lam_re, all_lam_im,
                all_log_step, all_b_re, all_b_im, _gather_last(g_mix_pre_part)]
    where = {'g_ffn_post': (0, 0), 'g_ffn_pre': (0, 1), 'g_ca_post': (0, 2), 'g_mem': (0, 3), 'g_ca_pre': (0, 4),
             'g_mix_post': (0, 5), 'gm_w_s': (1, None), 'gm_b_s': (2, None), 'gm_ln_g': (3, 0), 'gm_ln_b': (3, 1),
             's5_c_re': (4, None), 's5_c_im': (5, None), 's5_d': (6, None), 's5_lam_re': (7, None),
             's5_lam_im': (8, None), 's5_log_step': (9, 0), 's5_b_re': (10, None), 's5_b_im': (11, None),
             'g_mix_pre': (12, 0)}
    folded = ('s5_b_re', 's5_b_im', 's5_c_re', 's5_c_im')

    def as_updated(k, t):
        return t.reshape((1,) + lane_shape) if k in folded else t

    small, loss_row = _small_adamw(
        [(as_updated(k, args[k]), as_updated(k, args['m_' + k]), as_updated(k, args['v_' + k])) + where[k]
         for k in SMALL], gathered, (0, len(NORM_ROWS)))
    out.update({k: [t.reshape(args[k].shape) for t in quad] for k, quad in zip(SMALL, small)})

    res = [loss_row[0, 0], grad_x2d.reshape(x.shape)]
    for j in range(4):
        res += [out[k][j] for k in WEIGHTS]
    return tuple(res)
```

```python
import functools
import math

import jax
import jax.numpy as jnp
from jax import lax
from jax.experimental import pallas as pl
from jax.experimental.pallas import tpu as pltpu

F32 = jnp.float32
BF16 = jnp.bfloat16
EPS = 1e-6
N_DEV = 8
V7X_VMEM_LIMIT = 56 * 1024 * 1024
ROW_TILE = 256
GM_CHUNK = 128
GM_GROUPS = 8
S5_GROUPS = 32
S5_STATE = 64
S5_CH = 16
SCAN_CHUNKS = 32
SCAN_LANES = 128
S5_BLOCK_LANES = 256
CA_HEADS = 4
ADAM_LR, ADAM_B1, ADAM_B2, ADAM_EPS, ADAM_WD, ADAM_STEP = 0.001, 0.9, 0.999, 1e-08, 0.01, 10

WEIGHTS = ['g_mix_pre', 'w_in', 'gm_ln_g', 'gm_ln_b', 'gm_w_s', 'gm_b_s', 's5_lam_re', 's5_lam_im', 's5_log_step',
           's5_b_re', 's5_b_im', 's5_c_re', 's5_c_im', 's5_d', 's5_w_glu', 'w_br_gm', 'w_br_s5', 'w_mix_out',
           'g_mix_post', 'g_ca_pre', 'g_mem', 'ca_w_q', 'ca_w_kv', 'ca_w_o', 'g_ca_post', 'g_ffn_pre', 'ffn_w_gu',
           'ffn_w_down', 'g_ffn_post']
SHARDED = {'w_in': 'colT', 's5_w_glu': 'row', 'w_br_gm': 'row', 'w_br_s5': 'col', 'w_mix_out': 'row',
           'ca_w_q': 'row', 'ca_w_kv': 'col', 'ca_w_o': 'row', 'ffn_w_gu': 'colT', 'ffn_w_down': 'row'}
SMALL = [n for n in WEIGHTS if n not in SHARDED]


def _rms(x, g):
    r = lax.rsqrt(jnp.mean(x * x, axis=-1, keepdims=True) + EPS)
    return (x * r) * g


def _rms_bwd(x, g, dy):
    r = lax.rsqrt(jnp.mean(x * x, axis=-1, keepdims=True) + EPS)
    n = x * r
    dn = dy * g
    dx = r * (dn - n * jnp.mean(dn * n, axis=-1, keepdims=True))
    return dx, jnp.sum(dy * n, axis=0, keepdims=True)


_GELU_C = math.sqrt(2.0 / math.pi)


def _gelu(x):
    return 0.5 * x * (1.0 + jnp.tanh(_GELU_C * (x + 0.044715 * (x * x * x))))


def _gelu_and_grad(x):
    x2 = x * x
    t = jnp.tanh(_GELU_C * (x + 0.044715 * (x2 * x)))
    h = 0.5 * (1.0 + t)
    return x * h, h + 0.5 * x * (1.0 - t * t) * (_GELU_C * (1.0 + 3.0 * 0.044715 * x2))


def _sigmoid(x):
    return 0.5 * (1.0 + jnp.tanh(0.5 * x))


def _dot(a, b):
    return jnp.dot(a, b, preferred_element_type=F32)


def _dot_nt(a, b):
    return lax.dot_general(a, b, (((1,), (1,)), ((), ())), preferred_element_type=F32)


def _dot_tn(a, b):
    return lax.dot_general(a, b, (((0,), (0,)), ((), ())), preferred_element_type=F32)


def _adamw(w, g, m, v):
    m = ADAM_B1 * m + (1.0 - ADAM_B1) * g
    v = ADAM_B2 * v + (1.0 - ADAM_B2) * (g * g)
    m_hat = m / (1.0 - ADAM_B1 ** ADAM_STEP)
    v_hat = v / (1.0 - ADAM_B2 ** ADAM_STEP)
    delta = -ADAM_LR * (m_hat / (jnp.sqrt(v_hat) + ADAM_EPS) + ADAM_WD * w)
    return delta, m, v


def _params(n_grid):
    return pltpu.CompilerParams(dimension_semantics=("arbitrary",) * n_grid, vmem_limit_bytes=V7X_VMEM_LIMIT)


def _my_place():
    x, y, c = lax.axis_index("x"), lax.axis_index("y"), lax.axis_index("c")
    return x, y, c


def _peer(x, y, c, k):
    px = 1 - x if k & 4 else x
    py = 1 - y if k & 2 else y
    pc = 1 - c if k & 1 else c
    return (px, py, pc), 4 * px + 2 * py + pc


def _exchange(kind, src_refs, dst_refs, sems, first, phase):
    x, y, c = _my_place()
    me = 4 * x + 2 * y + c
    send_sems, recv_sems, own_sems = sems
    for j, (src, dst) in enumerate(zip(src_refs, dst_refs), start=first):
        own = pltpu.make_async_copy(src if kind == 'gather' else src.at[me], dst.at[me], own_sems.at[j])
        if phase == 'start':
            own.start()
        for k in range(1, N_DEV):
            peer, peer_block = _peer(x, y, c, k)
            out = pltpu.make_async_remote_copy(
                src_ref=src if kind == 'gather' else src.at[peer_block], dst_ref=dst.at[me],
                send_sem=send_sems.at[7 * j + k - 1], recv_sem=recv_sems.at[7 * j + k - 1], device_id=peer,
                device_id_type=pl.DeviceIdType.MESH)
            if phase == 'start':
                out.start()
            else:
                pltpu.make_async_remote_copy(
                    src_ref=src if kind == 'gather' else src.at[peer_block], dst_ref=dst.at[peer_block],
                    send_sem=send_sems.at[7 * j + k - 1], recv_sem=recv_sems.at[7 * j + k - 1], device_id=peer,
                    device_id_type=pl.DeviceIdType.MESH).wait_recv()
                out.wait_send()
        if phase == 'wait':
            own.wait()


class _Ride:
    def __init__(self, gather=(), scatter=()):
        self.n_gather = len(gather)
        self.arrays = list(gather) + list(scatter)
        n = len(self.arrays)
        hbm = pl.BlockSpec(memory_space=pl.ANY)
        self.in_specs = [hbm] * n
        self.out_specs = [hbm] * n
        self.out_shape = [jax.ShapeDtypeStruct((N_DEV,) + a.shape, a.dtype) for a in gather]
        self.out_shape += [jax.ShapeDtypeStruct(a.shape, a.dtype) for a in scatter]
        self.scratch = [pltpu.SemaphoreType.DMA((7 * n,)), pltpu.SemaphoreType.DMA((7 * n,)),
                        pltpu.SemaphoreType.DMA((n,))] if n else []

    def wrap(self, inner, n_in, n_out, n_scr, is_first, is_last):
        n_mv = len(self.arrays)
        if not n_mv:
            return inner

        def body(*refs):
            mv_src = refs[n_in:n_in + n_mv]
            mv_dst = refs[n_in + n_mv + n_out:n_in + 2 * n_mv + n_out]
            sems = refs[n_in + 2 * n_mv + n_out + n_scr:]

            def exchange(phase):
                for kind, lo, hi in (('gather', 0, self.n_gather), ('scatter', self.n_gather, n_mv)):
                    if hi > lo:
                        _exchange(kind, mv_src[lo:hi], mv_dst[lo:hi], sems, lo, phase)

            pl.when(is_first())(functools.partial(exchange, 'start'))
            inner(*refs[:n_in], *refs[n_in + n_mv:n_in + n_mv + n_out],
                  *refs[n_in + 2 * n_mv + n_out:n_in + 2 * n_mv + n_out + n_scr])
            pl.when(is_last())(functools.partial(exchange, 'wait'))

        return body


def _rowcall(name, body, n_rows, tm, ins, outs, accs=(), scratch=(), gather=(), scatter=()):
    n_steps = n_rows // tm
    ride = _Ride(gather, scatter)
    body = ride.wrap(body, len(ins), len(outs) + len(accs), len(scratch), lambda: pl.program_id(0) == 0,
                     lambda: pl.program_id(0) == n_steps - 1)
    arrays, in_specs = [], []
    for a, kind in ins:
        arrays.append(a)
        if kind == 'row':
            in_specs.append(pl.BlockSpec((tm,) + a.shape[1:], lambda i, nd=a.ndim: (i,) + (0,) * (nd - 1)))
        elif kind == 'full':
            in_specs.append(pl.BlockSpec(a.shape, lambda i, nd=a.ndim: (0,) * nd))
        else:
            in_specs.append(kind)
    out_shape, out_specs = [], []
    for cols, dt in outs:
        out_shape.append(jax.ShapeDtypeStruct((n_rows, cols), dt))
        out_specs.append(pl.BlockSpec((tm, cols), lambda i: (i, 0)))
    for shp, dt in accs:
        if isinstance(dt, tuple):
            dt, spec = dt
        else:
            spec = pl.BlockSpec(shp, lambda i, nd=len(shp): (0,) * nd)
        out_shape.append(jax.ShapeDtypeStruct(shp, dt))
        out_specs.append(spec)
    return pl.pallas_call(functools.partial(body), grid=(n_steps,), in_specs=in_specs + ride.in_specs,
                          out_specs=out_specs + ride.out_specs, out_shape=out_shape + ride.out_shape,
                          scratch_shapes=list(scratch) + ride.scratch, name=name,
                          compiler_params=_params(1))(*arrays, *ride.arrays)


DW_ACC_BYTES = 12 * 1024 * 1024
DW_LHS_BYTES = 6 * 1024 * 1024


def _dw_tiles(n, ka, nn, a_itemsize):
    tn = max(t for t in range(128, min(nn, 1536) + 1, 128) if nn % t == 0)
    tka = max(t for t in range(128, ka + 1, 128) if ka % t == 0 and t * tn * 4 <= DW_ACC_BYTES)
    tm = min(n, 2048)
    while tm > 256 and tm * tka * a_itemsize > DW_LHS_BYTES:
        tm //= 2
    return tm, tka, tn


def _dwcall(name, a, dc, out_dtype, gather=(), scatter=()):
    n, ka = a.shape
    nn = dc.shape[1]
    tm, tka, tn = _dw_tiles(n, ka, nn, a.dtype.itemsize)
    n_i, n_j, n_k = n // tm, nn // tn, ka // tka
    ride = _Ride(gather, scatter)

    def body(a_ref, dc_ref, o_ref, acc_ref):
        i = pl.program_id(2)

        @pl.when(i == 0)
        def _():
            acc_ref[...] = jnp.zeros_like(acc_ref)

        acc_ref[...] += _dot_tn(a_ref[...].astype(BF16), dc_ref[...].astype(BF16))

        @pl.when(i == n_i - 1)
        def _():
            o_ref[...] = acc_ref[...].astype(o_ref.dtype)

    def at_step(k, j, i):
        return lambda: (pl.program_id(0) == k) & (pl.program_id(1) == j) & (pl.program_id(2) == i)

    body = ride.wrap(body, 2, 1, 1, at_step(0, 0, 0), at_step(n_k - 1, n_j - 1, n_i - 1))
    res = pl.pallas_call(
        body, grid=(n_k, n_j, n_i),
        in_specs=[pl.BlockSpec((tm, tka), lambda k, j, i: (i, k)), pl.BlockSpec((tm, tn), lambda k, j, i: (i, j))]
        + ride.in_specs,
        out_specs=[pl.BlockSpec((tka, tn), lambda k, j, i: (k, j))] + ride.out_specs,
        out_shape=[jax.ShapeDtypeStruct((ka, nn), out_dtype)] + ride.out_shape,
        scratch_shapes=[pltpu.VMEM((tka, tn), F32)] + ride.scratch, name=name, compiler_params=_params(3))(
            a, dc, *ride.arrays)
    return res if ride.arrays else res[0]


def _first_gather(first, later):
    n = len(later)
    vm = pl.BlockSpec(memory_space=pltpu.VMEM)

    def body(*refs):
        first_ref, later_refs = refs[0], refs[1:1 + n]
        out_ref, cast_refs = refs[1 + n], refs[2 + n:2 + 2 * n]
        stage = refs[2 + 2 * n]
        sems = refs[3 + 2 * n:]
        send_sems, recv_sems, own_sems = sems
        stage[...] = first_ref[...].astype(BF16)
        x, y, c = _my_place()
        me, sibling = (x, y, c), (x, y, 1 - c)
        chips = [(1 - x, y), (x, 1 - y), (1 - x, 1 - y)]

        def rows(px, py, pc):
            return out_ref.at[4 * px + 2 * py + pc]

        def copy(k, block, to, src=None):
            return pltpu.make_async_remote_copy(
                src_ref=rows(*block) if src is None else src, dst_ref=rows(*block), send_sem=send_sems.at[k],
                recv_sem=recv_sems.at[k], device_id=to, device_id_type=pl.DeviceIdType.MESH)

        mine = pltpu.make_async_copy(stage, rows(*me), own_sems.at[0])
        mine.start()
        first_out = [copy(0, me, sibling, src=stage)]
        first_out += [copy(1 + j, me, (*chip, c), src=stage) for j, chip in enumerate(chips)]
        for cp in first_out:
            cp.start()
        for src, dst in zip(later_refs, cast_refs):
            dst[...] = src[...].astype(BF16)
        passed = [copy(4 + j, (*chip, c), sibling) for j, chip in enumerate(chips)]
        for j, chip in enumerate(chips):
            copy(1 + j, (*chip, c), me).wait_recv()
            passed[j].start()
        copy(0, sibling, me).wait_recv()
        for j, chip in enumerate(chips):
            copy(4 + j, (*chip, 1 - c), me).wait_recv()
        for cp in first_out + passed:
            cp.wait_send()
        mine.wait()

    return pl.pallas_call(
        body, out_shape=[jax.ShapeDtypeStruct((N_DEV,) + first.shape, BF16)]
        + [jax.ShapeDtypeStruct(s.shape, BF16) for s in later],
        in_specs=[vm] * (1 + n), out_specs=[pl.BlockSpec(memory_space=pl.ANY)] + [vm] * n,
        scratch_shapes=[pltpu.VMEM(first.shape, BF16), pltpu.SemaphoreType.DMA((7,)), pltpu.SemaphoreType.DMA((7,)),
                        pltpu.SemaphoreType.DMA((1,))],
        name="gather_first", compiler_params=pltpu.CompilerParams(vmem_limit_bytes=V7X_VMEM_LIMIT))(first, *later)


def _sum_adamw(name, recv, w, m, v):
    shp = w.shape

    def body(recv_ref, w_ref, m_ref, v_ref, g_ref, d_ref, nm_ref, nv_ref):
        g = recv_ref[0].astype(F32)
        for k in range(1, N_DEV):
            g = g + recv_ref[k].astype(F32)
        d, nm, nv = _adamw(w_ref[...], g, m_ref[...], v_ref[...])
        g_ref[...] = g
        d_ref[...] = d
        nm_ref[...] = nm
        nv_ref[...] = nv

    vm = pl.BlockSpec(memory_space=pltpu.VMEM)
    return pl.pallas_call(
        body, out_shape=[jax.ShapeDtypeStruct(shp, F32)] * 4, in_specs=[vm] * 4, out_specs=[vm] * 4,
        name=name, compiler_params=pltpu.CompilerParams(vmem_limit_bytes=V7X_VMEM_LIMIT))(recv, w, m, v)


def _gather_last(a):
    def body(a_ref, out_ref, send_sems, recv_sems, own_sems):
        sems = (send_sems, recv_sems, own_sems)
        _exchange('gather', [a_ref], [out_ref], sems, 0, 'start')
        _exchange('gather', [a_ref], [out_ref], sems, 0, 'wait')

    vm = pl.BlockSpec(memory_space=pltpu.VMEM)
    return pl.pallas_call(
        body, out_shape=jax.ShapeDtypeStruct((N_DEV,) + a.shape, a.dtype), in_specs=[vm], out_specs=vm,
        scratch_shapes=[pltpu.SemaphoreType.DMA((7,)), pltpu.SemaphoreType.DMA((7,)), pltpu.SemaphoreType.DMA((1,))],
        name="gather_last", compiler_params=pltpu.CompilerParams(vmem_limit_bytes=V7X_VMEM_LIMIT))(a)


def _small_adamw(entries, gathered, loss_at):
    n, ng = len(entries), len(gathered)

    def body(*refs):
        g_refs, wmv = refs[:ng], refs[ng:ng + 3 * n]
        outs, loss_out = refs[ng + 3 * n:ng + 7 * n], refs[ng + 7 * n]

        def total(ref, r):
            if r is None:
                t = ref[0]
                for k in range(1, N_DEV):
                    t = t + ref[k]
                return t[None]
            t = ref[0, r:r + 1, :]
            for k in range(1, N_DEV):
                t = t + ref[k, r:r + 1, :]
            return t

        for j, (_, _, _, gi, r) in enumerate(entries):
            g = total(g_refs[gi], r)
            d, nm, nv = _adamw(wmv[3 * j][...], g, wmv[3 * j + 1][...], wmv[3 * j + 2][...])
            for ref, val in zip(outs[4 * j:4 * j + 4], (g, d, nm, nv)):
                ref[...] = val
        loss_out[...] = total(g_refs[loss_at[0]], loss_at[1])[:, :128]

    vm = pl.BlockSpec(memory_space=pltpu.VMEM)
    out_shape, arrays = [], list(gathered)
    for w, m, v, _, _ in entries:
        out_shape += [jax.ShapeDtypeStruct(w.shape, F32)] * 4
        arrays += [w, m, v]
    out_shape.append(jax.ShapeDtypeStruct((1, 128), F32))
    res = pl.pallas_call(
        body, out_shape=out_shape, in_specs=[vm] * len(arrays), out_specs=[vm] * len(out_shape),
        name="small_adamw", compiler_params=pltpu.CompilerParams(vmem_limit_bytes=V7X_VMEM_LIMIT))(*arrays)
    return [res[4 * j:4 * j + 4] for j in range(n)], res[4 * n]


def _blockdiag8(t):
    g, per = t.shape[0], 8
    eye = jnp.eye(per, dtype=F32)
    t = t.reshape(g // per, per, t.shape[1], t.shape[2])
    return (t[:, :, :, None, :] * eye[None, :, None, :, None]).reshape(g // per, per * t.shape[2], per * t.shape[3])


def _s5_out_tables(c_re, c_im):
    return _blockdiag8(c_re.transpose(0, 2, 1)), _blockdiag8(c_im.transpose(0, 2, 1))


def _s5_in_tables(lam_re, lam_im, log_step, b_re, b_im):
    step = jnp.exp(log_step)[:, None]
    mag = jnp.exp(lam_re * step)
    ab_re = mag * jnp.cos(lam_im * step)
    ab_im = mag * jnp.sin(lam_im * step)
    den = lam_re * lam_re + lam_im * lam_im
    nr = ab_re - 1.0
    co_re = (nr * lam_re + ab_im * lam_im) / den
    co_im = (ab_im * lam_re - nr * lam_im) / den
    bb_re = co_re[..., None] * b_re - co_im[..., None] * b_im
    bb_im = co_re[..., None] * b_im + co_im[..., None] * b_re
    return (ab_re.reshape(1, -1), ab_im.reshape(1, -1), _blockdiag8(bb_re.transpose(0, 2, 1)),
            _blockdiag8(bb_im.transpose(0, 2, 1)))


def _scan_passes(x_s, a_ref, cr_ref, ci_ref, reverse, re, im):
    tc, nc = x_s.shape[:2]
    ln = SCAN_LANES
    n_sq = int(math.log2(tc))
    assert 2 ** n_sq == tc
    ar = jnp.broadcast_to(a_ref[:, re], (nc, ln))
    ai = jnp.broadcast_to(a_ref[:, im], (nc, ln))
    zero = jnp.zeros((nc, ln), F32)

    def at(t):
        return tc - 1 - t if reverse else t

    def local(t, carry):
        sr, si = carry
        j = at(t)
        nr = ar * sr - ai * si + x_s[j, :, re]
        ni = ar * si + ai * sr + x_s[j, :, im]
        x_s[j, :, re] = nr
        x_s[j, :, im] = ni
        return nr, ni

    lr, li = lax.fori_loop(0, tc, local, (zero, zero))
    pr, pi = a_ref[:, re], a_ref[:, im]
    for _ in range(n_sq):
        pr, pi = pr * pr - pi * pi, 2.0 * (pr * pi)
    cr_ref[...] = lr
    ci_ref[...] = li
    tr = jnp.zeros((1, ln), F32)
    ti = jnp.zeros((1, ln), F32)
    for c in (range(nc - 1, -1, -1) if reverse else range(nc)):
        l_r = cr_ref[c:c + 1, :]
        l_i = ci_ref[c:c + 1, :]
        cr_ref[c:c + 1, :] = tr
        ci_ref[c:c + 1, :] = ti
        tr, ti = pr * tr - pi * ti + l_r, pr * ti + pi * tr + l_i

    def second_pass(on_fixed):
        def fixup(t, carry):
            qr, qi, acc = carry
            j = at(t)
            qr, qi = ar * qr - ai * qi, ar * qi + ai * qr
            gr = x_s[j, :, re] + qr
            gi = x_s[j, :, im] + qi
            x_s[j, :, re] = gr
            x_s[j, :, im] = gi
            return qr, qi, on_fixed(j, gr, gi, acc)

        return fixup

    return second_pass, (cr_ref[...], ci_ref[...]), zero


def _s5_cat_tables(a_re, a_im, bb_re, bb_im, cc_re, cc_im):
    lb = S5_BLOCK_LANES
    n_blk, kq, nq = bb_re.shape
    sub = nq // lb

    def lanes(re, im):
        lead = re.shape[:-1]
        both = jnp.stack([re.reshape(lead + (-1, lb)), im.reshape(lead + (-1, lb))], axis=-2)
        return both.reshape(lead + (-1,))

    cc = jnp.stack([cc_re.reshape(n_blk, sub, lb, kq), -cc_im.reshape(n_blk, sub, lb, kq)], axis=2)
    return lanes(a_re, a_im), lanes(bb_re, bb_im), cc.reshape(n_blk, sub * 2 * lb, kq)


def _s5_uncat(t, axis):
    lb = S5_BLOCK_LANES
    shp = t.shape
    t = t.reshape(shp[:axis] + (-1, 2, lb) + shp[axis + 1:])
    re, im = jnp.take(t, 0, axis=axis + 1), jnp.take(t, 1, axis=axis + 1)
    return re.reshape(shp[:axis] + (-1,) + shp[axis + 1:]), im.reshape(shp[:axis] + (-1,) + shp[axis + 1:])


def _s5_scan_fwd(u5p, a_cat, bb_cat, cc_cat, nb):
    n, dh = u5p.shape
    n_blk, kq, cols = bb_cat.shape
    lb, ln, nc = S5_BLOCK_LANES, SCAN_LANES, SCAN_CHUNKS
    sub = cols // (2 * lb)
    rows = n // nb
    tc = rows // nc
    n_l = n_blk * sub

    def body(u_ref, a_ref, b_ref, c_ref, s_ref, y_ref, x_s, car_r, car_i):
        x_s[...] = _dot(u_ref[...].astype(BF16), b_ref[0]).reshape(tc, nc, 2 * lb)
        for h in range(lb // ln):
            re, im = slice(h * ln, (h + 1) * ln), slice(lb + h * ln, lb + (h + 1) * ln)

            def keep(j, gr, gi, acc, re=re, im=im):
                s_ref[0, j, :, re] = gr.astype(BF16)
                s_ref[0, j, :, im] = gi.astype(BF16)
                return acc

            second_pass, start, zero = _scan_passes(x_s, a_ref, car_r, car_i, False, re, im)
            lax.fori_loop(0, tc, second_pass(keep), (*start, zero))
        y = _dot(x_s[...].reshape(rows, 2 * lb).astype(BF16), c_ref[0])

        @pl.when(pl.program_id(1) % sub == 0)
        def _():
            y_ref[...] = y

        @pl.when(pl.program_id(1) % sub != 0)
        def _():
            y_ref[...] += y

    row_blk = pl.BlockSpec((rows, kq), lambda b, l: (b, l // sub))
    return pl.pallas_call(
        body, grid=(nb, n_l),
        in_specs=[row_blk, pl.BlockSpec((1, 2 * lb), lambda b, l: (0, l)),
                  pl.BlockSpec((1, kq, 2 * lb), lambda b, l: (l // sub, 0, l % sub)),
                  pl.BlockSpec((1, 2 * lb, kq), lambda b, l: (l // sub, l % sub, 0))],
        out_specs=[pl.BlockSpec((1, tc, nc, 2 * lb), lambda b, l: (b, 0, 0, l)), row_blk],
        out_shape=[jax.ShapeDtypeStruct((nb, tc, nc, n_l * 2 * lb), BF16), jax.ShapeDtypeStruct((n, dh), F32)],
        scratch_shapes=[pltpu.VMEM((tc, nc, 2 * lb), F32)] + [pltpu.VMEM((nc, ln), F32)] * 2,
        name="s5_scan_fwd", compiler_params=_params(2))(u5p, a_cat, bb_cat, cc_cat)


def _s5_scan_bwd(dy_pre, u5p, s_cat, a_cat_conj, bb_cat, cc_cat, d_row):
    n, dh = u5p.shape
    n_blk, kq, cols = bb_cat.shape
    nb, tc, nc, _ = s_cat.shape
    lb, ln = S5_BLOCK_LANES, SCAN_LANES
    sub = cols // (2 * lb)
    rows = n // nb
    n_l = n_blk * sub

    def body(dy_ref, u_ref, s_ref, a_ref, b_ref, c_ref, d_ref, du_ref, db_ref, dc_ref, da_ref, x_s, car_r, car_i):
        dyb = dy_ref[...].astype(BF16)
        x_s[...] = _dot_nt(dyb, c_ref[0]).reshape(tc, nc, 2 * lb)
        for h in range(lb // ln):
            re, im = slice(h * ln, (h + 1) * ln), slice(lb + h * ln, lb + (h + 1) * ln)

            def with_state_before(j, gr, gi, acc, re=re, im=im):
                dr, di = acc
                pr_, pi_ = s_ref[0, j - 1, :, re].astype(F32), s_ref[0, j - 1, :, im].astype(F32)
                return dr + (pr_ * gr + pi_ * gi), di + (pr_ * gi - pi_ * gr)

            second_pass, start, zero = _scan_passes(x_s, a_ref, car_r, car_i, True, re, im)
            carry = lax.fori_loop(0, tc - 1, second_pass(with_state_before), (*start, (zero, zero)))
            row = lax.broadcasted_iota(jnp.int32, (nc, ln), 0)
            pr_ = jnp.where(row == 0, 0.0, pltpu.roll(s_ref[0, tc - 1, :, re].astype(F32), 1, 0))
            pi_ = jnp.where(row == 0, 0.0, pltpu.roll(s_ref[0, tc - 1, :, im].astype(F32), 1, 0))

            def at_first_time(j, gr, gi, acc, pr_=pr_, pi_=pi_):
                return acc[0] + (pr_ * gr + pi_ * gi), acc[1] + (pr_ * gi - pi_ * gr)

            _, _, (dr, di) = second_pass(at_first_time)(tc - 1, carry)
            da_ref[0, :, re] = jnp.sum(dr, axis=0, keepdims=True)
            da_ref[0, :, im] = jnp.sum(di, axis=0, keepdims=True)

        gb = x_s[...].reshape(rows, 2 * lb).astype(BF16)
        du = _dot_nt(gb, b_ref[0])

        @pl.when(pl.program_id(1) % sub == 0)
        def _():
            du_ref[...] = du + d_ref[...] * dy_ref[...]

        @pl.when(pl.program_id(1) % sub != 0)
        def _():
            du_ref[...] += du

        db_ref[0, 0] = _dot_tn(u_ref[...].astype(BF16), gb)
        dc_ref[0, 0] = _dot_tn(s_ref[0].reshape(rows, 2 * lb), dyb)

    row_blk = pl.BlockSpec((rows, kq), lambda b, l: (b, l // sub))
    in_map = pl.BlockSpec((1, kq, 2 * lb), lambda b, l: (l // sub, 0, l % sub))
    out_map = pl.BlockSpec((1, 2 * lb, kq), lambda b, l: (l // sub, l % sub, 0))
    return pl.pallas_call(
        body, grid=(nb, n_l),
        in_specs=[row_blk, row_blk, pl.BlockSpec((1, tc, nc, 2 * lb), lambda b, l: (b, 0, 0, l)),
                  pl.BlockSpec((1, 2 * lb), lambda b, l: (0, l)), in_map, out_map,
                  pl.BlockSpec((1, kq), lambda b, l: (0, l // sub))],
        out_specs=[row_blk, pl.BlockSpec((1, 1, kq, 2 * lb), lambda b, l: (b, l // sub, 0, l % sub)),
                   pl.BlockSpec((1, 1, 2 * lb, kq), lambda b, l: (b, l // sub, l % sub, 0)),
                   pl.BlockSpec((1, 1, 2 * lb), lambda b, l: (b, 0, l))],
        out_shape=[jax.ShapeDtypeStruct((n, dh), F32), jax.ShapeDtypeStruct((nb, n_blk, kq, cols), F32),
                   jax.ShapeDtypeStruct((nb, n_blk, cols, kq), F32), jax.ShapeDtypeStruct((nb, 1, n_l * 2 * lb), F32)],
        scratch_shapes=[pltpu.VMEM((tc, nc, 2 * lb), F32)] + [pltpu.VMEM((nc, ln), F32)] * 2,
        name="s5_scan_bwd", compiler_params=_params(2))(dy_pre, u5p, s_cat, a_cat_conj, bb_cat, cc_cat, d_row)


def kernel(x, mem, g_mix_pre, w_in, gm_ln_g, gm_ln_b, gm_w_s, gm_b_s, s5_lam_re, s5_lam_im, s5_log_step, s5_b_re, s5_b_im, s5_c_re, s5_c_im, s5_d, s5_w_glu, w_br_gm, w_br_s5, w_mix_out, g_mix_post, g_ca_pre, g_mem, ca_w_q, ca_w_kv, ca_w_o, g_ca_post, g_ffn_pre, ffn_w_gu, ffn_w_down, g_ffn_post, loss_target, m_g_mix_pre, m_w_in, m_gm_ln_g, m_gm_ln_b, m_gm_w_s, m_gm_b_s, m_s5_lam_re, m_s5_lam_im, m_s5_log_step, m_s5_b_re, m_s5_b_im, m_s5_c_re, m_s5_c_im, m_s5_d, m_s5_w_glu, m_w_br_gm, m_w_br_s5, m_w_mix_out, m_g_mix_post, m_g_ca_pre, m_g_mem, m_ca_w_q, m_ca_w_kv, m_ca_w_o, m_g_ca_post, m_g_ffn_pre, m_ffn_w_gu, m_ffn_w_down, m_g_ffn_post, v_g_mix_pre, v_w_in, v_gm_ln_g, v_gm_ln_b, v_gm_w_s, v_gm_b_s, v_s5_lam_re, v_s5_lam_im, v_s5_log_step, v_s5_b_re, v_s5_b_im, v_s5_c_re, v_s5_c_im, v_s5_d, v_s5_w_glu, v_w_br_gm, v_w_br_s5, v_w_mix_out, v_g_mix_post, v_g_ca_pre, v_g_mem, v_ca_w_q, v_ca_w_kv, v_ca_w_o, v_g_ca_post, v_g_ffn_pre, v_ffn_w_gu, v_ffn_w_down, v_g_ffn_post):
    args = locals()
    W = {n: args[n][0] for n in WEIGHTS}
    M = {n: args['m_' + n][0] for n in WEIGHTS}
    V = {n: args['v_' + n][0] for n in WEIGHTS}
    for k, kind in SHARDED.items():
        if kind == 'colT':
            W[k], M[k], V[k] = W[k].T, M[k].T, V[k].T

    nb, seq, d = x.shape
    n = nb * seq
    tm = min(ROW_TILE, n)
    nmem = mem.shape[1]
    d2, dh = 2 * d, d // 2
    hd = d // CA_HEADS
    x2d = x.reshape(n, d)
    tgt = loss_target.reshape(n, d)
    mem2d = mem.reshape(nb * nmem, d)

    def row(v):
        return v.reshape(1, -1)

    names = list(SHARDED)
    later = [k for k in names if k != 'w_in']
    first_g, *casts = _first_gather(W['w_in'], [W[k] for k in later])
    shard_b = dict(zip(later, casts))
    half = shard_b['ffn_w_gu'].shape[0] // 2
    gu_halves = [shard_b['ffn_w_gu'][:half], shard_b['ffn_w_gu'][half:]]

    def assemble(k, gth):
        r, c = gth.shape[1:]
        return gth.transpose(1, 0, 2).reshape(r, N_DEV * c) if SHARDED[k] == 'col' else gth.reshape(N_DEV * r, c)

    w_in_t = assemble('w_in', first_g)
    ffh = N_DEV * W['ffn_w_down'].shape[0]

    (a_re, a_im, bb_re, bb_im), s5_in_vjp = jax.vjp(
        _s5_in_tables, W['s5_lam_re'], W['s5_lam_im'], W['s5_log_step'], W['s5_b_re'], W['s5_b_im'])
    (cc_re, cc_im), s5_out_vjp = jax.vjp(_s5_out_tables, W['s5_c_re'], W['s5_c_im'])
    bb_re_b, bb_im_b, cc_re_b, cc_im_b = (t.astype(BF16) for t in (bb_re, bb_im, cc_re, cc_im))
    s5_d_row = row(W['s5_d'])
    tril = jnp.tril(jnp.ones((GM_CHUNK, GM_CHUNK), bool))
    w_s = jnp.where(tril[None], W['gm_w_s'], 0.0).astype(BF16)
    w_s_t = w_s.transpose(0, 2, 1)
    gm_bias = jnp.repeat(W['gm_b_s'].T, d // GM_GROUPS, axis=1)

    def in_proj_body(x_ref, g_ref, w_ref, zgm_ref, u5_ref, zga_ref, zgb_ref, h_ref):
        hb = _rms(x_ref[...], g_ref[...]).astype(BF16)
        h_ref[...] = hb
        for lo in range(0, w_ref.shape[0], 512):
            acc = _dot_nt(hb, w_ref[lo:lo + 512, :])
            if lo < d2:
                zgm_ref[:, lo:lo + 512] = acc.astype(BF16)
            elif lo < d2 + dh:
                u5_ref[...] = acc
            elif lo < d2 + dh + d:
                zga_ref[:, lo - d2 - dh:lo - d2 - dh + 512] = acc.astype(BF16)
            else:
                zgb_ref[:, lo - d2 - dh - d:lo - d2 - dh - d + 512] = acc.astype(BF16)

    ride = ['s5_w_glu', 'w_br_gm', 'w_br_s5', 'w_mix_out', 'ca_w_q']
    z_gm, u5, z_ga, z_gb, h0, *got = _rowcall(
        "in_proj", in_proj_body, n, tm, [(x2d, 'row'), (row(W['g_mix_pre']), 'full'), (w_in_t, 'full')],
        [(d2, BF16), (dh, F32), (d, BF16), (d, BF16), (d, BF16)], gather=[shard_b[k] for k in ride])
    w_glu_f, w_brgm_f, w_brs5_f, w_mix_f, w_q_f = (assemble(k, g) for k, g in zip(ride, got))

    def gm_recompute(z_ref, lng_ref, lnb_ref, grad_ref=None):
        if grad_ref is None:
            zg = _gelu(z_ref[...].astype(F32))
        else:
            zg, grad_ref[...] = _gelu_and_grad(z_ref[...].astype(F32))
        u, v = zg[:, :d], zg[:, d:]
        vc = v - jnp.mean(v, axis=-1, keepdims=True)
        rstd = lax.rsqrt(jnp.mean(vc * vc, axis=-1, keepdims=True) + EPS)
        vhat = vc * rstd
        vn = vhat * lng_ref[...] + lnb_ref[...]
        return u, vhat, rstd, vn.astype(BF16)

    gw = d // GM_GROUPS

    def gm_fwd_body(z_ref, lng_ref, lnb_ref, ws_ref, bias_ref, y_ref):
        u, _, _, vnb = gm_recompute(z_ref, lng_ref, lnb_ref)
        for g in range(GM_GROUPS):
            sl = slice(g * gw, (g + 1) * gw)
            sv = _dot(ws_ref[g], vnb[:, sl]) + bias_ref[:, sl]
            y_ref[:, sl] = (u[:, sl] * sv).astype(BF16)

    y_gm, g_kv, g_o = _rowcall("gmlp_fwd", gm_fwd_body, n, GM_CHUNK,
                               [(z_gm, 'row'), (row(W['gm_ln_g']), 'full'), (row(W['gm_ln_b']), 'full'), (w_s, 'full'),
                                (gm_bias, 'full')], [(d, BF16)], gather=[shard_b['ca_w_kv'], shard_b['ca_w_o']])
    w_kv_f, w_o_f = assemble('ca_w_kv', g_kv), assemble('ca_w_o', g_o)

    tc = seq // SCAN_CHUNKS
    lt = S5_GROUPS * S5_STATE

    def to_scan_order(t):
        return t.reshape(nb, SCAN_CHUNKS, tc, t.shape[-1]).transpose(0, 2, 1, 3).reshape(n, t.shape[-1])

    def from_scan_order(t):
        return t.reshape(nb, tc, SCAN_CHUNKS, t.shape[-1]).transpose(0, 2, 1, 3).reshape(n, t.shape[-1])

    u5p = to_scan_order(u5)
    a_cat, bb_cat, cc_cat = _s5_cat_tables(a_re, a_im, bb_re_b, bb_im_b, cc_re_b, cc_im_b)
    a_cat_conj = _s5_cat_tables(a_re, -a_im, bb_re_b, bb_im_b, cc_re_b, cc_im_b)[0]
    s_cat, y_lin = _s5_scan_fwd(u5p, a_cat, bb_cat, cc_cat, nb)

    def s5_out_body(yl_ref, u_ref, d_ref, wg_ref, ypre_ref, gate_ref, y_ref):
        ypre = yl_ref[...] + d_ref[...] * u_ref[...]
        ypre_ref[...] = ypre
        yg = _gelu(ypre)
        gate = _dot(yg.astype(BF16), wg_ref[...])
        gate_ref[...] = gate
        y_ref[...] = (yg * _sigmoid(gate)).astype(BF16)

    tm5 = min(1024, n)
    y_pre, gate, y_s5p = _rowcall(
        "s5_out", s5_out_body, n, tm5, [(y_lin, 'row'), (u5p, 'row'), (s5_d_row, 'full'), (w_glu_f, 'full')],
        [(dh, F32), (dh, F32), (dh, BF16)])
    y_s5 = from_scan_order(y_s5p)

    def merge_body(ygm_ref, ys5_ref, zga_ref, zgb_ref, wa_ref, wb_ref, pa_ref, pb_ref, mg_ref):
        pa = _dot(ygm_ref[...], wa_ref[...])
        pb = _dot(ys5_ref[...], wb_ref[...])
        pa_ref[...] = pa.astype(BF16)
        pb_ref[...] = pb.astype(BF16)
        mg_ref[...] = (_sigmoid(zga_ref[...].astype(F32)) * pa + _sigmoid(zgb_ref[...].astype(F32)) * pb).astype(BF16)

    p_a, p_b, merged, g_down = _rowcall(
        "merge", merge_body, n, tm,
        [(y_gm, 'row'), (y_s5, 'row'), (z_ga, 'row'), (z_gb, 'row'), (w_brgm_f, 'full'), (w_brs5_f, 'full')],
        [(d, BF16), (d, BF16), (d, BF16)], gather=[shard_b['ffn_w_down']])
    w_down_f = assemble('ffn_w_down', g_down)

    def close_sublayer(name, a_in, w_out, x_res, g_post, g_next, w_next=None, gather=()):
        def body(*refs):
            a_ref, w_ref, x_ref, gp_ref, gn_ref = refs[:5]
            rest = refs[5:]
            if w_next is not None:
                wn_ref, rest = rest[0], rest[1:]
            o_ref, xo_ref, h_ref = rest[:3]
            o = _dot(a_ref[...], w_ref[...])
            o_ref[...] = o.astype(BF16)
            xo = x_ref[...] + _rms(o, gp_ref[...])
            xo_ref[...] = xo
            hb = _rms(xo, gn_ref[...]).astype(BF16)
            h_ref[...] = hb
            if w_next is not None:
                rest[3][...] = _dot(hb, wn_ref[...]).astype(BF16)

        ins = [(a_in, 'row'), (w_out, 'full'), (x_res, 'row'), (row(g_post), 'full'), (row(g_next), 'full')]
        outs = [(d, BF16), (d, F32), (d, BF16)]
        if w_next is not None:
            ins.append((w_next, 'full'))
            outs.append((w_next.shape[1], BF16))
        return _rowcall(name, body, n, tm, ins, outs, gather=gather)

    o1, x1, hc, q, g_gu0 = close_sublayer("mix_out", merged, w_mix_f, x2d, W['g_mix_post'], W['g_ca_pre'], w_q_f,
                                          gather=[gu_halves[0]])

    tmm = min(ROW_TILE, nb * nmem)

    def memkv_body(m_ref, g_ref, w_ref, mn_ref, k_ref, v_ref):
        mnb = _rms(m_ref[...], g_ref[...]).astype(BF16)
        mn_ref[...] = mnb
        k_ref[...] = _dot(mnb, w_ref[:, :d]).astype(BF16)
        v_ref[...] = _dot(mnb, w_ref[:, d:]).astype(BF16)

    mem_n, k_mem, v_mem = _rowcall("mem_kv", memkv_body, nb * nmem, tmm,
                                   [(mem2d, 'row'), (row(W['g_mem']), 'full'), (w_kv_f, 'full')],
                                   [(d, BF16), (d, BF16), (d, BF16)])

    tiles_per_ex = seq // tm
    kv_spec = pl.BlockSpec((nmem, d), lambda i: (i // tiles_per_ex, 0))
    scale = hd ** -0.5

    def softmax_rows(qh, kh):
        s = _dot_nt(qh, kh) * scale
        e = jnp.exp(s - jnp.max(s, axis=-1, keepdims=True))
        return e / jnp.sum(e, axis=-1, keepdims=True)

    def attn_body(q_ref, k_ref, v_ref, o_ref):
        for h in range(CA_HEADS):
            sl = slice(h * hd, (h + 1) * hd)
            p = softmax_rows(q_ref[:, sl], k_ref[:, sl])
            o_ref[:, sl] = _dot(p.astype(BF16), v_ref[:, sl]).astype(BF16)

    (att,) = _rowcall("attn_fwd", attn_body, n, tm, [(q, 'row'), (k_mem, kv_spec), (v_mem, kv_spec)], [(d, BF16)])

    o2, x2, hf, g_gu1 = close_sublayer("attn_out", att, w_o_f, x1, W['g_ca_post'], W['g_ffn_pre'],
                                       gather=[gu_halves[1]])
    w_gu_t = jnp.stack([g_gu0, g_gu1], axis=1).reshape(2 * ffh, d)

    ck = 256

    def ffn_up_body(h_ref, w_ref, gu_ref, a_ref):
        hb = h_ref[...]
        for lo in range(0, ffh, ck):
            gt = _dot_nt(hb, w_ref[lo:lo + ck, :])
            ut = _dot_nt(hb, w_ref[ffh + lo:ffh + lo + ck, :])
            gu_ref[:, lo:lo + ck] = gt.astype(BF16)
            gu_ref[:, ffh + lo:ffh + lo + ck] = ut.astype(BF16)
            a_ref[:, lo:lo + ck] = ((gt * _sigmoid(gt)) * ut).astype(BF16)

    gu, act = _rowcall("ffn_up", ffn_up_body, n, tm, [(hf, 'row'), (w_gu_t, 'full')], [(2 * ffh, BF16), (ffh, BF16)])

    def ffn_down_body(a_ref, w_ref, x_ref, t_ref, g_ref, dx_ref, do_ref, loss_ref, dg_ref):
        i = pl.program_id(0)

        @pl.when(i == 0)
        def _():
            loss_ref[...] = jnp.zeros_like(loss_ref)
            dg_ref[...] = jnp.zeros_like(dg_ref)

        o = _dot(a_ref[...], w_ref[...])
        diff = x_ref[...] + _rms(o, g_ref[...]) - t_ref[...]
        loss_ref[...] += jnp.full(loss_ref.shape, 0.5 / d, F32) * jnp.sum(diff * diff)
        dx = diff * (1.0 / d)
        dx_ref[...] = dx
        do, dg = _rms_bwd(o, g_ref[...], dx)
        do_ref[...] = do.astype(BF16)
        dg_ref[...] += dg

    dx3, do3, loss_part, dg_ffn_post = _rowcall(
        "ffn_down_loss", ffn_down_body, n, tm,
        [(act, 'row'), (w_down_f, 'full'), (x2, 'row'), (tgt, 'row'), (row(W['g_ffn_post']), 'full')],
        [(d, F32), (d, BF16)], accs=[((1, 128), F32), ((1, d), F32)])

    G = {'g_ffn_post': dg_ffn_post}
    RECV = {}

    def parts_of(k, gfull):
        r, c = W[k].shape
        return gfull.reshape(r, N_DEV, c).transpose(1, 0, 2) if SHARDED[k] == 'col' else gfull.reshape(N_DEV, r, c)

    p_down = parts_of('ffn_w_down', _dwcall("dw_ffn_down", act, do3, BF16))

    def ffn_act_bwd_body(do_ref, w_ref, gu_ref, dgu_ref):
        dob = do_ref[...]
        for lo in range(0, ffh, ck):
            da = _dot_nt(dob, w_ref[lo:lo + ck, :])
            gt = gu_ref[:, lo:lo + ck].astype(F32)
            ut = gu_ref[:, ffh + lo:ffh + lo + ck].astype(F32)
            sg = _sigmoid(gt)
            dgu_ref[:, lo:lo + ck] = (da * ut * (sg * (1.0 + gt * (1.0 - sg)))).astype(BF16)
            dgu_ref[:, ffh + lo:ffh + lo + ck] = (da * (gt * sg)).astype(BF16)

    dgu, RECV['ffn_w_down'] = _rowcall("ffn_act_bwd", ffn_act_bwd_body, n, tm,
                                       [(do3, 'row'), (w_down_f, 'full'), (gu, 'row')], [(2 * ffh, BF16)],
                                       scatter=[p_down])
    p_gu = parts_of('ffn_w_gu', _dwcall("dw_ffn_gu", dgu, hf, BF16))

    def open_sublayer(name, pieces, w_full, x_in, g_pre, dx_up, o_prev=None, g_post_prev=None, scatter=(), gather=(),
                      w_t=False):
        n_p = len(pieces)
        second = o_prev is not None

        def body(*refs):
            dc_refs, (w_ref, x_ref, g_ref, dxu_ref), rest = refs[:n_p], refs[n_p:n_p + 4], refs[n_p + 4:]
            if second:
                (op_ref, gp_ref), rest = rest[:2], rest[2:]
            i = pl.program_id(0)
            dhid = None
            for dc_ref, (_, lo, hi) in zip(dc_refs, pieces):
                part = _dot(dc_ref[...], w_ref[lo:hi, :]) if w_t else _dot_nt(dc_ref[...], w_ref[:, lo:hi])
                dhid = part if dhid is None else dhid + part
            dxn, dg = _rms_bwd(x_ref[...], g_ref[...], dhid)
            dx = dxu_ref[...] + dxn
            if second:
                dx_ref, do_ref, dg_ref, dg2_ref = rest
            else:
                dx_ref, dg_ref = rest

            @pl.when(i == 0)
            def _():
                dg_ref[...] = jnp.zeros_like(dg_ref)
                if second:
                    dg2_ref[...] = jnp.zeros_like(dg2_ref)

            dx_ref[...] = dx
            dg_ref[...] += dg
            if second:
                do, dg2 = _rms_bwd(op_ref[...].astype(F32), gp_ref[...], dx)
                do_ref[...] = do.astype(BF16)
                dg2_ref[...] += dg2

        ins = [(p[0], 'row') for p in pieces] + [(w_full, 'full'), (x_in, 'row'), (row(g_pre), 'full'), (dx_up, 'row')]
        outs = [(d, F32)]
        accs = [((1, d), F32)]
        if second:
            ins += [(o_prev, 'row'), (row(g_post_prev), 'full')]
            outs.append((d, BF16))
            accs.append(((1, d), F32))
        res = _rowcall(name, body, n, tm, ins, outs, accs=accs, scatter=scatter, gather=gather)
        if second:
            dx, do, dg, dg2 = res[:4]
            return dx, dg, do, dg2, res[4:]
        return res[0], res[1], res[2:]

    dx2, G['g_ffn_pre'], do2, G['g_ca_post'], (RECV['ffn_w_gu'],) = open_sublayer(
        "ffn_in_bwd", [(dgu, 0, 2 * ffh)], w_gu_t, x2, W['g_ffn_pre'], dx3, o2, W['g_ca_post'], scatter=[p_gu],
        w_t=True)
    p_o = parts_of('ca_w_o', _dwcall("dw_ca_o", att, do2, BF16))

    def attn_bwd_body(q_ref, k_ref, v_ref, do_ref, wo_ref, dq_ref, dk_ref, dv_ref):
        i = pl.program_id(0)

        @pl.when(i % tiles_per_ex == 0)
        def _():
            dk_ref[...] = jnp.zeros_like(dk_ref)
            dv_ref[...] = jnp.zeros_like(dv_ref)

        d_att = _dot_nt(do_ref[...], wo_ref[...]).astype(BF16)
        for h in range(CA_HEADS):
            sl = slice(h * hd, (h + 1) * hd)
            qh, kh, vh, dah = q_ref[:, sl], k_ref[:, sl], v_ref[:, sl], d_att[:, sl]
            p = softmax_rows(qh, kh)
            dp = _dot_nt(dah, vh)
            ds = (p * (dp - jnp.sum(p * dp, axis=-1, keepdims=True)) * scale).astype(BF16)
            dq_ref[:, sl] = _dot(ds, kh).astype(BF16)
            dk_ref[:, sl] += _dot_tn(ds, qh)
            dv_ref[:, sl] += _dot_tn(p.astype(BF16), dah)

    kv_acc = ((nb * nmem, d), (F32, pl.BlockSpec((nmem, d), lambda i: (i // tiles_per_ex, 0))))
    dq, dk_mem, dv_mem, RECV['ca_w_o'] = _rowcall(
        "attn_bwd", attn_bwd_body, n, tm,
        [(q, 'row'), (k_mem, kv_spec), (v_mem, kv_spec), (do2, 'row'), (w_o_f, 'full')], [(d, BF16)],
        accs=[kv_acc, kv_acc], scatter=[p_o])
    p_q = parts_of('ca_w_q', _dwcall("dw_ca_q", hc, dq, BF16))

    def memkv_bwd_body(dk_ref, dv_ref, m_ref, g_ref, w_ref, dkv_ref, dg_ref):
        i = pl.program_id(0)

        @pl.when(i == 0)
        def _():
            dg_ref[...] = jnp.zeros_like(dg_ref)

        dkb, dvb = dk_ref[...].astype(BF16), dv_ref[...].astype(BF16)
        dkv_ref[:, :d] = dkb
        dkv_ref[:, d:] = dvb
        dmn = _dot_nt(dkb, w_ref[:, :d]) + _dot_nt(dvb, w_ref[:, d:])
        _, dg = _rms_bwd(m_ref[...], g_ref[...], dmn)
        dg_ref[...] += dg

    dkv, G['g_mem'] = _rowcall(
        "mem_kv_bwd", memkv_bwd_body, nb * nmem, tmm,
        [(dk_mem, 'row'), (dv_mem, 'row'), (mem2d, 'row'), (row(W['g_mem']), 'full'), (w_kv_f, 'full')],
        [(d2, BF16)], accs=[((1, d), F32)])
    p_kv = parts_of('ca_w_kv', _dwcall("dw_ca_kv", mem_n, dkv, BF16))

    dx1, G['g_ca_pre'], do1, G['g_mix_post'], (RECV['ca_w_q'], RECV['ca_w_kv']) = open_sublayer(
        "attn_in_bwd", [(dq, 0, d)], w_q_f, x1, W['g_ca_pre'], dx2, o1, W['g_mix_post'], scatter=[p_q, p_kv])
    p_mix = parts_of('w_mix_out', _dwcall("dw_mix_out", merged, do1, BF16))

    def merge_bwd_body(do_ref, wm_ref, zga_ref, zgb_ref, pa_ref, pb_ref, wb_ref, dpa_ref, dpb_ref, dza_ref, dzb_ref,
                       dys_ref):
        dm = _dot_nt(do_ref[...], wm_ref[...])
        sa, sb = _sigmoid(zga_ref[...].astype(F32)), _sigmoid(zgb_ref[...].astype(F32))
        dpb = (dm * sb).astype(BF16)
        dpa_ref[...] = (dm * sa).astype(BF16)
        dpb_ref[...] = dpb
        dza_ref[...] = (dm * pa_ref[...].astype(F32) * (sa * (1.0 - sa))).astype(BF16)
        dzb_ref[...] = (dm * pb_ref[...].astype(F32) * (sb * (1.0 - sb))).astype(BF16)
        dys_ref[...] = _dot_nt(dpb, wb_ref[...]).astype(BF16)

    NORM_ROWS = ['g_ffn_post', 'g_ffn_pre', 'g_ca_post', 'g_mem', 'g_ca_pre', 'g_mix_post']
    norm_rows = jnp.concatenate([G[k] for k in NORM_ROWS] + [jnp.tile(loss_part, (1, d // 128))], axis=0)
    dp_a, dp_b, dz_ga, dz_gb, dy_s5, all_norm_rows, RECV['w_mix_out'] = _rowcall(
        "merge_bwd", merge_bwd_body, n, tm,
        [(do1, 'row'), (w_mix_f, 'full'), (z_ga, 'row'), (z_gb, 'row'), (p_a, 'row'), (p_b, 'row'), (w_brs5_f, 'full')],
        [(d, BF16), (d, BF16), (d, BF16), (d, BF16), (dh, BF16)], scatter=[p_mix], gather=[norm_rows])
    p_brgm = parts_of('w_br_gm', _dwcall("dw_br_gm", y_gm, dp_a, BF16))
    p_brs5 = parts_of('w_br_s5', _dwcall("dw_br_s5", y_s5, dp_b, BF16))

    def gm_bwd_body(z_ref, dpa_ref, wa_ref, lng_ref, lnb_ref, ws_ref, wst_ref, bias_ref, dz_ref, dws_ref, dbias_ref,
                    dlng_ref, dlnb_ref, du_s, dvn_s, dgelu_s):
        i = pl.program_id(0)

        @pl.when(i == 0)
        def _():
            dws_ref[...] = jnp.zeros_like(dws_ref)
            dbias_ref[...] = jnp.zeros_like(dbias_ref)
            dlng_ref[...] = jnp.zeros_like(dlng_ref)
            dlnb_ref[...] = jnp.zeros_like(dlnb_ref)

        u, vhat, rstd, vnb = gm_recompute(z_ref, lng_ref, lnb_ref, dgelu_s)
        dy = _dot_nt(dpa_ref[...], wa_ref[...])
        for g in range(GM_GROUPS):
            sl = slice(g * gw, (g + 1) * gw)
            sv = _dot(ws_ref[g], vnb[:, sl]) + bias_ref[:, sl]
            du_s[:, sl] = dy[:, sl] * sv
            dsv = dy[:, sl] * u[:, sl]
            dsvb = dsv.astype(BF16)
            dvn_s[:, sl] = _dot(wst_ref[g], dsvb)
            dws_ref[g] += _dot_nt(dsvb, vnb[:, sl])
            dbias_ref[:, sl] += dsv
        dvn = dvn_s[...]
        dlng_ref[...] += jnp.sum(dvn * vhat, axis=0, keepdims=True)
        dlnb_ref[...] += jnp.sum(dvn, axis=0, keepdims=True)
        dvh = dvn * lng_ref[...]
        dv = rstd * (dvh - jnp.mean(dvh, axis=-1, keepdims=True) - vhat * jnp.mean(dvh * vhat, axis=-1, keepdims=True))
        dz_ref[:, :d] = (du_s[...] * dgelu_s[:, :d]).astype(BF16)
        dz_ref[:, d:] = (dv * dgelu_s[:, d:]).astype(BF16)

    dz_gm, dws_full, dbias_full, G['gm_ln_g'], G['gm_ln_b'], RECV['w_br_gm'], RECV['w_br_s5'] = _rowcall(
        "gmlp_bwd", gm_bwd_body, n, GM_CHUNK,
        [(z_gm, 'row'), (dp_a, 'row'), (w_brgm_f, 'full'), (row(W['gm_ln_g']), 'full'), (row(W['gm_ln_b']), 'full'),
         (w_s, 'full'), (w_s_t, 'full'), (gm_bias, 'full')],
        [(d2, BF16)], accs=[((GM_GROUPS, GM_CHUNK, GM_CHUNK), F32), ((GM_CHUNK, d), F32), ((1, d), F32), ((1, d), F32)],
        scratch=[pltpu.VMEM((GM_CHUNK, d), F32), pltpu.VMEM((GM_CHUNK, d), F32), pltpu.VMEM((GM_CHUNK, d2), F32)],
        scatter=[p_brgm, p_brs5])
    G['gm_w_s'] = jnp.where(tril[None], dws_full, 0.0)
    G['gm_b_s'] = dbias_full.reshape(GM_CHUNK, GM_GROUPS, gw).sum(-1).T

    dy_s5p = to_scan_order(dy_s5)

    def s5_out_bwd_body(dy_ref, ypre_ref, gate_ref, u_ref, wg_ref, dyp_ref, dgate_ref, yg_ref, dd_ref):
        i = pl.program_id(0)

        @pl.when(i == 0)
        def _():
            dd_ref[...] = jnp.zeros_like(dd_ref)

        dy = dy_ref[...].astype(F32)
        yg, dgelu = _gelu_and_grad(ypre_ref[...])
        sg = _sigmoid(gate_ref[...])
        dgb = (dy * yg * (sg * (1.0 - sg))).astype(BF16)
        dgate_ref[...] = dgb
        yg_ref[...] = yg.astype(BF16)
        dyp = (dy * sg + _dot_nt(dgb, wg_ref[...])) * dgelu
        dyp_ref[...] = dyp
        dd_ref[...] += jnp.sum(dyp * u_ref[...], axis=0, keepdims=True)

    dy_pre, dgate, yg_b, dd = _rowcall(
        "s5_out_bwd", s5_out_bwd_body, n, tm5,
        [(dy_s5p, 'row'), (y_pre, 'row'), (gate, 'row'), (u5p, 'row'), (w_glu_f, 'full')],
        [(dh, F32), (dh, BF16), (dh, BF16)], accs=[((1, dh), F32)])
    du5p, d_bb_cat, d_cc_cat, da_cat = _s5_scan_bwd(dy_pre, u5p, s_cat, a_cat_conj, bb_cat, cc_cat, s5_d_row)
    du5 = from_scan_order(du5p.astype(BF16))
    d_bb_re, d_bb_im = _s5_uncat(jnp.sum(d_bb_cat, axis=0), 2)
    d_cc_re, d_cc_neg_im = _s5_uncat(jnp.sum(d_cc_cat, axis=0), 1)
    da_re, da_im = _s5_uncat(jnp.sum(da_cat, axis=0), 1)
    lane_shape = (S5_GROUPS, 8, 128)
    g_c_re, g_c_im = (t.reshape(lane_shape) for t in s5_out_vjp((d_cc_re, -d_cc_neg_im)))
    g_d = dd.reshape(S5_GROUPS, S5_CH)
    g_lam_re, g_lam_im, g_log_step, g_b_re, g_b_im = s5_in_vjp((da_re, da_im, d_bb_re, d_bb_im))
    g_log_step, g_b_re, g_b_im = g_log_step.reshape(1, -1), g_b_re.reshape(lane_shape), g_b_im.reshape(lane_shape)
    p_glu = parts_of('s5_w_glu', _dwcall("dw_s5_glu", yg_b, dgate, BF16))

    pieces = [(dz_gm, 0, d2), (du5, d2, d2 + dh), (dz_ga, d2 + dh, d2 + dh + d), (dz_gb, d2 + dh + d, d2 + dh + 2 * d)]
    gm_ln_rows = jnp.concatenate([G['gm_ln_g'], G['gm_ln_b']], axis=0)
    dw_gm, all_lam_re, all_lam_im, all_log_step, all_b_re, all_b_im = _dwcall(
        "dw_in_0", dz_gm, h0, BF16, gather=[g_lam_re, g_lam_im, g_log_step, g_b_re, g_b_im])
    dw_s5 = _dwcall("dw_in_1", du5, h0, BF16)
    dw_ga, all_gm_w_s, all_gm_b_s, all_gm_ln = _dwcall(
        "dw_in_2", dz_ga, h0, BF16, gather=[G['gm_w_s'], G['gm_b_s'], gm_ln_rows])
    dw_gb, all_c_re, all_c_im, all_d, RECV['s5_w_glu'] = _dwcall(
        "dw_in_3", dz_gb, h0, BF16, gather=[g_c_re, g_c_im, g_d], scatter=[p_glu])
    p_in = parts_of('w_in', jnp.concatenate([dw_gm, dw_s5, dw_ga, dw_gb], axis=0))
    grad_x2d, g_mix_pre_part, (RECV['w_in'],) = open_sublayer(
        "in_proj_bwd", pieces, w_in_t, x2d, W['g_mix_pre'], dx1, scatter=[p_in], w_t=True)

    out = {}
    for k in names:
        quad = _sum_adamw("sum_adamw_" + k, RECV[k], W[k], M[k], V[k])
        out[k] = [(t.T if SHARDED[k] == 'colT' else t)[None] for t in quad]
    gathered = [all_norm_rows, all_gm_w_s, all_gm_b_s, all_gm_ln, all_c_re, all_c_im, all_d, all_lam_re, all_lam_im,
                all_log_step, all_b_re, all_b_im, _gather_last(g_mix_pre_part)]
    where = {'g_ffn_post': (0, 0), 'g_ffn_pre': (0, 1), 'g_ca_post': (0, 2), 'g_mem': (0, 3), 'g_ca_pre': (0, 4),
             'g_mix_post': (0, 5), 'gm_w_s': (1, None), 'gm_b_s': (2, None), 'gm_ln_g': (3, 0), 'gm_ln_b': (3, 1),
             's5_c_re': (4, None), 's5_c_im': (5, None), 's5_d': (6, None), 's5_lam_re': (7, None),
             's5_lam_im': (8, None), 's5_log_step': (9, 0), 's5_b_re': (10, None), 's5_b_im': (11, None),
             'g_mix_pre': (12, 0)}
    folded = ('s5_b_re', 's5_b_im', 's5_c_re', 's5_c_im')

    def as_updated(k, t):
        return t.reshape((1,) + lane_shape) if k in folded else t

    small, loss_row = _small_adamw(
        [(as_updated(k, args[k]), as_updated(k, args['m_' + k]), as_updated(k, args['v_' + k])) + where[k]
         for k in SMALL], gathered, (0, len(NORM_ROWS)))
    out.update({k: [t.reshape(args[k].shape) for t in quad] for k, quad in zip(SMALL, small)})

    res = [loss_row[0, 0], grad_x2d.reshape(x.shape)]
    for j in range(4):
        res += [out[k][j] for k in WEIGHTS]
    return tuple(res)
```

```python
import functools
import math

import jax
import jax.numpy as jnp
from jax import lax
from jax.experimental import pallas as pl
from jax.experimental.pallas import tpu as pltpu

F32 = jnp.float32
BF16 = jnp.bfloat16
EPS = 1e-6
N_DEV = 8
V7X_VMEM_LIMIT = 56 * 1024 * 1024
ROW_TILE = 256
GM_CHUNK = 128
GM_GROUPS = 8
S5_GROUPS = 32
S5_STATE = 64
S5_CH = 16
SCAN_CHUNKS = 32
SCAN_LANES = 128
S5_BLOCK_LANES = 256
CA_HEADS = 4
ADAM_LR, ADAM_B1, ADAM_B2, ADAM_EPS, ADAM_WD, ADAM_STEP = 0.001, 0.9, 0.999, 1e-08, 0.01, 10

WEIGHTS = ['g_mix_pre', 'w_in', 'gm_ln_g', 'gm_ln_b', 'gm_w_s', 'gm_b_s', 's5_lam_re', 's5_lam_im', 's5_log_step',
           's5_b_re', 's5_b_im', 's5_c_re', 's5_c_im', 's5_d', 's5_w_glu', 'w_br_gm', 'w_br_s5', 'w_mix_out',
           'g_mix_post', 'g_ca_pre', 'g_mem', 'ca_w_q', 'ca_w_kv', 'ca_w_o', 'g_ca_post', 'g_ffn_pre', 'ffn_w_gu',
           'ffn_w_down', 'g_ffn_post']
SHARDED = {'w_in': 'colT', 's5_w_glu': 'row', 'w_br_gm': 'row', 'w_br_s5': 'col', 'w_mix_out': 'row',
           'ca_w_q': 'row', 'ca_w_kv': 'col', 'ca_w_o': 'row', 'ffn_w_gu': 'colT', 'ffn_w_down': 'row'}
SMALL = [n for n in WEIGHTS if n not in SHARDED]


def _rms(x, g):
    r = lax.rsqrt(jnp.mean(x * x, axis=-1, keepdims=True) + EPS)
    return (x * r) * g


def _rms_bwd(x, g, dy):
    r = lax.rsqrt(jnp.mean(x * x, axis=-1, keepdims=True) + EPS)
    n = x * r
    dn = dy * g
    dx = r * (dn - n * jnp.mean(dn * n, axis=-1, keepdims=True))
    return dx, jnp.sum(dy * n, axis=0, keepdims=True)


_GELU_C = math.sqrt(2.0 / math.pi)


def _gelu(x):
    return 0.5 * x * (1.0 + jnp.tanh(_GELU_C * (x + 0.044715 * (x * x * x))))


def _gelu_and_grad(x):
    x2 = x * x
    t = jnp.tanh(_GELU_C * (x + 0.044715 * (x2 * x)))
    h = 0.5 * (1.0 + t)
    return x * h, h + 0.5 * x * (1.0 - t * t) * (_GELU_C * (1.0 + 3.0 * 0.044715 * x2))


def _sigmoid(x):
    return 0.5 * (1.0 + jnp.tanh(0.5 * x))


def _dot(a, b):
    return jnp.dot(a, b, preferred_element_type=F32)


def _dot_nt(a, b):
    return lax.dot_general(a, b, (((1,), (1,)), ((), ())), preferred_element_type=F32)


def _dot_tn(a, b):
    return lax.dot_general(a, b, (((0,), (0,)), ((), ())), preferred_element_type=F32)


def _adamw(w, g, m, v):
    m = ADAM_B1 * m + (1.0 - ADAM_B1) * g
    v = ADAM_B2 * v + (1.0 - ADAM_B2) * (g * g)
    m_hat = m / (1.0 - ADAM_B1 ** ADAM_STEP)
    v_hat = v / (1.0 - ADAM_B2 ** ADAM_STEP)
    delta = -ADAM_LR * (m_hat / (jnp.sqrt(v_hat) + ADAM_EPS) + ADAM_WD * w)
    return delta, m, v


def _params(n_grid):
    return pltpu.CompilerParams(dimension_semantics=("arbitrary",) * n_grid, vmem_limit_bytes=V7X_VMEM_LIMIT)


def _my_place():
    x, y, c = lax.axis_index("x"), lax.axis_index("y"), lax.axis_index("c")
    return x, y, c


def _peer(x, y, c, k):
    px = 1 - x if k & 4 else x
    py = 1 - y if k & 2 else y
    pc = 1 - c if k & 1 else c
    return (px, py, pc), 4 * px + 2 * py + pc


def _exchange(kind, src_refs, dst_refs, sems, first, phase):
    x, y, c = _my_place()
    me = 4 * x + 2 * y + c
    send_sems, recv_sems, own_sems = sems
    for j, (src, dst) in enumerate(zip(src_refs, dst_refs), start=first):
        own = pltpu.make_async_copy(src if kind == 'gather' else src.at[me], dst.at[me], own_sems.at[j])
        if phase == 'start':
            own.start()
        for k in range(1, N_DEV):
            peer, peer_block = _peer(x, y, c, k)
            out = pltpu.make_async_remote_copy(
                src_ref=src if kind == 'gather' else src.at[peer_block], dst_ref=dst.at[me],
                send_sem=send_sems.at[7 * j + k - 1], recv_sem=recv_sems.at[7 * j + k - 1], device_id=peer,
                device_id_type=pl.DeviceIdType.MESH)
            if phase == 'start':
                out.start()
            else:
                pltpu.make_async_remote_copy(
                    src_ref=src if kind == 'gather' else src.at[peer_block], dst_ref=dst.at[peer_block],
                    send_sem=send_sems.at[7 * j + k - 1], recv_sem=recv_sems.at[7 * j + k - 1], device_id=peer,
                    device_id_type=pl.DeviceIdType.MESH).wait_recv()
                out.wait_send()
        if phase == 'wait':
            own.wait()


def _gather_two_level(src_refs, dst_refs, sems, first, phase):
    x, y, c = _my_place()
    me, sibling = (x, y, c), (x, y, 1 - c)
    chips = [(1 - x, y), (x, 1 - y), (1 - x, 1 - y)]
    send_sems, recv_sems, own_sems = sems
    for j, (src, dst) in enumerate(zip(src_refs, dst_refs), start=first):
        def rows(px, py, pc, dst=dst):
            return dst.at[4 * px + 2 * py + pc]

        def copy(k, block, to, from_src=False, j=j, src=src, rows=rows):
            return pltpu.make_async_remote_copy(
                src_ref=src if from_src else rows(*block), dst_ref=rows(*block), send_sem=send_sems.at[7 * j + k],
                recv_sem=recv_sems.at[7 * j + k], device_id=to, device_id_type=pl.DeviceIdType.MESH)

        mine = pltpu.make_async_copy(src, rows(*me), own_sems.at[j])
        first_out = [copy(0, me, sibling, True)] + [copy(1 + i, me, (*chip, c), True) for i, chip in enumerate(chips)]
        if phase == 'start':
            mine.start()
            for cp in first_out:
                cp.start()
        else:
            passed = [copy(4 + i, (*chip, c), sibling) for i, chip in enumerate(chips)]
            for i, chip in enumerate(chips):
                copy(1 + i, (*chip, c), me).wait_recv()
                passed[i].start()
            copy(0, sibling, me).wait_recv()
            for i, chip in enumerate(chips):
                copy(4 + i, (*chip, 1 - c), me).wait_recv()
            for cp in first_out + passed:
                cp.wait_send()
            mine.wait()


class _Ride:
    def __init__(self, gather=(), scatter=(), two_level=False):
        self.two_level = two_level
        self.n_gather = len(gather)
        self.arrays = list(gather) + list(scatter)
        n = len(self.arrays)
        hbm = pl.BlockSpec(memory_space=pl.ANY)
        self.in_specs = [hbm] * n
        self.out_specs = [hbm] * n
        self.out_shape = [jax.ShapeDtypeStruct((N_DEV,) + a.shape, a.dtype) for a in gather]
        self.out_shape += [jax.ShapeDtypeStruct(a.shape, a.dtype) for a in scatter]
        self.scratch = [pltpu.SemaphoreType.DMA((7 * n,)), pltpu.SemaphoreType.DMA((7 * n,)),
                        pltpu.SemaphoreType.DMA((n,))] if n else []

    def wrap(self, inner, n_in, n_out, n_scr, is_first, is_last):
        n_mv = len(self.arrays)
        if not n_mv:
            return inner

        def body(*refs):
            mv_src = refs[n_in:n_in + n_mv]
            mv_dst = refs[n_in + n_mv + n_out:n_in + 2 * n_mv + n_out]
            sems = refs[n_in + 2 * n_mv + n_out + n_scr:]

            def exchange(phase):
                if self.n_gather and self.two_level:
                    _gather_two_level(mv_src[:self.n_gather], mv_dst[:self.n_gather], sems, 0, phase)
                elif self.n_gather:
                    _exchange('gather', mv_src[:self.n_gather], mv_dst[:self.n_gather], sems, 0, phase)
                if n_mv > self.n_gather:
                    _exchange('scatter', mv_src[self.n_gather:], mv_dst[self.n_gather:], sems, self.n_gather, phase)

            pl.when(is_first())(functools.partial(exchange, 'start'))
            inner(*refs[:n_in], *refs[n_in + n_mv:n_in + n_mv + n_out],
                  *refs[n_in + 2 * n_mv + n_out:n_in + 2 * n_mv + n_out + n_scr])
            pl.when(is_last())(functools.partial(exchange, 'wait'))

        return body


def _rowcall(name, body, n_rows, tm, ins, outs, accs=(), scratch=(), gather=(), scatter=(), two_level=False):
    n_steps = n_rows // tm
    ride = _Ride(gather, scatter, two_level)
    body = ride.wrap(body, len(ins), len(outs) + len(accs), len(scratch), lambda: pl.program_id(0) == 0,
                     lambda: pl.program_id(0) == n_steps - 1)
    arrays, in_specs = [], []
    for a, kind in ins:
        arrays.append(a)
        if kind == 'row':
            in_specs.append(pl.BlockSpec((tm,) + a.shape[1:], lambda i, nd=a.ndim: (i,) + (0,) * (nd - 1)))
        elif kind == 'full':
            in_specs.append(pl.BlockSpec(a.shape, lambda i, nd=a.ndim: (0,) * nd))
        else:
            in_specs.append(kind)
    out_shape, out_specs = [], []
    for cols, dt in outs:
        out_shape.append(jax.ShapeDtypeStruct((n_rows, cols), dt))
        out_specs.append(pl.BlockSpec((tm, cols), lambda i: (i, 0)))
    for shp, dt in accs:
        if isinstance(dt, tuple):
            dt, spec = dt
        else:
            spec = pl.BlockSpec(shp, lambda i, nd=len(shp): (0,) * nd)
        out_shape.append(jax.ShapeDtypeStruct(shp, dt))
        out_specs.append(spec)
    return pl.pallas_call(functools.partial(body), grid=(n_steps,), in_specs=in_specs + ride.in_specs,
                          out_specs=out_specs + ride.out_specs, out_shape=out_shape + ride.out_shape,
                          scratch_shapes=list(scratch) + ride.scratch, name=name,
                          compiler_params=_params(1))(*arrays, *ride.arrays)


DW_ACC_BYTES = 12 * 1024 * 1024
DW_LHS_BYTES = 6 * 1024 * 1024


def _dw_tiles(n, ka, nn, a_itemsize):
    tn = max(t for t in range(128, min(nn, 1536) + 1, 128) if nn % t == 0)
    tka = max(t for t in range(128, ka + 1, 128) if ka % t == 0 and t * tn * 4 <= DW_ACC_BYTES)
    tm = min(n, 2048)
    while tm > 256 and tm * tka * a_itemsize > DW_LHS_BYTES:
        tm //= 2
    return tm, tka, tn


def _dwcall(name, a, dc, out_dtype, gather=(), scatter=()):
    n, ka = a.shape
    nn = dc.shape[1]
    tm, tka, tn = _dw_tiles(n, ka, nn, a.dtype.itemsize)
    n_i, n_j, n_k = n // tm, nn // tn, ka // tka
    ride = _Ride(gather, scatter)

    def body(a_ref, dc_ref, o_ref, acc_ref):
        i = pl.program_id(2)

        @pl.when(i == 0)
        def _():
            acc_ref[...] = jnp.zeros_like(acc_ref)

        acc_ref[...] += _dot_tn(a_ref[...].astype(BF16), dc_ref[...].astype(BF16))

        @pl.when(i == n_i - 1)
        def _():
            o_ref[...] = acc_ref[...].astype(o_ref.dtype)

    def at_step(k, j, i):
        return lambda: (pl.program_id(0) == k) & (pl.program_id(1) == j) & (pl.program_id(2) == i)

    body = ride.wrap(body, 2, 1, 1, at_step(0, 0, 0), at_step(n_k - 1, n_j - 1, n_i - 1))
    res = pl.pallas_call(
        body, grid=(n_k, n_j, n_i),
        in_specs=[pl.BlockSpec((tm, tka), lambda k, j, i: (i, k)), pl.BlockSpec((tm, tn), lambda k, j, i: (i, j))]
        + ride.in_specs,
        out_specs=[pl.BlockSpec((tka, tn), lambda k, j, i: (k, j))] + ride.out_specs,
        out_shape=[jax.ShapeDtypeStruct((ka, nn), out_dtype)] + ride.out_shape,
        scratch_shapes=[pltpu.VMEM((tka, tn), F32)] + ride.scratch, name=name, compiler_params=_params(3))(
            a, dc, *ride.arrays)
    return res if ride.arrays else res[0]


def _first_gather(first, later):
    n = len(later)
    vm = pl.BlockSpec(memory_space=pltpu.VMEM)

    def body(*refs):
        first_ref, later_refs = refs[0], refs[1:1 + n]
        out_ref, cast_refs = refs[1 + n], refs[2 + n:2 + 2 * n]
        stage = refs[2 + 2 * n]
        sems = refs[3 + 2 * n:]
        stage[...] = first_ref[...].astype(BF16)
        _gather_two_level([stage], [out_ref], sems, 0, 'start')
        for src, dst in zip(later_refs, cast_refs):
            dst[...] = src[...].astype(BF16)
        _gather_two_level([stage], [out_ref], sems, 0, 'wait')

    return pl.pallas_call(
        body, out_shape=[jax.ShapeDtypeStruct((N_DEV,) + first.shape, BF16)]
        + [jax.ShapeDtypeStruct(s.shape, BF16) for s in later],
        in_specs=[vm] * (1 + n), out_specs=[pl.BlockSpec(memory_space=pl.ANY)] + [vm] * n,
        scratch_shapes=[pltpu.VMEM(first.shape, BF16), pltpu.SemaphoreType.DMA((7,)), pltpu.SemaphoreType.DMA((7,)),
                        pltpu.SemaphoreType.DMA((1,))],
        name="gather_first", compiler_params=pltpu.CompilerParams(vmem_limit_bytes=V7X_VMEM_LIMIT))(first, *later)


def _sum_adamw(name, recv, w, m, v):
    shp = w.shape

    def body(recv_ref, w_ref, m_ref, v_ref, g_ref, d_ref, nm_ref, nv_ref):
        g = recv_ref[0].astype(F32)
        for k in range(1, N_DEV):
            g = g + recv_ref[k].astype(F32)
        d, nm, nv = _adamw(w_ref[...], g, m_ref[...], v_ref[...])
        g_ref[...] = g
        d_ref[...] = d
        nm_ref[...] = nm
        nv_ref[...] = nv

    vm = pl.BlockSpec(memory_space=pltpu.VMEM)
    return pl.pallas_call(
        body, out_shape=[jax.ShapeDtypeStruct(shp, F32)] * 4, in_specs=[vm] * 4, out_specs=[vm] * 4,
        name=name, compiler_params=pltpu.CompilerParams(vmem_limit_bytes=V7X_VMEM_LIMIT))(recv, w, m, v)


def _gather_last(a):
    def body(a_ref, out_ref, send_sems, recv_sems, own_sems):
        sems = (send_sems, recv_sems, own_sems)
        _exchange('gather', [a_ref], [out_ref], sems, 0, 'start')
        _exchange('gather', [a_ref], [out_ref], sems, 0, 'wait')

    vm = pl.BlockSpec(memory_space=pltpu.VMEM)
    return pl.pallas_call(
        body, out_shape=jax.ShapeDtypeStruct((N_DEV,) + a.shape, a.dtype), in_specs=[vm], out_specs=vm,
        scratch_shapes=[pltpu.SemaphoreType.DMA((7,)), pltpu.SemaphoreType.DMA((7,)), pltpu.SemaphoreType.DMA((1,))],
        name="gather_last", compiler_params=pltpu.CompilerParams(vmem_limit_bytes=V7X_VMEM_LIMIT))(a)


def _small_adamw(entries, gathered, loss_at):
    n, ng = len(entries), len(gathered)

    def body(*refs):
        g_refs, wmv = refs[:ng], refs[ng:ng + 3 * n]
        outs, loss_out = refs[ng + 3 * n:ng + 7 * n], refs[ng + 7 * n]

        def total(ref, r):
            if r is None:
                t = ref[0]
                for k in range(1, N_DEV):
                    t = t + ref[k]
                return t[None]
            t = ref[0, r:r + 1, :]
            for k in range(1, N_DEV):
                t = t + ref[k, r:r + 1, :]
            return t

        for j, (_, _, _, gi, r) in enumerate(entries):
            g = total(g_refs[gi], r)
            d, nm, nv = _adamw(wmv[3 * j][...], g, wmv[3 * j + 1][...], wmv[3 * j + 2][...])
            for ref, val in zip(outs[4 * j:4 * j + 4], (g, d, nm, nv)):
                ref[...] = val
        loss_out[...] = total(g_refs[loss_at[0]], loss_at[1])[:, :128]

    vm = pl.BlockSpec(memory_space=pltpu.VMEM)
    out_shape, arrays = [], list(gathered)
    for w, m, v, _, _ in entries:
        out_shape += [jax.ShapeDtypeStruct(w.shape, F32)] * 4
        arrays += [w, m, v]
    out_shape.append(jax.ShapeDtypeStruct((1, 128), F32))
    res = pl.pallas_call(
        body, out_shape=out_shape, in_specs=[vm] * len(arrays), out_specs=[vm] * len(out_shape),
        name="small_adamw", compiler_params=pltpu.CompilerParams(vmem_limit_bytes=V7X_VMEM_LIMIT))(*arrays)
    return [res[4 * j:4 * j + 4] for j in range(n)], res[4 * n]


def _blockdiag8(t):
    g, per = t.shape[0], 8
    eye = jnp.eye(per, dtype=F32)
    t = t.reshape(g // per, per, t.shape[1], t.shape[2])
    return (t[:, :, :, None, :] * eye[None, :, None, :, None]).reshape(g // per, per * t.shape[2], per * t.shape[3])


def _s5_out_tables(c_re, c_im):
    return _blockdiag8(c_re.transpose(0, 2, 1)), _blockdiag8(c_im.transpose(0, 2, 1))


def _s5_in_tables(lam_re, lam_im, log_step, b_re, b_im):
    step = jnp.exp(log_step)[:, None]
    mag = jnp.exp(lam_re * step)
    ab_re = mag * jnp.cos(lam_im * step)
    ab_im = mag * jnp.sin(lam_im * step)
    den = lam_re * lam_re + lam_im * lam_im
    nr = ab_re - 1.0
    co_re = (nr * lam_re + ab_im * lam_im) / den
    co_im = (ab_im * lam_re - nr * lam_im) / den
    bb_re = co_re[..., None] * b_re - co_im[..., None] * b_im
    bb_im = co_re[..., None] * b_im + co_im[..., None] * b_re
    return (ab_re.reshape(1, -1), ab_im.reshape(1, -1), _blockdiag8(bb_re.transpose(0, 2, 1)),
            _blockdiag8(bb_im.transpose(0, 2, 1)))


def _scan_passes(x_s, a_ref, cr_ref, ci_ref, reverse, re, im):
    tc, nc = x_s.shape[:2]
    ln = SCAN_LANES
    n_sq = int(math.log2(tc))
    assert 2 ** n_sq == tc
    ar = jnp.broadcast_to(a_ref[:, re], (nc, ln))
    ai = jnp.broadcast_to(a_ref[:, im], (nc, ln))
    zero = jnp.zeros((nc, ln), F32)

    def at(t):
        return tc - 1 - t if reverse else t

    def local(t, carry):
        sr, si = carry
        j = at(t)
        nr = ar * sr - ai * si + x_s[j, :, re]
        ni = ar * si + ai * sr + x_s[j, :, im]
        x_s[j, :, re] = nr
        x_s[j, :, im] = ni
        return nr, ni

    lr, li = lax.fori_loop(0, tc, local, (zero, zero))
    pr, pi = a_ref[:, re], a_ref[:, im]
    for _ in range(n_sq):
        pr, pi = pr * pr - pi * pi, 2.0 * (pr * pi)
    cr_ref[...] = lr
    ci_ref[...] = li
    tr = jnp.zeros((1, ln), F32)
    ti = jnp.zeros((1, ln), F32)
    for c in (range(nc - 1, -1, -1) if reverse else range(nc)):
        l_r = cr_ref[c:c + 1, :]
        l_i = ci_ref[c:c + 1, :]
        cr_ref[c:c + 1, :] = tr
        ci_ref[c:c + 1, :] = ti
        tr, ti = pr * tr - pi * ti + l_r, pr * ti + pi * tr + l_i

    def second_pass(on_fixed):
        def fixup(t, carry):
            qr, qi, acc = carry
            j = at(t)
            qr, qi = ar * qr - ai * qi, ar * qi + ai * qr
            gr = x_s[j, :, re] + qr
            gi = x_s[j, :, im] + qi
            x_s[j, :, re] = gr
            x_s[j, :, im] = gi
            return qr, qi, on_fixed(j, gr, gi, acc)

        return fixup

    return second_pass, (cr_ref[...], ci_ref[...]), zero


def _s5_cat_tables(a_re, a_im, bb_re, bb_im, cc_re, cc_im):
    lb = S5_BLOCK_LANES
    n_blk, kq, nq = bb_re.shape
    sub = nq // lb

    def lanes(re, im):
        lead = re.shape[:-1]
        both = jnp.stack([re.reshape(lead + (-1, lb)), im.reshape(lead + (-1, lb))], axis=-2)
        return both.reshape(lead + (-1,))

    cc = jnp.stack([cc_re.reshape(n_blk, sub, lb, kq), -cc_im.reshape(n_blk, sub, lb, kq)], axis=2)
    return lanes(a_re, a_im), lanes(bb_re, bb_im), cc.reshape(n_blk, sub * 2 * lb, kq)


def _s5_uncat(t, axis):
    lb = S5_BLOCK_LANES
    shp = t.shape
    t = t.reshape(shp[:axis] + (-1, 2, lb) + shp[axis + 1:])
    re, im = jnp.take(t, 0, axis=axis + 1), jnp.take(t, 1, axis=axis + 1)
    return re.reshape(shp[:axis] + (-1,) + shp[axis + 1:]), im.reshape(shp[:axis] + (-1,) + shp[axis + 1:])


def _s5_scan_fwd(u5p, a_cat, bb_cat, cc_cat, nb):
    n, dh = u5p.shape
    n_blk, kq, cols = bb_cat.shape
    lb, ln, nc = S5_BLOCK_LANES, SCAN_LANES, SCAN_CHUNKS
    sub = cols // (2 * lb)
    rows = n // nb
    tc = rows // nc
    n_l = n_blk * sub

    def body(u_ref, a_ref, b_ref, c_ref, s_ref, y_ref, x_s, car_r, car_i):
        x_s[...] = _dot(u_ref[...].astype(BF16), b_ref[0]).reshape(tc, nc, 2 * lb)
        for h in range(lb // ln):
            re, im = slice(h * ln, (h + 1) * ln), slice(lb + h * ln, lb + (h + 1) * ln)

            def keep(j, gr, gi, acc, re=re, im=im):
                s_ref[0, j, :, re] = gr.astype(BF16)
                s_ref[0, j, :, im] = gi.astype(BF16)
                return acc

            second_pass, start, zero = _scan_passes(x_s, a_ref, car_r, car_i, False, re, im)
            lax.fori_loop(0, tc, second_pass(keep), (*start, zero))
        y = _dot(x_s[...].reshape(rows, 2 * lb).astype(BF16), c_ref[0])

        @pl.when(pl.program_id(1) % sub == 0)
        def _():
            y_ref[...] = y

        @pl.when(pl.program_id(1) % sub != 0)
        def _():
            y_ref[...] += y

    row_blk = pl.BlockSpec((rows, kq), lambda b, l: (b, l // sub))
    return pl.pallas_call(
        body, grid=(nb, n_l),
        in_specs=[row_blk, pl.BlockSpec((1, 2 * lb), lambda b, l: (0, l)),
                  pl.BlockSpec((1, kq, 2 * lb), lambda b, l: (l // sub, 0, l % sub)),
                  pl.BlockSpec((1, 2 * lb, kq), lambda b, l: (l // sub, l % sub, 0))],
        out_specs=[pl.BlockSpec((1, tc, nc, 2 * lb), lambda b, l: (b, 0, 0, l)), row_blk],
        out_shape=[jax.ShapeDtypeStruct((nb, tc, nc, n_l * 2 * lb), BF16), jax.ShapeDtypeStruct((n, dh), F32)],
        scratch_shapes=[pltpu.VMEM((tc, nc, 2 * lb), F32)] + [pltpu.VMEM((nc, ln), F32)] * 2,
        name="s5_scan_fwd", compiler_params=_params(2))(u5p, a_cat, bb_cat, cc_cat)


def _s5_scan_bwd(dy_pre, u5p, s_cat, a_cat_conj, bb_cat, cc_cat, d_row):
    n, dh = u5p.shape
    n_blk, kq, cols = bb_cat.shape
    nb, tc, nc, _ = s_cat.shape
    lb, ln = S5_BLOCK_LANES, SCAN_LANES
    sub = cols // (2 * lb)
    rows = n // nb
    n_l = n_blk * sub

    def body(dy_ref, u_ref, s_ref, a_ref, b_ref, c_ref, d_ref, du_ref, db_ref, dc_ref, da_ref, x_s, car_r, car_i):
        dyb = dy_ref[...].astype(BF16)
        x_s[...] = _dot_nt(dyb, c_ref[0]).reshape(tc, nc, 2 * lb)
        for h in range(lb // ln):
            re, im = slice(h * ln, (h + 1) * ln), slice(lb + h * ln, lb + (h + 1) * ln)

            def with_state_before(j, gr, gi, acc, re=re, im=im):
                dr, di = acc
                pr_, pi_ = s_ref[0, j - 1, :, re].astype(F32), s_ref[0, j - 1, :, im].astype(F32)
                return dr + (pr_ * gr + pi_ * gi), di + (pr_ * gi - pi_ * gr)

            second_pass, start, zero = _scan_passes(x_s, a_ref, car_r, car_i, True, re, im)
            carry = lax.fori_loop(0, tc - 1, second_pass(with_state_before), (*start, (zero, zero)))
            row = lax.broadcasted_iota(jnp.int32, (nc, ln), 0)
            pr_ = jnp.where(row == 0, 0.0, pltpu.roll(s_ref[0, tc - 1, :, re].astype(F32), 1, 0))
            pi_ = jnp.where(row == 0, 0.0, pltpu.roll(s_ref[0, tc - 1, :, im].astype(F32), 1, 0))

            def at_first_time(j, gr, gi, acc, pr_=pr_, pi_=pi_):
                return acc[0] + (pr_ * gr + pi_ * gi), acc[1] + (pr_ * gi - pi_ * gr)

            _, _, (dr, di) = second_pass(at_first_time)(tc - 1, carry)
            da_ref[0, :, re] = jnp.sum(dr, axis=0, keepdims=True)
            da_ref[0, :, im] = jnp.sum(di, axis=0, keepdims=True)

        gb = x_s[...].reshape(rows, 2 * lb).astype(BF16)
        du = _dot_nt(gb, b_ref[0])

        @pl.when(pl.program_id(1) % sub == 0)
        def _():
            du_ref[...] = du + d_ref[...] * dy_ref[...]

        @pl.when(pl.program_id(1) % sub != 0)
        def _():
            du_ref[...] += du

        db_ref[0, 0] = _dot_tn(u_ref[...].astype(BF16), gb)
        dc_ref[0, 0] = _dot_tn(s_ref[0].reshape(rows, 2 * lb), dyb)

    row_blk = pl.BlockSpec((rows, kq), lambda b, l: (b, l // sub))
    in_map = pl.BlockSpec((1, kq, 2 * lb), lambda b, l: (l // sub, 0, l % sub))
    out_map = pl.BlockSpec((1, 2 * lb, kq), lambda b, l: (l // sub, l % sub, 0))
    return pl.pallas_call(
        body, grid=(nb, n_l),
        in_specs=[row_blk, row_blk, pl.BlockSpec((1, tc, nc, 2 * lb), lambda b, l: (b, 0, 0, l)),
                  pl.BlockSpec((1, 2 * lb), lambda b, l: (0, l)), in_map, out_map,
                  pl.BlockSpec((1, kq), lambda b, l: (0, l // sub))],
        out_specs=[row_blk, pl.BlockSpec((1, 1, kq, 2 * lb), lambda b, l: (b, l // sub, 0, l % sub)),
                   pl.BlockSpec((1, 1, 2 * lb, kq), lambda b, l: (b, l // sub, l % sub, 0)),
                   pl.BlockSpec((1, 1, 2 * lb), lambda b, l: (b, 0, l))],
        out_shape=[jax.ShapeDtypeStruct((n, dh), F32), jax.ShapeDtypeStruct((nb, n_blk, kq, cols), F32),
                   jax.ShapeDtypeStruct((nb, n_blk, cols, kq), F32), jax.ShapeDtypeStruct((nb, 1, n_l * 2 * lb), F32)],
        scratch_shapes=[pltpu.VMEM((tc, nc, 2 * lb), F32)] + [pltpu.VMEM((nc, ln), F32)] * 2,
        name="s5_scan_bwd", compiler_params=_params(2))(dy_pre, u5p, s_cat, a_cat_conj, bb_cat, cc_cat, d_row)


def kernel(x, mem, g_mix_pre, w_in, gm_ln_g, gm_ln_b, gm_w_s, gm_b_s, s5_lam_re, s5_lam_im, s5_log_step, s5_b_re, s5_b_im, s5_c_re, s5_c_im, s5_d, s5_w_glu, w_br_gm, w_br_s5, w_mix_out, g_mix_post, g_ca_pre, g_mem, ca_w_q, ca_w_kv, ca_w_o, g_ca_post, g_ffn_pre, ffn_w_gu, ffn_w_down, g_ffn_post, loss_target, m_g_mix_pre, m_w_in, m_gm_ln_g, m_gm_ln_b, m_gm_w_s, m_gm_b_s, m_s5_lam_re, m_s5_lam_im, m_s5_log_step, m_s5_b_re, m_s5_b_im, m_s5_c_re, m_s5_c_im, m_s5_d, m_s5_w_glu, m_w_br_gm, m_w_br_s5, m_w_mix_out, m_g_mix_post, m_g_ca_pre, m_g_mem, m_ca_w_q, m_ca_w_kv, m_ca_w_o, m_g_ca_post, m_g_ffn_pre, m_ffn_w_gu, m_ffn_w_down, m_g_ffn_post, v_g_mix_pre, v_w_in, v_gm_ln_g, v_gm_ln_b, v_gm_w_s, v_gm_b_s, v_s5_lam_re, v_s5_lam_im, v_s5_log_step, v_s5_b_re, v_s5_b_im, v_s5_c_re, v_s5_c_im, v_s5_d, v_s5_w_glu, v_w_br_gm, v_w_br_s5, v_w_mix_out, v_g_mix_post, v_g_ca_pre, v_g_mem, v_ca_w_q, v_ca_w_kv, v_ca_w_o, v_g_ca_post, v_g_ffn_pre, v_ffn_w_gu, v_ffn_w_down, v_g_ffn_post):
    args = locals()
    W = {n: args[n][0] for n in WEIGHTS}
    M = {n: args['m_' + n][0] for n in WEIGHTS}
    V = {n: args['v_' + n][0] for n in WEIGHTS}
    for k, kind in SHARDED.items():
        if kind == 'colT':
            W[k], M[k], V[k] = W[k].T, M[k].T, V[k].T

    nb, seq, d = x.shape
    n = nb * seq
    tm = min(ROW_TILE, n)
    nmem = mem.shape[1]
    d2, dh = 2 * d, d // 2
    hd = d // CA_HEADS
    x2d = x.reshape(n, d)
    tgt = loss_target.reshape(n, d)
    mem2d = mem.reshape(nb * nmem, d)

    def row(v):
        return v.reshape(1, -1)

    names = list(SHARDED)
    later = [k for k in names if k != 'w_in']
    first_g, *casts = _first_gather(W['w_in'], [W[k] for k in later])
    shard_b = dict(zip(later, casts))
    half = shard_b['ffn_w_gu'].shape[0] // 2
    gu_halves = [shard_b['ffn_w_gu'][:half], shard_b['ffn_w_gu'][half:]]

    def assemble(k, gth):
        r, c = gth.shape[1:]
        return gth.transpose(1, 0, 2).reshape(r, N_DEV * c) if SHARDED[k] == 'col' else gth.reshape(N_DEV * r, c)

    w_in_t = assemble('w_in', first_g)
    ffh = N_DEV * W['ffn_w_down'].shape[0]

    (a_re, a_im, bb_re, bb_im), s5_in_vjp = jax.vjp(
        _s5_in_tables, W['s5_lam_re'], W['s5_lam_im'], W['s5_log_step'], W['s5_b_re'], W['s5_b_im'])
    (cc_re, cc_im), s5_out_vjp = jax.vjp(_s5_out_tables, W['s5_c_re'], W['s5_c_im'])
    bb_re_b, bb_im_b, cc_re_b, cc_im_b = (t.astype(BF16) for t in (bb_re, bb_im, cc_re, cc_im))
    s5_d_row = row(W['s5_d'])
    tril = jnp.tril(jnp.ones((GM_CHUNK, GM_CHUNK), bool))
    w_s = jnp.where(tril[None], W['gm_w_s'], 0.0).astype(BF16)
    w_s_t = w_s.transpose(0, 2, 1)
    gm_bias = jnp.repeat(W['gm_b_s'].T, d // GM_GROUPS, axis=1)

    def in_proj_body(x_ref, g_ref, w_ref, zgm_ref, u5_ref, zga_ref, zgb_ref, h_ref):
        hb = _rms(x_ref[...], g_ref[...]).astype(BF16)
        h_ref[...] = hb
        for lo in range(0, w_ref.shape[0], 512):
            acc = _dot_nt(hb, w_ref[lo:lo + 512, :])
            if lo < d2:
                zgm_ref[:, lo:lo + 512] = acc.astype(BF16)
            elif lo < d2 + dh:
                u5_ref[...] = acc
            elif lo < d2 + dh + d:
                zga_ref[:, lo - d2 - dh:lo - d2 - dh + 512] = acc.astype(BF16)
            else:
                zgb_ref[:, lo - d2 - dh - d:lo - d2 - dh - d + 512] = acc.astype(BF16)

    ride = ['s5_w_glu', 'w_br_gm', 'w_br_s5', 'w_mix_out', 'ca_w_q']
    z_gm, u5, z_ga, z_gb, h0, *got = _rowcall(
        "in_proj", in_proj_body, n, tm, [(x2d, 'row'), (row(W['g_mix_pre']), 'full'), (w_in_t, 'full')],
        [(d2, BF16), (dh, F32), (d, BF16), (d, BF16), (d, BF16)], gather=[shard_b[k] for k in ride], two_level=True)
    w_glu_f, w_brgm_f, w_brs5_f, w_mix_f, w_q_f = (assemble(k, g) for k, g in zip(ride, got))

    def gm_recompute(z_ref, lng_ref, lnb_ref, grad_ref=None):
        if grad_ref is None:
            zg = _gelu(z_ref[...].astype(F32))
        else:
            zg, grad_ref[...] = _gelu_and_grad(z_ref[...].astype(F32))
        u, v = zg[:, :d], zg[:, d:]
        vc = v - jnp.mean(v, axis=-1, keepdims=True)
        rstd = lax.rsqrt(jnp.mean(vc * vc, axis=-1, keepdims=True) + EPS)
        vhat = vc * rstd
        vn = vhat * lng_ref[...] + lnb_ref[...]
        return u, vhat, rstd, vn.astype(BF16)

    gw = d // GM_GROUPS

    def gm_fwd_body(z_ref, lng_ref, lnb_ref, ws_ref, bias_ref, y_ref):
        u, _, _, vnb = gm_recompute(z_ref, lng_ref, lnb_ref)
        for g in range(GM_GROUPS):
            sl = slice(g * gw, (g + 1) * gw)
            sv = _dot(ws_ref[g], vnb[:, sl]) + bias_ref[:, sl]
            y_ref[:, sl] = (u[:, sl] * sv).astype(BF16)

    y_gm, g_kv, g_o = _rowcall("gmlp_fwd", gm_fwd_body, n, GM_CHUNK,
                               [(z_gm, 'row'), (row(W['gm_ln_g']), 'full'), (row(W['gm_ln_b']), 'full'), (w_s, 'full'),
                                (gm_bias, 'full')], [(d, BF16)], gather=[shard_b['ca_w_kv'], shard_b['ca_w_o']],
                               two_level=True)
    w_kv_f, w_o_f = assemble('ca_w_kv', g_kv), assemble('ca_w_o', g_o)

    tc = seq // SCAN_CHUNKS
    lt = S5_GROUPS * S5_STATE

    def to_scan_order(t):
        return t.reshape(nb, SCAN_CHUNKS, tc, t.shape[-1]).transpose(0, 2, 1, 3).reshape(n, t.shape[-1])

    def from_scan_order(t):
        return t.reshape(nb, tc, SCAN_CHUNKS, t.shape[-1]).transpose(0, 2, 1, 3).reshape(n, t.shape[-1])

    u5p = to_scan_order(u5)
    a_cat, bb_cat, cc_cat = _s5_cat_tables(a_re, a_im, bb_re_b, bb_im_b, cc_re_b, cc_im_b)
    a_cat_conj = _s5_cat_tables(a_re, -a_im, bb_re_b, bb_im_b, cc_re_b, cc_im_b)[0]
    s_cat, y_lin = _s5_scan_fwd(u5p, a_cat, bb_cat, cc_cat, nb)

    def s5_out_body(yl_ref, u_ref, d_ref, wg_ref, ypre_ref, gate_ref, y_ref):
        ypre = yl_ref[...] + d_ref[...] * u_ref[...]
        ypre_ref[...] = ypre
        yg = _gelu(ypre)
        gate = _dot(yg.astype(BF16), wg_ref[...])
        gate_ref[...] = gate
        y_ref[...] = (yg * _sigmoid(gate)).astype(BF16)

    tm5 = min(1024, n)
    y_pre, gate, y_s5p = _rowcall(
        "s5_out", s5_out_body, n, tm5, [(y_lin, 'row'), (u5p, 'row'), (s5_d_row, 'full'), (w_glu_f, 'full')],
        [(dh, F32), (dh, F32), (dh, BF16)])
    y_s5 = from_scan_order(y_s5p)

    def merge_body(ygm_ref, ys5_ref, zga_ref, zgb_ref, wa_ref, wb_ref, pa_ref, pb_ref, mg_ref):
        pa = _dot(ygm_ref[...], wa_ref[...])
        pb = _dot(ys5_ref[...], wb_ref[...])
        pa_ref[...] = pa.astype(BF16)
        pb_ref[...] = pb.astype(BF16)
        mg_ref[...] = (_sigmoid(zga_ref[...].astype(F32)) * pa + _sigmoid(zgb_ref[...].astype(F32)) * pb).astype(BF16)

    p_a, p_b, merged, g_down = _rowcall(
        "merge", merge_body, n, tm,
        [(y_gm, 'row'), (y_s5, 'row'), (z_ga, 'row'), (z_gb, 'row'), (w_brgm_f, 'full'), (w_brs5_f, 'full')],
        [(d, BF16), (d, BF16), (d, BF16)], gather=[shard_b['ffn_w_down']], two_level=True)
    w_down_f = assemble('ffn_w_down', g_down)

    def close_sublayer(name, a_in, w_out, x_res, g_post, g_next, w_next=None, gather=()):
        def body(*refs):
            a_ref, w_ref, x_ref, gp_ref, gn_ref = refs[:5]
            rest = refs[5:]
            if w_next is not None:
                wn_ref, rest = rest[0], rest[1:]
            o_ref, xo_ref, h_ref = rest[:3]
            o = _dot(a_ref[...], w_ref[...])
            o_ref[...] = o.astype(BF16)
            xo = x_ref[...] + _rms(o, gp_ref[...])
            xo_ref[...] = xo
            hb = _rms(xo, gn_ref[...]).astype(BF16)
            h_ref[...] = hb
            if w_next is not None:
                rest[3][...] = _dot(hb, wn_ref[...]).astype(BF16)

        ins = [(a_in, 'row'), (w_out, 'full'), (x_res, 'row'), (row(g_post), 'full'), (row(g_next), 'full')]
        outs = [(d, BF16), (d, F32), (d, BF16)]
        if w_next is not None:
            ins.append((w_next, 'full'))
            outs.append((w_next.shape[1], BF16))
        return _rowcall(name, body, n, tm, ins, outs, gather=gather, two_level=True)

    o1, x1, hc, q, g_gu0 = close_sublayer("mix_out", merged, w_mix_f, x2d, W['g_mix_post'], W['g_ca_pre'], w_q_f,
                                          gather=[gu_halves[0]])

    tmm = min(ROW_TILE, nb * nmem)

    def memkv_body(m_ref, g_ref, w_ref, mn_ref, k_ref, v_ref):
        mnb = _rms(m_ref[...], g_ref[...]).astype(BF16)
        mn_ref[...] = mnb
        k_ref[...] = _dot(mnb, w_ref[:, :d]).astype(BF16)
        v_ref[...] = _dot(mnb, w_ref[:, d:]).astype(BF16)

    mem_n, k_mem, v_mem = _rowcall("mem_kv", memkv_body, nb * nmem, tmm,
                                   [(mem2d, 'row'), (row(W['g_mem']), 'full'), (w_kv_f, 'full')],
                                   [(d, BF16), (d, BF16), (d, BF16)])

    tiles_per_ex = seq // tm
    kv_spec = pl.BlockSpec((nmem, d), lambda i: (i // tiles_per_ex, 0))
    scale = hd ** -0.5

    def softmax_rows(qh, kh):
        s = _dot_nt(qh, kh) * scale
        e = jnp.exp(s - jnp.max(s, axis=-1, keepdims=True))
        return e / jnp.sum(e, axis=-1, keepdims=True)

    def attn_body(q_ref, k_ref, v_ref, o_ref):
        for h in range(CA_HEADS):
            sl = slice(h * hd, (h + 1) * hd)
            p = softmax_rows(q_ref[:, sl], k_ref[:, sl])
            o_ref[:, sl] = _dot(p.astype(BF16), v_ref[:, sl]).astype(BF16)

    (att,) = _rowcall("attn_fwd", attn_body, n, tm, [(q, 'row'), (k_mem, kv_spec), (v_mem, kv_spec)], [(d, BF16)])

    o2, x2, hf, g_gu1 = close_sublayer("attn_out", att, w_o_f, x1, W['g_ca_post'], W['g_ffn_pre'],
                                       gather=[gu_halves[1]])
    w_gu_t = jnp.stack([g_gu0, g_gu1], axis=1).reshape(2 * ffh, d)

    ck = 256

    def ffn_up_body(h_ref, w_ref, gu_ref, a_ref):
        hb = h_ref[...]
        for lo in range(0, ffh, ck):
            gt = _dot_nt(hb, w_ref[lo:lo + ck, :])
            ut = _dot_nt(hb, w_ref[ffh + lo:ffh + lo + ck, :])
            gu_ref[:, lo:lo + ck] = gt.astype(BF16)
            gu_ref[:, ffh + lo:ffh + lo + ck] = ut.astype(BF16)
            a_ref[:, lo:lo + ck] = ((gt * _sigmoid(gt)) * ut).astype(BF16)

    gu, act = _rowcall("ffn_up", ffn_up_body, n, tm, [(hf, 'row'), (w_gu_t, 'full')], [(2 * ffh, BF16), (ffh, BF16)])

    def ffn_down_body(a_ref, w_ref, x_ref, t_ref, g_ref, dx_ref, do_ref, loss_ref, dg_ref):
        i = pl.program_id(0)

        @pl.when(i == 0)
        def _():
            loss_ref[...] = jnp.zeros_like(loss_ref)
            dg_ref[...] = jnp.zeros_like(dg_ref)

        o = _dot(a_ref[...], w_ref[...])
        diff = x_ref[...] + _rms(o, g_ref[...]) - t_ref[...]
        loss_ref[...] += jnp.full(loss_ref.shape, 0.5 / d, F32) * jnp.sum(diff * diff)
        dx = diff * (1.0 / d)
        dx_ref[...] = dx
        do, dg = _rms_bwd(o, g_ref[...], dx)
        do_ref[...] = do.astype(BF16)
        dg_ref[...] += dg

    dx3, do3, loss_part, dg_ffn_post = _rowcall(
        "ffn_down_loss", ffn_down_body, n, tm,
        [(act, 'row'), (w_down_f, 'full'), (x2, 'row'), (tgt, 'row'), (row(W['g_ffn_post']), 'full')],
        [(d, F32), (d, BF16)], accs=[((1, 128), F32), ((1, d), F32)])

    G = {'g_ffn_post': dg_ffn_post}
    RECV = {}

    def parts_of(k, gfull):
        r, c = W[k].shape
        return gfull.reshape(r, N_DEV, c).transpose(1, 0, 2) if SHARDED[k] == 'col' else gfull.reshape(N_DEV, r, c)

    p_down = parts_of('ffn_w_down', _dwcall("dw_ffn_down", act, do3, BF16))

    def ffn_act_bwd_body(do_ref, w_ref, gu_ref, dgu_ref):
        dob = do_ref[...]
        for lo in range(0, ffh, ck):
            da = _dot_nt(dob, w_ref[lo:lo + ck, :])
            gt = gu_ref[:, lo:lo + ck].astype(F32)
            ut = gu_ref[:, ffh + lo:ffh + lo + ck].astype(F32)
            sg = _sigmoid(gt)
            dgu_ref[:, lo:lo + ck] = (da * ut * (sg * (1.0 + gt * (1.0 - sg)))).astype(BF16)
            dgu_ref[:, ffh + lo:ffh + lo + ck] = (da * (gt * sg)).astype(BF16)

    dgu, RECV['ffn_w_down'] = _rowcall("ffn_act_bwd", ffn_act_bwd_body, n, tm,
                                       [(do3, 'row'), (w_down_f, 'full'), (gu, 'row')], [(2 * ffh, BF16)],
                                       scatter=[p_down])
    p_gu = parts_of('ffn_w_gu', _dwcall("dw_ffn_gu", dgu, hf, BF16))

    def open_sublayer(name, pieces, w_full, x_in, g_pre, dx_up, o_prev=None, g_post_prev=None, scatter=(), gather=(),
                      w_t=False):
        n_p = len(pieces)
        second = o_prev is not None

        def body(*refs):
            dc_refs, (w_ref, x_ref, g_ref, dxu_ref), rest = refs[:n_p], refs[n_p:n_p + 4], refs[n_p + 4:]
            if second:
                (op_ref, gp_ref), rest = rest[:2], rest[2:]
            i = pl.program_id(0)
            dhid = None
            for dc_ref, (_, lo, hi) in zip(dc_refs, pieces):
                part = _dot(dc_ref[...], w_ref[lo:hi, :]) if w_t else _dot_nt(dc_ref[...], w_ref[:, lo:hi])
                dhid = part if dhid is None else dhid + part
            dxn, dg = _rms_bwd(x_ref[...], g_ref[...], dhid)
            dx = dxu_ref[...] + dxn
            if second:
                dx_ref, do_ref, dg_ref, dg2_ref = rest
            else:
                dx_ref, dg_ref = rest

            @pl.when(i == 0)
            def _():
                dg_ref[...] = jnp.zeros_like(dg_ref)
                if second:
                    dg2_ref[...] = jnp.zeros_like(dg2_ref)

            dx_ref[...] = dx
            dg_ref[...] += dg
            if second:
                do, dg2 = _rms_bwd(op_ref[...].astype(F32), gp_ref[...], dx)
                do_ref[...] = do.astype(BF16)
                dg2_ref[...] += dg2

        ins = [(p[0], 'row') for p in pieces] + [(w_full, 'full'), (x_in, 'row'), (row(g_pre), 'full'), (dx_up, 'row')]
        outs = [(d, F32)]
        accs = [((1, d), F32)]
        if second:
            ins += [(o_prev, 'row'), (row(g_post_prev), 'full')]
            outs.append((d, BF16))
            accs.append(((1, d), F32))
        res = _rowcall(name, body, n, tm, ins, outs, accs=accs, scatter=scatter, gather=gather)
        if second:
            dx, do, dg, dg2 = res[:4]
            return dx, dg, do, dg2, res[4:]
        return res[0], res[1], res[2:]

    dx2, G['g_ffn_pre'], do2, G['g_ca_post'], (RECV['ffn_w_gu'],) = open_sublayer(
        "ffn_in_bwd", [(dgu, 0, 2 * ffh)], w_gu_t, x2, W['g_ffn_pre'], dx3, o2, W['g_ca_post'], scatter=[p_gu],
        w_t=True)
    p_o = parts_of('ca_w_o', _dwcall("dw_ca_o", att, do2, BF16))

    def attn_bwd_body(q_ref, k_ref, v_ref, do_ref, wo_ref, dq_ref, dk_ref, dv_ref):
        i = pl.program_id(0)

        @pl.when(i % tiles_per_ex == 0)
        def _():
            dk_ref[...] = jnp.zeros_like(dk_ref)
            dv_ref[...] = jnp.zeros_like(dv_ref)

        d_att = _dot_nt(do_ref[...], wo_ref[...]).astype(BF16)
        for h in range(CA_HEADS):
            sl = slice(h * hd, (h + 1) * hd)
            qh, kh, vh, dah = q_ref[:, sl], k_ref[:, sl], v_ref[:, sl], d_att[:, sl]
            p = softmax_rows(qh, kh)
            dp = _dot_nt(dah, vh)
            ds = (p * (dp - jnp.sum(p * dp, axis=-1, keepdims=True)) * scale).astype(BF16)
            dq_ref[:, sl] = _dot(ds, kh).astype(BF16)
            dk_ref[:, sl] += _dot_tn(ds, qh)
            dv_ref[:, sl] += _dot_tn(p.astype(BF16), dah)

    kv_acc = ((nb * nmem, d), (F32, pl.BlockSpec((nmem, d), lambda i: (i // tiles_per_ex, 0))))
    dq, dk_mem, dv_mem, RECV['ca_w_o'] = _rowcall(
        "attn_bwd", attn_bwd_body, n, tm,
        [(q, 'row'), (k_mem, kv_spec), (v_mem, kv_spec), (do2, 'row'), (w_o_f, 'full')], [(d, BF16)],
        accs=[kv_acc, kv_acc], scatter=[p_o])
    p_q = parts_of('ca_w_q', _dwcall("dw_ca_q", hc, dq, BF16))

    def memkv_bwd_body(dk_ref, dv_ref, m_ref, g_ref, w_ref, dkv_ref, dg_ref):
        i = pl.program_id(0)

        @pl.when(i == 0)
        def _():
            dg_ref[...] = jnp.zeros_like(dg_ref)

        dkb, dvb = dk_ref[...].astype(BF16), dv_ref[...].astype(BF16)
        dkv_ref[:, :d] = dkb
        dkv_ref[:, d:] = dvb
        dmn = _dot_nt(dkb, w_ref[:, :d]) + _dot_nt(dvb, w_ref[:, d:])
        _, dg = _rms_bwd(m_ref[...], g_ref[...], dmn)
        dg_ref[...] += dg

    dkv, G['g_mem'] = _rowcall(
        "mem_kv_bwd", memkv_bwd_body, nb * nmem, tmm,
        [(dk_mem, 'row'), (dv_mem, 'row'), (mem2d, 'row'), (row(W['g_mem']), 'full'), (w_kv_f, 'full')],
        [(d2, BF16)], accs=[((1, d), F32)])
    p_kv = parts_of('ca_w_kv', _dwcall("dw_ca_kv", mem_n, dkv, BF16))

    dx1, G['g_ca_pre'], do1, G['g_mix_post'], (RECV['ca_w_q'], RECV['ca_w_kv']) = open_sublayer(
        "attn_in_bwd", [(dq, 0, d)], w_q_f, x1, W['g_ca_pre'], dx2, o1, W['g_mix_post'], scatter=[p_q, p_kv])
    p_mix = parts_of('w_mix_out', _dwcall("dw_mix_out", merged, do1, BF16))

    def merge_bwd_body(do_ref, wm_ref, zga_ref, zgb_ref, pa_ref, pb_ref, wb_ref, dpa_ref, dpb_ref, dza_ref, dzb_ref,
                       dys_ref):
        dm = _dot_nt(do_ref[...], wm_ref[...])
        sa, sb = _sigmoid(zga_ref[...].astype(F32)), _sigmoid(zgb_ref[...].astype(F32))
        dpb = (dm * sb).astype(BF16)
        dpa_ref[...] = (dm * sa).astype(BF16)
        dpb_ref[...] = dpb
        dza_ref[...] = (dm * pa_ref[...].astype(F32) * (sa * (1.0 - sa))).astype(BF16)
        dzb_ref[...] = (dm * pb_ref[...].astype(F32) * (sb * (1.0 - sb))).astype(BF16)
        dys_ref[...] = _dot_nt(dpb, wb_ref[...]).astype(BF16)

    NORM_ROWS = ['g_ffn_post', 'g_ffn_pre', 'g_ca_post', 'g_mem', 'g_ca_pre', 'g_mix_post']
    norm_rows = jnp.concatenate([G[k] for k in NORM_ROWS] + [jnp.tile(loss_part, (1, d // 128))], axis=0)
    dp_a, dp_b, dz_ga, dz_gb, dy_s5, all_norm_rows, RECV['w_mix_out'] = _rowcall(
        "merge_bwd", merge_bwd_body, n, tm,
        [(do1, 'row'), (w_mix_f, 'full'), (z_ga, 'row'), (z_gb, 'row'), (p_a, 'row'), (p_b, 'row'), (w_brs5_f, 'full')],
        [(d, BF16), (d, BF16), (d, BF16), (d, BF16), (dh, BF16)], scatter=[p_mix], gather=[norm_rows])
    p_brgm = parts_of('w_br_gm', _dwcall("dw_br_gm", y_gm, dp_a, BF16))
    p_brs5 = parts_of('w_br_s5', _dwcall("dw_br_s5", y_s5, dp_b, BF16))

    def gm_bwd_body(z_ref, dpa_ref, wa_ref, lng_ref, lnb_ref, ws_ref, wst_ref, bias_ref, dz_ref, dws_ref, dbias_ref,
                    dlng_ref, dlnb_ref, du_s, dvn_s, dgelu_s):
        i = pl.program_id(0)

        @pl.when(i == 0)
        def _():
            dws_ref[...] = jnp.zeros_like(dws_ref)
            dbias_ref[...] = jnp.zeros_like(dbias_ref)
            dlng_ref[...] = jnp.zeros_like(dlng_ref)
            dlnb_ref[...] = jnp.zeros_like(dlnb_ref)

        u, vhat, rstd, vnb = gm_recompute(z_ref, lng_ref, lnb_ref, dgelu_s)
        dy = _dot_nt(dpa_ref[...], wa_ref[...])
        for g in range(GM_GROUPS):
            sl = slice(g * gw, (g + 1) * gw)
            sv = _dot(ws_ref[g], vnb[:, sl]) + bias_ref[:, sl]
            du_s[:, sl] = dy[:, sl] * sv
            dsv = dy[:, sl] * u[:, sl]
            dsvb = dsv.astype(BF16)
            dvn_s[:, sl] = _dot(wst_ref[g], dsvb)
            dws_ref[g] += _dot_nt(dsvb, vnb[:, sl])
            dbias_ref[:, sl] += dsv
        dvn = dvn_s[...]
        dlng_ref[...] += jnp.sum(dvn * vhat, axis=0, keepdims=True)
        dlnb_ref[...] += jnp.sum(dvn, axis=0, keepdims=True)
        dvh = dvn * lng_ref[...]
        dv = rstd * (dvh - jnp.mean(dvh, axis=-1, keepdims=True) - vhat * jnp.mean(dvh * vhat, axis=-1, keepdims=True))
        dz_ref[:, :d] = (du_s[...] * dgelu_s[:, :d]).astype(BF16)
        dz_ref[:, d:] = (dv * dgelu_s[:, d:]).astype(BF16)

    dz_gm, dws_full, dbias_full, G['gm_ln_g'], G['gm_ln_b'], RECV['w_br_gm'], RECV['w_br_s5'] = _rowcall(
        "gmlp_bwd", gm_bwd_body, n, GM_CHUNK,
        [(z_gm, 'row'), (dp_a, 'row'), (w_brgm_f, 'full'), (row(W['gm_ln_g']), 'full'), (row(W['gm_ln_b']), 'full'),
         (w_s, 'full'), (w_s_t, 'full'), (gm_bias, 'full')],
        [(d2, BF16)], accs=[((GM_GROUPS, GM_CHUNK, GM_CHUNK), F32), ((GM_CHUNK, d), F32), ((1, d), F32), ((1, d), F32)],
        scratch=[pltpu.VMEM((GM_CHUNK, d), F32), pltpu.VMEM((GM_CHUNK, d), F32), pltpu.VMEM((GM_CHUNK, d2), F32)],
        scatter=[p_brgm, p_brs5])
    G['gm_w_s'] = jnp.where(tril[None], dws_full, 0.0)
    G['gm_b_s'] = dbias_full.reshape(GM_CHUNK, GM_GROUPS, gw).sum(-1).T

    dy_s5p = to_scan_order(dy_s5)

    def s5_out_bwd_body(dy_ref, ypre_ref, gate_ref, u_ref, wg_ref, dyp_ref, dgate_ref, yg_ref, dd_ref):
        i = pl.program_id(0)

        @pl.when(i == 0)
        def _():
            dd_ref[...] = jnp.zeros_like(dd_ref)

        dy = dy_ref[...].astype(F32)
        yg, dgelu = _gelu_and_grad(ypre_ref[...])
        sg = _sigmoid(gate_ref[...])
        dgb = (dy * yg * (sg * (1.0 - sg))).astype(BF16)
        dgate_ref[...] = dgb
        yg_ref[...] = yg.astype(BF16)
        dyp = (dy * sg + _dot_nt(dgb, wg_ref[...])) * dgelu
        dyp_ref[...] = dyp
        dd_ref[...] += jnp.sum(dyp * u_ref[...], axis=0, keepdims=True)

    dy_pre, dgate, yg_b, dd = _rowcall(
        "s5_out_bwd", s5_out_bwd_body, n, tm5,
        [(dy_s5p, 'row'), (y_pre, 'row'), (gate, 'row'), (u5p, 'row'), (w_glu_f, 'full')],
        [(dh, F32), (dh, BF16), (dh, BF16)], accs=[((1, dh), F32)])
    du5p, d_bb_cat, d_cc_cat, da_cat = _s5_scan_bwd(dy_pre, u5p, s_cat, a_cat_conj, bb_cat, cc_cat, s5_d_row)
    du5 = from_scan_order(du5p.astype(BF16))
    d_bb_re, d_bb_im = _s5_uncat(jnp.sum(d_bb_cat, axis=0), 2)
    d_cc_re, d_cc_neg_im = _s5_uncat(jnp.sum(d_cc_cat, axis=0), 1)
    da_re, da_im = _s5_uncat(jnp.sum(da_cat, axis=0), 1)
    lane_shape = (S5_GROUPS, 8, 128)
    g_c_re, g_c_im = (t.reshape(lane_shape) for t in s5_out_vjp((d_cc_re, -d_cc_neg_im)))
    g_d = dd.reshape(S5_GROUPS, S5_CH)
    g_lam_re, g_lam_im, g_log_step, g_b_re, g_b_im = s5_in_vjp((da_re, da_im, d_bb_re, d_bb_im))
    g_log_step, g_b_re, g_b_im = g_log_step.reshape(1, -1), g_b_re.reshape(lane_shape), g_b_im.reshape(lane_shape)
    p_glu = parts_of('s5_w_glu', _dwcall("dw_s5_glu", yg_b, dgate, BF16))

    pieces = [(dz_gm, 0, d2), (du5, d2, d2 + dh), (dz_ga, d2 + dh, d2 + dh + d), (dz_gb, d2 + dh + d, d2 + dh + 2 * d)]
    gm_ln_rows = jnp.concatenate([G['gm_ln_g'], G['gm_ln_b']], axis=0)
    dw_gm, all_lam_re, all_lam_im, all_log_step, all_b_re, all_b_im = _dwcall(
        "dw_in_0", dz_gm, h0, BF16, gather=[g_lam_re, g_lam_im, g_log_step, g_b_re, g_b_im])
    dw_s5 = _dwcall("dw_in_1", du5, h0, BF16)
    dw_ga, all_gm_w_s, all_gm_b_s, all_gm_ln = _dwcall(
        "dw_in_2", dz_ga, h0, BF16, gather=[G['gm_w_s'], G['gm_b_s'], gm_ln_rows])
    dw_gb, all_c_re, all_c_im, all_d, RECV['s5_w_glu'] = _dwcall(
        "dw_in_3", dz_gb, h0, BF16, gather=[g_c_re, g_c_im, g_d], scatter=[p_glu])
    p_in = parts_of('w_in', jnp.concatenate([dw_gm, dw_s5, dw_ga, dw_gb], axis=0))
    grad_x2d, g_mix_pre_part, (RECV['w_in'],) = open_sublayer(
        "in_proj_bwd", pieces, w_in_t, x2d, W['g_mix_pre'], dx1, scatter=[p_in], w_t=True)

    out = {}
    for k in names:
        quad = _sum_adamw("sum_adamw_" + k, RECV[k], W[k], M[k], V[k])
        out[k] = [(t.T if SHARDED[k] == 'colT' else t)[None] for t in quad]
    gathered = [all_norm_rows, all_gm_w_s, all_gm_b_s, all_gm_ln, all_c_re, all_c_im, all_d, all_lam_re, all_lam_im,
                all_log_step, all_b_re, all_b_im, _gather_last(g_mix_pre_part)]
    where = {'g_ffn_post': (0, 0), 'g_ffn_pre': (0, 1), 'g_ca_post': (0, 2), 'g_mem': (0, 3), 'g_ca_pre': (0, 4),
             'g_mix_post': (0, 5), 'gm_w_s': (1, None), 'gm_b_s': (2, None), 'gm_ln_g': (3, 0), 'gm_ln_b': (3, 1),
             's5_c_re': (4, None), 's5_c_im': (5, None), 's5_d': (6, None), 's5_lam_re': (7, None),
             's5_lam_im': (8, None), 's5_log_step': (9, 0), 's5_b_re': (10, None), 's5_b_im': (11, None),
             'g_mix_pre': (12, 0)}
    folded = ('s5_b_re', 's5_b_im', 's5_c_re', 's5_c_im')

    def as_updated(k, t):
        return t.reshape((1,) + lane_shape) if k in folded else t

    small, loss_row = _small_adamw(
        [(as_updated(k, args[k]), as_updated(k, args['m_' + k]), as_updated(k, args['v_' + k])) + where[k]
         for k in SMALL], gathered, (0, len(NORM_ROWS)))
    out.update({k: [t.reshape(args[k].shape) for t in quad] for k, quad in zip(SMALL, small)})

    res = [loss_row[0, 0], grad_x2d.reshape(x.shape)]
    for j in range(4):
        res += [out[k][j] for k in WEIGHTS]
    return tuple(res)
```

```python
import functools
import math

import jax
import jax.numpy as jnp
from jax import lax
from jax.experimental import pallas as pl
from jax.experimental.pallas import tpu as pltpu

F32 = jnp.float32
BF16 = jnp.bfloat16
EPS = 1e-6
N_DEV = 8
V7X_VMEM_LIMIT = 56 * 1024 * 1024
ROW_TILE = 256
WIDE_ROW_TILE = 512
GM_CHUNK = 128
GM_GROUPS = 8
S5_GROUPS = 32
S5_STATE = 64
S5_CH = 16
SCAN_CHUNKS = 32
SCAN_LANES = 128
S5_BLOCK_LANES = 256
CA_HEADS = 4
ADAM_LR, ADAM_B1, ADAM_B2, ADAM_EPS, ADAM_WD, ADAM_STEP = 0.001, 0.9, 0.999, 1e-08, 0.01, 10

WEIGHTS = ['g_mix_pre', 'w_in', 'gm_ln_g', 'gm_ln_b', 'gm_w_s', 'gm_b_s', 's5_lam_re', 's5_lam_im', 's5_log_step',
           's5_b_re', 's5_b_im', 's5_c_re', 's5_c_im', 's5_d', 's5_w_glu', 'w_br_gm', 'w_br_s5', 'w_mix_out',
           'g_mix_post', 'g_ca_pre', 'g_mem', 'ca_w_q', 'ca_w_kv', 'ca_w_o', 'g_ca_post', 'g_ffn_pre', 'ffn_w_gu',
           'ffn_w_down', 'g_ffn_post']
SHARDED = {'w_in': 'colT', 's5_w_glu': 'row', 'w_br_gm': 'row', 'w_br_s5': 'col', 'w_mix_out': 'row',
           'ca_w_q': 'row', 'ca_w_kv': 'col', 'ca_w_o': 'row', 'ffn_w_gu': 'colT', 'ffn_w_down': 'row'}
SMALL = [n for n in WEIGHTS if n not in SHARDED]


def _rms(x, g):
    r = lax.rsqrt(jnp.mean(x * x, axis=-1, keepdims=True) + EPS)
    return (x * r) * g


def _rms_bwd(x, g, dy):
    r = lax.rsqrt(jnp.mean(x * x, axis=-1, keepdims=True) + EPS)
    n = x * r
    dn = dy * g
    dx = r * (dn - n * jnp.mean(dn * n, axis=-1, keepdims=True))
    return dx, jnp.sum(dy * n, axis=0, keepdims=True)


_GELU_C = math.sqrt(2.0 / math.pi)


def _gelu(x):
    return 0.5 * x * (1.0 + jnp.tanh(_GELU_C * (x + 0.044715 * (x * x * x))))


def _gelu_and_grad(x):
    x2 = x * x
    t = jnp.tanh(_GELU_C * (x + 0.044715 * (x2 * x)))
    h = 0.5 * (1.0 + t)
    return x * h, h + 0.5 * x * (1.0 - t * t) * (_GELU_C * (1.0 + 3.0 * 0.044715 * x2))


def _sigmoid(x):
    return 0.5 * (1.0 + jnp.tanh(0.5 * x))


def _dot(a, b):
    return jnp.dot(a, b, preferred_element_type=F32)


def _dot_nt(a, b):
    return lax.dot_general(a, b, (((1,), (1,)), ((), ())), preferred_element_type=F32)


def _dot_tn(a, b):
    return lax.dot_general(a, b, (((0,), (0,)), ((), ())), preferred_element_type=F32)


def _adamw(w, g, m, v):
    m = ADAM_B1 * m + (1.0 - ADAM_B1) * g
    v = ADAM_B2 * v + (1.0 - ADAM_B2) * (g * g)
    m_hat = m / (1.0 - ADAM_B1 ** ADAM_STEP)
    v_hat = v / (1.0 - ADAM_B2 ** ADAM_STEP)
    delta = -ADAM_LR * (m_hat / (jnp.sqrt(v_hat) + ADAM_EPS) + ADAM_WD * w)
    return delta, m, v


def _params(n_grid):
    return pltpu.CompilerParams(dimension_semantics=("arbitrary",) * n_grid, vmem_limit_bytes=V7X_VMEM_LIMIT)


def _my_place():
    x, y, c = lax.axis_index("x"), lax.axis_index("y"), lax.axis_index("c")
    return x, y, c


def _peer(x, y, c, k):
    px = 1 - x if k & 4 else x
    py = 1 - y if k & 2 else y
    pc = 1 - c if k & 1 else c
    return (px, py, pc), 4 * px + 2 * py + pc


def _exchange(kind, src_refs, dst_refs, sems, first, phase):
    x, y, c = _my_place()
    me = 4 * x + 2 * y + c
    send_sems, recv_sems, own_sems = sems
    for j, (src, dst) in enumerate(zip(src_refs, dst_refs), start=first):
        own = pltpu.make_async_copy(src if kind == 'gather' else src.at[me], dst.at[me], own_sems.at[j])
        if phase == 'start':
            own.start()
        for k in range(1, N_DEV):
            peer, peer_block = _peer(x, y, c, k)
            out = pltpu.make_async_remote_copy(
                src_ref=src if kind == 'gather' else src.at[peer_block], dst_ref=dst.at[me],
                send_sem=send_sems.at[7 * j + k - 1], recv_sem=recv_sems.at[7 * j + k - 1], device_id=peer,
                device_id_type=pl.DeviceIdType.MESH)
            if phase == 'start':
                out.start()
            else:
                pltpu.make_async_remote_copy(
                    src_ref=src if kind == 'gather' else src.at[peer_block], dst_ref=dst.at[peer_block],
                    send_sem=send_sems.at[7 * j + k - 1], recv_sem=recv_sems.at[7 * j + k - 1], device_id=peer,
                    device_id_type=pl.DeviceIdType.MESH).wait_recv()
                out.wait_send()
        if phase == 'wait':
            own.wait()


def _gather_two_level(src_refs, dst_refs, sems, first, phase):
    x, y, c = _my_place()
    me, sibling = (x, y, c), (x, y, 1 - c)
    chips = [(1 - x, y), (x, 1 - y), (1 - x, 1 - y)]
    send_sems, recv_sems, own_sems = sems
    for j, (src, dst) in enumerate(zip(src_refs, dst_refs), start=first):
        def rows(px, py, pc, dst=dst):
            return dst.at[4 * px + 2 * py + pc]

        def copy(k, block, to, from_src=False, j=j, src=src, rows=rows):
            return pltpu.make_async_remote_copy(
                src_ref=src if from_src else rows(*block), dst_ref=rows(*block), send_sem=send_sems.at[7 * j + k],
                recv_sem=recv_sems.at[7 * j + k], device_id=to, device_id_type=pl.DeviceIdType.MESH)

        mine = pltpu.make_async_copy(src, rows(*me), own_sems.at[j])
        first_out = [copy(0, me, sibling, True)] + [copy(1 + i, me, (*chip, c), True) for i, chip in enumerate(chips)]
        if phase == 'start':
            mine.start()
            for cp in first_out:
                cp.start()
        else:
            passed = [copy(4 + i, (*chip, c), sibling) for i, chip in enumerate(chips)]
            for i, chip in enumerate(chips):
                copy(1 + i, (*chip, c), me).wait_recv()
                passed[i].start()
            copy(0, sibling, me).wait_recv()
            for i, chip in enumerate(chips):
                copy(4 + i, (*chip, 1 - c), me).wait_recv()
            for cp in first_out + passed:
                cp.wait_send()
            mine.wait()


class _Ride:
    def __init__(self, gather=(), scatter=(), two_level=False):
        self.two_level = two_level
        self.n_gather = len(gather)
        self.arrays = list(gather) + list(scatter)
        n = len(self.arrays)
        hbm = pl.BlockSpec(memory_space=pl.ANY)
        self.in_specs = [hbm] * n
        self.out_specs = [hbm] * n
        self.out_shape = [jax.ShapeDtypeStruct((N_DEV,) + a.shape, a.dtype) for a in gather]
        self.out_shape += [jax.ShapeDtypeStruct(a.shape, a.dtype) for a in scatter]
        self.scratch = [pltpu.SemaphoreType.DMA((7 * n,)), pltpu.SemaphoreType.DMA((7 * n,)),
                        pltpu.SemaphoreType.DMA((n,))] if n else []

    def wrap(self, inner, n_in, n_out, n_scr, is_first, is_last):
        n_mv = len(self.arrays)
        if not n_mv:
            return inner

        def body(*refs):
            mv_src = refs[n_in:n_in + n_mv]
            mv_dst = refs[n_in + n_mv + n_out:n_in + 2 * n_mv + n_out]
            sems = refs[n_in + 2 * n_mv + n_out + n_scr:]

            def exchange(phase):
                if self.n_gather and self.two_level:
                    _gather_two_level(mv_src[:self.n_gather], mv_dst[:self.n_gather], sems, 0, phase)
                elif self.n_gather:
                    _exchange('gather', mv_src[:self.n_gather], mv_dst[:self.n_gather], sems, 0, phase)
                if n_mv > self.n_gather:
                    _exchange('scatter', mv_src[self.n_gather:], mv_dst[self.n_gather:], sems, self.n_gather, phase)

            pl.when(is_first())(functools.partial(exchange, 'start'))
            inner(*refs[:n_in], *refs[n_in + n_mv:n_in + n_mv + n_out],
                  *refs[n_in + 2 * n_mv + n_out:n_in + 2 * n_mv + n_out + n_scr])
            pl.when(is_last())(functools.partial(exchange, 'wait'))

        return body


def _rowcall(name, body, n_rows, tm, ins, outs, accs=(), scratch=(), gather=(), scatter=(), two_level=False):
    n_steps = n_rows // tm
    ride = _Ride(gather, scatter, two_level)
    body = ride.wrap(body, len(ins), len(outs) + len(accs), len(scratch), lambda: pl.program_id(0) == 0,
                     lambda: pl.program_id(0) == n_steps - 1)
    arrays, in_specs = [], []
    for a, kind in ins:
        arrays.append(a)
        if kind == 'row':
            in_specs.append(pl.BlockSpec((tm,) + a.shape[1:], lambda i, nd=a.ndim: (i,) + (0,) * (nd - 1)))
        elif kind == 'full':
            in_specs.append(pl.BlockSpec(a.shape, lambda i, nd=a.ndim: (0,) * nd))
        else:
            in_specs.append(kind)
    out_shape, out_specs = [], []
    for cols, dt in outs:
        out_shape.append(jax.ShapeDtypeStruct((n_rows, cols), dt))
        out_specs.append(pl.BlockSpec((tm, cols), lambda i: (i, 0)))
    for shp, dt in accs:
        if isinstance(dt, tuple):
            dt, spec = dt
        else:
            spec = pl.BlockSpec(shp, lambda i, nd=len(shp): (0,) * nd)
        out_shape.append(jax.ShapeDtypeStruct(shp, dt))
        out_specs.append(spec)
    return pl.pallas_call(functools.partial(body), grid=(n_steps,), in_specs=in_specs + ride.in_specs,
                          out_specs=out_specs + ride.out_specs, out_shape=out_shape + ride.out_shape,
                          scratch_shapes=list(scratch) + ride.scratch, name=name,
                          compiler_params=_params(1))(*arrays, *ride.arrays)


DW_ACC_BYTES = 12 * 1024 * 1024
DW_LHS_BYTES = 6 * 1024 * 1024


def _dw_tiles(n, ka, nn, a_itemsize):
    tn = max(t for t in range(128, min(nn, 1536) + 1, 128) if nn % t == 0)
    tka = max(t for t in range(128, ka + 1, 128) if ka % t == 0 and t * tn * 4 <= DW_ACC_BYTES)
    tm = min(n, 2048)
    while tm > 256 and tm * tka * a_itemsize > DW_LHS_BYTES:
        tm //= 2
    return tm, tka, tn


def _dwcall(name, a, dc, out_dtype, gather=(), scatter=()):
    n, ka = a.shape
    nn = dc.shape[1]
    tm, tka, tn = _dw_tiles(n, ka, nn, a.dtype.itemsize)
    n_i, n_j, n_k = n // tm, nn // tn, ka // tka
    ride = _Ride(gather, scatter)

    def body(a_ref, dc_ref, o_ref, acc_ref):
        i = pl.program_id(2)

        @pl.when(i == 0)
        def _():
            acc_ref[...] = jnp.zeros_like(acc_ref)

        acc_ref[...] += _dot_tn(a_ref[...].astype(BF16), dc_ref[...].astype(BF16))

        @pl.when(i == n_i - 1)
        def _():
            o_ref[...] = acc_ref[...].astype(o_ref.dtype)

    def at_step(k, j, i):
        return lambda: (pl.program_id(0) == k) & (pl.program_id(1) == j) & (pl.program_id(2) == i)

    body = ride.wrap(body, 2, 1, 1, at_step(0, 0, 0), at_step(n_k - 1, n_j - 1, n_i - 1))
    res = pl.pallas_call(
        body, grid=(n_k, n_j, n_i),
        in_specs=[pl.BlockSpec((tm, tka), lambda k, j, i: (i, k)), pl.BlockSpec((tm, tn), lambda k, j, i: (i, j))]
        + ride.in_specs,
        out_specs=[pl.BlockSpec((tka, tn), lambda k, j, i: (k, j))] + ride.out_specs,
        out_shape=[jax.ShapeDtypeStruct((ka, nn), out_dtype)] + ride.out_shape,
        scratch_shapes=[pltpu.VMEM((tka, tn), F32)] + ride.scratch, name=name, compiler_params=_params(3))(
            a, dc, *ride.arrays)
    return res if ride.arrays else res[0]


def _first_gather(first, later):
    n = len(later)
    vm = pl.BlockSpec(memory_space=pltpu.VMEM)

    def body(*refs):
        first_ref, later_refs = refs[0], refs[1:1 + n]
        out_ref, cast_refs = refs[1 + n], refs[2 + n:2 + 2 * n]
        stage = refs[2 + 2 * n]
        sems = refs[3 + 2 * n:]
        stage[...] = first_ref[...].astype(BF16)
        _gather_two_level([stage], [out_ref], sems, 0, 'start')
        for src, dst in zip(later_refs, cast_refs):
            dst[...] = src[...].astype(BF16)
        _gather_two_level([stage], [out_ref], sems, 0, 'wait')

    return pl.pallas_call(
        body, out_shape=[jax.ShapeDtypeStruct((N_DEV,) + first.shape, BF16)]
        + [jax.ShapeDtypeStruct(s.shape, BF16) for s in later],
        in_specs=[vm] * (1 + n), out_specs=[pl.BlockSpec(memory_space=pl.ANY)] + [vm] * n,
        scratch_shapes=[pltpu.VMEM(first.shape, BF16), pltpu.SemaphoreType.DMA((7,)), pltpu.SemaphoreType.DMA((7,)),
                        pltpu.SemaphoreType.DMA((1,))],
        name="gather_first", compiler_params=pltpu.CompilerParams(vmem_limit_bytes=V7X_VMEM_LIMIT))(first, *later)


def _sum_adamw(name, recv, w, m, v):
    r, c = w.shape
    steps = max(s for s in (4, 2, 1) if r % s == 0 and (r // s) % 16 == 0 or s == 1)

    def body(recv_ref, w_ref, m_ref, v_ref, g_ref, d_ref, nm_ref, nv_ref):
        g = recv_ref[0].astype(F32)
        for k in range(1, N_DEV):
            g = g + recv_ref[k].astype(F32)
        d, nm, nv = _adamw(w_ref[...], g, m_ref[...], v_ref[...])
        g_ref[...] = g
        d_ref[...] = d
        nm_ref[...] = nm
        nv_ref[...] = nv

    blk = pl.BlockSpec((r // steps, c), lambda i: (i, 0))
    return pl.pallas_call(
        body, grid=(steps,), out_shape=[jax.ShapeDtypeStruct((r, c), F32)] * 4,
        in_specs=[pl.BlockSpec((N_DEV, r // steps, c), lambda i: (0, i, 0)), blk, blk, blk], out_specs=[blk] * 4,
        name=name, compiler_params=_params(1))(recv, w, m, v)


def _gather_last(a):
    def body(a_ref, out_ref, send_sems, recv_sems, own_sems):
        sems = (send_sems, recv_sems, own_sems)
        _exchange('gather', [a_ref], [out_ref], sems, 0, 'start')
        _exchange('gather', [a_ref], [out_ref], sems, 0, 'wait')

    vm = pl.BlockSpec(memory_space=pltpu.VMEM)
    return pl.pallas_call(
        body, out_shape=jax.ShapeDtypeStruct((N_DEV,) + a.shape, a.dtype), in_specs=[vm], out_specs=vm,
        scratch_shapes=[pltpu.SemaphoreType.DMA((7,)), pltpu.SemaphoreType.DMA((7,)), pltpu.SemaphoreType.DMA((1,))],
        name="gather_last", compiler_params=pltpu.CompilerParams(vmem_limit_bytes=V7X_VMEM_LIMIT))(a)


def _small_adamw(entries, gathered, loss_at):
    n, ng = len(entries), len(gathered)

    def body(*refs):
        g_refs, wmv = refs[:ng], refs[ng:ng + 3 * n]
        outs, loss_out = refs[ng + 3 * n:ng + 7 * n], refs[ng + 7 * n]

        def total(ref, r):
            if r is None:
                t = ref[0]
                for k in range(1, N_DEV):
                    t = t + ref[k]
                return t[None]
            t = ref[0, r:r + 1, :]
            for k in range(1, N_DEV):
                t = t + ref[k, r:r + 1, :]
            return t

        for j, (_, _, _, gi, r) in enumerate(entries):
            g = total(g_refs[gi], r)
            d, nm, nv = _adamw(wmv[3 * j][...], g, wmv[3 * j + 1][...], wmv[3 * j + 2][...])
            for ref, val in zip(outs[4 * j:4 * j + 4], (g, d, nm, nv)):
                ref[...] = val
        loss_out[...] = total(g_refs[loss_at[0]], loss_at[1])[:, :128]

    vm = pl.BlockSpec(memory_space=pltpu.VMEM)
    out_shape, arrays = [], list(gathered)
    for w, m, v, _, _ in entries:
        out_shape += [jax.ShapeDtypeStruct(w.shape, F32)] * 4
        arrays += [w, m, v]
    out_shape.append(jax.ShapeDtypeStruct((1, 128), F32))
    res = pl.pallas_call(
        body, out_shape=out_shape, in_specs=[vm] * len(arrays), out_specs=[vm] * len(out_shape),
        name="small_adamw", compiler_params=pltpu.CompilerParams(vmem_limit_bytes=V7X_VMEM_LIMIT))(*arrays)
    return [res[4 * j:4 * j + 4] for j in range(n)], res[4 * n]


def _blockdiag8(t):
    g, per = t.shape[0], 8
    eye = jnp.eye(per, dtype=F32)
    t = t.reshape(g // per, per, t.shape[1], t.shape[2])
    return (t[:, :, :, None, :] * eye[None, :, None, :, None]).reshape(g // per, per * t.shape[2], per * t.shape[3])


def _s5_out_tables(c_re, c_im):
    return _blockdiag8(c_re.transpose(0, 2, 1)), _blockdiag8(c_im.transpose(0, 2, 1))


def _s5_in_tables(lam_re, lam_im, log_step, b_re, b_im):
    step = jnp.exp(log_step)[:, None]
    mag = jnp.exp(lam_re * step)
    ab_re = mag * jnp.cos(lam_im * step)
    ab_im = mag * jnp.sin(lam_im * step)
    den = lam_re * lam_re + lam_im * lam_im
    nr = ab_re - 1.0
    co_re = (nr * lam_re + ab_im * lam_im) / den
    co_im = (ab_im * lam_re - nr * lam_im) / den
    bb_re = co_re[..., None] * b_re - co_im[..., None] * b_im
    bb_im = co_re[..., None] * b_im + co_im[..., None] * b_re
    return (ab_re.reshape(1, -1), ab_im.reshape(1, -1), _blockdiag8(bb_re.transpose(0, 2, 1)),
            _blockdiag8(bb_im.transpose(0, 2, 1)))


def _scan_passes(x_s, a_ref, cr_ref, ci_ref, reverse, re, im):
    tc, nc = x_s.shape[:2]
    ln = SCAN_LANES
    n_sq = int(math.log2(tc))
    assert 2 ** n_sq == tc
    ar = jnp.broadcast_to(a_ref[:, re], (nc, ln))
    ai = jnp.broadcast_to(a_ref[:, im], (nc, ln))
    zero = jnp.zeros((nc, ln), F32)

    def at(t):
        return tc - 1 - t if reverse else t

    def local(t, carry):
        sr, si = carry
        j = at(t)
        nr = ar * sr - ai * si + x_s[j, :, re]
        ni = ar * si + ai * sr + x_s[j, :, im]
        x_s[j, :, re] = nr
        x_s[j, :, im] = ni
        return nr, ni

    lr, li = lax.fori_loop(0, tc, local, (zero, zero))
    pr, pi = a_ref[:, re], a_ref[:, im]
    for _ in range(n_sq):
        pr, pi = pr * pr - pi * pi, 2.0 * (pr * pi)
    cr_ref[...] = lr
    ci_ref[...] = li
    tr = jnp.zeros((1, ln), F32)
    ti = jnp.zeros((1, ln), F32)
    for c in (range(nc - 1, -1, -1) if reverse else range(nc)):
        l_r = cr_ref[c:c + 1, :]
        l_i = ci_ref[c:c + 1, :]
        cr_ref[c:c + 1, :] = tr
        ci_ref[c:c + 1, :] = ti
        tr, ti = pr * tr - pi * ti + l_r, pr * ti + pi * tr + l_i

    def second_pass(on_fixed):
        def fixup(t, carry):
            qr, qi, acc = carry
            j = at(t)
            qr, qi = ar * qr - ai * qi, ar * qi + ai * qr
            gr = x_s[j, :, re] + qr
            gi = x_s[j, :, im] + qi
            x_s[j, :, re] = gr
            x_s[j, :, im] = gi
            return qr, qi, on_fixed(j, gr, gi, acc)

        return fixup

    return second_pass, (cr_ref[...], ci_ref[...]), zero


def _s5_cat_tables(a_re, a_im, bb_re, bb_im, cc_re, cc_im):
    lb = S5_BLOCK_LANES
    n_blk, kq, nq = bb_re.shape
    sub = nq // lb

    def lanes(re, im):
        lead = re.shape[:-1]
        both = jnp.stack([re.reshape(lead + (-1, lb)), im.reshape(lead + (-1, lb))], axis=-2)
        return both.reshape(lead + (-1,))

    cc = jnp.stack([cc_re.reshape(n_blk, sub, lb, kq), -cc_im.reshape(n_blk, sub, lb, kq)], axis=2)
    return lanes(a_re, a_im), lanes(bb_re, bb_im), cc.reshape(n_blk, sub * 2 * lb, kq)


def _s5_uncat(t, axis):
    lb = S5_BLOCK_LANES
    shp = t.shape
    t = t.reshape(shp[:axis] + (-1, 2, lb) + shp[axis + 1:])
    re, im = jnp.take(t, 0, axis=axis + 1), jnp.take(t, 1, axis=axis + 1)
    return re.reshape(shp[:axis] + (-1,) + shp[axis + 1:]), im.reshape(shp[:axis] + (-1,) + shp[axis + 1:])


def _s5_scan_fwd(u5p, a_cat, bb_cat, cc_cat, nb):
    n, dh = u5p.shape
    n_blk, kq, cols = bb_cat.shape
    lb, ln, nc = S5_BLOCK_LANES, SCAN_LANES, SCAN_CHUNKS
    sub = cols // (2 * lb)
    rows = n // nb
    tc = rows // nc
    n_l = n_blk * sub

    def body(u_ref, a_ref, b_ref, c_ref, s_ref, y_ref, x_s, car_r, car_i):
        x_s[...] = _dot(u_ref[...].astype(BF16), b_ref[0]).reshape(tc, nc, 2 * lb)
        for h in range(lb // ln):
            re, im = slice(h * ln, (h + 1) * ln), slice(lb + h * ln, lb + (h + 1) * ln)

            def keep(j, gr, gi, acc, re=re, im=im):
                s_ref[0, j, :, re] = gr.astype(BF16)
                s_ref[0, j, :, im] = gi.astype(BF16)
                return acc

            second_pass, start, zero = _scan_passes(x_s, a_ref, car_r, car_i, False, re, im)
            lax.fori_loop(0, tc, second_pass(keep), (*start, zero))
        y = _dot(x_s[...].reshape(rows, 2 * lb).astype(BF16), c_ref[0])

        @pl.when(pl.program_id(1) % sub == 0)
        def _():
            y_ref[...] = y

        @pl.when(pl.program_id(1) % sub != 0)
        def _():
            y_ref[...] += y

    row_blk = pl.BlockSpec((rows, kq), lambda b, l: (b, l // sub))
    return pl.pallas_call(
        body, grid=(nb, n_l),
        in_specs=[row_blk, pl.BlockSpec((1, 2 * lb), lambda b, l: (0, l)),
                  pl.BlockSpec((1, kq, 2 * lb), lambda b, l: (l // sub, 0, l % sub)),
                  pl.BlockSpec((1, 2 * lb, kq), lambda b, l: (l // sub, l % sub, 0))],
        out_specs=[pl.BlockSpec((1, tc, nc, 2 * lb), lambda b, l: (b, 0, 0, l)), row_blk],
        out_shape=[jax.ShapeDtypeStruct((nb, tc, nc, n_l * 2 * lb), BF16), jax.ShapeDtypeStruct((n, dh), F32)],
        scratch_shapes=[pltpu.VMEM((tc, nc, 2 * lb), F32)] + [pltpu.VMEM((nc, ln), F32)] * 2,
        name="s5_scan_fwd", compiler_params=_params(2))(u5p, a_cat, bb_cat, cc_cat)


def _s5_scan_bwd(dy_pre, u5p, s_cat, a_cat_conj, bb_cat, cc_cat, d_row):
    n, dh = u5p.shape
    n_blk, kq, cols = bb_cat.shape
    nb, tc, nc, _ = s_cat.shape
    lb, ln = S5_BLOCK_LANES, SCAN_LANES
    sub = cols // (2 * lb)
    rows = n // nb
    n_l = n_blk * sub

    def body(dy_ref, u_ref, s_ref, a_ref, b_ref, c_ref, d_ref, du_ref, db_ref, dc_ref, da_ref, x_s, car_r, car_i):
        dyb = dy_ref[...].astype(BF16)
        x_s[...] = _dot_nt(dyb, c_ref[0]).reshape(tc, nc, 2 * lb)
        for h in range(lb // ln):
            re, im = slice(h * ln, (h + 1) * ln), slice(lb + h * ln, lb + (h + 1) * ln)

            def with_state_before(j, gr, gi, acc, re=re, im=im):
                dr, di = acc
                pr_, pi_ = s_ref[0, j - 1, :, re].astype(F32), s_ref[0, j - 1, :, im].astype(F32)
                return dr + (pr_ * gr + pi_ * gi), di + (pr_ * gi - pi_ * gr)

            second_pass, start, zero = _scan_passes(x_s, a_ref, car_r, car_i, True, re, im)
            carry = lax.fori_loop(0, tc - 1, second_pass(with_state_before), (*start, (zero, zero)))
            row = lax.broadcasted_iota(jnp.int32, (nc, ln), 0)
            pr_ = jnp.where(row == 0, 0.0, pltpu.roll(s_ref[0, tc - 1, :, re].astype(F32), 1, 0))
            pi_ = jnp.where(row == 0, 0.0, pltpu.roll(s_ref[0, tc - 1, :, im].astype(F32), 1, 0))

            def at_first_time(j, gr, gi, acc, pr_=pr_, pi_=pi_):
                return acc[0] + (pr_ * gr + pi_ * gi), acc[1] + (pr_ * gi - pi_ * gr)

            _, _, (dr, di) = second_pass(at_first_time)(tc - 1, carry)
            da_ref[0, :, re] = jnp.sum(dr, axis=0, keepdims=True)
            da_ref[0, :, im] = jnp.sum(di, axis=0, keepdims=True)

        gb = x_s[...].reshape(rows, 2 * lb).astype(BF16)
        du = _dot_nt(gb, b_ref[0])

        @pl.when(pl.program_id(1) % sub == 0)
        def _():
            du_ref[...] = du + d_ref[...] * dy_ref[...]

        @pl.when(pl.program_id(1) % sub != 0)
        def _():
            du_ref[...] += du

        db_ref[0, 0] = _dot_tn(u_ref[...].astype(BF16), gb)
        dc_ref[0, 0] = _dot_tn(s_ref[0].reshape(rows, 2 * lb), dyb)

    row_blk = pl.BlockSpec((rows, kq), lambda b, l: (b, l // sub))
    in_map = pl.BlockSpec((1, kq, 2 * lb), lambda b, l: (l // sub, 0, l % sub))
    out_map = pl.BlockSpec((1, 2 * lb, kq), lambda b, l: (l // sub, l % sub, 0))
    return pl.pallas_call(
        body, grid=(nb, n_l),
        in_specs=[row_blk, row_blk, pl.BlockSpec((1, tc, nc, 2 * lb), lambda b, l: (b, 0, 0, l)),
                  pl.BlockSpec((1, 2 * lb), lambda b, l: (0, l)), in_map, out_map,
                  pl.BlockSpec((1, kq), lambda b, l: (0, l // sub))],
        out_specs=[row_blk, pl.BlockSpec((1, 1, kq, 2 * lb), lambda b, l: (b, l // sub, 0, l % sub)),
                   pl.BlockSpec((1, 1, 2 * lb, kq), lambda b, l: (b, l // sub, l % sub, 0)),
                   pl.BlockSpec((1, 1, 2 * lb), lambda b, l: (b, 0, l))],
        out_shape=[jax.ShapeDtypeStruct((n, dh), F32), jax.ShapeDtypeStruct((nb, n_blk, kq, cols), F32),
                   jax.ShapeDtypeStruct((nb, n_blk, cols, kq), F32), jax.ShapeDtypeStruct((nb, 1, n_l * 2 * lb), F32)],
        scratch_shapes=[pltpu.VMEM((tc, nc, 2 * lb), F32)] + [pltpu.VMEM((nc, ln), F32)] * 2,
        name="s5_scan_bwd", compiler_params=_params(2))(dy_pre, u5p, s_cat, a_cat_conj, bb_cat, cc_cat, d_row)


def kernel(x, mem, g_mix_pre, w_in, gm_ln_g, gm_ln_b, gm_w_s, gm_b_s, s5_lam_re, s5_lam_im, s5_log_step, s5_b_re, s5_b_im, s5_c_re, s5_c_im, s5_d, s5_w_glu, w_br_gm, w_br_s5, w_mix_out, g_mix_post, g_ca_pre, g_mem, ca_w_q, ca_w_kv, ca_w_o, g_ca_post, g_ffn_pre, ffn_w_gu, ffn_w_down, g_ffn_post, loss_target, m_g_mix_pre, m_w_in, m_gm_ln_g, m_gm_ln_b, m_gm_w_s, m_gm_b_s, m_s5_lam_re, m_s5_lam_im, m_s5_log_step, m_s5_b_re, m_s5_b_im, m_s5_c_re, m_s5_c_im, m_s5_d, m_s5_w_glu, m_w_br_gm, m_w_br_s5, m_w_mix_out, m_g_mix_post, m_g_ca_pre, m_g_mem, m_ca_w_q, m_ca_w_kv, m_ca_w_o, m_g_ca_post, m_g_ffn_pre, m_ffn_w_gu, m_ffn_w_down, m_g_ffn_post, v_g_mix_pre, v_w_in, v_gm_ln_g, v_gm_ln_b, v_gm_w_s, v_gm_b_s, v_s5_lam_re, v_s5_lam_im, v_s5_log_step, v_s5_b_re, v_s5_b_im, v_s5_c_re, v_s5_c_im, v_s5_d, v_s5_w_glu, v_w_br_gm, v_w_br_s5, v_w_mix_out, v_g_mix_post, v_g_ca_pre, v_g_mem, v_ca_w_q, v_ca_w_kv, v_ca_w_o, v_g_ca_post, v_g_ffn_pre, v_ffn_w_gu, v_ffn_w_down, v_g_ffn_post):
    args = locals()
    W = {n: args[n][0] for n in WEIGHTS}
    M = {n: args['m_' + n][0] for n in WEIGHTS}
    V = {n: args['v_' + n][0] for n in WEIGHTS}
    for k, kind in SHARDED.items():
        if kind == 'colT':
            W[k], M[k], V[k] = W[k].T, M[k].T, V[k].T

    nb, seq, d = x.shape
    n = nb * seq
    tm = min(ROW_TILE, n)
    tmw = min(WIDE_ROW_TILE, n)
    nmem = mem.shape[1]
    d2, dh = 2 * d, d // 2
    hd = d // CA_HEADS
    x2d = x.reshape(n, d)
    tgt = loss_target.reshape(n, d)
    mem2d = mem.reshape(nb * nmem, d)

    def row(v):
        return v.reshape(1, -1)

    names = list(SHARDED)
    later = [k for k in names if k != 'w_in']
    first_g, *casts = _first_gather(W['w_in'], [W[k] for k in later])
    shard_b = dict(zip(later, casts))
    half = shard_b['ffn_w_gu'].shape[0] // 2
    gu_halves = [shard_b['ffn_w_gu'][:half], shard_b['ffn_w_gu'][half:]]

    def assemble(k, gth):
        r, c = gth.shape[1:]
        return gth.transpose(1, 0, 2).reshape(r, N_DEV * c) if SHARDED[k] == 'col' else gth.reshape(N_DEV * r, c)

    w_in_t = assemble('w_in', first_g)
    ffh = N_DEV * W['ffn_w_down'].shape[0]

    (a_re, a_im, bb_re, bb_im), s5_in_vjp = jax.vjp(
        _s5_in_tables, W['s5_lam_re'], W['s5_lam_im'], W['s5_log_step'], W['s5_b_re'], W['s5_b_im'])
    (cc_re, cc_im), s5_out_vjp = jax.vjp(_s5_out_tables, W['s5_c_re'], W['s5_c_im'])
    bb_re_b, bb_im_b, cc_re_b, cc_im_b = (t.astype(BF16) for t in (bb_re, bb_im, cc_re, cc_im))
    s5_d_row = row(W['s5_d'])
    tril = jnp.tril(jnp.ones((GM_CHUNK, GM_CHUNK), bool))
    w_s = jnp.where(tril[None], W['gm_w_s'], 0.0).astype(BF16)
    w_s_t = w_s.transpose(0, 2, 1)
    gm_bias = jnp.repeat(W['gm_b_s'].T, d // GM_GROUPS, axis=1)

    def in_proj_body(x_ref, g_ref, w_ref, zgm_ref, u5_ref, zga_ref, zgb_ref, h_ref):
        hb = _rms(x_ref[...], g_ref[...]).astype(BF16)
        h_ref[...] = hb
        for lo in range(0, w_ref.shape[0], 512):
            acc = _dot_nt(hb, w_ref[lo:lo + 512, :])
            if lo < d2:
                zgm_ref[:, lo:lo + 512] = acc.astype(BF16)
            elif lo < d2 + dh:
                u5_ref[...] = acc
            elif lo < d2 + dh + d:
                zga_ref[:, lo - d2 - dh:lo - d2 - dh + 512] = acc.astype(BF16)
            else:
                zgb_ref[:, lo - d2 - dh - d:lo - d2 - dh - d + 512] = acc.astype(BF16)

    ride = ['s5_w_glu', 'w_br_gm', 'w_br_s5', 'w_mix_out', 'ca_w_q']
    z_gm, u5, z_ga, z_gb, h0, *got = _rowcall(
        "in_proj", in_proj_body, n, tm, [(x2d, 'row'), (row(W['g_mix_pre']), 'full'), (w_in_t, 'full')],
        [(d2, BF16), (dh, F32), (d, BF16), (d, BF16), (d, BF16)], gather=[shard_b[k] for k in ride], two_level=True)
    w_glu_f, w_brgm_f, w_brs5_f, w_mix_f, w_q_f = (assemble(k, g) for k, g in zip(ride, got))

    def gm_recompute(z_ref, lng_ref, lnb_ref, grad_ref=None):
        if grad_ref is None:
            zg = _gelu(z_ref[...].astype(F32))
        else:
            zg, grad_ref[...] = _gelu_and_grad(z_ref[...].astype(F32))
        u, v = zg[:, :d], zg[:, d:]
        vc = v - jnp.mean(v, axis=-1, keepdims=True)
        rstd = lax.rsqrt(jnp.mean(vc * vc, axis=-1, keepdims=True) + EPS)
        vhat = vc * rstd
        vn = vhat * lng_ref[...] + lnb_ref[...]
        return u, vhat, rstd, vn.astype(BF16)

    gw = d // GM_GROUPS

    def gm_fwd_body(z_ref, lng_ref, lnb_ref, ws_ref, bias_ref, y_ref):
        u, _, _, vnb = gm_recompute(z_ref, lng_ref, lnb_ref)
        for g in range(GM_GROUPS):
            sl = slice(g * gw, (g + 1) * gw)
            sv = _dot(ws_ref[g], vnb[:, sl]) + bias_ref[:, sl]
            y_ref[:, sl] = (u[:, sl] * sv).astype(BF16)

    y_gm, g_kv, g_o = _rowcall("gmlp_fwd", gm_fwd_body, n, GM_CHUNK,
                               [(z_gm, 'row'), (row(W['gm_ln_g']), 'full'), (row(W['gm_ln_b']), 'full'), (w_s, 'full'),
                                (gm_bias, 'full')], [(d, BF16)], gather=[shard_b['ca_w_kv'], shard_b['ca_w_o']],
                               two_level=True)
    w_kv_f, w_o_f = assemble('ca_w_kv', g_kv), assemble('ca_w_o', g_o)

    tc = seq // SCAN_CHUNKS
    lt = S5_GROUPS * S5_STATE

    def to_scan_order(t):
        return t.reshape(nb, SCAN_CHUNKS, tc, t.shape[-1]).transpose(0, 2, 1, 3).reshape(n, t.shape[-1])

    def from_scan_order(t):
        return t.reshape(nb, tc, SCAN_CHUNKS, t.shape[-1]).transpose(0, 2, 1, 3).reshape(n, t.shape[-1])

    u5p = to_scan_order(u5)
    a_cat, bb_cat, cc_cat = _s5_cat_tables(a_re, a_im, bb_re_b, bb_im_b, cc_re_b, cc_im_b)
    a_cat_conj = _s5_cat_tables(a_re, -a_im, bb_re_b, bb_im_b, cc_re_b, cc_im_b)[0]
    s_cat, y_lin = _s5_scan_fwd(u5p, a_cat, bb_cat, cc_cat, nb)

    def s5_out_body(yl_ref, u_ref, d_ref, wg_ref, ypre_ref, gate_ref, y_ref):
        ypre = yl_ref[...] + d_ref[...] * u_ref[...]
        ypre_ref[...] = ypre
        yg = _gelu(ypre)
        gate = _dot(yg.astype(BF16), wg_ref[...])
        gate_ref[...] = gate
        y_ref[...] = (yg * _sigmoid(gate)).astype(BF16)

    tm5 = min(1024, n)
    y_pre, gate, y_s5p = _rowcall(
        "s5_out", s5_out_body, n, tm5, [(y_lin, 'row'), (u5p, 'row'), (s5_d_row, 'full'), (w_glu_f, 'full')],
        [(dh, F32), (dh, F32), (dh, BF16)])
    y_s5 = from_scan_order(y_s5p)

    def merge_body(ygm_ref, ys5_ref, zga_ref, zgb_ref, wa_ref, wb_ref, pa_ref, pb_ref, mg_ref):
        pa = _dot(ygm_ref[...], wa_ref[...])
        pb = _dot(ys5_ref[...], wb_ref[...])
        pa_ref[...] = pa.astype(BF16)
        pb_ref[...] = pb.astype(BF16)
        mg_ref[...] = (_sigmoid(zga_ref[...].astype(F32)) * pa + _sigmoid(zgb_ref[...].astype(F32)) * pb).astype(BF16)

    p_a, p_b, merged, g_down = _rowcall(
        "merge", merge_body, n, tmw,
        [(y_gm, 'row'), (y_s5, 'row'), (z_ga, 'row'), (z_gb, 'row'), (w_brgm_f, 'full'), (w_brs5_f, 'full')],
        [(d, BF16), (d, BF16), (d, BF16)], gather=[shard_b['ffn_w_down']], two_level=True)
    w_down_f = assemble('ffn_w_down', g_down)

    def close_sublayer(name, a_in, w_out, x_res, g_post, g_next, w_next=None, gather=()):
        def body(*refs):
            a_ref, w_ref, x_ref, gp_ref, gn_ref = refs[:5]
            rest = refs[5:]
            if w_next is not None:
                wn_ref, rest = rest[0], rest[1:]
            o_ref, xo_ref, h_ref = rest[:3]
            o = _dot(a_ref[...], w_ref[...])
            o_ref[...] = o.astype(BF16)
            xo = x_ref[...] + _rms(o, gp_ref[...])
            xo_ref[...] = xo
            hb = _rms(xo, gn_ref[...]).astype(BF16)
            h_ref[...] = hb
            if w_next is not None:
                rest[3][...] = _dot(hb, wn_ref[...]).astype(BF16)

        ins = [(a_in, 'row'), (w_out, 'full'), (x_res, 'row'), (row(g_post), 'full'), (row(g_next), 'full')]
        outs = [(d, BF16), (d, F32), (d, BF16)]
        if w_next is not None:
            ins.append((w_next, 'full'))
            outs.append((w_next.shape[1], BF16))
        return _rowcall(name, body, n, tmw, ins, outs, gather=gather, two_level=True)

    o1, x1, hc, q, g_gu0 = close_sublayer("mix_out", merged, w_mix_f, x2d, W['g_mix_post'], W['g_ca_pre'], w_q_f,
                                          gather=[gu_halves[0]])

    tmm = min(ROW_TILE, nb * nmem)

    def memkv_body(m_ref, g_ref, w_ref, mn_ref, k_ref, v_ref):
        mnb = _rms(m_ref[...], g_ref[...]).astype(BF16)
        mn_ref[...] = mnb
        k_ref[...] = _dot(mnb, w_ref[:, :d]).astype(BF16)
        v_ref[...] = _dot(mnb, w_ref[:, d:]).astype(BF16)

    mem_n, k_mem, v_mem = _rowcall("mem_kv", memkv_body, nb * nmem, tmm,
                                   [(mem2d, 'row'), (row(W['g_mem']), 'full'), (w_kv_f, 'full')],
                                   [(d, BF16), (d, BF16), (d, BF16)])

    tiles_per_ex = seq // tm
    kv_spec = pl.BlockSpec((nmem, d), lambda i: (i // tiles_per_ex, 0))
    scale = hd ** -0.5

    def softmax_rows(qh, kh):
        s = _dot_nt(qh, kh) * scale
        e = jnp.exp(s - jnp.max(s, axis=-1, keepdims=True))
        return e / jnp.sum(e, axis=-1, keepdims=True)

    def attn_body(q_ref, k_ref, v_ref, o_ref):
        for h in range(CA_HEADS):
            sl = slice(h * hd, (h + 1) * hd)
            p = softmax_rows(q_ref[:, sl], k_ref[:, sl])
            o_ref[:, sl] = _dot(p.astype(BF16), v_ref[:, sl]).astype(BF16)

    (att,) = _rowcall("attn_fwd", attn_body, n, tm, [(q, 'row'), (k_mem, kv_spec), (v_mem, kv_spec)], [(d, BF16)])

    o2, x2, hf, g_gu1 = close_sublayer("attn_out", att, w_o_f, x1, W['g_ca_post'], W['g_ffn_pre'],
                                       gather=[gu_halves[1]])
    w_gu_t = jnp.stack([g_gu0, g_gu1], axis=1).reshape(2 * ffh, d)

    ck = 256

    def ffn_up_body(h_ref, w_ref, gu_ref, a_ref):
        hb = h_ref[...]
        for lo in range(0, ffh, ck):
            gt = _dot_nt(hb, w_ref[lo:lo + ck, :])
            ut = _dot_nt(hb, w_ref[ffh + lo:ffh + lo + ck, :])
            gu_ref[:, lo:lo + ck] = gt.astype(BF16)
            gu_ref[:, ffh + lo:ffh + lo + ck] = ut.astype(BF16)
            a_ref[:, lo:lo + ck] = ((gt * _sigmoid(gt)) * ut).astype(BF16)

    gu, act = _rowcall("ffn_up", ffn_up_body, n, tm, [(hf, 'row'), (w_gu_t, 'full')], [(2 * ffh, BF16), (ffh, BF16)])

    def ffn_down_body(a_ref, w_ref, x_ref, t_ref, g_ref, dx_ref, do_ref, loss_ref, dg_ref):
        i = pl.program_id(0)

        @pl.when(i == 0)
        def _():
            loss_ref[...] = jnp.zeros_like(loss_ref)
            dg_ref[...] = jnp.zeros_like(dg_ref)

        o = _dot(a_ref[...], w_ref[...])
        diff = x_ref[...] + _rms(o, g_ref[...]) - t_ref[...]
        loss_ref[...] += jnp.full(loss_ref.shape, 0.5 / d, F32) * jnp.sum(diff * diff)
        dx = diff * (1.0 / d)
        dx_ref[...] = dx
        do, dg = _rms_bwd(o, g_ref[...], dx)
        do_ref[...] = do.astype(BF16)
        dg_ref[...] += dg

    dx3, do3, loss_part, dg_ffn_post = _rowcall(
        "ffn_down_loss", ffn_down_body, n, tmw,
        [(act, 'row'), (w_down_f, 'full'), (x2, 'row'), (tgt, 'row'), (row(W['g_ffn_post']), 'full')],
        [(d, F32), (d, BF16)], accs=[((1, 128), F32), ((1, d), F32)])

    G = {'g_ffn_post': dg_ffn_post}
    RECV = {}

    def parts_of(k, gfull):
        r, c = W[k].shape
        return gfull.reshape(r, N_DEV, c).transpose(1, 0, 2) if SHARDED[k] == 'col' else gfull.reshape(N_DEV, r, c)

    p_down = parts_of('ffn_w_down', _dwcall("dw_ffn_down", act, do3, BF16))

    def ffn_act_bwd_body(do_ref, w_ref, gu_ref, dgu_ref):
        dob = do_ref[...]
        for lo in range(0, ffh, ck):
            da = _dot_nt(dob, w_ref[lo:lo + ck, :])
            gt = gu_ref[:, lo:lo + ck].astype(F32)
            ut = gu_ref[:, ffh + lo:ffh + lo + ck].astype(F32)
            sg = _sigmoid(gt)
            dgu_ref[:, lo:lo + ck] = (da * ut * (sg * (1.0 + gt * (1.0 - sg)))).astype(BF16)
            dgu_ref[:, ffh + lo:ffh + lo + ck] = (da * (gt * sg)).astype(BF16)

    dgu, RECV['ffn_w_down'] = _rowcall("ffn_act_bwd", ffn_act_bwd_body, n, tmw,
                                       [(do3, 'row'), (w_down_f, 'full'), (gu, 'row')], [(2 * ffh, BF16)],
                                       scatter=[p_down])
    p_gu = parts_of('ffn_w_gu', _dwcall("dw_ffn_gu", dgu, hf, BF16))

    def open_sublayer(name, pieces, w_full, x_in, g_pre, dx_up, o_prev=None, g_post_prev=None, scatter=(), gather=(),
                      w_t=False):
        n_p = len(pieces)
        second = o_prev is not None

        def body(*refs):
            dc_refs, (w_ref, x_ref, g_ref, dxu_ref), rest = refs[:n_p], refs[n_p:n_p + 4], refs[n_p + 4:]
            if second:
                (op_ref, gp_ref), rest = rest[:2], rest[2:]
            i = pl.program_id(0)
            dhid = None
            for dc_ref, (_, lo, hi) in zip(dc_refs, pieces):
                part = _dot(dc_ref[...], w_ref[lo:hi, :]) if w_t else _dot_nt(dc_ref[...], w_ref[:, lo:hi])
                dhid = part if dhid is None else dhid + part
            dxn, dg = _rms_bwd(x_ref[...], g_ref[...], dhid)
            dx = dxu_ref[...] + dxn
            if second:
                dx_ref, do_ref, dg_ref, dg2_ref = rest
            else:
                dx_ref, dg_ref = rest

            @pl.when(i == 0)
            def _():
                dg_ref[...] = jnp.zeros_like(dg_ref)
                if second:
                    dg2_ref[...] = jnp.zeros_like(dg2_ref)

            dx_ref[...] = dx
            dg_ref[...] += dg
            if second:
                do, dg2 = _rms_bwd(op_ref[...].astype(F32), gp_ref[...], dx)
                do_ref[...] = do.astype(BF16)
                dg2_ref[...] += dg2

        ins = [(p[0], 'row') for p in pieces] + [(w_full, 'full'), (x_in, 'row'), (row(g_pre), 'full'), (dx_up, 'row')]
        outs = [(d, F32)]
        accs = [((1, d), F32)]
        if second:
            ins += [(o_prev, 'row'), (row(g_post_prev), 'full')]
            outs.append((d, BF16))
            accs.append(((1, d), F32))
        light = w_full.size <= d * d
        res = _rowcall(name, body, n, tmw if light else tm, ins, outs, accs=accs, scatter=scatter, gather=gather)
        if second:
            dx, do, dg, dg2 = res[:4]
            return dx, dg, do, dg2, res[4:]
        return res[0], res[1], res[2:]

    dx2, G['g_ffn_pre'], do2, G['g_ca_post'], (RECV['ffn_w_gu'],) = open_sublayer(
        "ffn_in_bwd", [(dgu, 0, 2 * ffh)], w_gu_t, x2, W['g_ffn_pre'], dx3, o2, W['g_ca_post'], scatter=[p_gu],
        w_t=True)
    p_o = parts_of('ca_w_o', _dwcall("dw_ca_o", att, do2, BF16))

    def attn_bwd_body(q_ref, k_ref, v_ref, do_ref, wo_ref, dq_ref, dk_ref, dv_ref):
        i = pl.program_id(0)

        @pl.when(i % tiles_per_ex == 0)
        def _():
            dk_ref[...] = jnp.zeros_like(dk_ref)
            dv_ref[...] = jnp.zeros_like(dv_ref)

        d_att = _dot_nt(do_ref[...], wo_ref[...]).astype(BF16)
        for h in range(CA_HEADS):
            sl = slice(h * hd, (h + 1) * hd)
            qh, kh, vh, dah = q_ref[:, sl], k_ref[:, sl], v_ref[:, sl], d_att[:, sl]
            p = softmax_rows(qh, kh)
            dp = _dot_nt(dah, vh)
            ds = (p * (dp - jnp.sum(p * dp, axis=-1, keepdims=True)) * scale).astype(BF16)
            dq_ref[:, sl] = _dot(ds, kh).astype(BF16)
            dk_ref[:, sl] += _dot_tn(ds, qh)
            dv_ref[:, sl] += _dot_tn(p.astype(BF16), dah)

    kv_acc = ((nb * nmem, d), (F32, pl.BlockSpec((nmem, d), lambda i: (i // tiles_per_ex, 0))))
    dq, dk_mem, dv_mem, RECV['ca_w_o'] = _rowcall(
        "attn_bwd", attn_bwd_body, n, tm,
        [(q, 'row'), (k_mem, kv_spec), (v_mem, kv_spec), (do2, 'row'), (w_o_f, 'full')], [(d, BF16)],
        accs=[kv_acc, kv_acc], scatter=[p_o])
    p_q = parts_of('ca_w_q', _dwcall("dw_ca_q", hc, dq, BF16))

    def memkv_bwd_body(dk_ref, dv_ref, m_ref, g_ref, w_ref, dkv_ref, dg_ref):
        i = pl.program_id(0)

        @pl.when(i == 0)
        def _():
            dg_ref[...] = jnp.zeros_like(dg_ref)

        dkb, dvb = dk_ref[...].astype(BF16), dv_ref[...].astype(BF16)
        dkv_ref[:, :d] = dkb
        dkv_ref[:, d:] = dvb
        dmn = _dot_nt(dkb, w_ref[:, :d]) + _dot_nt(dvb, w_ref[:, d:])
        _, dg = _rms_bwd(m_ref[...], g_ref[...], dmn)
        dg_ref[...] += dg

    dkv, G['g_mem'] = _rowcall(
        "mem_kv_bwd", memkv_bwd_body, nb * nmem, tmm,
        [(dk_mem, 'row'), (dv_mem, 'row'), (mem2d, 'row'), (row(W['g_mem']), 'full'), (w_kv_f, 'full')],
        [(d2, BF16)], accs=[((1, d), F32)])
    p_kv = parts_of('ca_w_kv', _dwcall("dw_ca_kv", mem_n, dkv, BF16))

    dx1, G['g_ca_pre'], do1, G['g_mix_post'], (RECV['ca_w_q'], RECV['ca_w_kv']) = open_sublayer(
        "attn_in_bwd", [(dq, 0, d)], w_q_f, x1, W['g_ca_pre'], dx2, o1, W['g_mix_post'], scatter=[p_q, p_kv])
    p_mix = parts_of('w_mix_out', _dwcall("dw_mix_out", merged, do1, BF16))

    def merge_bwd_body(do_ref, wm_ref, zga_ref, zgb_ref, pa_ref, pb_ref, wb_ref, dpa_ref, dpb_ref, dza_ref, dzb_ref,
                       dys_ref):
        dm = _dot_nt(do_ref[...], wm_ref[...])
        sa, sb = _sigmoid(zga_ref[...].astype(F32)), _sigmoid(zgb_ref[...].astype(F32))
        dpb = (dm * sb).astype(BF16)
        dpa_ref[...] = (dm * sa).astype(BF16)
        dpb_ref[...] = dpb
        dza_ref[...] = (dm * pa_ref[...].astype(F32) * (sa * (1.0 - sa))).astype(BF16)
        dzb_ref[...] = (dm * pb_ref[...].astype(F32) * (sb * (1.0 - sb))).astype(BF16)
        dys_ref[...] = _dot_nt(dpb, wb_ref[...]).astype(BF16)

    NORM_ROWS = ['g_ffn_post', 'g_ffn_pre', 'g_ca_post', 'g_mem', 'g_ca_pre', 'g_mix_post']
    norm_rows = jnp.concatenate([G[k] for k in NORM_ROWS] + [jnp.tile(loss_part, (1, d // 128))], axis=0)
    dp_a, dp_b, dz_ga, dz_gb, dy_s5, all_norm_rows, RECV['w_mix_out'] = _rowcall(
        "merge_bwd", merge_bwd_body, n, tmw,
        [(do1, 'row'), (w_mix_f, 'full'), (z_ga, 'row'), (z_gb, 'row'), (p_a, 'row'), (p_b, 'row'), (w_brs5_f, 'full')],
        [(d, BF16), (d, BF16), (d, BF16), (d, BF16), (dh, BF16)], scatter=[p_mix], gather=[norm_rows])
    p_brgm = parts_of('w_br_gm', _dwcall("dw_br_gm", y_gm, dp_a, BF16))
    p_brs5 = parts_of('w_br_s5', _dwcall("dw_br_s5", y_s5, dp_b, BF16))

    def gm_bwd_body(z_ref, dpa_ref, wa_ref, lng_ref, lnb_ref, ws_ref, wst_ref, bias_ref, dz_ref, dws_ref, dbias_ref,
                    dlng_ref, dlnb_ref, du_s, dvn_s, dgelu_s):
        i = pl.program_id(0)

        @pl.when(i == 0)
        def _():
            dws_ref[...] = jnp.zeros_like(dws_ref)
            dbias_ref[...] = jnp.zeros_like(dbias_ref)
            dlng_ref[...] = jnp.zeros_like(dlng_ref)
            dlnb_ref[...] = jnp.zeros_like(dlnb_ref)

        u, vhat, rstd, vnb = gm_recompute(z_ref, lng_ref, lnb_ref, dgelu_s)
        dy = _dot_nt(dpa_ref[...], wa_ref[...])
        for g in range(GM_GROUPS):
            sl = slice(g * gw, (g + 1) * gw)
            sv = _dot(ws_ref[g], vnb[:, sl]) + bias_ref[:, sl]
            du_s[:, sl] = dy[:, sl] * sv
            dsv = dy[:, sl] * u[:, sl]
            dsvb = dsv.astype(BF16)
            dvn_s[:, sl] = _dot(wst_ref[g], dsvb)
            dws_ref[g] += _dot_nt(dsvb, vnb[:, sl])
            dbias_ref[:, sl] += dsv
        dvn = dvn_s[...]
        dlng_ref[...] += jnp.sum(dvn * vhat, axis=0, keepdims=True)
        dlnb_ref[...] += jnp.sum(dvn, axis=0, keepdims=True)
        dvh = dvn * lng_ref[...]
        dv = rstd * (dvh - jnp.mean(dvh, axis=-1, keepdims=True) - vhat * jnp.mean(dvh * vhat, axis=-1, keepdims=True))
        dz_ref[:, :d] = (du_s[...] * dgelu_s[:, :d]).astype(BF16)
        dz_ref[:, d:] = (dv * dgelu_s[:, d:]).astype(BF16)

    dz_gm, dws_full, dbias_full, G['gm_ln_g'], G['gm_ln_b'], RECV['w_br_gm'], RECV['w_br_s5'] = _rowcall(
        "gmlp_bwd", gm_bwd_body, n, GM_CHUNK,
        [(z_gm, 'row'), (dp_a, 'row'), (w_brgm_f, 'full'), (row(W['gm_ln_g']), 'full'), (row(W['gm_ln_b']), 'full'),
         (w_s, 'full'), (w_s_t, 'full'), (gm_bias, 'full')],
        [(d2, BF16)], accs=[((GM_GROUPS, GM_CHUNK, GM_CHUNK), F32), ((GM_CHUNK, d), F32), ((1, d), F32), ((1, d), F32)],
        scratch=[pltpu.VMEM((GM_CHUNK, d), F32), pltpu.VMEM((GM_CHUNK, d), F32), pltpu.VMEM((GM_CHUNK, d2), F32)],
        scatter=[p_brgm, p_brs5])
    G['gm_w_s'] = jnp.where(tril[None], dws_full, 0.0)
    G['gm_b_s'] = dbias_full.reshape(GM_CHUNK, GM_GROUPS, gw).sum(-1).T

    dy_s5p = to_scan_order(dy_s5)

    def s5_out_bwd_body(dy_ref, ypre_ref, gate_ref, u_ref, wg_ref, dyp_ref, dgate_ref, yg_ref, dd_ref):
        i = pl.program_id(0)

        @pl.when(i == 0)
        def _():
            dd_ref[...] = jnp.zeros_like(dd_ref)

        dy = dy_ref[...].astype(F32)
        yg, dgelu = _gelu_and_grad(ypre_ref[...])
        sg = _sigmoid(gate_ref[...])
        dgb = (dy * yg * (sg * (1.0 - sg))).astype(BF16)
        dgate_ref[...] = dgb
        yg_ref[...] = yg.astype(BF16)
        dyp = (dy * sg + _dot_nt(dgb, wg_ref[...])) * dgelu
        dyp_ref[...] = dyp
        dd_ref[...] += jnp.sum(dyp * u_ref[...], axis=0, keepdims=True)

    dy_pre, dgate, yg_b, dd = _rowcall(
        "s5_out_bwd", s5_out_bwd_body, n, tm5,
        [(dy_s5p, 'row'), (y_pre, 'row'), (gate, 'row'), (u5p, 'row'), (w_glu_f, 'full')],
        [(dh, F32), (dh, BF16), (dh, BF16)], accs=[((1, dh), F32)])
    du5p, d_bb_cat, d_cc_cat, da_cat = _s5_scan_bwd(dy_pre, u5p, s_cat, a_cat_conj, bb_cat, cc_cat, s5_d_row)
    du5 = from_scan_order(du5p.astype(BF16))
    d_bb_re, d_bb_im = _s5_uncat(jnp.sum(d_bb_cat, axis=0), 2)
    d_cc_re, d_cc_neg_im = _s5_uncat(jnp.sum(d_cc_cat, axis=0), 1)
    da_re, da_im = _s5_uncat(jnp.sum(da_cat, axis=0), 1)
    lane_shape = (S5_GROUPS, 8, 128)
    g_c_re, g_c_im = (t.reshape(lane_shape) for t in s5_out_vjp((d_cc_re, -d_cc_neg_im)))
    g_d = dd.reshape(S5_GROUPS, S5_CH)
    g_lam_re, g_lam_im, g_log_step, g_b_re, g_b_im = s5_in_vjp((da_re, da_im, d_bb_re, d_bb_im))
    g_log_step, g_b_re, g_b_im = g_log_step.reshape(1, -1), g_b_re.reshape(lane_shape), g_b_im.reshape(lane_shape)
    p_glu = parts_of('s5_w_glu', _dwcall("dw_s5_glu", yg_b, dgate, BF16))

    pieces = [(dz_gm, 0, d2), (du5, d2, d2 + dh), (dz_ga, d2 + dh, d2 + dh + d), (dz_gb, d2 + dh + d, d2 + dh + 2 * d)]
    gm_ln_rows = jnp.concatenate([G['gm_ln_g'], G['gm_ln_b']], axis=0)
    dw_gm, all_lam_re, all_lam_im, all_log_step, all_b_re, all_b_im = _dwcall(
        "dw_in_0", dz_gm, h0, BF16, gather=[g_lam_re, g_lam_im, g_log_step, g_b_re, g_b_im])
    dw_s5 = _dwcall("dw_in_1", du5, h0, BF16)
    dw_ga, all_gm_w_s, all_gm_b_s, all_gm_ln = _dwcall(
        "dw_in_2", dz_ga, h0, BF16, gather=[G['gm_w_s'], G['gm_b_s'], gm_ln_rows])
    dw_gb, all_c_re, all_c_im, all_d, RECV['s5_w_glu'] = _dwcall(
        "dw_in_3", dz_gb, h0, BF16, gather=[g_c_re, g_c_im, g_d], scatter=[p_glu])
    p_in = parts_of('w_in', jnp.concatenate([dw_gm, dw_s5, dw_ga, dw_gb], axis=0))
    grad_x2d, g_mix_pre_part, (RECV['w_in'],) = open_sublayer(
        "in_proj_bwd", pieces, w_in_t, x2d, W['g_mix_pre'], dx1, scatter=[p_in], w_t=True)

    out = {}
    for k in names:
        quad = _sum_adamw("sum_adamw_" + k, RECV[k], W[k], M[k], V[k])
        out[k] = [(t.T if SHARDED[k] == 'colT' else t)[None] for t in quad]
    gathered = [all_norm_rows, all_gm_w_s, all_gm_b_s, all_gm_ln, all_c_re, all_c_im, all_d, all_lam_re, all_lam_im,
                all_log_step, all_b_re, all_b_im, _gather_last(g_mix_pre_part)]
    where = {'g_ffn_post': (0, 0), 'g_ffn_pre': (0, 1), 'g_ca_post': (0, 2), 'g_mem': (0, 3), 'g_ca_pre': (0, 4),
             'g_mix_post': (0, 5), 'gm_w_s': (1, None), 'gm_b_s': (2, None), 'gm_ln_g': (3, 0), 'gm_ln_b': (3, 1),
             's5_c_re': (4, None), 's5_c_im': (5, None), 's5_d': (6, None), 's5_lam_re': (7, None),
             's5_lam_im': (8, None), 's5_log_step': (9, 0), 's5_b_re': (10, None), 's5_b_im': (11, None),
             'g_mix_pre': (12, 0)}
    folded = ('s5_b_re', 's5_b_im', 's5_c_re', 's5_c_im')

    def as_updated(k, t):
        return t.reshape((1,) + lane_shape) if k in folded else t

    small, loss_row = _small_adamw(
        [(as_updated(k, args[k]), as_updated(k, args['m_' + k]), as_updated(k, args['v_' + k])) + where[k]
         for k in SMALL], gathered, (0, len(NORM_ROWS)))
    out.update({k: [t.reshape(args[k].shape) for t in quad] for k, quad in zip(SMALL, small)})

    res = [loss_row[0, 0], grad_x2d.reshape(x.shape)]
    for j in range(4):
        res += [out[k][j] for k in WEIGHTS]
    return tuple(res)
```

```python
import functools
import math

import jax
import jax.numpy as jnp
from jax import lax
from jax.experimental import pallas as pl
from jax.experimental.pallas import tpu as pltpu

F32 = jnp.float32
BF16 = jnp.bfloat16
EPS = 1e-6
N_DEV = 8
V7X_VMEM_LIMIT = 56 * 1024 * 1024
ROW_TILE = 256
WIDE_ROW_TILE = 512
GM_CHUNK = 128
GM_GROUPS = 8
S5_GROUPS = 32
S5_STATE = 64
S5_CH = 16
SCAN_CHUNKS = 32
SCAN_LANES = 128
S5_BLOCK_LANES = 256
CA_HEADS = 4
ADAM_LR, ADAM_B1, ADAM_B2, ADAM_EPS, ADAM_WD, ADAM_STEP = 0.001, 0.9, 0.999, 1e-08, 0.01, 10

WEIGHTS = ['g_mix_pre', 'w_in', 'gm_ln_g', 'gm_ln_b', 'gm_w_s', 'gm_b_s', 's5_lam_re', 's5_lam_im', 's5_log_step',
           's5_b_re', 's5_b_im', 's5_c_re', 's5_c_im', 's5_d', 's5_w_glu', 'w_br_gm', 'w_br_s5', 'w_mix_out',
           'g_mix_post', 'g_ca_pre', 'g_mem', 'ca_w_q', 'ca_w_kv', 'ca_w_o', 'g_ca_post', 'g_ffn_pre', 'ffn_w_gu',
           'ffn_w_down', 'g_ffn_post']
SHARDED = {'w_in': 'colT', 's5_w_glu': 'row', 'w_br_gm': 'row', 'w_br_s5': 'col', 'w_mix_out': 'row',
           'ca_w_q': 'row', 'ca_w_kv': 'col', 'ca_w_o': 'row', 'ffn_w_gu': 'colT', 'ffn_w_down': 'row'}
SMALL = [n for n in WEIGHTS if n not in SHARDED]


def _rms(x, g):
    r = lax.rsqrt(jnp.mean(x * x, axis=-1, keepdims=True) + EPS)
    return (x * r) * g


def _rms_bwd(x, g, dy):
    r = lax.rsqrt(jnp.mean(x * x, axis=-1, keepdims=True) + EPS)
    n = x * r
    dn = dy * g
    dx = r * (dn - n * jnp.mean(dn * n, axis=-1, keepdims=True))
    return dx, jnp.sum(dy * n, axis=0, keepdims=True)


_GELU_C = math.sqrt(2.0 / math.pi)


def _gelu(x):
    return 0.5 * x * (1.0 + jnp.tanh(_GELU_C * (x + 0.044715 * (x * x * x))))


def _gelu_and_grad(x):
    x2 = x * x
    t = jnp.tanh(_GELU_C * (x + 0.044715 * (x2 * x)))
    h = 0.5 * (1.0 + t)
    return x * h, h + 0.5 * x * (1.0 - t * t) * (_GELU_C * (1.0 + 3.0 * 0.044715 * x2))


def _sigmoid(x):
    return 0.5 * (1.0 + jnp.tanh(0.5 * x))


def _dot(a, b):
    return jnp.dot(a, b, preferred_element_type=F32)


def _dot_nt(a, b):
    return lax.dot_general(a, b, (((1,), (1,)), ((), ())), preferred_element_type=F32)


def _dot_tn(a, b):
    return lax.dot_general(a, b, (((0,), (0,)), ((), ())), preferred_element_type=F32)


def _adamw(w, g, m, v):
    m = ADAM_B1 * m + (1.0 - ADAM_B1) * g
    v = ADAM_B2 * v + (1.0 - ADAM_B2) * (g * g)
    m_hat = m / (1.0 - ADAM_B1 ** ADAM_STEP)
    v_hat = v / (1.0 - ADAM_B2 ** ADAM_STEP)
    delta = -ADAM_LR * (m_hat / (jnp.sqrt(v_hat) + ADAM_EPS) + ADAM_WD * w)
    return delta, m, v


def _params(n_grid):
    return pltpu.CompilerParams(dimension_semantics=("arbitrary",) * n_grid, vmem_limit_bytes=V7X_VMEM_LIMIT)


def _my_place():
    x, y, c = lax.axis_index("x"), lax.axis_index("y"), lax.axis_index("c")
    return x, y, c


def _peer(x, y, c, k):
    px = 1 - x if k & 4 else x
    py = 1 - y if k & 2 else y
    pc = 1 - c if k & 1 else c
    return (px, py, pc), 4 * px + 2 * py + pc


def _exchange(kind, src_refs, dst_refs, sems, first, phase):
    x, y, c = _my_place()
    me = 4 * x + 2 * y + c
    send_sems, recv_sems, own_sems = sems
    for j, (src, dst) in enumerate(zip(src_refs, dst_refs), start=first):
        own = pltpu.make_async_copy(src if kind == 'gather' else src.at[me], dst.at[me], own_sems.at[j])
        if phase == 'start':
            own.start()
        for k in range(1, N_DEV):
            peer, peer_block = _peer(x, y, c, k)
            out = pltpu.make_async_remote_copy(
                src_ref=src if kind == 'gather' else src.at[peer_block], dst_ref=dst.at[me],
                send_sem=send_sems.at[7 * j + k - 1], recv_sem=recv_sems.at[7 * j + k - 1], device_id=peer,
                device_id_type=pl.DeviceIdType.MESH)
            if phase == 'start':
                out.start()
            else:
                pltpu.make_async_remote_copy(
                    src_ref=src if kind == 'gather' else src.at[peer_block], dst_ref=dst.at[peer_block],
                    send_sem=send_sems.at[7 * j + k - 1], recv_sem=recv_sems.at[7 * j + k - 1], device_id=peer,
                    device_id_type=pl.DeviceIdType.MESH).wait_recv()
                out.wait_send()
        if phase == 'wait':
            own.wait()


def _gather_two_level(src_refs, dst_refs, sems, first, phase):
    x, y, c = _my_place()
    me, sibling = (x, y, c), (x, y, 1 - c)
    chips = [(1 - x, y), (x, 1 - y), (1 - x, 1 - y)]
    send_sems, recv_sems, own_sems = sems
    for j, (src, dst) in enumerate(zip(src_refs, dst_refs), start=first):
        def rows(px, py, pc, dst=dst):
            return dst.at[4 * px + 2 * py + pc]

        def copy(k, block, to, from_src=False, j=j, src=src, rows=rows):
            return pltpu.make_async_remote_copy(
                src_ref=src if from_src else rows(*block), dst_ref=rows(*block), send_sem=send_sems.at[7 * j + k],
                recv_sem=recv_sems.at[7 * j + k], device_id=to, device_id_type=pl.DeviceIdType.MESH)

        mine = pltpu.make_async_copy(src, rows(*me), own_sems.at[j])
        first_out = [copy(0, me, sibling, True)] + [copy(1 + i, me, (*chip, c), True) for i, chip in enumerate(chips)]
        if phase == 'start':
            mine.start()
            for cp in first_out:
                cp.start()
        else:
            passed = [copy(4 + i, (*chip, c), sibling) for i, chip in enumerate(chips)]
            for i, chip in enumerate(chips):
                copy(1 + i, (*chip, c), me).wait_recv()
                passed[i].start()
            copy(0, sibling, me).wait_recv()
            for i, chip in enumerate(chips):
                copy(4 + i, (*chip, 1 - c), me).wait_recv()
            for cp in first_out + passed:
                cp.wait_send()
            mine.wait()


class _Ride:
    def __init__(self, gather=(), scatter=(), two_level=False):
        self.two_level = two_level
        self.n_gather = len(gather)
        self.arrays = list(gather) + list(scatter)
        n = len(self.arrays)
        hbm = pl.BlockSpec(memory_space=pl.ANY)
        self.in_specs = [hbm] * n
        self.out_specs = [hbm] * n
        self.out_shape = [jax.ShapeDtypeStruct((N_DEV,) + a.shape, a.dtype) for a in gather]
        self.out_shape += [jax.ShapeDtypeStruct(a.shape, a.dtype) for a in scatter]
        self.scratch = [pltpu.SemaphoreType.DMA((7 * n,)), pltpu.SemaphoreType.DMA((7 * n,)),
                        pltpu.SemaphoreType.DMA((n,))] if n else []

    def wrap(self, inner, n_in, n_out, n_scr, is_first, is_last):
        n_mv = len(self.arrays)
        if not n_mv:
            return inner

        def body(*refs):
            mv_src = refs[n_in:n_in + n_mv]
            mv_dst = refs[n_in + n_mv + n_out:n_in + 2 * n_mv + n_out]
            sems = refs[n_in + 2 * n_mv + n_out + n_scr:]

            def exchange(phase):
                if self.n_gather and self.two_level:
                    _gather_two_level(mv_src[:self.n_gather], mv_dst[:self.n_gather], sems, 0, phase)
                elif self.n_gather:
                    _exchange('gather', mv_src[:self.n_gather], mv_dst[:self.n_gather], sems, 0, phase)
                if n_mv > self.n_gather:
                    _exchange('scatter', mv_src[self.n_gather:], mv_dst[self.n_gather:], sems, self.n_gather, phase)

            pl.when(is_first())(functools.partial(exchange, 'start'))
            inner(*refs[:n_in], *refs[n_in + n_mv:n_in + n_mv + n_out],
                  *refs[n_in + 2 * n_mv + n_out:n_in + 2 * n_mv + n_out + n_scr])
            pl.when(is_last())(functools.partial(exchange, 'wait'))

        return body


def _rowcall(name, body, n_rows, tm, ins, outs, accs=(), scratch=(), gather=(), scatter=(), two_level=False):
    n_steps = n_rows // tm
    ride = _Ride(gather, scatter, two_level)
    body = ride.wrap(body, len(ins), len(outs) + len(accs), len(scratch), lambda: pl.program_id(0) == 0,
                     lambda: pl.program_id(0) == n_steps - 1)
    arrays, in_specs = [], []
    for a, kind in ins:
        arrays.append(a)
        if kind == 'row':
            in_specs.append(pl.BlockSpec((tm,) + a.shape[1:], lambda i, nd=a.ndim: (i,) + (0,) * (nd - 1)))
        elif kind == 'full':
            in_specs.append(pl.BlockSpec(a.shape, lambda i, nd=a.ndim: (0,) * nd))
        else:
            in_specs.append(kind)
    out_shape, out_specs = [], []
    for cols, dt in outs:
        out_shape.append(jax.ShapeDtypeStruct((n_rows, cols), dt))
        out_specs.append(pl.BlockSpec((tm, cols), lambda i: (i, 0)))
    for shp, dt in accs:
        if isinstance(dt, tuple):
            dt, spec = dt
        else:
            spec = pl.BlockSpec(shp, lambda i, nd=len(shp): (0,) * nd)
        out_shape.append(jax.ShapeDtypeStruct(shp, dt))
        out_specs.append(spec)
    return pl.pallas_call(functools.partial(body), grid=(n_steps,), in_specs=in_specs + ride.in_specs,
                          out_specs=out_specs + ride.out_specs, out_shape=out_shape + ride.out_shape,
                          scratch_shapes=list(scratch) + ride.scratch, name=name,
                          compiler_params=_params(1))(*arrays, *ride.arrays)


DW_ACC_BYTES = 12 * 1024 * 1024
DW_LHS_BYTES = 6 * 1024 * 1024


def _dw_tiles(n, ka, nn, a_itemsize):
    tn = max(t for t in range(128, min(nn, 1536) + 1, 128) if nn % t == 0)
    tka = max(t for t in range(128, ka + 1, 128) if ka % t == 0 and t * tn * 4 <= DW_ACC_BYTES)
    tm = min(n, 2048)
    while tm > 256 and tm * tka * a_itemsize > DW_LHS_BYTES:
        tm //= 2
    return tm, tka, tn


def _dwcall(name, a, dc, out_dtype, gather=(), scatter=()):
    n, ka = a.shape
    nn = dc.shape[1]
    tm, tka, tn = _dw_tiles(n, ka, nn, a.dtype.itemsize)
    n_i, n_j, n_k = n // tm, nn // tn, ka // tka
    ride = _Ride(gather, scatter)

    def body(a_ref, dc_ref, o_ref, acc_ref):
        i = pl.program_id(2)

        @pl.when(i == 0)
        def _():
            acc_ref[...] = jnp.zeros_like(acc_ref)

        acc_ref[...] += _dot_tn(a_ref[...].astype(BF16), dc_ref[...].astype(BF16))

        @pl.when(i == n_i - 1)
        def _():
            o_ref[...] = acc_ref[...].astype(o_ref.dtype)

    def at_step(k, j, i):
        return lambda: (pl.program_id(0) == k) & (pl.program_id(1) == j) & (pl.program_id(2) == i)

    body = ride.wrap(body, 2, 1, 1, at_step(0, 0, 0), at_step(n_k - 1, n_j - 1, n_i - 1))
    res = pl.pallas_call(
        body, grid=(n_k, n_j, n_i),
        in_specs=[pl.BlockSpec((tm, tka), lambda k, j, i: (i, k)), pl.BlockSpec((tm, tn), lambda k, j, i: (i, j))]
        + ride.in_specs,
        out_specs=[pl.BlockSpec((tka, tn), lambda k, j, i: (k, j))] + ride.out_specs,
        out_shape=[jax.ShapeDtypeStruct((ka, nn), out_dtype)] + ride.out_shape,
        scratch_shapes=[pltpu.VMEM((tka, tn), F32)] + ride.scratch, name=name, compiler_params=_params(3))(
            a, dc, *ride.arrays)
    return res if ride.arrays else res[0]


def _first_gather(first, later):
    n = len(later)
    vm = pl.BlockSpec(memory_space=pltpu.VMEM)

    def body(*refs):
        first_ref, later_refs = refs[0], refs[1:1 + n]
        out_ref, cast_refs = refs[1 + n], refs[2 + n:2 + 2 * n]
        stage = refs[2 + 2 * n]
        sems = refs[3 + 2 * n:]
        stage[...] = first_ref[...].astype(BF16)
        _gather_two_level([stage], [out_ref], sems, 0, 'start')
        for src, dst in zip(later_refs, cast_refs):
            dst[...] = src[...].astype(BF16)
        _gather_two_level([stage], [out_ref], sems, 0, 'wait')

    return pl.pallas_call(
        body, out_shape=[jax.ShapeDtypeStruct((N_DEV,) + first.shape, BF16)]
        + [jax.ShapeDtypeStruct(s.shape, BF16) for s in later],
        in_specs=[vm] * (1 + n), out_specs=[pl.BlockSpec(memory_space=pl.ANY)] + [vm] * n,
        scratch_shapes=[pltpu.VMEM(first.shape, BF16), pltpu.SemaphoreType.DMA((7,)), pltpu.SemaphoreType.DMA((7,)),
                        pltpu.SemaphoreType.DMA((1,))],
        name="gather_first", compiler_params=pltpu.CompilerParams(vmem_limit_bytes=V7X_VMEM_LIMIT))(first, *later)


def _sum_adamw(name, recv, w, m, v):
    r, c = w.shape
    steps = max(s for s in (4, 2, 1) if r % s == 0 and (r // s) % 16 == 0 or s == 1)

    def body(recv_ref, w_ref, m_ref, v_ref, g_ref, d_ref, nm_ref, nv_ref):
        g = recv_ref[0].astype(F32)
        for k in range(1, N_DEV):
            g = g + recv_ref[k].astype(F32)
        d, nm, nv = _adamw(w_ref[...], g, m_ref[...], v_ref[...])
        g_ref[...] = g
        d_ref[...] = d
        nm_ref[...] = nm
        nv_ref[...] = nv

    blk = pl.BlockSpec((r // steps, c), lambda i: (i, 0))
    return pl.pallas_call(
        body, grid=(steps,), out_shape=[jax.ShapeDtypeStruct((r, c), F32)] * 4,
        in_specs=[pl.BlockSpec((N_DEV, r // steps, c), lambda i: (0, i, 0)), blk, blk, blk], out_specs=[blk] * 4,
        name=name, compiler_params=_params(1))(recv, w, m, v)


def _gather_last(a):
    def body(a_ref, out_ref, send_sems, recv_sems, own_sems):
        sems = (send_sems, recv_sems, own_sems)
        _exchange('gather', [a_ref], [out_ref], sems, 0, 'start')
        _exchange('gather', [a_ref], [out_ref], sems, 0, 'wait')

    vm = pl.BlockSpec(memory_space=pltpu.VMEM)
    return pl.pallas_call(
        body, out_shape=jax.ShapeDtypeStruct((N_DEV,) + a.shape, a.dtype), in_specs=[vm], out_specs=vm,
        scratch_shapes=[pltpu.SemaphoreType.DMA((7,)), pltpu.SemaphoreType.DMA((7,)), pltpu.SemaphoreType.DMA((1,))],
        name="gather_last", compiler_params=pltpu.CompilerParams(vmem_limit_bytes=V7X_VMEM_LIMIT))(a)


def _small_adamw(entries, gathered, loss_at):
    n, ng = len(entries), len(gathered)

    def body(*refs):
        g_refs, wmv = refs[:ng], refs[ng:ng + 3 * n]
        outs, loss_out = refs[ng + 3 * n:ng + 7 * n], refs[ng + 7 * n]

        def total(ref, r):
            if r is None:
                t = ref[0]
                for k in range(1, N_DEV):
                    t = t + ref[k]
                return t[None]
            t = ref[0, r:r + 1, :]
            for k in range(1, N_DEV):
                t = t + ref[k, r:r + 1, :]
            return t

        for j, (_, _, _, gi, r) in enumerate(entries):
            g = total(g_refs[gi], r)
            d, nm, nv = _adamw(wmv[3 * j][...], g, wmv[3 * j + 1][...], wmv[3 * j + 2][...])
            for ref, val in zip(outs[4 * j:4 * j + 4], (g, d, nm, nv)):
                ref[...] = val
        loss_out[...] = total(g_refs[loss_at[0]], loss_at[1])[:, :128]

    vm = pl.BlockSpec(memory_space=pltpu.VMEM)
    out_shape, arrays = [], list(gathered)
    for w, m, v, _, _ in entries:
        out_shape += [jax.ShapeDtypeStruct(w.shape, F32)] * 4
        arrays += [w, m, v]
    out_shape.append(jax.ShapeDtypeStruct((1, 128), F32))
    res = pl.pallas_call(
        body, out_shape=out_shape, in_specs=[vm] * len(arrays), out_specs=[vm] * len(out_shape),
        name="small_adamw", compiler_params=pltpu.CompilerParams(vmem_limit_bytes=V7X_VMEM_LIMIT))(*arrays)
    return [res[4 * j:4 * j + 4] for j in range(n)], res[4 * n]


def _blockdiag8(t):
    g, per = t.shape[0], 8
    eye = jnp.eye(per, dtype=F32)
    t = t.reshape(g // per, per, t.shape[1], t.shape[2])
    return (t[:, :, :, None, :] * eye[None, :, None, :, None]).reshape(g // per, per * t.shape[2], per * t.shape[3])


def _s5_out_tables(c_re, c_im):
    return _blockdiag8(c_re.transpose(0, 2, 1)), _blockdiag8(c_im.transpose(0, 2, 1))


def _s5_in_tables(lam_re, lam_im, log_step, b_re, b_im):
    step = jnp.exp(log_step)[:, None]
    mag = jnp.exp(lam_re * step)
    ab_re = mag * jnp.cos(lam_im * step)
    ab_im = mag * jnp.sin(lam_im * step)
    den = lam_re * lam_re + lam_im * lam_im
    nr = ab_re - 1.0
    co_re = (nr * lam_re + ab_im * lam_im) / den
    co_im = (ab_im * lam_re - nr * lam_im) / den
    bb_re = co_re[..., None] * b_re - co_im[..., None] * b_im
    bb_im = co_re[..., None] * b_im + co_im[..., None] * b_re
    return (ab_re.reshape(1, -1), ab_im.reshape(1, -1), _blockdiag8(bb_re.transpose(0, 2, 1)),
            _blockdiag8(bb_im.transpose(0, 2, 1)))


def _scan_passes(x_s, a_ref, cr_ref, ci_ref, reverse, re, im):
    tc, nc = x_s.shape[:2]
    ln = SCAN_LANES
    n_sq = int(math.log2(tc))
    assert 2 ** n_sq == tc
    ar = jnp.broadcast_to(a_ref[:, re], (nc, ln))
    ai = jnp.broadcast_to(a_ref[:, im], (nc, ln))
    zero = jnp.zeros((nc, ln), F32)

    def at(t):
        return tc - 1 - t if reverse else t

    def local(t, carry):
        sr, si = carry
        j = at(t)
        nr = ar * sr - ai * si + x_s[j, :, re]
        ni = ar * si + ai * sr + x_s[j, :, im]
        x_s[j, :, re] = nr
        x_s[j, :, im] = ni
        return nr, ni

    lr, li = lax.fori_loop(0, tc, local, (zero, zero))
    pr, pi = a_ref[:, re], a_ref[:, im]
    for _ in range(n_sq):
        pr, pi = pr * pr - pi * pi, 2.0 * (pr * pi)
    cr_ref[...] = lr
    ci_ref[...] = li
    tr = jnp.zeros((1, ln), F32)
    ti = jnp.zeros((1, ln), F32)
    for c in (range(nc - 1, -1, -1) if reverse else range(nc)):
        l_r = cr_ref[c:c + 1, :]
        l_i = ci_ref[c:c + 1, :]
        cr_ref[c:c + 1, :] = tr
        ci_ref[c:c + 1, :] = ti
        tr, ti = pr * tr - pi * ti + l_r, pr * ti + pi * tr + l_i

    def second_pass(on_fixed):
        def fixup(t, carry):
            qr, qi, acc = carry
            j = at(t)
            qr, qi = ar * qr - ai * qi, ar * qi + ai * qr
            gr = x_s[j, :, re] + qr
            gi = x_s[j, :, im] + qi
            x_s[j, :, re] = gr
            x_s[j, :, im] = gi
            return qr, qi, on_fixed(j, gr, gi, acc)

        return fixup

    return second_pass, (cr_ref[...], ci_ref[...]), zero


def _s5_cat_tables(a_re, a_im, bb_re, bb_im, cc_re, cc_im):
    lb = S5_BLOCK_LANES
    n_blk, kq, nq = bb_re.shape
    sub = nq // lb

    def lanes(re, im):
        lead = re.shape[:-1]
        both = jnp.stack([re.reshape(lead + (-1, lb)), im.reshape(lead + (-1, lb))], axis=-2)
        return both.reshape(lead + (-1,))

    cc = jnp.stack([cc_re.reshape(n_blk, sub, lb, kq), -cc_im.reshape(n_blk, sub, lb, kq)], axis=2)
    return lanes(a_re, a_im), lanes(bb_re, bb_im), cc.reshape(n_blk, sub * 2 * lb, kq)


def _s5_uncat(t, axis):
    lb = S5_BLOCK_LANES
    shp = t.shape
    t = t.reshape(shp[:axis] + (-1, 2, lb) + shp[axis + 1:])
    re, im = jnp.take(t, 0, axis=axis + 1), jnp.take(t, 1, axis=axis + 1)
    return re.reshape(shp[:axis] + (-1,) + shp[axis + 1:]), im.reshape(shp[:axis] + (-1,) + shp[axis + 1:])


def _grid2_edges(n0, n1):
    return (lambda: (pl.program_id(0) == 0) & (pl.program_id(1) == 0),
            lambda: (pl.program_id(0) == n0 - 1) & (pl.program_id(1) == n1 - 1))


def _s5_scan_fwd(u5p, a_cat, bb_cat, cc_cat, nb, gather=()):
    n, dh = u5p.shape
    n_blk, kq, cols = bb_cat.shape
    lb, ln, nc = S5_BLOCK_LANES, SCAN_LANES, SCAN_CHUNKS
    sub = cols // (2 * lb)
    rows = n // nb
    tc = rows // nc
    n_l = n_blk * sub

    def body(u_ref, a_ref, b_ref, c_ref, s_ref, y_ref, x_s, car_r, car_i):
        x_s[...] = _dot(u_ref[...].astype(BF16), b_ref[0]).reshape(tc, nc, 2 * lb)
        for h in range(lb // ln):
            re, im = slice(h * ln, (h + 1) * ln), slice(lb + h * ln, lb + (h + 1) * ln)

            def keep(j, gr, gi, acc, re=re, im=im):
                s_ref[0, j, :, re] = gr.astype(BF16)
                s_ref[0, j, :, im] = gi.astype(BF16)
                return acc

            second_pass, start, zero = _scan_passes(x_s, a_ref, car_r, car_i, False, re, im)
            lax.fori_loop(0, tc, second_pass(keep), (*start, zero))
        y = _dot(x_s[...].reshape(rows, 2 * lb).astype(BF16), c_ref[0])

        @pl.when(pl.program_id(1) % sub == 0)
        def _():
            y_ref[...] = y

        @pl.when(pl.program_id(1) % sub != 0)
        def _():
            y_ref[...] += y

    row_blk = pl.BlockSpec((rows, kq), lambda b, l: (b, l // sub))
    ride = _Ride(gather, two_level=True)
    return pl.pallas_call(
        ride.wrap(body, 4, 2, 3, *_grid2_edges(nb, n_l)), grid=(nb, n_l),
        in_specs=[row_blk, pl.BlockSpec((1, 2 * lb), lambda b, l: (0, l)),
                  pl.BlockSpec((1, kq, 2 * lb), lambda b, l: (l // sub, 0, l % sub)),
                  pl.BlockSpec((1, 2 * lb, kq), lambda b, l: (l // sub, l % sub, 0))] + ride.in_specs,
        out_specs=[pl.BlockSpec((1, tc, nc, 2 * lb), lambda b, l: (b, 0, 0, l)), row_blk] + ride.out_specs,
        out_shape=[jax.ShapeDtypeStruct((nb, tc, nc, n_l * 2 * lb), BF16), jax.ShapeDtypeStruct((n, dh), F32)]
        + ride.out_shape,
        scratch_shapes=[pltpu.VMEM((tc, nc, 2 * lb), F32)] + [pltpu.VMEM((nc, ln), F32)] * 2 + ride.scratch,
        name="s5_scan_fwd", compiler_params=_params(2))(u5p, a_cat, bb_cat, cc_cat, *ride.arrays)


def _s5_scan_bwd(dy_pre, u5p, s_cat, a_cat_conj, bb_cat, cc_cat, d_row, gather=()):
    n, dh = u5p.shape
    n_blk, kq, cols = bb_cat.shape
    nb, tc, nc, _ = s_cat.shape
    lb, ln = S5_BLOCK_LANES, SCAN_LANES
    sub = cols // (2 * lb)
    rows = n // nb
    n_l = n_blk * sub

    def body(dy_ref, u_ref, s_ref, a_ref, b_ref, c_ref, d_ref, du_ref, db_ref, dc_ref, da_ref, x_s, car_r, car_i):
        dyb = dy_ref[...].astype(BF16)
        x_s[...] = _dot_nt(dyb, c_ref[0]).reshape(tc, nc, 2 * lb)
        for h in range(lb // ln):
            re, im = slice(h * ln, (h + 1) * ln), slice(lb + h * ln, lb + (h + 1) * ln)

            def with_state_before(j, gr, gi, acc, re=re, im=im):
                dr, di = acc
                pr_, pi_ = s_ref[0, j - 1, :, re].astype(F32), s_ref[0, j - 1, :, im].astype(F32)
                return dr + (pr_ * gr + pi_ * gi), di + (pr_ * gi - pi_ * gr)

            second_pass, start, zero = _scan_passes(x_s, a_ref, car_r, car_i, True, re, im)
            carry = lax.fori_loop(0, tc - 1, second_pass(with_state_before), (*start, (zero, zero)))
            row = lax.broadcasted_iota(jnp.int32, (nc, ln), 0)
            pr_ = jnp.where(row == 0, 0.0, pltpu.roll(s_ref[0, tc - 1, :, re].astype(F32), 1, 0))
            pi_ = jnp.where(row == 0, 0.0, pltpu.roll(s_ref[0, tc - 1, :, im].astype(F32), 1, 0))

            def at_first_time(j, gr, gi, acc, pr_=pr_, pi_=pi_):
                return acc[0] + (pr_ * gr + pi_ * gi), acc[1] + (pr_ * gi - pi_ * gr)

            _, _, (dr, di) = second_pass(at_first_time)(tc - 1, carry)
            da_ref[0, :, re] = jnp.sum(dr, axis=0, keepdims=True)
            da_ref[0, :, im] = jnp.sum(di, axis=0, keepdims=True)

        gb = x_s[...].reshape(rows, 2 * lb).astype(BF16)
        du = _dot_nt(gb, b_ref[0])

        @pl.when(pl.program_id(1) % sub == 0)
        def _():
            du_ref[...] = du + d_ref[...] * dy_ref[...]

        @pl.when(pl.program_id(1) % sub != 0)
        def _():
            du_ref[...] += du

        db_ref[0, 0] = _dot_tn(u_ref[...].astype(BF16), gb)
        dc_ref[0, 0] = _dot_tn(s_ref[0].reshape(rows, 2 * lb), dyb)

    row_blk = pl.BlockSpec((rows, kq), lambda b, l: (b, l // sub))
    in_map = pl.BlockSpec((1, kq, 2 * lb), lambda b, l: (l // sub, 0, l % sub))
    out_map = pl.BlockSpec((1, 2 * lb, kq), lambda b, l: (l // sub, l % sub, 0))
    ride = _Ride(gather)
    return pl.pallas_call(
        ride.wrap(body, 7, 4, 3, *_grid2_edges(nb, n_l)), grid=(nb, n_l),
        in_specs=[row_blk, row_blk, pl.BlockSpec((1, tc, nc, 2 * lb), lambda b, l: (b, 0, 0, l)),
                  pl.BlockSpec((1, 2 * lb), lambda b, l: (0, l)), in_map, out_map,
                  pl.BlockSpec((1, kq), lambda b, l: (0, l // sub))] + ride.in_specs,
        out_specs=[row_blk, pl.BlockSpec((1, 1, kq, 2 * lb), lambda b, l: (b, l // sub, 0, l % sub)),
                   pl.BlockSpec((1, 1, 2 * lb, kq), lambda b, l: (b, l // sub, l % sub, 0)),
                   pl.BlockSpec((1, 1, 2 * lb), lambda b, l: (b, 0, l))] + ride.out_specs,
        out_shape=[jax.ShapeDtypeStruct((n, dh), F32), jax.ShapeDtypeStruct((nb, n_blk, kq, cols), F32),
                   jax.ShapeDtypeStruct((nb, n_blk, cols, kq), F32), jax.ShapeDtypeStruct((nb, 1, n_l * 2 * lb), F32)]
        + ride.out_shape,
        scratch_shapes=[pltpu.VMEM((tc, nc, 2 * lb), F32)] + [pltpu.VMEM((nc, ln), F32)] * 2 + ride.scratch,
        name="s5_scan_bwd", compiler_params=_params(2))(dy_pre, u5p, s_cat, a_cat_conj, bb_cat, cc_cat, d_row,
                                                        *ride.arrays)


def kernel(x, mem, g_mix_pre, w_in, gm_ln_g, gm_ln_b, gm_w_s, gm_b_s, s5_lam_re, s5_lam_im, s5_log_step, s5_b_re, s5_b_im, s5_c_re, s5_c_im, s5_d, s5_w_glu, w_br_gm, w_br_s5, w_mix_out, g_mix_post, g_ca_pre, g_mem, ca_w_q, ca_w_kv, ca_w_o, g_ca_post, g_ffn_pre, ffn_w_gu, ffn_w_down, g_ffn_post, loss_target, m_g_mix_pre, m_w_in, m_gm_ln_g, m_gm_ln_b, m_gm_w_s, m_gm_b_s, m_s5_lam_re, m_s5_lam_im, m_s5_log_step, m_s5_b_re, m_s5_b_im, m_s5_c_re, m_s5_c_im, m_s5_d, m_s5_w_glu, m_w_br_gm, m_w_br_s5, m_w_mix_out, m_g_mix_post, m_g_ca_pre, m_g_mem, m_ca_w_q, m_ca_w_kv, m_ca_w_o, m_g_ca_post, m_g_ffn_pre, m_ffn_w_gu, m_ffn_w_down, m_g_ffn_post, v_g_mix_pre, v_w_in, v_gm_ln_g, v_gm_ln_b, v_gm_w_s, v_gm_b_s, v_s5_lam_re, v_s5_lam_im, v_s5_log_step, v_s5_b_re, v_s5_b_im, v_s5_c_re, v_s5_c_im, v_s5_d, v_s5_w_glu, v_w_br_gm, v_w_br_s5, v_w_mix_out, v_g_mix_post, v_g_ca_pre, v_g_mem, v_ca_w_q, v_ca_w_kv, v_ca_w_o, v_g_ca_post, v_g_ffn_pre, v_ffn_w_gu, v_ffn_w_down, v_g_ffn_post):
    args = locals()
    W = {n: args[n][0] for n in WEIGHTS}
    M = {n: args['m_' + n][0] for n in WEIGHTS}
    V = {n: args['v_' + n][0] for n in WEIGHTS}
    for k, kind in SHARDED.items():
        if kind == 'colT':
            W[k], M[k], V[k] = W[k].T, M[k].T, V[k].T

    nb, seq, d = x.shape
    n = nb * seq
    tm = min(ROW_TILE, n)
    tmw = min(WIDE_ROW_TILE, n)
    nmem = mem.shape[1]
    d2, dh = 2 * d, d // 2
    hd = d // CA_HEADS
    x2d = x.reshape(n, d)
    tgt = loss_target.reshape(n, d)
    mem2d = mem.reshape(nb * nmem, d)

    def row(v):
        return v.reshape(1, -1)

    names = list(SHARDED)
    later = [k for k in names if k != 'w_in']
    first_g, *casts = _first_gather(W['w_in'], [W[k] for k in later])
    shard_b = dict(zip(later, casts))
    half = shard_b['ffn_w_gu'].shape[0] // 2
    gu_halves = [shard_b['ffn_w_gu'][:half], shard_b['ffn_w_gu'][half:]]

    def assemble(k, gth):
        r, c = gth.shape[1:]
        return gth.transpose(1, 0, 2).reshape(r, N_DEV * c) if SHARDED[k] == 'col' else gth.reshape(N_DEV * r, c)

    w_in_t = assemble('w_in', first_g)
    ffh = N_DEV * W['ffn_w_down'].shape[0]

    (a_re, a_im, bb_re, bb_im), s5_in_vjp = jax.vjp(
        _s5_in_tables, W['s5_lam_re'], W['s5_lam_im'], W['s5_log_step'], W['s5_b_re'], W['s5_b_im'])
    (cc_re, cc_im), s5_out_vjp = jax.vjp(_s5_out_tables, W['s5_c_re'], W['s5_c_im'])
    bb_re_b, bb_im_b, cc_re_b, cc_im_b = (t.astype(BF16) for t in (bb_re, bb_im, cc_re, cc_im))
    s5_d_row = row(W['s5_d'])
    tril = jnp.tril(jnp.ones((GM_CHUNK, GM_CHUNK), bool))
    w_s = jnp.where(tril[None], W['gm_w_s'], 0.0).astype(BF16)
    w_s_t = w_s.transpose(0, 2, 1)
    gm_bias = jnp.repeat(W['gm_b_s'].T, d // GM_GROUPS, axis=1)

    def in_proj_body(x_ref, g_ref, w_ref, zgm_ref, u5_ref, zga_ref, zgb_ref, h_ref):
        hb = _rms(x_ref[...], g_ref[...]).astype(BF16)
        h_ref[...] = hb
        for lo in range(0, w_ref.shape[0], 512):
            acc = _dot_nt(hb, w_ref[lo:lo + 512, :])
            if lo < d2:
                zgm_ref[:, lo:lo + 512] = acc.astype(BF16)
            elif lo < d2 + dh:
                u5_ref[...] = acc
            elif lo < d2 + dh + d:
                zga_ref[:, lo - d2 - dh:lo - d2 - dh + 512] = acc.astype(BF16)
            else:
                zgb_ref[:, lo - d2 - dh - d:lo - d2 - dh - d + 512] = acc.astype(BF16)

    ride = ['s5_w_glu', 'w_br_gm', 'w_br_s5']
    z_gm, u5, z_ga, z_gb, h0, *got = _rowcall(
        "in_proj", in_proj_body, n, tm, [(x2d, 'row'), (row(W['g_mix_pre']), 'full'), (w_in_t, 'full')],
        [(d2, BF16), (dh, F32), (d, BF16), (d, BF16), (d, BF16)], gather=[shard_b[k] for k in ride])
    w_glu_f, w_brgm_f, w_brs5_f = (assemble(k, g) for k, g in zip(ride, got))

    gw = d // GM_GROUPS

    def gm_fwd_body(z_ref, lng_ref, lnb_ref, ws_ref, bias_ref, y_ref, u_ref, vh_ref, dgelu_ref, rstd_ref):
        zg, dgelu = _gelu_and_grad(z_ref[...].astype(F32))
        dgelu_ref[...] = dgelu.astype(BF16)
        u, v = zg[:, :d], zg[:, d:]
        vc = v - jnp.mean(v, axis=-1, keepdims=True)
        rstd = lax.rsqrt(jnp.mean(vc * vc, axis=-1, keepdims=True) + EPS)
        vhat = vc * rstd
        u_ref[...] = u.astype(BF16)
        vh_ref[...] = vhat.astype(BF16)
        rstd_ref[...] = jnp.broadcast_to(rstd, rstd_ref.shape)
        vnb = (vhat * lng_ref[...] + lnb_ref[...]).astype(BF16)
        sv = jnp.concatenate([_dot(ws_ref[g], vnb[:, g * gw:(g + 1) * gw]) for g in range(GM_GROUPS)], axis=1)
        y_ref[...] = (u * (sv + bias_ref[...])).astype(BF16)

    y_gm, gm_u, gm_vhat, gm_dgelu, gm_rstd, g_mix, g_q = _rowcall(
        "gmlp_fwd", gm_fwd_body, n, GM_CHUNK,
        [(z_gm, 'row'), (row(W['gm_ln_g']), 'full'), (row(W['gm_ln_b']), 'full'), (w_s, 'full'), (gm_bias, 'full')],
        [(d, BF16), (d, BF16), (d, BF16), (d2, BF16), (128, F32)], gather=[shard_b['w_mix_out'], shard_b['ca_w_q']],
        two_level=True)
    w_mix_f, w_q_f = assemble('w_mix_out', g_mix), assemble('ca_w_q', g_q)

    tc = seq // SCAN_CHUNKS
    lt = S5_GROUPS * S5_STATE

    def to_scan_order(t):
        return t.reshape(nb, SCAN_CHUNKS, tc, t.shape[-1]).transpose(0, 2, 1, 3).reshape(n, t.shape[-1])

    def from_scan_order(t):
        return t.reshape(nb, tc, SCAN_CHUNKS, t.shape[-1]).transpose(0, 2, 1, 3).reshape(n, t.shape[-1])

    u5p = to_scan_order(u5)
    a_cat, bb_cat, cc_cat = _s5_cat_tables(a_re, a_im, bb_re_b, bb_im_b, cc_re_b, cc_im_b)
    a_cat_conj = _s5_cat_tables(a_re, -a_im, bb_re_b, bb_im_b, cc_re_b, cc_im_b)[0]
    s_cat, y_lin, g_kv, g_o = _s5_scan_fwd(u5p, a_cat, bb_cat, cc_cat, nb,
                                           gather=[shard_b['ca_w_kv'], shard_b['ca_w_o']])
    w_kv_f, w_o_f = assemble('ca_w_kv', g_kv), assemble('ca_w_o', g_o)

    def s5_out_body(yl_ref, u_ref, d_ref, wg_ref, ypre_ref, gate_ref, y_ref):
        ypre = yl_ref[...] + d_ref[...] * u_ref[...]
        ypre_ref[...] = ypre
        yg = _gelu(ypre)
        gate = _dot(yg.astype(BF16), wg_ref[...])
        gate_ref[...] = gate
        y_ref[...] = (yg * _sigmoid(gate)).astype(BF16)

    tm5 = min(1024, n)
    y_pre, gate, y_s5p = _rowcall(
        "s5_out", s5_out_body, n, tm5, [(y_lin, 'row'), (u5p, 'row'), (s5_d_row, 'full'), (w_glu_f, 'full')],
        [(dh, F32), (dh, F32), (dh, BF16)])
    y_s5 = from_scan_order(y_s5p)

    def merge_body(ygm_ref, ys5_ref, zga_ref, zgb_ref, wa_ref, wb_ref, pa_ref, pb_ref, mg_ref):
        pa = _dot(ygm_ref[...], wa_ref[...])
        pb = _dot(ys5_ref[...], wb_ref[...])
        pa_ref[...] = pa.astype(BF16)
        pb_ref[...] = pb.astype(BF16)
        mg_ref[...] = (_sigmoid(zga_ref[...].astype(F32)) * pa + _sigmoid(zgb_ref[...].astype(F32)) * pb).astype(BF16)

    p_a, p_b, merged, g_down = _rowcall(
        "merge", merge_body, n, tmw,
        [(y_gm, 'row'), (y_s5, 'row'), (z_ga, 'row'), (z_gb, 'row'), (w_brgm_f, 'full'), (w_brs5_f, 'full')],
        [(d, BF16), (d, BF16), (d, BF16)], gather=[shard_b['ffn_w_down']], two_level=True)
    w_down_f = assemble('ffn_w_down', g_down)

    def close_sublayer(name, a_in, w_out, x_res, g_post, g_next, w_next=None, gather=(), two_level=False):
        def body(*refs):
            a_ref, w_ref, x_ref, gp_ref, gn_ref = refs[:5]
            rest = refs[5:]
            if w_next is not None:
                wn_ref, rest = rest[0], rest[1:]
            o_ref, xo_ref, h_ref = rest[:3]
            o = _dot(a_ref[...], w_ref[...])
            o_ref[...] = o.astype(BF16)
            xo = x_ref[...] + _rms(o, gp_ref[...])
            xo_ref[...] = xo
            hb = _rms(xo, gn_ref[...]).astype(BF16)
            h_ref[...] = hb
            if w_next is not None:
                rest[3][...] = _dot(hb, wn_ref[...]).astype(BF16)

        ins = [(a_in, 'row'), (w_out, 'full'), (x_res, 'row'), (row(g_post), 'full'), (row(g_next), 'full')]
        outs = [(d, BF16), (d, F32), (d, BF16)]
        if w_next is not None:
            ins.append((w_next, 'full'))
            outs.append((w_next.shape[1], BF16))
        return _rowcall(name, body, n, tmw, ins, outs, gather=gather, two_level=two_level)

    o1, x1, hc, q, g_gu0 = close_sublayer("mix_out", merged, w_mix_f, x2d, W['g_mix_post'], W['g_ca_pre'], w_q_f,
                                          gather=[gu_halves[0]])

    tmm = min(ROW_TILE, nb * nmem)

    def memkv_body(m_ref, g_ref, w_ref, mn_ref, k_ref, v_ref):
        mnb = _rms(m_ref[...], g_ref[...]).astype(BF16)
        mn_ref[...] = mnb
        k_ref[...] = _dot(mnb, w_ref[:, :d]).astype(BF16)
        v_ref[...] = _dot(mnb, w_ref[:, d:]).astype(BF16)

    mem_n, k_mem, v_mem = _rowcall("mem_kv", memkv_body, nb * nmem, tmm,
                                   [(mem2d, 'row'), (row(W['g_mem']), 'full'), (w_kv_f, 'full')],
                                   [(d, BF16), (d, BF16), (d, BF16)])

    tiles_per_ex = seq // tm
    kv_spec = pl.BlockSpec((nmem, d), lambda i: (i // tiles_per_ex, 0))
    scale = hd ** -0.5

    def softmax_rows(qh, kh):
        s = _dot_nt(qh, kh) * scale
        e = jnp.exp(s - jnp.max(s, axis=-1, keepdims=True))
        return e / jnp.sum(e, axis=-1, keepdims=True)

    def attn_body(q_ref, k_ref, v_ref, o_ref):
        for h in range(CA_HEADS):
            sl = slice(h * hd, (h + 1) * hd)
            p = softmax_rows(q_ref[:, sl], k_ref[:, sl])
            o_ref[:, sl] = _dot(p.astype(BF16), v_ref[:, sl]).astype(BF16)

    (att,) = _rowcall("attn_fwd", attn_body, n, tm, [(q, 'row'), (k_mem, kv_spec), (v_mem, kv_spec)], [(d, BF16)])

    o2, x2, hf, g_gu1 = close_sublayer("attn_out", att, w_o_f, x1, W['g_ca_post'], W['g_ffn_pre'],
                                       gather=[gu_halves[1]], two_level=True)
    w_gu_t = jnp.stack([g_gu0, g_gu1], axis=1).reshape(2 * ffh, d)

    ck = 256

    def ffn_up_body(h_ref, w_ref, gu_ref, a_ref):
        hb = h_ref[...]
        for lo in range(0, ffh, ck):
            gt = _dot_nt(hb, w_ref[lo:lo + ck, :])
            ut = _dot_nt(hb, w_ref[ffh + lo:ffh + lo + ck, :])
            gu_ref[:, lo:lo + ck] = gt.astype(BF16)
            gu_ref[:, ffh + lo:ffh + lo + ck] = ut.astype(BF16)
            a_ref[:, lo:lo + ck] = ((gt * _sigmoid(gt)) * ut).astype(BF16)

    gu, act = _rowcall("ffn_up", ffn_up_body, n, tm, [(hf, 'row'), (w_gu_t, 'full')], [(2 * ffh, BF16), (ffh, BF16)])

    def ffn_down_body(a_ref, w_ref, x_ref, t_ref, g_ref, dx_ref, do_ref, loss_ref, dg_ref):
        i = pl.program_id(0)

        @pl.when(i == 0)
        def _():
            loss_ref[...] = jnp.zeros_like(loss_ref)
            dg_ref[...] = jnp.zeros_like(dg_ref)

        o = _dot(a_ref[...], w_ref[...])
        diff = x_ref[...] + _rms(o, g_ref[...]) - t_ref[...]
        loss_ref[...] += jnp.full(loss_ref.shape, 0.5 / d, F32) * jnp.sum(diff * diff)
        dx = diff * (1.0 / d)
        dx_ref[...] = dx
        do, dg = _rms_bwd(o, g_ref[...], dx)
        do_ref[...] = do.astype(BF16)
        dg_ref[...] += dg

    dx3, do3, loss_part, dg_ffn_post = _rowcall(
        "ffn_down_loss", ffn_down_body, n, tmw,
        [(act, 'row'), (w_down_f, 'full'), (x2, 'row'), (tgt, 'row'), (row(W['g_ffn_post']), 'full')],
        [(d, F32), (d, BF16)], accs=[((1, 128), F32), ((1, d), F32)])

    G = {'g_ffn_post': dg_ffn_post}
    RECV = {}

    def parts_of(k, gfull):
        r, c = W[k].shape
        return gfull.reshape(r, N_DEV, c).transpose(1, 0, 2) if SHARDED[k] == 'col' else gfull.reshape(N_DEV, r, c)

    p_down = parts_of('ffn_w_down', _dwcall("dw_ffn_down", act, do3, BF16))

    def ffn_act_bwd_body(do_ref, w_ref, gu_ref, dgu_ref):
        dob = do_ref[...]
        for lo in range(0, ffh, ck):
            da = _dot_nt(dob, w_ref[lo:lo + ck, :])
            gt = gu_ref[:, lo:lo + ck].astype(F32)
            ut = gu_ref[:, ffh + lo:ffh + lo + ck].astype(F32)
            sg = _sigmoid(gt)
            dgu_ref[:, lo:lo + ck] = (da * ut * (sg * (1.0 + gt * (1.0 - sg)))).astype(BF16)
            dgu_ref[:, ffh + lo:ffh + lo + ck] = (da * (gt * sg)).astype(BF16)

    dgu, RECV['ffn_w_down'] = _rowcall("ffn_act_bwd", ffn_act_bwd_body, n, tmw,
                                       [(do3, 'row'), (w_down_f, 'full'), (gu, 'row')], [(2 * ffh, BF16)],
                                       scatter=[p_down])
    p_gu = parts_of('ffn_w_gu', _dwcall("dw_ffn_gu", dgu, hf, BF16))

    def open_sublayer(name, pieces, w_full, x_in, g_pre, dx_up, o_prev=None, g_post_prev=None, scatter=(), gather=(),
                      w_t=False):
        n_p = len(pieces)
        second = o_prev is not None

        def body(*refs):
            dc_refs, (w_ref, x_ref, g_ref, dxu_ref), rest = refs[:n_p], refs[n_p:n_p + 4], refs[n_p + 4:]
            if second:
                (op_ref, gp_ref), rest = rest[:2], rest[2:]
            i = pl.program_id(0)
            dhid = None
            for dc_ref, (_, lo, hi) in zip(dc_refs, pieces):
                part = _dot(dc_ref[...], w_ref[lo:hi, :]) if w_t else _dot_nt(dc_ref[...], w_ref[:, lo:hi])
                dhid = part if dhid is None else dhid + part
            dxn, dg = _rms_bwd(x_ref[...], g_ref[...], dhid)
            dx = dxu_ref[...] + dxn
            if second:
                dx_ref, do_ref, dg_ref, dg2_ref = rest
            else:
                dx_ref, dg_ref = rest

            @pl.when(i == 0)
            def _():
                dg_ref[...] = jnp.zeros_like(dg_ref)
                if second:
                    dg2_ref[...] = jnp.zeros_like(dg2_ref)

            dx_ref[...] = dx
            dg_ref[...] += dg
            if second:
                do, dg2 = _rms_bwd(op_ref[...].astype(F32), gp_ref[...], dx)
                do_ref[...] = do.astype(BF16)
                dg2_ref[...] += dg2

        ins = [(p[0], 'row') for p in pieces] + [(w_full, 'full'), (x_in, 'row'), (row(g_pre), 'full'), (dx_up, 'row')]
        outs = [(d, F32)]
        accs = [((1, d), F32)]
        if second:
            ins += [(o_prev, 'row'), (row(g_post_prev), 'full')]
            outs.append((d, BF16))
            accs.append(((1, d), F32))
        light = w_full.size <= d * d
        res = _rowcall(name, body, n, tmw if light else tm, ins, outs, accs=accs, scatter=scatter, gather=gather)
        if second:
            dx, do, dg, dg2 = res[:4]
            return dx, dg, do, dg2, res[4:]
        return res[0], res[1], res[2:]

    dx2, G['g_ffn_pre'], do2, G['g_ca_post'], (RECV['ffn_w_gu'],) = open_sublayer(
        "ffn_in_bwd", [(dgu, 0, 2 * ffh)], w_gu_t, x2, W['g_ffn_pre'], dx3, o2, W['g_ca_post'], scatter=[p_gu],
        w_t=True)
    p_o = parts_of('ca_w_o', _dwcall("dw_ca_o", att, do2, BF16))

    def attn_bwd_body(q_ref, k_ref, v_ref, do_ref, wo_ref, dq_ref, dk_ref, dv_ref):
        i = pl.program_id(0)

        @pl.when(i % tiles_per_ex == 0)
        def _():
            dk_ref[...] = jnp.zeros_like(dk_ref)
            dv_ref[...] = jnp.zeros_like(dv_ref)

        d_att = _dot_nt(do_ref[...], wo_ref[...]).astype(BF16)
        for h in range(CA_HEADS):
            sl = slice(h * hd, (h + 1) * hd)
            qh, kh, vh, dah = q_ref[:, sl], k_ref[:, sl], v_ref[:, sl], d_att[:, sl]
            p = softmax_rows(qh, kh)
            dp = _dot_nt(dah, vh)
            ds = (p * (dp - jnp.sum(p * dp, axis=-1, keepdims=True)) * scale).astype(BF16)
            dq_ref[:, sl] = _dot(ds, kh).astype(BF16)
            dk_ref[:, sl] += _dot_tn(ds, qh)
            dv_ref[:, sl] += _dot_tn(p.astype(BF16), dah)

    kv_acc = ((nb * nmem, d), (F32, pl.BlockSpec((nmem, d), lambda i: (i // tiles_per_ex, 0))))
    dq, dk_mem, dv_mem, RECV['ca_w_o'] = _rowcall(
        "attn_bwd", attn_bwd_body, n, tm,
        [(q, 'row'), (k_mem, kv_spec), (v_mem, kv_spec), (do2, 'row'), (w_o_f, 'full')], [(d, BF16)],
        accs=[kv_acc, kv_acc], scatter=[p_o])
    p_q = parts_of('ca_w_q', _dwcall("dw_ca_q", hc, dq, BF16))

    def memkv_bwd_body(dk_ref, dv_ref, m_ref, g_ref, w_ref, dkv_ref, dg_ref):
        i = pl.program_id(0)

        @pl.when(i == 0)
        def _():
            dg_ref[...] = jnp.zeros_like(dg_ref)

        dkb, dvb = dk_ref[...].astype(BF16), dv_ref[...].astype(BF16)
        dkv_ref[:, :d] = dkb
        dkv_ref[:, d:] = dvb
        dmn = _dot_nt(dkb, w_ref[:, :d]) + _dot_nt(dvb, w_ref[:, d:])
        _, dg = _rms_bwd(m_ref[...], g_ref[...], dmn)
        dg_ref[...] += dg

    dkv, G['g_mem'] = _rowcall(
        "mem_kv_bwd", memkv_bwd_body, nb * nmem, tmm,
        [(dk_mem, 'row'), (dv_mem, 'row'), (mem2d, 'row'), (row(W['g_mem']), 'full'), (w_kv_f, 'full')],
        [(d2, BF16)], accs=[((1, d), F32)])
    p_kv = parts_of('ca_w_kv', _dwcall("dw_ca_kv", mem_n, dkv, BF16))

    dx1, G['g_ca_pre'], do1, G['g_mix_post'], (RECV['ca_w_q'], RECV['ca_w_kv']) = open_sublayer(
        "attn_in_bwd", [(dq, 0, d)], w_q_f, x1, W['g_ca_pre'], dx2, o1, W['g_mix_post'], scatter=[p_q, p_kv])
    p_mix = parts_of('w_mix_out', _dwcall("dw_mix_out", merged, do1, BF16))

    def merge_bwd_body(do_ref, wm_ref, zga_ref, zgb_ref, pa_ref, pb_ref, wb_ref, dpa_ref, dpb_ref, dza_ref, dzb_ref,
                       dys_ref):
        dm = _dot_nt(do_ref[...], wm_ref[...])
        sa, sb = _sigmoid(zga_ref[...].astype(F32)), _sigmoid(zgb_ref[...].astype(F32))
        dpb = (dm * sb).astype(BF16)
        dpa_ref[...] = (dm * sa).astype(BF16)
        dpb_ref[...] = dpb
        dza_ref[...] = (dm * pa_ref[...].astype(F32) * (sa * (1.0 - sa))).astype(BF16)
        dzb_ref[...] = (dm * pb_ref[...].astype(F32) * (sb * (1.0 - sb))).astype(BF16)
        dys_ref[...] = _dot_nt(dpb, wb_ref[...]).astype(BF16)

    NORM_ROWS = ['g_ffn_post', 'g_ffn_pre', 'g_ca_post', 'g_mem', 'g_ca_pre', 'g_mix_post']
    norm_rows = jnp.concatenate([G[k] for k in NORM_ROWS] + [jnp.tile(loss_part, (1, d // 128))], axis=0)
    dp_a, dp_b, dz_ga, dz_gb, dy_s5, all_norm_rows, RECV['w_mix_out'] = _rowcall(
        "merge_bwd", merge_bwd_body, n, tmw,
        [(do1, 'row'), (w_mix_f, 'full'), (z_ga, 'row'), (z_gb, 'row'), (p_a, 'row'), (p_b, 'row'), (w_brs5_f, 'full')],
        [(d, BF16), (d, BF16), (d, BF16), (d, BF16), (dh, BF16)], scatter=[p_mix], gather=[norm_rows])
    p_brgm = parts_of('w_br_gm', _dwcall("dw_br_gm", y_gm, dp_a, BF16))
    p_brs5 = parts_of('w_br_s5', _dwcall("dw_br_s5", y_s5, dp_b, BF16))

    def gm_bwd_body(u_ref, vh_ref, dgelu_ref, rstd_ref, dpa_ref, wa_ref, lng_ref, lnb_ref, ws_ref, wst_ref, bias_ref,
                    dz_ref, dws_ref, dbias_ref, dlng_ref, dlnb_ref):
        i = pl.program_id(0)

        @pl.when(i == 0)
        def _():
            dws_ref[...] = jnp.zeros_like(dws_ref)
            dbias_ref[...] = jnp.zeros_like(dbias_ref)
            dlng_ref[...] = jnp.zeros_like(dlng_ref)
            dlnb_ref[...] = jnp.zeros_like(dlnb_ref)

        u, vhat, rstd = u_ref[...].astype(F32), vh_ref[...].astype(F32), rstd_ref[:, :1]
        vnb = (vhat * lng_ref[...] + lnb_ref[...]).astype(BF16)
        dy = _dot_nt(dpa_ref[...], wa_ref[...])
        groups = [slice(g * gw, (g + 1) * gw) for g in range(GM_GROUPS)]
        dsv = dy * u
        dsvb = dsv.astype(BF16)
        dbias_ref[...] += dsv
        sv = jnp.concatenate([_dot(ws_ref[g], vnb[:, sl]) for g, sl in enumerate(groups)], axis=1) + bias_ref[...]
        dvn = jnp.concatenate([_dot(wst_ref[g], dsvb[:, sl]) for g, sl in enumerate(groups)], axis=1)
        for g, sl in enumerate(groups):
            dws_ref[g] += _dot_nt(dsvb[:, sl], vnb[:, sl])
        du = dy * sv
        dlng_ref[...] += jnp.sum(dvn * vhat, axis=0, keepdims=True)
        dlnb_ref[...] += jnp.sum(dvn, axis=0, keepdims=True)
        dvh = dvn * lng_ref[...]
        dv = rstd * (dvh - jnp.mean(dvh, axis=-1, keepdims=True) - vhat * jnp.mean(dvh * vhat, axis=-1, keepdims=True))
        dz_ref[:, :d] = (du * dgelu_ref[:, :d].astype(F32)).astype(BF16)
        dz_ref[:, d:] = (dv * dgelu_ref[:, d:].astype(F32)).astype(BF16)

    dz_gm, dws_full, dbias_full, G['gm_ln_g'], G['gm_ln_b'], RECV['w_br_gm'], RECV['w_br_s5'] = _rowcall(
        "gmlp_bwd", gm_bwd_body, n, GM_CHUNK,
        [(gm_u, 'row'), (gm_vhat, 'row'), (gm_dgelu, 'row'), (gm_rstd, 'row'), (dp_a, 'row'), (w_brgm_f, 'full'),
         (row(W['gm_ln_g']), 'full'), (row(W['gm_ln_b']), 'full'), (w_s, 'full'), (w_s_t, 'full'), (gm_bias, 'full')],
        [(d2, BF16)], accs=[((GM_GROUPS, GM_CHUNK, GM_CHUNK), F32), ((GM_CHUNK, d), F32), ((1, d), F32), ((1, d), F32)],
        scatter=[p_brgm, p_brs5])
    G['gm_w_s'] = jnp.where(tril[None], dws_full, 0.0)
    G['gm_b_s'] = dbias_full.reshape(GM_CHUNK, GM_GROUPS, gw).sum(-1).T

    dy_s5p = to_scan_order(dy_s5)

    def s5_out_bwd_body(dy_ref, ypre_ref, gate_ref, u_ref, wg_ref, dyp_ref, dgate_ref, yg_ref, dd_ref):
        i = pl.program_id(0)

        @pl.when(i == 0)
        def _():
            dd_ref[...] = jnp.zeros_like(dd_ref)

        dy = dy_ref[...].astype(F32)
        yg, dgelu = _gelu_and_grad(ypre_ref[...])
        sg = _sigmoid(gate_ref[...])
        dgb = (dy * yg * (sg * (1.0 - sg))).astype(BF16)
        dgate_ref[...] = dgb
        yg_ref[...] = yg.astype(BF16)
        dyp = (dy * sg + _dot_nt(dgb, wg_ref[...])) * dgelu
        dyp_ref[...] = dyp
        dd_ref[...] += jnp.sum(dyp * u_ref[...], axis=0, keepdims=True)

    dy_pre, dgate, yg_b, dd = _rowcall(
        "s5_out_bwd", s5_out_bwd_body, n, tm5,
        [(dy_s5p, 'row'), (y_pre, 'row'), (gate, 'row'), (u5p, 'row'), (w_glu_f, 'full')],
        [(dh, F32), (dh, BF16), (dh, BF16)], accs=[((1, dh), F32)])
    gm_ln_rows = jnp.concatenate([G['gm_ln_g'], G['gm_ln_b']], axis=0)
    du5p, d_bb_cat, d_cc_cat, da_cat, all_gm_w_s, all_gm_b_s, all_gm_ln = _s5_scan_bwd(
        dy_pre, u5p, s_cat, a_cat_conj, bb_cat, cc_cat, s5_d_row, gather=[G['gm_w_s'], G['gm_b_s'], gm_ln_rows])
    du5 = from_scan_order(du5p.astype(BF16))
    d_bb_re, d_bb_im = _s5_uncat(jnp.sum(d_bb_cat, axis=0), 2)
    d_cc_re, d_cc_neg_im = _s5_uncat(jnp.sum(d_cc_cat, axis=0), 1)
    da_re, da_im = _s5_uncat(jnp.sum(da_cat, axis=0), 1)
    lane_shape = (S5_GROUPS, 8, 128)
    g_c_re, g_c_im = (t.reshape(lane_shape) for t in s5_out_vjp((d_cc_re, -d_cc_neg_im)))
    g_d = dd.reshape(S5_GROUPS, S5_CH)
    g_lam_re, g_lam_im, g_log_step, g_b_re, g_b_im = s5_in_vjp((da_re, da_im, d_bb_re, d_bb_im))
    g_log_step, g_b_re, g_b_im = g_log_step.reshape(1, -1), g_b_re.reshape(lane_shape), g_b_im.reshape(lane_shape)
    p_glu = parts_of('s5_w_glu', _dwcall("dw_s5_glu", yg_b, dgate, BF16))

    pieces = [(dz_gm, 0, d2), (du5, d2, d2 + dh), (dz_ga, d2 + dh, d2 + dh + d), (dz_gb, d2 + dh + d, d2 + dh + 2 * d)]
    dw_gm, all_lam_re, all_lam_im, all_log_step, all_b_re, all_b_im = _dwcall(
        "dw_in_0", dz_gm, h0, BF16, gather=[g_lam_re, g_lam_im, g_log_step, g_b_re, g_b_im])
    dw_s5 = _dwcall("dw_in_1", du5, h0, BF16)
    dw_ga = _dwcall("dw_in_2", dz_ga, h0, BF16)
    dw_gb, all_c_re, all_c_im, all_d, RECV['s5_w_glu'] = _dwcall(
        "dw_in_3", dz_gb, h0, BF16, gather=[g_c_re, g_c_im, g_d], scatter=[p_glu])
    p_in = parts_of('w_in', jnp.concatenate([dw_gm, dw_s5, dw_ga, dw_gb], axis=0))
    grad_x2d, g_mix_pre_part, (RECV['w_in'],) = open_sublayer(
        "in_proj_bwd", pieces, w_in_t, x2d, W['g_mix_pre'], dx1, scatter=[p_in], w_t=True)

    out = {}
    for k in names:
        quad = _sum_adamw("sum_adamw_" + k, RECV[k], W[k], M[k], V[k])
        out[k] = [(t.T if SHARDED[k] == 'colT' else t)[None] for t in quad]
    gathered = [all_norm_rows, all_gm_w_s, all_gm_b_s, all_gm_ln, all_c_re, all_c_im, all_d, all_lam_re, all_lam_im,
                all_log_step, all_b_re, all_b_im, _gather_last(g_mix_pre_part)]
    where = {'g_ffn_post': (0, 0), 'g_ffn_pre': (0, 1), 'g_ca_post': (0, 2), 'g_mem': (0, 3), 'g_ca_pre': (0, 4),
             'g_mix_post': (0, 5), 'gm_w_s': (1, None), 'gm_b_s': (2, None), 'gm_ln_g': (3, 0), 'gm_ln_b': (3, 1),
             's5_c_re': (4, None), 's5_c_im': (5, None), 's5_d': (6, None), 's5_lam_re': (7, None),
             's5_lam_im': (8, None), 's5_log_step': (9, 0), 's5_b_re': (10, None), 's5_b_im': (11, None),
             'g_mix_pre': (12, 0)}
    folded = ('s5_b_re', 's5_b_im', 's5_c_re', 's5_c_im')

    def as_updated(k, t):
        return t.reshape((1,) + lane_shape) if k in folded else t

    small, loss_row = _small_adamw(
        [(as_updated(k, args[k]), as_updated(k, args['m_' + k]), as_updated(k, args['v_' + k])) + where[k]
         for k in SMALL], gathered, (0, len(NORM_ROWS)))
    out.update({k: [t.reshape(args[k].shape) for t in quad] for k, quad in zip(SMALL, small)})

    res = [loss_row[0, 0], grad_x2d.reshape(x.shape)]
    for j in range(4):
        res += [out[k][j] for k in WEIGHTS]
    return tuple(res)
```

```python
import functools
import math

import jax
import jax.numpy as jnp
from jax import lax
from jax.experimental import pallas as pl
from jax.experimental.pallas import tpu as pltpu

F32 = jnp.float32
BF16 = jnp.bfloat16
EPS = 1e-6
N_DEV = 8
V7X_VMEM_LIMIT = 56 * 1024 * 1024
ROW_TILE = 256
WIDE_ROW_TILE = 512
GM_CHUNK = 128
GM_GROUPS = 8
S5_GROUPS = 32
S5_STATE = 64
S5_CH = 16
SCAN_CHUNKS = 32
SCAN_LANES = 128
S5_BLOCK_LANES = 256
CA_HEADS = 4
ADAM_LR, ADAM_B1, ADAM_B2, ADAM_EPS, ADAM_WD, ADAM_STEP = 0.001, 0.9, 0.999, 1e-08, 0.01, 10

WEIGHTS = ['g_mix_pre', 'w_in', 'gm_ln_g', 'gm_ln_b', 'gm_w_s', 'gm_b_s', 's5_lam_re', 's5_lam_im', 's5_log_step',
           's5_b_re', 's5_b_im', 's5_c_re', 's5_c_im', 's5_d', 's5_w_glu', 'w_br_gm', 'w_br_s5', 'w_mix_out',
           'g_mix_post', 'g_ca_pre', 'g_mem', 'ca_w_q', 'ca_w_kv', 'ca_w_o', 'g_ca_post', 'g_ffn_pre', 'ffn_w_gu',
           'ffn_w_down', 'g_ffn_post']
SHARDED = {'w_in': 'colT', 's5_w_glu': 'row', 'w_br_gm': 'row', 'w_br_s5': 'col', 'w_mix_out': 'row',
           'ca_w_q': 'row', 'ca_w_kv': 'col', 'ca_w_o': 'row', 'ffn_w_gu': 'colT', 'ffn_w_down': 'row'}
SMALL = [n for n in WEIGHTS if n not in SHARDED]


def _rms(x, g):
    r = lax.rsqrt(jnp.mean(x * x, axis=-1, keepdims=True) + EPS)
    return (x * r) * g


def _rms_bwd(x, g, dy):
    r = lax.rsqrt(jnp.mean(x * x, axis=-1, keepdims=True) + EPS)
    n = x * r
    dn = dy * g
    dx = r * (dn - n * jnp.mean(dn * n, axis=-1, keepdims=True))
    return dx, jnp.sum(dy * n, axis=0, keepdims=True)


_GELU_C = math.sqrt(2.0 / math.pi)


def _gelu(x):
    return 0.5 * x * (1.0 + jnp.tanh(_GELU_C * (x + 0.044715 * (x * x * x))))


def _gelu_and_grad(x):
    x2 = x * x
    t = jnp.tanh(_GELU_C * (x + 0.044715 * (x2 * x)))
    h = 0.5 * (1.0 + t)
    return x * h, h + 0.5 * x * (1.0 - t * t) * (_GELU_C * (1.0 + 3.0 * 0.044715 * x2))


def _sigmoid(x):
    return 0.5 * (1.0 + jnp.tanh(0.5 * x))


def _dot(a, b):
    return jnp.dot(a, b, preferred_element_type=F32)


def _dot_nt(a, b):
    return lax.dot_general(a, b, (((1,), (1,)), ((), ())), preferred_element_type=F32)


def _dot_tn(a, b):
    return lax.dot_general(a, b, (((0,), (0,)), ((), ())), preferred_element_type=F32)


def _adamw(w, g, m, v):
    m = ADAM_B1 * m + (1.0 - ADAM_B1) * g
    v = ADAM_B2 * v + (1.0 - ADAM_B2) * (g * g)
    m_hat = m / (1.0 - ADAM_B1 ** ADAM_STEP)
    v_hat = v / (1.0 - ADAM_B2 ** ADAM_STEP)
    delta = -ADAM_LR * (m_hat / (jnp.sqrt(v_hat) + ADAM_EPS) + ADAM_WD * w)
    return delta, m, v


def _params(n_grid):
    return pltpu.CompilerParams(dimension_semantics=("arbitrary",) * n_grid, vmem_limit_bytes=V7X_VMEM_LIMIT)


def _my_place():
    x, y, c = lax.axis_index("x"), lax.axis_index("y"), lax.axis_index("c")
    return x, y, c


def _peer(x, y, c, k):
    px = 1 - x if k & 4 else x
    py = 1 - y if k & 2 else y
    pc = 1 - c if k & 1 else c
    return (px, py, pc), 4 * px + 2 * py + pc


def _exchange(kind, src_refs, dst_refs, sems, first, phase):
    x, y, c = _my_place()
    me = 4 * x + 2 * y + c
    send_sems, recv_sems, own_sems = sems
    for j, (src, dst) in enumerate(zip(src_refs, dst_refs), start=first):
        own = pltpu.make_async_copy(src if kind == 'gather' else src.at[me], dst.at[me], own_sems.at[j])
        if phase == 'start':
            own.start()
        for k in range(1, N_DEV):
            peer, peer_block = _peer(x, y, c, k)
            out = pltpu.make_async_remote_copy(
                src_ref=src if kind == 'gather' else src.at[peer_block], dst_ref=dst.at[me],
                send_sem=send_sems.at[7 * j + k - 1], recv_sem=recv_sems.at[7 * j + k - 1], device_id=peer,
                device_id_type=pl.DeviceIdType.MESH)
            if phase == 'start':
                out.start()
            else:
                pltpu.make_async_remote_copy(
                    src_ref=src if kind == 'gather' else src.at[peer_block], dst_ref=dst.at[peer_block],
                    send_sem=send_sems.at[7 * j + k - 1], recv_sem=recv_sems.at[7 * j + k - 1], device_id=peer,
                    device_id_type=pl.DeviceIdType.MESH).wait_recv()
                out.wait_send()
        if phase == 'wait':
            own.wait()


def _gather_two_level(src_refs, dst_refs, sems, first, phase):
    x, y, c = _my_place()
    me, sibling = (x, y, c), (x, y, 1 - c)
    chips = [(1 - x, y), (x, 1 - y), (1 - x, 1 - y)]
    send_sems, recv_sems, own_sems = sems
    for j, (src, dst) in enumerate(zip(src_refs, dst_refs), start=first):
        def rows(px, py, pc, dst=dst):
            return dst.at[4 * px + 2 * py + pc]

        def copy(k, block, to, from_src=False, j=j, src=src, rows=rows):
            return pltpu.make_async_remote_copy(
                src_ref=src if from_src else rows(*block), dst_ref=rows(*block), send_sem=send_sems.at[7 * j + k],
                recv_sem=recv_sems.at[7 * j + k], device_id=to, device_id_type=pl.DeviceIdType.MESH)

        mine = pltpu.make_async_copy(src, rows(*me), own_sems.at[j])
        first_out = [copy(0, me, sibling, True)] + [copy(1 + i, me, (*chip, c), True) for i, chip in enumerate(chips)]
        if phase == 'start':
            mine.start()
            for cp in first_out:
                cp.start()
        else:
            passed = [copy(4 + i, (*chip, c), sibling) for i, chip in enumerate(chips)]
            for i, chip in enumerate(chips):
                copy(1 + i, (*chip, c), me).wait_recv()
                passed[i].start()
            copy(0, sibling, me).wait_recv()
            for i, chip in enumerate(chips):
                copy(4 + i, (*chip, 1 - c), me).wait_recv()
            for cp in first_out + passed:
                cp.wait_send()
            mine.wait()


class _Ride:
    def __init__(self, gather=(), scatter=(), two_level=False):
        self.two_level = two_level
        self.n_gather = len(gather)
        self.arrays = list(gather) + list(scatter)
        n = len(self.arrays)
        hbm = pl.BlockSpec(memory_space=pl.ANY)
        self.in_specs = [hbm] * n
        self.out_specs = [hbm] * n
        self.out_shape = [jax.ShapeDtypeStruct((N_DEV,) + a.shape, a.dtype) for a in gather]
        self.out_shape += [jax.ShapeDtypeStruct(a.shape, a.dtype) for a in scatter]
        self.scratch = [pltpu.SemaphoreType.DMA((7 * n,)), pltpu.SemaphoreType.DMA((7 * n,)),
                        pltpu.SemaphoreType.DMA((n,))] if n else []

    def wrap(self, inner, n_in, n_out, n_scr, is_first, is_last):
        n_mv = len(self.arrays)
        if not n_mv:
            return inner

        def body(*refs):
            mv_src = refs[n_in:n_in + n_mv]
            mv_dst = refs[n_in + n_mv + n_out:n_in + 2 * n_mv + n_out]
            sems = refs[n_in + 2 * n_mv + n_out + n_scr:]

            def exchange(phase):
                if self.n_gather and self.two_level:
                    _gather_two_level(mv_src[:self.n_gather], mv_dst[:self.n_gather], sems, 0, phase)
                elif self.n_gather:
                    _exchange('gather', mv_src[:self.n_gather], mv_dst[:self.n_gather], sems, 0, phase)
                if n_mv > self.n_gather:
                    _exchange('scatter', mv_src[self.n_gather:], mv_dst[self.n_gather:], sems, self.n_gather, phase)

            pl.when(is_first())(functools.partial(exchange, 'start'))
            inner(*refs[:n_in], *refs[n_in + n_mv:n_in + n_mv + n_out],
                  *refs[n_in + 2 * n_mv + n_out:n_in + 2 * n_mv + n_out + n_scr])
            pl.when(is_last())(functools.partial(exchange, 'wait'))

        return body


def _rowcall(name, body, n_rows, tm, ins, outs, accs=(), scratch=(), gather=(), scatter=(), two_level=False):
    n_steps = n_rows // tm
    ride = _Ride(gather, scatter, two_level)
    body = ride.wrap(body, len(ins), len(outs) + len(accs), len(scratch), lambda: pl.program_id(0) == 0,
                     lambda: pl.program_id(0) == n_steps - 1)
    arrays, in_specs = [], []
    for a, kind in ins:
        arrays.append(a)
        if kind == 'row':
            in_specs.append(pl.BlockSpec((tm,) + a.shape[1:], lambda i, nd=a.ndim: (i,) + (0,) * (nd - 1)))
        elif kind == 'full':
            in_specs.append(pl.BlockSpec(a.shape, lambda i, nd=a.ndim: (0,) * nd))
        else:
            in_specs.append(kind)
    out_shape, out_specs = [], []
    for cols, dt in outs:
        out_shape.append(jax.ShapeDtypeStruct((n_rows, cols), dt))
        out_specs.append(pl.BlockSpec((tm, cols), lambda i: (i, 0)))
    for shp, dt in accs:
        if isinstance(dt, tuple):
            dt, spec = dt
        else:
            spec = pl.BlockSpec(shp, lambda i, nd=len(shp): (0,) * nd)
        out_shape.append(jax.ShapeDtypeStruct(shp, dt))
        out_specs.append(spec)
    return pl.pallas_call(functools.partial(body), grid=(n_steps,), in_specs=in_specs + ride.in_specs,
                          out_specs=out_specs + ride.out_specs, out_shape=out_shape + ride.out_shape,
                          scratch_shapes=list(scratch) + ride.scratch, name=name,
                          compiler_params=_params(1))(*arrays, *ride.arrays)


DW_ACC_BYTES = 12 * 1024 * 1024
DW_LHS_BYTES = 6 * 1024 * 1024


def _dw_tiles(n, ka, nn, a_itemsize):
    tn = max(t for t in range(128, min(nn, 1536) + 1, 128) if nn % t == 0)
    tka = max(t for t in range(128, ka + 1, 128) if ka % t == 0 and t * tn * 4 <= DW_ACC_BYTES)
    tm = min(n, 2048)
    while tm > 256 and tm * tka * a_itemsize > DW_LHS_BYTES:
        tm //= 2
    return tm, tka, tn


def _dwcall(name, a, dc, out_dtype, gather=(), scatter=()):
    n, ka = a.shape
    nn = dc.shape[1]
    tm, tka, tn = _dw_tiles(n, ka, nn, a.dtype.itemsize)
    n_i, n_j, n_k = n // tm, nn // tn, ka // tka
    ride = _Ride(gather, scatter)

    def body(a_ref, dc_ref, o_ref, acc_ref):
        i = pl.program_id(2)

        @pl.when(i == 0)
        def _():
            acc_ref[...] = jnp.zeros_like(acc_ref)

        acc_ref[...] += _dot_tn(a_ref[...].astype(BF16), dc_ref[...].astype(BF16))

        @pl.when(i == n_i - 1)
        def _():
            o_ref[...] = acc_ref[...].astype(o_ref.dtype)

    def at_step(k, j, i):
        return lambda: (pl.program_id(0) == k) & (pl.program_id(1) == j) & (pl.program_id(2) == i)

    body = ride.wrap(body, 2, 1, 1, at_step(0, 0, 0), at_step(n_k - 1, n_j - 1, n_i - 1))
    res = pl.pallas_call(
        body, grid=(n_k, n_j, n_i),
        in_specs=[pl.BlockSpec((tm, tka), lambda k, j, i: (i, k)), pl.BlockSpec((tm, tn), lambda k, j, i: (i, j))]
        + ride.in_specs,
        out_specs=[pl.BlockSpec((tka, tn), lambda k, j, i: (k, j))] + ride.out_specs,
        out_shape=[jax.ShapeDtypeStruct((ka, nn), out_dtype)] + ride.out_shape,
        scratch_shapes=[pltpu.VMEM((tka, tn), F32)] + ride.scratch, name=name, compiler_params=_params(3))(
            a, dc, *ride.arrays)
    return res if ride.arrays else res[0]


def _first_gather(first, later):
    n = len(later)
    vm = pl.BlockSpec(memory_space=pltpu.VMEM)

    def body(*refs):
        first_ref, later_refs = refs[0], refs[1:1 + n]
        out_ref, cast_refs = refs[1 + n], refs[2 + n:2 + 2 * n]
        stage = refs[2 + 2 * n]
        sems = refs[3 + 2 * n:]
        stage[...] = first_ref[...].astype(BF16)
        _gather_two_level([stage], [out_ref], sems, 0, 'start')
        for src, dst in zip(later_refs, cast_refs):
            dst[...] = src[...].astype(BF16)
        _gather_two_level([stage], [out_ref], sems, 0, 'wait')

    return pl.pallas_call(
        body, out_shape=[jax.ShapeDtypeStruct((N_DEV,) + first.shape, BF16)]
        + [jax.ShapeDtypeStruct(s.shape, BF16) for s in later],
        in_specs=[vm] * (1 + n), out_specs=[pl.BlockSpec(memory_space=pl.ANY)] + [vm] * n,
        scratch_shapes=[pltpu.VMEM(first.shape, BF16), pltpu.SemaphoreType.DMA((7,)), pltpu.SemaphoreType.DMA((7,)),
                        pltpu.SemaphoreType.DMA((1,))],
        name="gather_first", compiler_params=pltpu.CompilerParams(vmem_limit_bytes=V7X_VMEM_LIMIT))(first, *later)


def _sum_adamw(name, recv, w, m, v):
    r, c = w.shape
    steps = max(s for s in (4, 2, 1) if r % s == 0 and (r // s) % 16 == 0 or s == 1)

    def body(recv_ref, w_ref, m_ref, v_ref, g_ref, d_ref, nm_ref, nv_ref):
        g = recv_ref[0].astype(F32)
        for k in range(1, N_DEV):
            g = g + recv_ref[k].astype(F32)
        d, nm, nv = _adamw(w_ref[...], g, m_ref[...], v_ref[...])
        g_ref[...] = g
        d_ref[...] = d
        nm_ref[...] = nm
        nv_ref[...] = nv

    blk = pl.BlockSpec((r // steps, c), lambda i: (i, 0))
    return pl.pallas_call(
        body, grid=(steps,), out_shape=[jax.ShapeDtypeStruct((r, c), F32)] * 4,
        in_specs=[pl.BlockSpec((N_DEV, r // steps, c), lambda i: (0, i, 0)), blk, blk, blk], out_specs=[blk] * 4,
        name=name, compiler_params=_params(1))(recv, w, m, v)


def _gather_last(a):
    def body(a_ref, out_ref, send_sems, recv_sems, own_sems):
        sems = (send_sems, recv_sems, own_sems)
        _exchange('gather', [a_ref], [out_ref], sems, 0, 'start')
        _exchange('gather', [a_ref], [out_ref], sems, 0, 'wait')

    vm = pl.BlockSpec(memory_space=pltpu.VMEM)
    return pl.pallas_call(
        body, out_shape=jax.ShapeDtypeStruct((N_DEV,) + a.shape, a.dtype), in_specs=[vm], out_specs=vm,
        scratch_shapes=[pltpu.SemaphoreType.DMA((7,)), pltpu.SemaphoreType.DMA((7,)), pltpu.SemaphoreType.DMA((1,))],
        name="gather_last", compiler_params=pltpu.CompilerParams(vmem_limit_bytes=V7X_VMEM_LIMIT))(a)


def _small_adamw(entries, gathered, loss_at):
    n, ng = len(entries), len(gathered)

    def body(*refs):
        g_refs, wmv = refs[:ng], refs[ng:ng + 3 * n]
        outs, loss_out = refs[ng + 3 * n:ng + 7 * n], refs[ng + 7 * n]

        def total(ref, r):
            if r is None:
                t = ref[0]
                for k in range(1, N_DEV):
                    t = t + ref[k]
                return t[None]
            t = ref[0, r:r + 1, :]
            for k in range(1, N_DEV):
                t = t + ref[k, r:r + 1, :]
            return t

        for j, (_, _, _, gi, r) in enumerate(entries):
            g = total(g_refs[gi], r)
            d, nm, nv = _adamw(wmv[3 * j][...], g, wmv[3 * j + 1][...], wmv[3 * j + 2][...])
            for ref, val in zip(outs[4 * j:4 * j + 4], (g, d, nm, nv)):
                ref[...] = val
        loss_out[...] = total(g_refs[loss_at[0]], loss_at[1])[:, :128]

    vm = pl.BlockSpec(memory_space=pltpu.VMEM)
    out_shape, arrays = [], list(gathered)
    for w, m, v, _, _ in entries:
        out_shape += [jax.ShapeDtypeStruct(w.shape, F32)] * 4
        arrays += [w, m, v]
    out_shape.append(jax.ShapeDtypeStruct((1, 128), F32))
    res = pl.pallas_call(
        body, out_shape=out_shape, in_specs=[vm] * len(arrays), out_specs=[vm] * len(out_shape),
        name="small_adamw", compiler_params=pltpu.CompilerParams(vmem_limit_bytes=V7X_VMEM_LIMIT))(*arrays)
    return [res[4 * j:4 * j + 4] for j in range(n)], res[4 * n]


def _blockdiag8(t):
    g, per = t.shape[0], 8
    eye = jnp.eye(per, dtype=F32)
    t = t.reshape(g // per, per, t.shape[1], t.shape[2])
    return (t[:, :, :, None, :] * eye[None, :, None, :, None]).reshape(g // per, per * t.shape[2], per * t.shape[3])


def _s5_out_tables(c_re, c_im):
    return _blockdiag8(c_re.transpose(0, 2, 1)), _blockdiag8(c_im.transpose(0, 2, 1))


def _s5_in_tables(lam_re, lam_im, log_step, b_re, b_im):
    step = jnp.exp(log_step)[:, None]
    mag = jnp.exp(lam_re * step)
    ab_re = mag * jnp.cos(lam_im * step)
    ab_im = mag * jnp.sin(lam_im * step)
    den = lam_re * lam_re + lam_im * lam_im
    nr = ab_re - 1.0
    co_re = (nr * lam_re + ab_im * lam_im) / den
    co_im = (ab_im * lam_re - nr * lam_im) / den
    bb_re = co_re[..., None] * b_re - co_im[..., None] * b_im
    bb_im = co_re[..., None] * b_im + co_im[..., None] * b_re
    return (ab_re.reshape(1, -1), ab_im.reshape(1, -1), _blockdiag8(bb_re.transpose(0, 2, 1)),
            _blockdiag8(bb_im.transpose(0, 2, 1)))


def _scan_passes(x_s, a_ref, cr_ref, ci_ref, reverse, re, im):
    tc, nc = x_s.shape[:2]
    ln = SCAN_LANES
    n_sq = int(math.log2(tc))
    assert 2 ** n_sq == tc
    ar = jnp.broadcast_to(a_ref[:, re], (nc, ln))
    ai = jnp.broadcast_to(a_ref[:, im], (nc, ln))
    zero = jnp.zeros((nc, ln), F32)

    def at(t):
        return tc - 1 - t if reverse else t

    def local(t, carry):
        sr, si = carry
        j = at(t)
        nr = ar * sr - ai * si + x_s[j, :, re]
        ni = ar * si + ai * sr + x_s[j, :, im]
        x_s[j, :, re] = nr
        x_s[j, :, im] = ni
        return nr, ni

    lr, li = lax.fori_loop(0, tc, local, (zero, zero))
    pr, pi = a_ref[:, re], a_ref[:, im]
    for _ in range(n_sq):
        pr, pi = pr * pr - pi * pi, 2.0 * (pr * pi)
    cr_ref[...] = lr
    ci_ref[...] = li
    tr = jnp.zeros((1, ln), F32)
    ti = jnp.zeros((1, ln), F32)
    for c in (range(nc - 1, -1, -1) if reverse else range(nc)):
        l_r = cr_ref[c:c + 1, :]
        l_i = ci_ref[c:c + 1, :]
        cr_ref[c:c + 1, :] = tr
        ci_ref[c:c + 1, :] = ti
        tr, ti = pr * tr - pi * ti + l_r, pr * ti + pi * tr + l_i

    def second_pass(on_fixed):
        def fixup(t, carry):
            qr, qi, acc = carry
            j = at(t)
            qr, qi = ar * qr - ai * qi, ar * qi + ai * qr
            gr = x_s[j, :, re] + qr
            gi = x_s[j, :, im] + qi
            x_s[j, :, re] = gr
            x_s[j, :, im] = gi
            return qr, qi, on_fixed(j, gr, gi, acc)

        return fixup

    return second_pass, (cr_ref[...], ci_ref[...]), zero


def _s5_cat_tables(a_re, a_im, bb_re, bb_im, cc_re, cc_im):
    lb = S5_BLOCK_LANES
    n_blk, kq, nq = bb_re.shape
    sub = nq // lb

    def lanes(re, im):
        lead = re.shape[:-1]
        both = jnp.stack([re.reshape(lead + (-1, lb)), im.reshape(lead + (-1, lb))], axis=-2)
        return both.reshape(lead + (-1,))

    cc = jnp.stack([cc_re.reshape(n_blk, sub, lb, kq), -cc_im.reshape(n_blk, sub, lb, kq)], axis=2)
    return lanes(a_re, a_im), lanes(bb_re, bb_im), cc.reshape(n_blk, sub * 2 * lb, kq)


def _s5_uncat(t, axis):
    lb = S5_BLOCK_LANES
    shp = t.shape
    t = t.reshape(shp[:axis] + (-1, 2, lb) + shp[axis + 1:])
    re, im = jnp.take(t, 0, axis=axis + 1), jnp.take(t, 1, axis=axis + 1)
    return re.reshape(shp[:axis] + (-1,) + shp[axis + 1:]), im.reshape(shp[:axis] + (-1,) + shp[axis + 1:])


def _grid2_edges(n0, n1):
    return (lambda: (pl.program_id(0) == 0) & (pl.program_id(1) == 0),
            lambda: (pl.program_id(0) == n0 - 1) & (pl.program_id(1) == n1 - 1))


def _s5_scan_fwd(u5p, a_cat, bb_cat, cc_cat, nb, gather=()):
    n, dh = u5p.shape
    n_blk, kq, cols = bb_cat.shape
    lb, ln, nc = S5_BLOCK_LANES, SCAN_LANES, SCAN_CHUNKS
    sub = cols // (2 * lb)
    rows = n // nb
    tc = rows // nc
    n_l = n_blk * sub

    def body(u_ref, a_ref, b_ref, c_ref, s_ref, y_ref, x_s, car_r, car_i):
        x_s[...] = _dot(u_ref[...].astype(BF16), b_ref[0]).reshape(tc, nc, 2 * lb)
        for h in range(lb // ln):
            re, im = slice(h * ln, (h + 1) * ln), slice(lb + h * ln, lb + (h + 1) * ln)

            def keep(j, gr, gi, acc, re=re, im=im):
                s_ref[0, j, :, re] = gr.astype(BF16)
                s_ref[0, j, :, im] = gi.astype(BF16)
                return acc

            second_pass, start, zero = _scan_passes(x_s, a_ref, car_r, car_i, False, re, im)
            lax.fori_loop(0, tc, second_pass(keep), (*start, zero))
        y = _dot(x_s[...].reshape(rows, 2 * lb).astype(BF16), c_ref[0])

        @pl.when(pl.program_id(1) % sub == 0)
        def _():
            y_ref[...] = y

        @pl.when(pl.program_id(1) % sub != 0)
        def _():
            y_ref[...] += y

    row_blk = pl.BlockSpec((rows, kq), lambda b, l: (b, l // sub))
    ride = _Ride(gather, two_level=True)
    return pl.pallas_call(
        ride.wrap(body, 4, 2, 3, *_grid2_edges(nb, n_l)), grid=(nb, n_l),
        in_specs=[row_blk, pl.BlockSpec((1, 2 * lb), lambda b, l: (0, l)),
                  pl.BlockSpec((1, kq, 2 * lb), lambda b, l: (l // sub, 0, l % sub)),
                  pl.BlockSpec((1, 2 * lb, kq), lambda b, l: (l // sub, l % sub, 0))] + ride.in_specs,
        out_specs=[pl.BlockSpec((1, tc, nc, 2 * lb), lambda b, l: (b, 0, 0, l)), row_blk] + ride.out_specs,
        out_shape=[jax.ShapeDtypeStruct((nb, tc, nc, n_l * 2 * lb), BF16), jax.ShapeDtypeStruct((n, dh), F32)]
        + ride.out_shape,
        scratch_shapes=[pltpu.VMEM((tc, nc, 2 * lb), F32)] + [pltpu.VMEM((nc, ln), F32)] * 2 + ride.scratch,
        name="s5_scan_fwd", compiler_params=_params(2))(u5p, a_cat, bb_cat, cc_cat, *ride.arrays)


def _s5_scan_bwd(dy_pre, u5p, s_cat, a_cat_conj, bb_cat, cc_cat, d_row, gather=(), scatter=()):
    n, dh = u5p.shape
    n_blk, kq, cols = bb_cat.shape
    nb, tc, nc, _ = s_cat.shape
    lb, ln = S5_BLOCK_LANES, SCAN_LANES
    sub = cols // (2 * lb)
    rows = n // nb
    n_l = n_blk * sub

    def body(dy_ref, u_ref, s_ref, a_ref, b_ref, c_ref, d_ref, du_ref, db_ref, dc_ref, da_ref, x_s, car_r, car_i):
        dyb = dy_ref[...].astype(BF16)
        x_s[...] = _dot_nt(dyb, c_ref[0]).reshape(tc, nc, 2 * lb)
        for h in range(lb // ln):
            re, im = slice(h * ln, (h + 1) * ln), slice(lb + h * ln, lb + (h + 1) * ln)

            def with_state_before(j, gr, gi, acc, re=re, im=im):
                dr, di = acc
                pr_, pi_ = s_ref[0, j - 1, :, re].astype(F32), s_ref[0, j - 1, :, im].astype(F32)
                return dr + (pr_ * gr + pi_ * gi), di + (pr_ * gi - pi_ * gr)

            second_pass, start, zero = _scan_passes(x_s, a_ref, car_r, car_i, True, re, im)
            carry = lax.fori_loop(0, tc - 1, second_pass(with_state_before), (*start, (zero, zero)))
            row = lax.broadcasted_iota(jnp.int32, (nc, ln), 0)
            pr_ = jnp.where(row == 0, 0.0, pltpu.roll(s_ref[0, tc - 1, :, re].astype(F32), 1, 0))
            pi_ = jnp.where(row == 0, 0.0, pltpu.roll(s_ref[0, tc - 1, :, im].astype(F32), 1, 0))

            def at_first_time(j, gr, gi, acc, pr_=pr_, pi_=pi_):
                return acc[0] + (pr_ * gr + pi_ * gi), acc[1] + (pr_ * gi - pi_ * gr)

            _, _, (dr, di) = second_pass(at_first_time)(tc - 1, carry)
            da_ref[0, :, re] = jnp.sum(dr, axis=0, keepdims=True)
            da_ref[0, :, im] = jnp.sum(di, axis=0, keepdims=True)

        gb = x_s[...].reshape(rows, 2 * lb).astype(BF16)
        du = _dot_nt(gb, b_ref[0])

        @pl.when(pl.program_id(1) % sub == 0)
        def _():
            du_ref[...] = du + d_ref[...] * dy_ref[...]

        @pl.when(pl.program_id(1) % sub != 0)
        def _():
            du_ref[...] += du

        db_ref[0, 0] = _dot_tn(u_ref[...].astype(BF16), gb)
        dc_ref[0, 0] = _dot_tn(s_ref[0].reshape(rows, 2 * lb), dyb)

    row_blk = pl.BlockSpec((rows, kq), lambda b, l: (b, l // sub))
    in_map = pl.BlockSpec((1, kq, 2 * lb), lambda b, l: (l // sub, 0, l % sub))
    out_map = pl.BlockSpec((1, 2 * lb, kq), lambda b, l: (l // sub, l % sub, 0))
    ride = _Ride(gather, scatter)
    return pl.pallas_call(
        ride.wrap(body, 7, 4, 3, *_grid2_edges(nb, n_l)), grid=(nb, n_l),
        in_specs=[row_blk, row_blk, pl.BlockSpec((1, tc, nc, 2 * lb), lambda b, l: (b, 0, 0, l)),
                  pl.BlockSpec((1, 2 * lb), lambda b, l: (0, l)), in_map, out_map,
                  pl.BlockSpec((1, kq), lambda b, l: (0, l // sub))] + ride.in_specs,
        out_specs=[row_blk, pl.BlockSpec((1, 1, kq, 2 * lb), lambda b, l: (b, l // sub, 0, l % sub)),
                   pl.BlockSpec((1, 1, 2 * lb, kq), lambda b, l: (b, l // sub, l % sub, 0)),
                   pl.BlockSpec((1, 1, 2 * lb), lambda b, l: (b, 0, l))] + ride.out_specs,
        out_shape=[jax.ShapeDtypeStruct((n, dh), F32), jax.ShapeDtypeStruct((nb, n_blk, kq, cols), F32),
                   jax.ShapeDtypeStruct((nb, n_blk, cols, kq), F32), jax.ShapeDtypeStruct((nb, 1, n_l * 2 * lb), F32)]
        + ride.out_shape,
        scratch_shapes=[pltpu.VMEM((tc, nc, 2 * lb), F32)] + [pltpu.VMEM((nc, ln), F32)] * 2 + ride.scratch,
        name="s5_scan_bwd", compiler_params=_params(2))(dy_pre, u5p, s_cat, a_cat_conj, bb_cat, cc_cat, d_row,
                                                        *ride.arrays)


def kernel(x, mem, g_mix_pre, w_in, gm_ln_g, gm_ln_b, gm_w_s, gm_b_s, s5_lam_re, s5_lam_im, s5_log_step, s5_b_re, s5_b_im, s5_c_re, s5_c_im, s5_d, s5_w_glu, w_br_gm, w_br_s5, w_mix_out, g_mix_post, g_ca_pre, g_mem, ca_w_q, ca_w_kv, ca_w_o, g_ca_post, g_ffn_pre, ffn_w_gu, ffn_w_down, g_ffn_post, loss_target, m_g_mix_pre, m_w_in, m_gm_ln_g, m_gm_ln_b, m_gm_w_s, m_gm_b_s, m_s5_lam_re, m_s5_lam_im, m_s5_log_step, m_s5_b_re, m_s5_b_im, m_s5_c_re, m_s5_c_im, m_s5_d, m_s5_w_glu, m_w_br_gm, m_w_br_s5, m_w_mix_out, m_g_mix_post, m_g_ca_pre, m_g_mem, m_ca_w_q, m_ca_w_kv, m_ca_w_o, m_g_ca_post, m_g_ffn_pre, m_ffn_w_gu, m_ffn_w_down, m_g_ffn_post, v_g_mix_pre, v_w_in, v_gm_ln_g, v_gm_ln_b, v_gm_w_s, v_gm_b_s, v_s5_lam_re, v_s5_lam_im, v_s5_log_step, v_s5_b_re, v_s5_b_im, v_s5_c_re, v_s5_c_im, v_s5_d, v_s5_w_glu, v_w_br_gm, v_w_br_s5, v_w_mix_out, v_g_mix_post, v_g_ca_pre, v_g_mem, v_ca_w_q, v_ca_w_kv, v_ca_w_o, v_g_ca_post, v_g_ffn_pre, v_ffn_w_gu, v_ffn_w_down, v_g_ffn_post):
    args = locals()
    W = {n: args[n][0] for n in WEIGHTS}
    M = {n: args['m_' + n][0] for n in WEIGHTS}
    V = {n: args['v_' + n][0] for n in WEIGHTS}
    for k, kind in SHARDED.items():
        if kind == 'colT':
            W[k], M[k], V[k] = W[k].T, M[k].T, V[k].T

    nb, seq, d = x.shape
    n = nb * seq
    tm = min(ROW_TILE, n)
    tmw = min(WIDE_ROW_TILE, n)
    nmem = mem.shape[1]
    d2, dh = 2 * d, d // 2
    hd = d // CA_HEADS
    x2d = x.reshape(n, d)
    tgt = loss_target.reshape(n, d)
    mem2d = mem.reshape(nb * nmem, d)

    def row(v):
        return v.reshape(1, -1)

    names = list(SHARDED)
    later = [k for k in names if k != 'w_in']
    first_g, *casts = _first_gather(W['w_in'], [W[k] for k in later])
    shard_b = dict(zip(later, casts))
    half = shard_b['ffn_w_gu'].shape[0] // 2
    gu_halves = [shard_b['ffn_w_gu'][:half], shard_b['ffn_w_gu'][half:]]

    def assemble(k, gth):
        r, c = gth.shape[1:]
        return gth.transpose(1, 0, 2).reshape(r, N_DEV * c) if SHARDED[k] == 'col' else gth.reshape(N_DEV * r, c)

    w_in_t = assemble('w_in', first_g)
    ffh = N_DEV * W['ffn_w_down'].shape[0]

    (a_re, a_im, bb_re, bb_im), s5_in_vjp = jax.vjp(
        _s5_in_tables, W['s5_lam_re'], W['s5_lam_im'], W['s5_log_step'], W['s5_b_re'], W['s5_b_im'])
    (cc_re, cc_im), s5_out_vjp = jax.vjp(_s5_out_tables, W['s5_c_re'], W['s5_c_im'])
    bb_re_b, bb_im_b, cc_re_b, cc_im_b = (t.astype(BF16) for t in (bb_re, bb_im, cc_re, cc_im))
    s5_d_row = row(W['s5_d'])
    tril = jnp.tril(jnp.ones((GM_CHUNK, GM_CHUNK), bool))
    w_s = jnp.where(tril[None], W['gm_w_s'], 0.0).astype(BF16)
    w_s_t = w_s.transpose(0, 2, 1)
    gm_bias = jnp.repeat(W['gm_b_s'].T, d // GM_GROUPS, axis=1)

    def in_proj_body(x_ref, g_ref, w_ref, zgm_ref, u5_ref, zga_ref, zgb_ref, h_ref):
        hb = _rms(x_ref[...], g_ref[...]).astype(BF16)
        h_ref[...] = hb
        for lo in range(0, w_ref.shape[0], 512):
            acc = _dot_nt(hb, w_ref[lo:lo + 512, :])
            if lo < d2:
                zgm_ref[:, lo:lo + 512] = acc.astype(BF16)
            elif lo < d2 + dh:
                u5_ref[...] = acc
            elif lo < d2 + dh + d:
                zga_ref[:, lo - d2 - dh:lo - d2 - dh + 512] = acc.astype(BF16)
            else:
                zgb_ref[:, lo - d2 - dh - d:lo - d2 - dh - d + 512] = acc.astype(BF16)

    ride = ['s5_w_glu', 'w_br_gm', 'w_br_s5']
    z_gm, u5, z_ga, z_gb, h0, *got = _rowcall(
        "in_proj", in_proj_body, n, tm, [(x2d, 'row'), (row(W['g_mix_pre']), 'full'), (w_in_t, 'full')],
        [(d2, BF16), (dh, F32), (d, BF16), (d, BF16), (d, BF16)], gather=[shard_b[k] for k in ride])
    w_glu_f, w_brgm_f, w_brs5_f = (assemble(k, g) for k, g in zip(ride, got))

    gw = d // GM_GROUPS

    def gm_fwd_body(z_ref, lng_ref, lnb_ref, ws_ref, bias_ref, y_ref, u_ref, vh_ref, dgelu_ref, rstd_ref):
        zg, dgelu = _gelu_and_grad(z_ref[...].astype(F32))
        dgelu_ref[...] = dgelu.astype(BF16)
        u, v = zg[:, :d], zg[:, d:]
        vc = v - jnp.mean(v, axis=-1, keepdims=True)
        rstd = lax.rsqrt(jnp.mean(vc * vc, axis=-1, keepdims=True) + EPS)
        vhat = vc * rstd
        u_ref[...] = u.astype(BF16)
        vh_ref[...] = vhat.astype(BF16)
        rstd_ref[...] = jnp.broadcast_to(rstd, rstd_ref.shape)
        vnb = (vhat * lng_ref[...] + lnb_ref[...]).astype(BF16)
        sv = jnp.concatenate([_dot(ws_ref[g], vnb[:, g * gw:(g + 1) * gw]) for g in range(GM_GROUPS)], axis=1)
        y_ref[...] = (u * (sv + bias_ref[...])).astype(BF16)

    y_gm, gm_u, gm_vhat, gm_dgelu, gm_rstd, g_mix, g_q = _rowcall(
        "gmlp_fwd", gm_fwd_body, n, GM_CHUNK,
        [(z_gm, 'row'), (row(W['gm_ln_g']), 'full'), (row(W['gm_ln_b']), 'full'), (w_s, 'full'), (gm_bias, 'full')],
        [(d, BF16), (d, BF16), (d, BF16), (d2, BF16), (128, F32)], gather=[shard_b['w_mix_out'], shard_b['ca_w_q']],
        two_level=True)
    w_mix_f, w_q_f = assemble('w_mix_out', g_mix), assemble('ca_w_q', g_q)

    tc = seq // SCAN_CHUNKS
    lt = S5_GROUPS * S5_STATE

    def to_scan_order(t):
        return t.reshape(nb, SCAN_CHUNKS, tc, t.shape[-1]).transpose(0, 2, 1, 3).reshape(n, t.shape[-1])

    def from_scan_order(t):
        return t.reshape(nb, tc, SCAN_CHUNKS, t.shape[-1]).transpose(0, 2, 1, 3).reshape(n, t.shape[-1])

    u5p = to_scan_order(u5)
    a_cat, bb_cat, cc_cat = _s5_cat_tables(a_re, a_im, bb_re_b, bb_im_b, cc_re_b, cc_im_b)
    a_cat_conj = _s5_cat_tables(a_re, -a_im, bb_re_b, bb_im_b, cc_re_b, cc_im_b)[0]
    s_cat, y_lin, g_kv, g_o = _s5_scan_fwd(u5p, a_cat, bb_cat, cc_cat, nb,
                                           gather=[shard_b['ca_w_kv'], shard_b['ca_w_o']])
    w_kv_f, w_o_f = assemble('ca_w_kv', g_kv), assemble('ca_w_o', g_o)

    def s5_out_body(yl_ref, u_ref, d_ref, wg_ref, ypre_ref, gate_ref, y_ref):
        ypre = yl_ref[...] + d_ref[...] * u_ref[...]
        ypre_ref[...] = ypre
        yg = _gelu(ypre)
        gate = _dot(yg.astype(BF16), wg_ref[...])
        gate_ref[...] = gate
        y_ref[...] = (yg * _sigmoid(gate)).astype(BF16)

    tm5 = min(1024, n)
    y_pre, gate, y_s5p = _rowcall(
        "s5_out", s5_out_body, n, tm5, [(y_lin, 'row'), (u5p, 'row'), (s5_d_row, 'full'), (w_glu_f, 'full')],
        [(dh, F32), (dh, F32), (dh, BF16)])
    y_s5 = from_scan_order(y_s5p)

    def merge_body(ygm_ref, ys5_ref, zga_ref, zgb_ref, wa_ref, wb_ref, pa_ref, pb_ref, mg_ref):
        pa = _dot(ygm_ref[...], wa_ref[...])
        pb = _dot(ys5_ref[...], wb_ref[...])
        pa_ref[...] = pa.astype(BF16)
        pb_ref[...] = pb.astype(BF16)
        mg_ref[...] = (_sigmoid(zga_ref[...].astype(F32)) * pa + _sigmoid(zgb_ref[...].astype(F32)) * pb).astype(BF16)

    p_a, p_b, merged, g_down = _rowcall(
        "merge", merge_body, n, tmw,
        [(y_gm, 'row'), (y_s5, 'row'), (z_ga, 'row'), (z_gb, 'row'), (w_brgm_f, 'full'), (w_brs5_f, 'full')],
        [(d, BF16), (d, BF16), (d, BF16)], gather=[shard_b['ffn_w_down']], two_level=True)
    w_down_f = assemble('ffn_w_down', g_down)

    def close_sublayer(name, a_in, w_out, x_res, g_post, g_next, w_next=None, gather=(), two_level=False):
        def body(*refs):
            a_ref, w_ref, x_ref, gp_ref, gn_ref = refs[:5]
            rest = refs[5:]
            if w_next is not None:
                wn_ref, rest = rest[0], rest[1:]
            o_ref, xo_ref, h_ref = rest[:3]
            o = _dot(a_ref[...], w_ref[...])
            o_ref[...] = o.astype(BF16)
            xo = x_ref[...] + _rms(o, gp_ref[...])
            xo_ref[...] = xo
            hb = _rms(xo, gn_ref[...]).astype(BF16)
            h_ref[...] = hb
            if w_next is not None:
                rest[3][...] = _dot(hb, wn_ref[...]).astype(BF16)

        ins = [(a_in, 'row'), (w_out, 'full'), (x_res, 'row'), (row(g_post), 'full'), (row(g_next), 'full')]
        outs = [(d, BF16), (d, F32), (d, BF16)]
        if w_next is not None:
            ins.append((w_next, 'full'))
            outs.append((w_next.shape[1], BF16))
        return _rowcall(name, body, n, tmw, ins, outs, gather=gather, two_level=two_level)

    o1, x1, hc, q, g_gu0 = close_sublayer("mix_out", merged, w_mix_f, x2d, W['g_mix_post'], W['g_ca_pre'], w_q_f,
                                          gather=[gu_halves[0]])

    tmm = min(ROW_TILE, nb * nmem)

    def memkv_body(m_ref, g_ref, w_ref, mn_ref, k_ref, v_ref):
        mnb = _rms(m_ref[...], g_ref[...]).astype(BF16)
        mn_ref[...] = mnb
        k_ref[...] = _dot(mnb, w_ref[:, :d]).astype(BF16)
        v_ref[...] = _dot(mnb, w_ref[:, d:]).astype(BF16)

    mem_n, k_mem, v_mem = _rowcall("mem_kv", memkv_body, nb * nmem, tmm,
                                   [(mem2d, 'row'), (row(W['g_mem']), 'full'), (w_kv_f, 'full')],
                                   [(d, BF16), (d, BF16), (d, BF16)])

    tiles_per_ex = seq // tm
    kv_spec = pl.BlockSpec((nmem, d), lambda i: (i // tiles_per_ex, 0))
    scale = hd ** -0.5

    def softmax_rows(qh, kh):
        s = _dot_nt(qh, kh) * scale
        e = jnp.exp(s - jnp.max(s, axis=-1, keepdims=True))
        return e / jnp.sum(e, axis=-1, keepdims=True)

    def attn_body(q_ref, k_ref, v_ref, o_ref):
        for h in range(CA_HEADS):
            sl = slice(h * hd, (h + 1) * hd)
            p = softmax_rows(q_ref[:, sl], k_ref[:, sl])
            o_ref[:, sl] = _dot(p.astype(BF16), v_ref[:, sl]).astype(BF16)

    (att,) = _rowcall("attn_fwd", attn_body, n, tm, [(q, 'row'), (k_mem, kv_spec), (v_mem, kv_spec)], [(d, BF16)])

    o2, x2, hf, g_gu1 = close_sublayer("attn_out", att, w_o_f, x1, W['g_ca_post'], W['g_ffn_pre'],
                                       gather=[gu_halves[1]], two_level=True)
    w_gu_t = jnp.stack([g_gu0, g_gu1], axis=1).reshape(2 * ffh, d)

    ck = 256

    def ffn_up_body(h_ref, w_ref, gu_ref, a_ref):
        hb = h_ref[...]
        for lo in range(0, ffh, ck):
            gt = _dot_nt(hb, w_ref[lo:lo + ck, :])
            ut = _dot_nt(hb, w_ref[ffh + lo:ffh + lo + ck, :])
            gu_ref[:, lo:lo + ck] = gt.astype(BF16)
            gu_ref[:, ffh + lo:ffh + lo + ck] = ut.astype(BF16)
            a_ref[:, lo:lo + ck] = ((gt * _sigmoid(gt)) * ut).astype(BF16)

    gu, act = _rowcall("ffn_up", ffn_up_body, n, tm, [(hf, 'row'), (w_gu_t, 'full')], [(2 * ffh, BF16), (ffh, BF16)])

    def ffn_down_body(a_ref, w_ref, x_ref, t_ref, g_ref, dx_ref, do_ref, loss_ref, dg_ref):
        i = pl.program_id(0)

        @pl.when(i == 0)
        def _():
            loss_ref[...] = jnp.zeros_like(loss_ref)
            dg_ref[...] = jnp.zeros_like(dg_ref)

        o = _dot(a_ref[...], w_ref[...])
        diff = x_ref[...] + _rms(o, g_ref[...]) - t_ref[...]
        loss_ref[...] += jnp.full(loss_ref.shape, 0.5 / d, F32) * jnp.sum(diff * diff)
        dx = diff * (1.0 / d)
        dx_ref[...] = dx
        do, dg = _rms_bwd(o, g_ref[...], dx)
        do_ref[...] = do.astype(BF16)
        dg_ref[...] += dg

    dx3, do3, loss_part, dg_ffn_post = _rowcall(
        "ffn_down_loss", ffn_down_body, n, tmw,
        [(act, 'row'), (w_down_f, 'full'), (x2, 'row'), (tgt, 'row'), (row(W['g_ffn_post']), 'full')],
        [(d, F32), (d, BF16)], accs=[((1, 128), F32), ((1, d), F32)])

    G = {'g_ffn_post': dg_ffn_post}
    RECV = {}

    def parts_of(k, gfull):
        r, c = W[k].shape
        return gfull.reshape(r, N_DEV, c).transpose(1, 0, 2) if SHARDED[k] == 'col' else gfull.reshape(N_DEV, r, c)

    p_down = parts_of('ffn_w_down', _dwcall("dw_ffn_down", act, do3, BF16))

    def ffn_act_bwd_body(do_ref, w_ref, gu_ref, dgu_ref):
        dob = do_ref[...]
        for lo in range(0, ffh, ck):
            da = _dot_nt(dob, w_ref[lo:lo + ck, :])
            gt = gu_ref[:, lo:lo + ck].astype(F32)
            ut = gu_ref[:, ffh + lo:ffh + lo + ck].astype(F32)
            sg = _sigmoid(gt)
            dgu_ref[:, lo:lo + ck] = (da * ut * (sg * (1.0 + gt * (1.0 - sg)))).astype(BF16)
            dgu_ref[:, ffh + lo:ffh + lo + ck] = (da * (gt * sg)).astype(BF16)

    dgu, RECV['ffn_w_down'] = _rowcall("ffn_act_bwd", ffn_act_bwd_body, n, tmw,
                                       [(do3, 'row'), (w_down_f, 'full'), (gu, 'row')], [(2 * ffh, BF16)],
                                       scatter=[p_down])
    p_gu = parts_of('ffn_w_gu', _dwcall("dw_ffn_gu", dgu, hf, BF16))

    def open_sublayer(name, pieces, w_full, x_in, g_pre, dx_up, o_prev=None, g_post_prev=None, scatter=(), gather=(),
                      w_t=False):
        n_p = len(pieces)
        second = o_prev is not None

        def body(*refs):
            dc_refs, (w_ref, x_ref, g_ref, dxu_ref), rest = refs[:n_p], refs[n_p:n_p + 4], refs[n_p + 4:]
            if second:
                (op_ref, gp_ref), rest = rest[:2], rest[2:]
            i = pl.program_id(0)
            dhid = None
            for dc_ref, (_, lo, hi) in zip(dc_refs, pieces):
                part = _dot(dc_ref[...], w_ref[lo:hi, :]) if w_t else _dot_nt(dc_ref[...], w_ref[:, lo:hi])
                dhid = part if dhid is None else dhid + part
            dxn, dg = _rms_bwd(x_ref[...], g_ref[...], dhid)
            dx = dxu_ref[...] + dxn
            if second:
                dx_ref, do_ref, dg_ref, dg2_ref = rest
            else:
                dx_ref, dg_ref = rest

            @pl.when(i == 0)
            def _():
                dg_ref[...] = jnp.zeros_like(dg_ref)
                if second:
                    dg2_ref[...] = jnp.zeros_like(dg2_ref)

            dx_ref[...] = dx
            dg_ref[...] += dg
            if second:
                do, dg2 = _rms_bwd(op_ref[...].astype(F32), gp_ref[...], dx)
                do_ref[...] = do.astype(BF16)
                dg2_ref[...] += dg2

        ins = [(p[0], 'row') for p in pieces] + [(w_full, 'full'), (x_in, 'row'), (row(g_pre), 'full'), (dx_up, 'row')]
        outs = [(d, F32)]
        accs = [((1, d), F32)]
        if second:
            ins += [(o_prev, 'row'), (row(g_post_prev), 'full')]
            outs.append((d, BF16))
            accs.append(((1, d), F32))
        light = w_full.size <= d * d
        res = _rowcall(name, body, n, tmw if light else tm, ins, outs, accs=accs, scatter=scatter, gather=gather)
        if second:
            dx, do, dg, dg2 = res[:4]
            return dx, dg, do, dg2, res[4:]
        return res[0], res[1], res[2:]

    dx2, G['g_ffn_pre'], do2, G['g_ca_post'], (RECV['ffn_w_gu'],) = open_sublayer(
        "ffn_in_bwd", [(dgu, 0, 2 * ffh)], w_gu_t, x2, W['g_ffn_pre'], dx3, o2, W['g_ca_post'], scatter=[p_gu],
        w_t=True)
    p_o = parts_of('ca_w_o', _dwcall("dw_ca_o", att, do2, BF16))

    def attn_bwd_body(q_ref, k_ref, v_ref, do_ref, wo_ref, dq_ref, dk_ref, dv_ref):
        i = pl.program_id(0)

        @pl.when(i % tiles_per_ex == 0)
        def _():
            dk_ref[...] = jnp.zeros_like(dk_ref)
            dv_ref[...] = jnp.zeros_like(dv_ref)

        d_att = _dot_nt(do_ref[...], wo_ref[...]).astype(BF16)
        for h in range(CA_HEADS):
            sl = slice(h * hd, (h + 1) * hd)
            qh, kh, vh, dah = q_ref[:, sl], k_ref[:, sl], v_ref[:, sl], d_att[:, sl]
            s_t = _dot_nt(kh, qh) * scale
            e_t = jnp.exp(s_t - jnp.max(s_t, axis=0, keepdims=True))
            p_t = e_t / jnp.sum(e_t, axis=0, keepdims=True)
            dp_t = _dot_nt(vh, dah)
            ds_t = (p_t * (dp_t - jnp.sum(p_t * dp_t, axis=0, keepdims=True)) * scale).astype(BF16)
            dq_ref[:, sl] = _dot_tn(ds_t, kh).astype(BF16)
            dk_ref[:, sl] += _dot(ds_t, qh)
            dv_ref[:, sl] += _dot(p_t.astype(BF16), dah)

    kv_acc = ((nb * nmem, d), (F32, pl.BlockSpec((nmem, d), lambda i: (i // tiles_per_ex, 0))))
    dq, dk_mem, dv_mem, RECV['ca_w_o'] = _rowcall(
        "attn_bwd", attn_bwd_body, n, tm,
        [(q, 'row'), (k_mem, kv_spec), (v_mem, kv_spec), (do2, 'row'), (w_o_f, 'full')], [(d, BF16)],
        accs=[kv_acc, kv_acc], scatter=[p_o])
    p_q = parts_of('ca_w_q', _dwcall("dw_ca_q", hc, dq, BF16))

    def memkv_bwd_body(dk_ref, dv_ref, m_ref, g_ref, w_ref, dkv_ref, dg_ref):
        i = pl.program_id(0)

        @pl.when(i == 0)
        def _():
            dg_ref[...] = jnp.zeros_like(dg_ref)

        dkb, dvb = dk_ref[...].astype(BF16), dv_ref[...].astype(BF16)
        dkv_ref[:, :d] = dkb
        dkv_ref[:, d:] = dvb
        dmn = _dot_nt(dkb, w_ref[:, :d]) + _dot_nt(dvb, w_ref[:, d:])
        _, dg = _rms_bwd(m_ref[...], g_ref[...], dmn)
        dg_ref[...] += dg

    dkv, G['g_mem'] = _rowcall(
        "mem_kv_bwd", memkv_bwd_body, nb * nmem, tmm,
        [(dk_mem, 'row'), (dv_mem, 'row'), (mem2d, 'row'), (row(W['g_mem']), 'full'), (w_kv_f, 'full')],
        [(d2, BF16)], accs=[((1, d), F32)])
    p_kv = parts_of('ca_w_kv', _dwcall("dw_ca_kv", mem_n, dkv, BF16))

    dx1, G['g_ca_pre'], do1, G['g_mix_post'], _ = open_sublayer(
        "attn_in_bwd", [(dq, 0, d)], w_q_f, x1, W['g_ca_pre'], dx2, o1, W['g_mix_post'])
    p_mix = parts_of('w_mix_out', _dwcall("dw_mix_out", merged, do1, BF16))

    def merge_bwd_body(do_ref, wm_ref, zga_ref, zgb_ref, pa_ref, pb_ref, wb_ref, dpa_ref, dpb_ref, dza_ref, dzb_ref,
                       dys_ref):
        dm = _dot_nt(do_ref[...], wm_ref[...])
        sa, sb = _sigmoid(zga_ref[...].astype(F32)), _sigmoid(zgb_ref[...].astype(F32))
        dpb = (dm * sb).astype(BF16)
        dpa_ref[...] = (dm * sa).astype(BF16)
        dpb_ref[...] = dpb
        dza_ref[...] = (dm * pa_ref[...].astype(F32) * (sa * (1.0 - sa))).astype(BF16)
        dzb_ref[...] = (dm * pb_ref[...].astype(F32) * (sb * (1.0 - sb))).astype(BF16)
        dys_ref[...] = _dot_nt(dpb, wb_ref[...]).astype(BF16)

    NORM_ROWS = ['g_ffn_post', 'g_ffn_pre', 'g_ca_post', 'g_mem', 'g_ca_pre', 'g_mix_post']
    norm_rows = jnp.concatenate([G[k] for k in NORM_ROWS] + [jnp.tile(loss_part, (1, d // 128))], axis=0)
    dp_a, dp_b, dz_ga, dz_gb, dy_s5, all_norm_rows, RECV['w_mix_out'] = _rowcall(
        "merge_bwd", merge_bwd_body, n, tmw,
        [(do1, 'row'), (w_mix_f, 'full'), (z_ga, 'row'), (z_gb, 'row'), (p_a, 'row'), (p_b, 'row'), (w_brs5_f, 'full')],
        [(d, BF16), (d, BF16), (d, BF16), (d, BF16), (dh, BF16)], scatter=[p_mix], gather=[norm_rows])
    p_brgm = parts_of('w_br_gm', _dwcall("dw_br_gm", y_gm, dp_a, BF16))
    p_brs5 = parts_of('w_br_s5', _dwcall("dw_br_s5", y_s5, dp_b, BF16))

    def gm_bwd_body(u_ref, vh_ref, dgelu_ref, rstd_ref, dpa_ref, wa_ref, lng_ref, lnb_ref, ws_ref, wst_ref, bias_ref,
                    dz_ref, dws_ref, dbias_ref, dlng_ref, dlnb_ref):
        i = pl.program_id(0)

        @pl.when(i == 0)
        def _():
            dws_ref[...] = jnp.zeros_like(dws_ref)
            dbias_ref[...] = jnp.zeros_like(dbias_ref)
            dlng_ref[...] = jnp.zeros_like(dlng_ref)
            dlnb_ref[...] = jnp.zeros_like(dlnb_ref)

        u, vhat, rstd = u_ref[...].astype(F32), vh_ref[...].astype(F32), rstd_ref[:, :1]
        vnb = (vhat * lng_ref[...] + lnb_ref[...]).astype(BF16)
        dy = _dot_nt(dpa_ref[...], wa_ref[...])
        groups = [slice(g * gw, (g + 1) * gw) for g in range(GM_GROUPS)]
        dsv = dy * u
        dsvb = dsv.astype(BF16)
        dbias_ref[...] += dsv
        sv = jnp.concatenate([_dot(ws_ref[g], vnb[:, sl]) for g, sl in enumerate(groups)], axis=1) + bias_ref[...]
        dvn = jnp.concatenate([_dot(wst_ref[g], dsvb[:, sl]) for g, sl in enumerate(groups)], axis=1)
        for g, sl in enumerate(groups):
            dws_ref[g] += _dot_nt(dsvb[:, sl], vnb[:, sl])
        du = dy * sv
        dlng_ref[...] += jnp.sum(dvn * vhat, axis=0, keepdims=True)
        dlnb_ref[...] += jnp.sum(dvn, axis=0, keepdims=True)
        dvh = dvn * lng_ref[...]
        dv = rstd * (dvh - jnp.mean(dvh, axis=-1, keepdims=True) - vhat * jnp.mean(dvh * vhat, axis=-1, keepdims=True))
        dz_ref[:, :d] = (du * dgelu_ref[:, :d].astype(F32)).astype(BF16)
        dz_ref[:, d:] = (dv * dgelu_ref[:, d:].astype(F32)).astype(BF16)

    dz_gm, dws_full, dbias_full, G['gm_ln_g'], G['gm_ln_b'], RECV['w_br_gm'], RECV['w_br_s5'] = _rowcall(
        "gmlp_bwd", gm_bwd_body, n, GM_CHUNK,
        [(gm_u, 'row'), (gm_vhat, 'row'), (gm_dgelu, 'row'), (gm_rstd, 'row'), (dp_a, 'row'), (w_brgm_f, 'full'),
         (row(W['gm_ln_g']), 'full'), (row(W['gm_ln_b']), 'full'), (w_s, 'full'), (w_s_t, 'full'), (gm_bias, 'full')],
        [(d2, BF16)], accs=[((GM_GROUPS, GM_CHUNK, GM_CHUNK), F32), ((GM_CHUNK, d), F32), ((1, d), F32), ((1, d), F32)],
        scatter=[p_brgm, p_brs5])
    G['gm_w_s'] = jnp.where(tril[None], dws_full, 0.0)
    G['gm_b_s'] = dbias_full.reshape(GM_CHUNK, GM_GROUPS, gw).sum(-1).T

    dy_s5p = to_scan_order(dy_s5)

    def s5_out_bwd_body(dy_ref, ypre_ref, gate_ref, u_ref, wg_ref, dyp_ref, dgate_ref, yg_ref, dd_ref):
        i = pl.program_id(0)

        @pl.when(i == 0)
        def _():
            dd_ref[...] = jnp.zeros_like(dd_ref)

        dy = dy_ref[...].astype(F32)
        yg, dgelu = _gelu_and_grad(ypre_ref[...])
        sg = _sigmoid(gate_ref[...])
        dgb = (dy * yg * (sg * (1.0 - sg))).astype(BF16)
        dgate_ref[...] = dgb
        yg_ref[...] = yg.astype(BF16)
        dyp = (dy * sg + _dot_nt(dgb, wg_ref[...])) * dgelu
        dyp_ref[...] = dyp
        dd_ref[...] += jnp.sum(dyp * u_ref[...], axis=0, keepdims=True)

    dy_pre, dgate, yg_b, dd = _rowcall(
        "s5_out_bwd", s5_out_bwd_body, n, tm5,
        [(dy_s5p, 'row'), (y_pre, 'row'), (gate, 'row'), (u5p, 'row'), (w_glu_f, 'full')],
        [(dh, F32), (dh, BF16), (dh, BF16)], accs=[((1, dh), F32)])
    gm_ln_rows = jnp.concatenate([G['gm_ln_g'], G['gm_ln_b']], axis=0)
    (du5p, d_bb_cat, d_cc_cat, da_cat, all_gm_w_s, all_gm_b_s, all_gm_ln, RECV['ca_w_q'],
     RECV['ca_w_kv']) = _s5_scan_bwd(dy_pre, u5p, s_cat, a_cat_conj, bb_cat, cc_cat, s5_d_row,
                                     gather=[G['gm_w_s'], G['gm_b_s'], gm_ln_rows], scatter=[p_q, p_kv])
    du5 = from_scan_order(du5p.astype(BF16))
    d_bb_re, d_bb_im = _s5_uncat(jnp.sum(d_bb_cat, axis=0), 2)
    d_cc_re, d_cc_neg_im = _s5_uncat(jnp.sum(d_cc_cat, axis=0), 1)
    da_re, da_im = _s5_uncat(jnp.sum(da_cat, axis=0), 1)
    lane_shape = (S5_GROUPS, 8, 128)
    g_c_re, g_c_im = (t.reshape(lane_shape) for t in s5_out_vjp((d_cc_re, -d_cc_neg_im)))
    g_d = dd.reshape(S5_GROUPS, S5_CH)
    g_lam_re, g_lam_im, g_log_step, g_b_re, g_b_im = s5_in_vjp((da_re, da_im, d_bb_re, d_bb_im))
    g_log_step, g_b_re, g_b_im = g_log_step.reshape(1, -1), g_b_re.reshape(lane_shape), g_b_im.reshape(lane_shape)
    p_glu = parts_of('s5_w_glu', _dwcall("dw_s5_glu", yg_b, dgate, BF16))

    pieces = [(dz_gm, 0, d2), (du5, d2, d2 + dh), (dz_ga, d2 + dh, d2 + dh + d), (dz_gb, d2 + dh + d, d2 + dh + 2 * d)]
    dw_gm, all_lam_re, all_lam_im, all_log_step, all_b_re, all_b_im = _dwcall(
        "dw_in_0", dz_gm, h0, BF16, gather=[g_lam_re, g_lam_im, g_log_step, g_b_re, g_b_im])
    dw_s5 = _dwcall("dw_in_1", du5, h0, BF16)
    dw_ga = _dwcall("dw_in_2", dz_ga, h0, BF16)
    dw_gb, all_c_re, all_c_im, all_d, RECV['s5_w_glu'] = _dwcall(
        "dw_in_3", dz_gb, h0, BF16, gather=[g_c_re, g_c_im, g_d], scatter=[p_glu])
    p_in = parts_of('w_in', jnp.concatenate([dw_gm, dw_s5, dw_ga, dw_gb], axis=0))
    grad_x2d, g_mix_pre_part, (RECV['w_in'],) = open_sublayer(
        "in_proj_bwd", pieces, w_in_t, x2d, W['g_mix_pre'], dx1, scatter=[p_in], w_t=True)

    out = {}
    for k in names:
        quad = _sum_adamw("sum_adamw_" + k, RECV[k], W[k], M[k], V[k])
        out[k] = [(t.T if SHARDED[k] == 'colT' else t)[None] for t in quad]
    gathered = [all_norm_rows, all_gm_w_s, all_gm_b_s, all_gm_ln, all_c_re, all_c_im, all_d, all_lam_re, all_lam_im,
                all_log_step, all_b_re, all_b_im, _gather_last(g_mix_pre_part)]
    where = {'g_ffn_post': (0, 0), 'g_ffn_pre': (0, 1), 'g_ca_post': (0, 2), 'g_mem': (0, 3), 'g_ca_pre': (0, 4),
             'g_mix_post': (0, 5), 'gm_w_s': (1, None), 'gm_b_s': (2, None), 'gm_ln_g': (3, 0), 'gm_ln_b': (3, 1),
             's5_c_re': (4, None), 's5_c_im': (5, None), 's5_d': (6, None), 's5_lam_re': (7, None),
             's5_lam_im': (8, None), 's5_log_step': (9, 0), 's5_b_re': (10, None), 's5_b_im': (11, None),
             'g_mix_pre': (12, 0)}
    folded = ('s5_b_re', 's5_b_im', 's5_c_re', 's5_c_im')

    def as_updated(k, t):
        return t.reshape((1,) + lane_shape) if k in folded else t

    small, loss_row = _small_adamw(
        [(as_updated(k, args[k]), as_updated(k, args['m_' + k]), as_updated(k, args['v_' + k])) + where[k]
         for k in SMALL], gathered, (0, len(NORM_ROWS)))
    out.update({k: [t.reshape(args[k].shape) for t in quad] for k, quad in zip(SMALL, small)})

    res = [loss_row[0, 0], grad_x2d.reshape(x.shape)]
    for j in range(4):
        res += [out[k][j] for k in WEIGHTS]
    return tuple(res)
```

```python
import functools
import math

import jax
import jax.numpy as jnp
from jax import lax
from jax.experimental import pallas as pl
from jax.experimental.pallas import tpu as pltpu

F32 = jnp.float32
BF16 = jnp.bfloat16
EPS = 1e-6
N_DEV = 8
V7X_VMEM_LIMIT = 56 * 1024 * 1024
ROW_TILE = 256
WIDE_ROW_TILE = 1024
FFN_ROW_TILE = 512
GM_CHUNK = 128
GM_GROUPS = 8
S5_GROUPS = 32
S5_STATE = 64
S5_CH = 16
SCAN_CHUNKS = 32
SCAN_LANES = 128
S5_BLOCK_LANES = 256
CA_HEADS = 4
ADAM_LR, ADAM_B1, ADAM_B2, ADAM_EPS, ADAM_WD, ADAM_STEP = 0.001, 0.9, 0.999, 1e-08, 0.01, 10

WEIGHTS = ['g_mix_pre', 'w_in', 'gm_ln_g', 'gm_ln_b', 'gm_w_s', 'gm_b_s', 's5_lam_re', 's5_lam_im', 's5_log_step',
           's5_b_re', 's5_b_im', 's5_c_re', 's5_c_im', 's5_d', 's5_w_glu', 'w_br_gm', 'w_br_s5', 'w_mix_out',
           'g_mix_post', 'g_ca_pre', 'g_mem', 'ca_w_q', 'ca_w_kv', 'ca_w_o', 'g_ca_post', 'g_ffn_pre', 'ffn_w_gu',
           'ffn_w_down', 'g_ffn_post']
SHARDED = {'w_in': 'colT', 's5_w_glu': 'row', 'w_br_gm': 'row', 'w_br_s5': 'col', 'w_mix_out': 'row',
           'ca_w_q': 'row', 'ca_w_kv': 'col', 'ca_w_o': 'row', 'ffn_w_gu': 'colT', 'ffn_w_down': 'row'}
SMALL = [n for n in WEIGHTS if n not in SHARDED]


def _rms(x, g):
    r = lax.rsqrt(jnp.mean(x * x, axis=-1, keepdims=True) + EPS)
    return (x * r) * g


def _rms_bwd(x, g, dy):
    r = lax.rsqrt(jnp.mean(x * x, axis=-1, keepdims=True) + EPS)
    n = x * r
    dn = dy * g
    dx = r * (dn - n * jnp.mean(dn * n, axis=-1, keepdims=True))
    return dx, jnp.sum(dy * n, axis=0, keepdims=True)


_GELU_C = math.sqrt(2.0 / math.pi)


def _gelu(x):
    return 0.5 * x * (1.0 + jnp.tanh(_GELU_C * (x + 0.044715 * (x * x * x))))


def _gelu_and_grad(x):
    x2 = x * x
    t = jnp.tanh(_GELU_C * (x + 0.044715 * (x2 * x)))
    h = 0.5 * (1.0 + t)
    return x * h, h + 0.5 * x * (1.0 - t * t) * (_GELU_C * (1.0 + 3.0 * 0.044715 * x2))


def _sigmoid(x):
    return 0.5 * (1.0 + jnp.tanh(0.5 * x))


def _dot(a, b):
    return jnp.dot(a, b, preferred_element_type=F32)


def _dot_nt(a, b):
    return lax.dot_general(a, b, (((1,), (1,)), ((), ())), preferred_element_type=F32)


def _dot_tn(a, b):
    return lax.dot_general(a, b, (((0,), (0,)), ((), ())), preferred_element_type=F32)


def _adamw(w, g, m, v):
    m = ADAM_B1 * m + (1.0 - ADAM_B1) * g
    v = ADAM_B2 * v + (1.0 - ADAM_B2) * (g * g)
    m_hat = m / (1.0 - ADAM_B1 ** ADAM_STEP)
    v_hat = v / (1.0 - ADAM_B2 ** ADAM_STEP)
    delta = -ADAM_LR * (m_hat / (jnp.sqrt(v_hat) + ADAM_EPS) + ADAM_WD * w)
    return delta, m, v


def _params(n_grid):
    return pltpu.CompilerParams(dimension_semantics=("arbitrary",) * n_grid, vmem_limit_bytes=V7X_VMEM_LIMIT)


def _my_place():
    x, y, c = lax.axis_index("x"), lax.axis_index("y"), lax.axis_index("c")
    return x, y, c


def _peer(x, y, c, k):
    px = 1 - x if k & 4 else x
    py = 1 - y if k & 2 else y
    pc = 1 - c if k & 1 else c
    return (px, py, pc), 4 * px + 2 * py + pc


def _exchange(kind, src_refs, dst_refs, sems, first, phase):
    x, y, c = _my_place()
    me = 4 * x + 2 * y + c
    send_sems, recv_sems, own_sems = sems
    for j, (src, dst) in enumerate(zip(src_refs, dst_refs), start=first):
        own = pltpu.make_async_copy(src if kind == 'gather' else src.at[me], dst.at[me], own_sems.at[j])
        if phase == 'start':
            own.start()
        for k in range(1, N_DEV):
            peer, peer_block = _peer(x, y, c, k)
            out = pltpu.make_async_remote_copy(
                src_ref=src if kind == 'gather' else src.at[peer_block], dst_ref=dst.at[me],
                send_sem=send_sems.at[7 * j + k - 1], recv_sem=recv_sems.at[7 * j + k - 1], device_id=peer,
                device_id_type=pl.DeviceIdType.MESH)
            if phase == 'start':
                out.start()
            else:
                pltpu.make_async_remote_copy(
                    src_ref=src if kind == 'gather' else src.at[peer_block], dst_ref=dst.at[peer_block],
                    send_sem=send_sems.at[7 * j + k - 1], recv_sem=recv_sems.at[7 * j + k - 1], device_id=peer,
                    device_id_type=pl.DeviceIdType.MESH).wait_recv()
                out.wait_send()
        if phase == 'wait':
            own.wait()


def _gather_two_level(src_refs, dst_refs, sems, first, phase):
    x, y, c = _my_place()
    me, sibling = (x, y, c), (x, y, 1 - c)
    chips = [(1 - x, y), (x, 1 - y), (1 - x, 1 - y)]
    send_sems, recv_sems, own_sems = sems
    for j, (src, dst) in enumerate(zip(src_refs, dst_refs), start=first):
        def rows(px, py, pc, dst=dst):
            return dst.at[4 * px + 2 * py + pc]

        def copy(k, block, to, from_src=False, j=j, src=src, rows=rows):
            return pltpu.make_async_remote_copy(
                src_ref=src if from_src else rows(*block), dst_ref=rows(*block), send_sem=send_sems.at[7 * j + k],
                recv_sem=recv_sems.at[7 * j + k], device_id=to, device_id_type=pl.DeviceIdType.MESH)

        mine = pltpu.make_async_copy(src, rows(*me), own_sems.at[j])
        first_out = [copy(0, me, sibling, True)] + [copy(1 + i, me, (*chip, c), True) for i, chip in enumerate(chips)]
        if phase == 'start':
            mine.start()
            for cp in first_out:
                cp.start()
        else:
            passed = [copy(4 + i, (*chip, c), sibling) for i, chip in enumerate(chips)]
            for i, chip in enumerate(chips):
                copy(1 + i, (*chip, c), me).wait_recv()
                passed[i].start()
            copy(0, sibling, me).wait_recv()
            for i, chip in enumerate(chips):
                copy(4 + i, (*chip, 1 - c), me).wait_recv()
            for cp in first_out + passed:
                cp.wait_send()
            mine.wait()


class _Ride:
    def __init__(self, gather=(), scatter=(), two_level=False):
        self.two_level = two_level
        self.n_gather = len(gather)
        self.arrays = list(gather) + list(scatter)
        n = len(self.arrays)
        hbm = pl.BlockSpec(memory_space=pl.ANY)
        self.in_specs = [hbm] * n
        self.out_specs = [hbm] * n
        self.out_shape = [jax.ShapeDtypeStruct((N_DEV,) + a.shape, a.dtype) for a in gather]
        self.out_shape += [jax.ShapeDtypeStruct(a.shape, a.dtype) for a in scatter]
        self.scratch = [pltpu.SemaphoreType.DMA((7 * n,)), pltpu.SemaphoreType.DMA((7 * n,)),
                        pltpu.SemaphoreType.DMA((n,))] if n else []

    def wrap(self, inner, n_in, n_out, n_scr, is_first, is_last):
        n_mv = len(self.arrays)
        if not n_mv:
            return inner

        def body(*refs):
            mv_src = refs[n_in:n_in + n_mv]
            mv_dst = refs[n_in + n_mv + n_out:n_in + 2 * n_mv + n_out]
            sems = refs[n_in + 2 * n_mv + n_out + n_scr:]

            def exchange(phase):
                if self.n_gather and self.two_level:
                    _gather_two_level(mv_src[:self.n_gather], mv_dst[:self.n_gather], sems, 0, phase)
                elif self.n_gather:
                    _exchange('gather', mv_src[:self.n_gather], mv_dst[:self.n_gather], sems, 0, phase)
                if n_mv > self.n_gather:
                    _exchange('scatter', mv_src[self.n_gather:], mv_dst[self.n_gather:], sems, self.n_gather, phase)

            pl.when(is_first())(functools.partial(exchange, 'start'))
            inner(*refs[:n_in], *refs[n_in + n_mv:n_in + n_mv + n_out],
                  *refs[n_in + 2 * n_mv + n_out:n_in + 2 * n_mv + n_out + n_scr])
            pl.when(is_last())(functools.partial(exchange, 'wait'))

        return body


def _rowcall(name, body, n_rows, tm, ins, outs, accs=(), scratch=(), gather=(), scatter=(), two_level=False):
    n_steps = n_rows // tm
    ride = _Ride(gather, scatter, two_level)
    body = ride.wrap(body, len(ins), len(outs) + len(accs), len(scratch), lambda: pl.program_id(0) == 0,
                     lambda: pl.program_id(0) == n_steps - 1)
    arrays, in_specs = [], []
    for a, kind in ins:
        arrays.append(a)
        if kind == 'row':
            in_specs.append(pl.BlockSpec((tm,) + a.shape[1:], lambda i, nd=a.ndim: (i,) + (0,) * (nd - 1)))
        elif kind == 'full':
            in_specs.append(pl.BlockSpec(a.shape, lambda i, nd=a.ndim: (0,) * nd))
        else:
            in_specs.append(kind)
    out_shape, out_specs = [], []
    for cols, dt in outs:
        out_shape.append(jax.ShapeDtypeStruct((n_rows, cols), dt))
        out_specs.append(pl.BlockSpec((tm, cols), lambda i: (i, 0)))
    for shp, dt in accs:
        if isinstance(dt, tuple):
            dt, spec = dt
        else:
            spec = pl.BlockSpec(shp, lambda i, nd=len(shp): (0,) * nd)
        out_shape.append(jax.ShapeDtypeStruct(shp, dt))
        out_specs.append(spec)
    return pl.pallas_call(functools.partial(body), grid=(n_steps,), in_specs=in_specs + ride.in_specs,
                          out_specs=out_specs + ride.out_specs, out_shape=out_shape + ride.out_shape,
                          scratch_shapes=list(scratch) + ride.scratch, name=name,
                          compiler_params=_params(1))(*arrays, *ride.arrays)


DW_ACC_BYTES = 12 * 1024 * 1024
DW_LHS_BYTES = 6 * 1024 * 1024


def _dw_tiles(n, ka, nn, a_itemsize):
    tn = max(t for t in range(128, min(nn, 1536) + 1, 128) if nn % t == 0)
    tka = max(t for t in range(128, ka + 1, 128) if ka % t == 0 and t * tn * 4 <= DW_ACC_BYTES)
    tm = min(n, 2048)
    while tm > 256 and tm * tka * a_itemsize > DW_LHS_BYTES:
        tm //= 2
    return tm, tka, tn


def _dwcall(name, a, dc, out_dtype, gather=(), scatter=()):
    n, ka = a.shape
    nn = dc.shape[1]
    tm, tka, tn = _dw_tiles(n, ka, nn, a.dtype.itemsize)
    n_i, n_j, n_k = n // tm, nn // tn, ka // tka
    ride = _Ride(gather, scatter)

    def body(a_ref, dc_ref, o_ref, acc_ref):
        i = pl.program_id(2)

        @pl.when(i == 0)
        def _():
            acc_ref[...] = jnp.zeros_like(acc_ref)

        acc_ref[...] += _dot_tn(a_ref[...].astype(BF16), dc_ref[...].astype(BF16))

        @pl.when(i == n_i - 1)
        def _():
            o_ref[...] = acc_ref[...].astype(o_ref.dtype)

    def at_step(k, j, i):
        return lambda: (pl.program_id(0) == k) & (pl.program_id(1) == j) & (pl.program_id(2) == i)

    body = ride.wrap(body, 2, 1, 1, at_step(0, 0, 0), at_step(n_k - 1, n_j - 1, n_i - 1))
    res = pl.pallas_call(
        body, grid=(n_k, n_j, n_i),
        in_specs=[pl.BlockSpec((tm, tka), lambda k, j, i: (i, k)), pl.BlockSpec((tm, tn), lambda k, j, i: (i, j))]
        + ride.in_specs,
        out_specs=[pl.BlockSpec((tka, tn), lambda k, j, i: (k, j))] + ride.out_specs,
        out_shape=[jax.ShapeDtypeStruct((ka, nn), out_dtype)] + ride.out_shape,
        scratch_shapes=[pltpu.VMEM((tka, tn), F32)] + ride.scratch, name=name, compiler_params=_params(3))(
            a, dc, *ride.arrays)
    return res if ride.arrays else res[0]


def _first_gather(first, later):
    n = len(later)
    vm = pl.BlockSpec(memory_space=pltpu.VMEM)

    def body(*refs):
        first_ref, later_refs = refs[0], refs[1:1 + n]
        out_ref, cast_refs = refs[1 + n], refs[2 + n:2 + 2 * n]
        stage = refs[2 + 2 * n]
        sems = refs[3 + 2 * n:]
        stage[...] = first_ref[...].astype(BF16)
        _gather_two_level([stage], [out_ref], sems, 0, 'start')
        for src, dst in zip(later_refs, cast_refs):
            dst[...] = src[...].astype(BF16)
        _gather_two_level([stage], [out_ref], sems, 0, 'wait')

    return pl.pallas_call(
        body, out_shape=[jax.ShapeDtypeStruct((N_DEV,) + first.shape, BF16)]
        + [jax.ShapeDtypeStruct(s.shape, BF16) for s in later],
        in_specs=[vm] * (1 + n), out_specs=[pl.BlockSpec(memory_space=pl.ANY)] + [vm] * n,
        scratch_shapes=[pltpu.VMEM(first.shape, BF16), pltpu.SemaphoreType.DMA((7,)), pltpu.SemaphoreType.DMA((7,)),
                        pltpu.SemaphoreType.DMA((1,))],
        name="gather_first", compiler_params=pltpu.CompilerParams(vmem_limit_bytes=V7X_VMEM_LIMIT))(first, *later)


def _sum_adamw(name, recv, w, m, v):
    r, c = w.shape
    steps = max(s for s in (4, 2, 1) if r % s == 0 and (r // s) % 16 == 0 or s == 1)

    def body(recv_ref, w_ref, m_ref, v_ref, g_ref, d_ref, nm_ref, nv_ref):
        g = recv_ref[0].astype(F32)
        for k in range(1, N_DEV):
            g = g + recv_ref[k].astype(F32)
        d, nm, nv = _adamw(w_ref[...], g, m_ref[...], v_ref[...])
        g_ref[...] = g
        d_ref[...] = d
        nm_ref[...] = nm
        nv_ref[...] = nv

    blk = pl.BlockSpec((r // steps, c), lambda i: (i, 0))
    return pl.pallas_call(
        body, grid=(steps,), out_shape=[jax.ShapeDtypeStruct((r, c), F32)] * 4,
        in_specs=[pl.BlockSpec((N_DEV, r // steps, c), lambda i: (0, i, 0)), blk, blk, blk], out_specs=[blk] * 4,
        name=name, compiler_params=_params(1))(recv, w, m, v)


def _gather_last(a):
    def body(a_ref, out_ref, send_sems, recv_sems, own_sems):
        sems = (send_sems, recv_sems, own_sems)
        _exchange('gather', [a_ref], [out_ref], sems, 0, 'start')
        _exchange('gather', [a_ref], [out_ref], sems, 0, 'wait')

    vm = pl.BlockSpec(memory_space=pltpu.VMEM)
    return pl.pallas_call(
        body, out_shape=jax.ShapeDtypeStruct((N_DEV,) + a.shape, a.dtype), in_specs=[vm], out_specs=vm,
        scratch_shapes=[pltpu.SemaphoreType.DMA((7,)), pltpu.SemaphoreType.DMA((7,)), pltpu.SemaphoreType.DMA((1,))],
        name="gather_last", compiler_params=pltpu.CompilerParams(vmem_limit_bytes=V7X_VMEM_LIMIT))(a)


def _small_adamw(entries, gathered, loss_at):
    n, ng = len(entries), len(gathered)

    def body(*refs):
        g_refs, wmv = refs[:ng], refs[ng:ng + 3 * n]
        outs, loss_out = refs[ng + 3 * n:ng + 7 * n], refs[ng + 7 * n]

        def total(ref, r):
            if r is None:
                t = ref[0]
                for k in range(1, N_DEV):
                    t = t + ref[k]
                return t[None]
            t = ref[0, r:r + 1, :]
            for k in range(1, N_DEV):
                t = t + ref[k, r:r + 1, :]
            return t

        for j, (_, _, _, gi, r) in enumerate(entries):
            g = total(g_refs[gi], r)
            d, nm, nv = _adamw(wmv[3 * j][...], g, wmv[3 * j + 1][...], wmv[3 * j + 2][...])
            for ref, val in zip(outs[4 * j:4 * j + 4], (g, d, nm, nv)):
                ref[...] = val
        loss_out[...] = total(g_refs[loss_at[0]], loss_at[1])[:, :128]

    vm = pl.BlockSpec(memory_space=pltpu.VMEM)
    out_shape, arrays = [], list(gathered)
    for w, m, v, _, _ in entries:
        out_shape += [jax.ShapeDtypeStruct(w.shape, F32)] * 4
        arrays += [w, m, v]
    out_shape.append(jax.ShapeDtypeStruct((1, 128), F32))
    res = pl.pallas_call(
        body, out_shape=out_shape, in_specs=[vm] * len(arrays), out_specs=[vm] * len(out_shape),
        name="small_adamw", compiler_params=pltpu.CompilerParams(vmem_limit_bytes=V7X_VMEM_LIMIT))(*arrays)
    return [res[4 * j:4 * j + 4] for j in range(n)], res[4 * n]


def _blockdiag8(t):
    g, per = t.shape[0], 8
    eye = jnp.eye(per, dtype=F32)
    t = t.reshape(g // per, per, t.shape[1], t.shape[2])
    return (t[:, :, :, None, :] * eye[None, :, None, :, None]).reshape(g // per, per * t.shape[2], per * t.shape[3])


def _s5_out_tables(c_re, c_im):
    return _blockdiag8(c_re.transpose(0, 2, 1)), _blockdiag8(c_im.transpose(0, 2, 1))


def _s5_in_tables(lam_re, lam_im, log_step, b_re, b_im):
    step = jnp.exp(log_step)[:, None]
    mag = jnp.exp(lam_re * step)
    ab_re = mag * jnp.cos(lam_im * step)
    ab_im = mag * jnp.sin(lam_im * step)
    den = lam_re * lam_re + lam_im * lam_im
    nr = ab_re - 1.0
    co_re = (nr * lam_re + ab_im * lam_im) / den
    co_im = (ab_im * lam_re - nr * lam_im) / den
    bb_re = co_re[..., None] * b_re - co_im[..., None] * b_im
    bb_im = co_re[..., None] * b_im + co_im[..., None] * b_re
    return (ab_re.reshape(1, -1), ab_im.reshape(1, -1), _blockdiag8(bb_re.transpose(0, 2, 1)),
            _blockdiag8(bb_im.transpose(0, 2, 1)))


def _scan_passes(x_s, a_ref, cr_ref, ci_ref, reverse, re, im):
    tc, nc = x_s.shape[:2]
    ln = SCAN_LANES
    n_sq = int(math.log2(tc))
    assert 2 ** n_sq == tc
    ar = jnp.broadcast_to(a_ref[:, re], (nc, ln))
    ai = jnp.broadcast_to(a_ref[:, im], (nc, ln))
    zero = jnp.zeros((nc, ln), F32)

    def at(t):
        return tc - 1 - t if reverse else t

    def local(t, carry):
        sr, si = carry
        j = at(t)
        nr = ar * sr - ai * si + x_s[j, :, re]
        ni = ar * si + ai * sr + x_s[j, :, im]
        x_s[j, :, re] = nr
        x_s[j, :, im] = ni
        return nr, ni

    lr, li = lax.fori_loop(0, tc, local, (zero, zero))
    pr, pi = a_ref[:, re], a_ref[:, im]
    for _ in range(n_sq):
        pr, pi = pr * pr - pi * pi, 2.0 * (pr * pi)
    cr_ref[...] = lr
    ci_ref[...] = li
    tr = jnp.zeros((1, ln), F32)
    ti = jnp.zeros((1, ln), F32)
    for c in (range(nc - 1, -1, -1) if reverse else range(nc)):
        l_r = cr_ref[c:c + 1, :]
        l_i = ci_ref[c:c + 1, :]
        cr_ref[c:c + 1, :] = tr
        ci_ref[c:c + 1, :] = ti
        tr, ti = pr * tr - pi * ti + l_r, pr * ti + pi * tr + l_i

    def second_pass(on_fixed):
        def fixup(t, carry):
            qr, qi, acc = carry
            j = at(t)
            qr, qi = ar * qr - ai * qi, ar * qi + ai * qr
            gr = x_s[j, :, re] + qr
            gi = x_s[j, :, im] + qi
            x_s[j, :, re] = gr
            x_s[j, :, im] = gi
            return qr, qi, on_fixed(j, gr, gi, acc)

        return fixup

    return second_pass, (cr_ref[...], ci_ref[...]), zero


def _s5_cat_tables(a_re, a_im, bb_re, bb_im, cc_re, cc_im):
    lb = S5_BLOCK_LANES
    n_blk, kq, nq = bb_re.shape
    sub = nq // lb

    def lanes(re, im):
        lead = re.shape[:-1]
        both = jnp.stack([re.reshape(lead + (-1, lb)), im.reshape(lead + (-1, lb))], axis=-2)
        return both.reshape(lead + (-1,))

    cc = jnp.stack([cc_re.reshape(n_blk, sub, lb, kq), -cc_im.reshape(n_blk, sub, lb, kq)], axis=2)
    return lanes(a_re, a_im), lanes(bb_re, bb_im), cc.reshape(n_blk, sub * 2 * lb, kq)


def _s5_uncat(t, axis):
    lb = S5_BLOCK_LANES
    shp = t.shape
    t = t.reshape(shp[:axis] + (-1, 2, lb) + shp[axis + 1:])
    re, im = jnp.take(t, 0, axis=axis + 1), jnp.take(t, 1, axis=axis + 1)
    return re.reshape(shp[:axis] + (-1,) + shp[axis + 1:]), im.reshape(shp[:axis] + (-1,) + shp[axis + 1:])


def _grid2_edges(n0, n1):
    return (lambda: (pl.program_id(0) == 0) & (pl.program_id(1) == 0),
            lambda: (pl.program_id(0) == n0 - 1) & (pl.program_id(1) == n1 - 1))


def _s5_scan_fwd(u5p, a_cat, bb_cat, cc_cat, nb, gather=()):
    n, dh = u5p.shape
    n_blk, kq, cols = bb_cat.shape
    lb, ln, nc = S5_BLOCK_LANES, SCAN_LANES, SCAN_CHUNKS
    sub = cols // (2 * lb)
    rows = n // nb
    tc = rows // nc
    n_l = n_blk * sub

    def body(u_ref, a_ref, b_ref, c_ref, s_ref, y_ref, x_s, car_r, car_i):
        x_s[...] = _dot(u_ref[...].astype(BF16), b_ref[0]).reshape(tc, nc, 2 * lb)
        for h in range(lb // ln):
            re, im = slice(h * ln, (h + 1) * ln), slice(lb + h * ln, lb + (h + 1) * ln)

            def keep(j, gr, gi, acc, re=re, im=im):
                s_ref[0, j, :, re] = gr.astype(BF16)
                s_ref[0, j, :, im] = gi.astype(BF16)
                return acc

            second_pass, start, zero = _scan_passes(x_s, a_ref, car_r, car_i, False, re, im)
            lax.fori_loop(0, tc, second_pass(keep), (*start, zero))
        y = _dot(x_s[...].reshape(rows, 2 * lb).astype(BF16), c_ref[0])

        @pl.when(pl.program_id(1) % sub == 0)
        def _():
            y_ref[...] = y

        @pl.when(pl.program_id(1) % sub != 0)
        def _():
            y_ref[...] += y

    row_blk = pl.BlockSpec((rows, kq), lambda b, l: (b, l // sub))
    ride = _Ride(gather, two_level=True)
    return pl.pallas_call(
        ride.wrap(body, 4, 2, 3, *_grid2_edges(nb, n_l)), grid=(nb, n_l),
        in_specs=[row_blk, pl.BlockSpec((1, 2 * lb), lambda b, l: (0, l)),
                  pl.BlockSpec((1, kq, 2 * lb), lambda b, l: (l // sub, 0, l % sub)),
                  pl.BlockSpec((1, 2 * lb, kq), lambda b, l: (l // sub, l % sub, 0))] + ride.in_specs,
        out_specs=[pl.BlockSpec((1, tc, nc, 2 * lb), lambda b, l: (b, 0, 0, l)), row_blk] + ride.out_specs,
        out_shape=[jax.ShapeDtypeStruct((nb, tc, nc, n_l * 2 * lb), BF16), jax.ShapeDtypeStruct((n, dh), F32)]
        + ride.out_shape,
        scratch_shapes=[pltpu.VMEM((tc, nc, 2 * lb), F32)] + [pltpu.VMEM((nc, ln), F32)] * 2 + ride.scratch,
        name="s5_scan_fwd", compiler_params=_params(2))(u5p, a_cat, bb_cat, cc_cat, *ride.arrays)


def _s5_scan_bwd(dy_pre, u5p, s_cat, a_cat_conj, bb_cat, cc_cat, d_row, gather=(), scatter=()):
    n, dh = u5p.shape
    n_blk, kq, cols = bb_cat.shape
    nb, tc, nc, _ = s_cat.shape
    lb, ln = S5_BLOCK_LANES, SCAN_LANES
    sub = cols // (2 * lb)
    rows = n // nb
    n_l = n_blk * sub

    def body(dy_ref, u_ref, s_ref, a_ref, b_ref, c_ref, d_ref, du_ref, db_ref, dc_ref, da_ref, x_s, car_r, car_i):
        dyb = dy_ref[...].astype(BF16)
        x_s[...] = _dot_nt(dyb, c_ref[0]).reshape(tc, nc, 2 * lb)
        for h in range(lb // ln):
            re, im = slice(h * ln, (h + 1) * ln), slice(lb + h * ln, lb + (h + 1) * ln)

            def with_state_before(j, gr, gi, acc, re=re, im=im):
                dr, di = acc
                pr_, pi_ = s_ref[0, j - 1, :, re].astype(F32), s_ref[0, j - 1, :, im].astype(F32)
                return dr + (pr_ * gr + pi_ * gi), di + (pr_ * gi - pi_ * gr)

            second_pass, start, zero = _scan_passes(x_s, a_ref, car_r, car_i, True, re, im)
            carry = lax.fori_loop(0, tc - 1, second_pass(with_state_before), (*start, (zero, zero)))
            row = lax.broadcasted_iota(jnp.int32, (nc, ln), 0)
            pr_ = jnp.where(row == 0, 0.0, pltpu.roll(s_ref[0, tc - 1, :, re].astype(F32), 1, 0))
            pi_ = jnp.where(row == 0, 0.0, pltpu.roll(s_ref[0, tc - 1, :, im].astype(F32), 1, 0))

            def at_first_time(j, gr, gi, acc, pr_=pr_, pi_=pi_):
                return acc[0] + (pr_ * gr + pi_ * gi), acc[1] + (pr_ * gi - pi_ * gr)

            _, _, (dr, di) = second_pass(at_first_time)(tc - 1, carry)
            da_ref[0, :, re] = jnp.sum(dr, axis=0, keepdims=True)
            da_ref[0, :, im] = jnp.sum(di, axis=0, keepdims=True)

        gb = x_s[...].reshape(rows, 2 * lb).astype(BF16)
        du = _dot_nt(gb, b_ref[0])

        @pl.when(pl.program_id(1) % sub == 0)
        def _():
            du_ref[...] = du + d_ref[...] * dy_ref[...]

        @pl.when(pl.program_id(1) % sub != 0)
        def _():
            du_ref[...] += du

        db_ref[0, 0] = _dot_tn(u_ref[...].astype(BF16), gb)
        dc_ref[0, 0] = _dot_tn(s_ref[0].reshape(rows, 2 * lb), dyb)

    row_blk = pl.BlockSpec((rows, kq), lambda b, l: (b, l // sub))
    in_map = pl.BlockSpec((1, kq, 2 * lb), lambda b, l: (l // sub, 0, l % sub))
    out_map = pl.BlockSpec((1, 2 * lb, kq), lambda b, l: (l // sub, l % sub, 0))
    ride = _Ride(gather, scatter)
    return pl.pallas_call(
        ride.wrap(body, 7, 4, 3, *_grid2_edges(nb, n_l)), grid=(nb, n_l),
        in_specs=[row_blk, row_blk, pl.BlockSpec((1, tc, nc, 2 * lb), lambda b, l: (b, 0, 0, l)),
                  pl.BlockSpec((1, 2 * lb), lambda b, l: (0, l)), in_map, out_map,
                  pl.BlockSpec((1, kq), lambda b, l: (0, l // sub))] + ride.in_specs,
        out_specs=[row_blk, pl.BlockSpec((1, 1, kq, 2 * lb), lambda b, l: (b, l // sub, 0, l % sub)),
                   pl.BlockSpec((1, 1, 2 * lb, kq), lambda b, l: (b, l // sub, l % sub, 0)),
                   pl.BlockSpec((1, 1, 2 * lb), lambda b, l: (b, 0, l))] + ride.out_specs,
        out_shape=[jax.ShapeDtypeStruct((n, dh), F32), jax.ShapeDtypeStruct((nb, n_blk, kq, cols), F32),
                   jax.ShapeDtypeStruct((nb, n_blk, cols, kq), F32), jax.ShapeDtypeStruct((nb, 1, n_l * 2 * lb), F32)]
        + ride.out_shape,
        scratch_shapes=[pltpu.VMEM((tc, nc, 2 * lb), F32)] + [pltpu.VMEM((nc, ln), F32)] * 2 + ride.scratch,
        name="s5_scan_bwd", compiler_params=_params(2))(dy_pre, u5p, s_cat, a_cat_conj, bb_cat, cc_cat, d_row,
                                                        *ride.arrays)


def kernel(x, mem, g_mix_pre, w_in, gm_ln_g, gm_ln_b, gm_w_s, gm_b_s, s5_lam_re, s5_lam_im, s5_log_step, s5_b_re, s5_b_im, s5_c_re, s5_c_im, s5_d, s5_w_glu, w_br_gm, w_br_s5, w_mix_out, g_mix_post, g_ca_pre, g_mem, ca_w_q, ca_w_kv, ca_w_o, g_ca_post, g_ffn_pre, ffn_w_gu, ffn_w_down, g_ffn_post, loss_target, m_g_mix_pre, m_w_in, m_gm_ln_g, m_gm_ln_b, m_gm_w_s, m_gm_b_s, m_s5_lam_re, m_s5_lam_im, m_s5_log_step, m_s5_b_re, m_s5_b_im, m_s5_c_re, m_s5_c_im, m_s5_d, m_s5_w_glu, m_w_br_gm, m_w_br_s5, m_w_mix_out, m_g_mix_post, m_g_ca_pre, m_g_mem, m_ca_w_q, m_ca_w_kv, m_ca_w_o, m_g_ca_post, m_g_ffn_pre, m_ffn_w_gu, m_ffn_w_down, m_g_ffn_post, v_g_mix_pre, v_w_in, v_gm_ln_g, v_gm_ln_b, v_gm_w_s, v_gm_b_s, v_s5_lam_re, v_s5_lam_im, v_s5_log_step, v_s5_b_re, v_s5_b_im, v_s5_c_re, v_s5_c_im, v_s5_d, v_s5_w_glu, v_w_br_gm, v_w_br_s5, v_w_mix_out, v_g_mix_post, v_g_ca_pre, v_g_mem, v_ca_w_q, v_ca_w_kv, v_ca_w_o, v_g_ca_post, v_g_ffn_pre, v_ffn_w_gu, v_ffn_w_down, v_g_ffn_post):
    args = locals()
    W = {n: args[n][0] for n in WEIGHTS}
    M = {n: args['m_' + n][0] for n in WEIGHTS}
    V = {n: args['v_' + n][0] for n in WEIGHTS}
    for k, kind in SHARDED.items():
        if kind == 'colT':
            W[k], M[k], V[k] = W[k].T, M[k].T, V[k].T

    nb, seq, d = x.shape
    n = nb * seq
    tm = min(ROW_TILE, n)
    tmw = min(WIDE_ROW_TILE, n)
    tmf = min(FFN_ROW_TILE, n)
    nmem = mem.shape[1]
    d2, dh = 2 * d, d // 2
    hd = d // CA_HEADS
    x2d = x.reshape(n, d)
    tgt = loss_target.reshape(n, d)
    mem2d = mem.reshape(nb * nmem, d)

    def row(v):
        return v.reshape(1, -1)

    names = list(SHARDED)
    later = [k for k in names if k != 'w_in']
    first_g, *casts = _first_gather(W['w_in'], [W[k] for k in later])
    shard_b = dict(zip(later, casts))
    half = shard_b['ffn_w_gu'].shape[0] // 2
    gu_halves = [shard_b['ffn_w_gu'][:half], shard_b['ffn_w_gu'][half:]]

    def assemble(k, gth):
        r, c = gth.shape[1:]
        return gth.transpose(1, 0, 2).reshape(r, N_DEV * c) if SHARDED[k] == 'col' else gth.reshape(N_DEV * r, c)

    w_in_t = assemble('w_in', first_g)
    ffh = N_DEV * W['ffn_w_down'].shape[0]

    (a_re, a_im, bb_re, bb_im), s5_in_vjp = jax.vjp(
        _s5_in_tables, W['s5_lam_re'], W['s5_lam_im'], W['s5_log_step'], W['s5_b_re'], W['s5_b_im'])
    (cc_re, cc_im), s5_out_vjp = jax.vjp(_s5_out_tables, W['s5_c_re'], W['s5_c_im'])
    bb_re_b, bb_im_b, cc_re_b, cc_im_b = (t.astype(BF16) for t in (bb_re, bb_im, cc_re, cc_im))
    s5_d_row = row(W['s5_d'])
    tril = jnp.tril(jnp.ones((GM_CHUNK, GM_CHUNK), bool))
    w_s = jnp.where(tril[None], W['gm_w_s'], 0.0).astype(BF16)
    w_s_t = w_s.transpose(0, 2, 1)
    gm_bias = jnp.repeat(W['gm_b_s'].T, d // GM_GROUPS, axis=1)

    def in_proj_body(x_ref, g_ref, w_ref, zgm_ref, u5_ref, zga_ref, zgb_ref, h_ref):
        hb = _rms(x_ref[...], g_ref[...]).astype(BF16)
        h_ref[...] = hb
        for lo in range(0, w_ref.shape[0], 512):
            acc = _dot_nt(hb, w_ref[lo:lo + 512, :])
            if lo < d2:
                zgm_ref[:, lo:lo + 512] = acc.astype(BF16)
            elif lo < d2 + dh:
                u5_ref[...] = acc
            elif lo < d2 + dh + d:
                zga_ref[:, lo - d2 - dh:lo - d2 - dh + 512] = acc.astype(BF16)
            else:
                zgb_ref[:, lo - d2 - dh - d:lo - d2 - dh - d + 512] = acc.astype(BF16)

    ride = ['s5_w_glu', 'w_br_gm', 'w_br_s5']
    z_gm, u5, z_ga, z_gb, h0, *got = _rowcall(
        "in_proj", in_proj_body, n, tm, [(x2d, 'row'), (row(W['g_mix_pre']), 'full'), (w_in_t, 'full')],
        [(d2, BF16), (dh, F32), (d, BF16), (d, BF16), (d, BF16)], gather=[shard_b[k] for k in ride])
    w_glu_f, w_brgm_f, w_brs5_f = (assemble(k, g) for k, g in zip(ride, got))

    gw = d // GM_GROUPS

    def gm_fwd_body(z_ref, lng_ref, lnb_ref, ws_ref, bias_ref, y_ref, u_ref, vh_ref, dgelu_ref, rstd_ref):
        zg, dgelu = _gelu_and_grad(z_ref[...].astype(F32))
        dgelu_ref[...] = dgelu.astype(BF16)
        u, v = zg[:, :d], zg[:, d:]
        vc = v - jnp.mean(v, axis=-1, keepdims=True)
        rstd = lax.rsqrt(jnp.mean(vc * vc, axis=-1, keepdims=True) + EPS)
        vhat = vc * rstd
        u_ref[...] = u.astype(BF16)
        vh_ref[...] = vhat.astype(BF16)
        rstd_ref[...] = jnp.broadcast_to(rstd, rstd_ref.shape)
        vnb = (vhat * lng_ref[...] + lnb_ref[...]).astype(BF16)
        sv = jnp.concatenate([_dot(ws_ref[g], vnb[:, g * gw:(g + 1) * gw]) for g in range(GM_GROUPS)], axis=1)
        y_ref[...] = (u * (sv + bias_ref[...])).astype(BF16)

    y_gm, gm_u, gm_vhat, gm_dgelu, gm_rstd, g_mix, g_q = _rowcall(
        "gmlp_fwd", gm_fwd_body, n, GM_CHUNK,
        [(z_gm, 'row'), (row(W['gm_ln_g']), 'full'), (row(W['gm_ln_b']), 'full'), (w_s, 'full'), (gm_bias, 'full')],
        [(d, BF16), (d, BF16), (d, BF16), (d2, BF16), (128, F32)], gather=[shard_b['w_mix_out'], shard_b['ca_w_q']],
        two_level=True)
    w_mix_f, w_q_f = assemble('w_mix_out', g_mix), assemble('ca_w_q', g_q)

    tc = seq // SCAN_CHUNKS
    lt = S5_GROUPS * S5_STATE

    def to_scan_order(t):
        return t.reshape(nb, SCAN_CHUNKS, tc, t.shape[-1]).transpose(0, 2, 1, 3).reshape(n, t.shape[-1])

    def from_scan_order(t):
        return t.reshape(nb, tc, SCAN_CHUNKS, t.shape[-1]).transpose(0, 2, 1, 3).reshape(n, t.shape[-1])

    u5p = to_scan_order(u5)
    a_cat, bb_cat, cc_cat = _s5_cat_tables(a_re, a_im, bb_re_b, bb_im_b, cc_re_b, cc_im_b)
    a_cat_conj = _s5_cat_tables(a_re, -a_im, bb_re_b, bb_im_b, cc_re_b, cc_im_b)[0]
    s_cat, y_lin, g_kv, g_o = _s5_scan_fwd(u5p, a_cat, bb_cat, cc_cat, nb,
                                           gather=[shard_b['ca_w_kv'], shard_b['ca_w_o']])
    w_kv_f, w_o_f = assemble('ca_w_kv', g_kv), assemble('ca_w_o', g_o)

    def s5_out_body(yl_ref, u_ref, d_ref, wg_ref, ypre_ref, gate_ref, y_ref):
        ypre = yl_ref[...] + d_ref[...] * u_ref[...]
        ypre_ref[...] = ypre
        yg = _gelu(ypre)
        gate = _dot(yg.astype(BF16), wg_ref[...])
        gate_ref[...] = gate
        y_ref[...] = (yg * _sigmoid(gate)).astype(BF16)

    tm5 = min(1024, n)
    y_pre, gate, y_s5p = _rowcall(
        "s5_out", s5_out_body, n, tm5, [(y_lin, 'row'), (u5p, 'row'), (s5_d_row, 'full'), (w_glu_f, 'full')],
        [(dh, F32), (dh, F32), (dh, BF16)])
    y_s5 = from_scan_order(y_s5p)

    def merge_body(ygm_ref, ys5_ref, zga_ref, zgb_ref, wa_ref, wb_ref, pa_ref, pb_ref, mg_ref):
        pa = _dot(ygm_ref[...], wa_ref[...])
        pb = _dot(ys5_ref[...], wb_ref[...])
        pa_ref[...] = pa.astype(BF16)
        pb_ref[...] = pb.astype(BF16)
        mg_ref[...] = (_sigmoid(zga_ref[...].astype(F32)) * pa + _sigmoid(zgb_ref[...].astype(F32)) * pb).astype(BF16)

    p_a, p_b, merged, g_down = _rowcall(
        "merge", merge_body, n, tmw,
        [(y_gm, 'row'), (y_s5, 'row'), (z_ga, 'row'), (z_gb, 'row'), (w_brgm_f, 'full'), (w_brs5_f, 'full')],
        [(d, BF16), (d, BF16), (d, BF16)], gather=[shard_b['ffn_w_down']], two_level=True)
    w_down_f = assemble('ffn_w_down', g_down)

    def close_sublayer(name, a_in, w_out, x_res, g_post, g_next, w_next=None, gather=(), two_level=False):
        def body(*refs):
            a_ref, w_ref, x_ref, gp_ref, gn_ref = refs[:5]
            rest = refs[5:]
            if w_next is not None:
                wn_ref, rest = rest[0], rest[1:]
            o_ref, xo_ref, h_ref = rest[:3]
            o = _dot(a_ref[...], w_ref[...])
            o_ref[...] = o.astype(BF16)
            xo = x_ref[...] + _rms(o, gp_ref[...])
            xo_ref[...] = xo
            hb = _rms(xo, gn_ref[...]).astype(BF16)
            h_ref[...] = hb
            if w_next is not None:
                rest[3][...] = _dot(hb, wn_ref[...]).astype(BF16)

        ins = [(a_in, 'row'), (w_out, 'full'), (x_res, 'row'), (row(g_post), 'full'), (row(g_next), 'full')]
        outs = [(d, BF16), (d, F32), (d, BF16)]
        if w_next is not None:
            ins.append((w_next, 'full'))
            outs.append((w_next.shape[1], BF16))
        return _rowcall(name, body, n, tmw, ins, outs, gather=gather, two_level=two_level)

    o1, x1, hc, q, g_gu0 = close_sublayer("mix_out", merged, w_mix_f, x2d, W['g_mix_post'], W['g_ca_pre'], w_q_f,
                                          gather=[gu_halves[0]])

    tmm = min(ROW_TILE, nb * nmem)

    def memkv_body(m_ref, g_ref, w_ref, mn_ref, k_ref, v_ref):
        mnb = _rms(m_ref[...], g_ref[...]).astype(BF16)
        mn_ref[...] = mnb
        k_ref[...] = _dot(mnb, w_ref[:, :d]).astype(BF16)
        v_ref[...] = _dot(mnb, w_ref[:, d:]).astype(BF16)

    mem_n, k_mem, v_mem = _rowcall("mem_kv", memkv_body, nb * nmem, tmm,
                                   [(mem2d, 'row'), (row(W['g_mem']), 'full'), (w_kv_f, 'full')],
                                   [(d, BF16), (d, BF16), (d, BF16)])

    tiles_per_ex = seq // tm
    kv_spec = pl.BlockSpec((nmem, d), lambda i: (i // tiles_per_ex, 0))
    scale = hd ** -0.5

    def softmax_rows(qh, kh):
        s = _dot_nt(qh, kh) * scale
        e = jnp.exp(s - jnp.max(s, axis=-1, keepdims=True))
        return e / jnp.sum(e, axis=-1, keepdims=True)

    def attn_body(q_ref, k_ref, v_ref, o_ref):
        for h in range(CA_HEADS):
            sl = slice(h * hd, (h + 1) * hd)
            p = softmax_rows(q_ref[:, sl], k_ref[:, sl])
            o_ref[:, sl] = _dot(p.astype(BF16), v_ref[:, sl]).astype(BF16)

    (att,) = _rowcall("attn_fwd", attn_body, n, tm, [(q, 'row'), (k_mem, kv_spec), (v_mem, kv_spec)], [(d, BF16)])

    o2, x2, hf, g_gu1 = close_sublayer("attn_out", att, w_o_f, x1, W['g_ca_post'], W['g_ffn_pre'],
                                       gather=[gu_halves[1]], two_level=True)
    w_gu_t = jnp.stack([g_gu0, g_gu1], axis=1).reshape(2 * ffh, d)

    ck = 256

    def ffn_up_body(h_ref, w_ref, gu_ref, a_ref):
        hb = h_ref[...]
        for lo in range(0, ffh, ck):
            gt = _dot_nt(hb, w_ref[lo:lo + ck, :])
            ut = _dot_nt(hb, w_ref[ffh + lo:ffh + lo + ck, :])
            gu_ref[:, lo:lo + ck] = gt.astype(BF16)
            gu_ref[:, ffh + lo:ffh + lo + ck] = ut.astype(BF16)
            a_ref[:, lo:lo + ck] = ((gt * _sigmoid(gt)) * ut).astype(BF16)

    gu, act = _rowcall("ffn_up", ffn_up_body, n, tm, [(hf, 'row'), (w_gu_t, 'full')], [(2 * ffh, BF16), (ffh, BF16)])

    def ffn_down_body(a_ref, w_ref, x_ref, t_ref, g_ref, dx_ref, do_ref, loss_ref, dg_ref):
        i = pl.program_id(0)

        @pl.when(i == 0)
        def _():
            loss_ref[...] = jnp.zeros_like(loss_ref)
            dg_ref[...] = jnp.zeros_like(dg_ref)

        o = _dot(a_ref[...], w_ref[...])
        diff = x_ref[...] + _rms(o, g_ref[...]) - t_ref[...]
        loss_ref[...] += jnp.full(loss_ref.shape, 0.5 / d, F32) * jnp.sum(diff * diff)
        dx = diff * (1.0 / d)
        dx_ref[...] = dx
        do, dg = _rms_bwd(o, g_ref[...], dx)
        do_ref[...] = do.astype(BF16)
        dg_ref[...] += dg

    dx3, do3, loss_part, dg_ffn_post = _rowcall(
        "ffn_down_loss", ffn_down_body, n, tmf,
        [(act, 'row'), (w_down_f, 'full'), (x2, 'row'), (tgt, 'row'), (row(W['g_ffn_post']), 'full')],
        [(d, F32), (d, BF16)], accs=[((1, 128), F32), ((1, d), F32)])

    G = {'g_ffn_post': dg_ffn_post}
    RECV = {}

    def parts_of(k, gfull):
        r, c = W[k].shape
        return gfull.reshape(r, N_DEV, c).transpose(1, 0, 2) if SHARDED[k] == 'col' else gfull.reshape(N_DEV, r, c)

    p_down = parts_of('ffn_w_down', _dwcall("dw_ffn_down", act, do3, BF16))

    def ffn_act_bwd_body(do_ref, w_ref, gu_ref, dgu_ref):
        dob = do_ref[...]
        for lo in range(0, ffh, ck):
            da = _dot_nt(dob, w_ref[lo:lo + ck, :])
            gt = gu_ref[:, lo:lo + ck].astype(F32)
            ut = gu_ref[:, ffh + lo:ffh + lo + ck].astype(F32)
            sg = _sigmoid(gt)
            dgu_ref[:, lo:lo + ck] = (da * ut * (sg * (1.0 + gt * (1.0 - sg)))).astype(BF16)
            dgu_ref[:, ffh + lo:ffh + lo + ck] = (da * (gt * sg)).astype(BF16)

    dgu, RECV['ffn_w_down'] = _rowcall("ffn_act_bwd", ffn_act_bwd_body, n, tmf,
                                       [(do3, 'row'), (w_down_f, 'full'), (gu, 'row')], [(2 * ffh, BF16)],
                                       scatter=[p_down])
    p_gu = parts_of('ffn_w_gu', _dwcall("dw_ffn_gu", dgu, hf, BF16))

    def open_sublayer(name, pieces, w_full, x_in, g_pre, dx_up, o_prev=None, g_post_prev=None, scatter=(), gather=(),
                      w_t=False):
        n_p = len(pieces)
        second = o_prev is not None

        def body(*refs):
            dc_refs, (w_ref, x_ref, g_ref, dxu_ref), rest = refs[:n_p], refs[n_p:n_p + 4], refs[n_p + 4:]
            if second:
                (op_ref, gp_ref), rest = rest[:2], rest[2:]
            i = pl.program_id(0)
            dhid = None
            for dc_ref, (_, lo, hi) in zip(dc_refs, pieces):
                part = _dot(dc_ref[...], w_ref[lo:hi, :]) if w_t else _dot_nt(dc_ref[...], w_ref[:, lo:hi])
                dhid = part if dhid is None else dhid + part
            dxn, dg = _rms_bwd(x_ref[...], g_ref[...], dhid)
            dx = dxu_ref[...] + dxn
            if second:
                dx_ref, do_ref, dg_ref, dg2_ref = rest
            else:
                dx_ref, dg_ref = rest

            @pl.when(i == 0)
            def _():
                dg_ref[...] = jnp.zeros_like(dg_ref)
                if second:
                    dg2_ref[...] = jnp.zeros_like(dg2_ref)

            dx_ref[...] = dx
            dg_ref[...] += dg
            if second:
                do, dg2 = _rms_bwd(op_ref[...].astype(F32), gp_ref[...], dx)
                do_ref[...] = do.astype(BF16)
                dg2_ref[...] += dg2

        ins = [(p[0], 'row') for p in pieces] + [(w_full, 'full'), (x_in, 'row'), (row(g_pre), 'full'), (dx_up, 'row')]
        outs = [(d, F32)]
        accs = [((1, d), F32)]
        if second:
            ins += [(o_prev, 'row'), (row(g_post_prev), 'full')]
            outs.append((d, BF16))
            accs.append(((1, d), F32))
        light = w_full.size <= d * d
        res = _rowcall(name, body, n, tmw if light else tm, ins, outs, accs=accs, scatter=scatter, gather=gather)
        if second:
            dx, do, dg, dg2 = res[:4]
            return dx, dg, do, dg2, res[4:]
        return res[0], res[1], res[2:]

    dx2, G['g_ffn_pre'], do2, G['g_ca_post'], _ = open_sublayer(
        "ffn_in_bwd", [(dgu, 0, 2 * ffh)], w_gu_t, x2, W['g_ffn_pre'], dx3, o2, W['g_ca_post'], w_t=True)
    p_o = parts_of('ca_w_o', _dwcall("dw_ca_o", att, do2, BF16))

    def attn_bwd_body(q_ref, k_ref, v_ref, do_ref, wo_ref, dq_ref, dk_ref, dv_ref):
        i = pl.program_id(0)

        @pl.when(i % tiles_per_ex == 0)
        def _():
            dk_ref[...] = jnp.zeros_like(dk_ref)
            dv_ref[...] = jnp.zeros_like(dv_ref)

        d_att = _dot_nt(do_ref[...], wo_ref[...]).astype(BF16)
        for h in range(CA_HEADS):
            sl = slice(h * hd, (h + 1) * hd)
            qh, kh, vh, dah = q_ref[:, sl], k_ref[:, sl], v_ref[:, sl], d_att[:, sl]
            s_t = _dot_nt(kh, qh) * scale
            e_t = jnp.exp(s_t - jnp.max(s_t, axis=0, keepdims=True))
            p_t = e_t / jnp.sum(e_t, axis=0, keepdims=True)
            dp_t = _dot_nt(vh, dah)
            ds_t = (p_t * (dp_t - jnp.sum(p_t * dp_t, axis=0, keepdims=True)) * scale).astype(BF16)
            dq_ref[:, sl] = _dot_tn(ds_t, kh).astype(BF16)
            dk_ref[:, sl] += _dot(ds_t, qh)
            dv_ref[:, sl] += _dot(p_t.astype(BF16), dah)

    kv_acc = ((nb * nmem, d), (F32, pl.BlockSpec((nmem, d), lambda i: (i // tiles_per_ex, 0))))
    dq, dk_mem, dv_mem, RECV['ca_w_o'] = _rowcall(
        "attn_bwd", attn_bwd_body, n, tm,
        [(q, 'row'), (k_mem, kv_spec), (v_mem, kv_spec), (do2, 'row'), (w_o_f, 'full')], [(d, BF16)],
        accs=[kv_acc, kv_acc], scatter=[p_o])
    p_q = parts_of('ca_w_q', _dwcall("dw_ca_q", hc, dq, BF16))

    def memkv_bwd_body(dk_ref, dv_ref, m_ref, g_ref, w_ref, dkv_ref, dg_ref):
        i = pl.program_id(0)

        @pl.when(i == 0)
        def _():
            dg_ref[...] = jnp.zeros_like(dg_ref)

        dkb, dvb = dk_ref[...].astype(BF16), dv_ref[...].astype(BF16)
        dkv_ref[:, :d] = dkb
        dkv_ref[:, d:] = dvb
        dmn = _dot_nt(dkb, w_ref[:, :d]) + _dot_nt(dvb, w_ref[:, d:])
        _, dg = _rms_bwd(m_ref[...], g_ref[...], dmn)
        dg_ref[...] += dg

    dkv, G['g_mem'] = _rowcall(
        "mem_kv_bwd", memkv_bwd_body, nb * nmem, tmm,
        [(dk_mem, 'row'), (dv_mem, 'row'), (mem2d, 'row'), (row(W['g_mem']), 'full'), (w_kv_f, 'full')],
        [(d2, BF16)], accs=[((1, d), F32)])
    p_kv = parts_of('ca_w_kv', _dwcall("dw_ca_kv", mem_n, dkv, BF16))

    dx1, G['g_ca_pre'], do1, G['g_mix_post'], _ = open_sublayer(
        "attn_in_bwd", [(dq, 0, d)], w_q_f, x1, W['g_ca_pre'], dx2, o1, W['g_mix_post'])
    p_mix = parts_of('w_mix_out', _dwcall("dw_mix_out", merged, do1, BF16))

    def merge_bwd_body(do_ref, wm_ref, zga_ref, zgb_ref, pa_ref, pb_ref, wb_ref, dpa_ref, dpb_ref, dza_ref, dzb_ref,
                       dys_ref):
        dm = _dot_nt(do_ref[...], wm_ref[...])
        sa, sb = _sigmoid(zga_ref[...].astype(F32)), _sigmoid(zgb_ref[...].astype(F32))
        dpb = (dm * sb).astype(BF16)
        dpa_ref[...] = (dm * sa).astype(BF16)
        dpb_ref[...] = dpb
        dza_ref[...] = (dm * pa_ref[...].astype(F32) * (sa * (1.0 - sa))).astype(BF16)
        dzb_ref[...] = (dm * pb_ref[...].astype(F32) * (sb * (1.0 - sb))).astype(BF16)
        dys_ref[...] = _dot_nt(dpb, wb_ref[...]).astype(BF16)

    NORM_ROWS = ['g_ffn_post', 'g_ffn_pre', 'g_ca_post', 'g_mem', 'g_ca_pre', 'g_mix_post']
    norm_rows = jnp.concatenate([G[k] for k in NORM_ROWS] + [jnp.tile(loss_part, (1, d // 128))], axis=0)
    dp_a, dp_b, dz_ga, dz_gb, dy_s5, all_norm_rows, RECV['w_mix_out'], RECV['ca_w_q'] = _rowcall(
        "merge_bwd", merge_bwd_body, n, tmw,
        [(do1, 'row'), (w_mix_f, 'full'), (z_ga, 'row'), (z_gb, 'row'), (p_a, 'row'), (p_b, 'row'), (w_brs5_f, 'full')],
        [(d, BF16), (d, BF16), (d, BF16), (d, BF16), (dh, BF16)], scatter=[p_mix, p_q], gather=[norm_rows])
    p_brgm = parts_of('w_br_gm', _dwcall("dw_br_gm", y_gm, dp_a, BF16))
    p_brs5 = parts_of('w_br_s5', _dwcall("dw_br_s5", y_s5, dp_b, BF16))

    def gm_bwd_body(u_ref, vh_ref, dgelu_ref, rstd_ref, dpa_ref, wa_ref, lng_ref, lnb_ref, ws_ref, wst_ref, bias_ref,
                    dz_ref, dws_ref, dbias_ref, dlng_ref, dlnb_ref):
        i = pl.program_id(0)

        @pl.when(i == 0)
        def _():
            dws_ref[...] = jnp.zeros_like(dws_ref)
            dbias_ref[...] = jnp.zeros_like(dbias_ref)
            dlng_ref[...] = jnp.zeros_like(dlng_ref)
            dlnb_ref[...] = jnp.zeros_like(dlnb_ref)

        u, vhat, rstd = u_ref[...].astype(F32), vh_ref[...].astype(F32), rstd_ref[:, :1]
        vnb = (vhat * lng_ref[...] + lnb_ref[...]).astype(BF16)
        dy = _dot_nt(dpa_ref[...], wa_ref[...])
        groups = [slice(g * gw, (g + 1) * gw) for g in range(GM_GROUPS)]
        dsv = dy * u
        dsvb = dsv.astype(BF16)
        dbias_ref[...] += dsv
        sv = jnp.concatenate([_dot(ws_ref[g], vnb[:, sl]) for g, sl in enumerate(groups)], axis=1) + bias_ref[...]
        dvn = jnp.concatenate([_dot(wst_ref[g], dsvb[:, sl]) for g, sl in enumerate(groups)], axis=1)
        for g, sl in enumerate(groups):
            dws_ref[g] += _dot_nt(dsvb[:, sl], vnb[:, sl])
        du = dy * sv
        dlng_ref[...] += jnp.sum(dvn * vhat, axis=0, keepdims=True)
        dlnb_ref[...] += jnp.sum(dvn, axis=0, keepdims=True)
        dvh = dvn * lng_ref[...]
        dv = rstd * (dvh - jnp.mean(dvh, axis=-1, keepdims=True) - vhat * jnp.mean(dvh * vhat, axis=-1, keepdims=True))
        dz_ref[:, :d] = (du * dgelu_ref[:, :d].astype(F32)).astype(BF16)
        dz_ref[:, d:] = (dv * dgelu_ref[:, d:].astype(F32)).astype(BF16)

    dz_gm, dws_full, dbias_full, G['gm_ln_g'], G['gm_ln_b'], RECV['w_br_gm'], RECV['w_br_s5'], RECV['ca_w_kv'] = _rowcall(
        "gmlp_bwd", gm_bwd_body, n, GM_CHUNK,
        [(gm_u, 'row'), (gm_vhat, 'row'), (gm_dgelu, 'row'), (gm_rstd, 'row'), (dp_a, 'row'), (w_brgm_f, 'full'),
         (row(W['gm_ln_g']), 'full'), (row(W['gm_ln_b']), 'full'), (w_s, 'full'), (w_s_t, 'full'), (gm_bias, 'full')],
        [(d2, BF16)], accs=[((GM_GROUPS, GM_CHUNK, GM_CHUNK), F32), ((GM_CHUNK, d), F32), ((1, d), F32), ((1, d), F32)],
        scatter=[p_brgm, p_brs5, p_kv])
    G['gm_w_s'] = jnp.where(tril[None], dws_full, 0.0)
    G['gm_b_s'] = dbias_full.reshape(GM_CHUNK, GM_GROUPS, gw).sum(-1).T

    dy_s5p = to_scan_order(dy_s5)

    def s5_out_bwd_body(dy_ref, ypre_ref, gate_ref, u_ref, wg_ref, dyp_ref, dgate_ref, yg_ref, dd_ref):
        i = pl.program_id(0)

        @pl.when(i == 0)
        def _():
            dd_ref[...] = jnp.zeros_like(dd_ref)

        dy = dy_ref[...].astype(F32)
        yg, dgelu = _gelu_and_grad(ypre_ref[...])
        sg = _sigmoid(gate_ref[...])
        dgb = (dy * yg * (sg * (1.0 - sg))).astype(BF16)
        dgate_ref[...] = dgb
        yg_ref[...] = yg.astype(BF16)
        dyp = (dy * sg + _dot_nt(dgb, wg_ref[...])) * dgelu
        dyp_ref[...] = dyp
        dd_ref[...] += jnp.sum(dyp * u_ref[...], axis=0, keepdims=True)

    dy_pre, dgate, yg_b, dd = _rowcall(
        "s5_out_bwd", s5_out_bwd_body, n, tm5,
        [(dy_s5p, 'row'), (y_pre, 'row'), (gate, 'row'), (u5p, 'row'), (w_glu_f, 'full')],
        [(dh, F32), (dh, BF16), (dh, BF16)], accs=[((1, dh), F32)])
    gm_ln_rows = jnp.concatenate([G['gm_ln_g'], G['gm_ln_b']], axis=0)
    du5p, d_bb_cat, d_cc_cat, da_cat, all_gm_w_s, all_gm_b_s, all_gm_ln, RECV['ffn_w_gu'] = _s5_scan_bwd(
        dy_pre, u5p, s_cat, a_cat_conj, bb_cat, cc_cat, s5_d_row, gather=[G['gm_w_s'], G['gm_b_s'], gm_ln_rows],
        scatter=[p_gu])
    du5 = from_scan_order(du5p.astype(BF16))
    d_bb_re, d_bb_im = _s5_uncat(jnp.sum(d_bb_cat, axis=0), 2)
    d_cc_re, d_cc_neg_im = _s5_uncat(jnp.sum(d_cc_cat, axis=0), 1)
    da_re, da_im = _s5_uncat(jnp.sum(da_cat, axis=0), 1)
    lane_shape = (S5_GROUPS, 8, 128)
    g_c_re, g_c_im = (t.reshape(lane_shape) for t in s5_out_vjp((d_cc_re, -d_cc_neg_im)))
    g_d = dd.reshape(S5_GROUPS, S5_CH)
    g_lam_re, g_lam_im, g_log_step, g_b_re, g_b_im = s5_in_vjp((da_re, da_im, d_bb_re, d_bb_im))
    g_log_step, g_b_re, g_b_im = g_log_step.reshape(1, -1), g_b_re.reshape(lane_shape), g_b_im.reshape(lane_shape)
    p_glu = parts_of('s5_w_glu', _dwcall("dw_s5_glu", yg_b, dgate, BF16))

    pieces = [(dz_gm, 0, d2), (du5, d2, d2 + dh), (dz_ga, d2 + dh, d2 + dh + d), (dz_gb, d2 + dh + d, d2 + dh + 2 * d)]
    dw_gm, all_lam_re, all_lam_im, all_log_step, all_b_re, all_b_im = _dwcall(
        "dw_in_0", dz_gm, h0, BF16, gather=[g_lam_re, g_lam_im, g_log_step, g_b_re, g_b_im])
    dw_s5 = _dwcall("dw_in_1", du5, h0, BF16)
    dw_ga = _dwcall("dw_in_2", dz_ga, h0, BF16)
    dw_gb, all_c_re, all_c_im, all_d, RECV['s5_w_glu'] = _dwcall(
        "dw_in_3", dz_gb, h0, BF16, gather=[g_c_re, g_c_im, g_d], scatter=[p_glu])
    p_in = parts_of('w_in', jnp.concatenate([dw_gm, dw_s5, dw_ga, dw_gb], axis=0))
    grad_x2d, g_mix_pre_part, (RECV['w_in'],) = open_sublayer(
        "in_proj_bwd", pieces, w_in_t, x2d, W['g_mix_pre'], dx1, scatter=[p_in], w_t=True)

    out = {}
    for k in names:
        quad = _sum_adamw("sum_adamw_" + k, RECV[k], W[k], M[k], V[k])
        out[k] = [(t.T if SHARDED[k] == 'colT' else t)[None] for t in quad]
    gathered = [all_norm_rows, all_gm_w_s, all_gm_b_s, all_gm_ln, all_c_re, all_c_im, all_d, all_lam_re, all_lam_im,
                all_log_step, all_b_re, all_b_im, _gather_last(g_mix_pre_part)]
    where = {'g_ffn_post': (0, 0), 'g_ffn_pre': (0, 1), 'g_ca_post': (0, 2), 'g_mem': (0, 3), 'g_ca_pre': (0, 4),
             'g_mix_post': (0, 5), 'gm_w_s': (1, None), 'gm_b_s': (2, None), 'gm_ln_g': (3, 0), 'gm_ln_b': (3, 1),
             's5_c_re': (4, None), 's5_c_im': (5, None), 's5_d': (6, None), 's5_lam_re': (7, None),
             's5_lam_im': (8, None), 's5_log_step': (9, 0), 's5_b_re': (10, None), 's5_b_im': (11, None),
             'g_mix_pre': (12, 0)}
    folded = ('s5_b_re', 's5_b_im', 's5_c_re', 's5_c_im')

    def as_updated(k, t):
        return t.reshape((1,) + lane_shape) if k in folded else t

    small, loss_row = _small_adamw(
        [(as_updated(k, args[k]), as_updated(k, args['m_' + k]), as_updated(k, args['v_' + k])) + where[k]
         for k in SMALL], gathered, (0, len(NORM_ROWS)))
    out.update({k: [t.reshape(args[k].shape) for t in quad] for k, quad in zip(SMALL, small)})

    res = [loss_row[0, 0], grad_x2d.reshape(x.shape)]
    for j in range(4):
        res += [out[k][j] for k in WEIGHTS]
    return tuple(res)
```

```python
import functools
import math

import jax
import jax.numpy as jnp
from jax import lax
from jax.experimental import pallas as pl
from jax.experimental.pallas import tpu as pltpu

F32 = jnp.float32
BF16 = jnp.bfloat16
EPS = 1e-6
N_DEV = 8
V7X_VMEM_LIMIT = 56 * 1024 * 1024
ROW_TILE = 256
WIDE_ROW_TILE = 1024
FFN_ROW_TILE = 512
GM_CHUNK = 128
GM_GROUPS = 8
S5_GROUPS = 32
S5_STATE = 64
S5_CH = 16
SCAN_CHUNKS = 32
SCAN_LANES = 128
S5_BLOCK_LANES = 256
CA_HEADS = 4
ADAM_LR, ADAM_B1, ADAM_B2, ADAM_EPS, ADAM_WD, ADAM_STEP = 0.001, 0.9, 0.999, 1e-08, 0.01, 10

WEIGHTS = ['g_mix_pre', 'w_in', 'gm_ln_g', 'gm_ln_b', 'gm_w_s', 'gm_b_s', 's5_lam_re', 's5_lam_im', 's5_log_step',
           's5_b_re', 's5_b_im', 's5_c_re', 's5_c_im', 's5_d', 's5_w_glu', 'w_br_gm', 'w_br_s5', 'w_mix_out',
           'g_mix_post', 'g_ca_pre', 'g_mem', 'ca_w_q', 'ca_w_kv', 'ca_w_o', 'g_ca_post', 'g_ffn_pre', 'ffn_w_gu',
           'ffn_w_down', 'g_ffn_post']
SHARDED = {'w_in': 'colT', 's5_w_glu': 'row', 'w_br_gm': 'row', 'w_br_s5': 'col', 'w_mix_out': 'row',
           'ca_w_q': 'row', 'ca_w_kv': 'col', 'ca_w_o': 'row', 'ffn_w_gu': 'colT', 'ffn_w_down': 'row'}
SMALL = [n for n in WEIGHTS if n not in SHARDED]


def _rms(x, g):
    r = lax.rsqrt(jnp.mean(x * x, axis=-1, keepdims=True) + EPS)
    return (x * r) * g


def _rms_bwd(x, g, dy):
    r = lax.rsqrt(jnp.mean(x * x, axis=-1, keepdims=True) + EPS)
    n = x * r
    dn = dy * g
    dx = r * (dn - n * jnp.mean(dn * n, axis=-1, keepdims=True))
    return dx, jnp.sum(dy * n, axis=0, keepdims=True)


_GELU_C = math.sqrt(2.0 / math.pi)


def _gelu(x):
    return 0.5 * x * (1.0 + jnp.tanh(_GELU_C * (x + 0.044715 * (x * x * x))))


def _gelu_and_grad(x):
    x2 = x * x
    t = jnp.tanh(_GELU_C * (x + 0.044715 * (x2 * x)))
    h = 0.5 * (1.0 + t)
    return x * h, h + 0.5 * x * (1.0 - t * t) * (_GELU_C * (1.0 + 3.0 * 0.044715 * x2))


def _sigmoid(x):
    return 0.5 * (1.0 + jnp.tanh(0.5 * x))


def _dot(a, b):
    return jnp.dot(a, b, preferred_element_type=F32)


def _dot_nt(a, b):
    return lax.dot_general(a, b, (((1,), (1,)), ((), ())), preferred_element_type=F32)


def _dot_tn(a, b):
    return lax.dot_general(a, b, (((0,), (0,)), ((), ())), preferred_element_type=F32)


def _adamw(w, g, m, v):
    m = ADAM_B1 * m + (1.0 - ADAM_B1) * g
    v = ADAM_B2 * v + (1.0 - ADAM_B2) * (g * g)
    m_hat = m / (1.0 - ADAM_B1 ** ADAM_STEP)
    v_hat = v / (1.0 - ADAM_B2 ** ADAM_STEP)
    delta = -ADAM_LR * (m_hat / (jnp.sqrt(v_hat) + ADAM_EPS) + ADAM_WD * w)
    return delta, m, v


def _params(n_grid):
    return pltpu.CompilerParams(dimension_semantics=("arbitrary",) * n_grid, vmem_limit_bytes=V7X_VMEM_LIMIT)


def _my_place():
    x, y, c = lax.axis_index("x"), lax.axis_index("y"), lax.axis_index("c")
    return x, y, c


def _peer(x, y, c, k):
    px = 1 - x if k & 4 else x
    py = 1 - y if k & 2 else y
    pc = 1 - c if k & 1 else c
    return (px, py, pc), 4 * px + 2 * py + pc


def _exchange(kind, src_refs, dst_refs, sems, first, phase):
    x, y, c = _my_place()
    me = 4 * x + 2 * y + c
    send_sems, recv_sems, own_sems = sems
    for j, (src, dst) in enumerate(zip(src_refs, dst_refs), start=first):
        own = pltpu.make_async_copy(src if kind == 'gather' else src.at[me], dst.at[me], own_sems.at[j])
        if phase == 'start':
            own.start()
        for k in range(1, N_DEV):
            peer, peer_block = _peer(x, y, c, k)
            out = pltpu.make_async_remote_copy(
                src_ref=src if kind == 'gather' else src.at[peer_block], dst_ref=dst.at[me],
                send_sem=send_sems.at[7 * j + k - 1], recv_sem=recv_sems.at[7 * j + k - 1], device_id=peer,
                device_id_type=pl.DeviceIdType.MESH)
            if phase == 'start':
                out.start()
            else:
                pltpu.make_async_remote_copy(
                    src_ref=src if kind == 'gather' else src.at[peer_block], dst_ref=dst.at[peer_block],
                    send_sem=send_sems.at[7 * j + k - 1], recv_sem=recv_sems.at[7 * j + k - 1], device_id=peer,
                    device_id_type=pl.DeviceIdType.MESH).wait_recv()
                out.wait_send()
        if phase == 'wait':
            own.wait()


def _gather_two_level(src_refs, dst_refs, sems, first, phase):
    x, y, c = _my_place()
    me, sibling = (x, y, c), (x, y, 1 - c)
    chips = [(1 - x, y), (x, 1 - y), (1 - x, 1 - y)]
    send_sems, recv_sems, own_sems = sems
    for j, (src, dst) in enumerate(zip(src_refs, dst_refs), start=first):
        def rows(px, py, pc, dst=dst):
            return dst.at[4 * px + 2 * py + pc]

        def copy(k, block, to, from_src=False, j=j, src=src, rows=rows):
            return pltpu.make_async_remote_copy(
                src_ref=src if from_src else rows(*block), dst_ref=rows(*block), send_sem=send_sems.at[7 * j + k],
                recv_sem=recv_sems.at[7 * j + k], device_id=to, device_id_type=pl.DeviceIdType.MESH)

        mine = pltpu.make_async_copy(src, rows(*me), own_sems.at[j])
        first_out = [copy(0, me, sibling, True)] + [copy(1 + i, me, (*chip, c), True) for i, chip in enumerate(chips)]
        if phase == 'start':
            mine.start()
            for cp in first_out:
                cp.start()
        else:
            passed = [copy(4 + i, (*chip, c), sibling) for i, chip in enumerate(chips)]
            for i, chip in enumerate(chips):
                copy(1 + i, (*chip, c), me).wait_recv()
                passed[i].start()
            copy(0, sibling, me).wait_recv()
            for i, chip in enumerate(chips):
                copy(4 + i, (*chip, 1 - c), me).wait_recv()
            for cp in first_out + passed:
                cp.wait_send()
            mine.wait()


class _Ride:
    def __init__(self, gather=(), scatter=(), two_level=False):
        self.two_level = two_level
        self.n_gather = len(gather)
        self.arrays = list(gather) + list(scatter)
        n = len(self.arrays)
        hbm = pl.BlockSpec(memory_space=pl.ANY)
        self.in_specs = [hbm] * n
        self.out_specs = [hbm] * n
        self.out_shape = [jax.ShapeDtypeStruct((N_DEV,) + a.shape, a.dtype) for a in gather]
        self.out_shape += [jax.ShapeDtypeStruct(a.shape, a.dtype) for a in scatter]
        self.scratch = [pltpu.SemaphoreType.DMA((7 * n,)), pltpu.SemaphoreType.DMA((7 * n,)),
                        pltpu.SemaphoreType.DMA((n,))] if n else []

    def wrap(self, inner, n_in, n_out, n_scr, is_first, is_last):
        n_mv = len(self.arrays)
        if not n_mv:
            return inner

        def body(*refs):
            mv_src = refs[n_in:n_in + n_mv]
            mv_dst = refs[n_in + n_mv + n_out:n_in + 2 * n_mv + n_out]
            sems = refs[n_in + 2 * n_mv + n_out + n_scr:]

            def exchange(phase):
                if self.n_gather and self.two_level:
                    _gather_two_level(mv_src[:self.n_gather], mv_dst[:self.n_gather], sems, 0, phase)
                elif self.n_gather:
                    _exchange('gather', mv_src[:self.n_gather], mv_dst[:self.n_gather], sems, 0, phase)
                if n_mv > self.n_gather:
                    _exchange('scatter', mv_src[self.n_gather:], mv_dst[self.n_gather:], sems, self.n_gather, phase)

            pl.when(is_first())(functools.partial(exchange, 'start'))
            inner(*refs[:n_in], *refs[n_in + n_mv:n_in + n_mv + n_out],
                  *refs[n_in + 2 * n_mv + n_out:n_in + 2 * n_mv + n_out + n_scr])
            pl.when(is_last())(functools.partial(exchange, 'wait'))

        return body


def _rowcall(name, body, n_rows, tm, ins, outs, accs=(), scratch=(), gather=(), scatter=(), two_level=False):
    n_steps = n_rows // tm
    ride = _Ride(gather, scatter, two_level)
    body = ride.wrap(body, len(ins), len(outs) + len(accs), len(scratch), lambda: pl.program_id(0) == 0,
                     lambda: pl.program_id(0) == n_steps - 1)
    arrays, in_specs = [], []
    for a, kind in ins:
        arrays.append(a)
        if kind == 'row':
            in_specs.append(pl.BlockSpec((tm,) + a.shape[1:], lambda i, nd=a.ndim: (i,) + (0,) * (nd - 1)))
        elif kind == 'full':
            in_specs.append(pl.BlockSpec(a.shape, lambda i, nd=a.ndim: (0,) * nd))
        else:
            in_specs.append(kind)
    out_shape, out_specs = [], []
    for cols, dt in outs:
        out_shape.append(jax.ShapeDtypeStruct((n_rows, cols), dt))
        out_specs.append(pl.BlockSpec((tm, cols), lambda i: (i, 0)))
    for shp, dt in accs:
        if isinstance(dt, tuple):
            dt, spec = dt
        else:
            spec = pl.BlockSpec(shp, lambda i, nd=len(shp): (0,) * nd)
        out_shape.append(jax.ShapeDtypeStruct(shp, dt))
        out_specs.append(spec)
    return pl.pallas_call(functools.partial(body), grid=(n_steps,), in_specs=in_specs + ride.in_specs,
                          out_specs=out_specs + ride.out_specs, out_shape=out_shape + ride.out_shape,
                          scratch_shapes=list(scratch) + ride.scratch, name=name,
                          compiler_params=_params(1))(*arrays, *ride.arrays)


DW_ACC_BYTES = 12 * 1024 * 1024
DW_LHS_BYTES = 6 * 1024 * 1024


def _dw_tiles(n, ka, nn, a_itemsize):
    tn = max(t for t in range(128, min(nn, 1536) + 1, 128) if nn % t == 0)
    tka = max(t for t in range(128, ka + 1, 128) if ka % t == 0 and t * tn * 4 <= DW_ACC_BYTES)
    tm = min(n, 2048)
    while tm > 256 and tm * tka * a_itemsize > DW_LHS_BYTES:
        tm //= 2
    return tm, tka, tn


def _dwcall(name, a, dc, out_dtype, gather=(), scatter=()):
    n, ka = a.shape
    nn = dc.shape[1]
    tm, tka, tn = _dw_tiles(n, ka, nn, a.dtype.itemsize)
    n_i, n_j, n_k = n // tm, nn // tn, ka // tka
    ride = _Ride(gather, scatter)

    def body(a_ref, dc_ref, o_ref, acc_ref):
        i = pl.program_id(2)

        @pl.when(i == 0)
        def _():
            acc_ref[...] = jnp.zeros_like(acc_ref)

        acc_ref[...] += _dot_tn(a_ref[...].astype(BF16), dc_ref[...].astype(BF16))

        @pl.when(i == n_i - 1)
        def _():
            o_ref[...] = acc_ref[...].astype(o_ref.dtype)

    def at_step(k, j, i):
        return lambda: (pl.program_id(0) == k) & (pl.program_id(1) == j) & (pl.program_id(2) == i)

    body = ride.wrap(body, 2, 1, 1, at_step(0, 0, 0), at_step(n_k - 1, n_j - 1, n_i - 1))
    res = pl.pallas_call(
        body, grid=(n_k, n_j, n_i),
        in_specs=[pl.BlockSpec((tm, tka), lambda k, j, i: (i, k)), pl.BlockSpec((tm, tn), lambda k, j, i: (i, j))]
        + ride.in_specs,
        out_specs=[pl.BlockSpec((tka, tn), lambda k, j, i: (k, j))] + ride.out_specs,
        out_shape=[jax.ShapeDtypeStruct((ka, nn), out_dtype)] + ride.out_shape,
        scratch_shapes=[pltpu.VMEM((tka, tn), F32)] + ride.scratch, name=name, compiler_params=_params(3))(
            a, dc, *ride.arrays)
    return res if ride.arrays else res[0]


def _first_gather(first, later):
    n = len(later)
    vm = pl.BlockSpec(memory_space=pltpu.VMEM)

    def body(*refs):
        first_ref, later_refs = refs[0], refs[1:1 + n]
        out_ref, cast_refs = refs[1 + n], refs[2 + n:2 + 2 * n]
        stage = refs[2 + 2 * n]
        sems = refs[3 + 2 * n:]
        stage[...] = first_ref[...].astype(BF16)
        _gather_two_level([stage], [out_ref], sems, 0, 'start')
        for src, dst in zip(later_refs, cast_refs):
            dst[...] = src[...].astype(BF16)
        _gather_two_level([stage], [out_ref], sems, 0, 'wait')

    return pl.pallas_call(
        body, out_shape=[jax.ShapeDtypeStruct((N_DEV,) + first.shape, BF16)]
        + [jax.ShapeDtypeStruct(s.shape, BF16) for s in later],
        in_specs=[vm] * (1 + n), out_specs=[pl.BlockSpec(memory_space=pl.ANY)] + [vm] * n,
        scratch_shapes=[pltpu.VMEM(first.shape, BF16), pltpu.SemaphoreType.DMA((7,)), pltpu.SemaphoreType.DMA((7,)),
                        pltpu.SemaphoreType.DMA((1,))],
        name="gather_first", compiler_params=pltpu.CompilerParams(vmem_limit_bytes=V7X_VMEM_LIMIT))(first, *later)


def _sum_adamw(name, recv, w, m, v):
    r, c = w.shape
    steps = max(s for s in (4, 2, 1) if r % s == 0 and (r // s) % 16 == 0 or s == 1)

    def body(recv_ref, w_ref, m_ref, v_ref, g_ref, d_ref, nm_ref, nv_ref):
        g = recv_ref[0].astype(F32)
        for k in range(1, N_DEV):
            g = g + recv_ref[k].astype(F32)
        d, nm, nv = _adamw(w_ref[...], g, m_ref[...], v_ref[...])
        g_ref[...] = g
        d_ref[...] = d
        nm_ref[...] = nm
        nv_ref[...] = nv

    blk = pl.BlockSpec((r // steps, c), lambda i: (i, 0))
    return pl.pallas_call(
        body, grid=(steps,), out_shape=[jax.ShapeDtypeStruct((r, c), F32)] * 4,
        in_specs=[pl.BlockSpec((N_DEV, r // steps, c), lambda i: (0, i, 0)), blk, blk, blk], out_specs=[blk] * 4,
        name=name, compiler_params=_params(1))(recv, w, m, v)


def _gather_last(a):
    def body(a_ref, out_ref, send_sems, recv_sems, own_sems):
        sems = (send_sems, recv_sems, own_sems)
        _exchange('gather', [a_ref], [out_ref], sems, 0, 'start')
        _exchange('gather', [a_ref], [out_ref], sems, 0, 'wait')

    vm = pl.BlockSpec(memory_space=pltpu.VMEM)
    return pl.pallas_call(
        body, out_shape=jax.ShapeDtypeStruct((N_DEV,) + a.shape, a.dtype), in_specs=[vm], out_specs=vm,
        scratch_shapes=[pltpu.SemaphoreType.DMA((7,)), pltpu.SemaphoreType.DMA((7,)), pltpu.SemaphoreType.DMA((1,))],
        name="gather_last", compiler_params=pltpu.CompilerParams(vmem_limit_bytes=V7X_VMEM_LIMIT))(a)


def _small_adamw(entries, gathered, loss_at):
    n, ng = len(entries), len(gathered)

    def body(*refs):
        g_refs, wmv = refs[:ng], refs[ng:ng + 3 * n]
        outs, loss_out = refs[ng + 3 * n:ng + 7 * n], refs[ng + 7 * n]

        def total(ref, r):
            if r is None:
                t = ref[0]
                for k in range(1, N_DEV):
                    t = t + ref[k]
                return t[None]
            t = ref[0, r:r + 1, :]
            for k in range(1, N_DEV):
                t = t + ref[k, r:r + 1, :]
            return t

        for j, (_, _, _, gi, r) in enumerate(entries):
            g = total(g_refs[gi], r)
            d, nm, nv = _adamw(wmv[3 * j][...], g, wmv[3 * j + 1][...], wmv[3 * j + 2][...])
            for ref, val in zip(outs[4 * j:4 * j + 4], (g, d, nm, nv)):
                ref[...] = val
        loss_out[...] = total(g_refs[loss_at[0]], loss_at[1])[:, :128]

    vm = pl.BlockSpec(memory_space=pltpu.VMEM)
    out_shape, arrays = [], list(gathered)
    for w, m, v, _, _ in entries:
        out_shape += [jax.ShapeDtypeStruct(w.shape, F32)] * 4
        arrays += [w, m, v]
    out_shape.append(jax.ShapeDtypeStruct((1, 128), F32))
    res = pl.pallas_call(
        body, out_shape=out_shape, in_specs=[vm] * len(arrays), out_specs=[vm] * len(out_shape),
        name="small_adamw", compiler_params=pltpu.CompilerParams(vmem_limit_bytes=V7X_VMEM_LIMIT))(*arrays)
    return [res[4 * j:4 * j + 4] for j in range(n)], res[4 * n]


def _blockdiag8(t):
    g, per = t.shape[0], 8
    eye = jnp.eye(per, dtype=F32)
    t = t.reshape(g // per, per, t.shape[1], t.shape[2])
    return (t[:, :, :, None, :] * eye[None, :, None, :, None]).reshape(g // per, per * t.shape[2], per * t.shape[3])


def _s5_out_tables(c_re, c_im):
    return _blockdiag8(c_re.transpose(0, 2, 1)), _blockdiag8(c_im.transpose(0, 2, 1))


def _s5_in_tables(lam_re, lam_im, log_step, b_re, b_im):
    step = jnp.exp(log_step)[:, None]
    mag = jnp.exp(lam_re * step)
    ab_re = mag * jnp.cos(lam_im * step)
    ab_im = mag * jnp.sin(lam_im * step)
    den = lam_re * lam_re + lam_im * lam_im
    nr = ab_re - 1.0
    co_re = (nr * lam_re + ab_im * lam_im) / den
    co_im = (ab_im * lam_re - nr * lam_im) / den
    bb_re = co_re[..., None] * b_re - co_im[..., None] * b_im
    bb_im = co_re[..., None] * b_im + co_im[..., None] * b_re
    return (ab_re.reshape(1, -1), ab_im.reshape(1, -1), _blockdiag8(bb_re.transpose(0, 2, 1)),
            _blockdiag8(bb_im.transpose(0, 2, 1)))


def _scan_passes(x_s, a_ref, cr_ref, ci_ref, reverse, re, im):
    tc, nc = x_s.shape[:2]
    ln = SCAN_LANES
    n_sq = int(math.log2(tc))
    assert 2 ** n_sq == tc
    ar = jnp.broadcast_to(a_ref[:, re], (nc, ln))
    ai = jnp.broadcast_to(a_ref[:, im], (nc, ln))
    zero = jnp.zeros((nc, ln), F32)

    def at(t):
        return tc - 1 - t if reverse else t

    def local(t, carry):
        sr, si = carry
        j = at(t)
        nr = ar * sr - ai * si + x_s[j, :, re]
        ni = ar * si + ai * sr + x_s[j, :, im]
        x_s[j, :, re] = nr
        x_s[j, :, im] = ni
        return nr, ni

    lr, li = lax.fori_loop(0, tc, local, (zero, zero))
    pr, pi = a_ref[:, re], a_ref[:, im]
    for _ in range(n_sq):
        pr, pi = pr * pr - pi * pi, 2.0 * (pr * pi)
    cr_ref[...] = lr
    ci_ref[...] = li
    tr = jnp.zeros((1, ln), F32)
    ti = jnp.zeros((1, ln), F32)
    for c in (range(nc - 1, -1, -1) if reverse else range(nc)):
        l_r = cr_ref[c:c + 1, :]
        l_i = ci_ref[c:c + 1, :]
        cr_ref[c:c + 1, :] = tr
        ci_ref[c:c + 1, :] = ti
        tr, ti = pr * tr - pi * ti + l_r, pr * ti + pi * tr + l_i

    def second_pass(on_fixed):
        def fixup(t, carry):
            qr, qi, acc = carry
            j = at(t)
            qr, qi = ar * qr - ai * qi, ar * qi + ai * qr
            gr = x_s[j, :, re] + qr
            gi = x_s[j, :, im] + qi
            x_s[j, :, re] = gr
            x_s[j, :, im] = gi
            return qr, qi, on_fixed(j, gr, gi, acc)

        return fixup

    return second_pass, (cr_ref[...], ci_ref[...]), zero


def _s5_cat_tables(a_re, a_im, bb_re, bb_im, cc_re, cc_im):
    lb = S5_BLOCK_LANES
    n_blk, kq, nq = bb_re.shape
    sub = nq // lb

    def lanes(re, im):
        lead = re.shape[:-1]
        both = jnp.stack([re.reshape(lead + (-1, lb)), im.reshape(lead + (-1, lb))], axis=-2)
        return both.reshape(lead + (-1,))

    cc = jnp.stack([cc_re.reshape(n_blk, sub, lb, kq), -cc_im.reshape(n_blk, sub, lb, kq)], axis=2)
    return lanes(a_re, a_im), lanes(bb_re, bb_im), cc.reshape(n_blk, sub * 2 * lb, kq)


def _s5_uncat(t, axis):
    lb = S5_BLOCK_LANES
    shp = t.shape
    t = t.reshape(shp[:axis] + (-1, 2, lb) + shp[axis + 1:])
    re, im = jnp.take(t, 0, axis=axis + 1), jnp.take(t, 1, axis=axis + 1)
    return re.reshape(shp[:axis] + (-1,) + shp[axis + 1:]), im.reshape(shp[:axis] + (-1,) + shp[axis + 1:])


def _grid2_edges(n0, n1):
    return (lambda: (pl.program_id(0) == 0) & (pl.program_id(1) == 0),
            lambda: (pl.program_id(0) == n0 - 1) & (pl.program_id(1) == n1 - 1))


def _s5_scan_fwd(u5p, a_cat, bb_cat, cc_cat, nb, gather=()):
    n, dh = u5p.shape
    n_blk, kq, cols = bb_cat.shape
    lb, ln, nc = S5_BLOCK_LANES, SCAN_LANES, SCAN_CHUNKS
    sub = cols // (2 * lb)
    rows = n // nb
    tc = rows // nc
    n_l = n_blk * sub

    def body(u_ref, a_ref, b_ref, c_ref, s_ref, y_ref, x_s, car_r, car_i):
        x_s[...] = _dot(u_ref[...].astype(BF16), b_ref[0]).reshape(tc, nc, 2 * lb)
        for h in range(lb // ln):
            re, im = slice(h * ln, (h + 1) * ln), slice(lb + h * ln, lb + (h + 1) * ln)

            def keep(j, gr, gi, acc, re=re, im=im):
                s_ref[0, j, :, re] = gr.astype(BF16)
                s_ref[0, j, :, im] = gi.astype(BF16)
                return acc

            second_pass, start, zero = _scan_passes(x_s, a_ref, car_r, car_i, False, re, im)
            lax.fori_loop(0, tc, second_pass(keep), (*start, zero))
        y = _dot(x_s[...].reshape(rows, 2 * lb).astype(BF16), c_ref[0])

        @pl.when(pl.program_id(1) % sub == 0)
        def _():
            y_ref[...] = y

        @pl.when(pl.program_id(1) % sub != 0)
        def _():
            y_ref[...] += y

    row_blk = pl.BlockSpec((rows, kq), lambda b, l: (b, l // sub))
    ride = _Ride(gather, two_level=True)
    return pl.pallas_call(
        ride.wrap(body, 4, 2, 3, *_grid2_edges(nb, n_l)), grid=(nb, n_l),
        in_specs=[row_blk, pl.BlockSpec((1, 2 * lb), lambda b, l: (0, l)),
                  pl.BlockSpec((1, kq, 2 * lb), lambda b, l: (l // sub, 0, l % sub)),
                  pl.BlockSpec((1, 2 * lb, kq), lambda b, l: (l // sub, l % sub, 0))] + ride.in_specs,
        out_specs=[pl.BlockSpec((1, tc, nc, 2 * lb), lambda b, l: (b, 0, 0, l)), row_blk] + ride.out_specs,
        out_shape=[jax.ShapeDtypeStruct((nb, tc, nc, n_l * 2 * lb), BF16), jax.ShapeDtypeStruct((n, dh), F32)]
        + ride.out_shape,
        scratch_shapes=[pltpu.VMEM((tc, nc, 2 * lb), F32)] + [pltpu.VMEM((nc, ln), F32)] * 2 + ride.scratch,
        name="s5_scan_fwd", compiler_params=_params(2))(u5p, a_cat, bb_cat, cc_cat, *ride.arrays)


def _s5_scan_bwd(dy_pre, u5p, s_cat, a_cat_conj, bb_cat, cc_cat, d_row, gather=(), scatter=()):
    n, dh = u5p.shape
    n_blk, kq, cols = bb_cat.shape
    nb, tc, nc, _ = s_cat.shape
    lb, ln = S5_BLOCK_LANES, SCAN_LANES
    sub = cols // (2 * lb)
    rows = n // nb
    n_l = n_blk * sub

    def body(dy_ref, u_ref, s_ref, a_ref, b_ref, c_ref, d_ref, du_ref, db_ref, dc_ref, da_ref, x_s, car_r, car_i):
        dyb = dy_ref[...].astype(BF16)
        x_s[...] = _dot_nt(dyb, c_ref[0]).reshape(tc, nc, 2 * lb)
        for h in range(lb // ln):
            re, im = slice(h * ln, (h + 1) * ln), slice(lb + h * ln, lb + (h + 1) * ln)

            def with_state_before(j, gr, gi, acc, re=re, im=im):
                dr, di = acc
                pr_, pi_ = s_ref[0, j - 1, :, re].astype(F32), s_ref[0, j - 1, :, im].astype(F32)
                return dr + (pr_ * gr + pi_ * gi), di + (pr_ * gi - pi_ * gr)

            second_pass, start, zero = _scan_passes(x_s, a_ref, car_r, car_i, True, re, im)
            carry = lax.fori_loop(0, tc - 1, second_pass(with_state_before), (*start, (zero, zero)))
            row = lax.broadcasted_iota(jnp.int32, (nc, ln), 0)
            pr_ = jnp.where(row == 0, 0.0, pltpu.roll(s_ref[0, tc - 1, :, re].astype(F32), 1, 0))
            pi_ = jnp.where(row == 0, 0.0, pltpu.roll(s_ref[0, tc - 1, :, im].astype(F32), 1, 0))

            def at_first_time(j, gr, gi, acc, pr_=pr_, pi_=pi_):
                return acc[0] + (pr_ * gr + pi_ * gi), acc[1] + (pr_ * gi - pi_ * gr)

            _, _, (dr, di) = second_pass(at_first_time)(tc - 1, carry)
            da_ref[0, :, re] = jnp.sum(dr, axis=0, keepdims=True)
            da_ref[0, :, im] = jnp.sum(di, axis=0, keepdims=True)

        gb = x_s[...].reshape(rows, 2 * lb).astype(BF16)
        du = _dot_nt(gb, b_ref[0])

        @pl.when(pl.program_id(1) % sub == 0)
        def _():
            du_ref[...] = du + d_ref[...] * dy_ref[...]

        @pl.when(pl.program_id(1) % sub != 0)
        def _():
            du_ref[...] += du

        db_ref[0, 0] = _dot_tn(u_ref[...].astype(BF16), gb)
        dc_ref[0, 0] = _dot_tn(s_ref[0].reshape(rows, 2 * lb), dyb)

    row_blk = pl.BlockSpec((rows, kq), lambda b, l: (b, l // sub))
    in_map = pl.BlockSpec((1, kq, 2 * lb), lambda b, l: (l // sub, 0, l % sub))
    out_map = pl.BlockSpec((1, 2 * lb, kq), lambda b, l: (l // sub, l % sub, 0))
    ride = _Ride(gather, scatter)
    return pl.pallas_call(
        ride.wrap(body, 7, 4, 3, *_grid2_edges(nb, n_l)), grid=(nb, n_l),
        in_specs=[row_blk, row_blk, pl.BlockSpec((1, tc, nc, 2 * lb), lambda b, l: (b, 0, 0, l)),
                  pl.BlockSpec((1, 2 * lb), lambda b, l: (0, l)), in_map, out_map,
                  pl.BlockSpec((1, kq), lambda b, l: (0, l // sub))] + ride.in_specs,
        out_specs=[row_blk, pl.BlockSpec((1, 1, kq, 2 * lb), lambda b, l: (b, l // sub, 0, l % sub)),
                   pl.BlockSpec((1, 1, 2 * lb, kq), lambda b, l: (b, l // sub, l % sub, 0)),
                   pl.BlockSpec((1, 1, 2 * lb), lambda b, l: (b, 0, l))] + ride.out_specs,
        out_shape=[jax.ShapeDtypeStruct((n, dh), F32), jax.ShapeDtypeStruct((nb, n_blk, kq, cols), F32),
                   jax.ShapeDtypeStruct((nb, n_blk, cols, kq), F32), jax.ShapeDtypeStruct((nb, 1, n_l * 2 * lb), F32)]
        + ride.out_shape,
        scratch_shapes=[pltpu.VMEM((tc, nc, 2 * lb), F32)] + [pltpu.VMEM((nc, ln), F32)] * 2 + ride.scratch,
        name="s5_scan_bwd", compiler_params=_params(2))(dy_pre, u5p, s_cat, a_cat_conj, bb_cat, cc_cat, d_row,
                                                        *ride.arrays)


def kernel(x, mem, g_mix_pre, w_in, gm_ln_g, gm_ln_b, gm_w_s, gm_b_s, s5_lam_re, s5_lam_im, s5_log_step, s5_b_re, s5_b_im, s5_c_re, s5_c_im, s5_d, s5_w_glu, w_br_gm, w_br_s5, w_mix_out, g_mix_post, g_ca_pre, g_mem, ca_w_q, ca_w_kv, ca_w_o, g_ca_post, g_ffn_pre, ffn_w_gu, ffn_w_down, g_ffn_post, loss_target, m_g_mix_pre, m_w_in, m_gm_ln_g, m_gm_ln_b, m_gm_w_s, m_gm_b_s, m_s5_lam_re, m_s5_lam_im, m_s5_log_step, m_s5_b_re, m_s5_b_im, m_s5_c_re, m_s5_c_im, m_s5_d, m_s5_w_glu, m_w_br_gm, m_w_br_s5, m_w_mix_out, m_g_mix_post, m_g_ca_pre, m_g_mem, m_ca_w_q, m_ca_w_kv, m_ca_w_o, m_g_ca_post, m_g_ffn_pre, m_ffn_w_gu, m_ffn_w_down, m_g_ffn_post, v_g_mix_pre, v_w_in, v_gm_ln_g, v_gm_ln_b, v_gm_w_s, v_gm_b_s, v_s5_lam_re, v_s5_lam_im, v_s5_log_step, v_s5_b_re, v_s5_b_im, v_s5_c_re, v_s5_c_im, v_s5_d, v_s5_w_glu, v_w_br_gm, v_w_br_s5, v_w_mix_out, v_g_mix_post, v_g_ca_pre, v_g_mem, v_ca_w_q, v_ca_w_kv, v_ca_w_o, v_g_ca_post, v_g_ffn_pre, v_ffn_w_gu, v_ffn_w_down, v_g_ffn_post):
    args = locals()
    W = {n: args[n][0] for n in WEIGHTS}
    M = {n: args['m_' + n][0] for n in WEIGHTS}
    V = {n: args['v_' + n][0] for n in WEIGHTS}
    for k, kind in SHARDED.items():
        if kind == 'colT':
            W[k], M[k], V[k] = W[k].T, M[k].T, V[k].T

    nb, seq, d = x.shape
    n = nb * seq
    tm = min(ROW_TILE, n)
    tmw = min(WIDE_ROW_TILE, n)
    tmf = min(FFN_ROW_TILE, n)
    nmem = mem.shape[1]
    d2, dh = 2 * d, d // 2
    hd = d // CA_HEADS
    x2d = x.reshape(n, d)
    tgt = loss_target.reshape(n, d)
    mem2d = mem.reshape(nb * nmem, d)

    def row(v):
        return v.reshape(1, -1)

    names = list(SHARDED)
    later = [k for k in names if k != 'w_in']
    first_g, *casts = _first_gather(W['w_in'], [W[k] for k in later])
    shard_b = dict(zip(later, casts))
    half = shard_b['ffn_w_gu'].shape[0] // 2
    gu_halves = [shard_b['ffn_w_gu'][:half], shard_b['ffn_w_gu'][half:]]

    def assemble(k, gth):
        r, c = gth.shape[1:]
        return gth.transpose(1, 0, 2).reshape(r, N_DEV * c) if SHARDED[k] == 'col' else gth.reshape(N_DEV * r, c)

    w_in_t = assemble('w_in', first_g)
    ffh = N_DEV * W['ffn_w_down'].shape[0]

    (a_re, a_im, bb_re, bb_im), s5_in_vjp = jax.vjp(
        _s5_in_tables, W['s5_lam_re'], W['s5_lam_im'], W['s5_log_step'], W['s5_b_re'], W['s5_b_im'])
    (cc_re, cc_im), s5_out_vjp = jax.vjp(_s5_out_tables, W['s5_c_re'], W['s5_c_im'])
    bb_re_b, bb_im_b, cc_re_b, cc_im_b = (t.astype(BF16) for t in (bb_re, bb_im, cc_re, cc_im))
    s5_d_row = row(W['s5_d'])
    tril = jnp.tril(jnp.ones((GM_CHUNK, GM_CHUNK), bool))
    w_s = jnp.where(tril[None], W['gm_w_s'], 0.0).astype(BF16)
    w_s_t = w_s.transpose(0, 2, 1)
    gm_bias = jnp.repeat(W['gm_b_s'].T, d // GM_GROUPS, axis=1)

    def in_proj_body(x_ref, g_ref, w_ref, zgm_ref, u5_ref, zga_ref, zgb_ref, h_ref):
        hb = _rms(x_ref[...], g_ref[...]).astype(BF16)
        h_ref[...] = hb
        for lo in range(0, w_ref.shape[0], 512):
            acc = _dot_nt(hb, w_ref[lo:lo + 512, :])
            if lo < d2:
                zgm_ref[:, lo:lo + 512] = acc.astype(BF16)
            elif lo < d2 + dh:
                u5_ref[...] = acc
            elif lo < d2 + dh + d:
                zga_ref[:, lo - d2 - dh:lo - d2 - dh + 512] = acc.astype(BF16)
            else:
                zgb_ref[:, lo - d2 - dh - d:lo - d2 - dh - d + 512] = acc.astype(BF16)

    ride = ['s5_w_glu', 'w_br_gm', 'w_br_s5']
    z_gm, u5, z_ga, z_gb, h0, *got = _rowcall(
        "in_proj", in_proj_body, n, tm, [(x2d, 'row'), (row(W['g_mix_pre']), 'full'), (w_in_t, 'full')],
        [(d2, BF16), (dh, F32), (d, BF16), (d, BF16), (d, BF16)], gather=[shard_b[k] for k in ride] + [gu_halves[0]])
    w_glu_f, w_brgm_f, w_brs5_f = (assemble(k, g) for k, g in zip(ride, got))
    g_gu0 = got[len(ride)]

    gw = d // GM_GROUPS

    def gm_fwd_body(z_ref, lng_ref, lnb_ref, ws_ref, bias_ref, y_ref, u_ref, vh_ref, dgelu_ref, rstd_ref):
        zg, dgelu = _gelu_and_grad(z_ref[...].astype(F32))
        dgelu_ref[...] = dgelu.astype(BF16)
        u, v = zg[:, :d], zg[:, d:]
        vc = v - jnp.mean(v, axis=-1, keepdims=True)
        rstd = lax.rsqrt(jnp.mean(vc * vc, axis=-1, keepdims=True) + EPS)
        vhat = vc * rstd
        u_ref[...] = u.astype(BF16)
        vh_ref[...] = vhat.astype(BF16)
        rstd_ref[...] = jnp.broadcast_to(rstd, rstd_ref.shape)
        vnb = (vhat * lng_ref[...] + lnb_ref[...]).astype(BF16)
        sv = jnp.concatenate([_dot(ws_ref[g], vnb[:, g * gw:(g + 1) * gw]) for g in range(GM_GROUPS)], axis=1)
        y_ref[...] = (u * (sv + bias_ref[...])).astype(BF16)

    y_gm, gm_u, gm_vhat, gm_dgelu, gm_rstd, g_mix, g_q = _rowcall(
        "gmlp_fwd", gm_fwd_body, n, GM_CHUNK,
        [(z_gm, 'row'), (row(W['gm_ln_g']), 'full'), (row(W['gm_ln_b']), 'full'), (w_s, 'full'), (gm_bias, 'full')],
        [(d, BF16), (d, BF16), (d, BF16), (d2, BF16), (128, F32)], gather=[shard_b['w_mix_out'], shard_b['ca_w_q']],
        two_level=True)
    w_mix_f, w_q_f = assemble('w_mix_out', g_mix), assemble('ca_w_q', g_q)

    tc = seq // SCAN_CHUNKS
    lt = S5_GROUPS * S5_STATE

    def to_scan_order(t):
        return t.reshape(nb, SCAN_CHUNKS, tc, t.shape[-1]).transpose(0, 2, 1, 3).reshape(n, t.shape[-1])

    def from_scan_order(t):
        return t.reshape(nb, tc, SCAN_CHUNKS, t.shape[-1]).transpose(0, 2, 1, 3).reshape(n, t.shape[-1])

    u5p = to_scan_order(u5)
    a_cat, bb_cat, cc_cat = _s5_cat_tables(a_re, a_im, bb_re_b, bb_im_b, cc_re_b, cc_im_b)
    a_cat_conj = _s5_cat_tables(a_re, -a_im, bb_re_b, bb_im_b, cc_re_b, cc_im_b)[0]
    s_cat, y_lin, g_kv, g_o = _s5_scan_fwd(u5p, a_cat, bb_cat, cc_cat, nb,
                                           gather=[shard_b['ca_w_kv'], shard_b['ca_w_o']])
    w_kv_f, w_o_f = assemble('ca_w_kv', g_kv), assemble('ca_w_o', g_o)

    def s5_out_body(yl_ref, u_ref, d_ref, wg_ref, ypre_ref, gate_ref, y_ref):
        ypre = yl_ref[...] + d_ref[...] * u_ref[...]
        ypre_ref[...] = ypre
        yg = _gelu(ypre)
        gate = _dot(yg.astype(BF16), wg_ref[...])
        gate_ref[...] = gate
        y_ref[...] = (yg * _sigmoid(gate)).astype(BF16)

    tm5 = min(1024, n)
    y_pre, gate, y_s5p = _rowcall(
        "s5_out", s5_out_body, n, tm5, [(y_lin, 'row'), (u5p, 'row'), (s5_d_row, 'full'), (w_glu_f, 'full')],
        [(dh, F32), (dh, F32), (dh, BF16)])
    y_s5 = from_scan_order(y_s5p)

    def merge_body(ygm_ref, ys5_ref, zga_ref, zgb_ref, wa_ref, wb_ref, pa_ref, pb_ref, mg_ref):
        pa = _dot(ygm_ref[...], wa_ref[...])
        pb = _dot(ys5_ref[...], wb_ref[...])
        pa_ref[...] = pa.astype(BF16)
        pb_ref[...] = pb.astype(BF16)
        mg_ref[...] = (_sigmoid(zga_ref[...].astype(F32)) * pa + _sigmoid(zgb_ref[...].astype(F32)) * pb).astype(BF16)

    p_a, p_b, merged, g_down = _rowcall(
        "merge", merge_body, n, tmw,
        [(y_gm, 'row'), (y_s5, 'row'), (z_ga, 'row'), (z_gb, 'row'), (w_brgm_f, 'full'), (w_brs5_f, 'full')],
        [(d, BF16), (d, BF16), (d, BF16)], gather=[shard_b['ffn_w_down']], two_level=True)
    w_down_f = assemble('ffn_w_down', g_down)

    def close_sublayer(name, a_in, w_out, x_res, g_post, g_next, w_next=None, gather=(), two_level=False):
        def body(*refs):
            a_ref, w_ref, x_ref, gp_ref, gn_ref = refs[:5]
            rest = refs[5:]
            if w_next is not None:
                wn_ref, rest = rest[0], rest[1:]
            o_ref, xo_ref, h_ref = rest[:3]
            o = _dot(a_ref[...], w_ref[...])
            o_ref[...] = o.astype(BF16)
            xo = x_ref[...] + _rms(o, gp_ref[...])
            xo_ref[...] = xo
            hb = _rms(xo, gn_ref[...]).astype(BF16)
            h_ref[...] = hb
            if w_next is not None:
                rest[3][...] = _dot(hb, wn_ref[...]).astype(BF16)

        ins = [(a_in, 'row'), (w_out, 'full'), (x_res, 'row'), (row(g_post), 'full'), (row(g_next), 'full')]
        outs = [(d, BF16), (d, F32), (d, BF16)]
        if w_next is not None:
            ins.append((w_next, 'full'))
            outs.append((w_next.shape[1], BF16))
        return _rowcall(name, body, n, tmw, ins, outs, gather=gather, two_level=two_level)

    o1, x1, hc, q = close_sublayer("mix_out", merged, w_mix_f, x2d, W['g_mix_post'], W['g_ca_pre'], w_q_f)

    tmm = min(ROW_TILE, nb * nmem)

    def memkv_body(m_ref, g_ref, w_ref, mn_ref, k_ref, v_ref):
        mnb = _rms(m_ref[...], g_ref[...]).astype(BF16)
        mn_ref[...] = mnb
        k_ref[...] = _dot(mnb, w_ref[:, :d]).astype(BF16)
        v_ref[...] = _dot(mnb, w_ref[:, d:]).astype(BF16)

    mem_n, k_mem, v_mem = _rowcall("mem_kv", memkv_body, nb * nmem, tmm,
                                   [(mem2d, 'row'), (row(W['g_mem']), 'full'), (w_kv_f, 'full')],
                                   [(d, BF16), (d, BF16), (d, BF16)])

    tiles_per_ex = seq // tm
    kv_spec = pl.BlockSpec((nmem, d), lambda i: (i // tiles_per_ex, 0))
    scale = hd ** -0.5

    def softmax_rows(qh, kh):
        s = _dot_nt(qh, kh) * scale
        e = jnp.exp(s - jnp.max(s, axis=-1, keepdims=True))
        return e / jnp.sum(e, axis=-1, keepdims=True)

    def attn_body(q_ref, k_ref, v_ref, o_ref):
        for h in range(CA_HEADS):
            sl = slice(h * hd, (h + 1) * hd)
            p = softmax_rows(q_ref[:, sl], k_ref[:, sl])
            o_ref[:, sl] = _dot(p.astype(BF16), v_ref[:, sl]).astype(BF16)

    (att,) = _rowcall("attn_fwd", attn_body, n, tm, [(q, 'row'), (k_mem, kv_spec), (v_mem, kv_spec)], [(d, BF16)])

    o2, x2, hf, g_gu1 = close_sublayer("attn_out", att, w_o_f, x1, W['g_ca_post'], W['g_ffn_pre'],
                                       gather=[gu_halves[1]], two_level=True)
    w_gu_t = jnp.stack([g_gu0, g_gu1], axis=1).reshape(2 * ffh, d)

    ck = 256

    def ffn_up_body(h_ref, w_ref, gu_ref, a_ref):
        hb = h_ref[...]
        for lo in range(0, ffh, ck):
            gt = _dot_nt(hb, w_ref[lo:lo + ck, :])
            ut = _dot_nt(hb, w_ref[ffh + lo:ffh + lo + ck, :])
            gu_ref[:, lo:lo + ck] = gt.astype(BF16)
            gu_ref[:, ffh + lo:ffh + lo + ck] = ut.astype(BF16)
            a_ref[:, lo:lo + ck] = ((gt * _sigmoid(gt)) * ut).astype(BF16)

    gu, act = _rowcall("ffn_up", ffn_up_body, n, tm, [(hf, 'row'), (w_gu_t, 'full')], [(2 * ffh, BF16), (ffh, BF16)])

    def ffn_down_body(a_ref, w_ref, x_ref, t_ref, g_ref, dx_ref, do_ref, loss_ref, dg_ref):
        i = pl.program_id(0)

        @pl.when(i == 0)
        def _():
            loss_ref[...] = jnp.zeros_like(loss_ref)
            dg_ref[...] = jnp.zeros_like(dg_ref)

        o = _dot(a_ref[...], w_ref[...])
        diff = x_ref[...] + _rms(o, g_ref[...]) - t_ref[...]
        loss_ref[...] += jnp.full(loss_ref.shape, 0.5 / d, F32) * jnp.sum(diff * diff)
        dx = diff * (1.0 / d)
        dx_ref[...] = dx
        do, dg = _rms_bwd(o, g_ref[...], dx)
        do_ref[...] = do.astype(BF16)
        dg_ref[...] += dg

    dx3, do3, loss_part, dg_ffn_post = _rowcall(
        "ffn_down_loss", ffn_down_body, n, tmf,
        [(act, 'row'), (w_down_f, 'full'), (x2, 'row'), (tgt, 'row'), (row(W['g_ffn_post']), 'full')],
        [(d, F32), (d, BF16)], accs=[((1, 128), F32), ((1, d), F32)])

    G = {'g_ffn_post': dg_ffn_post}
    RECV = {}

    def parts_of(k, gfull):
        r, c = W[k].shape
        return gfull.reshape(r, N_DEV, c).transpose(1, 0, 2) if SHARDED[k] == 'col' else gfull.reshape(N_DEV, r, c)

    p_down = parts_of('ffn_w_down', _dwcall("dw_ffn_down", act, do3, BF16))

    def ffn_act_bwd_body(do_ref, w_ref, gu_ref, dgu_ref):
        dob = do_ref[...]
        for lo in range(0, ffh, ck):
            da = _dot_nt(dob, w_ref[lo:lo + ck, :])
            gt = gu_ref[:, lo:lo + ck].astype(F32)
            ut = gu_ref[:, ffh + lo:ffh + lo + ck].astype(F32)
            sg = _sigmoid(gt)
            dgu_ref[:, lo:lo + ck] = (da * ut * (sg * (1.0 + gt * (1.0 - sg)))).astype(BF16)
            dgu_ref[:, ffh + lo:ffh + lo + ck] = (da * (gt * sg)).astype(BF16)

    dgu, RECV['ffn_w_down'] = _rowcall("ffn_act_bwd", ffn_act_bwd_body, n, tmf,
                                       [(do3, 'row'), (w_down_f, 'full'), (gu, 'row')], [(2 * ffh, BF16)],
                                       scatter=[p_down])
    p_gu = parts_of('ffn_w_gu', _dwcall("dw_ffn_gu", dgu, hf, BF16))

    def open_sublayer(name, pieces, w_full, x_in, g_pre, dx_up, o_prev=None, g_post_prev=None, scatter=(), gather=(),
                      w_t=False):
        n_p = len(pieces)
        second = o_prev is not None

        def body(*refs):
            dc_refs, (w_ref, x_ref, g_ref, dxu_ref), rest = refs[:n_p], refs[n_p:n_p + 4], refs[n_p + 4:]
            if second:
                (op_ref, gp_ref), rest = rest[:2], rest[2:]
            i = pl.program_id(0)
            dhid = None
            for dc_ref, (_, lo, hi) in zip(dc_refs, pieces):
                part = _dot(dc_ref[...], w_ref[lo:hi, :]) if w_t else _dot_nt(dc_ref[...], w_ref[:, lo:hi])
                dhid = part if dhid is None else dhid + part
            dxn, dg = _rms_bwd(x_ref[...], g_ref[...], dhid)
            dx = dxu_ref[...] + dxn
            if second:
                dx_ref, do_ref, dg_ref, dg2_ref = rest
            else:
                dx_ref, dg_ref = rest

            @pl.when(i == 0)
            def _():
                dg_ref[...] = jnp.zeros_like(dg_ref)
                if second:
                    dg2_ref[...] = jnp.zeros_like(dg2_ref)

            dx_ref[...] = dx
            dg_ref[...] += dg
            if second:
                do, dg2 = _rms_bwd(op_ref[...].astype(F32), gp_ref[...], dx)
                do_ref[...] = do.astype(BF16)
                dg2_ref[...] += dg2

        ins = [(p[0], 'row') for p in pieces] + [(w_full, 'full'), (x_in, 'row'), (row(g_pre), 'full'), (dx_up, 'row')]
        outs = [(d, F32)]
        accs = [((1, d), F32)]
        if second:
            ins += [(o_prev, 'row'), (row(g_post_prev), 'full')]
            outs.append((d, BF16))
            accs.append(((1, d), F32))
        light = w_full.size <= d * d
        res = _rowcall(name, body, n, tmw if light else tm, ins, outs, accs=accs, scatter=scatter, gather=gather)
        if second:
            dx, do, dg, dg2 = res[:4]
            return dx, dg, do, dg2, res[4:]
        return res[0], res[1], res[2:]

    dx2, G['g_ffn_pre'], do2, G['g_ca_post'], _ = open_sublayer(
        "ffn_in_bwd", [(dgu, 0, 2 * ffh)], w_gu_t, x2, W['g_ffn_pre'], dx3, o2, W['g_ca_post'], w_t=True)
    p_o = parts_of('ca_w_o', _dwcall("dw_ca_o", att, do2, BF16))

    def attn_bwd_body(q_ref, k_ref, v_ref, do_ref, wo_ref, dq_ref, dk_ref, dv_ref):
        i = pl.program_id(0)

        @pl.when(i % tiles_per_ex == 0)
        def _():
            dk_ref[...] = jnp.zeros_like(dk_ref)
            dv_ref[...] = jnp.zeros_like(dv_ref)

        d_att = _dot_nt(do_ref[...], wo_ref[...]).astype(BF16)
        for h in range(CA_HEADS):
            sl = slice(h * hd, (h + 1) * hd)
            qh, kh, vh, dah = q_ref[:, sl], k_ref[:, sl], v_ref[:, sl], d_att[:, sl]
            s_t = _dot_nt(kh, qh) * scale
            e_t = jnp.exp(s_t - jnp.max(s_t, axis=0, keepdims=True))
            p_t = e_t / jnp.sum(e_t, axis=0, keepdims=True)
            dp_t = _dot_nt(vh, dah)
            ds_t = (p_t * (dp_t - jnp.sum(p_t * dp_t, axis=0, keepdims=True)) * scale).astype(BF16)
            dq_ref[:, sl] = _dot_tn(ds_t, kh).astype(BF16)
            dk_ref[:, sl] += _dot(ds_t, qh)
            dv_ref[:, sl] += _dot(p_t.astype(BF16), dah)

    kv_acc = ((nb * nmem, d), (F32, pl.BlockSpec((nmem, d), lambda i: (i // tiles_per_ex, 0))))
    dq, dk_mem, dv_mem, RECV['ca_w_o'] = _rowcall(
        "attn_bwd", attn_bwd_body, n, tm,
        [(q, 'row'), (k_mem, kv_spec), (v_mem, kv_spec), (do2, 'row'), (w_o_f, 'full')], [(d, BF16)],
        accs=[kv_acc, kv_acc], scatter=[p_o])
    p_q = parts_of('ca_w_q', _dwcall("dw_ca_q", hc, dq, BF16))

    def memkv_bwd_body(dk_ref, dv_ref, m_ref, g_ref, w_ref, dkv_ref, dg_ref):
        i = pl.program_id(0)

        @pl.when(i == 0)
        def _():
            dg_ref[...] = jnp.zeros_like(dg_ref)

        dkb, dvb = dk_ref[...].astype(BF16), dv_ref[...].astype(BF16)
        dkv_ref[:, :d] = dkb
        dkv_ref[:, d:] = dvb
        dmn = _dot_nt(dkb, w_ref[:, :d]) + _dot_nt(dvb, w_ref[:, d:])
        _, dg = _rms_bwd(m_ref[...], g_ref[...], dmn)
        dg_ref[...] += dg

    dkv, G['g_mem'] = _rowcall(
        "mem_kv_bwd", memkv_bwd_body, nb * nmem, tmm,
        [(dk_mem, 'row'), (dv_mem, 'row'), (mem2d, 'row'), (row(W['g_mem']), 'full'), (w_kv_f, 'full')],
        [(d2, BF16)], accs=[((1, d), F32)])
    p_kv = parts_of('ca_w_kv', _dwcall("dw_ca_kv", mem_n, dkv, BF16))

    dx1, G['g_ca_pre'], do1, G['g_mix_post'], _ = open_sublayer(
        "attn_in_bwd", [(dq, 0, d)], w_q_f, x1, W['g_ca_pre'], dx2, o1, W['g_mix_post'])
    p_mix = parts_of('w_mix_out', _dwcall("dw_mix_out", merged, do1, BF16))

    def merge_bwd_body(do_ref, wm_ref, zga_ref, zgb_ref, pa_ref, pb_ref, wb_ref, dpa_ref, dpb_ref, dza_ref, dzb_ref,
                       dys_ref):
        dm = _dot_nt(do_ref[...], wm_ref[...])
        sa, sb = _sigmoid(zga_ref[...].astype(F32)), _sigmoid(zgb_ref[...].astype(F32))
        dpb = (dm * sb).astype(BF16)
        dpa_ref[...] = (dm * sa).astype(BF16)
        dpb_ref[...] = dpb
        dza_ref[...] = (dm * pa_ref[...].astype(F32) * (sa * (1.0 - sa))).astype(BF16)
        dzb_ref[...] = (dm * pb_ref[...].astype(F32) * (sb * (1.0 - sb))).astype(BF16)
        dys_ref[...] = _dot_nt(dpb, wb_ref[...]).astype(BF16)

    NORM_ROWS = ['g_ffn_post', 'g_ffn_pre', 'g_ca_post', 'g_mem', 'g_ca_pre', 'g_mix_post']
    norm_rows = jnp.concatenate([G[k] for k in NORM_ROWS] + [jnp.tile(loss_part, (1, d // 128))], axis=0)
    dp_a, dp_b, dz_ga, dz_gb, dy_s5, all_norm_rows, RECV['w_mix_out'], RECV['ca_w_q'] = _rowcall(
        "merge_bwd", merge_bwd_body, n, tmw,
        [(do1, 'row'), (w_mix_f, 'full'), (z_ga, 'row'), (z_gb, 'row'), (p_a, 'row'), (p_b, 'row'), (w_brs5_f, 'full')],
        [(d, BF16), (d, BF16), (d, BF16), (d, BF16), (dh, BF16)], scatter=[p_mix, p_q], gather=[norm_rows])
    p_brgm = parts_of('w_br_gm', _dwcall("dw_br_gm", y_gm, dp_a, BF16))
    p_brs5 = parts_of('w_br_s5', _dwcall("dw_br_s5", y_s5, dp_b, BF16))

    def gm_bwd_body(u_ref, vh_ref, dgelu_ref, rstd_ref, dpa_ref, wa_ref, lng_ref, lnb_ref, ws_ref, wst_ref, bias_ref,
                    dz_ref, dws_ref, dbias_ref, dlng_ref, dlnb_ref):
        i = pl.program_id(0)

        @pl.when(i == 0)
        def _():
            dws_ref[...] = jnp.zeros_like(dws_ref)
            dbias_ref[...] = jnp.zeros_like(dbias_ref)
            dlng_ref[...] = jnp.zeros_like(dlng_ref)
            dlnb_ref[...] = jnp.zeros_like(dlnb_ref)

        u, vhat, rstd = u_ref[...].astype(F32), vh_ref[...].astype(F32), rstd_ref[:, :1]
        vnb = (vhat * lng_ref[...] + lnb_ref[...]).astype(BF16)
        dy = _dot_nt(dpa_ref[...], wa_ref[...])
        groups = [slice(g * gw, (g + 1) * gw) for g in range(GM_GROUPS)]
        dsv = dy * u
        dsvb = dsv.astype(BF16)
        dbias_ref[...] += dsv
        sv = jnp.concatenate([_dot(ws_ref[g], vnb[:, sl]) for g, sl in enumerate(groups)], axis=1) + bias_ref[...]
        dvn = jnp.concatenate([_dot(wst_ref[g], dsvb[:, sl]) for g, sl in enumerate(groups)], axis=1)
        for g, sl in enumerate(groups):
            dws_ref[g] += _dot_nt(dsvb[:, sl], vnb[:, sl])
        du = dy * sv
        dlng_ref[...] += jnp.sum(dvn * vhat, axis=0, keepdims=True)
        dlnb_ref[...] += jnp.sum(dvn, axis=0, keepdims=True)
        dvh = dvn * lng_ref[...]
        dv = rstd * (dvh - jnp.mean(dvh, axis=-1, keepdims=True) - vhat * jnp.mean(dvh * vhat, axis=-1, keepdims=True))
        dz_ref[:, :d] = (du * dgelu_ref[:, :d].astype(F32)).astype(BF16)
        dz_ref[:, d:] = (dv * dgelu_ref[:, d:].astype(F32)).astype(BF16)

    dz_gm, dws_full, dbias_full, G['gm_ln_g'], G['gm_ln_b'], RECV['w_br_gm'], RECV['w_br_s5'], RECV['ca_w_kv'] = _rowcall(
        "gmlp_bwd", gm_bwd_body, n, GM_CHUNK,
        [(gm_u, 'row'), (gm_vhat, 'row'), (gm_dgelu, 'row'), (gm_rstd, 'row'), (dp_a, 'row'), (w_brgm_f, 'full'),
         (row(W['gm_ln_g']), 'full'), (row(W['gm_ln_b']), 'full'), (w_s, 'full'), (w_s_t, 'full'), (gm_bias, 'full')],
        [(d2, BF16)], accs=[((GM_GROUPS, GM_CHUNK, GM_CHUNK), F32), ((GM_CHUNK, d), F32), ((1, d), F32), ((1, d), F32)],
        scatter=[p_brgm, p_brs5, p_kv])
    G['gm_w_s'] = jnp.where(tril[None], dws_full, 0.0)
    G['gm_b_s'] = dbias_full.reshape(GM_CHUNK, GM_GROUPS, gw).sum(-1).T

    dy_s5p = to_scan_order(dy_s5)

    def s5_out_bwd_body(dy_ref, ypre_ref, gate_ref, u_ref, wg_ref, dyp_ref, dgate_ref, yg_ref, dd_ref):
        i = pl.program_id(0)

        @pl.when(i == 0)
        def _():
            dd_ref[...] = jnp.zeros_like(dd_ref)

        dy = dy_ref[...].astype(F32)
        yg, dgelu = _gelu_and_grad(ypre_ref[...])
        sg = _sigmoid(gate_ref[...])
        dgb = (dy * yg * (sg * (1.0 - sg))).astype(BF16)
        dgate_ref[...] = dgb
        yg_ref[...] = yg.astype(BF16)
        dyp = (dy * sg + _dot_nt(dgb, wg_ref[...])) * dgelu
        dyp_ref[...] = dyp
        dd_ref[...] += jnp.sum(dyp * u_ref[...], axis=0, keepdims=True)

    dy_pre, dgate, yg_b, dd = _rowcall(
        "s5_out_bwd", s5_out_bwd_body, n, tm5,
        [(dy_s5p, 'row'), (y_pre, 'row'), (gate, 'row'), (u5p, 'row'), (w_glu_f, 'full')],
        [(dh, F32), (dh, BF16), (dh, BF16)], accs=[((1, dh), F32)])
    gm_ln_rows = jnp.concatenate([G['gm_ln_g'], G['gm_ln_b']], axis=0)
    du5p, d_bb_cat, d_cc_cat, da_cat, all_gm_w_s, all_gm_b_s, all_gm_ln, RECV['ffn_w_gu'] = _s5_scan_bwd(
        dy_pre, u5p, s_cat, a_cat_conj, bb_cat, cc_cat, s5_d_row, gather=[G['gm_w_s'], G['gm_b_s'], gm_ln_rows],
        scatter=[p_gu])
    du5 = from_scan_order(du5p.astype(BF16))
    d_bb_re, d_bb_im = _s5_uncat(jnp.sum(d_bb_cat, axis=0), 2)
    d_cc_re, d_cc_neg_im = _s5_uncat(jnp.sum(d_cc_cat, axis=0), 1)
    da_re, da_im = _s5_uncat(jnp.sum(da_cat, axis=0), 1)
    lane_shape = (S5_GROUPS, 8, 128)
    g_c_re, g_c_im = (t.reshape(lane_shape) for t in s5_out_vjp((d_cc_re, -d_cc_neg_im)))
    g_d = dd.reshape(S5_GROUPS, S5_CH)
    g_lam_re, g_lam_im, g_log_step, g_b_re, g_b_im = s5_in_vjp((da_re, da_im, d_bb_re, d_bb_im))
    g_log_step, g_b_re, g_b_im = g_log_step.reshape(1, -1), g_b_re.reshape(lane_shape), g_b_im.reshape(lane_shape)
    p_glu = parts_of('s5_w_glu', _dwcall("dw_s5_glu", yg_b, dgate, BF16))

    pieces = [(dz_gm, 0, d2), (du5, d2, d2 + dh), (dz_ga, d2 + dh, d2 + dh + d), (dz_gb, d2 + dh + d, d2 + dh + 2 * d)]
    dw_gm, all_lam_re, all_lam_im, all_log_step, all_b_re, all_b_im = _dwcall(
        "dw_in_0", dz_gm, h0, BF16, gather=[g_lam_re, g_lam_im, g_log_step, g_b_re, g_b_im])
    dw_s5 = _dwcall("dw_in_1", du5, h0, BF16)
    dw_ga = _dwcall("dw_in_2", dz_ga, h0, BF16)
    dw_gb, all_c_re, all_c_im, all_d, RECV['s5_w_glu'] = _dwcall(
        "dw_in_3", dz_gb, h0, BF16, gather=[g_c_re, g_c_im, g_d], scatter=[p_glu])
    p_in = parts_of('w_in', jnp.concatenate([dw_gm, dw_s5, dw_ga, dw_gb], axis=0))
    grad_x2d, g_mix_pre_part, (RECV['w_in'],) = open_sublayer(
        "in_proj_bwd", pieces, w_in_t, x2d, W['g_mix_pre'], dx1, scatter=[p_in], w_t=True)

    out = {}
    for k in names:
        quad = _sum_adamw("sum_adamw_" + k, RECV[k], W[k], M[k], V[k])
        out[k] = [(t.T if SHARDED[k] == 'colT' else t)[None] for t in quad]
    gathered = [all_norm_rows, all_gm_w_s, all_gm_b_s, all_gm_ln, all_c_re, all_c_im, all_d, all_lam_re, all_lam_im,
                all_log_step, all_b_re, all_b_im, _gather_last(g_mix_pre_part)]
    where = {'g_ffn_post': (0, 0), 'g_ffn_pre': (0, 1), 'g_ca_post': (0, 2), 'g_mem': (0, 3), 'g_ca_pre': (0, 4),
             'g_mix_post': (0, 5), 'gm_w_s': (1, None), 'gm_b_s': (2, None), 'gm_ln_g': (3, 0), 'gm_ln_b': (3, 1),
             's5_c_re': (4, None), 's5_c_im': (5, None), 's5_d': (6, None), 's5_lam_re': (7, None),
             's5_lam_im': (8, None), 's5_log_step': (9, 0), 's5_b_re': (10, None), 's5_b_im': (11, None),
             'g_mix_pre': (12, 0)}
    folded = ('s5_b_re', 's5_b_im', 's5_c_re', 's5_c_im')

    def as_updated(k, t):
        return t.reshape((1,) + lane_shape) if k in folded else t

    small, loss_row = _small_adamw(
        [(as_updated(k, args[k]), as_updated(k, args['m_' + k]), as_updated(k, args['v_' + k])) + where[k]
         for k in SMALL], gathered, (0, len(NORM_ROWS)))
    out.update({k: [t.reshape(args[k].shape) for t in quad] for k, quad in zip(SMALL, small)})

    res = [loss_row[0, 0], grad_x2d.reshape(x.shape)]
    for j in range(4):
        res += [out[k][j] for k in WEIGHTS]
    return tuple(res)
```

```python
import functools
import math

import jax
import jax.numpy as jnp
from jax import lax
from jax.experimental import pallas as pl
from jax.experimental.pallas import tpu as pltpu

F32 = jnp.float32
BF16 = jnp.bfloat16
EPS = 1e-6
N_DEV = 8
V7X_VMEM_LIMIT = 56 * 1024 * 1024
ROW_TILE = 256
WIDE_ROW_TILE = 1024
FFN_ROW_TILE = 512
GM_CHUNK = 128
GM_GROUPS = 8
S5_GROUPS = 32
S5_STATE = 64
S5_CH = 16
SCAN_CHUNKS = 32
SCAN_LANES = 128
S5_BLOCK_LANES = 256
CA_HEADS = 4
ADAM_LR, ADAM_B1, ADAM_B2, ADAM_EPS, ADAM_WD, ADAM_STEP = 0.001, 0.9, 0.999, 1e-08, 0.01, 10

WEIGHTS = ['g_mix_pre', 'w_in', 'gm_ln_g', 'gm_ln_b', 'gm_w_s', 'gm_b_s', 's5_lam_re', 's5_lam_im', 's5_log_step',
           's5_b_re', 's5_b_im', 's5_c_re', 's5_c_im', 's5_d', 's5_w_glu', 'w_br_gm', 'w_br_s5', 'w_mix_out',
           'g_mix_post', 'g_ca_pre', 'g_mem', 'ca_w_q', 'ca_w_kv', 'ca_w_o', 'g_ca_post', 'g_ffn_pre', 'ffn_w_gu',
           'ffn_w_down', 'g_ffn_post']
SHARDED = {'w_in': 'colT', 's5_w_glu': 'row', 'w_br_gm': 'row', 'w_br_s5': 'col', 'w_mix_out': 'row',
           'ca_w_q': 'row', 'ca_w_kv': 'col', 'ca_w_o': 'row', 'ffn_w_gu': 'colT', 'ffn_w_down': 'row'}
SMALL = [n for n in WEIGHTS if n not in SHARDED]


def _rms(x, g):
    r = lax.rsqrt(jnp.mean(x * x, axis=-1, keepdims=True) + EPS)
    return (x * r) * g


def _rms_bwd(x, g, dy):
    r = lax.rsqrt(jnp.mean(x * x, axis=-1, keepdims=True) + EPS)
    n = x * r
    dn = dy * g
    dx = r * (dn - n * jnp.mean(dn * n, axis=-1, keepdims=True))
    return dx, jnp.sum(dy * n, axis=0, keepdims=True)


_GELU_C = math.sqrt(2.0 / math.pi)


def _gelu(x):
    return 0.5 * x * (1.0 + jnp.tanh(_GELU_C * (x + 0.044715 * (x * x * x))))


def _gelu_and_grad(x):
    x2 = x * x
    t = jnp.tanh(_GELU_C * (x + 0.044715 * (x2 * x)))
    h = 0.5 * (1.0 + t)
    return x * h, h + 0.5 * x * (1.0 - t * t) * (_GELU_C * (1.0 + 3.0 * 0.044715 * x2))


def _sigmoid(x):
    return 0.5 * (1.0 + jnp.tanh(0.5 * x))


def _dot(a, b):
    return jnp.dot(a, b, preferred_element_type=F32)


def _dot_nt(a, b):
    return lax.dot_general(a, b, (((1,), (1,)), ((), ())), preferred_element_type=F32)


def _dot_tn(a, b):
    return lax.dot_general(a, b, (((0,), (0,)), ((), ())), preferred_element_type=F32)


def _adamw(w, g, m, v):
    m = ADAM_B1 * m + (1.0 - ADAM_B1) * g
    v = ADAM_B2 * v + (1.0 - ADAM_B2) * (g * g)
    m_hat = m / (1.0 - ADAM_B1 ** ADAM_STEP)
    v_hat = v / (1.0 - ADAM_B2 ** ADAM_STEP)
    delta = -ADAM_LR * (m_hat / (jnp.sqrt(v_hat) + ADAM_EPS) + ADAM_WD * w)
    return delta, m, v


def _params(n_grid):
    return pltpu.CompilerParams(dimension_semantics=("arbitrary",) * n_grid, vmem_limit_bytes=V7X_VMEM_LIMIT)


def _my_place():
    x, y, c = lax.axis_index("x"), lax.axis_index("y"), lax.axis_index("c")
    return x, y, c


def _peer(x, y, c, k):
    px = 1 - x if k & 4 else x
    py = 1 - y if k & 2 else y
    pc = 1 - c if k & 1 else c
    return (px, py, pc), 4 * px + 2 * py + pc


def _exchange(kind, src_refs, dst_refs, sems, first, phase):
    x, y, c = _my_place()
    me = 4 * x + 2 * y + c
    send_sems, recv_sems, own_sems = sems
    for j, (src, dst) in enumerate(zip(src_refs, dst_refs), start=first):
        own = pltpu.make_async_copy(src if kind == 'gather' else src.at[me], dst.at[me], own_sems.at[j])
        if phase == 'start':
            own.start()
        for k in range(1, N_DEV):
            peer, peer_block = _peer(x, y, c, k)
            out = pltpu.make_async_remote_copy(
                src_ref=src if kind == 'gather' else src.at[peer_block], dst_ref=dst.at[me],
                send_sem=send_sems.at[7 * j + k - 1], recv_sem=recv_sems.at[7 * j + k - 1], device_id=peer,
                device_id_type=pl.DeviceIdType.MESH)
            if phase == 'start':
                out.start()
            else:
                pltpu.make_async_remote_copy(
                    src_ref=src if kind == 'gather' else src.at[peer_block], dst_ref=dst.at[peer_block],
                    send_sem=send_sems.at[7 * j + k - 1], recv_sem=recv_sems.at[7 * j + k - 1], device_id=peer,
                    device_id_type=pl.DeviceIdType.MESH).wait_recv()
                out.wait_send()
        if phase == 'wait':
            own.wait()


def _gather_two_level(src_refs, dst_refs, sems, first, phase):
    x, y, c = _my_place()
    me, sibling = (x, y, c), (x, y, 1 - c)
    chips = [(1 - x, y), (x, 1 - y), (1 - x, 1 - y)]
    send_sems, recv_sems, own_sems = sems
    for j, (src, dst) in enumerate(zip(src_refs, dst_refs), start=first):
        def rows(px, py, pc, dst=dst):
            return dst.at[4 * px + 2 * py + pc]

        def copy(k, block, to, from_src=False, j=j, src=src, rows=rows):
            return pltpu.make_async_remote_copy(
                src_ref=src if from_src else rows(*block), dst_ref=rows(*block), send_sem=send_sems.at[7 * j + k],
                recv_sem=recv_sems.at[7 * j + k], device_id=to, device_id_type=pl.DeviceIdType.MESH)

        mine = pltpu.make_async_copy(src, rows(*me), own_sems.at[j])
        first_out = [copy(0, me, sibling, True)] + [copy(1 + i, me, (*chip, c), True) for i, chip in enumerate(chips)]
        if phase == 'start':
            mine.start()
            for cp in first_out:
                cp.start()
        else:
            passed = [copy(4 + i, (*chip, c), sibling) for i, chip in enumerate(chips)]
            for i, chip in enumerate(chips):
                copy(1 + i, (*chip, c), me).wait_recv()
                passed[i].start()
            copy(0, sibling, me).wait_recv()
            for i, chip in enumerate(chips):
                copy(4 + i, (*chip, 1 - c), me).wait_recv()
            for cp in first_out + passed:
                cp.wait_send()
            mine.wait()


class _Ride:
    def __init__(self, gather=(), scatter=(), two_level=False):
        self.two_level = two_level
        self.n_gather = len(gather)
        self.arrays = list(gather) + list(scatter)
        n = len(self.arrays)
        hbm = pl.BlockSpec(memory_space=pl.ANY)
        self.in_specs = [hbm] * n
        self.out_specs = [hbm] * n
        self.out_shape = [jax.ShapeDtypeStruct((N_DEV,) + a.shape, a.dtype) for a in gather]
        self.out_shape += [jax.ShapeDtypeStruct(a.shape, a.dtype) for a in scatter]
        self.scratch = [pltpu.SemaphoreType.DMA((7 * n,)), pltpu.SemaphoreType.DMA((7 * n,)),
                        pltpu.SemaphoreType.DMA((n,))] if n else []

    def wrap(self, inner, n_in, n_out, n_scr, is_first, is_last):
        n_mv = len(self.arrays)
        if not n_mv:
            return inner

        def body(*refs):
            mv_src = refs[n_in:n_in + n_mv]
            mv_dst = refs[n_in + n_mv + n_out:n_in + 2 * n_mv + n_out]
            sems = refs[n_in + 2 * n_mv + n_out + n_scr:]

            def exchange(phase):
                if self.n_gather and self.two_level:
                    _gather_two_level(mv_src[:self.n_gather], mv_dst[:self.n_gather], sems, 0, phase)
                elif self.n_gather:
                    _exchange('gather', mv_src[:self.n_gather], mv_dst[:self.n_gather], sems, 0, phase)
                if n_mv > self.n_gather:
                    _exchange('scatter', mv_src[self.n_gather:], mv_dst[self.n_gather:], sems, self.n_gather, phase)

            pl.when(is_first())(functools.partial(exchange, 'start'))
            inner(*refs[:n_in], *refs[n_in + n_mv:n_in + n_mv + n_out],
                  *refs[n_in + 2 * n_mv + n_out:n_in + 2 * n_mv + n_out + n_scr])
            pl.when(is_last())(functools.partial(exchange, 'wait'))

        return body


def _rowcall(name, body, n_rows, tm, ins, outs, accs=(), scratch=(), gather=(), scatter=(), two_level=False):
    n_steps = n_rows // tm
    ride = _Ride(gather, scatter, two_level)
    body = ride.wrap(body, len(ins), len(outs) + len(accs), len(scratch), lambda: pl.program_id(0) == 0,
                     lambda: pl.program_id(0) == n_steps - 1)
    arrays, in_specs = [], []
    for a, kind in ins:
        arrays.append(a)
        if kind == 'row':
            in_specs.append(pl.BlockSpec((tm,) + a.shape[1:], lambda i, nd=a.ndim: (i,) + (0,) * (nd - 1)))
        elif kind == 'full':
            in_specs.append(pl.BlockSpec(a.shape, lambda i, nd=a.ndim: (0,) * nd))
        else:
            in_specs.append(kind)
    out_shape, out_specs = [], []
    for cols, dt in outs:
        out_shape.append(jax.ShapeDtypeStruct((n_rows, cols), dt))
        out_specs.append(pl.BlockSpec((tm, cols), lambda i: (i, 0)))
    for shp, dt in accs:
        if isinstance(dt, tuple):
            dt, spec = dt
        else:
            spec = pl.BlockSpec(shp, lambda i, nd=len(shp): (0,) * nd)
        out_shape.append(jax.ShapeDtypeStruct(shp, dt))
        out_specs.append(spec)
    return pl.pallas_call(functools.partial(body), grid=(n_steps,), in_specs=in_specs + ride.in_specs,
                          out_specs=out_specs + ride.out_specs, out_shape=out_shape + ride.out_shape,
                          scratch_shapes=list(scratch) + ride.scratch, name=name,
                          compiler_params=_params(1))(*arrays, *ride.arrays)


DW_ACC_BYTES = 12 * 1024 * 1024
DW_LHS_BYTES = 6 * 1024 * 1024


def _dw_tiles(n, ka, nn, a_itemsize):
    tn = max(t for t in range(128, min(nn, 1536) + 1, 128) if nn % t == 0)
    tka = max(t for t in range(128, ka + 1, 128) if ka % t == 0 and t * tn * 4 <= DW_ACC_BYTES)
    tm = min(n, 2048)
    while tm > 256 and tm * tka * a_itemsize > DW_LHS_BYTES:
        tm //= 2
    return tm, tka, tn


def _dwcall(name, a, dc, out_dtype, gather=(), scatter=()):
    n, ka = a.shape
    nn = dc.shape[1]
    tm, tka, tn = _dw_tiles(n, ka, nn, a.dtype.itemsize)
    n_i, n_j, n_k = n // tm, nn // tn, ka // tka
    ride = _Ride(gather, scatter)

    def body(a_ref, dc_ref, o_ref, acc_ref):
        i = pl.program_id(2)

        @pl.when(i == 0)
        def _():
            acc_ref[...] = jnp.zeros_like(acc_ref)

        acc_ref[...] += _dot_tn(a_ref[...].astype(BF16), dc_ref[...].astype(BF16))

        @pl.when(i == n_i - 1)
        def _():
            o_ref[...] = acc_ref[...].astype(o_ref.dtype)

    def at_step(k, j, i):
        return lambda: (pl.program_id(0) == k) & (pl.program_id(1) == j) & (pl.program_id(2) == i)

    body = ride.wrap(body, 2, 1, 1, at_step(0, 0, 0), at_step(n_k - 1, n_j - 1, n_i - 1))
    res = pl.pallas_call(
        body, grid=(n_k, n_j, n_i),
        in_specs=[pl.BlockSpec((tm, tka), lambda k, j, i: (i, k)), pl.BlockSpec((tm, tn), lambda k, j, i: (i, j))]
        + ride.in_specs,
        out_specs=[pl.BlockSpec((tka, tn), lambda k, j, i: (k, j))] + ride.out_specs,
        out_shape=[jax.ShapeDtypeStruct((ka, nn), out_dtype)] + ride.out_shape,
        scratch_shapes=[pltpu.VMEM((tka, tn), F32)] + ride.scratch, name=name, compiler_params=_params(3))(
            a, dc, *ride.arrays)
    return res if ride.arrays else res[0]


def _first_gather(first, later):
    n = len(later)
    vm = pl.BlockSpec(memory_space=pltpu.VMEM)

    def body(*refs):
        first_ref, later_refs = refs[0], refs[1:1 + n]
        out_ref, cast_refs = refs[1 + n], refs[2 + n:2 + 2 * n]
        stage = refs[2 + 2 * n]
        sems = refs[3 + 2 * n:]
        stage[...] = first_ref[...].astype(BF16)
        _gather_two_level([stage], [out_ref], sems, 0, 'start')
        for src, dst in zip(later_refs, cast_refs):
            dst[...] = src[...].astype(BF16)
        _gather_two_level([stage], [out_ref], sems, 0, 'wait')

    return pl.pallas_call(
        body, out_shape=[jax.ShapeDtypeStruct((N_DEV,) + first.shape, BF16)]
        + [jax.ShapeDtypeStruct(s.shape, BF16) for s in later],
        in_specs=[vm] * (1 + n), out_specs=[pl.BlockSpec(memory_space=pl.ANY)] + [vm] * n,
        scratch_shapes=[pltpu.VMEM(first.shape, BF16), pltpu.SemaphoreType.DMA((7,)), pltpu.SemaphoreType.DMA((7,)),
                        pltpu.SemaphoreType.DMA((1,))],
        name="gather_first", compiler_params=pltpu.CompilerParams(vmem_limit_bytes=V7X_VMEM_LIMIT))(first, *later)


def _sum_adamw(name, recv, w, m, v):
    r, c = w.shape
    steps = max(s for s in (4, 2, 1) if r % s == 0 and (r // s) % 16 == 0 or s == 1)

    def body(recv_ref, w_ref, m_ref, v_ref, g_ref, d_ref, nm_ref, nv_ref):
        g = recv_ref[0].astype(F32)
        for k in range(1, N_DEV):
            g = g + recv_ref[k].astype(F32)
        d, nm, nv = _adamw(w_ref[...], g, m_ref[...], v_ref[...])
        g_ref[...] = g
        d_ref[...] = d
        nm_ref[...] = nm
        nv_ref[...] = nv

    blk = pl.BlockSpec((r // steps, c), lambda i: (i, 0))
    return pl.pallas_call(
        body, grid=(steps,), out_shape=[jax.ShapeDtypeStruct((r, c), F32)] * 4,
        in_specs=[pl.BlockSpec((N_DEV, r // steps, c), lambda i: (0, i, 0)), blk, blk, blk], out_specs=[blk] * 4,
        name=name, compiler_params=_params(1))(recv, w, m, v)


def _gather_last(a):
    def body(a_ref, out_ref, send_sems, recv_sems, own_sems):
        sems = (send_sems, recv_sems, own_sems)
        _exchange('gather', [a_ref], [out_ref], sems, 0, 'start')
        _exchange('gather', [a_ref], [out_ref], sems, 0, 'wait')

    vm = pl.BlockSpec(memory_space=pltpu.VMEM)
    return pl.pallas_call(
        body, out_shape=jax.ShapeDtypeStruct((N_DEV,) + a.shape, a.dtype), in_specs=[vm], out_specs=vm,
        scratch_shapes=[pltpu.SemaphoreType.DMA((7,)), pltpu.SemaphoreType.DMA((7,)), pltpu.SemaphoreType.DMA((1,))],
        name="gather_last", compiler_params=pltpu.CompilerParams(vmem_limit_bytes=V7X_VMEM_LIMIT))(a)


def _small_adamw(entries, gathered, loss_at):
    n, ng = len(entries), len(gathered)

    def body(*refs):
        g_refs, wmv = refs[:ng], refs[ng:ng + 3 * n]
        outs, loss_out = refs[ng + 3 * n:ng + 7 * n], refs[ng + 7 * n]

        def total(ref, r):
            if r is None:
                t = ref[0]
                for k in range(1, N_DEV):
                    t = t + ref[k]
                return t[None]
            t = ref[0, r:r + 1, :]
            for k in range(1, N_DEV):
                t = t + ref[k, r:r + 1, :]
            return t

        for j, (_, _, _, gi, r) in enumerate(entries):
            g = total(g_refs[gi], r)
            d, nm, nv = _adamw(wmv[3 * j][...], g, wmv[3 * j + 1][...], wmv[3 * j + 2][...])
            for ref, val in zip(outs[4 * j:4 * j + 4], (g, d, nm, nv)):
                ref[...] = val
        loss_out[...] = total(g_refs[loss_at[0]], loss_at[1])[:, :128]

    vm = pl.BlockSpec(memory_space=pltpu.VMEM)
    out_shape, arrays = [], list(gathered)
    for w, m, v, _, _ in entries:
        out_shape += [jax.ShapeDtypeStruct(w.shape, F32)] * 4
        arrays += [w, m, v]
    out_shape.append(jax.ShapeDtypeStruct((1, 128), F32))
    res = pl.pallas_call(
        body, out_shape=out_shape, in_specs=[vm] * len(arrays), out_specs=[vm] * len(out_shape),
        name="small_adamw", compiler_params=pltpu.CompilerParams(vmem_limit_bytes=V7X_VMEM_LIMIT))(*arrays)
    return [res[4 * j:4 * j + 4] for j in range(n)], res[4 * n]


def _blockdiag8(t):
    g, per = t.shape[0], 8
    eye = jnp.eye(per, dtype=F32)
    t = t.reshape(g // per, per, t.shape[1], t.shape[2])
    return (t[:, :, :, None, :] * eye[None, :, None, :, None]).reshape(g // per, per * t.shape[2], per * t.shape[3])


def _s5_out_tables(c_re, c_im):
    return _blockdiag8(c_re.transpose(0, 2, 1)), _blockdiag8(c_im.transpose(0, 2, 1))


def _s5_in_tables(lam_re, lam_im, log_step, b_re, b_im):
    step = jnp.exp(log_step)[:, None]
    mag = jnp.exp(lam_re * step)
    ab_re = mag * jnp.cos(lam_im * step)
    ab_im = mag * jnp.sin(lam_im * step)
    den = lam_re * lam_re + lam_im * lam_im
    nr = ab_re - 1.0
    co_re = (nr * lam_re + ab_im * lam_im) / den
    co_im = (ab_im * lam_re - nr * lam_im) / den
    bb_re = co_re[..., None] * b_re - co_im[..., None] * b_im
    bb_im = co_re[..., None] * b_im + co_im[..., None] * b_re
    return (ab_re.reshape(1, -1), ab_im.reshape(1, -1), _blockdiag8(bb_re.transpose(0, 2, 1)),
            _blockdiag8(bb_im.transpose(0, 2, 1)))


def _scan_passes(x_s, a_ref, cr_ref, ci_ref, reverse, re, im):
    tc, nc = x_s.shape[:2]
    ln = SCAN_LANES
    n_sq = int(math.log2(tc))
    assert 2 ** n_sq == tc
    ar = jnp.broadcast_to(a_ref[:, re], (nc, ln))
    ai = jnp.broadcast_to(a_ref[:, im], (nc, ln))
    zero = jnp.zeros((nc, ln), F32)

    def at(t):
        return tc - 1 - t if reverse else t

    def local(t, carry):
        sr, si = carry
        j = at(t)
        nr = ar * sr - ai * si + x_s[j, :, re]
        ni = ar * si + ai * sr + x_s[j, :, im]
        x_s[j, :, re] = nr
        x_s[j, :, im] = ni
        return nr, ni

    lr, li = lax.fori_loop(0, tc, local, (zero, zero))
    pr, pi = a_ref[:, re], a_ref[:, im]
    for _ in range(n_sq):
        pr, pi = pr * pr - pi * pi, 2.0 * (pr * pi)
    cr_ref[...] = lr
    ci_ref[...] = li
    tr = jnp.zeros((1, ln), F32)
    ti = jnp.zeros((1, ln), F32)
    for c in (range(nc - 1, -1, -1) if reverse else range(nc)):
        l_r = cr_ref[c:c + 1, :]
        l_i = ci_ref[c:c + 1, :]
        cr_ref[c:c + 1, :] = tr
        ci_ref[c:c + 1, :] = ti
        tr, ti = pr * tr - pi * ti + l_r, pr * ti + pi * tr + l_i

    def second_pass(on_fixed):
        def fixup(t, carry):
            qr, qi, acc = carry
            j = at(t)
            qr, qi = ar * qr - ai * qi, ar * qi + ai * qr
            gr = x_s[j, :, re] + qr
            gi = x_s[j, :, im] + qi
            x_s[j, :, re] = gr
            x_s[j, :, im] = gi
            return qr, qi, on_fixed(j, gr, gi, acc)

        return fixup

    return second_pass, (cr_ref[...], ci_ref[...]), zero


def _s5_cat_tables(a_re, a_im, bb_re, bb_im, cc_re, cc_im):
    lb = S5_BLOCK_LANES
    n_blk, kq, nq = bb_re.shape
    sub = nq // lb

    def lanes(re, im):
        lead = re.shape[:-1]
        both = jnp.stack([re.reshape(lead + (-1, lb)), im.reshape(lead + (-1, lb))], axis=-2)
        return both.reshape(lead + (-1,))

    cc = jnp.stack([cc_re.reshape(n_blk, sub, lb, kq), -cc_im.reshape(n_blk, sub, lb, kq)], axis=2)
    return lanes(a_re, a_im), lanes(bb_re, bb_im), cc.reshape(n_blk, sub * 2 * lb, kq)


def _s5_uncat(t, axis):
    lb = S5_BLOCK_LANES
    shp = t.shape
    t = t.reshape(shp[:axis] + (-1, 2, lb) + shp[axis + 1:])
    re, im = jnp.take(t, 0, axis=axis + 1), jnp.take(t, 1, axis=axis + 1)
    return re.reshape(shp[:axis] + (-1,) + shp[axis + 1:]), im.reshape(shp[:axis] + (-1,) + shp[axis + 1:])


def _grid2_edges(n0, n1):
    return (lambda: (pl.program_id(0) == 0) & (pl.program_id(1) == 0),
            lambda: (pl.program_id(0) == n0 - 1) & (pl.program_id(1) == n1 - 1))


def _s5_scan_fwd(u5p, a_cat, bb_cat, cc_cat, nb, gather=()):
    n, dh = u5p.shape
    n_blk, kq, cols = bb_cat.shape
    lb, ln, nc = S5_BLOCK_LANES, SCAN_LANES, SCAN_CHUNKS
    sub = cols // (2 * lb)
    rows = n // nb
    tc = rows // nc
    n_l = n_blk * sub

    def body(u_ref, a_ref, b_ref, c_ref, s_ref, y_ref, x_s, car_r, car_i):
        x_s[...] = _dot(u_ref[...].astype(BF16), b_ref[0]).reshape(tc, nc, 2 * lb)
        for h in range(lb // ln):
            re, im = slice(h * ln, (h + 1) * ln), slice(lb + h * ln, lb + (h + 1) * ln)

            def keep(j, gr, gi, acc, re=re, im=im):
                s_ref[0, j, :, re] = gr.astype(BF16)
                s_ref[0, j, :, im] = gi.astype(BF16)
                return acc

            second_pass, start, zero = _scan_passes(x_s, a_ref, car_r, car_i, False, re, im)
            lax.fori_loop(0, tc, second_pass(keep), (*start, zero))
        y = _dot(x_s[...].reshape(rows, 2 * lb).astype(BF16), c_ref[0])

        @pl.when(pl.program_id(1) % sub == 0)
        def _():
            y_ref[...] = y

        @pl.when(pl.program_id(1) % sub != 0)
        def _():
            y_ref[...] += y

    row_blk = pl.BlockSpec((rows, kq), lambda b, l: (b, l // sub))
    ride = _Ride(gather, two_level=True)
    return pl.pallas_call(
        ride.wrap(body, 4, 2, 3, *_grid2_edges(nb, n_l)), grid=(nb, n_l),
        in_specs=[row_blk, pl.BlockSpec((1, 2 * lb), lambda b, l: (0, l)),
                  pl.BlockSpec((1, kq, 2 * lb), lambda b, l: (l // sub, 0, l % sub)),
                  pl.BlockSpec((1, 2 * lb, kq), lambda b, l: (l // sub, l % sub, 0))] + ride.in_specs,
        out_specs=[pl.BlockSpec((1, tc, nc, 2 * lb), lambda b, l: (b, 0, 0, l)), row_blk] + ride.out_specs,
        out_shape=[jax.ShapeDtypeStruct((nb, tc, nc, n_l * 2 * lb), BF16), jax.ShapeDtypeStruct((n, dh), F32)]
        + ride.out_shape,
        scratch_shapes=[pltpu.VMEM((tc, nc, 2 * lb), F32)] + [pltpu.VMEM((nc, ln), F32)] * 2 + ride.scratch,
        name="s5_scan_fwd", compiler_params=_params(2))(u5p, a_cat, bb_cat, cc_cat, *ride.arrays)


def _s5_scan_bwd(dy_pre, u5p, s_cat, a_cat_conj, bb_cat, cc_cat, d_row, gather=(), scatter=()):
    n, dh = u5p.shape
    n_blk, kq, cols = bb_cat.shape
    nb, tc, nc, _ = s_cat.shape
    lb, ln = S5_BLOCK_LANES, SCAN_LANES
    sub = cols // (2 * lb)
    rows = n // nb
    n_l = n_blk * sub

    def body(dy_ref, u_ref, s_ref, a_ref, b_ref, c_ref, d_ref, du_ref, db_ref, dc_ref, da_ref, x_s, car_r, car_i):
        dyb = dy_ref[...].astype(BF16)
        x_s[...] = _dot_nt(dyb, c_ref[0]).reshape(tc, nc, 2 * lb)
        for h in range(lb // ln):
            re, im = slice(h * ln, (h + 1) * ln), slice(lb + h * ln, lb + (h + 1) * ln)

            def with_state_before(j, gr, gi, acc, re=re, im=im):
                dr, di = acc
                pr_, pi_ = s_ref[0, j - 1, :, re].astype(F32), s_ref[0, j - 1, :, im].astype(F32)
                return dr + (pr_ * gr + pi_ * gi), di + (pr_ * gi - pi_ * gr)

            second_pass, start, zero = _scan_passes(x_s, a_ref, car_r, car_i, True, re, im)
            carry = lax.fori_loop(0, tc - 1, second_pass(with_state_before), (*start, (zero, zero)))
            row = lax.broadcasted_iota(jnp.int32, (nc, ln), 0)
            pr_ = jnp.where(row == 0, 0.0, pltpu.roll(s_ref[0, tc - 1, :, re].astype(F32), 1, 0))
            pi_ = jnp.where(row == 0, 0.0, pltpu.roll(s_ref[0, tc - 1, :, im].astype(F32), 1, 0))

            def at_first_time(j, gr, gi, acc, pr_=pr_, pi_=pi_):
                return acc[0] + (pr_ * gr + pi_ * gi), acc[1] + (pr_ * gi - pi_ * gr)

            _, _, (dr, di) = second_pass(at_first_time)(tc - 1, carry)
            da_ref[0, :, re] = jnp.sum(dr, axis=0, keepdims=True)
            da_ref[0, :, im] = jnp.sum(di, axis=0, keepdims=True)

        gb = x_s[...].reshape(rows, 2 * lb).astype(BF16)
        du = _dot_nt(gb, b_ref[0])

        @pl.when(pl.program_id(1) % sub == 0)
        def _():
            du_ref[...] = du + d_ref[...] * dy_ref[...]

        @pl.when(pl.program_id(1) % sub != 0)
        def _():
            du_ref[...] += du

        db_ref[0, 0] = _dot_tn(u_ref[...].astype(BF16), gb)
        dc_ref[0, 0] = _dot_tn(s_ref[0].reshape(rows, 2 * lb), dyb)

    row_blk = pl.BlockSpec((rows, kq), lambda b, l: (b, l // sub))
    in_map = pl.BlockSpec((1, kq, 2 * lb), lambda b, l: (l // sub, 0, l % sub))
    out_map = pl.BlockSpec((1, 2 * lb, kq), lambda b, l: (l // sub, l % sub, 0))
    ride = _Ride(gather, scatter)
    return pl.pallas_call(
        ride.wrap(body, 7, 4, 3, *_grid2_edges(nb, n_l)), grid=(nb, n_l),
        in_specs=[row_blk, row_blk, pl.BlockSpec((1, tc, nc, 2 * lb), lambda b, l: (b, 0, 0, l)),
                  pl.BlockSpec((1, 2 * lb), lambda b, l: (0, l)), in_map, out_map,
                  pl.BlockSpec((1, kq), lambda b, l: (0, l // sub))] + ride.in_specs,
        out_specs=[row_blk, pl.BlockSpec((1, 1, kq, 2 * lb), lambda b, l: (b, l // sub, 0, l % sub)),
                   pl.BlockSpec((1, 1, 2 * lb, kq), lambda b, l: (b, l // sub, l % sub, 0)),
                   pl.BlockSpec((1, 1, 2 * lb), lambda b, l: (b, 0, l))] + ride.out_specs,
        out_shape=[jax.ShapeDtypeStruct((n, dh), F32), jax.ShapeDtypeStruct((nb, n_blk, kq, cols), F32),
                   jax.ShapeDtypeStruct((nb, n_blk, cols, kq), F32), jax.ShapeDtypeStruct((nb, 1, n_l * 2 * lb), F32)]
        + ride.out_shape,
        scratch_shapes=[pltpu.VMEM((tc, nc, 2 * lb), F32)] + [pltpu.VMEM((nc, ln), F32)] * 2 + ride.scratch,
        name="s5_scan_bwd", compiler_params=_params(2))(dy_pre, u5p, s_cat, a_cat_conj, bb_cat, cc_cat, d_row,
                                                        *ride.arrays)


def kernel(x, mem, g_mix_pre, w_in, gm_ln_g, gm_ln_b, gm_w_s, gm_b_s, s5_lam_re, s5_lam_im, s5_log_step, s5_b_re, s5_b_im, s5_c_re, s5_c_im, s5_d, s5_w_glu, w_br_gm, w_br_s5, w_mix_out, g_mix_post, g_ca_pre, g_mem, ca_w_q, ca_w_kv, ca_w_o, g_ca_post, g_ffn_pre, ffn_w_gu, ffn_w_down, g_ffn_post, loss_target, m_g_mix_pre, m_w_in, m_gm_ln_g, m_gm_ln_b, m_gm_w_s, m_gm_b_s, m_s5_lam_re, m_s5_lam_im, m_s5_log_step, m_s5_b_re, m_s5_b_im, m_s5_c_re, m_s5_c_im, m_s5_d, m_s5_w_glu, m_w_br_gm, m_w_br_s5, m_w_mix_out, m_g_mix_post, m_g_ca_pre, m_g_mem, m_ca_w_q, m_ca_w_kv, m_ca_w_o, m_g_ca_post, m_g_ffn_pre, m_ffn_w_gu, m_ffn_w_down, m_g_ffn_post, v_g_mix_pre, v_w_in, v_gm_ln_g, v_gm_ln_b, v_gm_w_s, v_gm_b_s, v_s5_lam_re, v_s5_lam_im, v_s5_log_step, v_s5_b_re, v_s5_b_im, v_s5_c_re, v_s5_c_im, v_s5_d, v_s5_w_glu, v_w_br_gm, v_w_br_s5, v_w_mix_out, v_g_mix_post, v_g_ca_pre, v_g_mem, v_ca_w_q, v_ca_w_kv, v_ca_w_o, v_g_ca_post, v_g_ffn_pre, v_ffn_w_gu, v_ffn_w_down, v_g_ffn_post):
    args = locals()
    W = {n: args[n][0] for n in WEIGHTS}
    M = {n: args['m_' + n][0] for n in WEIGHTS}
    V = {n: args['v_' + n][0] for n in WEIGHTS}
    for k, kind in SHARDED.items():
        if kind == 'colT':
            W[k], M[k], V[k] = W[k].T, M[k].T, V[k].T

    nb, seq, d = x.shape
    n = nb * seq
    tm = min(ROW_TILE, n)
    tmw = min(WIDE_ROW_TILE, n)
    tmf = min(FFN_ROW_TILE, n)
    nmem = mem.shape[1]
    d2, dh = 2 * d, d // 2
    hd = d // CA_HEADS
    x2d = x.reshape(n, d)
    tgt = loss_target.reshape(n, d)
    mem2d = mem.reshape(nb * nmem, d)

    def row(v):
        return v.reshape(1, -1)

    names = list(SHARDED)
    later = [k for k in names if k != 'w_in']
    first_g, *casts = _first_gather(W['w_in'], [W[k] for k in later])
    shard_b = dict(zip(later, casts))
    half = shard_b['ffn_w_gu'].shape[0] // 2
    gu_halves = [shard_b['ffn_w_gu'][:half], shard_b['ffn_w_gu'][half:]]

    def assemble(k, gth):
        r, c = gth.shape[1:]
        return gth.transpose(1, 0, 2).reshape(r, N_DEV * c) if SHARDED[k] == 'col' else gth.reshape(N_DEV * r, c)

    w_in_t = assemble('w_in', first_g)
    ffh = N_DEV * W['ffn_w_down'].shape[0]

    (a_re, a_im, bb_re, bb_im), s5_in_vjp = jax.vjp(
        _s5_in_tables, W['s5_lam_re'], W['s5_lam_im'], W['s5_log_step'], W['s5_b_re'], W['s5_b_im'])
    (cc_re, cc_im), s5_out_vjp = jax.vjp(_s5_out_tables, W['s5_c_re'], W['s5_c_im'])
    bb_re_b, bb_im_b, cc_re_b, cc_im_b = (t.astype(BF16) for t in (bb_re, bb_im, cc_re, cc_im))
    s5_d_row = row(W['s5_d'])
    tril = jnp.tril(jnp.ones((GM_CHUNK, GM_CHUNK), bool))
    w_s = jnp.where(tril[None], W['gm_w_s'], 0.0).astype(BF16)
    w_s_t = w_s.transpose(0, 2, 1)
    gm_bias = jnp.repeat(W['gm_b_s'].T, d // GM_GROUPS, axis=1)

    def in_proj_body(x_ref, g_ref, w_ref, zgm_ref, u5_ref, zga_ref, zgb_ref, h_ref):
        hb = _rms(x_ref[...], g_ref[...]).astype(BF16)
        h_ref[...] = hb
        for lo in range(0, w_ref.shape[0], 512):
            acc = _dot_nt(hb, w_ref[lo:lo + 512, :])
            if lo < d2:
                zgm_ref[:, lo:lo + 512] = acc.astype(BF16)
            elif lo < d2 + dh:
                u5_ref[...] = acc
            elif lo < d2 + dh + d:
                zga_ref[:, lo - d2 - dh:lo - d2 - dh + 512] = acc.astype(BF16)
            else:
                zgb_ref[:, lo - d2 - dh - d:lo - d2 - dh - d + 512] = acc.astype(BF16)

    z_gm, u5, z_ga, z_gb, h0, g_gu0 = _rowcall(
        "in_proj", in_proj_body, n, tm, [(x2d, 'row'), (row(W['g_mix_pre']), 'full'), (w_in_t, 'full')],
        [(d2, BF16), (dh, F32), (d, BF16), (d, BF16), (d, BF16)], gather=[gu_halves[0]])

    gw = d // GM_GROUPS

    def gm_fwd_body(z_ref, lng_ref, lnb_ref, ws_ref, bias_ref, y_ref, u_ref, vh_ref, dgelu_ref, rstd_ref):
        zg, dgelu = _gelu_and_grad(z_ref[...].astype(F32))
        dgelu_ref[...] = dgelu.astype(BF16)
        u, v = zg[:, :d], zg[:, d:]
        vc = v - jnp.mean(v, axis=-1, keepdims=True)
        rstd = lax.rsqrt(jnp.mean(vc * vc, axis=-1, keepdims=True) + EPS)
        vhat = vc * rstd
        u_ref[...] = u.astype(BF16)
        vh_ref[...] = vhat.astype(BF16)
        rstd_ref[...] = jnp.broadcast_to(rstd, rstd_ref.shape)
        vnb = (vhat * lng_ref[...] + lnb_ref[...]).astype(BF16)
        sv = jnp.concatenate([_dot(ws_ref[g], vnb[:, g * gw:(g + 1) * gw]) for g in range(GM_GROUPS)], axis=1)
        y_ref[...] = (u * (sv + bias_ref[...])).astype(BF16)

    ride = ['w_mix_out', 'ca_w_q', 's5_w_glu', 'w_br_gm', 'w_br_s5']
    y_gm, gm_u, gm_vhat, gm_dgelu, gm_rstd, *got = _rowcall(
        "gmlp_fwd", gm_fwd_body, n, GM_CHUNK,
        [(z_gm, 'row'), (row(W['gm_ln_g']), 'full'), (row(W['gm_ln_b']), 'full'), (w_s, 'full'), (gm_bias, 'full')],
        [(d, BF16), (d, BF16), (d, BF16), (d2, BF16), (128, F32)], gather=[shard_b[k] for k in ride])
    w_mix_f, w_q_f, w_glu_f, w_brgm_f, w_brs5_f = (assemble(k, g) for k, g in zip(ride, got))

    tc = seq // SCAN_CHUNKS
    lt = S5_GROUPS * S5_STATE

    def to_scan_order(t):
        return t.reshape(nb, SCAN_CHUNKS, tc, t.shape[-1]).transpose(0, 2, 1, 3).reshape(n, t.shape[-1])

    def from_scan_order(t):
        return t.reshape(nb, tc, SCAN_CHUNKS, t.shape[-1]).transpose(0, 2, 1, 3).reshape(n, t.shape[-1])

    u5p = to_scan_order(u5)
    a_cat, bb_cat, cc_cat = _s5_cat_tables(a_re, a_im, bb_re_b, bb_im_b, cc_re_b, cc_im_b)
    a_cat_conj = _s5_cat_tables(a_re, -a_im, bb_re_b, bb_im_b, cc_re_b, cc_im_b)[0]
    s_cat, y_lin, g_kv, g_o = _s5_scan_fwd(u5p, a_cat, bb_cat, cc_cat, nb,
                                           gather=[shard_b['ca_w_kv'], shard_b['ca_w_o']])
    w_kv_f, w_o_f = assemble('ca_w_kv', g_kv), assemble('ca_w_o', g_o)

    def s5_out_body(yl_ref, u_ref, d_ref, wg_ref, ypre_ref, gate_ref, y_ref):
        ypre = yl_ref[...] + d_ref[...] * u_ref[...]
        ypre_ref[...] = ypre
        yg = _gelu(ypre)
        gate = _dot(yg.astype(BF16), wg_ref[...])
        gate_ref[...] = gate
        y_ref[...] = (yg * _sigmoid(gate)).astype(BF16)

    tm5 = min(1024, n)
    y_pre, gate, y_s5p = _rowcall(
        "s5_out", s5_out_body, n, tm5, [(y_lin, 'row'), (u5p, 'row'), (s5_d_row, 'full'), (w_glu_f, 'full')],
        [(dh, F32), (dh, F32), (dh, BF16)])
    y_s5 = from_scan_order(y_s5p)

    def merge_body(ygm_ref, ys5_ref, zga_ref, zgb_ref, wa_ref, wb_ref, pa_ref, pb_ref, mg_ref):
        pa = _dot(ygm_ref[...], wa_ref[...])
        pb = _dot(ys5_ref[...], wb_ref[...])
        pa_ref[...] = pa.astype(BF16)
        pb_ref[...] = pb.astype(BF16)
        mg_ref[...] = (_sigmoid(zga_ref[...].astype(F32)) * pa + _sigmoid(zgb_ref[...].astype(F32)) * pb).astype(BF16)

    p_a, p_b, merged, g_down = _rowcall(
        "merge", merge_body, n, tmw,
        [(y_gm, 'row'), (y_s5, 'row'), (z_ga, 'row'), (z_gb, 'row'), (w_brgm_f, 'full'), (w_brs5_f, 'full')],
        [(d, BF16), (d, BF16), (d, BF16)], gather=[shard_b['ffn_w_down']], two_level=True)
    w_down_f = assemble('ffn_w_down', g_down)

    def close_sublayer(name, a_in, w_out, x_res, g_post, g_next, w_next=None, gather=(), two_level=False):
        def body(*refs):
            a_ref, w_ref, x_ref, gp_ref, gn_ref = refs[:5]
            rest = refs[5:]
            if w_next is not None:
                wn_ref, rest = rest[0], rest[1:]
            o_ref, xo_ref, h_ref = rest[:3]
            o = _dot(a_ref[...], w_ref[...])
            o_ref[...] = o.astype(BF16)
            xo = x_ref[...] + _rms(o, gp_ref[...])
            xo_ref[...] = xo
            hb = _rms(xo, gn_ref[...]).astype(BF16)
            h_ref[...] = hb
            if w_next is not None:
                rest[3][...] = _dot(hb, wn_ref[...]).astype(BF16)

        ins = [(a_in, 'row'), (w_out, 'full'), (x_res, 'row'), (row(g_post), 'full'), (row(g_next), 'full')]
        outs = [(d, BF16), (d, F32), (d, BF16)]
        if w_next is not None:
            ins.append((w_next, 'full'))
            outs.append((w_next.shape[1], BF16))
        return _rowcall(name, body, n, tmw, ins, outs, gather=gather, two_level=two_level)

    o1, x1, hc, q = close_sublayer("mix_out", merged, w_mix_f, x2d, W['g_mix_post'], W['g_ca_pre'], w_q_f)

    tmm = min(ROW_TILE, nb * nmem)

    def memkv_body(m_ref, g_ref, w_ref, mn_ref, k_ref, v_ref):
        mnb = _rms(m_ref[...], g_ref[...]).astype(BF16)
        mn_ref[...] = mnb
        k_ref[...] = _dot(mnb, w_ref[:, :d]).astype(BF16)
        v_ref[...] = _dot(mnb, w_ref[:, d:]).astype(BF16)

    mem_n, k_mem, v_mem = _rowcall("mem_kv", memkv_body, nb * nmem, tmm,
                                   [(mem2d, 'row'), (row(W['g_mem']), 'full'), (w_kv_f, 'full')],
                                   [(d, BF16), (d, BF16), (d, BF16)])

    tiles_per_ex = seq // tm
    kv_spec = pl.BlockSpec((nmem, d), lambda i: (i // tiles_per_ex, 0))
    scale = hd ** -0.5

    def softmax_rows(qh, kh):
        s = _dot_nt(qh, kh) * scale
        e = jnp.exp(s - jnp.max(s, axis=-1, keepdims=True))
        return e / jnp.sum(e, axis=-1, keepdims=True)

    def attn_body(q_ref, k_ref, v_ref, o_ref):
        for h in range(CA_HEADS):
            sl = slice(h * hd, (h + 1) * hd)
            p = softmax_rows(q_ref[:, sl], k_ref[:, sl])
            o_ref[:, sl] = _dot(p.astype(BF16), v_ref[:, sl]).astype(BF16)

    (att,) = _rowcall("attn_fwd", attn_body, n, tm, [(q, 'row'), (k_mem, kv_spec), (v_mem, kv_spec)], [(d, BF16)])

    o2, x2, hf, g_gu1 = close_sublayer("attn_out", att, w_o_f, x1, W['g_ca_post'], W['g_ffn_pre'],
                                       gather=[gu_halves[1]], two_level=True)
    w_gu_t = jnp.stack([g_gu0, g_gu1], axis=1).reshape(2 * ffh, d)

    ck = 256

    def ffn_up_body(h_ref, w_ref, gu_ref, a_ref):
        hb = h_ref[...]
        for lo in range(0, ffh, ck):
            gt = _dot_nt(hb, w_ref[lo:lo + ck, :])
            ut = _dot_nt(hb, w_ref[ffh + lo:ffh + lo + ck, :])
            gu_ref[:, lo:lo + ck] = gt.astype(BF16)
            gu_ref[:, ffh + lo:ffh + lo + ck] = ut.astype(BF16)
            a_ref[:, lo:lo + ck] = ((gt * _sigmoid(gt)) * ut).astype(BF16)

    gu, act = _rowcall("ffn_up", ffn_up_body, n, tm, [(hf, 'row'), (w_gu_t, 'full')], [(2 * ffh, BF16), (ffh, BF16)])

    def ffn_down_body(a_ref, w_ref, x_ref, t_ref, g_ref, dx_ref, do_ref, loss_ref, dg_ref):
        i = pl.program_id(0)

        @pl.when(i == 0)
        def _():
            loss_ref[...] = jnp.zeros_like(loss_ref)
            dg_ref[...] = jnp.zeros_like(dg_ref)

        o = _dot(a_ref[...], w_ref[...])
        diff = x_ref[...] + _rms(o, g_ref[...]) - t_ref[...]
        loss_ref[...] += jnp.full(loss_ref.shape, 0.5 / d, F32) * jnp.sum(diff * diff)
        dx = diff * (1.0 / d)
        dx_ref[...] = dx
        do, dg = _rms_bwd(o, g_ref[...], dx)
        do_ref[...] = do.astype(BF16)
        dg_ref[...] += dg

    dx3, do3, loss_part, dg_ffn_post = _rowcall(
        "ffn_down_loss", ffn_down_body, n, tmf,
        [(act, 'row'), (w_down_f, 'full'), (x2, 'row'), (tgt, 'row'), (row(W['g_ffn_post']), 'full')],
        [(d, F32), (d, BF16)], accs=[((1, 128), F32), ((1, d), F32)])

    G = {'g_ffn_post': dg_ffn_post}
    RECV = {}

    def parts_of(k, gfull):
        r, c = W[k].shape
        return gfull.reshape(r, N_DEV, c).transpose(1, 0, 2) if SHARDED[k] == 'col' else gfull.reshape(N_DEV, r, c)

    p_down = parts_of('ffn_w_down', _dwcall("dw_ffn_down", act, do3, BF16))

    def ffn_act_bwd_body(do_ref, w_ref, gu_ref, dgu_ref):
        dob = do_ref[...]
        for lo in range(0, ffh, ck):
            da = _dot_nt(dob, w_ref[lo:lo + ck, :])
            gt = gu_ref[:, lo:lo + ck].astype(F32)
            ut = gu_ref[:, ffh + lo:ffh + lo + ck].astype(F32)
            sg = _sigmoid(gt)
            dgu_ref[:, lo:lo + ck] = (da * ut * (sg * (1.0 + gt * (1.0 - sg)))).astype(BF16)
            dgu_ref[:, ffh + lo:ffh + lo + ck] = (da * (gt * sg)).astype(BF16)

    dgu, RECV['ffn_w_down'] = _rowcall("ffn_act_bwd", ffn_act_bwd_body, n, tmf,
                                       [(do3, 'row'), (w_down_f, 'full'), (gu, 'row')], [(2 * ffh, BF16)],
                                       scatter=[p_down])
    p_gu = parts_of('ffn_w_gu', _dwcall("dw_ffn_gu", dgu, hf, BF16))

    def open_sublayer(name, pieces, w_full, x_in, g_pre, dx_up, o_prev=None, g_post_prev=None, scatter=(), gather=(),
                      w_t=False):
        n_p = len(pieces)
        second = o_prev is not None

        def body(*refs):
            dc_refs, (w_ref, x_ref, g_ref, dxu_ref), rest = refs[:n_p], refs[n_p:n_p + 4], refs[n_p + 4:]
            if second:
                (op_ref, gp_ref), rest = rest[:2], rest[2:]
            i = pl.program_id(0)
            dhid = None
            for dc_ref, (_, lo, hi) in zip(dc_refs, pieces):
                part = _dot(dc_ref[...], w_ref[lo:hi, :]) if w_t else _dot_nt(dc_ref[...], w_ref[:, lo:hi])
                dhid = part if dhid is None else dhid + part
            dxn, dg = _rms_bwd(x_ref[...], g_ref[...], dhid)
            dx = dxu_ref[...] + dxn
            if second:
                dx_ref, do_ref, dg_ref, dg2_ref = rest
            else:
                dx_ref, dg_ref = rest

            @pl.when(i == 0)
            def _():
                dg_ref[...] = jnp.zeros_like(dg_ref)
                if second:
                    dg2_ref[...] = jnp.zeros_like(dg2_ref)

            dx_ref[...] = dx
            dg_ref[...] += dg
            if second:
                do, dg2 = _rms_bwd(op_ref[...].astype(F32), gp_ref[...], dx)
                do_ref[...] = do.astype(BF16)
                dg2_ref[...] += dg2

        ins = [(p[0], 'row') for p in pieces] + [(w_full, 'full'), (x_in, 'row'), (row(g_pre), 'full'), (dx_up, 'row')]
        outs = [(d, F32)]
        accs = [((1, d), F32)]
        if second:
            ins += [(o_prev, 'row'), (row(g_post_prev), 'full')]
            outs.append((d, BF16))
            accs.append(((1, d), F32))
        light = w_full.size <= d * d
        res = _rowcall(name, body, n, tmw if light else tm, ins, outs, accs=accs, scatter=scatter, gather=gather)
        if second:
            dx, do, dg, dg2 = res[:4]
            return dx, dg, do, dg2, res[4:]
        return res[0], res[1], res[2:]

    dx2, G['g_ffn_pre'], do2, G['g_ca_post'], _ = open_sublayer(
        "ffn_in_bwd", [(dgu, 0, 2 * ffh)], w_gu_t, x2, W['g_ffn_pre'], dx3, o2, W['g_ca_post'], w_t=True)
    p_o = parts_of('ca_w_o', _dwcall("dw_ca_o", att, do2, BF16))

    def attn_bwd_body(q_ref, k_ref, v_ref, do_ref, wo_ref, dq_ref, dk_ref, dv_ref):
        i = pl.program_id(0)

        @pl.when(i % tiles_per_ex == 0)
        def _():
            dk_ref[...] = jnp.zeros_like(dk_ref)
            dv_ref[...] = jnp.zeros_like(dv_ref)

        d_att = _dot_nt(do_ref[...], wo_ref[...]).astype(BF16)
        for h in range(CA_HEADS):
            sl = slice(h * hd, (h + 1) * hd)
            qh, kh, vh, dah = q_ref[:, sl], k_ref[:, sl], v_ref[:, sl], d_att[:, sl]
            s_t = _dot_nt(kh, qh) * scale
            e_t = jnp.exp(s_t - jnp.max(s_t, axis=0, keepdims=True))
            p_t = e_t / jnp.sum(e_t, axis=0, keepdims=True)
            dp_t = _dot_nt(vh, dah)
            ds_t = (p_t * (dp_t - jnp.sum(p_t * dp_t, axis=0, keepdims=True)) * scale).astype(BF16)
            dq_ref[:, sl] = _dot_tn(ds_t, kh).astype(BF16)
            dk_ref[:, sl] += _dot(ds_t, qh)
            dv_ref[:, sl] += _dot(p_t.astype(BF16), dah)

    kv_acc = ((nb * nmem, d), (F32, pl.BlockSpec((nmem, d), lambda i: (i // tiles_per_ex, 0))))
    dq, dk_mem, dv_mem, RECV['ca_w_o'] = _rowcall(
        "attn_bwd", attn_bwd_body, n, tm,
        [(q, 'row'), (k_mem, kv_spec), (v_mem, kv_spec), (do2, 'row'), (w_o_f, 'full')], [(d, BF16)],
        accs=[kv_acc, kv_acc], scatter=[p_o])
    p_q = parts_of('ca_w_q', _dwcall("dw_ca_q", hc, dq, BF16))

    def memkv_bwd_body(dk_ref, dv_ref, m_ref, g_ref, w_ref, dkv_ref, dg_ref):
        i = pl.program_id(0)

        @pl.when(i == 0)
        def _():
            dg_ref[...] = jnp.zeros_like(dg_ref)

        dkb, dvb = dk_ref[...].astype(BF16), dv_ref[...].astype(BF16)
        dkv_ref[:, :d] = dkb
        dkv_ref[:, d:] = dvb
        dmn = _dot_nt(dkb, w_ref[:, :d]) + _dot_nt(dvb, w_ref[:, d:])
        _, dg = _rms_bwd(m_ref[...], g_ref[...], dmn)
        dg_ref[...] += dg

    dkv, G['g_mem'] = _rowcall(
        "mem_kv_bwd", memkv_bwd_body, nb * nmem, tmm,
        [(dk_mem, 'row'), (dv_mem, 'row'), (mem2d, 'row'), (row(W['g_mem']), 'full'), (w_kv_f, 'full')],
        [(d2, BF16)], accs=[((1, d), F32)])
    p_kv = parts_of('ca_w_kv', _dwcall("dw_ca_kv", mem_n, dkv, BF16))

    dx1, G['g_ca_pre'], do1, G['g_mix_post'], _ = open_sublayer(
        "attn_in_bwd", [(dq, 0, d)], w_q_f, x1, W['g_ca_pre'], dx2, o1, W['g_mix_post'])
    p_mix = parts_of('w_mix_out', _dwcall("dw_mix_out", merged, do1, BF16))

    def merge_bwd_body(do_ref, wm_ref, zga_ref, zgb_ref, pa_ref, pb_ref, wb_ref, dpa_ref, dpb_ref, dza_ref, dzb_ref,
                       dys_ref):
        dm = _dot_nt(do_ref[...], wm_ref[...])
        sa, sb = _sigmoid(zga_ref[...].astype(F32)), _sigmoid(zgb_ref[...].astype(F32))
        dpb = (dm * sb).astype(BF16)
        dpa_ref[...] = (dm * sa).astype(BF16)
        dpb_ref[...] = dpb
        dza_ref[...] = (dm * pa_ref[...].astype(F32) * (sa * (1.0 - sa))).astype(BF16)
        dzb_ref[...] = (dm * pb_ref[...].astype(F32) * (sb * (1.0 - sb))).astype(BF16)
        dys_ref[...] = _dot_nt(dpb, wb_ref[...]).astype(BF16)

    NORM_ROWS = ['g_ffn_post', 'g_ffn_pre', 'g_ca_post', 'g_mem', 'g_ca_pre', 'g_mix_post']
    norm_rows = jnp.concatenate([G[k] for k in NORM_ROWS] + [jnp.tile(loss_part, (1, d // 128))], axis=0)
    dp_a, dp_b, dz_ga, dz_gb, dy_s5, all_norm_rows, RECV['w_mix_out'], RECV['ca_w_q'] = _rowcall(
        "merge_bwd", merge_bwd_body, n, tmw,
        [(do1, 'row'), (w_mix_f, 'full'), (z_ga, 'row'), (z_gb, 'row'), (p_a, 'row'), (p_b, 'row'), (w_brs5_f, 'full')],
        [(d, BF16), (d, BF16), (d, BF16), (d, BF16), (dh, BF16)], scatter=[p_mix, p_q], gather=[norm_rows])
    p_brgm = parts_of('w_br_gm', _dwcall("dw_br_gm", y_gm, dp_a, BF16))
    p_brs5 = parts_of('w_br_s5', _dwcall("dw_br_s5", y_s5, dp_b, BF16))

    def gm_bwd_body(u_ref, vh_ref, dgelu_ref, rstd_ref, dpa_ref, wa_ref, lng_ref, lnb_ref, ws_ref, wst_ref, bias_ref,
                    dz_ref, dws_ref, dbias_ref, dlng_ref, dlnb_ref):
        i = pl.program_id(0)

        @pl.when(i == 0)
        def _():
            dws_ref[...] = jnp.zeros_like(dws_ref)
            dbias_ref[...] = jnp.zeros_like(dbias_ref)
            dlng_ref[...] = jnp.zeros_like(dlng_ref)
            dlnb_ref[...] = jnp.zeros_like(dlnb_ref)

        u, vhat, rstd = u_ref[...].astype(F32), vh_ref[...].astype(F32), rstd_ref[:, :1]
        vnb = (vhat * lng_ref[...] + lnb_ref[...]).astype(BF16)
        dy = _dot_nt(dpa_ref[...], wa_ref[...])
        groups = [slice(g * gw, (g + 1) * gw) for g in range(GM_GROUPS)]
        dsv = dy * u
        dsvb = dsv.astype(BF16)
        dbias_ref[...] += dsv
        sv = jnp.concatenate([_dot(ws_ref[g], vnb[:, sl]) for g, sl in enumerate(groups)], axis=1) + bias_ref[...]
        dvn = jnp.concatenate([_dot(wst_ref[g], dsvb[:, sl]) for g, sl in enumerate(groups)], axis=1)
        for g, sl in enumerate(groups):
            dws_ref[g] += _dot_nt(dsvb[:, sl], vnb[:, sl])
        du = dy * sv
        dlng_ref[...] += jnp.sum(dvn * vhat, axis=0, keepdims=True)
        dlnb_ref[...] += jnp.sum(dvn, axis=0, keepdims=True)
        dvh = dvn * lng_ref[...]
        dv = rstd * (dvh - jnp.mean(dvh, axis=-1, keepdims=True) - vhat * jnp.mean(dvh * vhat, axis=-1, keepdims=True))
        dz_ref[:, :d] = (du * dgelu_ref[:, :d].astype(F32)).astype(BF16)
        dz_ref[:, d:] = (dv * dgelu_ref[:, d:].astype(F32)).astype(BF16)

    dz_gm, dws_full, dbias_full, G['gm_ln_g'], G['gm_ln_b'], RECV['w_br_gm'], RECV['w_br_s5'], RECV['ca_w_kv'] = _rowcall(
        "gmlp_bwd", gm_bwd_body, n, GM_CHUNK,
        [(gm_u, 'row'), (gm_vhat, 'row'), (gm_dgelu, 'row'), (gm_rstd, 'row'), (dp_a, 'row'), (w_brgm_f, 'full'),
         (row(W['gm_ln_g']), 'full'), (row(W['gm_ln_b']), 'full'), (w_s, 'full'), (w_s_t, 'full'), (gm_bias, 'full')],
        [(d2, BF16)], accs=[((GM_GROUPS, GM_CHUNK, GM_CHUNK), F32), ((GM_CHUNK, d), F32), ((1, d), F32), ((1, d), F32)],
        scatter=[p_brgm, p_brs5, p_kv])
    G['gm_w_s'] = jnp.where(tril[None], dws_full, 0.0)
    G['gm_b_s'] = dbias_full.reshape(GM_CHUNK, GM_GROUPS, gw).sum(-1).T

    dy_s5p = to_scan_order(dy_s5)

    def s5_out_bwd_body(dy_ref, ypre_ref, gate_ref, u_ref, wg_ref, dyp_ref, dgate_ref, yg_ref, dd_ref):
        i = pl.program_id(0)

        @pl.when(i == 0)
        def _():
            dd_ref[...] = jnp.zeros_like(dd_ref)

        dy = dy_ref[...].astype(F32)
        yg, dgelu = _gelu_and_grad(ypre_ref[...])
        sg = _sigmoid(gate_ref[...])
        dgb = (dy * yg * (sg * (1.0 - sg))).astype(BF16)
        dgate_ref[...] = dgb
        yg_ref[...] = yg.astype(BF16)
        dyp = (dy * sg + _dot_nt(dgb, wg_ref[...])) * dgelu
        dyp_ref[...] = dyp
        dd_ref[...] += jnp.sum(dyp * u_ref[...], axis=0, keepdims=True)

    dy_pre, dgate, yg_b, dd = _rowcall(
        "s5_out_bwd", s5_out_bwd_body, n, tm5,
        [(dy_s5p, 'row'), (y_pre, 'row'), (gate, 'row'), (u5p, 'row'), (w_glu_f, 'full')],
        [(dh, F32), (dh, BF16), (dh, BF16)], accs=[((1, dh), F32)])
    gm_ln_rows = jnp.concatenate([G['gm_ln_g'], G['gm_ln_b']], axis=0)
    du5p, d_bb_cat, d_cc_cat, da_cat, all_gm_w_s, all_gm_b_s, all_gm_ln, RECV['ffn_w_gu'] = _s5_scan_bwd(
        dy_pre, u5p, s_cat, a_cat_conj, bb_cat, cc_cat, s5_d_row, gather=[G['gm_w_s'], G['gm_b_s'], gm_ln_rows],
        scatter=[p_gu])
    du5 = from_scan_order(du5p.astype(BF16))
    d_bb_re, d_bb_im = _s5_uncat(jnp.sum(d_bb_cat, axis=0), 2)
    d_cc_re, d_cc_neg_im = _s5_uncat(jnp.sum(d_cc_cat, axis=0), 1)
    da_re, da_im = _s5_uncat(jnp.sum(da_cat, axis=0), 1)
    lane_shape = (S5_GROUPS, 8, 128)
    g_c_re, g_c_im = (t.reshape(lane_shape) for t in s5_out_vjp((d_cc_re, -d_cc_neg_im)))
    g_d = dd.reshape(S5_GROUPS, S5_CH)
    g_lam_re, g_lam_im, g_log_step, g_b_re, g_b_im = s5_in_vjp((da_re, da_im, d_bb_re, d_bb_im))
    g_log_step, g_b_re, g_b_im = g_log_step.reshape(1, -1), g_b_re.reshape(lane_shape), g_b_im.reshape(lane_shape)
    p_glu = parts_of('s5_w_glu', _dwcall("dw_s5_glu", yg_b, dgate, BF16))

    pieces = [(dz_gm, 0, d2), (du5, d2, d2 + dh), (dz_ga, d2 + dh, d2 + dh + d), (dz_gb, d2 + dh + d, d2 + dh + 2 * d)]
    dw_gm, all_lam_re, all_lam_im, all_log_step, all_b_re, all_b_im = _dwcall(
        "dw_in_0", dz_gm, h0, BF16, gather=[g_lam_re, g_lam_im, g_log_step, g_b_re, g_b_im])
    dw_s5 = _dwcall("dw_in_1", du5, h0, BF16)
    dw_ga = _dwcall("dw_in_2", dz_ga, h0, BF16)
    dw_gb, all_c_re, all_c_im, all_d, RECV['s5_w_glu'] = _dwcall(
        "dw_in_3", dz_gb, h0, BF16, gather=[g_c_re, g_c_im, g_d], scatter=[p_glu])
    p_in = parts_of('w_in', jnp.concatenate([dw_gm, dw_s5, dw_ga, dw_gb], axis=0))
    grad_x2d, g_mix_pre_part, (RECV['w_in'],) = open_sublayer(
        "in_proj_bwd", pieces, w_in_t, x2d, W['g_mix_pre'], dx1, scatter=[p_in], w_t=True)

    out = {}
    for k in names:
        quad = _sum_adamw("sum_adamw_" + k, RECV[k], W[k], M[k], V[k])
        out[k] = [(t.T if SHARDED[k] == 'colT' else t)[None] for t in quad]
    gathered = [all_norm_rows, all_gm_w_s, all_gm_b_s, all_gm_ln, all_c_re, all_c_im, all_d, all_lam_re, all_lam_im,
                all_log_step, all_b_re, all_b_im, _gather_last(g_mix_pre_part)]
    where = {'g_ffn_post': (0, 0), 'g_ffn_pre': (0, 1), 'g_ca_post': (0, 2), 'g_mem': (0, 3), 'g_ca_pre': (0, 4),
             'g_mix_post': (0, 5), 'gm_w_s': (1, None), 'gm_b_s': (2, None), 'gm_ln_g': (3, 0), 'gm_ln_b': (3, 1),
             's5_c_re': (4, None), 's5_c_im': (5, None), 's5_d': (6, None), 's5_lam_re': (7, None),
             's5_lam_im': (8, None), 's5_log_step': (9, 0), 's5_b_re': (10, None), 's5_b_im': (11, None),
             'g_mix_pre': (12, 0)}
    folded = ('s5_b_re', 's5_b_im', 's5_c_re', 's5_c_im')

    def as_updated(k, t):
        return t.reshape((1,) + lane_shape) if k in folded else t

    small, loss_row = _small_adamw(
        [(as_updated(k, args[k]), as_updated(k, args['m_' + k]), as_updated(k, args['v_' + k])) + where[k]
         for k in SMALL], gathered, (0, len(NORM_ROWS)))
    out.update({k: [t.reshape(args[k].shape) for t in quad] for k, quad in zip(SMALL, small)})

    res = [loss_row[0, 0], grad_x2d.reshape(x.shape)]
    for j in range(4):
        res += [out[k][j] for k in WEIGHTS]
    return tuple(res)
```

```python
import functools
import math

import jax
import jax.numpy as jnp
from jax import lax
from jax.experimental import pallas as pl
from jax.experimental.pallas import tpu as pltpu

F32 = jnp.float32
BF16 = jnp.bfloat16
EPS = 1e-6
N_DEV = 8
V7X_VMEM_LIMIT = 56 * 1024 * 1024
ROW_TILE = 256
WIDE_ROW_TILE = 1024
FFN_ROW_TILE = 512
GM_CHUNK = 128
GM_GROUPS = 8
S5_GROUPS = 32
S5_STATE = 64
S5_CH = 16
SCAN_CHUNKS = 32
SCAN_LANES = 128
S5_BLOCK_LANES = 256
CA_HEADS = 4
ADAM_LR, ADAM_B1, ADAM_B2, ADAM_EPS, ADAM_WD, ADAM_STEP = 0.001, 0.9, 0.999, 1e-08, 0.01, 10

WEIGHTS = ['g_mix_pre', 'w_in', 'gm_ln_g', 'gm_ln_b', 'gm_w_s', 'gm_b_s', 's5_lam_re', 's5_lam_im', 's5_log_step',
           's5_b_re', 's5_b_im', 's5_c_re', 's5_c_im', 's5_d', 's5_w_glu', 'w_br_gm', 'w_br_s5', 'w_mix_out',
           'g_mix_post', 'g_ca_pre', 'g_mem', 'ca_w_q', 'ca_w_kv', 'ca_w_o', 'g_ca_post', 'g_ffn_pre', 'ffn_w_gu',
           'ffn_w_down', 'g_ffn_post']
SHARDED = {'w_in': 'colT', 's5_w_glu': 'row', 'w_br_gm': 'row', 'w_br_s5': 'col', 'w_mix_out': 'row',
           'ca_w_q': 'row', 'ca_w_kv': 'col', 'ca_w_o': 'row', 'ffn_w_gu': 'colT', 'ffn_w_down': 'row'}
SMALL = [n for n in WEIGHTS if n not in SHARDED]


def _rms(x, g):
    r = lax.rsqrt(jnp.mean(x * x, axis=-1, keepdims=True) + EPS)
    return (x * r) * g


def _rms_bwd(x, g, dy):
    r = lax.rsqrt(jnp.mean(x * x, axis=-1, keepdims=True) + EPS)
    n = x * r
    dn = dy * g
    dx = r * (dn - n * jnp.mean(dn * n, axis=-1, keepdims=True))
    return dx, jnp.sum(dy * n, axis=0, keepdims=True)


_GELU_C = math.sqrt(2.0 / math.pi)


def _gelu(x):
    return 0.5 * x * (1.0 + jnp.tanh(_GELU_C * (x + 0.044715 * (x * x * x))))


def _gelu_and_grad(x):
    x2 = x * x
    t = jnp.tanh(_GELU_C * (x + 0.044715 * (x2 * x)))
    h = 0.5 * (1.0 + t)
    return x * h, h + 0.5 * x * (1.0 - t * t) * (_GELU_C * (1.0 + 3.0 * 0.044715 * x2))


def _sigmoid(x):
    return 0.5 * (1.0 + jnp.tanh(0.5 * x))


def _dot(a, b):
    return jnp.dot(a, b, preferred_element_type=F32)


def _dot_nt(a, b):
    return lax.dot_general(a, b, (((1,), (1,)), ((), ())), preferred_element_type=F32)


def _dot_tn(a, b):
    return lax.dot_general(a, b, (((0,), (0,)), ((), ())), preferred_element_type=F32)


def _adamw(w, g, m, v):
    m = ADAM_B1 * m + (1.0 - ADAM_B1) * g
    v = ADAM_B2 * v + (1.0 - ADAM_B2) * (g * g)
    m_hat = m / (1.0 - ADAM_B1 ** ADAM_STEP)
    v_hat = v / (1.0 - ADAM_B2 ** ADAM_STEP)
    delta = -ADAM_LR * (m_hat / (jnp.sqrt(v_hat) + ADAM_EPS) + ADAM_WD * w)
    return delta, m, v


def _params(n_grid):
    return pltpu.CompilerParams(dimension_semantics=("arbitrary",) * n_grid, vmem_limit_bytes=V7X_VMEM_LIMIT)


def _my_place():
    x, y, c = lax.axis_index("x"), lax.axis_index("y"), lax.axis_index("c")
    return x, y, c


def _peer(x, y, c, k):
    px = 1 - x if k & 4 else x
    py = 1 - y if k & 2 else y
    pc = 1 - c if k & 1 else c
    return (px, py, pc), 4 * px + 2 * py + pc


def _exchange(kind, src_refs, dst_refs, sems, first, phase):
    x, y, c = _my_place()
    me = 4 * x + 2 * y + c
    send_sems, recv_sems, own_sems = sems
    for j, (src, dst) in enumerate(zip(src_refs, dst_refs), start=first):
        own = pltpu.make_async_copy(src if kind == 'gather' else src.at[me], dst.at[me], own_sems.at[j])
        if phase == 'start':
            own.start()
        for k in range(1, N_DEV):
            peer, peer_block = _peer(x, y, c, k)
            out = pltpu.make_async_remote_copy(
                src_ref=src if kind == 'gather' else src.at[peer_block], dst_ref=dst.at[me],
                send_sem=send_sems.at[7 * j + k - 1], recv_sem=recv_sems.at[7 * j + k - 1], device_id=peer,
                device_id_type=pl.DeviceIdType.MESH)
            if phase == 'start':
                out.start()
            else:
                pltpu.make_async_remote_copy(
                    src_ref=src if kind == 'gather' else src.at[peer_block], dst_ref=dst.at[peer_block],
                    send_sem=send_sems.at[7 * j + k - 1], recv_sem=recv_sems.at[7 * j + k - 1], device_id=peer,
                    device_id_type=pl.DeviceIdType.MESH).wait_recv()
                out.wait_send()
        if phase == 'wait':
            own.wait()


def _gather_two_level(src_refs, dst_refs, sems, first, phase):
    x, y, c = _my_place()
    me, sibling = (x, y, c), (x, y, 1 - c)
    chips = [(1 - x, y), (x, 1 - y), (1 - x, 1 - y)]
    send_sems, recv_sems, own_sems = sems
    for j, (src, dst) in enumerate(zip(src_refs, dst_refs), start=first):
        def rows(px, py, pc, dst=dst):
            return dst.at[4 * px + 2 * py + pc]

        def copy(k, block, to, from_src=False, j=j, src=src, rows=rows):
            return pltpu.make_async_remote_copy(
                src_ref=src if from_src else rows(*block), dst_ref=rows(*block), send_sem=send_sems.at[7 * j + k],
                recv_sem=recv_sems.at[7 * j + k], device_id=to, device_id_type=pl.DeviceIdType.MESH)

        mine = pltpu.make_async_copy(src, rows(*me), own_sems.at[j])
        first_out = [copy(0, me, sibling, True)] + [copy(1 + i, me, (*chip, c), True) for i, chip in enumerate(chips)]
        if phase == 'start':
            mine.start()
            for cp in first_out:
                cp.start()
        else:
            passed = [copy(4 + i, (*chip, c), sibling) for i, chip in enumerate(chips)]
            for i, chip in enumerate(chips):
                copy(1 + i, (*chip, c), me).wait_recv()
                passed[i].start()
            copy(0, sibling, me).wait_recv()
            for i, chip in enumerate(chips):
                copy(4 + i, (*chip, 1 - c), me).wait_recv()
            for cp in first_out + passed:
                cp.wait_send()
            mine.wait()


class _Ride:
    def __init__(self, gather=(), scatter=(), two_level=False):
        self.two_level = two_level
        self.n_gather = len(gather)
        self.arrays = list(gather) + list(scatter)
        n = len(self.arrays)
        hbm = pl.BlockSpec(memory_space=pl.ANY)
        self.in_specs = [hbm] * n
        self.out_specs = [hbm] * n
        self.out_shape = [jax.ShapeDtypeStruct((N_DEV,) + a.shape, a.dtype) for a in gather]
        self.out_shape += [jax.ShapeDtypeStruct(a.shape, a.dtype) for a in scatter]
        self.scratch = [pltpu.SemaphoreType.DMA((7 * n,)), pltpu.SemaphoreType.DMA((7 * n,)),
                        pltpu.SemaphoreType.DMA((n,))] if n else []

    def wrap(self, inner, n_in, n_out, n_scr, is_first, is_last):
        n_mv = len(self.arrays)
        if not n_mv:
            return inner

        def body(*refs):
            mv_src = refs[n_in:n_in + n_mv]
            mv_dst = refs[n_in + n_mv + n_out:n_in + 2 * n_mv + n_out]
            sems = refs[n_in + 2 * n_mv + n_out + n_scr:]

            def exchange(phase):
                if self.n_gather and self.two_level:
                    _gather_two_level(mv_src[:self.n_gather], mv_dst[:self.n_gather], sems, 0, phase)
                elif self.n_gather:
                    _exchange('gather', mv_src[:self.n_gather], mv_dst[:self.n_gather], sems, 0, phase)
                if n_mv > self.n_gather:
                    _exchange('scatter', mv_src[self.n_gather:], mv_dst[self.n_gather:], sems, self.n_gather, phase)

            pl.when(is_first())(functools.partial(exchange, 'start'))
            inner(*refs[:n_in], *refs[n_in + n_mv:n_in + n_mv + n_out],
                  *refs[n_in + 2 * n_mv + n_out:n_in + 2 * n_mv + n_out + n_scr])
            pl.when(is_last())(functools.partial(exchange, 'wait'))

        return body


def _rowcall(name, body, n_rows, tm, ins, outs, accs=(), scratch=(), gather=(), scatter=(), two_level=False):
    n_steps = n_rows // tm
    ride = _Ride(gather, scatter, two_level)
    body = ride.wrap(body, len(ins), len(outs) + len(accs), len(scratch), lambda: pl.program_id(0) == 0,
                     lambda: pl.program_id(0) == n_steps - 1)
    arrays, in_specs = [], []
    for a, kind in ins:
        arrays.append(a)
        if kind == 'row':
            in_specs.append(pl.BlockSpec((tm,) + a.shape[1:], lambda i, nd=a.ndim: (i,) + (0,) * (nd - 1)))
        elif kind == 'full':
            in_specs.append(pl.BlockSpec(a.shape, lambda i, nd=a.ndim: (0,) * nd, pipeline_mode=pl.Buffered(1)))
        else:
            in_specs.append(kind)
    out_shape, out_specs = [], []
    for cols, dt in outs:
        out_shape.append(jax.ShapeDtypeStruct((n_rows, cols), dt))
        out_specs.append(pl.BlockSpec((tm, cols), lambda i: (i, 0)))
    for shp, dt in accs:
        if isinstance(dt, tuple):
            dt, spec = dt
        else:
            spec = pl.BlockSpec(shp, lambda i, nd=len(shp): (0,) * nd)
        out_shape.append(jax.ShapeDtypeStruct(shp, dt))
        out_specs.append(spec)
    return pl.pallas_call(functools.partial(body), grid=(n_steps,), in_specs=in_specs + ride.in_specs,
                          out_specs=out_specs + ride.out_specs, out_shape=out_shape + ride.out_shape,
                          scratch_shapes=list(scratch) + ride.scratch, name=name,
                          compiler_params=_params(1))(*arrays, *ride.arrays)


DW_ACC_BYTES = 12 * 1024 * 1024
DW_LHS_BYTES = 6 * 1024 * 1024


def _dw_tiles(n, ka, nn, a_itemsize):
    tn = max(t for t in range(128, min(nn, 1536) + 1, 128) if nn % t == 0)
    tka = max(t for t in range(128, ka + 1, 128) if ka % t == 0 and t * tn * 4 <= DW_ACC_BYTES)
    tm = min(n, 2048)
    while tm > 256 and tm * tka * a_itemsize > DW_LHS_BYTES:
        tm //= 2
    return tm, tka, tn


def _dwcall(name, a, dc, out_dtype, gather=(), scatter=()):
    n, ka = a.shape
    nn = dc.shape[1]
    tm, tka, tn = _dw_tiles(n, ka, nn, a.dtype.itemsize)
    n_i, n_j, n_k = n // tm, nn // tn, ka // tka
    ride = _Ride(gather, scatter)

    def body(a_ref, dc_ref, o_ref, acc_ref):
        i = pl.program_id(2)

        @pl.when(i == 0)
        def _():
            acc_ref[...] = jnp.zeros_like(acc_ref)

        acc_ref[...] += _dot_tn(a_ref[...].astype(BF16), dc_ref[...].astype(BF16))

        @pl.when(i == n_i - 1)
        def _():
            o_ref[...] = acc_ref[...].astype(o_ref.dtype)

    def at_step(k, j, i):
        return lambda: (pl.program_id(0) == k) & (pl.program_id(1) == j) & (pl.program_id(2) == i)

    body = ride.wrap(body, 2, 1, 1, at_step(0, 0, 0), at_step(n_k - 1, n_j - 1, n_i - 1))
    res = pl.pallas_call(
        body, grid=(n_k, n_j, n_i),
        in_specs=[pl.BlockSpec((tm, tka), lambda k, j, i: (i, k)), pl.BlockSpec((tm, tn), lambda k, j, i: (i, j))]
        + ride.in_specs,
        out_specs=[pl.BlockSpec((tka, tn), lambda k, j, i: (k, j))] + ride.out_specs,
        out_shape=[jax.ShapeDtypeStruct((ka, nn), out_dtype)] + ride.out_shape,
        scratch_shapes=[pltpu.VMEM((tka, tn), F32)] + ride.scratch, name=name, compiler_params=_params(3))(
            a, dc, *ride.arrays)
    return res if ride.arrays else res[0]


def _first_gather(first, later):
    n = len(later)
    vm = pl.BlockSpec(memory_space=pltpu.VMEM)

    def body(*refs):
        first_ref, later_refs = refs[0], refs[1:1 + n]
        out_ref, cast_refs = refs[1 + n], refs[2 + n:2 + 2 * n]
        stage = refs[2 + 2 * n]
        sems = refs[3 + 2 * n:]
        stage[...] = first_ref[...].astype(BF16)
        _gather_two_level([stage], [out_ref], sems, 0, 'start')
        for src, dst in zip(later_refs, cast_refs):
            dst[...] = src[...].astype(BF16)
        _gather_two_level([stage], [out_ref], sems, 0, 'wait')

    return pl.pallas_call(
        body, out_shape=[jax.ShapeDtypeStruct((N_DEV,) + first.shape, BF16)]
        + [jax.ShapeDtypeStruct(s.shape, BF16) for s in later],
        in_specs=[vm] * (1 + n), out_specs=[pl.BlockSpec(memory_space=pl.ANY)] + [vm] * n,
        scratch_shapes=[pltpu.VMEM(first.shape, BF16), pltpu.SemaphoreType.DMA((7,)), pltpu.SemaphoreType.DMA((7,)),
                        pltpu.SemaphoreType.DMA((1,))],
        name="gather_first", compiler_params=pltpu.CompilerParams(vmem_limit_bytes=V7X_VMEM_LIMIT))(first, *later)


def _sum_adamw(name, recv, w, m, v):
    r, c = w.shape
    steps = max(s for s in (4, 2, 1) if r % s == 0 and (r // s) % 16 == 0 or s == 1)

    def body(recv_ref, w_ref, m_ref, v_ref, g_ref, d_ref, nm_ref, nv_ref):
        g = recv_ref[0].astype(F32)
        for k in range(1, N_DEV):
            g = g + recv_ref[k].astype(F32)
        d, nm, nv = _adamw(w_ref[...], g, m_ref[...], v_ref[...])
        g_ref[...] = g
        d_ref[...] = d
        nm_ref[...] = nm
        nv_ref[...] = nv

    blk = pl.BlockSpec((r // steps, c), lambda i: (i, 0))
    return pl.pallas_call(
        body, grid=(steps,), out_shape=[jax.ShapeDtypeStruct((r, c), F32)] * 4,
        in_specs=[pl.BlockSpec((N_DEV, r // steps, c), lambda i: (0, i, 0)), blk, blk, blk], out_specs=[blk] * 4,
        name=name, compiler_params=_params(1))(recv, w, m, v)


def _gather_last(a):
    def body(a_ref, out_ref, send_sems, recv_sems, own_sems):
        sems = (send_sems, recv_sems, own_sems)
        _exchange('gather', [a_ref], [out_ref], sems, 0, 'start')
        _exchange('gather', [a_ref], [out_ref], sems, 0, 'wait')

    vm = pl.BlockSpec(memory_space=pltpu.VMEM)
    return pl.pallas_call(
        body, out_shape=jax.ShapeDtypeStruct((N_DEV,) + a.shape, a.dtype), in_specs=[vm], out_specs=vm,
        scratch_shapes=[pltpu.SemaphoreType.DMA((7,)), pltpu.SemaphoreType.DMA((7,)), pltpu.SemaphoreType.DMA((1,))],
        name="gather_last", compiler_params=pltpu.CompilerParams(vmem_limit_bytes=V7X_VMEM_LIMIT))(a)


def _small_adamw(entries, gathered, loss_at):
    n, ng = len(entries), len(gathered)

    def body(*refs):
        g_refs, wmv = refs[:ng], refs[ng:ng + 3 * n]
        outs, loss_out = refs[ng + 3 * n:ng + 7 * n], refs[ng + 7 * n]

        def total(ref, r):
            if r is None:
                t = ref[0]
                for k in range(1, N_DEV):
                    t = t + ref[k]
                return t[None]
            t = ref[0, r:r + 1, :]
            for k in range(1, N_DEV):
                t = t + ref[k, r:r + 1, :]
            return t

        for j, (_, _, _, gi, r) in enumerate(entries):
            g = total(g_refs[gi], r)
            d, nm, nv = _adamw(wmv[3 * j][...], g, wmv[3 * j + 1][...], wmv[3 * j + 2][...])
            for ref, val in zip(outs[4 * j:4 * j + 4], (g, d, nm, nv)):
                ref[...] = val
        loss_out[...] = total(g_refs[loss_at[0]], loss_at[1])[:, :128]

    vm = pl.BlockSpec(memory_space=pltpu.VMEM)
    out_shape, arrays = [], list(gathered)
    for w, m, v, _, _ in entries:
        out_shape += [jax.ShapeDtypeStruct(w.shape, F32)] * 4
        arrays += [w, m, v]
    out_shape.append(jax.ShapeDtypeStruct((1, 128), F32))
    res = pl.pallas_call(
        body, out_shape=out_shape, in_specs=[vm] * len(arrays), out_specs=[vm] * len(out_shape),
        name="small_adamw", compiler_params=pltpu.CompilerParams(vmem_limit_bytes=V7X_VMEM_LIMIT))(*arrays)
    return [res[4 * j:4 * j + 4] for j in range(n)], res[4 * n]


def _blockdiag8(t):
    g, per = t.shape[0], 8
    eye = jnp.eye(per, dtype=F32)
    t = t.reshape(g // per, per, t.shape[1], t.shape[2])
    return (t[:, :, :, None, :] * eye[None, :, None, :, None]).reshape(g // per, per * t.shape[2], per * t.shape[3])


def _s5_out_tables(c_re, c_im):
    return _blockdiag8(c_re.transpose(0, 2, 1)), _blockdiag8(c_im.transpose(0, 2, 1))


def _s5_in_tables(lam_re, lam_im, log_step, b_re, b_im):
    step = jnp.exp(log_step)[:, None]
    mag = jnp.exp(lam_re * step)
    ab_re = mag * jnp.cos(lam_im * step)
    ab_im = mag * jnp.sin(lam_im * step)
    den = lam_re * lam_re + lam_im * lam_im
    nr = ab_re - 1.0
    co_re = (nr * lam_re + ab_im * lam_im) / den
    co_im = (ab_im * lam_re - nr * lam_im) / den
    bb_re = co_re[..., None] * b_re - co_im[..., None] * b_im
    bb_im = co_re[..., None] * b_im + co_im[..., None] * b_re
    return (ab_re.reshape(1, -1), ab_im.reshape(1, -1), _blockdiag8(bb_re.transpose(0, 2, 1)),
            _blockdiag8(bb_im.transpose(0, 2, 1)))


def _scan_passes(x_s, a_ref, cr_ref, ci_ref, reverse, re, im):
    tc, nc = x_s.shape[:2]
    ln = SCAN_LANES
    n_sq = int(math.log2(tc))
    assert 2 ** n_sq == tc
    ar = jnp.broadcast_to(a_ref[:, re], (nc, ln))
    ai = jnp.broadcast_to(a_ref[:, im], (nc, ln))
    zero = jnp.zeros((nc, ln), F32)

    def at(t):
        return tc - 1 - t if reverse else t

    def local(t, carry):
        sr, si = carry
        j = at(t)
        nr = ar * sr - ai * si + x_s[j, :, re]
        ni = ar * si + ai * sr + x_s[j, :, im]
        x_s[j, :, re] = nr
        x_s[j, :, im] = ni
        return nr, ni

    lr, li = lax.fori_loop(0, tc, local, (zero, zero))
    pr, pi = a_ref[:, re], a_ref[:, im]
    for _ in range(n_sq):
        pr, pi = pr * pr - pi * pi, 2.0 * (pr * pi)
    cr_ref[...] = lr
    ci_ref[...] = li
    tr = jnp.zeros((1, ln), F32)
    ti = jnp.zeros((1, ln), F32)
    for c in (range(nc - 1, -1, -1) if reverse else range(nc)):
        l_r = cr_ref[c:c + 1, :]
        l_i = ci_ref[c:c + 1, :]
        cr_ref[c:c + 1, :] = tr
        ci_ref[c:c + 1, :] = ti
        tr, ti = pr * tr - pi * ti + l_r, pr * ti + pi * tr + l_i

    def second_pass(on_fixed):
        def fixup(t, carry):
            qr, qi, acc = carry
            j = at(t)
            qr, qi = ar * qr - ai * qi, ar * qi + ai * qr
            gr = x_s[j, :, re] + qr
            gi = x_s[j, :, im] + qi
            x_s[j, :, re] = gr
            x_s[j, :, im] = gi
            return qr, qi, on_fixed(j, gr, gi, acc)

        return fixup

    return second_pass, (cr_ref[...], ci_ref[...]), zero


def _s5_cat_tables(a_re, a_im, bb_re, bb_im, cc_re, cc_im):
    lb = S5_BLOCK_LANES
    n_blk, kq, nq = bb_re.shape
    sub = nq // lb

    def lanes(re, im):
        lead = re.shape[:-1]
        both = jnp.stack([re.reshape(lead + (-1, lb)), im.reshape(lead + (-1, lb))], axis=-2)
        return both.reshape(lead + (-1,))

    cc = jnp.stack([cc_re.reshape(n_blk, sub, lb, kq), -cc_im.reshape(n_blk, sub, lb, kq)], axis=2)
    return lanes(a_re, a_im), lanes(bb_re, bb_im), cc.reshape(n_blk, sub * 2 * lb, kq)


def _s5_uncat(t, axis):
    lb = S5_BLOCK_LANES
    shp = t.shape
    t = t.reshape(shp[:axis] + (-1, 2, lb) + shp[axis + 1:])
    re, im = jnp.take(t, 0, axis=axis + 1), jnp.take(t, 1, axis=axis + 1)
    return re.reshape(shp[:axis] + (-1,) + shp[axis + 1:]), im.reshape(shp[:axis] + (-1,) + shp[axis + 1:])


def _grid2_edges(n0, n1):
    return (lambda: (pl.program_id(0) == 0) & (pl.program_id(1) == 0),
            lambda: (pl.program_id(0) == n0 - 1) & (pl.program_id(1) == n1 - 1))


def _s5_scan_fwd(u5p, a_cat, bb_cat, cc_cat, nb, gather=()):
    n, dh = u5p.shape
    n_blk, kq, cols = bb_cat.shape
    lb, ln, nc = S5_BLOCK_LANES, SCAN_LANES, SCAN_CHUNKS
    sub = cols // (2 * lb)
    rows = n // nb
    tc = rows // nc
    n_l = n_blk * sub

    def body(u_ref, a_ref, b_ref, c_ref, s_ref, y_ref, x_s, car_r, car_i):
        x_s[...] = _dot(u_ref[...].astype(BF16), b_ref[0]).reshape(tc, nc, 2 * lb)
        for h in range(lb // ln):
            re, im = slice(h * ln, (h + 1) * ln), slice(lb + h * ln, lb + (h + 1) * ln)

            def keep(j, gr, gi, acc, re=re, im=im):
                s_ref[0, j, :, re] = gr.astype(BF16)
                s_ref[0, j, :, im] = gi.astype(BF16)
                return acc

            second_pass, start, zero = _scan_passes(x_s, a_ref, car_r, car_i, False, re, im)
            lax.fori_loop(0, tc, second_pass(keep), (*start, zero))
        y = _dot(x_s[...].reshape(rows, 2 * lb).astype(BF16), c_ref[0])

        @pl.when(pl.program_id(1) % sub == 0)
        def _():
            y_ref[...] = y

        @pl.when(pl.program_id(1) % sub != 0)
        def _():
            y_ref[...] += y

    row_blk = pl.BlockSpec((rows, kq), lambda b, l: (b, l // sub))
    ride = _Ride(gather, two_level=True)
    return pl.pallas_call(
        ride.wrap(body, 4, 2, 3, *_grid2_edges(nb, n_l)), grid=(nb, n_l),
        in_specs=[row_blk, pl.BlockSpec((1, 2 * lb), lambda b, l: (0, l)),
                  pl.BlockSpec((1, kq, 2 * lb), lambda b, l: (l // sub, 0, l % sub)),
                  pl.BlockSpec((1, 2 * lb, kq), lambda b, l: (l // sub, l % sub, 0))] + ride.in_specs,
        out_specs=[pl.BlockSpec((1, tc, nc, 2 * lb), lambda b, l: (b, 0, 0, l)), row_blk] + ride.out_specs,
        out_shape=[jax.ShapeDtypeStruct((nb, tc, nc, n_l * 2 * lb), BF16), jax.ShapeDtypeStruct((n, dh), F32)]
        + ride.out_shape,
        scratch_shapes=[pltpu.VMEM((tc, nc, 2 * lb), F32)] + [pltpu.VMEM((nc, ln), F32)] * 2 + ride.scratch,
        name="s5_scan_fwd", compiler_params=_params(2))(u5p, a_cat, bb_cat, cc_cat, *ride.arrays)


def _s5_scan_bwd(dy_pre, u5p, s_cat, a_cat_conj, bb_cat, cc_cat, d_row, gather=(), scatter=()):
    n, dh = u5p.shape
    n_blk, kq, cols = bb_cat.shape
    nb, tc, nc, _ = s_cat.shape
    lb, ln = S5_BLOCK_LANES, SCAN_LANES
    sub = cols // (2 * lb)
    rows = n // nb
    n_l = n_blk * sub

    def body(dy_ref, u_ref, s_ref, a_ref, b_ref, c_ref, d_ref, du_ref, db_ref, dc_ref, da_ref, x_s, car_r, car_i):
        dyb = dy_ref[...].astype(BF16)
        x_s[...] = _dot_nt(dyb, c_ref[0]).reshape(tc, nc, 2 * lb)
        for h in range(lb // ln):
            re, im = slice(h * ln, (h + 1) * ln), slice(lb + h * ln, lb + (h + 1) * ln)

            def with_state_before(j, gr, gi, acc, re=re, im=im):
                dr, di = acc
                pr_, pi_ = s_ref[0, j - 1, :, re].astype(F32), s_ref[0, j - 1, :, im].astype(F32)
                return dr + (pr_ * gr + pi_ * gi), di + (pr_ * gi - pi_ * gr)

            second_pass, start, zero = _scan_passes(x_s, a_ref, car_r, car_i, True, re, im)
            carry = lax.fori_loop(0, tc - 1, second_pass(with_state_before), (*start, (zero, zero)))
            row = lax.broadcasted_iota(jnp.int32, (nc, ln), 0)
            pr_ = jnp.where(row == 0, 0.0, pltpu.roll(s_ref[0, tc - 1, :, re].astype(F32), 1, 0))
            pi_ = jnp.where(row == 0, 0.0, pltpu.roll(s_ref[0, tc - 1, :, im].astype(F32), 1, 0))

            def at_first_time(j, gr, gi, acc, pr_=pr_, pi_=pi_):
                return acc[0] + (pr_ * gr + pi_ * gi), acc[1] + (pr_ * gi - pi_ * gr)

            _, _, (dr, di) = second_pass(at_first_time)(tc - 1, carry)
            da_ref[0, :, re] = jnp.sum(dr, axis=0, keepdims=True)
            da_ref[0, :, im] = jnp.sum(di, axis=0, keepdims=True)

        gb = x_s[...].reshape(rows, 2 * lb).astype(BF16)
        du = _dot_nt(gb, b_ref[0])

        @pl.when(pl.program_id(1) % sub == 0)
        def _():
            du_ref[...] = du + d_ref[...] * dy_ref[...]

        @pl.when(pl.program_id(1) % sub != 0)
        def _():
            du_ref[...] += du

        db_ref[0, 0] = _dot_tn(u_ref[...].astype(BF16), gb)
        dc_ref[0, 0] = _dot_tn(s_ref[0].reshape(rows, 2 * lb), dyb)

    row_blk = pl.BlockSpec((rows, kq), lambda b, l: (b, l // sub))
    in_map = pl.BlockSpec((1, kq, 2 * lb), lambda b, l: (l // sub, 0, l % sub))
    out_map = pl.BlockSpec((1, 2 * lb, kq), lambda b, l: (l // sub, l % sub, 0))
    ride = _Ride(gather, scatter)
    return pl.pallas_call(
        ride.wrap(body, 7, 4, 3, *_grid2_edges(nb, n_l)), grid=(nb, n_l),
        in_specs=[row_blk, row_blk, pl.BlockSpec((1, tc, nc, 2 * lb), lambda b, l: (b, 0, 0, l)),
                  pl.BlockSpec((1, 2 * lb), lambda b, l: (0, l)), in_map, out_map,
                  pl.BlockSpec((1, kq), lambda b, l: (0, l // sub))] + ride.in_specs,
        out_specs=[row_blk, pl.BlockSpec((1, 1, kq, 2 * lb), lambda b, l: (b, l // sub, 0, l % sub)),
                   pl.BlockSpec((1, 1, 2 * lb, kq), lambda b, l: (b, l // sub, l % sub, 0)),
                   pl.BlockSpec((1, 1, 2 * lb), lambda b, l: (b, 0, l))] + ride.out_specs,
        out_shape=[jax.ShapeDtypeStruct((n, dh), F32), jax.ShapeDtypeStruct((nb, n_blk, kq, cols), F32),
                   jax.ShapeDtypeStruct((nb, n_blk, cols, kq), F32), jax.ShapeDtypeStruct((nb, 1, n_l * 2 * lb), F32)]
        + ride.out_shape,
        scratch_shapes=[pltpu.VMEM((tc, nc, 2 * lb), F32)] + [pltpu.VMEM((nc, ln), F32)] * 2 + ride.scratch,
        name="s5_scan_bwd", compiler_params=_params(2))(dy_pre, u5p, s_cat, a_cat_conj, bb_cat, cc_cat, d_row,
                                                        *ride.arrays)


def kernel(x, mem, g_mix_pre, w_in, gm_ln_g, gm_ln_b, gm_w_s, gm_b_s, s5_lam_re, s5_lam_im, s5_log_step, s5_b_re, s5_b_im, s5_c_re, s5_c_im, s5_d, s5_w_glu, w_br_gm, w_br_s5, w_mix_out, g_mix_post, g_ca_pre, g_mem, ca_w_q, ca_w_kv, ca_w_o, g_ca_post, g_ffn_pre, ffn_w_gu, ffn_w_down, g_ffn_post, loss_target, m_g_mix_pre, m_w_in, m_gm_ln_g, m_gm_ln_b, m_gm_w_s, m_gm_b_s, m_s5_lam_re, m_s5_lam_im, m_s5_log_step, m_s5_b_re, m_s5_b_im, m_s5_c_re, m_s5_c_im, m_s5_d, m_s5_w_glu, m_w_br_gm, m_w_br_s5, m_w_mix_out, m_g_mix_post, m_g_ca_pre, m_g_mem, m_ca_w_q, m_ca_w_kv, m_ca_w_o, m_g_ca_post, m_g_ffn_pre, m_ffn_w_gu, m_ffn_w_down, m_g_ffn_post, v_g_mix_pre, v_w_in, v_gm_ln_g, v_gm_ln_b, v_gm_w_s, v_gm_b_s, v_s5_lam_re, v_s5_lam_im, v_s5_log_step, v_s5_b_re, v_s5_b_im, v_s5_c_re, v_s5_c_im, v_s5_d, v_s5_w_glu, v_w_br_gm, v_w_br_s5, v_w_mix_out, v_g_mix_post, v_g_ca_pre, v_g_mem, v_ca_w_q, v_ca_w_kv, v_ca_w_o, v_g_ca_post, v_g_ffn_pre, v_ffn_w_gu, v_ffn_w_down, v_g_ffn_post):
    args = locals()
    W = {n: args[n][0] for n in WEIGHTS}
    M = {n: args['m_' + n][0] for n in WEIGHTS}
    V = {n: args['v_' + n][0] for n in WEIGHTS}
    for k, kind in SHARDED.items():
        if kind == 'colT':
            W[k], M[k], V[k] = W[k].T, M[k].T, V[k].T

    nb, seq, d = x.shape
    n = nb * seq
    tm = min(ROW_TILE, n)
    tmw = min(WIDE_ROW_TILE, n)
    tmf = min(FFN_ROW_TILE, n)
    nmem = mem.shape[1]
    d2, dh = 2 * d, d // 2
    hd = d // CA_HEADS
    x2d = x.reshape(n, d)
    tgt = loss_target.reshape(n, d)
    mem2d = mem.reshape(nb * nmem, d)

    def row(v):
        return v.reshape(1, -1)

    names = list(SHARDED)
    later = [k for k in names if k != 'w_in']
    first_g, *casts = _first_gather(W['w_in'], [W[k] for k in later])
    shard_b = dict(zip(later, casts))
    half = shard_b['ffn_w_gu'].shape[0] // 2
    gu_halves = [shard_b['ffn_w_gu'][:half], shard_b['ffn_w_gu'][half:]]

    def assemble(k, gth):
        r, c = gth.shape[1:]
        return gth.transpose(1, 0, 2).reshape(r, N_DEV * c) if SHARDED[k] == 'col' else gth.reshape(N_DEV * r, c)

    w_in_t = assemble('w_in', first_g)
    ffh = N_DEV * W['ffn_w_down'].shape[0]

    (a_re, a_im, bb_re, bb_im), s5_in_vjp = jax.vjp(
        _s5_in_tables, W['s5_lam_re'], W['s5_lam_im'], W['s5_log_step'], W['s5_b_re'], W['s5_b_im'])
    (cc_re, cc_im), s5_out_vjp = jax.vjp(_s5_out_tables, W['s5_c_re'], W['s5_c_im'])
    bb_re_b, bb_im_b, cc_re_b, cc_im_b = (t.astype(BF16) for t in (bb_re, bb_im, cc_re, cc_im))
    s5_d_row = row(W['s5_d'])
    tril = jnp.tril(jnp.ones((GM_CHUNK, GM_CHUNK), bool))
    w_s = jnp.where(tril[None], W['gm_w_s'], 0.0).astype(BF16)
    w_s_t = w_s.transpose(0, 2, 1)
    gm_bias = jnp.repeat(W['gm_b_s'].T, d // GM_GROUPS, axis=1)

    def in_proj_body(x_ref, g_ref, w_ref, zgm_ref, u5_ref, zga_ref, zgb_ref, h_ref):
        hb = _rms(x_ref[...], g_ref[...]).astype(BF16)
        h_ref[...] = hb
        for lo in range(0, w_ref.shape[0], 512):
            acc = _dot_nt(hb, w_ref[lo:lo + 512, :])
            if lo < d2:
                zgm_ref[:, lo:lo + 512] = acc.astype(BF16)
            elif lo < d2 + dh:
                u5_ref[...] = acc
            elif lo < d2 + dh + d:
                zga_ref[:, lo - d2 - dh:lo - d2 - dh + 512] = acc.astype(BF16)
            else:
                zgb_ref[:, lo - d2 - dh - d:lo - d2 - dh - d + 512] = acc.astype(BF16)

    z_gm, u5, z_ga, z_gb, h0, g_gu0 = _rowcall(
        "in_proj", in_proj_body, n, tmf, [(x2d, 'row'), (row(W['g_mix_pre']), 'full'), (w_in_t, 'full')],
        [(d2, BF16), (dh, F32), (d, BF16), (d, BF16), (d, BF16)], gather=[gu_halves[0]])

    gw = d // GM_GROUPS

    def gm_fwd_body(z_ref, lng_ref, lnb_ref, ws_ref, bias_ref, y_ref, u_ref, vh_ref, dgelu_ref, rstd_ref):
        zg, dgelu = _gelu_and_grad(z_ref[...].astype(F32))
        dgelu_ref[...] = dgelu.astype(BF16)
        u, v = zg[:, :d], zg[:, d:]
        vc = v - jnp.mean(v, axis=-1, keepdims=True)
        rstd = lax.rsqrt(jnp.mean(vc * vc, axis=-1, keepdims=True) + EPS)
        vhat = vc * rstd
        u_ref[...] = u.astype(BF16)
        vh_ref[...] = vhat.astype(BF16)
        rstd_ref[...] = jnp.broadcast_to(rstd, rstd_ref.shape)
        vnb = (vhat * lng_ref[...] + lnb_ref[...]).astype(BF16)
        sv = jnp.concatenate([_dot(ws_ref[g], vnb[:, g * gw:(g + 1) * gw]) for g in range(GM_GROUPS)], axis=1)
        y_ref[...] = (u * (sv + bias_ref[...])).astype(BF16)

    ride = ['w_mix_out', 'ca_w_q', 's5_w_glu', 'w_br_gm', 'w_br_s5']
    y_gm, gm_u, gm_vhat, gm_dgelu, gm_rstd, *got = _rowcall(
        "gmlp_fwd", gm_fwd_body, n, GM_CHUNK,
        [(z_gm, 'row'), (row(W['gm_ln_g']), 'full'), (row(W['gm_ln_b']), 'full'), (w_s, 'full'), (gm_bias, 'full')],
        [(d, BF16), (d, BF16), (d, BF16), (d2, BF16), (128, F32)], gather=[shard_b[k] for k in ride])
    w_mix_f, w_q_f, w_glu_f, w_brgm_f, w_brs5_f = (assemble(k, g) for k, g in zip(ride, got))

    tc = seq // SCAN_CHUNKS
    lt = S5_GROUPS * S5_STATE

    def to_scan_order(t):
        return t.reshape(nb, SCAN_CHUNKS, tc, t.shape[-1]).transpose(0, 2, 1, 3).reshape(n, t.shape[-1])

    def from_scan_order(t):
        return t.reshape(nb, tc, SCAN_CHUNKS, t.shape[-1]).transpose(0, 2, 1, 3).reshape(n, t.shape[-1])

    u5p = to_scan_order(u5)
    a_cat, bb_cat, cc_cat = _s5_cat_tables(a_re, a_im, bb_re_b, bb_im_b, cc_re_b, cc_im_b)
    a_cat_conj = _s5_cat_tables(a_re, -a_im, bb_re_b, bb_im_b, cc_re_b, cc_im_b)[0]
    s_cat, y_lin, g_kv, g_o = _s5_scan_fwd(u5p, a_cat, bb_cat, cc_cat, nb,
                                           gather=[shard_b['ca_w_kv'], shard_b['ca_w_o']])
    w_kv_f, w_o_f = assemble('ca_w_kv', g_kv), assemble('ca_w_o', g_o)

    def s5_out_body(yl_ref, u_ref, d_ref, wg_ref, ypre_ref, gate_ref, y_ref):
        ypre = yl_ref[...] + d_ref[...] * u_ref[...]
        ypre_ref[...] = ypre
        yg = _gelu(ypre)
        gate = _dot(yg.astype(BF16), wg_ref[...])
        gate_ref[...] = gate
        y_ref[...] = (yg * _sigmoid(gate)).astype(BF16)

    tm5 = min(1024, n)
    y_pre, gate, y_s5p = _rowcall(
        "s5_out", s5_out_body, n, tm5, [(y_lin, 'row'), (u5p, 'row'), (s5_d_row, 'full'), (w_glu_f, 'full')],
        [(dh, F32), (dh, F32), (dh, BF16)])
    y_s5 = from_scan_order(y_s5p)

    def merge_body(ygm_ref, ys5_ref, zga_ref, zgb_ref, wa_ref, wb_ref, pa_ref, pb_ref, mg_ref):
        pa = _dot(ygm_ref[...], wa_ref[...])
        pb = _dot(ys5_ref[...], wb_ref[...])
        pa_ref[...] = pa.astype(BF16)
        pb_ref[...] = pb.astype(BF16)
        mg_ref[...] = (_sigmoid(zga_ref[...].astype(F32)) * pa + _sigmoid(zgb_ref[...].astype(F32)) * pb).astype(BF16)

    p_a, p_b, merged, g_down = _rowcall(
        "merge", merge_body, n, tmw,
        [(y_gm, 'row'), (y_s5, 'row'), (z_ga, 'row'), (z_gb, 'row'), (w_brgm_f, 'full'), (w_brs5_f, 'full')],
        [(d, BF16), (d, BF16), (d, BF16)], gather=[shard_b['ffn_w_down']], two_level=True)
    w_down_f = assemble('ffn_w_down', g_down)

    def close_sublayer(name, a_in, w_out, x_res, g_post, g_next, w_next=None, gather=(), two_level=False):
        def body(*refs):
            a_ref, w_ref, x_ref, gp_ref, gn_ref = refs[:5]
            rest = refs[5:]
            if w_next is not None:
                wn_ref, rest = rest[0], rest[1:]
            o_ref, xo_ref, h_ref = rest[:3]
            o = _dot(a_ref[...], w_ref[...])
            o_ref[...] = o.astype(BF16)
            xo = x_ref[...] + _rms(o, gp_ref[...])
            xo_ref[...] = xo
            hb = _rms(xo, gn_ref[...]).astype(BF16)
            h_ref[...] = hb
            if w_next is not None:
                rest[3][...] = _dot(hb, wn_ref[...]).astype(BF16)

        ins = [(a_in, 'row'), (w_out, 'full'), (x_res, 'row'), (row(g_post), 'full'), (row(g_next), 'full')]
        outs = [(d, BF16), (d, F32), (d, BF16)]
        if w_next is not None:
            ins.append((w_next, 'full'))
            outs.append((w_next.shape[1], BF16))
        return _rowcall(name, body, n, tmw, ins, outs, gather=gather, two_level=two_level)

    o1, x1, hc, q = close_sublayer("mix_out", merged, w_mix_f, x2d, W['g_mix_post'], W['g_ca_pre'], w_q_f)

    tmm = min(ROW_TILE, nb * nmem)

    def memkv_body(m_ref, g_ref, w_ref, mn_ref, k_ref, v_ref):
        mnb = _rms(m_ref[...], g_ref[...]).astype(BF16)
        mn_ref[...] = mnb
        k_ref[...] = _dot(mnb, w_ref[:, :d]).astype(BF16)
        v_ref[...] = _dot(mnb, w_ref[:, d:]).astype(BF16)

    mem_n, k_mem, v_mem = _rowcall("mem_kv", memkv_body, nb * nmem, tmm,
                                   [(mem2d, 'row'), (row(W['g_mem']), 'full'), (w_kv_f, 'full')],
                                   [(d, BF16), (d, BF16), (d, BF16)])

    tiles_per_ex = seq // tm
    kv_spec = pl.BlockSpec((nmem, d), lambda i: (i // tiles_per_ex, 0))
    scale = hd ** -0.5

    def softmax_rows(qh, kh):
        s = _dot_nt(qh, kh) * scale
        e = jnp.exp(s - jnp.max(s, axis=-1, keepdims=True))
        return e / jnp.sum(e, axis=-1, keepdims=True)

    def attn_body(q_ref, k_ref, v_ref, o_ref):
        for h in range(CA_HEADS):
            sl = slice(h * hd, (h + 1) * hd)
            p = softmax_rows(q_ref[:, sl], k_ref[:, sl])
            o_ref[:, sl] = _dot(p.astype(BF16), v_ref[:, sl]).astype(BF16)

    (att,) = _rowcall("attn_fwd", attn_body, n, tm, [(q, 'row'), (k_mem, kv_spec), (v_mem, kv_spec)], [(d, BF16)])

    o2, x2, hf, g_gu1 = close_sublayer("attn_out", att, w_o_f, x1, W['g_ca_post'], W['g_ffn_pre'],
                                       gather=[gu_halves[1]], two_level=True)
    w_gu_t = jnp.stack([g_gu0, g_gu1], axis=1).reshape(2 * ffh, d)

    ck = 256

    def ffn_up_body(h_ref, w_ref, gu_ref, a_ref):
        hb = h_ref[...]
        for lo in range(0, ffh, ck):
            gt = _dot_nt(hb, w_ref[lo:lo + ck, :])
            ut = _dot_nt(hb, w_ref[ffh + lo:ffh + lo + ck, :])
            gu_ref[:, lo:lo + ck] = gt.astype(BF16)
            gu_ref[:, ffh + lo:ffh + lo + ck] = ut.astype(BF16)
            a_ref[:, lo:lo + ck] = ((gt * _sigmoid(gt)) * ut).astype(BF16)

    gu, act = _rowcall("ffn_up", ffn_up_body, n, tmf, [(hf, 'row'), (w_gu_t, 'full')], [(2 * ffh, BF16), (ffh, BF16)])

    def ffn_down_body(a_ref, w_ref, x_ref, t_ref, g_ref, dx_ref, do_ref, loss_ref, dg_ref):
        i = pl.program_id(0)

        @pl.when(i == 0)
        def _():
            loss_ref[...] = jnp.zeros_like(loss_ref)
            dg_ref[...] = jnp.zeros_like(dg_ref)

        o = _dot(a_ref[...], w_ref[...])
        diff = x_ref[...] + _rms(o, g_ref[...]) - t_ref[...]
        loss_ref[...] += jnp.full(loss_ref.shape, 0.5 / d, F32) * jnp.sum(diff * diff)
        dx = diff * (1.0 / d)
        dx_ref[...] = dx
        do, dg = _rms_bwd(o, g_ref[...], dx)
        do_ref[...] = do.astype(BF16)
        dg_ref[...] += dg

    dx3, do3, loss_part, dg_ffn_post = _rowcall(
        "ffn_down_loss", ffn_down_body, n, tmf,
        [(act, 'row'), (w_down_f, 'full'), (x2, 'row'), (tgt, 'row'), (row(W['g_ffn_post']), 'full')],
        [(d, F32), (d, BF16)], accs=[((1, 128), F32), ((1, d), F32)])

    G = {'g_ffn_post': dg_ffn_post}
    RECV = {}

    def parts_of(k, gfull):
        r, c = W[k].shape
        return gfull.reshape(r, N_DEV, c).transpose(1, 0, 2) if SHARDED[k] == 'col' else gfull.reshape(N_DEV, r, c)

    p_down = parts_of('ffn_w_down', _dwcall("dw_ffn_down", act, do3, BF16))

    def ffn_act_bwd_body(do_ref, w_ref, gu_ref, dgu_ref):
        dob = do_ref[...]
        for lo in range(0, ffh, ck):
            da = _dot_nt(dob, w_ref[lo:lo + ck, :])
            gt = gu_ref[:, lo:lo + ck].astype(F32)
            ut = gu_ref[:, ffh + lo:ffh + lo + ck].astype(F32)
            sg = _sigmoid(gt)
            dgu_ref[:, lo:lo + ck] = (da * ut * (sg * (1.0 + gt * (1.0 - sg)))).astype(BF16)
            dgu_ref[:, ffh + lo:ffh + lo + ck] = (da * (gt * sg)).astype(BF16)

    dgu, RECV['ffn_w_down'] = _rowcall("ffn_act_bwd", ffn_act_bwd_body, n, tmf,
                                       [(do3, 'row'), (w_down_f, 'full'), (gu, 'row')], [(2 * ffh, BF16)],
                                       scatter=[p_down])
    p_gu = parts_of('ffn_w_gu', _dwcall("dw_ffn_gu", dgu, hf, BF16))

    def open_sublayer(name, pieces, w_full, x_in, g_pre, dx_up, o_prev=None, g_post_prev=None, scatter=(), gather=(),
                      w_t=False):
        n_p = len(pieces)
        second = o_prev is not None

        def body(*refs):
            dc_refs, (w_ref, x_ref, g_ref, dxu_ref), rest = refs[:n_p], refs[n_p:n_p + 4], refs[n_p + 4:]
            if second:
                (op_ref, gp_ref), rest = rest[:2], rest[2:]
            i = pl.program_id(0)
            dhid = None
            for dc_ref, (_, lo, hi) in zip(dc_refs, pieces):
                part = _dot(dc_ref[...], w_ref[lo:hi, :]) if w_t else _dot_nt(dc_ref[...], w_ref[:, lo:hi])
                dhid = part if dhid is None else dhid + part
            dxn, dg = _rms_bwd(x_ref[...], g_ref[...], dhid)
            dx = dxu_ref[...] + dxn
            if second:
                dx_ref, do_ref, dg_ref, dg2_ref = rest
            else:
                dx_ref, dg_ref = rest

            @pl.when(i == 0)
            def _():
                dg_ref[...] = jnp.zeros_like(dg_ref)
                if second:
                    dg2_ref[...] = jnp.zeros_like(dg2_ref)

            dx_ref[...] = dx
            dg_ref[...] += dg
            if second:
                do, dg2 = _rms_bwd(op_ref[...].astype(F32), gp_ref[...], dx)
                do_ref[...] = do.astype(BF16)
                dg2_ref[...] += dg2

        ins = [(p[0], 'row') for p in pieces] + [(w_full, 'full'), (x_in, 'row'), (row(g_pre), 'full'), (dx_up, 'row')]
        outs = [(d, F32)]
        accs = [((1, d), F32)]
        if second:
            ins += [(o_prev, 'row'), (row(g_post_prev), 'full')]
            outs.append((d, BF16))
            accs.append(((1, d), F32))
        light = w_full.size <= d * d
        res = _rowcall(name, body, n, tmw if light else tmf, ins, outs, accs=accs, scatter=scatter, gather=gather)
        if second:
            dx, do, dg, dg2 = res[:4]
            return dx, dg, do, dg2, res[4:]
        return res[0], res[1], res[2:]

    dx2, G['g_ffn_pre'], do2, G['g_ca_post'], _ = open_sublayer(
        "ffn_in_bwd", [(dgu, 0, 2 * ffh)], w_gu_t, x2, W['g_ffn_pre'], dx3, o2, W['g_ca_post'], w_t=True)
    p_o = parts_of('ca_w_o', _dwcall("dw_ca_o", att, do2, BF16))

    def attn_bwd_body(q_ref, k_ref, v_ref, do_ref, wo_ref, dq_ref, dk_ref, dv_ref):
        i = pl.program_id(0)

        @pl.when(i % tiles_per_ex == 0)
        def _():
            dk_ref[...] = jnp.zeros_like(dk_ref)
            dv_ref[...] = jnp.zeros_like(dv_ref)

        d_att = _dot_nt(do_ref[...], wo_ref[...]).astype(BF16)
        for h in range(CA_HEADS):
            sl = slice(h * hd, (h + 1) * hd)
            qh, kh, vh, dah = q_ref[:, sl], k_ref[:, sl], v_ref[:, sl], d_att[:, sl]
            s_t = _dot_nt(kh, qh) * scale
            e_t = jnp.exp(s_t - jnp.max(s_t, axis=0, keepdims=True))
            p_t = e_t / jnp.sum(e_t, axis=0, keepdims=True)
            dp_t = _dot_nt(vh, dah)
            ds_t = (p_t * (dp_t - jnp.sum(p_t * dp_t, axis=0, keepdims=True)) * scale).astype(BF16)
            dq_ref[:, sl] = _dot_tn(ds_t, kh).astype(BF16)
            dk_ref[:, sl] += _dot(ds_t, qh)
            dv_ref[:, sl] += _dot(p_t.astype(BF16), dah)

    kv_acc = ((nb * nmem, d), (F32, pl.BlockSpec((nmem, d), lambda i: (i // tiles_per_ex, 0))))
    dq, dk_mem, dv_mem, RECV['ca_w_o'] = _rowcall(
        "attn_bwd", attn_bwd_body, n, tm,
        [(q, 'row'), (k_mem, kv_spec), (v_mem, kv_spec), (do2, 'row'), (w_o_f, 'full')], [(d, BF16)],
        accs=[kv_acc, kv_acc], scatter=[p_o])
    p_q = parts_of('ca_w_q', _dwcall("dw_ca_q", hc, dq, BF16))

    def memkv_bwd_body(dk_ref, dv_ref, m_ref, g_ref, w_ref, dkv_ref, dg_ref):
        i = pl.program_id(0)

        @pl.when(i == 0)
        def _():
            dg_ref[...] = jnp.zeros_like(dg_ref)

        dkb, dvb = dk_ref[...].astype(BF16), dv_ref[...].astype(BF16)
        dkv_ref[:, :d] = dkb
        dkv_ref[:, d:] = dvb
        dmn = _dot_nt(dkb, w_ref[:, :d]) + _dot_nt(dvb, w_ref[:, d:])
        _, dg = _rms_bwd(m_ref[...], g_ref[...], dmn)
        dg_ref[...] += dg

    dkv, G['g_mem'] = _rowcall(
        "mem_kv_bwd", memkv_bwd_body, nb * nmem, tmm,
        [(dk_mem, 'row'), (dv_mem, 'row'), (mem2d, 'row'), (row(W['g_mem']), 'full'), (w_kv_f, 'full')],
        [(d2, BF16)], accs=[((1, d), F32)])
    p_kv = parts_of('ca_w_kv', _dwcall("dw_ca_kv", mem_n, dkv, BF16))

    dx1, G['g_ca_pre'], do1, G['g_mix_post'], _ = open_sublayer(
        "attn_in_bwd", [(dq, 0, d)], w_q_f, x1, W['g_ca_pre'], dx2, o1, W['g_mix_post'])
    p_mix = parts_of('w_mix_out', _dwcall("dw_mix_out", merged, do1, BF16))

    def merge_bwd_body(do_ref, wm_ref, zga_ref, zgb_ref, pa_ref, pb_ref, wb_ref, dpa_ref, dpb_ref, dza_ref, dzb_ref,
                       dys_ref):
        dm = _dot_nt(do_ref[...], wm_ref[...])
        sa, sb = _sigmoid(zga_ref[...].astype(F32)), _sigmoid(zgb_ref[...].astype(F32))
        dpb = (dm * sb).astype(BF16)
        dpa_ref[...] = (dm * sa).astype(BF16)
        dpb_ref[...] = dpb
        dza_ref[...] = (dm * pa_ref[...].astype(F32) * (sa * (1.0 - sa))).astype(BF16)
        dzb_ref[...] = (dm * pb_ref[...].astype(F32) * (sb * (1.0 - sb))).astype(BF16)
        dys_ref[...] = _dot_nt(dpb, wb_ref[...]).astype(BF16)

    NORM_ROWS = ['g_ffn_post', 'g_ffn_pre', 'g_ca_post', 'g_mem', 'g_ca_pre', 'g_mix_post']
    norm_rows = jnp.concatenate([G[k] for k in NORM_ROWS] + [jnp.tile(loss_part, (1, d // 128))], axis=0)
    dp_a, dp_b, dz_ga, dz_gb, dy_s5, all_norm_rows, RECV['w_mix_out'], RECV['ca_w_q'] = _rowcall(
        "merge_bwd", merge_bwd_body, n, tmw,
        [(do1, 'row'), (w_mix_f, 'full'), (z_ga, 'row'), (z_gb, 'row'), (p_a, 'row'), (p_b, 'row'), (w_brs5_f, 'full')],
        [(d, BF16), (d, BF16), (d, BF16), (d, BF16), (dh, BF16)], scatter=[p_mix, p_q], gather=[norm_rows])
    p_brgm = parts_of('w_br_gm', _dwcall("dw_br_gm", y_gm, dp_a, BF16))
    p_brs5 = parts_of('w_br_s5', _dwcall("dw_br_s5", y_s5, dp_b, BF16))

    def gm_bwd_body(u_ref, vh_ref, dgelu_ref, rstd_ref, dpa_ref, wa_ref, lng_ref, lnb_ref, ws_ref, wst_ref, bias_ref,
                    dz_ref, dws_ref, dbias_ref, dlng_ref, dlnb_ref):
        i = pl.program_id(0)

        @pl.when(i == 0)
        def _():
            dws_ref[...] = jnp.zeros_like(dws_ref)
            dbias_ref[...] = jnp.zeros_like(dbias_ref)
            dlng_ref[...] = jnp.zeros_like(dlng_ref)
            dlnb_ref[...] = jnp.zeros_like(dlnb_ref)

        u, vhat, rstd = u_ref[...].astype(F32), vh_ref[...].astype(F32), rstd_ref[:, :1]
        vnb = (vhat * lng_ref[...] + lnb_ref[...]).astype(BF16)
        dy = _dot_nt(dpa_ref[...], wa_ref[...])
        groups = [slice(g * gw, (g + 1) * gw) for g in range(GM_GROUPS)]
        dsv = dy * u
        dsvb = dsv.astype(BF16)
        dbias_ref[...] += dsv
        sv = jnp.concatenate([_dot(ws_ref[g], vnb[:, sl]) for g, sl in enumerate(groups)], axis=1) + bias_ref[...]
        dvn = jnp.concatenate([_dot(wst_ref[g], dsvb[:, sl]) for g, sl in enumerate(groups)], axis=1)
        for g, sl in enumerate(groups):
            dws_ref[g] += _dot_nt(dsvb[:, sl], vnb[:, sl])
        du = dy * sv
        dlng_ref[...] += jnp.sum(dvn * vhat, axis=0, keepdims=True)
        dlnb_ref[...] += jnp.sum(dvn, axis=0, keepdims=True)
        dvh = dvn * lng_ref[...]
        dv = rstd * (dvh - jnp.mean(dvh, axis=-1, keepdims=True) - vhat * jnp.mean(dvh * vhat, axis=-1, keepdims=True))
        dz_ref[:, :d] = (du * dgelu_ref[:, :d].astype(F32)).astype(BF16)
        dz_ref[:, d:] = (dv * dgelu_ref[:, d:].astype(F32)).astype(BF16)

    dz_gm, dws_full, dbias_full, G['gm_ln_g'], G['gm_ln_b'], RECV['w_br_gm'], RECV['w_br_s5'], RECV['ca_w_kv'] = _rowcall(
        "gmlp_bwd", gm_bwd_body, n, GM_CHUNK,
        [(gm_u, 'row'), (gm_vhat, 'row'), (gm_dgelu, 'row'), (gm_rstd, 'row'), (dp_a, 'row'), (w_brgm_f, 'full'),
         (row(W['gm_ln_g']), 'full'), (row(W['gm_ln_b']), 'full'), (w_s, 'full'), (w_s_t, 'full'), (gm_bias, 'full')],
        [(d2, BF16)], accs=[((GM_GROUPS, GM_CHUNK, GM_CHUNK), F32), ((GM_CHUNK, d), F32), ((1, d), F32), ((1, d), F32)],
        scatter=[p_brgm, p_brs5, p_kv])
    G['gm_w_s'] = jnp.where(tril[None], dws_full, 0.0)
    G['gm_b_s'] = dbias_full.reshape(GM_CHUNK, GM_GROUPS, gw).sum(-1).T

    dy_s5p = to_scan_order(dy_s5)

    def s5_out_bwd_body(dy_ref, ypre_ref, gate_ref, u_ref, wg_ref, dyp_ref, dgate_ref, yg_ref, dd_ref):
        i = pl.program_id(0)

        @pl.when(i == 0)
        def _():
            dd_ref[...] = jnp.zeros_like(dd_ref)

        dy = dy_ref[...].astype(F32)
        yg, dgelu = _gelu_and_grad(ypre_ref[...])
        sg = _sigmoid(gate_ref[...])
        dgb = (dy * yg * (sg * (1.0 - sg))).astype(BF16)
        dgate_ref[...] = dgb
        yg_ref[...] = yg.astype(BF16)
        dyp = (dy * sg + _dot_nt(dgb, wg_ref[...])) * dgelu
        dyp_ref[...] = dyp
        dd_ref[...] += jnp.sum(dyp * u_ref[...], axis=0, keepdims=True)

    dy_pre, dgate, yg_b, dd = _rowcall(
        "s5_out_bwd", s5_out_bwd_body, n, tm5,
        [(dy_s5p, 'row'), (y_pre, 'row'), (gate, 'row'), (u5p, 'row'), (w_glu_f, 'full')],
        [(dh, F32), (dh, BF16), (dh, BF16)], accs=[((1, dh), F32)])
    gm_ln_rows = jnp.concatenate([G['gm_ln_g'], G['gm_ln_b']], axis=0)
    du5p, d_bb_cat, d_cc_cat, da_cat, all_gm_w_s, all_gm_b_s, all_gm_ln, RECV['ffn_w_gu'] = _s5_scan_bwd(
        dy_pre, u5p, s_cat, a_cat_conj, bb_cat, cc_cat, s5_d_row, gather=[G['gm_w_s'], G['gm_b_s'], gm_ln_rows],
        scatter=[p_gu])
    du5 = from_scan_order(du5p.astype(BF16))
    d_bb_re, d_bb_im = _s5_uncat(jnp.sum(d_bb_cat, axis=0), 2)
    d_cc_re, d_cc_neg_im = _s5_uncat(jnp.sum(d_cc_cat, axis=0), 1)
    da_re, da_im = _s5_uncat(jnp.sum(da_cat, axis=0), 1)
    lane_shape = (S5_GROUPS, 8, 128)
    g_c_re, g_c_im = (t.reshape(lane_shape) for t in s5_out_vjp((d_cc_re, -d_cc_neg_im)))
    g_d = dd.reshape(S5_GROUPS, S5_CH)
    g_lam_re, g_lam_im, g_log_step, g_b_re, g_b_im = s5_in_vjp((da_re, da_im, d_bb_re, d_bb_im))
    g_log_step, g_b_re, g_b_im = g_log_step.reshape(1, -1), g_b_re.reshape(lane_shape), g_b_im.reshape(lane_shape)
    p_glu = parts_of('s5_w_glu', _dwcall("dw_s5_glu", yg_b, dgate, BF16))

    pieces = [(dz_gm, 0, d2), (du5, d2, d2 + dh), (dz_ga, d2 + dh, d2 + dh + d), (dz_gb, d2 + dh + d, d2 + dh + 2 * d)]
    dw_gm, all_lam_re, all_lam_im, all_log_step, all_b_re, all_b_im = _dwcall(
        "dw_in_0", dz_gm, h0, BF16, gather=[g_lam_re, g_lam_im, g_log_step, g_b_re, g_b_im])
    dw_s5 = _dwcall("dw_in_1", du5, h0, BF16)
    dw_ga = _dwcall("dw_in_2", dz_ga, h0, BF16)
    dw_gb, all_c_re, all_c_im, all_d, RECV['s5_w_glu'] = _dwcall(
        "dw_in_3", dz_gb, h0, BF16, gather=[g_c_re, g_c_im, g_d], scatter=[p_glu])
    p_in = parts_of('w_in', jnp.concatenate([dw_gm, dw_s5, dw_ga, dw_gb], axis=0))
    grad_x2d, g_mix_pre_part, (RECV['w_in'],) = open_sublayer(
        "in_proj_bwd", pieces, w_in_t, x2d, W['g_mix_pre'], dx1, scatter=[p_in], w_t=True)

    out = {}
    for k in names:
        quad = _sum_adamw("sum_adamw_" + k, RECV[k], W[k], M[k], V[k])
        out[k] = [(t.T if SHARDED[k] == 'colT' else t)[None] for t in quad]
    gathered = [all_norm_rows, all_gm_w_s, all_gm_b_s, all_gm_ln, all_c_re, all_c_im, all_d, all_lam_re, all_lam_im,
                all_log_step, all_b_re, all_b_im, _gather_last(g_mix_pre_part)]
    where = {'g_ffn_post': (0, 0), 'g_ffn_pre': (0, 1), 'g_ca_post': (0, 2), 'g_mem': (0, 3), 'g_ca_pre': (0, 4),
             'g_mix_post': (0, 5), 'gm_w_s': (1, None), 'gm_b_s': (2, None), 'gm_ln_g': (3, 0), 'gm_ln_b': (3, 1),
             's5_c_re': (4, None), 's5_c_im': (5, None), 's5_d': (6, None), 's5_lam_re': (7, None),
             's5_lam_im': (8, None), 's5_log_step': (9, 0), 's5_b_re': (10, None), 's5_b_im': (11, None),
             'g_mix_pre': (12, 0)}
    folded = ('s5_b_re', 's5_b_im', 's5_c_re', 's5_c_im')

    def as_updated(k, t):
        return t.reshape((1,) + lane_shape) if k in folded else t

    small, loss_row = _small_adamw(
        [(as_updated(k, args[k]), as_updated(k, args['m_' + k]), as_updated(k, args['v_' + k])) + where[k]
         for k in SMALL], gathered, (0, len(NORM_ROWS)))
    out.update({k: [t.reshape(args[k].shape) for t in quad] for k, quad in zip(SMALL, small)})

    res = [loss_row[0, 0], grad_x2d.reshape(x.shape)]
    for j in range(4):
        res += [out[k][j] for k in WEIGHTS]
    return tuple(res)
```
